```python
import jax, jax.numpy as jnp
from jax import lax
import numpy as np

D_MODEL = 2048
BATCH = 8
SEQ = 4096
DEPTH = 4

CHUNK = 64
N_MIXERS = 2
N_A = (DEPTH + N_MIXERS - 1) // N_MIXERS
N_B = DEPTH // N_MIXERS
NORM_EPS = 1e-6

SGU_BLOCK = 128
SGU_WIDTH = 2 * D_MODEL
SGU_GROUPS = 16
SGU_GROUP_DIM = SGU_WIDTH // SGU_GROUPS

MLA_HEADS = 16
Q_LORA_RANK = 448
KV_LORA_RANK = 512
QK_NOPE_DIM = 128
QK_ROPE_DIM = 64
V_HEAD_DIM = 128
MLA_WIDTH = MLA_HEADS * V_HEAD_DIM
ROPE_THETA = 10000.0
Q_BLOCK = 128

kernel_name = "hybrid_sgu_mla_adaln_sandwich"


def rms_norm(x, g):
    xf = x.astype(jnp.float32)
    y = xf * lax.rsqrt(jnp.mean(xf * xf, axis=-1, keepdims=True) + NORM_EPS)
    return (y * g.astype(jnp.float32)).astype(x.dtype)


def layer_norm(x, g):
    xf = x.astype(jnp.float32)
    mu = jnp.mean(xf, axis=-1, keepdims=True)
    var = jnp.mean(jnp.square(xf - mu), axis=-1, keepdims=True)
    return ((xf - mu) * lax.rsqrt(var + NORM_EPS) * g.astype(jnp.float32)).astype(x.dtype)


def apply_rope(x, cos, sin):
    xf = x.astype(jnp.float32)
    x1, x2 = jnp.split(xf, 2, axis=-1)
    return jnp.concatenate([x1 * cos - x2 * sin, x1 * sin + x2 * cos], axis=-1).astype(x.dtype)


def sgu_mixer(h, w_in, norm_g, w_s, b_s, w_out):
    B, S, _ = h.shape
    u, v, z = jnp.split(h @ w_in, 3, axis=-1)
    u = jax.nn.gelu(u, approximate=False)
    v = layer_norm(jax.nn.gelu(v, approximate=False), norm_g)
    t_chunk = jnp.arange(SGU_BLOCK) // CHUNK
    mask = (t_chunk[None, :] <= t_chunk[:, None]).astype(w_s.dtype)
    w = w_s * mask[None]
    vb = v.reshape(B, S // SGU_BLOCK, SGU_BLOCK, SGU_GROUPS, SGU_GROUP_DIM)
    vm = jnp.einsum('gts,bnsgc->bntgc', w, vb) + b_s.T[:, :, None]
    y = u * vm.reshape(B, S, SGU_WIDTH) * jax.nn.silu(z)
    return y @ w_out


def chunk_causal_attention(q_nope, q_rope, k_nope, k_rope, v):
    B, S, H, _ = q_nope.shape
    nqb = S // Q_BLOCK
    scale = (QK_NOPE_DIM + QK_ROPE_DIM) ** -0.5
    k_chunk = jnp.arange(S) // CHUNK

    def to_blocks(t):
        return jnp.moveaxis(t.reshape(B, nqb, Q_BLOCK, *t.shape[2:]), 1, 0)

    def one_block(args):
        idx, qn, qr = args
        s = (jnp.einsum('bqhd,bkhd->bhqk', qn, k_nope)
             + jnp.einsum('bqhd,bkd->bhqk', qr, k_rope)).astype(jnp.float32) * scale
        q_chunk = (idx * Q_BLOCK + jnp.arange(Q_BLOCK)) // CHUNK
        mask = k_chunk[None, :] <= q_chunk[:, None]
        p = jax.nn.softmax(jnp.where(mask, s, -1e30), axis=-1).astype(v.dtype)
        return jnp.einsum('bhqk,bkhd->bqhd', p, v)

    out = lax.map(one_block, (jnp.arange(nqb), to_blocks(q_nope), to_blocks(q_rope)))
    return jnp.moveaxis(out, 0, 1).reshape(B, S, H, V_HEAD_DIM)


def mla_mixer(h, w_in, q_norm_g, kv_norm_g, w_uq, w_ukv, w_out):
    B, S, _ = h.shape
    o1 = Q_LORA_RANK
    o2 = o1 + KV_LORA_RANK
    o3 = o2 + QK_ROPE_DIM
    cq, ckv, k_rope, z = jnp.split(h @ w_in, [o1, o2, o3], axis=-1)
    q = (rms_norm(cq, q_norm_g) @ w_uq).reshape(B, S, MLA_HEADS, QK_NOPE_DIM + QK_ROPE_DIM)
    kv = (rms_norm(ckv, kv_norm_g) @ w_ukv).reshape(B, S, MLA_HEADS, QK_NOPE_DIM + V_HEAD_DIM)
    q_nope, q_rope = q[..., :QK_NOPE_DIM], q[..., QK_NOPE_DIM:]
    k_nope, v = kv[..., :QK_NOPE_DIM], kv[..., QK_NOPE_DIM:]
    pos = jnp.arange(S, dtype=jnp.float32)
    inv_freq = ROPE_THETA ** (-jnp.arange(0, QK_ROPE_DIM, 2, dtype=jnp.float32) / QK_ROPE_DIM)
    ang = pos[:, None] * inv_freq[None, :]
    cos, sin = jnp.cos(ang), jnp.sin(ang)
    q_rope = apply_rope(q_rope, cos[:, None, :], sin[:, None, :])
    k_rope = apply_rope(k_rope, cos, sin)
    o = chunk_causal_attention(q_nope, q_rope, k_nope, k_rope, v)
    y = o.reshape(B, S, MLA_WIDTH) * jax.nn.silu(z)
    return y @ w_out


def _fwd_setup_inputs(seed: int = 0) -> dict:
    key = jax.random.key(seed)
    ks = jax.random.split(key, 17)
    f32 = jnp.float32
    D = D_MODEL

    def nrm(k, shape, s):
        return jax.random.normal(k, shape, f32) * s

    def gain(k, shape):
        return 1.0 + 0.05 * jax.random.normal(k, shape, f32)

    mla_in_cols = Q_LORA_RANK + KV_LORA_RANK + QK_ROPE_DIM + MLA_WIDTH
    return {
        'x': nrm(ks[0], (BATCH, SEQ, D), 1.0),
        'c': nrm(ks[1], (BATCH, D), 1.0),
        'ada_w': nrm(ks[2], (DEPTH, D, 3 * D), 0.5 * D ** -0.5),
        'ada_b': nrm(ks[3], (DEPTH, 3 * D), 0.01),
        'pre_g': gain(ks[4], (DEPTH, D)),
        'post_g': gain(ks[5], (DEPTH, D)),
        'sgu_w_in': nrm(ks[6], (N_A, D, 3 * SGU_WIDTH), D ** -0.5),
        'sgu_norm_g': gain(ks[7], (N_A, SGU_WIDTH)),
        'sgu_w_s': nrm(ks[8], (N_A, SGU_GROUPS, SGU_BLOCK, SGU_BLOCK), SGU_BLOCK ** -0.5),
        'sgu_b_s': gain(ks[9], (N_A, SGU_GROUPS, SGU_BLOCK)),
        'sgu_w_out': nrm(ks[10], (N_A, SGU_WIDTH, D), SGU_WIDTH ** -0.5),
        'mla_w_in': nrm(ks[11], (N_B, D, mla_in_cols), D ** -0.5),
        'mla_q_norm_g': gain(ks[12], (N_B, Q_LORA_RANK)),
        'mla_kv_norm_g': gain(ks[13], (N_B, KV_LORA_RANK)),
        'mla_w_uq': nrm(ks[14], (N_B, Q_LORA_RANK, MLA_HEADS * (QK_NOPE_DIM + QK_ROPE_DIM)), Q_LORA_RANK ** -0.5),
        'mla_w_ukv': nrm(ks[15], (N_B, KV_LORA_RANK, MLA_HEADS * (QK_NOPE_DIM + V_HEAD_DIM)), KV_LORA_RANK ** -0.5),
        'mla_w_out': nrm(ks[16], (N_B, MLA_WIDTH, D), MLA_WIDTH ** -0.5),
    }


def _fwd_reference(x, c, ada_w, ada_b, pre_g, post_g, sgu_w_in, sgu_norm_g, sgu_w_s, sgu_b_s, sgu_w_out,
              mla_w_in, mla_q_norm_g, mla_kv_norm_g, mla_w_uq, mla_w_ukv, mla_w_out):
    cond = jax.nn.silu(c)
    for i in range(DEPTH):
        mod = cond @ ada_w[i] + ada_b[i]
        shift, scale, gate = jnp.split(mod, 3, axis=-1)
        h = rms_norm(x, pre_g[i]) * (1 + scale[:, None, :]) + shift[:, None, :]
        j = i // N_MIXERS
        if i % N_MIXERS == 0:
            y = sgu_mixer(h, sgu_w_in[j], sgu_norm_g[j], sgu_w_s[j], sgu_b_s[j], sgu_w_out[j])
        else:
            y = mla_mixer(h, mla_w_in[j], mla_q_norm_g[j], mla_kv_norm_g[j], mla_w_uq[j], mla_w_ukv[j], mla_w_out[j])
        x = x + gate[:, None, :] * rms_norm(y, post_g[i])
    return x


import jax as _jax
import jax.numpy as _jnp

TWIN_FORMAT = 'train_step'
FWD_PARAMS = ['x', 'c', 'ada_w', 'ada_b', 'pre_g', 'post_g', 'sgu_w_in', 'sgu_norm_g', 'sgu_w_s', 'sgu_b_s', 'sgu_w_out', 'mla_w_in', 'mla_q_norm_g', 'mla_kv_norm_g', 'mla_w_uq', 'mla_w_ukv', 'mla_w_out']
TWIN_WEIGHTS = ['ada_w', 'ada_b', 'pre_g', 'post_g', 'sgu_w_in', 'sgu_norm_g', 'sgu_w_s', 'sgu_b_s', 'sgu_w_out', 'mla_w_in', 'mla_q_norm_g', 'mla_kv_norm_g', 'mla_w_uq', 'mla_w_ukv', 'mla_w_out']
TWIN_DIFF_INPUT = 'x'
TWIN_INPUTS = ['x', 'c', 'ada_w', 'ada_b', 'pre_g', 'post_g', 'sgu_w_in', 'sgu_norm_g', 'sgu_w_s', 'sgu_b_s', 'sgu_w_out', 'mla_w_in', 'mla_q_norm_g', 'mla_kv_norm_g', 'mla_w_uq', 'mla_w_ukv', 'mla_w_out', 'loss_target', 'm_ada_w', 'm_ada_b', 'm_pre_g', 'm_post_g', 'm_sgu_w_in', 'm_sgu_norm_g', 'm_sgu_w_s', 'm_sgu_b_s', 'm_sgu_w_out', 'm_mla_w_in', 'm_mla_q_norm_g', 'm_mla_kv_norm_g', 'm_mla_w_uq', 'm_mla_w_ukv', 'm_mla_w_out', 'v_ada_w', 'v_ada_b', 'v_pre_g', 'v_post_g', 'v_sgu_w_in', 'v_sgu_norm_g', 'v_sgu_w_s', 'v_sgu_b_s', 'v_sgu_w_out', 'v_mla_w_in', 'v_mla_q_norm_g', 'v_mla_kv_norm_g', 'v_mla_w_uq', 'v_mla_w_ukv', 'v_mla_w_out']
TWIN_OUTPUTS = ['loss', 'grad_x', 'grad_ada_w', 'grad_ada_b', 'grad_pre_g', 'grad_post_g', 'grad_sgu_w_in', 'grad_sgu_norm_g', 'grad_sgu_w_s', 'grad_sgu_b_s', 'grad_sgu_w_out', 'grad_mla_w_in', 'grad_mla_q_norm_g', 'grad_mla_kv_norm_g', 'grad_mla_w_uq', 'grad_mla_w_ukv', 'grad_mla_w_out', 'delta_ada_w', 'delta_ada_b', 'delta_pre_g', 'delta_post_g', 'delta_sgu_w_in', 'delta_sgu_norm_g', 'delta_sgu_w_s', 'delta_sgu_b_s', 'delta_sgu_w_out', 'delta_mla_w_in', 'delta_mla_q_norm_g', 'delta_mla_kv_norm_g', 'delta_mla_w_uq', 'delta_mla_w_ukv', 'delta_mla_w_out', 'new_m_ada_w', 'new_m_ada_b', 'new_m_pre_g', 'new_m_post_g', 'new_m_sgu_w_in', 'new_m_sgu_norm_g', 'new_m_sgu_w_s', 'new_m_sgu_b_s', 'new_m_sgu_w_out', 'new_m_mla_w_in', 'new_m_mla_q_norm_g', 'new_m_mla_kv_norm_g', 'new_m_mla_w_uq', 'new_m_mla_w_ukv', 'new_m_mla_w_out', 'new_v_ada_w', 'new_v_ada_b', 'new_v_pre_g', 'new_v_post_g', 'new_v_sgu_w_in', 'new_v_sgu_norm_g', 'new_v_sgu_w_s', 'new_v_sgu_b_s', 'new_v_sgu_w_out', 'new_v_mla_w_in', 'new_v_mla_q_norm_g', 'new_v_mla_kv_norm_g', 'new_v_mla_w_uq', 'new_v_mla_w_ukv', 'new_v_mla_w_out']
TWIN_LEAF_KINDS = {'loss': 'loss', 'grad_x': 'grad_x', 'grad_ada_w': 'grad_w', 'grad_ada_b': 'grad_w', 'grad_pre_g': 'grad_w', 'grad_post_g': 'grad_w', 'grad_sgu_w_in': 'grad_w', 'grad_sgu_norm_g': 'grad_w', 'grad_sgu_w_s': 'grad_w', 'grad_sgu_b_s': 'grad_w', 'grad_sgu_w_out': 'grad_w', 'grad_mla_w_in': 'grad_w', 'grad_mla_q_norm_g': 'grad_w', 'grad_mla_kv_norm_g': 'grad_w', 'grad_mla_w_uq': 'grad_w', 'grad_mla_w_ukv': 'grad_w', 'grad_mla_w_out': 'grad_w', 'delta_ada_w': 'delta_w', 'delta_ada_b': 'delta_w', 'delta_pre_g': 'delta_w', 'delta_post_g': 'delta_w', 'delta_sgu_w_in': 'delta_w', 'delta_sgu_norm_g': 'delta_w', 'delta_sgu_w_s': 'delta_w', 'delta_sgu_b_s': 'delta_w', 'delta_sgu_w_out': 'delta_w', 'delta_mla_w_in': 'delta_w', 'delta_mla_q_norm_g': 'delta_w', 'delta_mla_kv_norm_g': 'delta_w', 'delta_mla_w_uq': 'delta_w', 'delta_mla_w_ukv': 'delta_w', 'delta_mla_w_out': 'delta_w', 'new_m_ada_w': 'new_m', 'new_m_ada_b': 'new_m', 'new_m_pre_g': 'new_m', 'new_m_post_g': 'new_m', 'new_m_sgu_w_in': 'new_m', 'new_m_sgu_norm_g': 'new_m', 'new_m_sgu_w_s': 'new_m', 'new_m_sgu_b_s': 'new_m', 'new_m_sgu_w_out': 'new_m', 'new_m_mla_w_in': 'new_m', 'new_m_mla_q_norm_g': 'new_m', 'new_m_mla_kv_norm_g': 'new_m', 'new_m_mla_w_uq': 'new_m', 'new_m_mla_w_ukv': 'new_m', 'new_m_mla_w_out': 'new_m', 'new_v_ada_w': 'new_v', 'new_v_ada_b': 'new_v', 'new_v_pre_g': 'new_v', 'new_v_post_g': 'new_v', 'new_v_sgu_w_in': 'new_v', 'new_v_sgu_norm_g': 'new_v', 'new_v_sgu_w_s': 'new_v', 'new_v_sgu_b_s': 'new_v', 'new_v_sgu_w_out': 'new_v', 'new_v_mla_w_in': 'new_v', 'new_v_mla_q_norm_g': 'new_v', 'new_v_mla_kv_norm_g': 'new_v', 'new_v_mla_w_uq': 'new_v', 'new_v_mla_w_ukv': 'new_v', 'new_v_mla_w_out': 'new_v'}


def _forward(args):
    return _fwd_reference(*[args[k] for k in FWD_PARAMS])


def _output_shape():
    out = _jax.eval_shape(lambda: _forward(_fwd_setup_inputs(0)))
    return out.shape, out.dtype

N_MICROBATCH = 1
ADAM_LR = 0.001
ADAM_B1 = 0.9
ADAM_B2 = 0.999
ADAM_EPS = 1e-08
ADAM_WD = 0.01
ADAM_STEP = 10
PER_EXAMPLE_BATCH_AXIS = {'x': 0, 'c': 0, 'loss_target': 0}
SHARED_INPUTS = []
_WEIGHT_DTYPES = {'ada_w': _jnp.float32, 'ada_b': _jnp.float32, 'pre_g': _jnp.float32, 'post_g': _jnp.float32, 'sgu_w_in': _jnp.float32, 'sgu_norm_g': _jnp.float32, 'sgu_w_s': _jnp.float32, 'sgu_b_s': _jnp.float32, 'sgu_w_out': _jnp.float32, 'mla_w_in': _jnp.float32, 'mla_q_norm_g': _jnp.float32, 'mla_kv_norm_g': _jnp.float32, 'mla_w_uq': _jnp.float32, 'mla_w_ukv': _jnp.float32, 'mla_w_out': _jnp.float32}
MOMENT_SCALE = {'ada_w': 8.643923e-01, 'ada_b': 1.629668e+00, 'pre_g': 7.504856e-02, 'post_g': 1.781647e+00, 'sgu_w_in': 3.521779e-02, 'sgu_norm_g': 2.166185e-02, 'sgu_w_s': 3.053168e-02, 'sgu_b_s': 3.555488e-02, 'sgu_w_out': 7.579116e-02, 'mla_w_in': 1.794114e-01, 'mla_q_norm_g': 4.256058e-02, 'mla_kv_norm_g': 4.460877e-01, 'mla_w_uq': 1.583679e-02, 'mla_w_ukv': 1.458047e-01, 'mla_w_out': 2.118251e-01}


def _to_microbatches(a, axis):
    t = _jnp.moveaxis(a, axis, 0)
    t = t.reshape((N_MICROBATCH, t.shape[0] // N_MICROBATCH) + t.shape[1:])
    return _jnp.moveaxis(t, 1, axis + 1)


def setup_inputs(seed: int = 0) -> dict:
    inp = _fwd_setup_inputs(seed)
    key = _jax.random.fold_in(_jax.random.key(seed), 7919)
    shape, _ = _output_shape()
    out = dict(inp)
    out["loss_target"] = _jax.random.normal(_jax.random.fold_in(key, 0), shape, _jnp.float32)
    for i, name in enumerate(TWIN_WEIGHTS):
        w = inp[name].astype(_jnp.float32)
        if MOMENT_SCALE is None:
            s = _jnp.sqrt(_jnp.mean(_jnp.square(w)) + 1e-30)
        else:
            s = MOMENT_SCALE[name]
        km, kv = _jax.random.split(_jax.random.fold_in(key, i + 1))
        out[name] = w
        out["m_" + name] = s * _jax.random.normal(km, w.shape, _jnp.float32)
        out["v_" + name] = (s * s) * _jax.random.uniform(kv, w.shape, _jnp.float32, 0.5, 1.5)
    if N_MICROBATCH > 1:
        for name, axis in PER_EXAMPLE_BATCH_AXIS.items():
            out[name] = _to_microbatches(out[name], axis)
    return {'x': out['x'], 'c': out['c'], 'ada_w': out['ada_w'], 'ada_b': out['ada_b'], 'pre_g': out['pre_g'], 'post_g': out['post_g'], 'sgu_w_in': out['sgu_w_in'], 'sgu_norm_g': out['sgu_norm_g'], 'sgu_w_s': out['sgu_w_s'], 'sgu_b_s': out['sgu_b_s'], 'sgu_w_out': out['sgu_w_out'], 'mla_w_in': out['mla_w_in'], 'mla_q_norm_g': out['mla_q_norm_g'], 'mla_kv_norm_g': out['mla_kv_norm_g'], 'mla_w_uq': out['mla_w_uq'], 'mla_w_ukv': out['mla_w_ukv'], 'mla_w_out': out['mla_w_out'], 'loss_target': out['loss_target'], 'm_ada_w': out['m_ada_w'], 'm_ada_b': out['m_ada_b'], 'm_pre_g': out['m_pre_g'], 'm_post_g': out['m_post_g'], 'm_sgu_w_in': out['m_sgu_w_in'], 'm_sgu_norm_g': out['m_sgu_norm_g'], 'm_sgu_w_s': out['m_sgu_w_s'], 'm_sgu_b_s': out['m_sgu_b_s'], 'm_sgu_w_out': out['m_sgu_w_out'], 'm_mla_w_in': out['m_mla_w_in'], 'm_mla_q_norm_g': out['m_mla_q_norm_g'], 'm_mla_kv_norm_g': out['m_mla_kv_norm_g'], 'm_mla_w_uq': out['m_mla_w_uq'], 'm_mla_w_ukv': out['m_mla_w_ukv'], 'm_mla_w_out': out['m_mla_w_out'], 'v_ada_w': out['v_ada_w'], 'v_ada_b': out['v_ada_b'], 'v_pre_g': out['v_pre_g'], 'v_post_g': out['v_post_g'], 'v_sgu_w_in': out['v_sgu_w_in'], 'v_sgu_norm_g': out['v_sgu_norm_g'], 'v_sgu_w_s': out['v_sgu_w_s'], 'v_sgu_b_s': out['v_sgu_b_s'], 'v_sgu_w_out': out['v_sgu_w_out'], 'v_mla_w_in': out['v_mla_w_in'], 'v_mla_q_norm_g': out['v_mla_q_norm_g'], 'v_mla_kv_norm_g': out['v_mla_kv_norm_g'], 'v_mla_w_uq': out['v_mla_w_uq'], 'v_mla_w_ukv': out['v_mla_w_ukv'], 'v_mla_w_out': out['v_mla_w_out']}


def _loss(weights, diff, rest, loss_target):
    with _jax.named_scope("forward"):
        args = {**rest, TWIN_DIFF_INPUT: diff, **{k: w.astype(_WEIGHT_DTYPES[k]) for k, w in weights.items()}}
        y = _forward(args)
    with _jax.named_scope("loss_head"):
        err = _jnp.square(y.astype(_jnp.float32) - loss_target)
        return 0.5 * _jnp.sum(_jnp.mean(err, axis=-1)) if err.ndim else 0.5 * err


def _adamw(w, g, m, v):
    m = ADAM_B1 * m + (1.0 - ADAM_B1) * g
    v = ADAM_B2 * v + (1.0 - ADAM_B2) * _jnp.square(g)
    m_hat = m / (1.0 - ADAM_B1 ** ADAM_STEP)
    v_hat = v / (1.0 - ADAM_B2 ** ADAM_STEP)
    delta = -ADAM_LR * (m_hat / (_jnp.sqrt(v_hat) + ADAM_EPS) + ADAM_WD * w)
    return delta, m, v


def reference(x, c, ada_w, ada_b, pre_g, post_g, sgu_w_in, sgu_norm_g, sgu_w_s, sgu_b_s, sgu_w_out, mla_w_in, mla_q_norm_g, mla_kv_norm_g, mla_w_uq, mla_w_ukv, mla_w_out, loss_target, m_ada_w, m_ada_b, m_pre_g, m_post_g, m_sgu_w_in, m_sgu_norm_g, m_sgu_w_s, m_sgu_b_s, m_sgu_w_out, m_mla_w_in, m_mla_q_norm_g, m_mla_kv_norm_g, m_mla_w_uq, m_mla_w_ukv, m_mla_w_out, v_ada_w, v_ada_b, v_pre_g, v_post_g, v_sgu_w_in, v_sgu_norm_g, v_sgu_w_s, v_sgu_b_s, v_sgu_w_out, v_mla_w_in, v_mla_q_norm_g, v_mla_kv_norm_g, v_mla_w_uq, v_mla_w_ukv, v_mla_w_out):
    given = dict(x=x, c=c, ada_w=ada_w, ada_b=ada_b, pre_g=pre_g, post_g=post_g, sgu_w_in=sgu_w_in, sgu_norm_g=sgu_norm_g, sgu_w_s=sgu_w_s, sgu_b_s=sgu_b_s, sgu_w_out=sgu_w_out, mla_w_in=mla_w_in, mla_q_norm_g=mla_q_norm_g, mla_kv_norm_g=mla_kv_norm_g, mla_w_uq=mla_w_uq, mla_w_ukv=mla_w_ukv, mla_w_out=mla_w_out, loss_target=loss_target, m_ada_w=m_ada_w, m_ada_b=m_ada_b, m_pre_g=m_pre_g, m_post_g=m_post_g, m_sgu_w_in=m_sgu_w_in, m_sgu_norm_g=m_sgu_norm_g, m_sgu_w_s=m_sgu_w_s, m_sgu_b_s=m_sgu_b_s, m_sgu_w_out=m_sgu_w_out, m_mla_w_in=m_mla_w_in, m_mla_q_norm_g=m_mla_q_norm_g, m_mla_kv_norm_g=m_mla_kv_norm_g, m_mla_w_uq=m_mla_w_uq, m_mla_w_ukv=m_mla_w_ukv, m_mla_w_out=m_mla_w_out, v_ada_w=v_ada_w, v_ada_b=v_ada_b, v_pre_g=v_pre_g, v_post_g=v_post_g, v_sgu_w_in=v_sgu_w_in, v_sgu_norm_g=v_sgu_norm_g, v_sgu_w_s=v_sgu_w_s, v_sgu_b_s=v_sgu_b_s, v_sgu_w_out=v_sgu_w_out, v_mla_w_in=v_mla_w_in, v_mla_q_norm_g=v_mla_q_norm_g, v_mla_kv_norm_g=v_mla_kv_norm_g, v_mla_w_uq=v_mla_w_uq, v_mla_w_ukv=v_mla_w_ukv, v_mla_w_out=v_mla_w_out)
    weights = {n: given[n] for n in TWIN_WEIGHTS}
    shared = {n: given[n] for n in SHARED_INPUTS}
    per_example = {n: given[n] for n in ['x', 'c']}
    grad_fn = _jax.value_and_grad(_loss, argnums=(0, 1))

    def one_microbatch(ex, loss_target):
        ex = dict(ex)
        diff = ex.pop(TWIN_DIFF_INPUT)
        return grad_fn(weights, diff, {**shared, **ex}, loss_target)

    if N_MICROBATCH == 1:
        loss, (grad_w, grad_x) = one_microbatch(per_example, given["loss_target"])
    else:
        def body(carry, xs):
            loss_sum, grad_sum = carry
            l_k, (gw_k, gx_k) = one_microbatch(xs[0], xs[1])
            with _jax.named_scope("update"):
                return (loss_sum + l_k, _jax.tree.map(_jnp.add, grad_sum, gw_k)), gx_k

        init = (_jnp.zeros((), _jnp.float32), _jax.tree.map(_jnp.zeros_like, weights))
        (loss, grad_w), grad_x = _jax.lax.scan(body, init, (per_example, given["loss_target"]))
    with _jax.named_scope("update"):
        delta_w, new_m, new_v = {}, {}, {}
        for n in TWIN_WEIGHTS:
            delta_w[n], new_m[n], new_v[n] = _adamw(weights[n], grad_w[n], given["m_" + n], given["v_" + n])
    return (loss, grad_x, *[grad_w[n] for n in TWIN_WEIGHTS], *[delta_w[n] for n in TWIN_WEIGHTS],
            *[new_m[n] for n in TWIN_WEIGHTS], *[new_v[n] for n in TWIN_WEIGHTS])
```

```python
import functools
import math

import numpy as np
import jax
import jax.numpy as jnp
from jax import lax
from jax.experimental import pallas as pl
from jax.experimental.pallas import tpu as pltpu

F32 = jnp.float32
BF16 = jnp.bfloat16
MESH = pl.DeviceIdType.MESH

N_DEV = 8
DEPTH = 4
N_MIX = 2
NORM_EPS = 1e-6
CHUNK = 64
SGU_BLOCK = 128
SGU_GROUPS = 16
HEADS = 16
Q_RANK = 448
Q_RANK_PAD = 512
KV_RANK = 512
NOPE = 128
ROPE = 64
HALF = ROPE // 2
V_DIM = 128
HEAD_PAD = 256
ROPE_THETA = 10000.0
MLA_WIDTH = HEADS * V_DIM
LANE = 128
SUBLANE = 8
PROJ_CQ = 0
PROJ_CKV = Q_RANK_PAD
PROJ_KR = Q_RANK_PAD + KV_RANK
PROJ_Z = PROJ_KR + LANE
PROJ_W = PROJ_Z + MLA_WIDTH

ADAM_LR = 0.001
ADAM_B1 = 0.9
ADAM_B2 = 0.999
ADAM_EPS = 1e-08
ADAM_WD = 0.01
ADAM_STEP = 10

VMEM_LIMIT = 56 * 1024 * 1024
ATT_BLK = 512
ROW_BLK = 256
MM_TM, MM_TN, MM_TK = 1024, 1024, 512
INV_SQRT2 = 1.0 / math.sqrt(2.0)
INV_SQRT_2PI = 1.0 / math.sqrt(2.0 * math.pi)


def _pcall(body, comm=False, **kw):
    return pl.pallas_call(body, **kw)


def _params(sem=None):
    return pltpu.CompilerParams(dimension_semantics=sem, vmem_limit_bytes=VMEM_LIMIT)


def _pick(dim, pref):
    if dim <= pref:
        return dim
    t = (pref // LANE) * LANE
    while t >= LANE:
        if dim % t == 0:
            return t
        t -= LANE
    return dim


def _gelu(x):
    return 0.5 * x * (1.0 + lax.erf(x * INV_SQRT2))


def _gelu_grad(x):
    return 0.5 * (1.0 + lax.erf(x * INV_SQRT2)) + x * jnp.exp(-0.5 * x * x) * INV_SQRT_2PI


def _sigmoid(x):
    return 1.0 / (1.0 + jnp.exp(-x))


def _dot_nt(a, b):
    return lax.dot_general(a, b, (((1,), (1,)), ((), ())), preferred_element_type=F32)


def _dot_tn(a, b):
    return lax.dot_general(a, b, (((0,), (0,)), ((), ())), preferred_element_type=F32)


def _mm(a, b, dims, out_dtype, name):
    if dims == "nn":
        (m, k), (k2, n) = a.shape, b.shape
    elif dims == "nt":
        (m, k), (n, k2) = a.shape, b.shape
    else:
        (k, m), (k2, n) = a.shape, b.shape
    assert k == k2, (a.shape, b.shape, dims)
    tm, tn, tk = _pick(m, MM_TM), _pick(n, MM_TN), _pick(k, MM_TK)
    nk = k // tk

    def body(a_ref, b_ref, o_ref, acc_ref):
        kk = pl.program_id(2)

        @pl.when(kk == 0)
        def _():
            acc_ref[...] = jnp.zeros_like(acc_ref)

        if dims == "nn":
            p = jnp.dot(a_ref[...], b_ref[...], preferred_element_type=F32)
        elif dims == "nt":
            p = _dot_nt(a_ref[...], b_ref[...])
        else:
            p = _dot_tn(a_ref[...], b_ref[...])
        acc_ref[...] += p

        @pl.when(kk == nk - 1)
        def _():
            o_ref[...] = acc_ref[...].astype(o_ref.dtype)

    if dims == "tn":
        a_spec = pl.BlockSpec((tk, tm), lambda i, j, kk: (kk, i))
    else:
        a_spec = pl.BlockSpec((tm, tk), lambda i, j, kk: (i, kk))
    if dims == "nt":
        b_spec = pl.BlockSpec((tn, tk), lambda i, j, kk: (j, kk))
    else:
        b_spec = pl.BlockSpec((tk, tn), lambda i, j, kk: (kk, j))
    return _pcall(
        body, name=name,
        grid=(m // tm, n // tn, nk),
        in_specs=[a_spec, b_spec],
        out_specs=pl.BlockSpec((tm, tn), lambda i, j, kk: (i, j)),
        out_shape=jax.ShapeDtypeStruct((m, n), out_dtype),
        scratch_shapes=[pltpu.VMEM((tm, tn), F32)],
        compiler_params=_params(("parallel", "parallel", "arbitrary")),
    )(a, b)


def _row_spec(ts, d):
    return pl.BlockSpec((ts, d), lambda i: (i, 0))


def _vec_spec(d):
    return pl.BlockSpec((1, d), lambda i: (0, 0))


def _pre_fwd(x, g, scale, shift, name):
    s, d = x.shape
    ts = _pick(s, ROW_BLK)

    def body(x_ref, g_ref, sc_ref, sh_ref, h_ref):
        xv = x_ref[...]
        r = lax.rsqrt(jnp.mean(xv * xv, axis=-1, keepdims=True) + NORM_EPS)
        h_ref[...] = ((xv * r * g_ref[...]) * (1.0 + sc_ref[...]) + sh_ref[...]).astype(BF16)

    return _pcall(
        body, name=name, grid=(s // ts,),
        in_specs=[_row_spec(ts, d), _vec_spec(d), _vec_spec(d), _vec_spec(d)],
        out_specs=_row_spec(ts, d),
        out_shape=jax.ShapeDtypeStruct((s, d), BF16),
        compiler_params=_params(("parallel",)),
    )(x, g, scale, shift)


def _post_fwd(x, out, gate, g, name):
    s, d = x.shape
    ts = _pick(s, ROW_BLK)

    def body(x_ref, o_ref, gate_ref, g_ref, y_ref):
        o = o_ref[...]
        r = lax.rsqrt(jnp.mean(o * o, axis=-1, keepdims=True) + NORM_EPS)
        y_ref[...] = x_ref[...] + gate_ref[...] * (o * r * g_ref[...])

    return _pcall(
        body, name=name, grid=(s // ts,),
        in_specs=[_row_spec(ts, d), _row_spec(ts, d), _vec_spec(d), _vec_spec(d)],
        out_specs=_row_spec(ts, d),
        out_shape=jax.ShapeDtypeStruct((s, d), F32),
        compiler_params=_params(("parallel",)),
    )(x, out, gate, g)


def _loss_head(xf, tgt, name):
    s, d = xf.shape
    ts = _pick(s, ROW_BLK)
    ns = s // ts

    def body(x_ref, t_ref, dx_ref, loss_ref, acc_ref):
        i = pl.program_id(0)

        @pl.when(i == 0)
        def _():
            acc_ref[...] = jnp.zeros_like(acc_ref)

        e = x_ref[...] - t_ref[...]
        dx_ref[...] = e * (1.0 / d)
        acc_ref[...] += jnp.sum(e * e, axis=0, keepdims=True)

        @pl.when(i == ns - 1)
        def _():
            tot = jnp.sum(acc_ref[...], axis=1, keepdims=True) * (0.5 / d)
            loss_ref[...] = jnp.broadcast_to(tot, loss_ref.shape)

    return _pcall(
        body, name=name, grid=(ns,),
        in_specs=[_row_spec(ts, d), _row_spec(ts, d)],
        out_specs=[_row_spec(ts, d), pl.BlockSpec((1, LANE), lambda i: (0, 0))],
        out_shape=[jax.ShapeDtypeStruct((s, d), F32), jax.ShapeDtypeStruct((1, LANE), F32)],
        scratch_shapes=[pltpu.VMEM((1, d), F32)],
        compiler_params=_params(("arbitrary",)),
    )(xf, tgt)


def _post_bwd(dxo, out, gate, g, name):
    s, d = dxo.shape
    ts = _pick(s, ROW_BLK)

    def body(dx_ref, o_ref, gate_ref, g_ref, do_ref, dgate_ref, dg_ref):
        i = pl.program_id(0)

        @pl.when(i == 0)
        def _():
            dgate_ref[...] = jnp.zeros_like(dgate_ref)
            dg_ref[...] = jnp.zeros_like(dg_ref)

        o = o_ref[...]
        dx = dx_ref[...]
        gv = g_ref[...]
        r = lax.rsqrt(jnp.mean(o * o, axis=-1, keepdims=True) + NORM_EPS)
        n = o * r
        dyn = dx * gate_ref[...]
        dgate_ref[...] += jnp.sum(dx * (n * gv), axis=0, keepdims=True)
        dg_ref[...] += jnp.sum(dyn * n, axis=0, keepdims=True)
        dn = dyn * gv
        do_ref[...] = (r * (dn - n * jnp.mean(dn * n, axis=-1, keepdims=True))).astype(BF16)

    return _pcall(
        body, name=name, grid=(s // ts,),
        in_specs=[_row_spec(ts, d), _row_spec(ts, d), _vec_spec(d), _vec_spec(d)],
        out_specs=[_row_spec(ts, d), _vec_spec(d), _vec_spec(d)],
        out_shape=[jax.ShapeDtypeStruct((s, d), BF16), jax.ShapeDtypeStruct((1, d), F32),
                   jax.ShapeDtypeStruct((1, d), F32)],
        compiler_params=_params(("arbitrary",)),
    )(dxo, out, gate, g)


def _pre_bwd(dh, x, dxo, g, scale, name):
    s, d = x.shape
    ts = _pick(s, ROW_BLK)

    def body(dh_ref, x_ref, dxo_ref, g_ref, sc_ref, dx_ref, dsh_ref, dsc_ref, dg_ref):
        i = pl.program_id(0)

        @pl.when(i == 0)
        def _():
            dsh_ref[...] = jnp.zeros_like(dsh_ref)
            dsc_ref[...] = jnp.zeros_like(dsc_ref)
            dg_ref[...] = jnp.zeros_like(dg_ref)

        xv = x_ref[...]
        dhv = dh_ref[...]
        gv = g_ref[...]
        one_sc = 1.0 + sc_ref[...]
        r = lax.rsqrt(jnp.mean(xv * xv, axis=-1, keepdims=True) + NORM_EPS)
        n = xv * r
        dsh_ref[...] += jnp.sum(dhv, axis=0, keepdims=True)
        dsc_ref[...] += jnp.sum(dhv * (n * gv), axis=0, keepdims=True)
        dng = dhv * one_sc
        dg_ref[...] += jnp.sum(dng * n, axis=0, keepdims=True)
        dn = dng * gv
        dx_ref[...] = dxo_ref[...] + r * (dn - n * jnp.mean(dn * n, axis=-1, keepdims=True))

    return _pcall(
        body, name=name, grid=(s // ts,),
        in_specs=[_row_spec(ts, d), _row_spec(ts, d), _row_spec(ts, d), _vec_spec(d), _vec_spec(d)],
        out_specs=[_row_spec(ts, d), _vec_spec(d), _vec_spec(d), _vec_spec(d)],
        out_shape=[jax.ShapeDtypeStruct((s, d), F32)] + [jax.ShapeDtypeStruct((1, d), F32)] * 3,
        compiler_params=_params(("arbitrary",)),
    )(dh, x, dxo, g, scale)


def _sgu_mask():
    t = lax.broadcasted_iota(jnp.int32, (SGU_BLOCK, SGU_BLOCK), 0) // CHUNK
    s = lax.broadcasted_iota(jnp.int32, (SGU_BLOCK, SGU_BLOCK), 1) // CHUNK
    return s <= t


def _sgu_norm(v_pre, g):
    e = v_pre.shape[-1]
    vg = _gelu(v_pre)
    mu = jnp.sum(vg, axis=-1, keepdims=True) * (1.0 / e)
    dlt = vg - mu
    var = jnp.sum(dlt * dlt, axis=-1, keepdims=True) * (1.0 / e)
    rstd = lax.rsqrt(var + NORM_EPS)
    vhat = dlt * rstd
    return vhat, rstd, (vhat * g).astype(BF16)


def _sgu_mid_fwd(uvz, norm_g, w_s, bias_full, name):
    s, e3 = uvz.shape
    e = e3 // 3
    gd = e // SGU_GROUPS
    nb = s // SGU_BLOCK

    def body(uvz_ref, g_ref, w_ref, b_ref, y_ref, wsc):
        @pl.when(pl.program_id(0) == 0)
        def _():
            msk = _sgu_mask()
            for gi in range(SGU_GROUPS):
                wsc[gi] = jnp.where(msk, w_ref[gi], 0.0).astype(BF16)

        _, _, vb = _sgu_norm(uvz_ref[:, e:2 * e], g_ref[...])
        for gi in range(SGU_GROUPS):
            lo = gi * gd
            vm = jnp.dot(wsc[gi], vb[:, lo:lo + gd], preferred_element_type=F32) + b_ref[:, lo:lo + gd]
            zg = uvz_ref[:, 2 * e + lo:2 * e + lo + gd]
            y_ref[:, lo:lo + gd] = (_gelu(uvz_ref[:, lo:lo + gd]) * vm * (zg * _sigmoid(zg))).astype(BF16)

    return _pcall(
        body, name=name, grid=(nb,),
        in_specs=[pl.BlockSpec((SGU_BLOCK, e3), lambda n: (n, 0)),
                  pl.BlockSpec((1, e), lambda n: (0, 0)),
                  pl.BlockSpec((SGU_GROUPS, SGU_BLOCK, SGU_BLOCK), lambda n: (0, 0, 0)),
                  pl.BlockSpec((SGU_BLOCK, e), lambda n: (0, 0))],
        out_specs=pl.BlockSpec((SGU_BLOCK, e), lambda n: (n, 0)),
        out_shape=jax.ShapeDtypeStruct((s, e), BF16),
        scratch_shapes=[pltpu.VMEM((SGU_GROUPS, SGU_BLOCK, SGU_BLOCK), BF16)],
        compiler_params=_params(("arbitrary",)),
    )(uvz, norm_g, w_s, bias_full)


def _sgu_mid_bwd(uvz, dy, norm_g, w_s, bias_full, name):
    s, e3 = uvz.shape
    e = e3 // 3
    gd = e // SGU_GROUPS
    nb = s // SGU_BLOCK

    def body(uvz_ref, dy_ref, g_ref, w_ref, b_ref, d_ref, dw_ref, db_ref, dg_ref, wsc, wtsc, dvh_sc, dbacc):
        n = pl.program_id(0)

        @pl.when(n == 0)
        def _():
            msk = _sgu_mask()
            for gi in range(SGU_GROUPS):
                wm = jnp.where(msk, w_ref[gi], 0.0)
                wsc[gi] = wm.astype(BF16)
                wtsc[gi] = wm.T.astype(BF16)
            dw_ref[...] = jnp.zeros_like(dw_ref)
            dg_ref[...] = jnp.zeros_like(dg_ref)
            dbacc[...] = jnp.zeros_like(dbacc)

        v_pre = uvz_ref[:, e:2 * e]
        gv = g_ref[...]
        vhat, rstd, vb = _sgu_norm(v_pre, gv)
        s1 = jnp.zeros((SGU_BLOCK, 1), F32)
        s2 = jnp.zeros((SGU_BLOCK, 1), F32)
        for gi in range(SGU_GROUPS):
            lo = gi * gd
            u_pre = uvz_ref[:, lo:lo + gd]
            zg = uvz_ref[:, 2 * e + lo:2 * e + lo + gd]
            dyg = dy_ref[:, lo:lo + gd]
            ug = _gelu(u_pre)
            sig = _sigmoid(zg)
            vbg = vb[:, lo:lo + gd]
            vhg = vhat[:, lo:lo + gd]
            vm = jnp.dot(wsc[gi], vbg, preferred_element_type=F32) + b_ref[:, lo:lo + gd]
            t = dyg * (zg * sig)
            d_ref[:, lo:lo + gd] = (t * vm * _gelu_grad(u_pre)).astype(BF16)
            dvm = t * ug
            d_ref[:, 2 * e + lo:2 * e + lo + gd] = (dyg * ug * vm * (sig * (1.0 + zg * (1.0 - sig)))).astype(BF16)
            dvm_b = dvm.astype(BF16)
            dv = jnp.dot(wtsc[gi], dvm_b, preferred_element_type=F32)
            dw_ref[gi] += _dot_nt(dvm_b, vbg)
            dbacc[:, lo:lo + gd] += dvm
            dg_ref[:, lo:lo + gd] += jnp.sum(dv * vhg, axis=0, keepdims=True)
            dvh = dv * gv[:, lo:lo + gd]
            dvh_sc[:, lo:lo + gd] = dvh
            s1 = s1 + jnp.sum(dvh, axis=-1, keepdims=True)
            s2 = s2 + jnp.sum(dvh * vhg, axis=-1, keepdims=True)
        dvg = rstd * (dvh_sc[...] - s1 * (1.0 / e) - vhat * (s2 * (1.0 / e)))
        d_ref[:, e:2 * e] = (dvg * _gelu_grad(v_pre)).astype(BF16)

        @pl.when(n == nb - 1)
        def _():
            msk = _sgu_mask()
            for gi in range(SGU_GROUPS):
                dw_ref[gi] = jnp.where(msk, dw_ref[gi], 0.0)
                db_ref[gi] = jnp.sum(dbacc[:, gi * gd:(gi + 1) * gd], axis=1, keepdims=True)

    return _pcall(
        body, name=name, grid=(nb,),
        in_specs=[pl.BlockSpec((SGU_BLOCK, e3), lambda n: (n, 0)),
                  pl.BlockSpec((SGU_BLOCK, e), lambda n: (n, 0)),
                  pl.BlockSpec((1, e), lambda n: (0, 0)),
                  pl.BlockSpec((SGU_GROUPS, SGU_BLOCK, SGU_BLOCK), lambda n: (0, 0, 0)),
                  pl.BlockSpec((SGU_BLOCK, e), lambda n: (0, 0))],
        out_specs=[pl.BlockSpec((SGU_BLOCK, e3), lambda n: (n, 0)),
                   pl.BlockSpec((SGU_GROUPS, SGU_BLOCK, SGU_BLOCK), lambda n: (0, 0, 0)),
                   pl.BlockSpec((SGU_GROUPS, SGU_BLOCK, 1), lambda n: (0, 0, 0)),
                   pl.BlockSpec((1, e), lambda n: (0, 0))],
        out_shape=[jax.ShapeDtypeStruct((s, e3), BF16),
                   jax.ShapeDtypeStruct((SGU_GROUPS, SGU_BLOCK, SGU_BLOCK), F32),
                   jax.ShapeDtypeStruct((SGU_GROUPS, SGU_BLOCK, 1), F32),
                   jax.ShapeDtypeStruct((1, e), F32)],
        scratch_shapes=[pltpu.VMEM((SGU_GROUPS, SGU_BLOCK, SGU_BLOCK), BF16),
                        pltpu.VMEM((SGU_GROUPS, SGU_BLOCK, SGU_BLOCK), BF16),
                        pltpu.VMEM((SGU_BLOCK, e), F32),
                        pltpu.VMEM((SGU_BLOCK, e), F32)],
        compiler_params=_params(("arbitrary",)),
    )(uvz, dy, norm_g, w_s, bias_full)


def _rope_tables(s):
    pos = jnp.arange(s, dtype=F32)
    inv_freq = ROPE_THETA ** (-jnp.arange(0, ROPE, 2, dtype=F32) / ROPE)
    ang = pos[:, None] * inv_freq[None, :]
    cos, sin = jnp.cos(ang), jnp.sin(ang)
    z32 = jnp.zeros((s, HALF), F32)
    z64 = jnp.zeros((s, ROPE), F32)
    ck = jnp.concatenate([cos, cos, z64], axis=1)
    s1k = jnp.concatenate([-sin, z32, z64], axis=1)
    s2k = jnp.concatenate([z32, sin, z64], axis=1)
    one = jnp.ones((s, NOPE), F32)
    zn = jnp.zeros((s, NOPE), F32)
    return (ck, s1k, s2k), (jnp.concatenate([one, ck], axis=1), jnp.concatenate([zn, s1k], axis=1),
                            jnp.concatenate([zn, s2k], axis=1))


def _rot(x, c, s1, s2):
    w = x.shape[-1]
    return x * c + pltpu.roll(x, w - HALF, 1) * s1 + pltpu.roll(x, HALF, 1) * s2


def _rms(cv, n_real):
    r = lax.rsqrt(jnp.sum(cv * cv, axis=-1, keepdims=True) * (1.0 / n_real) + NORM_EPS)
    return r, cv * r


def _mla_norm_fwd(proj, gq, gkv, tabs, name):
    s = proj.shape[0]
    ts = _pick(s, ROW_BLK)
    ck, s1k, s2k = tabs

    def body(cq_ref, ckv_ref, kr_ref, gq_ref, gkv_ref, c_ref, s1_ref, s2_ref, qn_ref, kvn_ref, kro_ref):
        _, nq = _rms(cq_ref[...], Q_RANK)
        qn_ref[...] = (nq * gq_ref[...]).astype(BF16)
        _, nkv = _rms(ckv_ref[...], KV_RANK)
        kvn_ref[...] = (nkv * gkv_ref[...]).astype(BF16)
        kro_ref[...] = _rot(kr_ref[...], c_ref[...], s1_ref[...], s2_ref[...]).astype(BF16)

    tab = pl.BlockSpec((ts, LANE), lambda i: (i, 0))
    return _pcall(
        body, name=name, grid=(s // ts,),
        in_specs=[pl.BlockSpec((ts, Q_RANK_PAD), lambda i: (i, 0)),
                  pl.BlockSpec((ts, KV_RANK), lambda i: (i, PROJ_CKV // KV_RANK)),
                  pl.BlockSpec((ts, LANE), lambda i: (i, PROJ_KR // LANE)),
                  _vec_spec(Q_RANK_PAD), _vec_spec(KV_RANK), tab, tab, tab],
        out_specs=[pl.BlockSpec((ts, Q_RANK_PAD), lambda i: (i, 0)),
                   pl.BlockSpec((ts, KV_RANK), lambda i: (i, 0)), tab],
        out_shape=[jax.ShapeDtypeStruct((s, Q_RANK_PAD), BF16), jax.ShapeDtypeStruct((s, KV_RANK), BF16),
                   jax.ShapeDtypeStruct((s, LANE), BF16)],
        compiler_params=_params(("parallel",)),
    )(proj, proj, proj, gq, gkv, ck, s1k, s2k)


def _rope_heads(q, tabs, sign, out_dtype, name):
    s, w = q.shape
    ts = _pick(s, ROW_BLK)
    c, s1, s2 = tabs

    def body(q_ref, c_ref, s1_ref, s2_ref, o_ref):
        o_ref[...] = _rot(q_ref[...].astype(F32), c_ref[...], sign * s1_ref[...], sign * s2_ref[...]).astype(out_dtype)

    blk = pl.BlockSpec((ts, HEAD_PAD), lambda i, h: (i, h))
    tab = pl.BlockSpec((ts, HEAD_PAD), lambda i, h: (i, 0))
    return _pcall(
        body, name=name, grid=(s // ts, w // HEAD_PAD),
        in_specs=[blk, tab, tab, tab], out_specs=blk,
        out_shape=jax.ShapeDtypeStruct((s, w), out_dtype),
        compiler_params=_params(("parallel", "arbitrary")),
    )(q, c, s1, s2)


def _att_mask(qi, ki, tq, tk):
    qc = (qi * tq + lax.broadcasted_iota(jnp.int32, (tq, tk), 0)) // CHUNK
    kc = (ki * tk + lax.broadcasted_iota(jnp.int32, (tq, tk), 1)) // CHUNK
    return kc <= qc


def _attn_fwd(q_cat, kv, kr, proj, name):
    s = q_cat.shape[0]
    tb = _pick(s, ATT_BLK)
    nb = s // tb
    scale = (NOPE + ROPE) ** -0.5
    zcol = PROJ_Z // V_DIM

    def body(q_ref, kn_ref, v_ref, kr_ref, z_ref, o_ref, y_ref, lse_ref, m_sc, l_sc, acc_sc):
        qi, ki = pl.program_id(1), pl.program_id(2)

        @pl.when(ki == 0)
        def _():
            m_sc[...] = jnp.full_like(m_sc, -1e30)
            l_sc[...] = jnp.zeros_like(l_sc)
            acc_sc[...] = jnp.zeros_like(acc_sc)

        @pl.when(ki <= qi)
        def _():
            k = jnp.concatenate([kn_ref[...], kr_ref[...]], axis=1)
            sc = _dot_nt(q_ref[...], k) * scale
            sc = jnp.where(_att_mask(qi, ki, tb, tb), sc, -1e30)
            m_prev = m_sc[...]
            m_new = jnp.maximum(m_prev, jnp.max(sc, axis=-1, keepdims=True))
            alpha = jnp.exp(m_prev - m_new)
            p = jnp.exp(sc - m_new)
            l_sc[...] = alpha * l_sc[...] + jnp.sum(p, axis=-1, keepdims=True)
            acc_sc[...] = alpha * acc_sc[...] + jnp.dot(p.astype(BF16), v_ref[...], preferred_element_type=F32)
            m_sc[...] = m_new

        @pl.when(ki == nb - 1)
        def _():
            l = l_sc[...]
            o = acc_sc[...] / l
            z = z_ref[...]
            o_ref[...] = o.astype(BF16)
            y_ref[...] = (o * (z * _sigmoid(z))).astype(BF16)
            lse_ref[...] = m_sc[...] + jnp.log(l)

    kmap = lambda off: (lambda h, qi, ki: (jnp.minimum(ki, qi), 2 * h + off))
    oblk = pl.BlockSpec((tb, V_DIM), lambda h, qi, ki: (qi, h))
    return _pcall(
        body, name=name, grid=(HEADS, nb, nb),
        in_specs=[pl.BlockSpec((tb, HEAD_PAD), lambda h, qi, ki: (qi, h)),
                  pl.BlockSpec((tb, NOPE), kmap(0)),
                  pl.BlockSpec((tb, V_DIM), kmap(1)),
                  pl.BlockSpec((tb, LANE), lambda h, qi, ki: (jnp.minimum(ki, qi), 0)),
                  pl.BlockSpec((tb, V_DIM), lambda h, qi, ki: (qi, zcol + h))],
        out_specs=[oblk, oblk, pl.BlockSpec((None, tb, 1), lambda h, qi, ki: (h, qi, 0))],
        out_shape=[jax.ShapeDtypeStruct((s, MLA_WIDTH), BF16), jax.ShapeDtypeStruct((s, MLA_WIDTH), BF16),
                   jax.ShapeDtypeStruct((HEADS, s, 1), F32)],
        scratch_shapes=[pltpu.VMEM((tb, 1), F32), pltpu.VMEM((tb, 1), F32), pltpu.VMEM((tb, V_DIM), F32)],
        compiler_params=_params(("parallel", "parallel", "arbitrary")),
    )(q_cat, kv, kv, kr, proj)


def _attn_bwd(q_cat, kv, kr, do, o, lse, name):
    s = q_cat.shape[0]
    tb = _pick(s, ATT_BLK)
    nb = s // tb
    scale = (NOPE + ROPE) ** -0.5

    def body(q_ref, kn_ref, v_ref, kr_ref, do_ref, o_ref, lse_ref, dq_ref, dkv_ref, dkr_ref, dk_sc, dv_sc):
        ki, qi = pl.program_id(1), pl.program_id(2)

        @pl.when(jnp.logical_and(ki == 0, qi == 0))
        def _():
            dq_ref[...] = jnp.zeros_like(dq_ref)

        @pl.when(qi == 0)
        def _():
            dk_sc[...] = jnp.zeros_like(dk_sc)
            dv_sc[...] = jnp.zeros_like(dv_sc)

        @pl.when(qi >= ki)
        def _():
            q = q_ref[...]
            k = jnp.concatenate([kn_ref[...], kr_ref[...]], axis=1)
            dov = do_ref[...]
            sc = _dot_nt(q, k) * scale
            sc = jnp.where(_att_mask(qi, ki, tb, tb), sc, -1e30)
            p = jnp.exp(sc - lse_ref[...])
            dv_sc[...] += _dot_tn(p.astype(BF16), dov)
            dp = _dot_nt(dov, v_ref[...])
            delta = jnp.sum(dov.astype(F32) * o_ref[...].astype(F32), axis=-1, keepdims=True)
            ds = (p * (dp - delta) * scale).astype(BF16)
            dk_sc[...] += _dot_tn(ds, q)
            rows = pl.ds(pl.multiple_of(qi * tb, tb), tb)
            dq_ref[rows, :] += jnp.dot(ds, k, preferred_element_type=F32)

        @pl.when(qi == nb - 1)
        def _():
            dkv_ref[:, :NOPE] = dk_sc[:, :NOPE].astype(BF16)
            dkv_ref[:, NOPE:] = dv_sc[...].astype(BF16)
            dkr_ref[...] = dk_sc[:, NOPE:]

    qmap = lambda h, ki, qi: (jnp.maximum(qi, ki), h)
    return _pcall(
        body, name=name, grid=(HEADS, nb, nb),
        in_specs=[pl.BlockSpec((tb, HEAD_PAD), qmap),
                  pl.BlockSpec((tb, NOPE), lambda h, ki, qi: (ki, 2 * h)),
                  pl.BlockSpec((tb, V_DIM), lambda h, ki, qi: (ki, 2 * h + 1)),
                  pl.BlockSpec((tb, LANE), lambda h, ki, qi: (ki, 0)),
                  pl.BlockSpec((tb, V_DIM), qmap),
                  pl.BlockSpec((tb, V_DIM), qmap),
                  pl.BlockSpec((None, tb, 1), lambda h, ki, qi: (h, jnp.maximum(qi, ki), 0))],
        out_specs=[pl.BlockSpec((s, HEAD_PAD), lambda h, ki, qi: (0, h)),
                   pl.BlockSpec((tb, HEAD_PAD), lambda h, ki, qi: (ki, h)),
                   pl.BlockSpec((None, tb, LANE), lambda h, ki, qi: (h, ki, 0))],
        out_shape=[jax.ShapeDtypeStruct((s, HEADS * HEAD_PAD), F32),
                   jax.ShapeDtypeStruct((s, HEADS * HEAD_PAD), BF16),
                   jax.ShapeDtypeStruct((HEADS, s, LANE), F32)],
        scratch_shapes=[pltpu.VMEM((tb, HEAD_PAD), F32), pltpu.VMEM((tb, V_DIM), F32)],
        compiler_params=_params(("arbitrary", "arbitrary", "arbitrary")),
    )(q_cat, kv, kv, kr, do, o, lse)


def _mla_gate_bwd(dy, o, proj, name):
    s = dy.shape[0]
    ts = _pick(s, ROW_BLK)
    zcol = PROJ_Z // V_DIM

    def body(dy_ref, o_ref, z_ref, do_ref, dz_ref):
        z = z_ref[...]
        dyv = dy_ref[...]
        sig = _sigmoid(z)
        do_ref[...] = (dyv * (z * sig)).astype(BF16)
        dz_ref[...] = (dyv * o_ref[...].astype(F32) * (sig * (1.0 + z * (1.0 - sig)))).astype(BF16)

    blk = pl.BlockSpec((ts, V_DIM), lambda i, h: (i, h))
    return _pcall(
        body, name=name, grid=(s // ts, HEADS),
        in_specs=[blk, blk, pl.BlockSpec((ts, V_DIM), lambda i, h: (i, zcol + h))],
        out_specs=[blk, blk],
        out_shape=[jax.ShapeDtypeStruct((s, MLA_WIDTH), BF16), jax.ShapeDtypeStruct((s, MLA_WIDTH), BF16)],
        compiler_params=_params(("parallel", "parallel")),
    )(dy, o, proj)


def _mla_norm_bwd(dqn, dkvn, dkr_heads, proj, gq, gkv, tabs, name):
    s = proj.shape[0]
    ts = _pick(s, ROW_BLK)
    ck, s1k, s2k = tabs

    def rms_bwd(cv, dn_in, g, n_real):
        r, n = _rms(cv, n_real)
        dg = jnp.sum(dn_in * n, axis=0, keepdims=True)
        dn = dn_in * g
        dc = r * (dn - n * (jnp.sum(dn * n, axis=-1, keepdims=True) * (1.0 / n_real)))
        return dc, dg

    def body(dqn_ref, dkvn_ref, dkr_ref, cq_ref, ckv_ref, gq_ref, gkv_ref, c_ref, s1_ref, s2_ref,
             dp_ref, dgq_ref, dgkv_ref):
        @pl.when(pl.program_id(0) == 0)
        def _():
            dgq_ref[...] = jnp.zeros_like(dgq_ref)
            dgkv_ref[...] = jnp.zeros_like(dgkv_ref)

        dcq, dgq = rms_bwd(cq_ref[...], dqn_ref[...], gq_ref[...], Q_RANK)
        dckv, dgkv = rms_bwd(ckv_ref[...], dkvn_ref[...], gkv_ref[...], KV_RANK)
        dgq_ref[...] += dgq
        dgkv_ref[...] += dgkv
        dkr = dkr_ref[0]
        for h in range(1, HEADS):
            dkr = dkr + dkr_ref[h]
        dp_ref[:, PROJ_CQ:PROJ_CKV] = dcq.astype(BF16)
        dp_ref[:, PROJ_CKV:PROJ_KR] = dckv.astype(BF16)
        dp_ref[:, PROJ_KR:PROJ_Z] = _rot(dkr, c_ref[...], -s1_ref[...], -s2_ref[...]).astype(BF16)

    tab = pl.BlockSpec((ts, LANE), lambda i: (i, 0))
    return _pcall(
        body, name=name, grid=(s // ts,),
        in_specs=[pl.BlockSpec((ts, Q_RANK_PAD), lambda i: (i, 0)),
                  pl.BlockSpec((ts, KV_RANK), lambda i: (i, 0)),
                  pl.BlockSpec((HEADS, ts, LANE), lambda i: (0, i, 0)),
                  pl.BlockSpec((ts, Q_RANK_PAD), lambda i: (i, 0)),
                  pl.BlockSpec((ts, KV_RANK), lambda i: (i, PROJ_CKV // KV_RANK)),
                  _vec_spec(Q_RANK_PAD), _vec_spec(KV_RANK), tab, tab, tab],
        out_specs=[pl.BlockSpec((ts, PROJ_Z), lambda i: (i, 0)), _vec_spec(Q_RANK_PAD), _vec_spec(KV_RANK)],
        out_shape=[jax.ShapeDtypeStruct((s, PROJ_Z), BF16), jax.ShapeDtypeStruct((1, Q_RANK_PAD), F32),
                   jax.ShapeDtypeStruct((1, KV_RANK), F32)],
        compiler_params=_params(("arbitrary",)),
    )(dqn, dkvn, dkr_heads, proj, proj, gq, gkv, ck, s1k, s2k)


def _ada_mod(cond_raw, ada_w, bias_my, name):
    nl, d, ncol = ada_w.shape
    tk = _pick(d, 512)
    nk = d // tk

    def body(c_ref, w_ref, b_ref, o_ref, acc_ref):
        kk = pl.program_id(1)

        @pl.when(kk == 0)
        def _():
            acc_ref[...] = jnp.zeros_like(acc_ref)

        cv = c_ref[...]
        cond = (cv * _sigmoid(cv)).astype(BF16)
        acc_ref[...] += jnp.dot(cond, w_ref[...].astype(BF16), preferred_element_type=F32)

        @pl.when(kk == nk - 1)
        def _():
            o_ref[...] = acc_ref[...] + b_ref[...]

    return _pcall(
        body, name=name, grid=(nl, nk),
        in_specs=[pl.BlockSpec((N_DEV, tk), lambda l, kk: (0, kk)),
                  pl.BlockSpec((None, tk, ncol), lambda l, kk: (l, kk, 0)),
                  pl.BlockSpec((None, 1, ncol), lambda l, kk: (l, 0, 0))],
        out_specs=pl.BlockSpec((None, N_DEV, ncol), lambda l, kk: (l, 0, 0)),
        out_shape=jax.ShapeDtypeStruct((nl, N_DEV, ncol), F32),
        scratch_shapes=[pltpu.VMEM((N_DEV, ncol), F32)],
        compiler_params=_params(("parallel", "arbitrary")),
    )(cond_raw, ada_w, bias_my.reshape(nl, 1, ncol))


def _adam(w, g, m, v):
    m = ADAM_B1 * m + (1.0 - ADAM_B1) * g
    v = ADAM_B2 * v + (1.0 - ADAM_B2) * (g * g)
    m_hat = m / (1.0 - ADAM_B1 ** ADAM_STEP)
    v_hat = v / (1.0 - ADAM_B2 ** ADAM_STEP)
    delta = -ADAM_LR * (m_hat / (jnp.sqrt(v_hat) + ADAM_EPS) + ADAM_WD * w)
    return delta, m, v


def _ada_bwd_adam(cond_t, dmod_cols, w, m, v, name):
    nl, d, ncol = w.shape
    tk = _pick(d, 512)

    def body(c_ref, dm_ref, w_ref, m_ref, v_ref, g_ref, d_ref, mo_ref, vo_ref):
        cv = c_ref[...]
        cond = (cv * _sigmoid(cv)).astype(BF16)
        g = jnp.dot(cond, dm_ref[...].astype(BF16), preferred_element_type=F32)
        delta, m2, v2 = _adam(w_ref[...], g, m_ref[...], v_ref[...])
        g_ref[...] = g
        d_ref[...] = delta
        mo_ref[...] = m2
        vo_ref[...] = v2

    blk = pl.BlockSpec((None, tk, ncol), lambda l, kk: (l, kk, 0))
    shp = jax.ShapeDtypeStruct((nl, d, ncol), F32)
    return _pcall(
        body, name=name, grid=(nl, d // tk),
        in_specs=[pl.BlockSpec((tk, N_DEV), lambda l, kk: (kk, 0)),
                  pl.BlockSpec((None, N_DEV, ncol), lambda l, kk: (l, 0, 0)), blk, blk, blk],
        out_specs=[blk, blk, blk, blk], out_shape=[shp, shp, shp, shp],
        compiler_params=_params(("parallel", "parallel")),
    )(cond_t, dmod_cols, w, m, v)


def _adam_reduce(recv0, recv1, w, m, v, name):
    nl, r, c = w.shape
    tr = _pick(r, 128) if r % 128 == 0 else r
    tc = _pick(c, 1024)

    def body(r0_ref, r1_ref, w_ref, m_ref, v_ref, g_ref, d_ref, mo_ref, vo_ref):
        l = pl.program_id(0)

        def run(rr):
            g = rr[0].astype(F32)
            for sidx in range(1, N_DEV):
                g = g + rr[sidx].astype(F32)
            delta, m2, v2 = _adam(w_ref[...], g, m_ref[...], v_ref[...])
            g_ref[...] = g
            d_ref[...] = delta
            mo_ref[...] = m2
            vo_ref[...] = v2

        @pl.when(l == 0)
        def _():
            run(r0_ref)

        @pl.when(l == 1)
        def _():
            run(r1_ref)

    rblk = pl.BlockSpec((N_DEV, tr, tc), lambda l, i, j: (0, i, j))
    blk = pl.BlockSpec((None, tr, tc), lambda l, i, j: (l, i, j))
    shp = jax.ShapeDtypeStruct((nl, r, c), F32)
    return _pcall(
        body, name=name, grid=(nl, r // tr, c // tc),
        in_specs=[rblk, rblk, blk, blk, blk],
        out_specs=[blk, blk, blk, blk], out_shape=[shp, shp, shp, shp],
        compiler_params=_params(("arbitrary", "parallel", "parallel")),
    )(recv0.reshape(N_DEV, r, c), recv1.reshape(N_DEV, r, c), w, m, v)


def _adam_small(gathered, w, m, v, name):
    r = w.shape[0]
    tr = _pick(r, 512) if r % 512 == 0 else r

    def body(p_ref, w_ref, m_ref, v_ref, g_ref, d_ref, mo_ref, vo_ref):
        g = p_ref[0]
        for sidx in range(1, N_DEV):
            g = g + p_ref[sidx]
        delta, m2, v2 = _adam(w_ref[...], g, m_ref[...], v_ref[...])
        g_ref[...] = g
        d_ref[...] = delta
        mo_ref[...] = m2
        vo_ref[...] = v2

    blk = pl.BlockSpec((tr, LANE), lambda i: (i, 0))
    shp = jax.ShapeDtypeStruct((r, LANE), F32)
    return _pcall(
        body, name=name, grid=(r // tr,),
        in_specs=[pl.BlockSpec((N_DEV, tr, LANE), lambda i: (0, i, 0)), blk, blk, blk],
        out_specs=[blk, blk, blk, blk], out_shape=[shp, shp, shp, shp],
        compiler_params=_params(("parallel",)),
    )(gathered, w, m, v)


def _my_place():
    x, y, c = lax.axis_index("x"), lax.axis_index("y"), lax.axis_index("c")
    return x, y, c, 4 * x + 2 * y + c


def _peer(x, y, c, k):
    px = 1 - x if (k >> 2) & 1 else x
    py = 1 - y if (k >> 1) & 1 else y
    pc = 1 - c if k & 1 else c
    return (px, py, pc), 4 * px + 2 * py + pc


def _slab(ref, shape, kind, p):
    r, cd = shape
    if kind == "row":
        return ref.at[pl.ds(pl.multiple_of(p * r, SUBLANE), r), :]
    return ref.at[:, pl.ds(pl.multiple_of(p * cd, LANE), cd)]


def _exchange(arrays, kinds, gather, name):
    n = len(arrays)
    shard_shapes, dst_kinds, out_shapes = [], [], []
    for a, kind in zip(arrays, kinds):
        r, cd = a.shape
        if gather:
            shard, dst_kind = (r, cd), kind
        else:
            shard, dst_kind = ((r // N_DEV, cd) if kind == "row" else (r, cd // N_DEV)), "row"
        shard_shapes.append(shard)
        dst_kinds.append(dst_kind)
        full = (shard[0] * N_DEV, shard[1]) if dst_kind == "row" else (shard[0], shard[1] * N_DEV)
        out_shapes.append(jax.ShapeDtypeStruct(full, a.dtype))

    def body(*refs):
        ins, outs = refs[:n], refs[n:2 * n]
        send_sems, recv_sems, local_sems = refs[2 * n:]
        x, y, c, me = _my_place()

        def src_for(a, p):
            return ins[a] if gather else _slab(ins[a], shard_shapes[a], kinds[a], p)

        def dst_slot(a, p):
            return _slab(outs[a], shard_shapes[a], dst_kinds[a], p)

        def remote(a, k, slot):
            peer, pidx = _peer(x, y, c, k)
            sem = a * (N_DEV - 1) + k - 1
            return pltpu.make_async_remote_copy(
                src_ref=src_for(a, pidx), dst_ref=dst_slot(a, me if slot == "mine" else pidx),
                send_sem=send_sems.at[sem], recv_sem=recv_sems.at[sem], device_id=peer, device_id_type=MESH)

        local = [pltpu.make_async_copy(src_for(a, me), dst_slot(a, me), local_sems.at[a]) for a in range(n)]
        for a in range(n):
            local[a].start()
            for k in range(1, N_DEV):
                remote(a, k, "mine").start()
        for a in range(n):
            for k in range(1, N_DEV):
                arrival = remote(a, k, "theirs")
                arrival.wait_send()
                arrival.wait_recv()
            local[a].wait()

    anyspec = pl.BlockSpec(memory_space=pl.ANY)
    outs = _pcall(
        body, comm=True, name=name,
        in_specs=[anyspec] * n, out_specs=[anyspec] * n, out_shape=out_shapes,
        scratch_shapes=[pltpu.SemaphoreType.DMA((n * (N_DEV - 1),)), pltpu.SemaphoreType.DMA((n * (N_DEV - 1),)),
                        pltpu.SemaphoreType.DMA((n,))],
    )(*arrays)
    return list(outs)


def _pad_mla_w_in(w):
    d = w.shape[0]
    z = lambda n: jnp.zeros((d, n), w.dtype)
    o1, o2, o3 = Q_RANK, Q_RANK + KV_RANK, Q_RANK + KV_RANK + ROPE
    return jnp.concatenate([w[:, :o1], z(Q_RANK_PAD - Q_RANK), w[:, o1:o2], w[:, o2:o3], z(LANE - ROPE), w[:, o3:]], axis=1)


def _unpad_mla_w_in(g):
    return jnp.concatenate([g[:, :Q_RANK], g[:, PROJ_CKV:PROJ_KR], g[:, PROJ_KR:PROJ_KR + ROPE], g[:, PROJ_Z:]], axis=1)


def _pad_w_uq(w):
    w3 = w.reshape(Q_RANK, HEADS, NOPE + ROPE)
    w3 = jnp.pad(w3, ((0, Q_RANK_PAD - Q_RANK), (0, 0), (0, HEAD_PAD - NOPE - ROPE)))
    return w3.reshape(Q_RANK_PAD, HEADS * HEAD_PAD)


def _unpad_w_uq(g):
    return g[:Q_RANK].reshape(Q_RANK, HEADS, HEAD_PAD)[:, :, :NOPE + ROPE].reshape(Q_RANK, HEADS * (NOPE + ROPE))


def _pack(pieces):
    flat = [p.reshape(-1).astype(F32) for p in pieces]
    tot = sum(f.shape[0] for f in flat)
    unit = SUBLANE * LANE
    padn = (-tot) % unit
    if padn:
        flat.append(jnp.zeros((padn,), F32))
    return jnp.concatenate(flat).reshape(-1, LANE)


def _unpack(packed, shapes):
    flat = packed.reshape(-1)
    out, off = [], 0
    for shp in shapes:
        nel = int(np.prod(shp))
        out.append(flat[off:off + nel].reshape(shp))
        off += nel
    return out


WEIGHTS = ['ada_w', 'ada_b', 'pre_g', 'post_g', 'sgu_w_in', 'sgu_norm_g', 'sgu_w_s', 'sgu_b_s', 'sgu_w_out',
           'mla_w_in', 'mla_q_norm_g', 'mla_kv_norm_g', 'mla_w_uq', 'mla_w_ukv', 'mla_w_out']
INPUTS = ['x', 'c'] + WEIGHTS + ['loss_target'] + ['m_' + n for n in WEIGHTS] + ['v_' + n for n in WEIGHTS]


def kernel(x, c, ada_w, ada_b, pre_g, post_g, sgu_w_in, sgu_norm_g, sgu_w_s, sgu_b_s, sgu_w_out, mla_w_in, mla_q_norm_g, mla_kv_norm_g, mla_w_uq, mla_w_ukv, mla_w_out, loss_target, m_ada_w, m_ada_b, m_pre_g, m_post_g, m_sgu_w_in, m_sgu_norm_g, m_sgu_w_s, m_sgu_b_s, m_sgu_w_out, m_mla_w_in, m_mla_q_norm_g, m_mla_kv_norm_g, m_mla_w_uq, m_mla_w_ukv, m_mla_w_out, v_ada_w, v_ada_b, v_pre_g, v_post_g, v_sgu_w_in, v_sgu_norm_g, v_sgu_w_s, v_sgu_b_s, v_sgu_w_out, v_mla_w_in, v_mla_q_norm_g, v_mla_kv_norm_g, v_mla_w_uq, v_mla_w_ukv, v_mla_w_out):
    given = locals()
    A = {name: given[name] for name in INPUTS}
    x0 = A['x'][0]
    tgt = A['loss_target'][0]
    s, d = x0.shape
    e = 2 * d
    ncol = 3 * d // N_DEV
    _, _, _, me = _my_place()
    ktabs, qtabs = _rope_tables(s)

    gains = jnp.zeros((SUBLANE, LANE), F32)
    gains = gains.at[0:2, :Q_RANK // N_DEV].set(A['mla_q_norm_g'])
    gains = gains.at[2:4, :KV_RANK // N_DEV].set(A['mla_kv_norm_g'])
    c8 = jnp.broadcast_to(A['c'], (SUBLANE, d))
    cg, gg = _exchange([c8, gains], ["row", "row"], True, "ag_cond")
    cond_raw = cg.reshape(N_DEV, SUBLANE, d)[:, 0, :]
    gg = gg.reshape(N_DEV, SUBLANE, LANE)
    gq_full = jnp.transpose(gg[:, 0:2, :Q_RANK // N_DEV], (1, 0, 2)).reshape(N_MIX, Q_RANK)
    gkv_full = jnp.transpose(gg[:, 2:4, :KV_RANK // N_DEV], (1, 0, 2)).reshape(N_MIX, KV_RANK)
    gq_pad = jnp.pad(gq_full, ((0, 0), (0, Q_RANK_PAD - Q_RANK)))

    bias_my = lax.dynamic_slice_in_dim(A['ada_b'], me * ncol, ncol, axis=1)
    mod_part = _ada_mod(cond_raw, A['ada_w'], bias_my, "ada_mod")
    send = jnp.pad(jnp.transpose(mod_part, (1, 0, 2)), ((0, 0), (0, SUBLANE - DEPTH), (0, 0)))
    (rb,) = _exchange([send.reshape(N_DEV * SUBLANE, ncol)], ["row"], False, "a2a_mod")
    mod = jnp.transpose(rb.reshape(N_DEV, SUBLANE, ncol)[:, :DEPTH, :], (1, 0, 2)).reshape(DEPTH, 3 * d)
    shift = [mod[i:i + 1, :d] for i in range(DEPTH)]
    scale = [mod[i:i + 1, d:2 * d] for i in range(DEPTH)]
    gate = [mod[i:i + 1, 2 * d:] for i in range(DEPTH)]

    saved = []
    x = x0
    for i in range(DEPTH):
        j = i // N_MIX
        h = _pre_fwd(x, A['pre_g'][i:i + 1], scale[i], shift[i], f"pre_fwd")
        if i % N_MIX == 0:
            w_in, w_out = _exchange([A['sgu_w_in'][j].astype(BF16), A['sgu_w_out'][j].astype(BF16)],
                                    ["col", "row"], True, "ag_sgu")
            uvz = _mm(h, w_in, "nn", F32, "sgu_in")
            bias_full = jnp.repeat(A['sgu_b_s'][j].T, e // SGU_GROUPS, axis=1)
            ng = A['sgu_norm_g'][j:j + 1]
            y = _sgu_mid_fwd(uvz, ng, A['sgu_w_s'][j], bias_full, "sgu_mid_fwd")
            out = _mm(y, w_out, "nn", F32, "sgu_out")
            saved.append(dict(x=x, h=h, uvz=uvz, y=y, out=out, w_in=w_in, w_out=w_out, bias=bias_full, ng=ng))
        else:
            w_in, w_uq, w_ukv, w_out = _exchange(
                [A['mla_w_in'][j].astype(BF16), A['mla_w_uq'][j].astype(BF16), A['mla_w_ukv'][j].astype(BF16),
                 A['mla_w_out'][j].astype(BF16)], ["col", "col", "col", "row"], True, "ag_mla")
            w_in = _pad_mla_w_in(w_in)
            w_uq = _pad_w_uq(w_uq)
            gq, gkv = gq_pad[j:j + 1], gkv_full[j:j + 1]
            proj = _mm(h, w_in, "nn", F32, "mla_in")
            qn, kvn, kr = _mla_norm_fwd(proj, gq, gkv, ktabs, "mla_norm_fwd")
            q = _mm(qn, w_uq, "nn", F32, "mla_uq")
            q_cat = _rope_heads(q, qtabs, 1.0, BF16, "rope_q_fwd")
            kv = _mm(kvn, w_ukv, "nn", BF16, "mla_ukv")
            o, y, lse = _attn_fwd(q_cat, kv, kr, proj, "attn_fwd")
            out = _mm(y, w_out, "nn", F32, "mla_out")
            saved.append(dict(x=x, h=h, proj=proj, qn=qn, kvn=kvn, kr=kr, q_cat=q_cat, kv=kv, o=o, y=y, lse=lse,
                              out=out, w_in=w_in, w_uq=w_uq, w_ukv=w_ukv, w_out=w_out, gq=gq, gkv=gkv))
        x = _post_fwd(x, out, gate[i], A['post_g'][i:i + 1], "post_fwd")

    dx, loss_row = _loss_head(x, tgt, "loss_head")
    loss = lax.psum(loss_row[0, 0], ("x", "y", "c"))

    d_shift, d_scale, d_gate = [None] * DEPTH, [None] * DEPTH, [None] * DEPTH
    d_pre, d_post = [None] * DEPTH, [None] * DEPTH
    recv = {}
    small = {}
    for i in reversed(range(DEPTH)):
        j = i // N_MIX
        sv = saved[i]
        dout, d_gate[i], d_post[i] = _post_bwd(dx, sv['out'], gate[i], A['post_g'][i:i + 1], "post_bwd")
        if i % N_MIX == 0:
            dy = _mm(dout, sv['w_out'], "nt", F32, "sgu_out_dx")
            g_w_out = _mm(sv['y'], dout, "tn", BF16, "sgu_out_dw")
            duvz, dws, dbs, dng = _sgu_mid_bwd(sv['uvz'], dy, sv['ng'], A['sgu_w_s'][j], sv['bias'], "sgu_mid_bwd")
            dh = _mm(duvz, sv['w_in'], "nt", F32, "sgu_in_dx")
            g_w_in = _mm(sv['h'], duvz, "tn", BF16, "sgu_in_dw")
            recv[('sgu', j)] = _exchange([g_w_in, g_w_out], ["col", "row"], False, "rs_sgu")
            small[('sgu', j)] = (dws, dbs.reshape(SGU_GROUPS, SGU_BLOCK), dng)
        else:
            dy = _mm(dout, sv['w_out'], "nt", F32, "mla_out_dx")
            g_w_out = _mm(sv['y'], dout, "tn", BF16, "mla_out_dw")
            do, dz = _mla_gate_bwd(dy, sv['o'], sv['proj'], "mla_gate_bwd")
            dq, dkv, dkr_heads = _attn_bwd(sv['q_cat'], sv['kv'], sv['kr'], do, sv['o'], sv['lse'], "attn_bwd")
            dq_b = _rope_heads(dq, qtabs, -1.0, BF16, "rope_q_bwd")
            dqn = _mm(dq_b, sv['w_uq'], "nt", F32, "mla_uq_dx")
            g_w_uq = _unpad_w_uq(_mm(sv['qn'], dq_b, "tn", BF16, "mla_uq_dw"))
            dkvn = _mm(dkv, sv['w_ukv'], "nt", F32, "mla_ukv_dx")
            g_w_ukv = _mm(sv['kvn'], dkv, "tn", BF16, "mla_ukv_dw")
            dpa, dgq, dgkv = _mla_norm_bwd(dqn, dkvn, dkr_heads, sv['proj'], sv['gq'], sv['gkv'], ktabs, "mla_norm_bwd")
            dproj = jnp.concatenate([dpa, dz], axis=1)
            dh = _mm(dproj, sv['w_in'], "nt", F32, "mla_in_dx")
            g_w_in = _unpad_mla_w_in(_mm(sv['h'], dproj, "tn", BF16, "mla_in_dw"))
            recv[('mla', j)] = _exchange([g_w_in, g_w_uq, g_w_ukv, g_w_out], ["col", "col", "col", "row"], False, "rs_mla")
            small[('mla', j)] = (dgq[:, :Q_RANK], dgkv)
        dx, d_shift[i], d_scale[i], d_pre[i] = _pre_bwd(dh, sv['x'], dx, A['pre_g'][i:i + 1], scale[i], "pre_bwd")

    res = {}

    def big(name, recv0, recv1):
        res[name] = _adam_reduce(recv0, recv1, A[name], A['m_' + name], A['v_' + name], "adam_" + name)

    for idx_w, name in enumerate(['sgu_w_in', 'sgu_w_out']):
        big(name, recv[('sgu', 0)][idx_w], recv[('sgu', 1)][idx_w])
    for idx_w, name in enumerate(['mla_w_in', 'mla_w_uq', 'mla_w_ukv', 'mla_w_out']):
        big(name, recv[('mla', 0)][idx_w], recv[('mla', 1)][idx_w])

    dmod = jnp.concatenate([jnp.concatenate([d_shift[i], d_scale[i], d_gate[i]], axis=1) for i in range(DEPTH)], axis=0)
    pieces = [dmod, jnp.concatenate(d_pre, axis=0), jnp.concatenate(d_post, axis=0),
              jnp.concatenate([small[('sgu', j)][2] for j in range(N_MIX)], axis=0),
              jnp.stack([small[('sgu', j)][0] for j in range(N_MIX)]),
              jnp.stack([small[('sgu', j)][1] for j in range(N_MIX)]),
              jnp.concatenate([small[('mla', j)][0] for j in range(N_MIX)], axis=0),
              jnp.concatenate([small[('mla', j)][1] for j in range(N_MIX)], axis=0)]
    small_names = ['ada_b', 'pre_g', 'post_g', 'sgu_norm_g', 'sgu_w_s', 'sgu_b_s', 'mla_q_norm_g', 'mla_kv_norm_g']
    full_shapes = [p.shape for p in pieces]
    packed = _pack(pieces)
    rows = packed.shape[0]
    (gathered,) = _exchange([packed], ["row"], True, "ag_small")
    gathered = gathered.reshape(N_DEV, rows, LANE)

    def full_size(prefix):
        out = []
        for name, shp in zip(small_names, full_shapes):
            t = A[prefix + name]
            if t.shape != tuple(shp):
                t = lax.dynamic_update_slice_in_dim(jnp.zeros(shp, F32), t, me * t.shape[1], axis=1)
            out.append(t)
        return _pack(out)

    sm = _adam_small(gathered, full_size(''), full_size('m_'), full_size('v_'), "adam_small")
    sm = [_unpack(t, full_shapes) for t in sm]
    for k_out in range(4):
        for name, val in zip(small_names, sm[k_out]):
            t = A[name]
            if t.shape != val.shape:
                val = lax.dynamic_slice_in_dim(val, me * t.shape[1], t.shape[1], axis=1)
            res.setdefault(name, [None] * 4)[k_out] = val

    nrow_dmod = DEPTH * 3 * d // LANE
    dmod_all = gathered[:, :nrow_dmod, :].reshape(N_DEV, DEPTH, 3 * d)
    dmod_cols = jnp.transpose(lax.dynamic_slice_in_dim(dmod_all, me * ncol, ncol, axis=2), (1, 0, 2))
    res['ada_w'] = _ada_bwd_adam(jnp.transpose(cond_raw), dmod_cols, A['ada_w'], A['m_ada_w'], A['v_ada_w'], "ada_bwd")

    outs = [loss, dx[None]]
    for k_out in range(4):
        outs += [res[n][k_out] for n in WEIGHTS]
    return tuple(outs)
```

```python
import functools
import math

import numpy as np
import jax
import jax.numpy as jnp
from jax import lax
from jax.experimental import pallas as pl
from jax.experimental.pallas import tpu as pltpu

F32 = jnp.float32
BF16 = jnp.bfloat16
MESH = pl.DeviceIdType.MESH

N_DEV = 8
DEPTH = 4
N_MIX = 2
NORM_EPS = 1e-6
CHUNK = 64
SGU_BLOCK = 128
SGU_GROUPS = 16
HEADS = 16
Q_RANK = 448
Q_RANK_PAD = 512
KV_RANK = 512
NOPE = 128
ROPE = 64
HALF = ROPE // 2
V_DIM = 128
HEAD_PAD = 256
ROPE_THETA = 10000.0
MLA_WIDTH = HEADS * V_DIM
LANE = 128
SUBLANE = 8
PROJ_CQ = 0
PROJ_CKV = Q_RANK_PAD
PROJ_KR = Q_RANK_PAD + KV_RANK
PROJ_Z = PROJ_KR + LANE
PROJ_W = PROJ_Z + MLA_WIDTH

ADAM_LR = 0.001
ADAM_B1 = 0.9
ADAM_B2 = 0.999
ADAM_EPS = 1e-08
ADAM_WD = 0.01
ADAM_STEP = 10

VMEM_LIMIT = 56 * 1024 * 1024
ATT_BLK = 512
ROW_BLK = 256
MM_TM, MM_TN, MM_TK = 1024, 1024, 2048
MM_TILE_BYTES = 40 * 1024 * 1024
INV_SQRT2 = 1.0 / math.sqrt(2.0)
INV_SQRT_2PI = 1.0 / math.sqrt(2.0 * math.pi)


def _pcall(body, comm=False, **kw):
    return pl.pallas_call(body, **kw)


def _params(sem=None):
    return pltpu.CompilerParams(dimension_semantics=sem, vmem_limit_bytes=VMEM_LIMIT)


def _pick(dim, pref):
    if dim <= pref:
        return dim
    t = (pref // LANE) * LANE
    while t >= LANE:
        if dim % t == 0:
            return t
        t -= LANE
    return dim


def _gelu(x):
    return 0.5 * x * (1.0 + lax.erf(x * INV_SQRT2))


def _gelu_grad(x):
    return 0.5 * (1.0 + lax.erf(x * INV_SQRT2)) + x * jnp.exp(-0.5 * x * x) * INV_SQRT_2PI


def _sigmoid(x):
    return 1.0 / (1.0 + jnp.exp(-x))


def _dot_nt(a, b):
    return lax.dot_general(a, b, (((1,), (1,)), ((), ())), preferred_element_type=F32)


def _dot_tn(a, b):
    return lax.dot_general(a, b, (((0,), (0,)), ((), ())), preferred_element_type=F32)


def _mm(a, b, dims, out_dtype, name):
    if dims == "nn":
        (m, k), (k2, n) = a.shape, b.shape
    elif dims == "nt":
        (m, k), (n, k2) = a.shape, b.shape
    else:
        (k, m), (k2, n) = a.shape, b.shape
    assert k == k2, (a.shape, b.shape, dims)
    tm, tn = _pick(m, MM_TM), _pick(n, MM_TN)
    out_bytes = 2 * tm * tn * jnp.dtype(out_dtype).itemsize
    whole_k = 2 * (tm + tn) * k * a.dtype.itemsize + out_bytes <= MM_TILE_BYTES
    tk = k if whole_k else _pick(k, MM_TK)
    nk = k // tk

    def body(a_ref, b_ref, o_ref, *scratch):
        if dims == "nn":
            p = jnp.dot(a_ref[...], b_ref[...], preferred_element_type=F32)
        elif dims == "nt":
            p = _dot_nt(a_ref[...], b_ref[...])
        else:
            p = _dot_tn(a_ref[...], b_ref[...])
        if nk == 1:
            o_ref[...] = p.astype(o_ref.dtype)
            return
        acc_ref, = scratch
        kk = pl.program_id(2)

        @pl.when(kk == 0)
        def _():
            acc_ref[...] = p

        @pl.when(kk > 0)
        def _():
            acc_ref[...] += p

        @pl.when(kk == nk - 1)
        def _():
            o_ref[...] = acc_ref[...].astype(o_ref.dtype)

    if dims == "tn":
        a_spec = pl.BlockSpec((tk, tm), lambda i, j, kk: (kk, i))
    else:
        a_spec = pl.BlockSpec((tm, tk), lambda i, j, kk: (i, kk))
    if dims == "nt":
        b_spec = pl.BlockSpec((tn, tk), lambda i, j, kk: (j, kk))
    else:
        b_spec = pl.BlockSpec((tk, tn), lambda i, j, kk: (kk, j))
    return _pcall(
        body, name=name,
        grid=(m // tm, n // tn, nk),
        in_specs=[a_spec, b_spec],
        out_specs=pl.BlockSpec((tm, tn), lambda i, j, kk: (i, j)),
        out_shape=jax.ShapeDtypeStruct((m, n), out_dtype),
        scratch_shapes=[pltpu.VMEM((tm, tn), F32)] if nk > 1 else [],
        compiler_params=_params(("parallel", "parallel", "arbitrary")),
    )(a, b)


def _row_spec(ts, d):
    return pl.BlockSpec((ts, d), lambda i: (i, 0))


def _vec_spec(d):
    return pl.BlockSpec((1, d), lambda i: (0, 0))


def _pre_fwd(x, g, scale, shift, name):
    s, d = x.shape
    ts = _pick(s, ROW_BLK)

    def body(x_ref, g_ref, sc_ref, sh_ref, h_ref):
        xv = x_ref[...]
        r = lax.rsqrt(jnp.mean(xv * xv, axis=-1, keepdims=True) + NORM_EPS)
        h_ref[...] = ((xv * r * g_ref[...]) * (1.0 + sc_ref[...]) + sh_ref[...]).astype(BF16)

    return _pcall(
        body, name=name, grid=(s // ts,),
        in_specs=[_row_spec(ts, d), _vec_spec(d), _vec_spec(d), _vec_spec(d)],
        out_specs=_row_spec(ts, d),
        out_shape=jax.ShapeDtypeStruct((s, d), BF16),
        compiler_params=_params(("parallel",)),
    )(x, g, scale, shift)


def _post_fwd(x, out, gate, g, name):
    s, d = x.shape
    ts = _pick(s, ROW_BLK)

    def body(x_ref, o_ref, gate_ref, g_ref, y_ref):
        o = o_ref[...]
        r = lax.rsqrt(jnp.mean(o * o, axis=-1, keepdims=True) + NORM_EPS)
        y_ref[...] = x_ref[...] + gate_ref[...] * (o * r * g_ref[...])

    return _pcall(
        body, name=name, grid=(s // ts,),
        in_specs=[_row_spec(ts, d), _row_spec(ts, d), _vec_spec(d), _vec_spec(d)],
        out_specs=_row_spec(ts, d),
        out_shape=jax.ShapeDtypeStruct((s, d), F32),
        compiler_params=_params(("parallel",)),
    )(x, out, gate, g)


def _loss_head(xf, tgt, name):
    s, d = xf.shape
    ts = _pick(s, ROW_BLK)
    ns = s // ts

    def body(x_ref, t_ref, dx_ref, loss_ref, acc_ref):
        i = pl.program_id(0)

        @pl.when(i == 0)
        def _():
            acc_ref[...] = jnp.zeros_like(acc_ref)

        e = x_ref[...] - t_ref[...]
        dx_ref[...] = e * (1.0 / d)
        acc_ref[...] += jnp.sum(e * e, axis=0, keepdims=True)

        @pl.when(i == ns - 1)
        def _():
            tot = jnp.sum(acc_ref[...], axis=1, keepdims=True) * (0.5 / d)
            loss_ref[...] = jnp.broadcast_to(tot, loss_ref.shape)

    return _pcall(
        body, name=name, grid=(ns,),
        in_specs=[_row_spec(ts, d), _row_spec(ts, d)],
        out_specs=[_row_spec(ts, d), pl.BlockSpec((1, LANE), lambda i: (0, 0))],
        out_shape=[jax.ShapeDtypeStruct((s, d), F32), jax.ShapeDtypeStruct((1, LANE), F32)],
        scratch_shapes=[pltpu.VMEM((1, d), F32)],
        compiler_params=_params(("arbitrary",)),
    )(xf, tgt)


def _post_bwd(dxo, out, gate, g, name):
    s, d = dxo.shape
    ts = _pick(s, ROW_BLK)

    def body(dx_ref, o_ref, gate_ref, g_ref, do_ref, dgate_ref, dg_ref):
        i = pl.program_id(0)

        @pl.when(i == 0)
        def _():
            dgate_ref[...] = jnp.zeros_like(dgate_ref)
            dg_ref[...] = jnp.zeros_like(dg_ref)

        o = o_ref[...]
        dx = dx_ref[...]
        gv = g_ref[...]
        r = lax.rsqrt(jnp.mean(o * o, axis=-1, keepdims=True) + NORM_EPS)
        n = o * r
        dyn = dx * gate_ref[...]
        dgate_ref[...] += jnp.sum(dx * (n * gv), axis=0, keepdims=True)
        dg_ref[...] += jnp.sum(dyn * n, axis=0, keepdims=True)
        dn = dyn * gv
        do_ref[...] = (r * (dn - n * jnp.mean(dn * n, axis=-1, keepdims=True))).astype(BF16)

    return _pcall(
        body, name=name, grid=(s // ts,),
        in_specs=[_row_spec(ts, d), _row_spec(ts, d), _vec_spec(d), _vec_spec(d)],
        out_specs=[_row_spec(ts, d), _vec_spec(d), _vec_spec(d)],
        out_shape=[jax.ShapeDtypeStruct((s, d), BF16), jax.ShapeDtypeStruct((1, d), F32),
                   jax.ShapeDtypeStruct((1, d), F32)],
        compiler_params=_params(("arbitrary",)),
    )(dxo, out, gate, g)


def _pre_bwd(dh, x, dxo, g, scale, name):
    s, d = x.shape
    ts = _pick(s, ROW_BLK)

    def body(dh_ref, x_ref, dxo_ref, g_ref, sc_ref, dx_ref, dsh_ref, dsc_ref, dg_ref):
        i = pl.program_id(0)

        @pl.when(i == 0)
        def _():
            dsh_ref[...] = jnp.zeros_like(dsh_ref)
            dsc_ref[...] = jnp.zeros_like(dsc_ref)
            dg_ref[...] = jnp.zeros_like(dg_ref)

        xv = x_ref[...]
        dhv = dh_ref[...]
        gv = g_ref[...]
        one_sc = 1.0 + sc_ref[...]
        r = lax.rsqrt(jnp.mean(xv * xv, axis=-1, keepdims=True) + NORM_EPS)
        n = xv * r
        dsh_ref[...] += jnp.sum(dhv, axis=0, keepdims=True)
        dsc_ref[...] += jnp.sum(dhv * (n * gv), axis=0, keepdims=True)
        dng = dhv * one_sc
        dg_ref[...] += jnp.sum(dng * n, axis=0, keepdims=True)
        dn = dng * gv
        dx_ref[...] = dxo_ref[...] + r * (dn - n * jnp.mean(dn * n, axis=-1, keepdims=True))

    return _pcall(
        body, name=name, grid=(s // ts,),
        in_specs=[_row_spec(ts, d), _row_spec(ts, d), _row_spec(ts, d), _vec_spec(d), _vec_spec(d)],
        out_specs=[_row_spec(ts, d), _vec_spec(d), _vec_spec(d), _vec_spec(d)],
        out_shape=[jax.ShapeDtypeStruct((s, d), F32)] + [jax.ShapeDtypeStruct((1, d), F32)] * 3,
        compiler_params=_params(("arbitrary",)),
    )(dh, x, dxo, g, scale)


def _sgu_mask():
    t = lax.broadcasted_iota(jnp.int32, (SGU_BLOCK, SGU_BLOCK), 0) // CHUNK
    s = lax.broadcasted_iota(jnp.int32, (SGU_BLOCK, SGU_BLOCK), 1) // CHUNK
    return s <= t


def _sgu_norm(v_pre, g):
    e = v_pre.shape[-1]
    vg = _gelu(v_pre)
    mu = jnp.sum(vg, axis=-1, keepdims=True) * (1.0 / e)
    dlt = vg - mu
    var = jnp.sum(dlt * dlt, axis=-1, keepdims=True) * (1.0 / e)
    rstd = lax.rsqrt(var + NORM_EPS)
    vhat = dlt * rstd
    return vhat, rstd, (vhat * g).astype(BF16)


def _sgu_mid_fwd(uvz, norm_g, w_s, bias_full, name):
    s, e3 = uvz.shape
    e = e3 // 3
    gd = e // SGU_GROUPS
    nb = s // SGU_BLOCK

    def body(uvz_ref, g_ref, w_ref, b_ref, y_ref, wsc):
        @pl.when(pl.program_id(0) == 0)
        def _():
            msk = _sgu_mask()
            for gi in range(SGU_GROUPS):
                wsc[gi] = jnp.where(msk, w_ref[gi], 0.0).astype(BF16)

        _, _, vb = _sgu_norm(uvz_ref[:, e:2 * e], g_ref[...])
        for gi in range(SGU_GROUPS):
            lo = gi * gd
            vm = jnp.dot(wsc[gi], vb[:, lo:lo + gd], preferred_element_type=F32) + b_ref[:, lo:lo + gd]
            zg = uvz_ref[:, 2 * e + lo:2 * e + lo + gd]
            y_ref[:, lo:lo + gd] = (_gelu(uvz_ref[:, lo:lo + gd]) * vm * (zg * _sigmoid(zg))).astype(BF16)

    return _pcall(
        body, name=name, grid=(nb,),
        in_specs=[pl.BlockSpec((SGU_BLOCK, e3), lambda n: (n, 0)),
                  pl.BlockSpec((1, e), lambda n: (0, 0)),
                  pl.BlockSpec((SGU_GROUPS, SGU_BLOCK, SGU_BLOCK), lambda n: (0, 0, 0)),
                  pl.BlockSpec((SGU_BLOCK, e), lambda n: (0, 0))],
        out_specs=pl.BlockSpec((SGU_BLOCK, e), lambda n: (n, 0)),
        out_shape=jax.ShapeDtypeStruct((s, e), BF16),
        scratch_shapes=[pltpu.VMEM((SGU_GROUPS, SGU_BLOCK, SGU_BLOCK), BF16)],
        compiler_params=_params(("arbitrary",)),
    )(uvz, norm_g, w_s, bias_full)


def _sgu_mid_bwd(uvz, dy, norm_g, w_s, bias_full, name):
    s, e3 = uvz.shape
    e = e3 // 3
    gd = e // SGU_GROUPS
    nb = s // SGU_BLOCK

    def body(uvz_ref, dy_ref, g_ref, w_ref, b_ref, d_ref, dw_ref, db_ref, dg_ref, wsc, wtsc, dvh_sc, dbacc):
        n = pl.program_id(0)

        @pl.when(n == 0)
        def _():
            msk = _sgu_mask()
            for gi in range(SGU_GROUPS):
                wm = jnp.where(msk, w_ref[gi], 0.0)
                wsc[gi] = wm.astype(BF16)
                wtsc[gi] = wm.T.astype(BF16)
            dw_ref[...] = jnp.zeros_like(dw_ref)
            dg_ref[...] = jnp.zeros_like(dg_ref)
            dbacc[...] = jnp.zeros_like(dbacc)

        v_pre = uvz_ref[:, e:2 * e]
        gv = g_ref[...]
        vhat, rstd, vb = _sgu_norm(v_pre, gv)
        s1 = jnp.zeros((SGU_BLOCK, 1), F32)
        s2 = jnp.zeros((SGU_BLOCK, 1), F32)
        for gi in range(SGU_GROUPS):
            lo = gi * gd
            u_pre = uvz_ref[:, lo:lo + gd]
            zg = uvz_ref[:, 2 * e + lo:2 * e + lo + gd]
            dyg = dy_ref[:, lo:lo + gd]
            ug = _gelu(u_pre)
            sig = _sigmoid(zg)
            vbg = vb[:, lo:lo + gd]
            vhg = vhat[:, lo:lo + gd]
            vm = jnp.dot(wsc[gi], vbg, preferred_element_type=F32) + b_ref[:, lo:lo + gd]
            t = dyg * (zg * sig)
            d_ref[:, lo:lo + gd] = (t * vm * _gelu_grad(u_pre)).astype(BF16)
            dvm = t * ug
            d_ref[:, 2 * e + lo:2 * e + lo + gd] = (dyg * ug * vm * (sig * (1.0 + zg * (1.0 - sig)))).astype(BF16)
            dvm_b = dvm.astype(BF16)
            dv = jnp.dot(wtsc[gi], dvm_b, preferred_element_type=F32)
            dw_ref[gi] += _dot_nt(dvm_b, vbg)
            dbacc[:, lo:lo + gd] += dvm
            dg_ref[:, lo:lo + gd] += jnp.sum(dv * vhg, axis=0, keepdims=True)
            dvh = dv * gv[:, lo:lo + gd]
            dvh_sc[:, lo:lo + gd] = dvh
            s1 = s1 + jnp.sum(dvh, axis=-1, keepdims=True)
            s2 = s2 + jnp.sum(dvh * vhg, axis=-1, keepdims=True)
        dvg = rstd * (dvh_sc[...] - s1 * (1.0 / e) - vhat * (s2 * (1.0 / e)))
        d_ref[:, e:2 * e] = (dvg * _gelu_grad(v_pre)).astype(BF16)

        @pl.when(n == nb - 1)
        def _():
            msk = _sgu_mask()
            for gi in range(SGU_GROUPS):
                dw_ref[gi] = jnp.where(msk, dw_ref[gi], 0.0)
                db_ref[gi] = jnp.sum(dbacc[:, gi * gd:(gi + 1) * gd], axis=1, keepdims=True)

    return _pcall(
        body, name=name, grid=(nb,),
        in_specs=[pl.BlockSpec((SGU_BLOCK, e3), lambda n: (n, 0)),
                  pl.BlockSpec((SGU_BLOCK, e), lambda n: (n, 0)),
                  pl.BlockSpec((1, e), lambda n: (0, 0)),
                  pl.BlockSpec((SGU_GROUPS, SGU_BLOCK, SGU_BLOCK), lambda n: (0, 0, 0)),
                  pl.BlockSpec((SGU_BLOCK, e), lambda n: (0, 0))],
        out_specs=[pl.BlockSpec((SGU_BLOCK, e3), lambda n: (n, 0)),
                   pl.BlockSpec((SGU_GROUPS, SGU_BLOCK, SGU_BLOCK), lambda n: (0, 0, 0)),
                   pl.BlockSpec((SGU_GROUPS, SGU_BLOCK, 1), lambda n: (0, 0, 0)),
                   pl.BlockSpec((1, e), lambda n: (0, 0))],
        out_shape=[jax.ShapeDtypeStruct((s, e3), BF16),
                   jax.ShapeDtypeStruct((SGU_GROUPS, SGU_BLOCK, SGU_BLOCK), F32),
                   jax.ShapeDtypeStruct((SGU_GROUPS, SGU_BLOCK, 1), F32),
                   jax.ShapeDtypeStruct((1, e), F32)],
        scratch_shapes=[pltpu.VMEM((SGU_GROUPS, SGU_BLOCK, SGU_BLOCK), BF16),
                        pltpu.VMEM((SGU_GROUPS, SGU_BLOCK, SGU_BLOCK), BF16),
                        pltpu.VMEM((SGU_BLOCK, e), F32),
                        pltpu.VMEM((SGU_BLOCK, e), F32)],
        compiler_params=_params(("arbitrary",)),
    )(uvz, dy, norm_g, w_s, bias_full)


def _rope_tables(s):
    pos = jnp.arange(s, dtype=F32)
    inv_freq = ROPE_THETA ** (-jnp.arange(0, ROPE, 2, dtype=F32) / ROPE)
    ang = pos[:, None] * inv_freq[None, :]
    cos, sin = jnp.cos(ang), jnp.sin(ang)
    z32 = jnp.zeros((s, HALF), F32)
    z64 = jnp.zeros((s, ROPE), F32)
    ck = jnp.concatenate([cos, cos, z64], axis=1)
    s1k = jnp.concatenate([-sin, z32, z64], axis=1)
    s2k = jnp.concatenate([z32, sin, z64], axis=1)
    one = jnp.ones((s, NOPE), F32)
    zn = jnp.zeros((s, NOPE), F32)
    return (ck, s1k, s2k), (jnp.concatenate([one, ck], axis=1), jnp.concatenate([zn, s1k], axis=1),
                            jnp.concatenate([zn, s2k], axis=1))


def _rot(x, c, s1, s2):
    w = x.shape[-1]
    return x * c + pltpu.roll(x, w - HALF, 1) * s1 + pltpu.roll(x, HALF, 1) * s2


def _rms(cv, n_real):
    r = lax.rsqrt(jnp.sum(cv * cv, axis=-1, keepdims=True) * (1.0 / n_real) + NORM_EPS)
    return r, cv * r


def _mla_norm_fwd(proj, gq, gkv, tabs, name):
    s = proj.shape[0]
    ts = _pick(s, ROW_BLK)
    ck, s1k, s2k = tabs

    def body(cq_ref, ckv_ref, kr_ref, gq_ref, gkv_ref, c_ref, s1_ref, s2_ref, qn_ref, kvn_ref, kro_ref):
        _, nq = _rms(cq_ref[...], Q_RANK)
        qn_ref[...] = (nq * gq_ref[...]).astype(BF16)
        _, nkv = _rms(ckv_ref[...], KV_RANK)
        kvn_ref[...] = (nkv * gkv_ref[...]).astype(BF16)
        kro_ref[...] = _rot(kr_ref[...], c_ref[...], s1_ref[...], s2_ref[...]).astype(BF16)

    tab = pl.BlockSpec((ts, LANE), lambda i: (i, 0))
    return _pcall(
        body, name=name, grid=(s // ts,),
        in_specs=[pl.BlockSpec((ts, Q_RANK_PAD), lambda i: (i, 0)),
                  pl.BlockSpec((ts, KV_RANK), lambda i: (i, PROJ_CKV // KV_RANK)),
                  pl.BlockSpec((ts, LANE), lambda i: (i, PROJ_KR // LANE)),
                  _vec_spec(Q_RANK_PAD), _vec_spec(KV_RANK), tab, tab, tab],
        out_specs=[pl.BlockSpec((ts, Q_RANK_PAD), lambda i: (i, 0)),
                   pl.BlockSpec((ts, KV_RANK), lambda i: (i, 0)), tab],
        out_shape=[jax.ShapeDtypeStruct((s, Q_RANK_PAD), BF16), jax.ShapeDtypeStruct((s, KV_RANK), BF16),
                   jax.ShapeDtypeStruct((s, LANE), BF16)],
        compiler_params=_params(("parallel",)),
    )(proj, proj, proj, gq, gkv, ck, s1k, s2k)


def _rope_heads(q, tabs, sign, out_dtype, name):
    s, w = q.shape
    ts = _pick(s, ROW_BLK)
    c, s1, s2 = tabs

    def body(q_ref, c_ref, s1_ref, s2_ref, o_ref):
        cv, s1v, s2v = c_ref[...], sign * s1_ref[...], sign * s2_ref[...]
        for h in range(w // HEAD_PAD):
            lo = h * HEAD_PAD
            o_ref[:, lo:lo + HEAD_PAD] = _rot(q_ref[:, lo:lo + HEAD_PAD].astype(F32), cv, s1v, s2v).astype(out_dtype)

    blk = pl.BlockSpec((ts, w), lambda i: (i, 0))
    tab = pl.BlockSpec((ts, HEAD_PAD), lambda i: (i, 0))
    return _pcall(
        body, name=name, grid=(s // ts,),
        in_specs=[blk, tab, tab, tab], out_specs=blk,
        out_shape=jax.ShapeDtypeStruct((s, w), out_dtype),
        compiler_params=_params(("parallel",)),
    )(q, c, s1, s2)


def _att_mask(qi, ki, tq, tk):
    qc = (qi * tq + lax.broadcasted_iota(jnp.int32, (tq, tk), 0)) // CHUNK
    kc = (ki * tk + lax.broadcasted_iota(jnp.int32, (tq, tk), 1)) // CHUNK
    return kc <= qc


def _attn_fwd(q_cat, kv, kr, proj, name):
    s = q_cat.shape[0]
    tb = _pick(s, ATT_BLK)
    nb = s // tb
    scale = (NOPE + ROPE) ** -0.5
    zcol = PROJ_Z // V_DIM

    def body(q_ref, kn_ref, v_ref, kr_ref, z_ref, o_ref, y_ref, lse_ref, m_sc, l_sc, acc_sc):
        qi, ki = pl.program_id(1), pl.program_id(2)

        @pl.when(ki == 0)
        def _():
            m_sc[...] = jnp.full_like(m_sc, -1e30)
            l_sc[...] = jnp.zeros_like(l_sc)
            acc_sc[...] = jnp.zeros_like(acc_sc)

        @pl.when(ki <= qi)
        def _():
            k = jnp.concatenate([kn_ref[...], kr_ref[...]], axis=1)
            sc = _dot_nt(q_ref[...], k) * scale
            sc = jnp.where(_att_mask(qi, ki, tb, tb), sc, -1e30)
            m_prev = m_sc[...]
            m_new = jnp.maximum(m_prev, jnp.max(sc, axis=-1, keepdims=True))
            alpha = jnp.exp(m_prev - m_new)
            p = jnp.exp(sc - m_new)
            l_sc[...] = alpha * l_sc[...] + jnp.sum(p, axis=-1, keepdims=True)
            acc_sc[...] = alpha * acc_sc[...] + jnp.dot(p.astype(BF16), v_ref[...], preferred_element_type=F32)
            m_sc[...] = m_new

        @pl.when(ki == nb - 1)
        def _():
            l = l_sc[...]
            o = acc_sc[...] / l
            z = z_ref[...]
            o_ref[...] = o.astype(BF16)
            y_ref[...] = (o * (z * _sigmoid(z))).astype(BF16)
            lse_ref[...] = m_sc[...] + jnp.log(l)

    kmap = lambda off: (lambda h, qi, ki: (jnp.minimum(ki, qi), 2 * h + off))
    oblk = pl.BlockSpec((tb, V_DIM), lambda h, qi, ki: (qi, h))
    return _pcall(
        body, name=name, grid=(HEADS, nb, nb),
        in_specs=[pl.BlockSpec((tb, HEAD_PAD), lambda h, qi, ki: (qi, h)),
                  pl.BlockSpec((tb, NOPE), kmap(0)),
                  pl.BlockSpec((tb, V_DIM), kmap(1)),
                  pl.BlockSpec((tb, LANE), lambda h, qi, ki: (jnp.minimum(ki, qi), 0)),
                  pl.BlockSpec((tb, V_DIM), lambda h, qi, ki: (qi, zcol + h))],
        out_specs=[oblk, oblk, pl.BlockSpec((None, tb, 1), lambda h, qi, ki: (h, qi, 0))],
        out_shape=[jax.ShapeDtypeStruct((s, MLA_WIDTH), BF16), jax.ShapeDtypeStruct((s, MLA_WIDTH), BF16),
                   jax.ShapeDtypeStruct((HEADS, s, 1), F32)],
        scratch_shapes=[pltpu.VMEM((tb, 1), F32), pltpu.VMEM((tb, 1), F32), pltpu.VMEM((tb, V_DIM), F32)],
        compiler_params=_params(("parallel", "parallel", "arbitrary")),
    )(q_cat, kv, kv, kr, proj)


def _attn_bwd(q_cat, kv, kr, do, o, lse, name):
    s = q_cat.shape[0]
    tb = _pick(s, ATT_BLK)
    nb = s // tb
    scale = (NOPE + ROPE) ** -0.5

    def body(q_ref, kn_ref, v_ref, kr_ref, do_ref, o_ref, lse_ref, dq_ref, dkv_ref, dkr_ref, dk_sc, dv_sc):
        ki, qi = pl.program_id(1), pl.program_id(2)

        @pl.when(jnp.logical_and(ki == 0, qi == 0))
        def _():
            dq_ref[...] = jnp.zeros_like(dq_ref)

        @pl.when(qi == 0)
        def _():
            dk_sc[...] = jnp.zeros_like(dk_sc)
            dv_sc[...] = jnp.zeros_like(dv_sc)

        @pl.when(qi >= ki)
        def _():
            q = q_ref[...]
            k = jnp.concatenate([kn_ref[...], kr_ref[...]], axis=1)
            dov = do_ref[...]
            sc = _dot_nt(q, k) * scale
            sc = jnp.where(_att_mask(qi, ki, tb, tb), sc, -1e30)
            p = jnp.exp(sc - lse_ref[...])
            dv_sc[...] += _dot_tn(p.astype(BF16), dov)
            dp = _dot_nt(dov, v_ref[...])
            delta = jnp.sum(dov.astype(F32) * o_ref[...].astype(F32), axis=-1, keepdims=True)
            ds = (p * (dp - delta) * scale).astype(BF16)
            dk_sc[...] += _dot_tn(ds, q)
            rows = pl.ds(pl.multiple_of(qi * tb, tb), tb)
            dq_ref[rows, :] += jnp.dot(ds, k, preferred_element_type=F32)

        @pl.when(qi == nb - 1)
        def _():
            dkv_ref[:, :NOPE] = dk_sc[:, :NOPE].astype(BF16)
            dkv_ref[:, NOPE:] = dv_sc[...].astype(BF16)
            dkr_ref[...] = dk_sc[:, NOPE:]

    qmap = lambda h, ki, qi: (jnp.maximum(qi, ki), h)
    return _pcall(
        body, name=name, grid=(HEADS, nb, nb),
        in_specs=[pl.BlockSpec((tb, HEAD_PAD), qmap),
                  pl.BlockSpec((tb, NOPE), lambda h, ki, qi: (ki, 2 * h)),
                  pl.BlockSpec((tb, V_DIM), lambda h, ki, qi: (ki, 2 * h + 1)),
                  pl.BlockSpec((tb, LANE), lambda h, ki, qi: (ki, 0)),
                  pl.BlockSpec((tb, V_DIM), qmap),
                  pl.BlockSpec((tb, V_DIM), qmap),
                  pl.BlockSpec((None, tb, 1), lambda h, ki, qi: (h, jnp.maximum(qi, ki), 0))],
        out_specs=[pl.BlockSpec((s, HEAD_PAD), lambda h, ki, qi: (0, h)),
                   pl.BlockSpec((tb, HEAD_PAD), lambda h, ki, qi: (ki, h)),
                   pl.BlockSpec((None, tb, LANE), lambda h, ki, qi: (h, ki, 0))],
        out_shape=[jax.ShapeDtypeStruct((s, HEADS * HEAD_PAD), F32),
                   jax.ShapeDtypeStruct((s, HEADS * HEAD_PAD), BF16),
                   jax.ShapeDtypeStruct((HEADS, s, LANE), F32)],
        scratch_shapes=[pltpu.VMEM((tb, HEAD_PAD), F32), pltpu.VMEM((tb, V_DIM), F32)],
        compiler_params=_params(("arbitrary", "arbitrary", "arbitrary")),
    )(q_cat, kv, kv, kr, do, o, lse)


def _mla_gate_bwd(dy, o, proj, name):
    s = dy.shape[0]
    ts = _pick(s, ROW_BLK)
    zcol = PROJ_Z // V_DIM

    def body(dy_ref, o_ref, p_ref, do_ref, dz_ref):
        z = p_ref[:, PROJ_Z:]
        dyv = dy_ref[...]
        sig = _sigmoid(z)
        do_ref[...] = (dyv * (z * sig)).astype(BF16)
        dz_ref[...] = (dyv * o_ref[...].astype(F32) * (sig * (1.0 + z * (1.0 - sig)))).astype(BF16)

    blk = pl.BlockSpec((ts, MLA_WIDTH), lambda i: (i, 0))
    return _pcall(
        body, name=name, grid=(s // ts,),
        in_specs=[blk, blk, pl.BlockSpec((ts, PROJ_W), lambda i: (i, 0))],
        out_specs=[blk, blk],
        out_shape=[jax.ShapeDtypeStruct((s, MLA_WIDTH), BF16), jax.ShapeDtypeStruct((s, MLA_WIDTH), BF16)],
        compiler_params=_params(("parallel",)),
    )(dy, o, proj)


def _mla_norm_bwd(dqn, dkvn, dkr_heads, proj, gq, gkv, tabs, name):
    s = proj.shape[0]
    ts = _pick(s, ROW_BLK)
    ck, s1k, s2k = tabs

    def rms_bwd(cv, dn_in, g, n_real):
        r, n = _rms(cv, n_real)
        dg = jnp.sum(dn_in * n, axis=0, keepdims=True)
        dn = dn_in * g
        dc = r * (dn - n * (jnp.sum(dn * n, axis=-1, keepdims=True) * (1.0 / n_real)))
        return dc, dg

    def body(dqn_ref, dkvn_ref, dkr_ref, cq_ref, ckv_ref, gq_ref, gkv_ref, c_ref, s1_ref, s2_ref,
             dp_ref, dgq_ref, dgkv_ref):
        @pl.when(pl.program_id(0) == 0)
        def _():
            dgq_ref[...] = jnp.zeros_like(dgq_ref)
            dgkv_ref[...] = jnp.zeros_like(dgkv_ref)

        dcq, dgq = rms_bwd(cq_ref[...], dqn_ref[...], gq_ref[...], Q_RANK)
        dckv, dgkv = rms_bwd(ckv_ref[...], dkvn_ref[...], gkv_ref[...], KV_RANK)
        dgq_ref[...] += dgq
        dgkv_ref[...] += dgkv
        dkr = dkr_ref[0]
        for h in range(1, HEADS):
            dkr = dkr + dkr_ref[h]
        dp_ref[:, PROJ_CQ:PROJ_CKV] = dcq.astype(BF16)
        dp_ref[:, PROJ_CKV:PROJ_KR] = dckv.astype(BF16)
        dp_ref[:, PROJ_KR:PROJ_Z] = _rot(dkr, c_ref[...], -s1_ref[...], -s2_ref[...]).astype(BF16)

    tab = pl.BlockSpec((ts, LANE), lambda i: (i, 0))
    return _pcall(
        body, name=name, grid=(s // ts,),
        in_specs=[pl.BlockSpec((ts, Q_RANK_PAD), lambda i: (i, 0)),
                  pl.BlockSpec((ts, KV_RANK), lambda i: (i, 0)),
                  pl.BlockSpec((HEADS, ts, LANE), lambda i: (0, i, 0)),
                  pl.BlockSpec((ts, Q_RANK_PAD), lambda i: (i, 0)),
                  pl.BlockSpec((ts, KV_RANK), lambda i: (i, PROJ_CKV // KV_RANK)),
                  _vec_spec(Q_RANK_PAD), _vec_spec(KV_RANK), tab, tab, tab],
        out_specs=[pl.BlockSpec((ts, PROJ_Z), lambda i: (i, 0)), _vec_spec(Q_RANK_PAD), _vec_spec(KV_RANK)],
        out_shape=[jax.ShapeDtypeStruct((s, PROJ_Z), BF16), jax.ShapeDtypeStruct((1, Q_RANK_PAD), F32),
                   jax.ShapeDtypeStruct((1, KV_RANK), F32)],
        compiler_params=_params(("arbitrary",)),
    )(dqn, dkvn, dkr_heads, proj, proj, gq, gkv, ck, s1k, s2k)


def _ada_mod(cond_raw, ada_w, bias_my, name):
    nl, d, ncol = ada_w.shape
    tk = _pick(d, 512)
    nk = d // tk

    def body(c_ref, w_ref, b_ref, o_ref, acc_ref):
        kk = pl.program_id(1)

        @pl.when(kk == 0)
        def _():
            acc_ref[...] = jnp.zeros_like(acc_ref)

        cv = c_ref[...]
        cond = (cv * _sigmoid(cv)).astype(BF16)
        acc_ref[...] += jnp.dot(cond, w_ref[...].astype(BF16), preferred_element_type=F32)

        @pl.when(kk == nk - 1)
        def _():
            o_ref[...] = acc_ref[...] + b_ref[...]

    return _pcall(
        body, name=name, grid=(nl, nk),
        in_specs=[pl.BlockSpec((N_DEV, tk), lambda l, kk: (0, kk)),
                  pl.BlockSpec((None, tk, ncol), lambda l, kk: (l, kk, 0)),
                  pl.BlockSpec((None, 1, ncol), lambda l, kk: (l, 0, 0))],
        out_specs=pl.BlockSpec((None, N_DEV, ncol), lambda l, kk: (l, 0, 0)),
        out_shape=jax.ShapeDtypeStruct((nl, N_DEV, ncol), F32),
        scratch_shapes=[pltpu.VMEM((N_DEV, ncol), F32)],
        compiler_params=_params(("parallel", "arbitrary")),
    )(cond_raw, ada_w, bias_my.reshape(nl, 1, ncol))


def _adam(w, g, m, v):
    m = ADAM_B1 * m + (1.0 - ADAM_B1) * g
    v = ADAM_B2 * v + (1.0 - ADAM_B2) * (g * g)
    m_hat = m / (1.0 - ADAM_B1 ** ADAM_STEP)
    v_hat = v / (1.0 - ADAM_B2 ** ADAM_STEP)
    delta = -ADAM_LR * (m_hat / (jnp.sqrt(v_hat) + ADAM_EPS) + ADAM_WD * w)
    return delta, m, v


def _ada_bwd_adam(cond_t, dmod_cols, w, m, v, name):
    nl, d, ncol = w.shape
    tk = _pick(d, 512)

    def body(c_ref, dm_ref, w_ref, m_ref, v_ref, g_ref, d_ref, mo_ref, vo_ref):
        cv = c_ref[...]
        cond = (cv * _sigmoid(cv)).astype(BF16)
        g = jnp.dot(cond, dm_ref[...].astype(BF16), preferred_element_type=F32)
        delta, m2, v2 = _adam(w_ref[...], g, m_ref[...], v_ref[...])
        g_ref[...] = g
        d_ref[...] = delta
        mo_ref[...] = m2
        vo_ref[...] = v2

    blk = pl.BlockSpec((None, tk, ncol), lambda l, kk: (l, kk, 0))
    shp = jax.ShapeDtypeStruct((nl, d, ncol), F32)
    return _pcall(
        body, name=name, grid=(nl, d // tk),
        in_specs=[pl.BlockSpec((tk, N_DEV), lambda l, kk: (kk, 0)),
                  pl.BlockSpec((None, N_DEV, ncol), lambda l, kk: (l, 0, 0)), blk, blk, blk],
        out_specs=[blk, blk, blk, blk], out_shape=[shp, shp, shp, shp],
        compiler_params=_params(("parallel", "parallel")),
    )(cond_t, dmod_cols, w, m, v)


def _adam_reduce(recv0, recv1, w, m, v, name):
    nl, r, c = w.shape
    tr = _pick(r, 128) if r % 128 == 0 else r
    tc = _pick(c, 1024)

    def body(r0_ref, r1_ref, w_ref, m_ref, v_ref, g_ref, d_ref, mo_ref, vo_ref):
        l = pl.program_id(0)

        def run(rr):
            g = rr[0].astype(F32)
            for sidx in range(1, N_DEV):
                g = g + rr[sidx].astype(F32)
            delta, m2, v2 = _adam(w_ref[...], g, m_ref[...], v_ref[...])
            g_ref[...] = g
            d_ref[...] = delta
            mo_ref[...] = m2
            vo_ref[...] = v2

        @pl.when(l == 0)
        def _():
            run(r0_ref)

        @pl.when(l == 1)
        def _():
            run(r1_ref)

    rblk = pl.BlockSpec((N_DEV, tr, tc), lambda l, i, j: (0, i, j))
    blk = pl.BlockSpec((None, tr, tc), lambda l, i, j: (l, i, j))
    shp = jax.ShapeDtypeStruct((nl, r, c), F32)
    return _pcall(
        body, name=name, grid=(nl, r // tr, c // tc),
        in_specs=[rblk, rblk, blk, blk, blk],
        out_specs=[blk, blk, blk, blk], out_shape=[shp, shp, shp, shp],
        compiler_params=_params(("arbitrary", "parallel", "parallel")),
    )(recv0.reshape(N_DEV, r, c), recv1.reshape(N_DEV, r, c), w, m, v)


def _adam_small(gathered, w, m, v, name):
    r = w.shape[0]
    tr = _pick(r, 512) if r % 512 == 0 else r

    def body(p_ref, w_ref, m_ref, v_ref, g_ref, d_ref, mo_ref, vo_ref):
        g = p_ref[0]
        for sidx in range(1, N_DEV):
            g = g + p_ref[sidx]
        delta, m2, v2 = _adam(w_ref[...], g, m_ref[...], v_ref[...])
        g_ref[...] = g
        d_ref[...] = delta
        mo_ref[...] = m2
        vo_ref[...] = v2

    blk = pl.BlockSpec((tr, LANE), lambda i: (i, 0))
    shp = jax.ShapeDtypeStruct((r, LANE), F32)
    return _pcall(
        body, name=name, grid=(r // tr,),
        in_specs=[pl.BlockSpec((N_DEV, tr, LANE), lambda i: (0, i, 0)), blk, blk, blk],
        out_specs=[blk, blk, blk, blk], out_shape=[shp, shp, shp, shp],
        compiler_params=_params(("parallel",)),
    )(gathered, w, m, v)


def _my_place():
    x, y, c = lax.axis_index("x"), lax.axis_index("y"), lax.axis_index("c")
    return x, y, c, 4 * x + 2 * y + c


def _peer(x, y, c, k):
    px = 1 - x if (k >> 2) & 1 else x
    py = 1 - y if (k >> 1) & 1 else y
    pc = 1 - c if k & 1 else c
    return (px, py, pc), 4 * px + 2 * py + pc


def _slab(ref, shape, kind, p):
    r, cd = shape
    if kind == "row":
        return ref.at[pl.ds(pl.multiple_of(p * r, SUBLANE), r), :]
    return ref.at[:, pl.ds(pl.multiple_of(p * cd, LANE), cd)]


def _exchange_layout(arrays, kinds, gather):
    shard_shapes, dst_kinds, out_shapes = [], [], []
    for a, kind in zip(arrays, kinds):
        r, cd = a.shape
        if gather:
            shard, dst_kind = (r, cd), kind
        else:
            shard, dst_kind = ((r // N_DEV, cd) if kind == "row" else (r, cd // N_DEV)), "row"
        shard_shapes.append(shard)
        dst_kinds.append(dst_kind)
        full = (shard[0] * N_DEV, shard[1]) if dst_kind == "row" else (shard[0], shard[1] * N_DEV)
        out_shapes.append(jax.ShapeDtypeStruct(full, a.dtype))
    return shard_shapes, dst_kinds, out_shapes


def _exchange_copies(ins, outs, send_sems, recv_sems, sem_of, layout, kinds, gather):
    shard_shapes, dst_kinds, _ = layout
    x, y, c, me = _my_place()

    def src_for(a, p):
        return ins[a] if gather else _slab(ins[a], shard_shapes[a], kinds[a], p)

    def dst_slot(a, p):
        return _slab(outs[a], shard_shapes[a], dst_kinds[a], p)

    def local(a, sem):
        return pltpu.make_async_copy(src_for(a, me), dst_slot(a, me), sem)

    def remote(a, k, slot):
        peer, pidx = _peer(x, y, c, k)
        return pltpu.make_async_remote_copy(
            src_ref=src_for(a, pidx), dst_ref=dst_slot(a, me if slot == "mine" else pidx),
            send_sem=send_sems.at[sem_of(a, k)], recv_sem=recv_sems.at[sem_of(a, k)],
            device_id=peer, device_id_type=MESH)

    return local, remote


def _exchange(arrays, kinds, gather, name):
    n = len(arrays)
    layout = _exchange_layout(arrays, kinds, gather)

    def body(*refs):
        ins, outs = refs[:n], refs[n:2 * n]
        send_sems, recv_sems, local_sems = refs[2 * n:]
        local, remote = _exchange_copies(ins, outs, send_sems, recv_sems,
                                         lambda a, k: a * (N_DEV - 1) + k - 1, layout, kinds, gather)
        for a in range(n):
            local(a, local_sems.at[a]).start()
            for k in range(1, N_DEV):
                remote(a, k, "mine").start()
        for a in range(n):
            for k in range(1, N_DEV):
                arrival = remote(a, k, "theirs")
                arrival.wait_send()
                arrival.wait_recv()
            local(a, local_sems.at[a]).wait()

    anyspec = pl.BlockSpec(memory_space=pl.ANY)
    outs = _pcall(
        body, comm=True, name=name,
        in_specs=[anyspec] * n, out_specs=[anyspec] * n, out_shape=layout[2],
        scratch_shapes=[pltpu.SemaphoreType.DMA((n * (N_DEV - 1),)), pltpu.SemaphoreType.DMA((n * (N_DEV - 1),)),
                        pltpu.SemaphoreType.DMA((n,))],
    )(*arrays)
    return list(outs)


HBM_SPEC = pl.BlockSpec(memory_space=pltpu.HBM)
SEM_SPEC = pl.BlockSpec(memory_space=pltpu.SEMAPHORE)
ANY_SPEC = pl.BlockSpec(memory_space=pl.ANY)
DATAFLOW = pltpu.SideEffectType.DATAFLOW_SIDE_EFFECTING


def _exchange_start(arrays, kinds, gather, name, after, carry=()):
    n, nc = len(arrays), len(carry)
    layout = _exchange_layout(arrays, kinds, gather)
    lands = [lax.empty(s.shape, s.dtype) for s in layout[2]]

    def body(*refs):
        ins, outs = refs[:n], refs[n:2 * n]
        send_sems, recv_sems = refs[2 * n + nc + 1], refs[2 * n + nc + 2]
        token = refs[2 * n + nc + 3 + 2 * n + nc]
        local_sems = refs[-1]
        local, remote = _exchange_copies(ins, outs, send_sems, recv_sems, lambda a, k: a, layout, kinds, gather)
        for a in range(n):
            local(a, local_sems.at[a]).start()
            for k in range(1, N_DEV):
                remote(a, k, "mine").start()
        for a in range(n):
            local(a, local_sems.at[a]).wait()
        token[...] = jnp.zeros_like(token)

    passed = list(arrays) + lands + list(carry)
    res = pl.pallas_call(
        body, name=name,
        out_shape=(pltpu.SemaphoreType.DMA((n,)), pltpu.SemaphoreType.DMA((n,)),
                   *[pltpu.HBM(t.shape, t.dtype) for t in passed], jax.ShapeDtypeStruct((SUBLANE, LANE), F32)),
        in_specs=[HBM_SPEC] * (2 * n + nc) + [ANY_SPEC],
        out_specs=(SEM_SPEC, SEM_SPEC, *([HBM_SPEC] * (2 * n + nc)), pl.BlockSpec(memory_space=pltpu.VMEM)),
        input_output_aliases={i: 2 + i for i in range(2 * n + nc)},
        scratch_shapes=[pltpu.SemaphoreType.DMA((n,))],
        compiler_params=pltpu.CompilerParams(has_side_effects=DATAFLOW),
    )(*[pltpu.with_memory_space_constraint(t, pltpu.HBM) for t in passed], after)
    handle = (res[0], res[1], list(res[2:2 + n]), list(res[2 + n:2 + 2 * n]), tuple(kinds), gather)
    return handle, res[-1], list(res[2 + 2 * n:2 + 2 * n + nc])


def _exchange_wait(handle, name, after):
    send_sems, recv_sems, ins_thru, lands_thru, kinds, gather = handle
    n = len(ins_thru)
    layout = _exchange_layout(ins_thru, kinds, gather)

    def body(*refs):
        ins, outs = refs[:n], refs[n:2 * n]
        s_sems, r_sems = refs[2 * n], refs[2 * n + 1]
        _, remote = _exchange_copies(ins, outs, s_sems, r_sems, lambda a, k: a, layout, kinds, gather)
        for a in range(n):
            for k in range(1, N_DEV):
                arrival = remote(a, k, "theirs")
                arrival.wait_send()
                arrival.wait_recv()

    res = pl.pallas_call(
        body, name=name,
        out_shape=[pltpu.HBM(t.shape, t.dtype) for t in ins_thru + lands_thru],
        in_specs=[HBM_SPEC] * (2 * n) + [SEM_SPEC, SEM_SPEC, ANY_SPEC],
        out_specs=[HBM_SPEC] * (2 * n),
        input_output_aliases={i: i for i in range(2 * n)},
        compiler_params=pltpu.CompilerParams(has_side_effects=DATAFLOW),
    )(*ins_thru, *lands_thru, send_sems, recv_sems, after)
    return list(res[n:2 * n])


def _pad_mla_w_in(w):
    d = w.shape[0]
    z = lambda n: jnp.zeros((d, n), w.dtype)
    o1, o2, o3 = Q_RANK, Q_RANK + KV_RANK, Q_RANK + KV_RANK + ROPE
    return jnp.concatenate([w[:, :o1], z(Q_RANK_PAD - Q_RANK), w[:, o1:o2], w[:, o2:o3], z(LANE - ROPE), w[:, o3:]], axis=1)


def _unpad_mla_w_in(g):
    return jnp.concatenate([g[:, :Q_RANK], g[:, PROJ_CKV:PROJ_KR], g[:, PROJ_KR:PROJ_KR + ROPE], g[:, PROJ_Z:]], axis=1)


def _pad_w_uq(w):
    w3 = w.reshape(Q_RANK, HEADS, NOPE + ROPE)
    w3 = jnp.pad(w3, ((0, Q_RANK_PAD - Q_RANK), (0, 0), (0, HEAD_PAD - NOPE - ROPE)))
    return w3.reshape(Q_RANK_PAD, HEADS * HEAD_PAD)


def _unpad_w_uq(g):
    return g[:Q_RANK].reshape(Q_RANK, HEADS, HEAD_PAD)[:, :, :NOPE + ROPE].reshape(Q_RANK, HEADS * (NOPE + ROPE))


def _pack(pieces):
    flat = [p.reshape(-1).astype(F32) for p in pieces]
    tot = sum(f.shape[0] for f in flat)
    unit = SUBLANE * LANE
    padn = (-tot) % unit
    if padn:
        flat.append(jnp.zeros((padn,), F32))
    return jnp.concatenate(flat).reshape(-1, LANE)


def _unpack(packed, shapes):
    flat = packed.reshape(-1)
    out, off = [], 0
    for shp in shapes:
        nel = int(np.prod(shp))
        out.append(flat[off:off + nel].reshape(shp))
        off += nel
    return out


WEIGHTS = ['ada_w', 'ada_b', 'pre_g', 'post_g', 'sgu_w_in', 'sgu_norm_g', 'sgu_w_s', 'sgu_b_s', 'sgu_w_out',
           'mla_w_in', 'mla_q_norm_g', 'mla_kv_norm_g', 'mla_w_uq', 'mla_w_ukv', 'mla_w_out']
INPUTS = ['x', 'c'] + WEIGHTS + ['loss_target'] + ['m_' + n for n in WEIGHTS] + ['v_' + n for n in WEIGHTS]


def kernel(x, c, ada_w, ada_b, pre_g, post_g, sgu_w_in, sgu_norm_g, sgu_w_s, sgu_b_s, sgu_w_out, mla_w_in, mla_q_norm_g, mla_kv_norm_g, mla_w_uq, mla_w_ukv, mla_w_out, loss_target, m_ada_w, m_ada_b, m_pre_g, m_post_g, m_sgu_w_in, m_sgu_norm_g, m_sgu_w_s, m_sgu_b_s, m_sgu_w_out, m_mla_w_in, m_mla_q_norm_g, m_mla_kv_norm_g, m_mla_w_uq, m_mla_w_ukv, m_mla_w_out, v_ada_w, v_ada_b, v_pre_g, v_post_g, v_sgu_w_in, v_sgu_norm_g, v_sgu_w_s, v_sgu_b_s, v_sgu_w_out, v_mla_w_in, v_mla_q_norm_g, v_mla_kv_norm_g, v_mla_w_uq, v_mla_w_ukv, v_mla_w_out):
    given = locals()
    A = {name: given[name] for name in INPUTS}
    x0 = A['x'][0]
    tgt = A['loss_target'][0]
    s, d = x0.shape
    e = 2 * d
    ncol = 3 * d // N_DEV
    _, _, _, me = _my_place()
    ktabs, qtabs = _rope_tables(s)

    token = jnp.zeros((SUBLANE, LANE), F32)
    gathers = []
    for i in range(DEPTH):
        j = i // N_MIX
        if i % N_MIX == 0:
            shards, kinds = [A['sgu_w_in'][j], A['sgu_w_out'][j]], ["col", "row"]
        else:
            shards = [A['mla_w_in'][j], A['mla_w_uq'][j], A['mla_w_ukv'][j], A['mla_w_out'][j]]
            kinds = ["col", "col", "col", "row"]
        handle, token, _ = _exchange_start([t.astype(BF16) for t in shards], kinds, True, f"ag_start_{i}", token)
        gathers.append(handle)

    gains = jnp.zeros((SUBLANE, LANE), F32)
    gains = gains.at[0:2, :Q_RANK // N_DEV].set(A['mla_q_norm_g'])
    gains = gains.at[2:4, :KV_RANK // N_DEV].set(A['mla_kv_norm_g'])
    c8 = jnp.broadcast_to(A['c'], (SUBLANE, d)) + token[0, 0]
    cg, gg = _exchange([c8, gains], ["row", "row"], True, "ag_cond")
    cond_raw = cg.reshape(N_DEV, SUBLANE, d)[:, 0, :]
    gg = gg.reshape(N_DEV, SUBLANE, LANE)
    gq_full = jnp.transpose(gg[:, 0:2, :Q_RANK // N_DEV], (1, 0, 2)).reshape(N_MIX, Q_RANK)
    gkv_full = jnp.transpose(gg[:, 2:4, :KV_RANK // N_DEV], (1, 0, 2)).reshape(N_MIX, KV_RANK)
    gq_pad = jnp.pad(gq_full, ((0, 0), (0, Q_RANK_PAD - Q_RANK)))

    bias_my = lax.dynamic_slice_in_dim(A['ada_b'], me * ncol, ncol, axis=1)
    mod_part = _ada_mod(cond_raw, A['ada_w'], bias_my, "ada_mod")
    send = jnp.pad(jnp.transpose(mod_part, (1, 0, 2)), ((0, 0), (0, SUBLANE - DEPTH), (0, 0)))
    (rb,) = _exchange([send.reshape(N_DEV * SUBLANE, ncol)], ["row"], False, "a2a_mod")
    mod = jnp.transpose(rb.reshape(N_DEV, SUBLANE, ncol)[:, :DEPTH, :], (1, 0, 2)).reshape(DEPTH, 3 * d)
    shift = [mod[i:i + 1, :d] for i in range(DEPTH)]
    scale = [mod[i:i + 1, d:2 * d] for i in range(DEPTH)]
    gate = [mod[i:i + 1, 2 * d:] for i in range(DEPTH)]

    saved = []
    x = x0
    for i in range(DEPTH):
        j = i // N_MIX
        h = _pre_fwd(x, A['pre_g'][i:i + 1], scale[i], shift[i], f"pre_fwd")
        if i % N_MIX == 0:
            w_in, w_out = _exchange_wait(gathers[i], f"ag_wait_{i}", h)
            uvz = _mm(h, w_in, "nn", F32, "sgu_in")
            bias_full = jnp.repeat(A['sgu_b_s'][j].T, e // SGU_GROUPS, axis=1)
            ng = A['sgu_norm_g'][j:j + 1]
            y = _sgu_mid_fwd(uvz, ng, A['sgu_w_s'][j], bias_full, "sgu_mid_fwd")
            out = _mm(y, w_out, "nn", F32, "sgu_out")
            saved.append(dict(x=x, h=h, uvz=uvz, y=y, out=out, w_in=w_in, w_out=w_out, bias=bias_full, ng=ng))
        else:
            w_in, w_uq, w_ukv, w_out = _exchange_wait(gathers[i], f"ag_wait_{i}", h)
            w_in = _pad_mla_w_in(w_in)
            w_uq = _pad_w_uq(w_uq)
            gq, gkv = gq_pad[j:j + 1], gkv_full[j:j + 1]
            proj = _mm(h, w_in, "nn", F32, "mla_in")
            qn, kvn, kr = _mla_norm_fwd(proj, gq, gkv, ktabs, "mla_norm_fwd")
            q = _mm(qn, w_uq, "nn", F32, "mla_uq")
            q_cat = _rope_heads(q, qtabs, 1.0, BF16, "rope_q_fwd")
            kv = _mm(kvn, w_ukv, "nn", BF16, "mla_ukv")
            o, y, lse = _attn_fwd(q_cat, kv, kr, proj, "attn_fwd")
            out = _mm(y, w_out, "nn", F32, "mla_out")
            saved.append(dict(x=x, h=h, proj=proj, qn=qn, kvn=kvn, kr=kr, q_cat=q_cat, kv=kv, o=o, y=y, lse=lse,
                              out=out, w_in=w_in, w_uq=w_uq, w_ukv=w_ukv, w_out=w_out, gq=gq, gkv=gkv))
        x = _post_fwd(x, out, gate[i], A['post_g'][i:i + 1], "post_fwd")

    dx, loss_row = _loss_head(x, tgt, "loss_head")
    loss = lax.psum(loss_row[0, 0], ("x", "y", "c"))

    d_shift, d_scale, d_gate = [None] * DEPTH, [None] * DEPTH, [None] * DEPTH
    d_pre, d_post = [None] * DEPTH, [None] * DEPTH
    scatters = [None] * DEPTH
    small = {}
    for i in reversed(range(DEPTH)):
        j = i // N_MIX
        sv = saved[i]
        dout, d_gate[i], d_post[i] = _post_bwd(dx, sv['out'], gate[i], A['post_g'][i:i + 1], "post_bwd")
        if i % N_MIX == 0:
            dy = _mm(dout, sv['w_out'], "nt", F32, "sgu_out_dx")
            g_w_out = _mm(sv['y'], dout, "tn", BF16, "sgu_out_dw")
            duvz, dws, dbs, dng = _sgu_mid_bwd(sv['uvz'], dy, sv['ng'], A['sgu_w_s'][j], sv['bias'], "sgu_mid_bwd")
            g_w_in = _mm(sv['h'], duvz, "tn", BF16, "sgu_in_dw")
            scatters[i], token, (duvz,) = _exchange_start([g_w_in, g_w_out], ["col", "row"], False, f"rs_start_{i}",
                                                          token, carry=[duvz])
            dh = _mm(duvz, sv['w_in'], "nt", F32, "sgu_in_dx")
            small[('sgu', j)] = (dws, dbs.reshape(SGU_GROUPS, SGU_BLOCK), dng)
        else:
            dy = _mm(dout, sv['w_out'], "nt", F32, "mla_out_dx")
            g_w_out = _mm(sv['y'], dout, "tn", BF16, "mla_out_dw")
            do, dz = _mla_gate_bwd(dy, sv['o'], sv['proj'], "mla_gate_bwd")
            dq, dkv, dkr_heads = _attn_bwd(sv['q_cat'], sv['kv'], sv['kr'], do, sv['o'], sv['lse'], "attn_bwd")
            dq_b = _rope_heads(dq, qtabs, -1.0, BF16, "rope_q_bwd")
            dqn = _mm(dq_b, sv['w_uq'], "nt", F32, "mla_uq_dx")
            g_w_uq = _unpad_w_uq(_mm(sv['qn'], dq_b, "tn", BF16, "mla_uq_dw"))
            dkvn = _mm(dkv, sv['w_ukv'], "nt", F32, "mla_ukv_dx")
            g_w_ukv = _mm(sv['kvn'], dkv, "tn", BF16, "mla_ukv_dw")
            dpa, dgq, dgkv = _mla_norm_bwd(dqn, dkvn, dkr_heads, sv['proj'], sv['gq'], sv['gkv'], ktabs, "mla_norm_bwd")
            dproj = jnp.concatenate([dpa, dz], axis=1)
            g_w_in = _unpad_mla_w_in(_mm(sv['h'], dproj, "tn", BF16, "mla_in_dw"))
            scatters[i], token, (dproj,) = _exchange_start(
                [g_w_in, g_w_uq, g_w_ukv, g_w_out], ["col", "col", "col", "row"], False, f"rs_start_{i}", token,
                carry=[dproj])
            dh = _mm(dproj, sv['w_in'], "nt", F32, "mla_in_dx")
            small[('mla', j)] = (dgq[:, :Q_RANK], dgkv)
        dx, d_shift[i], d_scale[i], d_pre[i] = _pre_bwd(dh, sv['x'], dx, A['pre_g'][i:i + 1], scale[i], "pre_bwd")

    res = {}

    def big(name, recv0, recv1):
        res[name] = _adam_reduce(recv0, recv1, A[name], A['m_' + name], A['v_' + name], "adam_" + name)

    recv_mla = {j: _exchange_wait(scatters[N_MIX * j + 1], f"rs_wait_{N_MIX * j + 1}", dx) for j in reversed(range(N_MIX))}
    for idx_w, name in enumerate(['mla_w_in', 'mla_w_uq', 'mla_w_ukv', 'mla_w_out']):
        big(name, recv_mla[0][idx_w], recv_mla[1][idx_w])

    dmod = jnp.concatenate([jnp.concatenate([d_shift[i], d_scale[i], d_gate[i]], axis=1) for i in range(DEPTH)], axis=0)
    pieces = [dmod, jnp.concatenate(d_pre, axis=0), jnp.concatenate(d_post, axis=0),
              jnp.concatenate([small[('sgu', j)][2] for j in range(N_MIX)], axis=0),
              jnp.stack([small[('sgu', j)][0] for j in range(N_MIX)]),
              jnp.stack([small[('sgu', j)][1] for j in range(N_MIX)]),
              jnp.concatenate([small[('mla', j)][0] for j in range(N_MIX)], axis=0),
              jnp.concatenate([small[('mla', j)][1] for j in range(N_MIX)], axis=0)]
    small_names = ['ada_b', 'pre_g', 'post_g', 'sgu_norm_g', 'sgu_w_s', 'sgu_b_s', 'mla_q_norm_g', 'mla_kv_norm_g']
    full_shapes = [p.shape for p in pieces]
    packed = _pack(pieces)
    rows = packed.shape[0]
    (gathered,) = _exchange([packed], ["row"], True, "ag_small")
    gathered = gathered.reshape(N_DEV, rows, LANE)

    def full_size(prefix):
        out = []
        for name, shp in zip(small_names, full_shapes):
            t = A[prefix + name]
            if t.shape != tuple(shp):
                t = lax.dynamic_update_slice_in_dim(jnp.zeros(shp, F32), t, me * t.shape[1], axis=1)
            out.append(t)
        return _pack(out)

    sm = _adam_small(gathered, full_size(''), full_size('m_'), full_size('v_'), "adam_small")
    sm = [_unpack(t, full_shapes) for t in sm]
    for k_out in range(4):
        for name, val in zip(small_names, sm[k_out]):
            t = A[name]
            if t.shape != val.shape:
                val = lax.dynamic_slice_in_dim(val, me * t.shape[1], t.shape[1], axis=1)
            res.setdefault(name, [None] * 4)[k_out] = val

    nrow_dmod = DEPTH * 3 * d // LANE
    dmod_all = gathered[:, :nrow_dmod, :].reshape(N_DEV, DEPTH, 3 * d)
    dmod_cols = jnp.transpose(lax.dynamic_slice_in_dim(dmod_all, me * ncol, ncol, axis=2), (1, 0, 2))
    res['ada_w'] = _ada_bwd_adam(jnp.transpose(cond_raw), dmod_cols, A['ada_w'], A['m_ada_w'], A['v_ada_w'], "ada_bwd")

    recv_sgu = {j: _exchange_wait(scatters[N_MIX * j], f"rs_wait_{N_MIX * j}", res['ada_w'][0]) for j in reversed(range(N_MIX))}
    for idx_w, name in enumerate(['sgu_w_in', 'sgu_w_out']):
        big(name, recv_sgu[0][idx_w], recv_sgu[1][idx_w])

    outs = [loss, dx[None]]
    for k_out in range(4):
        outs += [res[n][k_out] for n in WEIGHTS]
    return tuple(outs)
```

```python
import functools
import math

import numpy as np
import jax
import jax.numpy as jnp
from jax import lax
from jax.experimental import pallas as pl
from jax.experimental.pallas import tpu as pltpu

F32 = jnp.float32
BF16 = jnp.bfloat16
MESH = pl.DeviceIdType.MESH

N_DEV = 8
DEPTH = 4
N_MIX = 2
NORM_EPS = 1e-6
CHUNK = 64
SGU_BLOCK = 128
SGU_GROUPS = 16
HEADS = 16
Q_RANK = 448
Q_RANK_PAD = 512
KV_RANK = 512
NOPE = 128
ROPE = 64
HALF = ROPE // 2
V_DIM = 128
HEAD_PAD = 256
ROPE_THETA = 10000.0
MLA_WIDTH = HEADS * V_DIM
LANE = 128
SUBLANE = 8
PROJ_CQ = 0
PROJ_CKV = Q_RANK_PAD
PROJ_KR = Q_RANK_PAD + KV_RANK
PROJ_Z = PROJ_KR + LANE
PROJ_W = PROJ_Z + MLA_WIDTH

ADAM_LR = 0.001
ADAM_B1 = 0.9
ADAM_B2 = 0.999
ADAM_EPS = 1e-08
ADAM_WD = 0.01
ADAM_STEP = 10

VMEM_LIMIT = 56 * 1024 * 1024
ATT_BLK = 512
ATT_SUB = 128
ROW_BLK = 256
MM_TM, MM_TN, MM_TK = 1024, 1024, 2048
MM_TILE_BYTES = 40 * 1024 * 1024
SOFTMAX_SCALE = (NOPE + ROPE) ** -0.5
LOG2_E = 1.0 / math.log(2.0)
INV_SQRT2 = 1.0 / math.sqrt(2.0)
INV_SQRT_2PI = 1.0 / math.sqrt(2.0 * math.pi)


def _pcall(body, comm=False, **kw):
    return pl.pallas_call(body, **kw)


def _params(sem=None):
    return pltpu.CompilerParams(dimension_semantics=sem, vmem_limit_bytes=VMEM_LIMIT)


def _pick(dim, pref):
    if dim <= pref:
        return dim
    t = (pref // LANE) * LANE
    while t >= LANE:
        if dim % t == 0:
            return t
        t -= LANE
    return dim


def _gelu(x):
    return 0.5 * x * (1.0 + lax.erf(x * INV_SQRT2))


def _gelu_grad(x):
    return 0.5 * (1.0 + lax.erf(x * INV_SQRT2)) + x * jnp.exp(-0.5 * x * x) * INV_SQRT_2PI


def _sigmoid(x):
    return 1.0 / (1.0 + jnp.exp(-x))


def _dot_nt(a, b):
    return lax.dot_general(a, b, (((1,), (1,)), ((), ())), preferred_element_type=F32)


def _dot_tn(a, b):
    return lax.dot_general(a, b, (((0,), (0,)), ((), ())), preferred_element_type=F32)


def _mm(a, b, dims, out_dtype, name):
    if dims == "nn":
        (m, k), (k2, n) = a.shape, b.shape
    elif dims == "nt":
        (m, k), (n, k2) = a.shape, b.shape
    else:
        (k, m), (k2, n) = a.shape, b.shape
    assert k == k2, (a.shape, b.shape, dims)
    tm, tn = _pick(m, MM_TM), _pick(n, MM_TN)
    out_bytes = 2 * tm * tn * jnp.dtype(out_dtype).itemsize
    whole_k = 2 * (tm + tn) * k * a.dtype.itemsize + out_bytes <= MM_TILE_BYTES
    tk = k if whole_k else _pick(k, MM_TK)
    nk = k // tk

    def body(a_ref, b_ref, o_ref, *scratch):
        if dims == "nn":
            p = jnp.dot(a_ref[...], b_ref[...], preferred_element_type=F32)
        elif dims == "nt":
            p = _dot_nt(a_ref[...], b_ref[...])
        else:
            p = _dot_tn(a_ref[...], b_ref[...])
        if nk == 1:
            o_ref[...] = p.astype(o_ref.dtype)
            return
        acc_ref, = scratch
        kk = pl.program_id(2)

        @pl.when(kk == 0)
        def _():
            acc_ref[...] = p

        @pl.when(kk > 0)
        def _():
            acc_ref[...] += p

        @pl.when(kk == nk - 1)
        def _():
            o_ref[...] = acc_ref[...].astype(o_ref.dtype)

    if dims == "tn":
        a_spec = pl.BlockSpec((tk, tm), lambda i, j, kk: (kk, i))
    else:
        a_spec = pl.BlockSpec((tm, tk), lambda i, j, kk: (i, kk))
    if dims == "nt":
        b_spec = pl.BlockSpec((tn, tk), lambda i, j, kk: (j, kk))
    else:
        b_spec = pl.BlockSpec((tk, tn), lambda i, j, kk: (kk, j))
    return _pcall(
        body, name=name,
        grid=(m // tm, n // tn, nk),
        in_specs=[a_spec, b_spec],
        out_specs=pl.BlockSpec((tm, tn), lambda i, j, kk: (i, j)),
        out_shape=jax.ShapeDtypeStruct((m, n), out_dtype),
        scratch_shapes=[pltpu.VMEM((tm, tn), F32)] if nk > 1 else [],
        compiler_params=_params(("parallel", "parallel", "arbitrary")),
    )(a, b)


def _row_spec(ts, d):
    return pl.BlockSpec((ts, d), lambda i: (i, 0))


def _vec_spec(d):
    return pl.BlockSpec((1, d), lambda i: (0, 0))


def _pre_fwd(x, g, scale, shift, name):
    s, d = x.shape
    ts = _pick(s, ROW_BLK)

    def body(x_ref, g_ref, sc_ref, sh_ref, h_ref):
        xv = x_ref[...]
        r = lax.rsqrt(jnp.mean(xv * xv, axis=-1, keepdims=True) + NORM_EPS)
        h_ref[...] = ((xv * r * g_ref[...]) * (1.0 + sc_ref[...]) + sh_ref[...]).astype(BF16)

    return _pcall(
        body, name=name, grid=(s // ts,),
        in_specs=[_row_spec(ts, d), _vec_spec(d), _vec_spec(d), _vec_spec(d)],
        out_specs=_row_spec(ts, d),
        out_shape=jax.ShapeDtypeStruct((s, d), BF16),
        compiler_params=_params(("parallel",)),
    )(x, g, scale, shift)


def _post_fwd(x, out, gate, g, name):
    s, d = x.shape
    ts = _pick(s, ROW_BLK)

    def body(x_ref, o_ref, gate_ref, g_ref, y_ref):
        o = o_ref[...]
        r = lax.rsqrt(jnp.mean(o * o, axis=-1, keepdims=True) + NORM_EPS)
        y_ref[...] = x_ref[...] + gate_ref[...] * (o * r * g_ref[...])

    return _pcall(
        body, name=name, grid=(s // ts,),
        in_specs=[_row_spec(ts, d), _row_spec(ts, d), _vec_spec(d), _vec_spec(d)],
        out_specs=_row_spec(ts, d),
        out_shape=jax.ShapeDtypeStruct((s, d), F32),
        compiler_params=_params(("parallel",)),
    )(x, out, gate, g)


def _loss_head(xf, tgt, name):
    s, d = xf.shape
    ts = _pick(s, ROW_BLK)
    ns = s // ts

    def body(x_ref, t_ref, dx_ref, loss_ref, acc_ref):
        i = pl.program_id(0)

        @pl.when(i == 0)
        def _():
            acc_ref[...] = jnp.zeros_like(acc_ref)

        e = x_ref[...] - t_ref[...]
        dx_ref[...] = e * (1.0 / d)
        acc_ref[...] += jnp.sum(e * e, axis=0, keepdims=True)

        @pl.when(i == ns - 1)
        def _():
            tot = jnp.sum(acc_ref[...], axis=1, keepdims=True) * (0.5 / d)
            loss_ref[...] = jnp.broadcast_to(tot, loss_ref.shape)

    return _pcall(
        body, name=name, grid=(ns,),
        in_specs=[_row_spec(ts, d), _row_spec(ts, d)],
        out_specs=[_row_spec(ts, d), pl.BlockSpec((1, LANE), lambda i: (0, 0))],
        out_shape=[jax.ShapeDtypeStruct((s, d), F32), jax.ShapeDtypeStruct((1, LANE), F32)],
        scratch_shapes=[pltpu.VMEM((1, d), F32)],
        compiler_params=_params(("arbitrary",)),
    )(xf, tgt)


def _post_bwd(dxo, out, gate, g, name):
    s, d = dxo.shape
    ts = _pick(s, ROW_BLK)

    def body(dx_ref, o_ref, gate_ref, g_ref, do_ref, dgate_ref, dg_ref):
        i = pl.program_id(0)

        @pl.when(i == 0)
        def _():
            dgate_ref[...] = jnp.zeros_like(dgate_ref)
            dg_ref[...] = jnp.zeros_like(dg_ref)

        o = o_ref[...]
        dx = dx_ref[...]
        gv = g_ref[...]
        r = lax.rsqrt(jnp.mean(o * o, axis=-1, keepdims=True) + NORM_EPS)
        n = o * r
        dyn = dx * gate_ref[...]
        dgate_ref[...] += jnp.sum(dx * (n * gv), axis=0, keepdims=True)
        dg_ref[...] += jnp.sum(dyn * n, axis=0, keepdims=True)
        dn = dyn * gv
        do_ref[...] = (r * (dn - n * jnp.mean(dn * n, axis=-1, keepdims=True))).astype(BF16)

    return _pcall(
        body, name=name, grid=(s // ts,),
        in_specs=[_row_spec(ts, d), _row_spec(ts, d), _vec_spec(d), _vec_spec(d)],
        out_specs=[_row_spec(ts, d), _vec_spec(d), _vec_spec(d)],
        out_shape=[jax.ShapeDtypeStruct((s, d), BF16), jax.ShapeDtypeStruct((1, d), F32),
                   jax.ShapeDtypeStruct((1, d), F32)],
        compiler_params=_params(("arbitrary",)),
    )(dxo, out, gate, g)


def _pre_bwd(dh, x, dxo, g, scale, name):
    s, d = x.shape
    ts = _pick(s, ROW_BLK)

    def body(dh_ref, x_ref, dxo_ref, g_ref, sc_ref, dx_ref, dsh_ref, dsc_ref, dg_ref):
        i = pl.program_id(0)

        @pl.when(i == 0)
        def _():
            dsh_ref[...] = jnp.zeros_like(dsh_ref)
            dsc_ref[...] = jnp.zeros_like(dsc_ref)
            dg_ref[...] = jnp.zeros_like(dg_ref)

        xv = x_ref[...]
        dhv = dh_ref[...]
        gv = g_ref[...]
        one_sc = 1.0 + sc_ref[...]
        r = lax.rsqrt(jnp.mean(xv * xv, axis=-1, keepdims=True) + NORM_EPS)
        n = xv * r
        dsh_ref[...] += jnp.sum(dhv, axis=0, keepdims=True)
        dsc_ref[...] += jnp.sum(dhv * (n * gv), axis=0, keepdims=True)
        dng = dhv * one_sc
        dg_ref[...] += jnp.sum(dng * n, axis=0, keepdims=True)
        dn = dng * gv
        dx_ref[...] = dxo_ref[...] + r * (dn - n * jnp.mean(dn * n, axis=-1, keepdims=True))

    return _pcall(
        body, name=name, grid=(s // ts,),
        in_specs=[_row_spec(ts, d), _row_spec(ts, d), _row_spec(ts, d), _vec_spec(d), _vec_spec(d)],
        out_specs=[_row_spec(ts, d), _vec_spec(d), _vec_spec(d), _vec_spec(d)],
        out_shape=[jax.ShapeDtypeStruct((s, d), F32)] + [jax.ShapeDtypeStruct((1, d), F32)] * 3,
        compiler_params=_params(("arbitrary",)),
    )(dh, x, dxo, g, scale)


def _sgu_mask():
    t = lax.broadcasted_iota(jnp.int32, (SGU_BLOCK, SGU_BLOCK), 0) // CHUNK
    s = lax.broadcasted_iota(jnp.int32, (SGU_BLOCK, SGU_BLOCK), 1) // CHUNK
    return s <= t


def _sgu_norm(v_pre, g):
    e = v_pre.shape[-1]
    vg = _gelu(v_pre)
    mu = jnp.sum(vg, axis=-1, keepdims=True) * (1.0 / e)
    dlt = vg - mu
    var = jnp.sum(dlt * dlt, axis=-1, keepdims=True) * (1.0 / e)
    rstd = lax.rsqrt(var + NORM_EPS)
    vhat = dlt * rstd
    return vhat, rstd, (vhat * g).astype(BF16)


def _sgu_mid_fwd(uvz, norm_g, w_s, bias_full, name):
    s, e3 = uvz.shape
    e = e3 // 3
    gd = e // SGU_GROUPS
    nb = s // SGU_BLOCK

    def body(uvz_ref, g_ref, w_ref, b_ref, y_ref, wsc):
        @pl.when(pl.program_id(0) == 0)
        def _():
            msk = _sgu_mask()
            for gi in range(SGU_GROUPS):
                wsc[gi] = jnp.where(msk, w_ref[gi], 0.0).astype(BF16)

        _, _, vb = _sgu_norm(uvz_ref[:, e:2 * e], g_ref[...])
        for gi in range(SGU_GROUPS):
            lo = gi * gd
            vm = jnp.dot(wsc[gi], vb[:, lo:lo + gd], preferred_element_type=F32) + b_ref[:, lo:lo + gd]
            zg = uvz_ref[:, 2 * e + lo:2 * e + lo + gd]
            y_ref[:, lo:lo + gd] = (_gelu(uvz_ref[:, lo:lo + gd]) * vm * (zg * _sigmoid(zg))).astype(BF16)

    return _pcall(
        body, name=name, grid=(nb,),
        in_specs=[pl.BlockSpec((SGU_BLOCK, e3), lambda n: (n, 0)),
                  pl.BlockSpec((1, e), lambda n: (0, 0)),
                  pl.BlockSpec((SGU_GROUPS, SGU_BLOCK, SGU_BLOCK), lambda n: (0, 0, 0)),
                  pl.BlockSpec((SGU_BLOCK, e), lambda n: (0, 0))],
        out_specs=pl.BlockSpec((SGU_BLOCK, e), lambda n: (n, 0)),
        out_shape=jax.ShapeDtypeStruct((s, e), BF16),
        scratch_shapes=[pltpu.VMEM((SGU_GROUPS, SGU_BLOCK, SGU_BLOCK), BF16)],
        compiler_params=_params(("arbitrary",)),
    )(uvz, norm_g, w_s, bias_full)


def _sgu_mid_bwd(uvz, dy, norm_g, w_s, bias_full, name):
    s, e3 = uvz.shape
    e = e3 // 3
    gd = e // SGU_GROUPS
    nb = s // SGU_BLOCK

    def body(uvz_ref, dy_ref, g_ref, w_ref, b_ref, d_ref, dw_ref, db_ref, dg_ref, wsc, wtsc, dvh_sc, dbacc):
        n = pl.program_id(0)

        @pl.when(n == 0)
        def _():
            msk = _sgu_mask()
            for gi in range(SGU_GROUPS):
                wm = jnp.where(msk, w_ref[gi], 0.0)
                wsc[gi] = wm.astype(BF16)
                wtsc[gi] = wm.T.astype(BF16)
            dw_ref[...] = jnp.zeros_like(dw_ref)
            dg_ref[...] = jnp.zeros_like(dg_ref)
            dbacc[...] = jnp.zeros_like(dbacc)

        v_pre = uvz_ref[:, e:2 * e]
        gv = g_ref[...]
        vhat, rstd, vb = _sgu_norm(v_pre, gv)
        s1 = jnp.zeros((SGU_BLOCK, 1), F32)
        s2 = jnp.zeros((SGU_BLOCK, 1), F32)
        for gi in range(SGU_GROUPS):
            lo = gi * gd
            u_pre = uvz_ref[:, lo:lo + gd]
            zg = uvz_ref[:, 2 * e + lo:2 * e + lo + gd]
            dyg = dy_ref[:, lo:lo + gd]
            ug = _gelu(u_pre)
            sig = _sigmoid(zg)
            vbg = vb[:, lo:lo + gd]
            vhg = vhat[:, lo:lo + gd]
            vm = jnp.dot(wsc[gi], vbg, preferred_element_type=F32) + b_ref[:, lo:lo + gd]
            t = dyg * (zg * sig)
            d_ref[:, lo:lo + gd] = (t * vm * _gelu_grad(u_pre)).astype(BF16)
            dvm = t * ug
            d_ref[:, 2 * e + lo:2 * e + lo + gd] = (dyg * ug * vm * (sig * (1.0 + zg * (1.0 - sig)))).astype(BF16)
            dvm_b = dvm.astype(BF16)
            dv = jnp.dot(wtsc[gi], dvm_b, preferred_element_type=F32)
            dw_ref[gi] += _dot_nt(dvm_b, vbg)
            dbacc[:, lo:lo + gd] += dvm
            dg_ref[:, lo:lo + gd] += jnp.sum(dv * vhg, axis=0, keepdims=True)
            dvh = dv * gv[:, lo:lo + gd]
            dvh_sc[:, lo:lo + gd] = dvh
            s1 = s1 + jnp.sum(dvh, axis=-1, keepdims=True)
            s2 = s2 + jnp.sum(dvh * vhg, axis=-1, keepdims=True)
        dvg = rstd * (dvh_sc[...] - s1 * (1.0 / e) - vhat * (s2 * (1.0 / e)))
        d_ref[:, e:2 * e] = (dvg * _gelu_grad(v_pre)).astype(BF16)

        @pl.when(n == nb - 1)
        def _():
            msk = _sgu_mask()
            for gi in range(SGU_GROUPS):
                dw_ref[gi] = jnp.where(msk, dw_ref[gi], 0.0)
                db_ref[gi] = jnp.sum(dbacc[:, gi * gd:(gi + 1) * gd], axis=1, keepdims=True)

    return _pcall(
        body, name=name, grid=(nb,),
        in_specs=[pl.BlockSpec((SGU_BLOCK, e3), lambda n: (n, 0)),
                  pl.BlockSpec((SGU_BLOCK, e), lambda n: (n, 0)),
                  pl.BlockSpec((1, e), lambda n: (0, 0)),
                  pl.BlockSpec((SGU_GROUPS, SGU_BLOCK, SGU_BLOCK), lambda n: (0, 0, 0)),
                  pl.BlockSpec((SGU_BLOCK, e), lambda n: (0, 0))],
        out_specs=[pl.BlockSpec((SGU_BLOCK, e3), lambda n: (n, 0)),
                   pl.BlockSpec((SGU_GROUPS, SGU_BLOCK, SGU_BLOCK), lambda n: (0, 0, 0)),
                   pl.BlockSpec((SGU_GROUPS, SGU_BLOCK, 1), lambda n: (0, 0, 0)),
                   pl.BlockSpec((1, e), lambda n: (0, 0))],
        out_shape=[jax.ShapeDtypeStruct((s, e3), BF16),
                   jax.ShapeDtypeStruct((SGU_GROUPS, SGU_BLOCK, SGU_BLOCK), F32),
                   jax.ShapeDtypeStruct((SGU_GROUPS, SGU_BLOCK, 1), F32),
                   jax.ShapeDtypeStruct((1, e), F32)],
        scratch_shapes=[pltpu.VMEM((SGU_GROUPS, SGU_BLOCK, SGU_BLOCK), BF16),
                        pltpu.VMEM((SGU_GROUPS, SGU_BLOCK, SGU_BLOCK), BF16),
                        pltpu.VMEM((SGU_BLOCK, e), F32),
                        pltpu.VMEM((SGU_BLOCK, e), F32)],
        compiler_params=_params(("arbitrary",)),
    )(uvz, dy, norm_g, w_s, bias_full)


def _rope_tables(s):
    pos = jnp.arange(s, dtype=F32)
    inv_freq = ROPE_THETA ** (-jnp.arange(0, ROPE, 2, dtype=F32) / ROPE)
    ang = pos[:, None] * inv_freq[None, :]
    cos, sin = jnp.cos(ang), jnp.sin(ang)
    z32 = jnp.zeros((s, HALF), F32)
    z64 = jnp.zeros((s, ROPE), F32)
    ck = jnp.concatenate([cos, cos, z64], axis=1)
    s1k = jnp.concatenate([-sin, z32, z64], axis=1)
    s2k = jnp.concatenate([z32, sin, z64], axis=1)
    one = jnp.ones((s, NOPE), F32)
    zn = jnp.zeros((s, NOPE), F32)
    return (ck, s1k, s2k), (jnp.concatenate([one, ck], axis=1), jnp.concatenate([zn, s1k], axis=1),
                            jnp.concatenate([zn, s2k], axis=1))


def _rot(x, c, s1, s2):
    w = x.shape[-1]
    return x * c + pltpu.roll(x, w - HALF, 1) * s1 + pltpu.roll(x, HALF, 1) * s2


def _rms(cv, n_real):
    r = lax.rsqrt(jnp.sum(cv * cv, axis=-1, keepdims=True) * (1.0 / n_real) + NORM_EPS)
    return r, cv * r


def _mla_norm_fwd(proj, gq, gkv, tabs, name):
    s = proj.shape[0]
    ts = _pick(s, ROW_BLK)
    ck, s1k, s2k = tabs

    def body(cq_ref, ckv_ref, kr_ref, gq_ref, gkv_ref, c_ref, s1_ref, s2_ref, qn_ref, kvn_ref, kro_ref):
        _, nq = _rms(cq_ref[...], Q_RANK)
        qn_ref[...] = (nq * gq_ref[...]).astype(BF16)
        _, nkv = _rms(ckv_ref[...], KV_RANK)
        kvn_ref[...] = (nkv * gkv_ref[...]).astype(BF16)
        kro_ref[...] = _rot(kr_ref[...], c_ref[...], s1_ref[...], s2_ref[...]).astype(BF16)

    tab = pl.BlockSpec((ts, LANE), lambda i: (i, 0))
    return _pcall(
        body, name=name, grid=(s // ts,),
        in_specs=[pl.BlockSpec((ts, Q_RANK_PAD), lambda i: (i, 0)),
                  pl.BlockSpec((ts, KV_RANK), lambda i: (i, PROJ_CKV // KV_RANK)),
                  pl.BlockSpec((ts, LANE), lambda i: (i, PROJ_KR // LANE)),
                  _vec_spec(Q_RANK_PAD), _vec_spec(KV_RANK), tab, tab, tab],
        out_specs=[pl.BlockSpec((ts, Q_RANK_PAD), lambda i: (i, 0)),
                   pl.BlockSpec((ts, KV_RANK), lambda i: (i, 0)), tab],
        out_shape=[jax.ShapeDtypeStruct((s, Q_RANK_PAD), BF16), jax.ShapeDtypeStruct((s, KV_RANK), BF16),
                   jax.ShapeDtypeStruct((s, LANE), BF16)],
        compiler_params=_params(("parallel",)),
    )(proj, proj, proj, gq, gkv, ck, s1k, s2k)


def _rope_heads(q, tabs, sign, mult, out_dtype, name):
    s, w = q.shape
    ts = _pick(s, ROW_BLK)
    c, s1, s2 = tabs

    def body(q_ref, c_ref, s1_ref, s2_ref, o_ref):
        cv, s1v, s2v = mult * c_ref[...], (sign * mult) * s1_ref[...], (sign * mult) * s2_ref[...]
        for h in range(w // HEAD_PAD):
            lo = h * HEAD_PAD
            o_ref[:, lo:lo + HEAD_PAD] = _rot(q_ref[:, lo:lo + HEAD_PAD].astype(F32), cv, s1v, s2v).astype(out_dtype)

    blk = pl.BlockSpec((ts, w), lambda i: (i, 0))
    tab = pl.BlockSpec((ts, HEAD_PAD), lambda i: (i, 0))
    return _pcall(
        body, name=name, grid=(s // ts,),
        in_specs=[blk, tab, tab, tab], out_specs=blk,
        out_shape=jax.ShapeDtypeStruct((s, w), out_dtype),
        compiler_params=_params(("parallel",)),
    )(q, c, s1, s2)


def _transpose_bf16(t):
    return t.astype(F32).T.astype(BF16)


def _diag_mask(tb, transposed):
    r = lax.broadcasted_iota(jnp.int32, (tb, tb), 0) // CHUNK
    c = lax.broadcasted_iota(jnp.int32, (tb, tb), 1) // CHUNK
    return (r <= c) if transposed else (c <= r)


def _attn_fwd(q_cat, kv, kr, proj, name):
    s = q_cat.shape[0]
    tb = _pick(s, ATT_BLK)
    nb = s // tb
    zcol = PROJ_Z // V_DIM

    def body(q_ref, kn_ref, v_ref, kr_ref, z_ref, o_ref, y_ref, lse_ref, kt_sc, vx_sc, m_sc, acc_sc, sa_sc, sb_sc):
        qi = pl.program_id(1)

        @pl.when(qi == 0)
        def _():
            for b in range(nb):
                rows = slice(b * tb, (b + 1) * tb)
                kt_sc[b] = _transpose_bf16(jnp.concatenate([kn_ref[rows, :], kr_ref[rows, :]], axis=1))
                vx_sc[b] = jnp.concatenate([v_ref[rows, :], jnp.ones((tb, V_DIM), BF16)], axis=1)

        m_sc[...] = jnp.full_like(m_sc, -1e30)
        acc_sc[...] = jnp.zeros_like(acc_sc)
        sub = min(tb, ATT_SUB)

        def scores(ki, s_ref):
            s_ref[...] = jnp.dot(q_ref[...], kt_sc[ki], preferred_element_type=F32)

        def step(ki, s_ref, masked):
            for r in range(tb // sub):
                rs = slice(r * sub, (r + 1) * sub)
                sc = s_ref[rs, :]
                if masked:
                    sc = jnp.where(_diag_mask(tb, False)[rs, :], sc, -1e30)
                m_prev = m_sc[rs, :]
                m_new = jnp.maximum(m_prev, jnp.max(sc, axis=-1, keepdims=True))
                p = jnp.exp2(sc - m_new).astype(BF16)
                acc_sc[rs, :] = (jnp.exp2(m_prev - m_new) * acc_sc[rs, :]
                                 + jnp.dot(p, vx_sc[ki], preferred_element_type=F32))
                m_sc[rs, :] = m_new

        def pair(t, carry):
            scores(2 * t + 1, sb_sc)
            step(2 * t, sa_sc, False)
            scores(2 * t + 2, sa_sc)
            step(2 * t + 1, sb_sc, False)
            return carry

        scores(0, sa_sc)
        lax.fori_loop(0, qi // 2, pair, 0)

        @pl.when(qi % 2 == 1)
        def _():
            scores(qi, sb_sc)
            step(qi - 1, sa_sc, False)
            step(qi, sb_sc, True)

        @pl.when(qi % 2 == 0)
        def _():
            step(qi, sa_sc, True)

        l = acc_sc[:, V_DIM:V_DIM + 1]
        o = acc_sc[:, :V_DIM] / l
        z = z_ref[...]
        o_ref[...] = o.astype(BF16)
        y_ref[...] = (o * (z * _sigmoid(z))).astype(BF16)
        lse_cols = jnp.broadcast_to(m_sc[...] + jnp.log2(l), (tb, LANE))
        lse_ref[...] = lse_cols.T[0:1, :]

    oblk = pl.BlockSpec((tb, V_DIM), lambda h, qi: (qi, h))
    return _pcall(
        body, name=name, grid=(HEADS, nb),
        in_specs=[pl.BlockSpec((tb, HEAD_PAD), lambda h, qi: (qi, h)),
                  pl.BlockSpec((s, NOPE), lambda h, qi: (0, 2 * h)),
                  pl.BlockSpec((s, V_DIM), lambda h, qi: (0, 2 * h + 1)),
                  pl.BlockSpec((s, LANE), lambda h, qi: (0, 0)),
                  pl.BlockSpec((tb, V_DIM), lambda h, qi: (qi, zcol + h))],
        out_specs=[oblk, oblk, pl.BlockSpec((None, None, 1, tb), lambda h, qi: (h, qi, 0, 0))],
        out_shape=[jax.ShapeDtypeStruct((s, MLA_WIDTH), BF16), jax.ShapeDtypeStruct((s, MLA_WIDTH), BF16),
                   jax.ShapeDtypeStruct((HEADS, nb, 1, tb), F32)],
        scratch_shapes=[pltpu.VMEM((nb, HEAD_PAD, tb), BF16), pltpu.VMEM((nb, tb, HEAD_PAD), BF16),
                        pltpu.VMEM((tb, 1), F32), pltpu.VMEM((tb, HEAD_PAD), F32),
                        pltpu.VMEM((tb, tb), F32), pltpu.VMEM((tb, tb), F32)],
        compiler_params=_params(("parallel", "arbitrary")),
    )(q_cat, kv, kv, kr, proj)


def _attn_bwd(q_cat, kv, kr, do, o, lse, name):
    s = q_cat.shape[0]
    tb = _pick(s, ATT_BLK)
    nb = s // tb
    ln2 = math.log(2.0)

    def body(q_ref, do_ref, o_ref, lse_ref, kn_ref, v_ref, kr_ref, dq_ref, dkv_ref, dkr_ref,
             qt_sc, dot_sc, delta_sc, dqt_sc, dk_sc, dv_sc):
        ki = pl.program_id(1)

        @pl.when(ki == 0)
        def _():
            for b in range(nb):
                rows = slice(b * tb, (b + 1) * tb)
                qt_sc[b] = _transpose_bf16(q_ref[rows, :])
                do_t = do_ref[rows, :].astype(F32).T
                dot_sc[b] = do_t.astype(BF16)
                delta_sc[b] = jnp.sum(do_t * o_ref[rows, :].astype(F32).T, axis=0, keepdims=True)
            dqt_sc[...] = jnp.zeros_like(dqt_sc)

        k = jnp.concatenate([kn_ref[...], kr_ref[...]], axis=1)
        kt = _transpose_bf16(k)
        vb = v_ref[...]
        dk_sc[...] = jnp.zeros_like(dk_sc)
        dv_sc[...] = jnp.zeros_like(dv_sc)

        def step(qi, masked):
            rows = pl.ds(pl.multiple_of(qi * tb, tb), tb)
            sc_t = jnp.dot(k, qt_sc[qi], preferred_element_type=F32)
            if masked:
                sc_t = jnp.where(_diag_mask(tb, True), sc_t, -1e30)
            p_t = jnp.exp2(sc_t - lse_ref[qi])
            dp_t = jnp.dot(vb, dot_sc[qi], preferred_element_type=F32)
            ds_t = (p_t * (dp_t - delta_sc[qi])).astype(BF16)
            dv_sc[...] += jnp.dot(p_t.astype(BF16), do_ref[rows, :], preferred_element_type=F32)
            dk_sc[...] += jnp.dot(ds_t, q_ref[rows, :], preferred_element_type=F32)
            dqt_sc[qi] += jnp.dot(kt, ds_t, preferred_element_type=F32)

        step(ki, True)

        def loop_body(qi, carry):
            step(qi, False)
            return carry

        lax.fori_loop(ki + 1, nb, loop_body, 0)

        dkv_ref[:, :NOPE] = (dk_sc[:, :NOPE] * ln2).astype(BF16)
        dkv_ref[:, NOPE:] = dv_sc[...].astype(BF16)
        dkr_ref[...] = dk_sc[:, NOPE:] * ln2

        @pl.when(ki == nb - 1)
        def _():
            for b in range(nb):
                dq_ref[b * tb:(b + 1) * tb, :] = dqt_sc[b].T

    return _pcall(
        body, name=name, grid=(HEADS, nb),
        in_specs=[pl.BlockSpec((s, HEAD_PAD), lambda h, ki: (0, h)),
                  pl.BlockSpec((s, V_DIM), lambda h, ki: (0, h)),
                  pl.BlockSpec((s, V_DIM), lambda h, ki: (0, h)),
                  pl.BlockSpec((None, nb, 1, tb), lambda h, ki: (h, 0, 0, 0)),
                  pl.BlockSpec((tb, NOPE), lambda h, ki: (ki, 2 * h)),
                  pl.BlockSpec((tb, V_DIM), lambda h, ki: (ki, 2 * h + 1)),
                  pl.BlockSpec((tb, LANE), lambda h, ki: (ki, 0))],
        out_specs=[pl.BlockSpec((s, HEAD_PAD), lambda h, ki: (0, h)),
                   pl.BlockSpec((tb, HEAD_PAD), lambda h, ki: (ki, h)),
                   pl.BlockSpec((None, tb, LANE), lambda h, ki: (h, ki, 0))],
        out_shape=[jax.ShapeDtypeStruct((s, HEADS * HEAD_PAD), F32),
                   jax.ShapeDtypeStruct((s, HEADS * HEAD_PAD), BF16),
                   jax.ShapeDtypeStruct((HEADS, s, LANE), F32)],
        scratch_shapes=[pltpu.VMEM((nb, HEAD_PAD, tb), BF16), pltpu.VMEM((nb, V_DIM, tb), BF16),
                        pltpu.VMEM((nb, 1, tb), F32), pltpu.VMEM((nb, HEAD_PAD, tb), F32),
                        pltpu.VMEM((tb, HEAD_PAD), F32), pltpu.VMEM((tb, V_DIM), F32)],
        compiler_params=_params(("parallel", "arbitrary")),
    )(q_cat, do, o, lse, kv, kv, kr)


def _mla_gate_bwd(dy, o, proj, name):
    s = dy.shape[0]
    ts = _pick(s, ROW_BLK)
    zcol = PROJ_Z // V_DIM

    def body(dy_ref, o_ref, p_ref, do_ref, dz_ref):
        z = p_ref[:, PROJ_Z:]
        dyv = dy_ref[...]
        sig = _sigmoid(z)
        do_ref[...] = (dyv * (z * sig)).astype(BF16)
        dz_ref[...] = (dyv * o_ref[...].astype(F32) * (sig * (1.0 + z * (1.0 - sig)))).astype(BF16)

    blk = pl.BlockSpec((ts, MLA_WIDTH), lambda i: (i, 0))
    return _pcall(
        body, name=name, grid=(s // ts,),
        in_specs=[blk, blk, pl.BlockSpec((ts, PROJ_W), lambda i: (i, 0))],
        out_specs=[blk, blk],
        out_shape=[jax.ShapeDtypeStruct((s, MLA_WIDTH), BF16), jax.ShapeDtypeStruct((s, MLA_WIDTH), BF16)],
        compiler_params=_params(("parallel",)),
    )(dy, o, proj)


def _mla_norm_bwd(dqn, dkvn, dkr_heads, proj, gq, gkv, tabs, name):
    s = proj.shape[0]
    ts = _pick(s, ROW_BLK)
    ck, s1k, s2k = tabs

    def rms_bwd(cv, dn_in, g, n_real):
        r, n = _rms(cv, n_real)
        dg = jnp.sum(dn_in * n, axis=0, keepdims=True)
        dn = dn_in * g
        dc = r * (dn - n * (jnp.sum(dn * n, axis=-1, keepdims=True) * (1.0 / n_real)))
        return dc, dg

    def body(dqn_ref, dkvn_ref, dkr_ref, cq_ref, ckv_ref, gq_ref, gkv_ref, c_ref, s1_ref, s2_ref,
             dp_ref, dgq_ref, dgkv_ref):
        @pl.when(pl.program_id(0) == 0)
        def _():
            dgq_ref[...] = jnp.zeros_like(dgq_ref)
            dgkv_ref[...] = jnp.zeros_like(dgkv_ref)

        dcq, dgq = rms_bwd(cq_ref[...], dqn_ref[...], gq_ref[...], Q_RANK)
        dckv, dgkv = rms_bwd(ckv_ref[...], dkvn_ref[...], gkv_ref[...], KV_RANK)
        dgq_ref[...] += dgq
        dgkv_ref[...] += dgkv
        dkr = dkr_ref[0]
        for h in range(1, HEADS):
            dkr = dkr + dkr_ref[h]
        dp_ref[:, PROJ_CQ:PROJ_CKV] = dcq.astype(BF16)
        dp_ref[:, PROJ_CKV:PROJ_KR] = dckv.astype(BF16)
        dp_ref[:, PROJ_KR:PROJ_Z] = _rot(dkr, c_ref[...], -s1_ref[...], -s2_ref[...]).astype(BF16)

    tab = pl.BlockSpec((ts, LANE), lambda i: (i, 0))
    return _pcall(
        body, name=name, grid=(s // ts,),
        in_specs=[pl.BlockSpec((ts, Q_RANK_PAD), lambda i: (i, 0)),
                  pl.BlockSpec((ts, KV_RANK), lambda i: (i, 0)),
                  pl.BlockSpec((HEADS, ts, LANE), lambda i: (0, i, 0)),
                  pl.BlockSpec((ts, Q_RANK_PAD), lambda i: (i, 0)),
                  pl.BlockSpec((ts, KV_RANK), lambda i: (i, PROJ_CKV // KV_RANK)),
                  _vec_spec(Q_RANK_PAD), _vec_spec(KV_RANK), tab, tab, tab],
        out_specs=[pl.BlockSpec((ts, PROJ_Z), lambda i: (i, 0)), _vec_spec(Q_RANK_PAD), _vec_spec(KV_RANK)],
        out_shape=[jax.ShapeDtypeStruct((s, PROJ_Z), BF16), jax.ShapeDtypeStruct((1, Q_RANK_PAD), F32),
                   jax.ShapeDtypeStruct((1, KV_RANK), F32)],
        compiler_params=_params(("arbitrary",)),
    )(dqn, dkvn, dkr_heads, proj, proj, gq, gkv, ck, s1k, s2k)


def _ada_mod(cond_raw, ada_w, bias_my, name):
    nl, d, ncol = ada_w.shape
    tk = _pick(d, 512)
    nk = d // tk

    def body(c_ref, w_ref, b_ref, o_ref, acc_ref):
        kk = pl.program_id(1)

        @pl.when(kk == 0)
        def _():
            acc_ref[...] = jnp.zeros_like(acc_ref)

        cv = c_ref[...]
        cond = (cv * _sigmoid(cv)).astype(BF16)
        acc_ref[...] += jnp.dot(cond, w_ref[...].astype(BF16), preferred_element_type=F32)

        @pl.when(kk == nk - 1)
        def _():
            o_ref[...] = acc_ref[...] + b_ref[...]

    return _pcall(
        body, name=name, grid=(nl, nk),
        in_specs=[pl.BlockSpec((N_DEV, tk), lambda l, kk: (0, kk)),
                  pl.BlockSpec((None, tk, ncol), lambda l, kk: (l, kk, 0)),
                  pl.BlockSpec((None, 1, ncol), lambda l, kk: (l, 0, 0))],
        out_specs=pl.BlockSpec((None, N_DEV, ncol), lambda l, kk: (l, 0, 0)),
        out_shape=jax.ShapeDtypeStruct((nl, N_DEV, ncol), F32),
        scratch_shapes=[pltpu.VMEM((N_DEV, ncol), F32)],
        compiler_params=_params(("parallel", "arbitrary")),
    )(cond_raw, ada_w, bias_my.reshape(nl, 1, ncol))


def _adam(w, g, m, v):
    m = ADAM_B1 * m + (1.0 - ADAM_B1) * g
    v = ADAM_B2 * v + (1.0 - ADAM_B2) * (g * g)
    m_hat = m / (1.0 - ADAM_B1 ** ADAM_STEP)
    v_hat = v / (1.0 - ADAM_B2 ** ADAM_STEP)
    delta = -ADAM_LR * (m_hat / (jnp.sqrt(v_hat) + ADAM_EPS) + ADAM_WD * w)
    return delta, m, v


def _ada_bwd_adam(cond_t, dmod_cols, w, m, v, name):
    nl, d, ncol = w.shape
    tk = _pick(d, 512)

    def body(c_ref, dm_ref, w_ref, m_ref, v_ref, g_ref, d_ref, mo_ref, vo_ref):
        cv = c_ref[...]
        cond = (cv * _sigmoid(cv)).astype(BF16)
        g = jnp.dot(cond, dm_ref[...].astype(BF16), preferred_element_type=F32)
        delta, m2, v2 = _adam(w_ref[...], g, m_ref[...], v_ref[...])
        g_ref[...] = g
        d_ref[...] = delta
        mo_ref[...] = m2
        vo_ref[...] = v2

    blk = pl.BlockSpec((None, tk, ncol), lambda l, kk: (l, kk, 0))
    shp = jax.ShapeDtypeStruct((nl, d, ncol), F32)
    return _pcall(
        body, name=name, grid=(nl, d // tk),
        in_specs=[pl.BlockSpec((tk, N_DEV), lambda l, kk: (kk, 0)),
                  pl.BlockSpec((None, N_DEV, ncol), lambda l, kk: (l, 0, 0)), blk, blk, blk],
        out_specs=[blk, blk, blk, blk], out_shape=[shp, shp, shp, shp],
        compiler_params=_params(("parallel", "parallel")),
    )(cond_t, dmod_cols, w, m, v)


def _adam_reduce(recv0, recv1, w, m, v, name):
    nl, r, c = w.shape
    tr = _pick(r, 128) if r % 128 == 0 else r
    tc = _pick(c, 1024)

    def body(r0_ref, r1_ref, w_ref, m_ref, v_ref, g_ref, d_ref, mo_ref, vo_ref):
        l = pl.program_id(0)

        def run(rr):
            g = rr[0].astype(F32)
            for sidx in range(1, N_DEV):
                g = g + rr[sidx].astype(F32)
            delta, m2, v2 = _adam(w_ref[...], g, m_ref[...], v_ref[...])
            g_ref[...] = g
            d_ref[...] = delta
            mo_ref[...] = m2
            vo_ref[...] = v2

        @pl.when(l == 0)
        def _():
            run(r0_ref)

        @pl.when(l == 1)
        def _():
            run(r1_ref)

    rblk = pl.BlockSpec((N_DEV, tr, tc), lambda l, i, j: (0, i, j))
    blk = pl.BlockSpec((None, tr, tc), lambda l, i, j: (l, i, j))
    shp = jax.ShapeDtypeStruct((nl, r, c), F32)
    return _pcall(
        body, name=name, grid=(nl, r // tr, c // tc),
        in_specs=[rblk, rblk, blk, blk, blk],
        out_specs=[blk, blk, blk, blk], out_shape=[shp, shp, shp, shp],
        compiler_params=_params(("arbitrary", "parallel", "parallel")),
    )(recv0.reshape(N_DEV, r, c), recv1.reshape(N_DEV, r, c), w, m, v)


def _adam_small(gathered, w, m, v, name):
    r = w.shape[0]
    tr = _pick(r, 512) if r % 512 == 0 else r

    def body(p_ref, w_ref, m_ref, v_ref, g_ref, d_ref, mo_ref, vo_ref):
        g = p_ref[0]
        for sidx in range(1, N_DEV):
            g = g + p_ref[sidx]
        delta, m2, v2 = _adam(w_ref[...], g, m_ref[...], v_ref[...])
        g_ref[...] = g
        d_ref[...] = delta
        mo_ref[...] = m2
        vo_ref[...] = v2

    blk = pl.BlockSpec((tr, LANE), lambda i: (i, 0))
    shp = jax.ShapeDtypeStruct((r, LANE), F32)
    return _pcall(
        body, name=name, grid=(r // tr,),
        in_specs=[pl.BlockSpec((N_DEV, tr, LANE), lambda i: (0, i, 0)), blk, blk, blk],
        out_specs=[blk, blk, blk, blk], out_shape=[shp, shp, shp, shp],
        compiler_params=_params(("parallel",)),
    )(gathered, w, m, v)


def _my_place():
    x, y, c = lax.axis_index("x"), lax.axis_index("y"), lax.axis_index("c")
    return x, y, c, 4 * x + 2 * y + c


def _peer(x, y, c, k):
    px = 1 - x if (k >> 2) & 1 else x
    py = 1 - y if (k >> 1) & 1 else y
    pc = 1 - c if k & 1 else c
    return (px, py, pc), 4 * px + 2 * py + pc


def _slab(ref, shape, kind, p):
    r, cd = shape
    if kind == "row":
        return ref.at[pl.ds(pl.multiple_of(p * r, SUBLANE), r), :]
    return ref.at[:, pl.ds(pl.multiple_of(p * cd, LANE), cd)]


def _exchange_layout(arrays, kinds, gather):
    shard_shapes, dst_kinds, out_shapes = [], [], []
    for a, kind in zip(arrays, kinds):
        r, cd = a.shape
        if gather:
            shard, dst_kind = (r, cd), kind
        else:
            shard, dst_kind = ((r // N_DEV, cd) if kind == "row" else (r, cd // N_DEV)), "row"
        shard_shapes.append(shard)
        dst_kinds.append(dst_kind)
        full = (shard[0] * N_DEV, shard[1]) if dst_kind == "row" else (shard[0], shard[1] * N_DEV)
        out_shapes.append(jax.ShapeDtypeStruct(full, a.dtype))
    return shard_shapes, dst_kinds, out_shapes


def _exchange_copies(ins, outs, send_sems, recv_sems, sem_of, layout, kinds, gather):
    shard_shapes, dst_kinds, _ = layout
    x, y, c, me = _my_place()

    def src_for(a, p):
        return ins[a] if gather else _slab(ins[a], shard_shapes[a], kinds[a], p)

    def dst_slot(a, p):
        return _slab(outs[a], shard_shapes[a], dst_kinds[a], p)

    def local(a, sem):
        return pltpu.make_async_copy(src_for(a, me), dst_slot(a, me), sem)

    def remote(a, k, slot):
        peer, pidx = _peer(x, y, c, k)
        return pltpu.make_async_remote_copy(
            src_ref=src_for(a, pidx), dst_ref=dst_slot(a, me if slot == "mine" else pidx),
            send_sem=send_sems.at[sem_of(a, k)], recv_sem=recv_sems.at[sem_of(a, k)],
            device_id=peer, device_id_type=MESH)

    return local, remote


def _exchange(arrays, kinds, gather, name):
    n = len(arrays)
    layout = _exchange_layout(arrays, kinds, gather)

    def body(*refs):
        ins, outs = refs[:n], refs[n:2 * n]
        send_sems, recv_sems, local_sems = refs[2 * n:]
        local, remote = _exchange_copies(ins, outs, send_sems, recv_sems,
                                         lambda a, k: a * (N_DEV - 1) + k - 1, layout, kinds, gather)
        for a in range(n):
            local(a, local_sems.at[a]).start()
            for k in range(1, N_DEV):
                remote(a, k, "mine").start()
        for a in range(n):
            for k in range(1, N_DEV):
                arrival = remote(a, k, "theirs")
                arrival.wait_send()
                arrival.wait_recv()
            local(a, local_sems.at[a]).wait()

    anyspec = pl.BlockSpec(memory_space=pl.ANY)
    outs = _pcall(
        body, comm=True, name=name,
        in_specs=[anyspec] * n, out_specs=[anyspec] * n, out_shape=layout[2],
        scratch_shapes=[pltpu.SemaphoreType.DMA((n * (N_DEV - 1),)), pltpu.SemaphoreType.DMA((n * (N_DEV - 1),)),
                        pltpu.SemaphoreType.DMA((n,))],
    )(*arrays)
    return list(outs)


HBM_SPEC = pl.BlockSpec(memory_space=pltpu.HBM)
SEM_SPEC = pl.BlockSpec(memory_space=pltpu.SEMAPHORE)
ANY_SPEC = pl.BlockSpec(memory_space=pl.ANY)
DATAFLOW = pltpu.SideEffectType.DATAFLOW_SIDE_EFFECTING


def _exchange_start(arrays, kinds, gather, name, after, carry=()):
    n, nc = len(arrays), len(carry)
    layout = _exchange_layout(arrays, kinds, gather)
    lands = [lax.empty(s.shape, s.dtype) for s in layout[2]]

    def body(*refs):
        ins, outs = refs[:n], refs[n:2 * n]
        send_sems, recv_sems = refs[2 * n + nc + 1], refs[2 * n + nc + 2]
        token = refs[2 * n + nc + 3 + 2 * n + nc]
        _, remote = _exchange_copies(ins, outs, send_sems, recv_sems, lambda a, k: a, layout, kinds, gather)
        for a in range(n):
            for k in range(1, N_DEV):
                remote(a, k, "mine").start()
        token[...] = jnp.zeros_like(token)

    passed = list(arrays) + lands + list(carry)
    res = pl.pallas_call(
        body, name=name,
        out_shape=(pltpu.SemaphoreType.DMA((n,)), pltpu.SemaphoreType.DMA((n,)),
                   *[pltpu.HBM(t.shape, t.dtype) for t in passed], jax.ShapeDtypeStruct((SUBLANE, LANE), F32)),
        in_specs=[HBM_SPEC] * (2 * n + nc) + [ANY_SPEC],
        out_specs=(SEM_SPEC, SEM_SPEC, *([HBM_SPEC] * (2 * n + nc)), pl.BlockSpec(memory_space=pltpu.VMEM)),
        input_output_aliases={i: 2 + i for i in range(2 * n + nc)},
        compiler_params=pltpu.CompilerParams(has_side_effects=DATAFLOW),
    )(*[pltpu.with_memory_space_constraint(t, pltpu.HBM) for t in passed], after)
    handle = (res[0], res[1], list(res[2:2 + n]), list(res[2 + n:2 + 2 * n]), tuple(kinds), gather)
    return handle, res[-1], list(res[2 + 2 * n:2 + 2 * n + nc])


def _exchange_wait(handle, name, after):
    send_sems, recv_sems, ins_thru, lands_thru, kinds, gather = handle
    n = len(ins_thru)
    layout = _exchange_layout(ins_thru, kinds, gather)

    def body(*refs):
        ins, outs = refs[:n], refs[n:2 * n]
        s_sems, r_sems = refs[2 * n], refs[2 * n + 1]
        local_sems = refs[-1]
        local, remote = _exchange_copies(ins, outs, s_sems, r_sems, lambda a, k: a, layout, kinds, gather)
        for a in range(n):
            local(a, local_sems.at[a]).start()
        for a in range(n):
            for k in range(1, N_DEV):
                arrival = remote(a, k, "theirs")
                arrival.wait_send()
                arrival.wait_recv()
        for a in range(n):
            local(a, local_sems.at[a]).wait()

    res = pl.pallas_call(
        body, name=name,
        out_shape=[pltpu.HBM(t.shape, t.dtype) for t in ins_thru + lands_thru],
        in_specs=[HBM_SPEC] * (2 * n) + [SEM_SPEC, SEM_SPEC, ANY_SPEC],
        out_specs=[HBM_SPEC] * (2 * n),
        input_output_aliases={i: i for i in range(2 * n)},
        scratch_shapes=[pltpu.SemaphoreType.DMA((n,))],
        compiler_params=pltpu.CompilerParams(has_side_effects=DATAFLOW),
    )(*ins_thru, *lands_thru, send_sems, recv_sems, after)
    return list(res[n:2 * n])


def _pad_mla_w_in(w):
    d = w.shape[0]
    z = lambda n: jnp.zeros((d, n), w.dtype)
    o1, o2, o3 = Q_RANK, Q_RANK + KV_RANK, Q_RANK + KV_RANK + ROPE
    return jnp.concatenate([w[:, :o1], z(Q_RANK_PAD - Q_RANK), w[:, o1:o2], w[:, o2:o3], z(LANE - ROPE), w[:, o3:]], axis=1)


def _unpad_mla_w_in(g):
    return jnp.concatenate([g[:, :Q_RANK], g[:, PROJ_CKV:PROJ_KR], g[:, PROJ_KR:PROJ_KR + ROPE], g[:, PROJ_Z:]], axis=1)


def _pad_w_uq(w):
    w3 = w.reshape(Q_RANK, HEADS, NOPE + ROPE)
    w3 = jnp.pad(w3, ((0, Q_RANK_PAD - Q_RANK), (0, 0), (0, HEAD_PAD - NOPE - ROPE)))
    return w3.reshape(Q_RANK_PAD, HEADS * HEAD_PAD)


def _unpad_w_uq(g):
    return g[:Q_RANK].reshape(Q_RANK, HEADS, HEAD_PAD)[:, :, :NOPE + ROPE].reshape(Q_RANK, HEADS * (NOPE + ROPE))


def _pack(pieces):
    flat = [p.reshape(-1).astype(F32) for p in pieces]
    tot = sum(f.shape[0] for f in flat)
    unit = SUBLANE * LANE
    padn = (-tot) % unit
    if padn:
        flat.append(jnp.zeros((padn,), F32))
    return jnp.concatenate(flat).reshape(-1, LANE)


def _unpack(packed, shapes):
    flat = packed.reshape(-1)
    out, off = [], 0
    for shp in shapes:
        nel = int(np.prod(shp))
        out.append(flat[off:off + nel].reshape(shp))
        off += nel
    return out


WEIGHTS = ['ada_w', 'ada_b', 'pre_g', 'post_g', 'sgu_w_in', 'sgu_norm_g', 'sgu_w_s', 'sgu_b_s', 'sgu_w_out',
           'mla_w_in', 'mla_q_norm_g', 'mla_kv_norm_g', 'mla_w_uq', 'mla_w_ukv', 'mla_w_out']
INPUTS = ['x', 'c'] + WEIGHTS + ['loss_target'] + ['m_' + n for n in WEIGHTS] + ['v_' + n for n in WEIGHTS]


def kernel(x, c, ada_w, ada_b, pre_g, post_g, sgu_w_in, sgu_norm_g, sgu_w_s, sgu_b_s, sgu_w_out, mla_w_in, mla_q_norm_g, mla_kv_norm_g, mla_w_uq, mla_w_ukv, mla_w_out, loss_target, m_ada_w, m_ada_b, m_pre_g, m_post_g, m_sgu_w_in, m_sgu_norm_g, m_sgu_w_s, m_sgu_b_s, m_sgu_w_out, m_mla_w_in, m_mla_q_norm_g, m_mla_kv_norm_g, m_mla_w_uq, m_mla_w_ukv, m_mla_w_out, v_ada_w, v_ada_b, v_pre_g, v_post_g, v_sgu_w_in, v_sgu_norm_g, v_sgu_w_s, v_sgu_b_s, v_sgu_w_out, v_mla_w_in, v_mla_q_norm_g, v_mla_kv_norm_g, v_mla_w_uq, v_mla_w_ukv, v_mla_w_out):
    given = locals()
    A = {name: given[name] for name in INPUTS}
    x0 = A['x'][0]
    tgt = A['loss_target'][0]
    s, d = x0.shape
    e = 2 * d
    ncol = 3 * d // N_DEV
    _, _, _, me = _my_place()
    ktabs, qtabs = _rope_tables(s)

    gains = jnp.zeros((SUBLANE, LANE), F32)
    gains = gains.at[0:2, :Q_RANK // N_DEV].set(A['mla_q_norm_g'])
    gains = gains.at[2:4, :KV_RANK // N_DEV].set(A['mla_kv_norm_g'])
    c8 = jnp.broadcast_to(A['c'], (SUBLANE, d))
    cg, gg = _exchange([c8, gains], ["row", "row"], True, "ag_cond")
    cond_raw = cg.reshape(N_DEV, SUBLANE, d)[:, 0, :]
    gg = gg.reshape(N_DEV, SUBLANE, LANE)
    gq_full = jnp.transpose(gg[:, 0:2, :Q_RANK // N_DEV], (1, 0, 2)).reshape(N_MIX, Q_RANK)
    gkv_full = jnp.transpose(gg[:, 2:4, :KV_RANK // N_DEV], (1, 0, 2)).reshape(N_MIX, KV_RANK)
    gq_pad = jnp.pad(gq_full, ((0, 0), (0, Q_RANK_PAD - Q_RANK)))

    bias_my = lax.dynamic_slice_in_dim(A['ada_b'], me * ncol, ncol, axis=1)
    mod_part = _ada_mod(cond_raw, A['ada_w'], bias_my, "ada_mod")
    send = jnp.pad(jnp.transpose(mod_part, (1, 0, 2)), ((0, 0), (0, SUBLANE - DEPTH), (0, 0)))
    (rb,) = _exchange([send.reshape(N_DEV * SUBLANE, ncol)], ["row"], False, "a2a_mod")

    token = rb
    gathers = []
    for i in range(DEPTH):
        j = i // N_MIX
        if i % N_MIX == 0:
            shards, kinds = [A['sgu_w_in'][j], A['sgu_w_out'][j]], ["col", "row"]
        else:
            shards = [A['mla_w_in'][j], A['mla_w_uq'][j], A['mla_w_ukv'][j], A['mla_w_out'][j]]
            kinds = ["col", "col", "col", "row"]
        handle, token, _ = _exchange_start([t.astype(BF16) for t in shards], kinds, True, f"ag_start_{i}", token)
        gathers.append(handle)

    mod = jnp.transpose(rb.reshape(N_DEV, SUBLANE, ncol)[:, :DEPTH, :], (1, 0, 2)).reshape(DEPTH, 3 * d) + token[0, 0]
    shift = [mod[i:i + 1, :d] for i in range(DEPTH)]
    scale = [mod[i:i + 1, d:2 * d] for i in range(DEPTH)]
    gate = [mod[i:i + 1, 2 * d:] for i in range(DEPTH)]

    saved = []
    x = x0
    for i in range(DEPTH):
        j = i // N_MIX
        h = _pre_fwd(x, A['pre_g'][i:i + 1], scale[i], shift[i], f"pre_fwd")
        if i % N_MIX == 0:
            w_in, w_out = _exchange_wait(gathers[i], f"ag_wait_{i}", h)
            uvz = _mm(h, w_in, "nn", F32, "sgu_in")
            bias_full = jnp.repeat(A['sgu_b_s'][j].T, e // SGU_GROUPS, axis=1)
            ng = A['sgu_norm_g'][j:j + 1]
            y = _sgu_mid_fwd(uvz, ng, A['sgu_w_s'][j], bias_full, "sgu_mid_fwd")
            out = _mm(y, w_out, "nn", F32, "sgu_out")
            saved.append(dict(x=x, h=h, uvz=uvz, y=y, out=out, w_in=w_in, w_out=w_out, bias=bias_full, ng=ng))
        else:
            w_in, w_uq, w_ukv, w_out = _exchange_wait(gathers[i], f"ag_wait_{i}", h)
            w_in = _pad_mla_w_in(w_in)
            w_uq = _pad_w_uq(w_uq)
            gq, gkv = gq_pad[j:j + 1], gkv_full[j:j + 1]
            proj = _mm(h, w_in, "nn", F32, "mla_in")
            qn, kvn, kr = _mla_norm_fwd(proj, gq, gkv, ktabs, "mla_norm_fwd")
            q = _mm(qn, w_uq, "nn", F32, "mla_uq")
            q_cat = _rope_heads(q, qtabs, 1.0, SOFTMAX_SCALE * LOG2_E, BF16, "rope_q_fwd")
            kv = _mm(kvn, w_ukv, "nn", BF16, "mla_ukv")
            o, y, lse = _attn_fwd(q_cat, kv, kr, proj, "attn_fwd")
            out = _mm(y, w_out, "nn", F32, "mla_out")
            saved.append(dict(x=x, h=h, proj=proj, qn=qn, kvn=kvn, kr=kr, q_cat=q_cat, kv=kv, o=o, y=y, lse=lse,
                              out=out, w_in=w_in, w_uq=w_uq, w_ukv=w_ukv, w_out=w_out, gq=gq, gkv=gkv))
        x = _post_fwd(x, out, gate[i], A['post_g'][i:i + 1], "post_fwd")

    dx, loss_row = _loss_head(x, tgt, "loss_head")
    loss = lax.psum(loss_row[0, 0], ("x", "y", "c"))

    d_shift, d_scale, d_gate = [None] * DEPTH, [None] * DEPTH, [None] * DEPTH
    d_pre, d_post = [None] * DEPTH, [None] * DEPTH
    scatters = [None] * DEPTH
    small = {}
    for i in reversed(range(DEPTH)):
        j = i // N_MIX
        sv = saved[i]
        dout, d_gate[i], d_post[i] = _post_bwd(dx, sv['out'], gate[i], A['post_g'][i:i + 1], "post_bwd")
        if i % N_MIX == 0:
            dy = _mm(dout, sv['w_out'], "nt", F32, "sgu_out_dx")
            g_w_out = _mm(sv['y'], dout, "tn", BF16, "sgu_out_dw")
            duvz, dws, dbs, dng = _sgu_mid_bwd(sv['uvz'], dy, sv['ng'], A['sgu_w_s'][j], sv['bias'], "sgu_mid_bwd")
            g_w_in = _mm(sv['h'], duvz, "tn", BF16, "sgu_in_dw")
            scatters[i], token, (duvz,) = _exchange_start([g_w_in, g_w_out], ["col", "row"], False, f"rs_start_{i}",
                                                          token, carry=[duvz])
            dh = _mm(duvz, sv['w_in'], "nt", F32, "sgu_in_dx")
            small[('sgu', j)] = (dws, dbs.reshape(SGU_GROUPS, SGU_BLOCK), dng)
        else:
            dy = _mm(dout, sv['w_out'], "nt", F32, "mla_out_dx")
            g_w_out = _mm(sv['y'], dout, "tn", BF16, "mla_out_dw")
            do, dz = _mla_gate_bwd(dy, sv['o'], sv['proj'], "mla_gate_bwd")
            dq, dkv, dkr_heads = _attn_bwd(sv['q_cat'], sv['kv'], sv['kr'], do, sv['o'], sv['lse'], "attn_bwd")
            dq_b = _rope_heads(dq, qtabs, -1.0, SOFTMAX_SCALE, BF16, "rope_q_bwd")
            dqn = _mm(dq_b, sv['w_uq'], "nt", F32, "mla_uq_dx")
            g_w_uq = _unpad_w_uq(_mm(sv['qn'], dq_b, "tn", BF16, "mla_uq_dw"))
            dkvn = _mm(dkv, sv['w_ukv'], "nt", F32, "mla_ukv_dx")
            g_w_ukv = _mm(sv['kvn'], dkv, "tn", BF16, "mla_ukv_dw")
            dpa, dgq, dgkv = _mla_norm_bwd(dqn, dkvn, dkr_heads, sv['proj'], sv['gq'], sv['gkv'], ktabs, "mla_norm_bwd")
            dproj = jnp.concatenate([dpa, dz], axis=1)
            g_w_in = _unpad_mla_w_in(_mm(sv['h'], dproj, "tn", BF16, "mla_in_dw"))
            scatters[i], token, (dproj,) = _exchange_start(
                [g_w_in, g_w_uq, g_w_ukv, g_w_out], ["col", "col", "col", "row"], False, f"rs_start_{i}", token,
                carry=[dproj])
            dh = _mm(dproj, sv['w_in'], "nt", F32, "mla_in_dx")
            small[('mla', j)] = (dgq[:, :Q_RANK], dgkv)
        dx, d_shift[i], d_scale[i], d_pre[i] = _pre_bwd(dh, sv['x'], dx, A['pre_g'][i:i + 1], scale[i], "pre_bwd")

    res = {}

    def big(name, recv0, recv1):
        res[name] = _adam_reduce(recv0, recv1, A[name], A['m_' + name], A['v_' + name], "adam_" + name)

    recv_mla = {j: _exchange_wait(scatters[N_MIX * j + 1], f"rs_wait_{N_MIX * j + 1}", dx) for j in reversed(range(N_MIX))}
    for idx_w, name in enumerate(['mla_w_in', 'mla_w_uq', 'mla_w_ukv', 'mla_w_out']):
        big(name, recv_mla[0][idx_w], recv_mla[1][idx_w])

    dmod = jnp.concatenate([jnp.concatenate([d_shift[i], d_scale[i], d_gate[i]], axis=1) for i in range(DEPTH)], axis=0)
    pieces = [dmod, jnp.concatenate(d_pre, axis=0), jnp.concatenate(d_post, axis=0),
              jnp.concatenate([small[('sgu', j)][2] for j in range(N_MIX)], axis=0),
              jnp.stack([small[('sgu', j)][0] for j in range(N_MIX)]),
              jnp.stack([small[('sgu', j)][1] for j in range(N_MIX)]),
              jnp.concatenate([small[('mla', j)][0] for j in range(N_MIX)], axis=0),
              jnp.concatenate([small[('mla', j)][1] for j in range(N_MIX)], axis=0)]
    small_names = ['ada_b', 'pre_g', 'post_g', 'sgu_norm_g', 'sgu_w_s', 'sgu_b_s', 'mla_q_norm_g', 'mla_kv_norm_g']
    full_shapes = [p.shape for p in pieces]
    packed = _pack(pieces)
    rows = packed.shape[0]
    (gathered,) = _exchange([packed], ["row"], True, "ag_small")
    gathered = gathered.reshape(N_DEV, rows, LANE)

    def full_size(prefix):
        out = []
        for name, shp in zip(small_names, full_shapes):
            t = A[prefix + name]
            if t.shape != tuple(shp):
                t = lax.dynamic_update_slice_in_dim(jnp.zeros(shp, F32), t, me * t.shape[1], axis=1)
            out.append(t)
        return _pack(out)

    sm = _adam_small(gathered, full_size(''), full_size('m_'), full_size('v_'), "adam_small")
    sm = [_unpack(t, full_shapes) for t in sm]
    for k_out in range(4):
        for name, val in zip(small_names, sm[k_out]):
            t = A[name]
            if t.shape != val.shape:
                val = lax.dynamic_slice_in_dim(val, me * t.shape[1], t.shape[1], axis=1)
            res.setdefault(name, [None] * 4)[k_out] = val

    nrow_dmod = DEPTH * 3 * d // LANE
    dmod_all = gathered[:, :nrow_dmod, :].reshape(N_DEV, DEPTH, 3 * d)
    dmod_cols = jnp.transpose(lax.dynamic_slice_in_dim(dmod_all, me * ncol, ncol, axis=2), (1, 0, 2))
    res['ada_w'] = _ada_bwd_adam(jnp.transpose(cond_raw), dmod_cols, A['ada_w'], A['m_ada_w'], A['v_ada_w'], "ada_bwd")

    recv_sgu = {j: _exchange_wait(scatters[N_MIX * j], f"rs_wait_{N_MIX * j}", res['ada_w'][0]) for j in reversed(range(N_MIX))}
    for idx_w, name in enumerate(['sgu_w_in', 'sgu_w_out']):
        big(name, recv_sgu[0][idx_w], recv_sgu[1][idx_w])

    outs = [loss, dx[None]]
    for k_out in range(4):
        outs += [res[n][k_out] for n in WEIGHTS]
    return tuple(outs)
```

```python
import functools
import math

import numpy as np
import jax
import jax.numpy as jnp
from jax import lax
from jax.experimental import pallas as pl
from jax.experimental.pallas import tpu as pltpu

F32 = jnp.float32
BF16 = jnp.bfloat16
MESH = pl.DeviceIdType.MESH

N_DEV = 8
DEPTH = 4
N_MIX = 2
NORM_EPS = 1e-6
CHUNK = 64
SGU_BLOCK = 128
SGU_GROUPS = 16
HEADS = 16
Q_RANK = 448
Q_RANK_PAD = 512
KV_RANK = 512
NOPE = 128
ROPE = 64
HALF = ROPE // 2
V_DIM = 128
HEAD_PAD = 256
ROPE_THETA = 10000.0
MLA_WIDTH = HEADS * V_DIM
LANE = 128
SUBLANE = 8
PROJ_CQ = 0
PROJ_CKV = Q_RANK_PAD
PROJ_KR = Q_RANK_PAD + KV_RANK
PROJ_Z = PROJ_KR + LANE
PROJ_W = PROJ_Z + MLA_WIDTH

ADAM_LR = 0.001
ADAM_B1 = 0.9
ADAM_B2 = 0.999
ADAM_EPS = 1e-08
ADAM_WD = 0.01
ADAM_STEP = 10

VMEM_LIMIT = 56 * 1024 * 1024
ATT_BLK = 512
ATT_SUB = 128
ROW_BLK = 256
MM_TM, MM_TN, MM_TK = 1024, 1024, 2048
MM_TILE_BYTES = 40 * 1024 * 1024
SOFTMAX_SCALE = (NOPE + ROPE) ** -0.5
LOG2_E = 1.0 / math.log(2.0)
INV_SQRT2 = 1.0 / math.sqrt(2.0)
INV_SQRT_2PI = 1.0 / math.sqrt(2.0 * math.pi)


def _pcall(body, comm=False, **kw):
    return pl.pallas_call(body, **kw)


def _params(sem=None):
    return pltpu.CompilerParams(dimension_semantics=sem, vmem_limit_bytes=VMEM_LIMIT)


def _pick(dim, pref):
    if dim <= pref:
        return dim
    t = (pref // LANE) * LANE
    while t >= LANE:
        if dim % t == 0:
            return t
        t -= LANE
    return dim


def _gelu(x):
    return 0.5 * x * (1.0 + lax.erf(x * INV_SQRT2))


def _gelu_grad(x):
    return 0.5 * (1.0 + lax.erf(x * INV_SQRT2)) + x * jnp.exp(-0.5 * x * x) * INV_SQRT_2PI


def _sigmoid(x):
    return 1.0 / (1.0 + jnp.exp(-x))


def _dot_nt(a, b):
    return lax.dot_general(a, b, (((1,), (1,)), ((), ())), preferred_element_type=F32)


def _dot_tn(a, b):
    return lax.dot_general(a, b, (((0,), (0,)), ((), ())), preferred_element_type=F32)


def _mm(a, b, dims, out_dtype, name):
    if dims == "nn":
        (m, k), (k2, n) = a.shape, b.shape
    elif dims == "nt":
        (m, k), (n, k2) = a.shape, b.shape
    else:
        (k, m), (k2, n) = a.shape, b.shape
    assert k == k2, (a.shape, b.shape, dims)
    tm, tn = _pick(m, MM_TM), _pick(n, MM_TN)
    out_bytes = 2 * tm * tn * jnp.dtype(out_dtype).itemsize
    whole_k = 2 * (tm + tn) * k * a.dtype.itemsize + out_bytes <= MM_TILE_BYTES
    tk = k if whole_k else _pick(k, MM_TK)
    nk = k // tk

    def body(a_ref, b_ref, o_ref, *scratch):
        if dims == "nn":
            p = jnp.dot(a_ref[...], b_ref[...], preferred_element_type=F32)
        elif dims == "nt":
            p = _dot_nt(a_ref[...], b_ref[...])
        else:
            p = _dot_tn(a_ref[...], b_ref[...])
        if nk == 1:
            o_ref[...] = p.astype(o_ref.dtype)
            return
        acc_ref, = scratch
        kk = pl.program_id(2)

        @pl.when(kk == 0)
        def _():
            acc_ref[...] = p

        @pl.when(kk > 0)
        def _():
            acc_ref[...] += p

        @pl.when(kk == nk - 1)
        def _():
            o_ref[...] = acc_ref[...].astype(o_ref.dtype)

    if dims == "tn":
        a_spec = pl.BlockSpec((tk, tm), lambda i, j, kk: (kk, i))
    else:
        a_spec = pl.BlockSpec((tm, tk), lambda i, j, kk: (i, kk))
    if dims == "nt":
        b_spec = pl.BlockSpec((tn, tk), lambda i, j, kk: (j, kk))
    else:
        b_spec = pl.BlockSpec((tk, tn), lambda i, j, kk: (kk, j))
    return _pcall(
        body, name=name,
        grid=(m // tm, n // tn, nk),
        in_specs=[a_spec, b_spec],
        out_specs=pl.BlockSpec((tm, tn), lambda i, j, kk: (i, j)),
        out_shape=jax.ShapeDtypeStruct((m, n), out_dtype),
        scratch_shapes=[pltpu.VMEM((tm, tn), F32)] if nk > 1 else [],
        compiler_params=_params(("parallel", "parallel", "arbitrary")),
    )(a, b)


def _row_spec(ts, d):
    return pl.BlockSpec((ts, d), lambda i: (i, 0))


def _vec_spec(d):
    return pl.BlockSpec((1, d), lambda i: (0, 0))


def _pre_fwd(x, g, scale, shift, name):
    s, d = x.shape
    ts = _pick(s, ROW_BLK)

    def body(x_ref, g_ref, sc_ref, sh_ref, h_ref):
        xv = x_ref[...]
        r = lax.rsqrt(jnp.mean(xv * xv, axis=-1, keepdims=True) + NORM_EPS)
        h_ref[...] = ((xv * r * g_ref[...]) * (1.0 + sc_ref[...]) + sh_ref[...]).astype(BF16)

    return _pcall(
        body, name=name, grid=(s // ts,),
        in_specs=[_row_spec(ts, d), _vec_spec(d), _vec_spec(d), _vec_spec(d)],
        out_specs=_row_spec(ts, d),
        out_shape=jax.ShapeDtypeStruct((s, d), BF16),
        compiler_params=_params(("parallel",)),
    )(x, g, scale, shift)


def _post_fwd(x, out, gate, g, name):
    s, d = x.shape
    ts = _pick(s, ROW_BLK)

    def body(x_ref, o_ref, gate_ref, g_ref, y_ref):
        o = o_ref[...]
        r = lax.rsqrt(jnp.mean(o * o, axis=-1, keepdims=True) + NORM_EPS)
        y_ref[...] = x_ref[...] + gate_ref[...] * (o * r * g_ref[...])

    return _pcall(
        body, name=name, grid=(s // ts,),
        in_specs=[_row_spec(ts, d), _row_spec(ts, d), _vec_spec(d), _vec_spec(d)],
        out_specs=_row_spec(ts, d),
        out_shape=jax.ShapeDtypeStruct((s, d), F32),
        compiler_params=_params(("parallel",)),
    )(x, out, gate, g)


def _loss_head(xf, tgt, name):
    s, d = xf.shape
    ts = _pick(s, ROW_BLK)
    ns = s // ts

    def body(x_ref, t_ref, dx_ref, loss_ref, acc_ref):
        i = pl.program_id(0)

        @pl.when(i == 0)
        def _():
            acc_ref[...] = jnp.zeros_like(acc_ref)

        e = x_ref[...] - t_ref[...]
        dx_ref[...] = e * (1.0 / d)
        acc_ref[...] += jnp.sum(e * e, axis=0, keepdims=True)

        @pl.when(i == ns - 1)
        def _():
            tot = jnp.sum(acc_ref[...], axis=1, keepdims=True) * (0.5 / d)
            loss_ref[...] = jnp.broadcast_to(tot, loss_ref.shape)

    return _pcall(
        body, name=name, grid=(ns,),
        in_specs=[_row_spec(ts, d), _row_spec(ts, d)],
        out_specs=[_row_spec(ts, d), pl.BlockSpec((1, LANE), lambda i: (0, 0))],
        out_shape=[jax.ShapeDtypeStruct((s, d), F32), jax.ShapeDtypeStruct((1, LANE), F32)],
        scratch_shapes=[pltpu.VMEM((1, d), F32)],
        compiler_params=_params(("arbitrary",)),
    )(xf, tgt)


def _post_bwd(dxo, out, gate, g, name):
    s, d = dxo.shape
    ts = _pick(s, ROW_BLK)

    def body(dx_ref, o_ref, gate_ref, g_ref, do_ref, dgate_ref, dg_ref):
        i = pl.program_id(0)

        @pl.when(i == 0)
        def _():
            dgate_ref[...] = jnp.zeros_like(dgate_ref)
            dg_ref[...] = jnp.zeros_like(dg_ref)

        o = o_ref[...]
        dx = dx_ref[...]
        gv = g_ref[...]
        r = lax.rsqrt(jnp.mean(o * o, axis=-1, keepdims=True) + NORM_EPS)
        n = o * r
        dyn = dx * gate_ref[...]
        dgate_ref[...] += jnp.sum(dx * (n * gv), axis=0, keepdims=True)
        dg_ref[...] += jnp.sum(dyn * n, axis=0, keepdims=True)
        dn = dyn * gv
        do_ref[...] = (r * (dn - n * jnp.mean(dn * n, axis=-1, keepdims=True))).astype(BF16)

    return _pcall(
        body, name=name, grid=(s // ts,),
        in_specs=[_row_spec(ts, d), _row_spec(ts, d), _vec_spec(d), _vec_spec(d)],
        out_specs=[_row_spec(ts, d), _vec_spec(d), _vec_spec(d)],
        out_shape=[jax.ShapeDtypeStruct((s, d), BF16), jax.ShapeDtypeStruct((1, d), F32),
                   jax.ShapeDtypeStruct((1, d), F32)],
        compiler_params=_params(("arbitrary",)),
    )(dxo, out, gate, g)


def _pre_bwd(dh, x, dxo, g, scale, name):
    s, d = x.shape
    ts = _pick(s, ROW_BLK)

    def body(dh_ref, x_ref, dxo_ref, g_ref, sc_ref, dx_ref, dsh_ref, dsc_ref, dg_ref):
        i = pl.program_id(0)

        @pl.when(i == 0)
        def _():
            dsh_ref[...] = jnp.zeros_like(dsh_ref)
            dsc_ref[...] = jnp.zeros_like(dsc_ref)
            dg_ref[...] = jnp.zeros_like(dg_ref)

        xv = x_ref[...]
        dhv = dh_ref[...]
        gv = g_ref[...]
        one_sc = 1.0 + sc_ref[...]
        r = lax.rsqrt(jnp.mean(xv * xv, axis=-1, keepdims=True) + NORM_EPS)
        n = xv * r
        dsh_ref[...] += jnp.sum(dhv, axis=0, keepdims=True)
        dsc_ref[...] += jnp.sum(dhv * (n * gv), axis=0, keepdims=True)
        dng = dhv * one_sc
        dg_ref[...] += jnp.sum(dng * n, axis=0, keepdims=True)
        dn = dng * gv
        dx_ref[...] = dxo_ref[...] + r * (dn - n * jnp.mean(dn * n, axis=-1, keepdims=True))

    return _pcall(
        body, name=name, grid=(s // ts,),
        in_specs=[_row_spec(ts, d), _row_spec(ts, d), _row_spec(ts, d), _vec_spec(d), _vec_spec(d)],
        out_specs=[_row_spec(ts, d), _vec_spec(d), _vec_spec(d), _vec_spec(d)],
        out_shape=[jax.ShapeDtypeStruct((s, d), F32)] + [jax.ShapeDtypeStruct((1, d), F32)] * 3,
        compiler_params=_params(("arbitrary",)),
    )(dh, x, dxo, g, scale)


def _sgu_mask():
    t = lax.broadcasted_iota(jnp.int32, (SGU_BLOCK, SGU_BLOCK), 0) // CHUNK
    s = lax.broadcasted_iota(jnp.int32, (SGU_BLOCK, SGU_BLOCK), 1) // CHUNK
    return s <= t


def _sgu_norm(v_pre, g):
    e = v_pre.shape[-1]
    vg = _gelu(v_pre)
    mu = jnp.sum(vg, axis=-1, keepdims=True) * (1.0 / e)
    dlt = vg - mu
    var = jnp.sum(dlt * dlt, axis=-1, keepdims=True) * (1.0 / e)
    rstd = lax.rsqrt(var + NORM_EPS)
    vhat = dlt * rstd
    return vhat, rstd, (vhat * g).astype(BF16)


def _sgu_mid_fwd(uvz, norm_g, w_s, bias_full, name):
    s, e3 = uvz.shape
    e = e3 // 3
    gd = e // SGU_GROUPS
    nb = s // SGU_BLOCK

    def body(uvz_ref, g_ref, w_ref, b_ref, y_ref, wsc):
        @pl.when(pl.program_id(0) == 0)
        def _():
            msk = _sgu_mask()
            for gi in range(SGU_GROUPS):
                wsc[gi] = jnp.where(msk, w_ref[gi], 0.0).astype(BF16)

        _, _, vb = _sgu_norm(uvz_ref[:, e:2 * e], g_ref[...])
        for gi in range(SGU_GROUPS):
            lo = gi * gd
            vm = jnp.dot(wsc[gi], vb[:, lo:lo + gd], preferred_element_type=F32) + b_ref[:, lo:lo + gd]
            zg = uvz_ref[:, 2 * e + lo:2 * e + lo + gd]
            y_ref[:, lo:lo + gd] = (_gelu(uvz_ref[:, lo:lo + gd]) * vm * (zg * _sigmoid(zg))).astype(BF16)

    return _pcall(
        body, name=name, grid=(nb,),
        in_specs=[pl.BlockSpec((SGU_BLOCK, e3), lambda n: (n, 0)),
                  pl.BlockSpec((1, e), lambda n: (0, 0)),
                  pl.BlockSpec((SGU_GROUPS, SGU_BLOCK, SGU_BLOCK), lambda n: (0, 0, 0)),
                  pl.BlockSpec((SGU_BLOCK, e), lambda n: (0, 0))],
        out_specs=pl.BlockSpec((SGU_BLOCK, e), lambda n: (n, 0)),
        out_shape=jax.ShapeDtypeStruct((s, e), BF16),
        scratch_shapes=[pltpu.VMEM((SGU_GROUPS, SGU_BLOCK, SGU_BLOCK), BF16)],
        compiler_params=_params(("arbitrary",)),
    )(uvz, norm_g, w_s, bias_full)


def _sgu_mid_bwd(uvz, dy, norm_g, w_s, bias_full, name):
    s, e3 = uvz.shape
    e = e3 // 3
    gd = e // SGU_GROUPS
    nb = s // SGU_BLOCK

    def body(uvz_ref, dy_ref, g_ref, w_ref, b_ref, d_ref, dw_ref, db_ref, dg_ref, wsc, wtsc, dvh_sc, dbacc):
        n = pl.program_id(0)

        @pl.when(n == 0)
        def _():
            msk = _sgu_mask()
            for gi in range(SGU_GROUPS):
                wm = jnp.where(msk, w_ref[gi], 0.0)
                wsc[gi] = wm.astype(BF16)
                wtsc[gi] = wm.T.astype(BF16)
            dw_ref[...] = jnp.zeros_like(dw_ref)
            dg_ref[...] = jnp.zeros_like(dg_ref)
            dbacc[...] = jnp.zeros_like(dbacc)

        v_pre = uvz_ref[:, e:2 * e]
        gv = g_ref[...]
        vhat, rstd, vb = _sgu_norm(v_pre, gv)
        s1 = jnp.zeros((SGU_BLOCK, 1), F32)
        s2 = jnp.zeros((SGU_BLOCK, 1), F32)
        for gi in range(SGU_GROUPS):
            lo = gi * gd
            u_pre = uvz_ref[:, lo:lo + gd]
            zg = uvz_ref[:, 2 * e + lo:2 * e + lo + gd]
            dyg = dy_ref[:, lo:lo + gd]
            ug = _gelu(u_pre)
            sig = _sigmoid(zg)
            vbg = vb[:, lo:lo + gd]
            vhg = vhat[:, lo:lo + gd]
            vm = jnp.dot(wsc[gi], vbg, preferred_element_type=F32) + b_ref[:, lo:lo + gd]
            t = dyg * (zg * sig)
            d_ref[:, lo:lo + gd] = (t * vm * _gelu_grad(u_pre)).astype(BF16)
            dvm = t * ug
            d_ref[:, 2 * e + lo:2 * e + lo + gd] = (dyg * ug * vm * (sig * (1.0 + zg * (1.0 - sig)))).astype(BF16)
            dvm_b = dvm.astype(BF16)
            dv = jnp.dot(wtsc[gi], dvm_b, preferred_element_type=F32)
            dw_ref[gi] += _dot_nt(dvm_b, vbg)
            dbacc[:, lo:lo + gd] += dvm
            dg_ref[:, lo:lo + gd] += jnp.sum(dv * vhg, axis=0, keepdims=True)
            dvh = dv * gv[:, lo:lo + gd]
            dvh_sc[:, lo:lo + gd] = dvh
            s1 = s1 + jnp.sum(dvh, axis=-1, keepdims=True)
            s2 = s2 + jnp.sum(dvh * vhg, axis=-1, keepdims=True)
        dvg = rstd * (dvh_sc[...] - s1 * (1.0 / e) - vhat * (s2 * (1.0 / e)))
        d_ref[:, e:2 * e] = (dvg * _gelu_grad(v_pre)).astype(BF16)

        @pl.when(n == nb - 1)
        def _():
            msk = _sgu_mask()
            for gi in range(SGU_GROUPS):
                dw_ref[gi] = jnp.where(msk, dw_ref[gi], 0.0)
                db_ref[gi] = jnp.sum(dbacc[:, gi * gd:(gi + 1) * gd], axis=1, keepdims=True)

    return _pcall(
        body, name=name, grid=(nb,),
        in_specs=[pl.BlockSpec((SGU_BLOCK, e3), lambda n: (n, 0)),
                  pl.BlockSpec((SGU_BLOCK, e), lambda n: (n, 0)),
                  pl.BlockSpec((1, e), lambda n: (0, 0)),
                  pl.BlockSpec((SGU_GROUPS, SGU_BLOCK, SGU_BLOCK), lambda n: (0, 0, 0)),
                  pl.BlockSpec((SGU_BLOCK, e), lambda n: (0, 0))],
        out_specs=[pl.BlockSpec((SGU_BLOCK, e3), lambda n: (n, 0)),
                   pl.BlockSpec((SGU_GROUPS, SGU_BLOCK, SGU_BLOCK), lambda n: (0, 0, 0)),
                   pl.BlockSpec((SGU_GROUPS, SGU_BLOCK, 1), lambda n: (0, 0, 0)),
                   pl.BlockSpec((1, e), lambda n: (0, 0))],
        out_shape=[jax.ShapeDtypeStruct((s, e3), BF16),
                   jax.ShapeDtypeStruct((SGU_GROUPS, SGU_BLOCK, SGU_BLOCK), F32),
                   jax.ShapeDtypeStruct((SGU_GROUPS, SGU_BLOCK, 1), F32),
                   jax.ShapeDtypeStruct((1, e), F32)],
        scratch_shapes=[pltpu.VMEM((SGU_GROUPS, SGU_BLOCK, SGU_BLOCK), BF16),
                        pltpu.VMEM((SGU_GROUPS, SGU_BLOCK, SGU_BLOCK), BF16),
                        pltpu.VMEM((SGU_BLOCK, e), F32),
                        pltpu.VMEM((SGU_BLOCK, e), F32)],
        compiler_params=_params(("arbitrary",)),
    )(uvz, dy, norm_g, w_s, bias_full)


def _rope_tables(s):
    pos = jnp.arange(s, dtype=F32)
    inv_freq = ROPE_THETA ** (-jnp.arange(0, ROPE, 2, dtype=F32) / ROPE)
    ang = pos[:, None] * inv_freq[None, :]
    cos, sin = jnp.cos(ang), jnp.sin(ang)
    z32 = jnp.zeros((s, HALF), F32)
    z64 = jnp.zeros((s, ROPE), F32)
    ck = jnp.concatenate([cos, cos, z64], axis=1)
    s1k = jnp.concatenate([-sin, z32, z64], axis=1)
    s2k = jnp.concatenate([z32, sin, z64], axis=1)
    one = jnp.ones((s, NOPE), F32)
    zn = jnp.zeros((s, NOPE), F32)
    return (ck, s1k, s2k), (jnp.concatenate([one, ck], axis=1), jnp.concatenate([zn, s1k], axis=1),
                            jnp.concatenate([zn, s2k], axis=1))


def _rot(x, c, s1, s2):
    w = x.shape[-1]
    return x * c + pltpu.roll(x, w - HALF, 1) * s1 + pltpu.roll(x, HALF, 1) * s2


def _rms(cv, n_real):
    r = lax.rsqrt(jnp.sum(cv * cv, axis=-1, keepdims=True) * (1.0 / n_real) + NORM_EPS)
    return r, cv * r


def _mla_norm_fwd(proj, gq, gkv, tabs, name):
    s = proj.shape[0]
    ts = _pick(s, ROW_BLK)
    ck, s1k, s2k = tabs

    def body(cq_ref, ckv_ref, kr_ref, gq_ref, gkv_ref, c_ref, s1_ref, s2_ref, qn_ref, kvn_ref, kro_ref):
        _, nq = _rms(cq_ref[...], Q_RANK)
        qn_ref[...] = (nq * gq_ref[...]).astype(BF16)
        _, nkv = _rms(ckv_ref[...], KV_RANK)
        kvn_ref[...] = (nkv * gkv_ref[...]).astype(BF16)
        kro_ref[...] = _rot(kr_ref[...], c_ref[...], s1_ref[...], s2_ref[...]).astype(BF16)

    tab = pl.BlockSpec((ts, LANE), lambda i: (i, 0))
    return _pcall(
        body, name=name, grid=(s // ts,),
        in_specs=[pl.BlockSpec((ts, Q_RANK_PAD), lambda i: (i, 0)),
                  pl.BlockSpec((ts, KV_RANK), lambda i: (i, PROJ_CKV // KV_RANK)),
                  pl.BlockSpec((ts, LANE), lambda i: (i, PROJ_KR // LANE)),
                  _vec_spec(Q_RANK_PAD), _vec_spec(KV_RANK), tab, tab, tab],
        out_specs=[pl.BlockSpec((ts, Q_RANK_PAD), lambda i: (i, 0)),
                   pl.BlockSpec((ts, KV_RANK), lambda i: (i, 0)), tab],
        out_shape=[jax.ShapeDtypeStruct((s, Q_RANK_PAD), BF16), jax.ShapeDtypeStruct((s, KV_RANK), BF16),
                   jax.ShapeDtypeStruct((s, LANE), BF16)],
        compiler_params=_params(("parallel",)),
    )(proj, proj, proj, gq, gkv, ck, s1k, s2k)


def _rope_heads(q, tabs, sign, mult, out_dtype, name):
    s, w = q.shape
    ts = _pick(s, ROW_BLK)
    c, s1, s2 = tabs

    def body(q_ref, c_ref, s1_ref, s2_ref, o_ref):
        cv, s1v, s2v = mult * c_ref[...], (sign * mult) * s1_ref[...], (sign * mult) * s2_ref[...]
        for h in range(w // HEAD_PAD):
            lo = h * HEAD_PAD
            o_ref[:, lo:lo + HEAD_PAD] = _rot(q_ref[:, lo:lo + HEAD_PAD].astype(F32), cv, s1v, s2v).astype(out_dtype)

    blk = pl.BlockSpec((ts, w), lambda i: (i, 0))
    tab = pl.BlockSpec((ts, HEAD_PAD), lambda i: (i, 0))
    return _pcall(
        body, name=name, grid=(s // ts,),
        in_specs=[blk, tab, tab, tab], out_specs=blk,
        out_shape=jax.ShapeDtypeStruct((s, w), out_dtype),
        compiler_params=_params(("parallel",)),
    )(q, c, s1, s2)


def _transpose_bf16(t):
    return t.astype(F32).T.astype(BF16)


def _diag_mask(tb, transposed):
    r = lax.broadcasted_iota(jnp.int32, (tb, tb), 0) // CHUNK
    c = lax.broadcasted_iota(jnp.int32, (tb, tb), 1) // CHUNK
    return (r <= c) if transposed else (c <= r)


def _attn_fwd(q_cat, kv, kr, proj, name):
    s = q_cat.shape[0]
    tb = _pick(s, ATT_BLK)
    nb = s // tb
    zcol = PROJ_Z // V_DIM

    def body(q_ref, kn_ref, v_ref, kr_ref, z_ref, o_ref, y_ref, lse_ref, kt_sc, vx_sc, m_sc, acc_sc, sa_sc, sb_sc):
        qi = pl.program_id(1)

        @pl.when(qi == 0)
        def _():
            for b in range(nb):
                rows = slice(b * tb, (b + 1) * tb)
                kt_sc[b] = _transpose_bf16(jnp.concatenate([kn_ref[rows, :], kr_ref[rows, :]], axis=1))
                vx_sc[b] = jnp.concatenate([v_ref[rows, :], jnp.ones((tb, V_DIM), BF16)], axis=1)

        m_sc[...] = jnp.full_like(m_sc, -1e30)
        acc_sc[...] = jnp.zeros_like(acc_sc)
        sub = min(tb, ATT_SUB)

        def scores(ki, s_ref):
            s_ref[...] = jnp.dot(q_ref[...], kt_sc[ki], preferred_element_type=F32)

        def step(ki, s_ref, masked):
            for r in range(tb // sub):
                rs = slice(r * sub, (r + 1) * sub)
                sc = s_ref[rs, :]
                if masked:
                    sc = jnp.where(_diag_mask(tb, False)[rs, :], sc, -1e30)
                m_prev = m_sc[rs, :]
                m_new = jnp.maximum(m_prev, jnp.max(sc, axis=-1, keepdims=True))
                p = jnp.exp2(sc - m_new).astype(BF16)
                acc_sc[rs, :] = (jnp.exp2(m_prev - m_new) * acc_sc[rs, :]
                                 + jnp.dot(p, vx_sc[ki], preferred_element_type=F32))
                m_sc[rs, :] = m_new

        def pair(t, carry):
            scores(2 * t + 1, sb_sc)
            step(2 * t, sa_sc, False)
            scores(2 * t + 2, sa_sc)
            step(2 * t + 1, sb_sc, False)
            return carry

        scores(0, sa_sc)
        lax.fori_loop(0, qi // 2, pair, 0)

        @pl.when(qi % 2 == 1)
        def _():
            scores(qi, sb_sc)
            step(qi - 1, sa_sc, False)
            step(qi, sb_sc, True)

        @pl.when(qi % 2 == 0)
        def _():
            step(qi, sa_sc, True)

        l = acc_sc[:, V_DIM:V_DIM + 1]
        o = acc_sc[:, :V_DIM] / l
        z = z_ref[...]
        o_ref[...] = o.astype(BF16)
        y_ref[...] = (o * (z * _sigmoid(z))).astype(BF16)
        lse_cols = jnp.broadcast_to(m_sc[...] + jnp.log2(l), (tb, LANE))
        lse_ref[...] = lse_cols.T[0:1, :]

    oblk = pl.BlockSpec((tb, V_DIM), lambda h, qi: (qi, h))
    return _pcall(
        body, name=name, grid=(HEADS, nb),
        in_specs=[pl.BlockSpec((tb, HEAD_PAD), lambda h, qi: (qi, h)),
                  pl.BlockSpec((s, NOPE), lambda h, qi: (0, 2 * h)),
                  pl.BlockSpec((s, V_DIM), lambda h, qi: (0, 2 * h + 1)),
                  pl.BlockSpec((s, LANE), lambda h, qi: (0, 0)),
                  pl.BlockSpec((tb, V_DIM), lambda h, qi: (qi, zcol + h))],
        out_specs=[oblk, oblk, pl.BlockSpec((None, None, 1, tb), lambda h, qi: (h, qi, 0, 0))],
        out_shape=[jax.ShapeDtypeStruct((s, MLA_WIDTH), BF16), jax.ShapeDtypeStruct((s, MLA_WIDTH), BF16),
                   jax.ShapeDtypeStruct((HEADS, nb, 1, tb), F32)],
        scratch_shapes=[pltpu.VMEM((nb, HEAD_PAD, tb), BF16), pltpu.VMEM((nb, tb, HEAD_PAD), BF16),
                        pltpu.VMEM((tb, 1), F32), pltpu.VMEM((tb, HEAD_PAD), F32),
                        pltpu.VMEM((tb, tb), F32), pltpu.VMEM((tb, tb), F32)],
        compiler_params=_params(("parallel", "arbitrary")),
    )(q_cat, kv, kv, kr, proj)


def _attn_bwd(q_cat, kv, kr, do, o, lse, name):
    s = q_cat.shape[0]
    tb = _pick(s, ATT_BLK)
    nb = s // tb
    ln2 = math.log(2.0)

    def body(q_ref, do_ref, o_ref, lse_ref, kn_ref, v_ref, kr_ref, dq_ref, dkv_ref, dkr_ref,
             qt_sc, dot_sc, delta_sc, dqt_sc, dk_sc, dv_sc):
        ki = pl.program_id(1)

        @pl.when(ki == 0)
        def _():
            for b in range(nb):
                rows = slice(b * tb, (b + 1) * tb)
                qt_sc[b] = _transpose_bf16(q_ref[rows, :])
                do_t = do_ref[rows, :].astype(F32).T
                dot_sc[b] = do_t.astype(BF16)
                delta_sc[b] = jnp.sum(do_t * o_ref[rows, :].astype(F32).T, axis=0, keepdims=True)
            dqt_sc[...] = jnp.zeros_like(dqt_sc)

        k = jnp.concatenate([kn_ref[...], kr_ref[...]], axis=1)
        kt = _transpose_bf16(k)
        vb = v_ref[...]
        dk_sc[...] = jnp.zeros_like(dk_sc)
        dv_sc[...] = jnp.zeros_like(dv_sc)

        def step(qi, masked):
            rows = pl.ds(pl.multiple_of(qi * tb, tb), tb)
            sc_t = jnp.dot(k, qt_sc[qi], preferred_element_type=F32)
            if masked:
                sc_t = jnp.where(_diag_mask(tb, True), sc_t, -1e30)
            p_t = jnp.exp2(sc_t - lse_ref[qi])
            dp_t = jnp.dot(vb, dot_sc[qi], preferred_element_type=F32)
            ds_t = (p_t * (dp_t - delta_sc[qi])).astype(BF16)
            dv_sc[...] += jnp.dot(p_t.astype(BF16), do_ref[rows, :], preferred_element_type=F32)
            dk_sc[...] += jnp.dot(ds_t, q_ref[rows, :], preferred_element_type=F32)
            dqt_sc[qi] += jnp.dot(kt, ds_t, preferred_element_type=F32)

        step(ki, True)

        def loop_body(qi, carry):
            step(qi, False)
            return carry

        lax.fori_loop(ki + 1, nb, loop_body, 0)

        dkv_ref[:, :NOPE] = (dk_sc[:, :NOPE] * ln2).astype(BF16)
        dkv_ref[:, NOPE:] = dv_sc[...].astype(BF16)
        dkr_ref[...] = dk_sc[:, NOPE:] * ln2

        @pl.when(ki == nb - 1)
        def _():
            for b in range(nb):
                dq_ref[b * tb:(b + 1) * tb, :] = dqt_sc[b].T

    return _pcall(
        body, name=name, grid=(HEADS, nb),
        in_specs=[pl.BlockSpec((s, HEAD_PAD), lambda h, ki: (0, h)),
                  pl.BlockSpec((s, V_DIM), lambda h, ki: (0, h)),
                  pl.BlockSpec((s, V_DIM), lambda h, ki: (0, h)),
                  pl.BlockSpec((None, nb, 1, tb), lambda h, ki: (h, 0, 0, 0)),
                  pl.BlockSpec((tb, NOPE), lambda h, ki: (ki, 2 * h)),
                  pl.BlockSpec((tb, V_DIM), lambda h, ki: (ki, 2 * h + 1)),
                  pl.BlockSpec((tb, LANE), lambda h, ki: (ki, 0))],
        out_specs=[pl.BlockSpec((s, HEAD_PAD), lambda h, ki: (0, h)),
                   pl.BlockSpec((tb, HEAD_PAD), lambda h, ki: (ki, h)),
                   pl.BlockSpec((None, tb, LANE), lambda h, ki: (h, ki, 0))],
        out_shape=[jax.ShapeDtypeStruct((s, HEADS * HEAD_PAD), F32),
                   jax.ShapeDtypeStruct((s, HEADS * HEAD_PAD), BF16),
                   jax.ShapeDtypeStruct((HEADS, s, LANE), F32)],
        scratch_shapes=[pltpu.VMEM((nb, HEAD_PAD, tb), BF16), pltpu.VMEM((nb, V_DIM, tb), BF16),
                        pltpu.VMEM((nb, 1, tb), F32), pltpu.VMEM((nb, HEAD_PAD, tb), F32),
                        pltpu.VMEM((tb, HEAD_PAD), F32), pltpu.VMEM((tb, V_DIM), F32)],
        compiler_params=_params(("parallel", "arbitrary")),
    )(q_cat, do, o, lse, kv, kv, kr)


def _mla_gate_bwd(dy, o, proj, name):
    s = dy.shape[0]
    ts = _pick(s, ROW_BLK)
    zcol = PROJ_Z // V_DIM

    def body(dy_ref, o_ref, p_ref, do_ref, dz_ref):
        z = p_ref[:, PROJ_Z:]
        dyv = dy_ref[...]
        sig = _sigmoid(z)
        do_ref[...] = (dyv * (z * sig)).astype(BF16)
        dz_ref[...] = (dyv * o_ref[...].astype(F32) * (sig * (1.0 + z * (1.0 - sig)))).astype(BF16)

    blk = pl.BlockSpec((ts, MLA_WIDTH), lambda i: (i, 0))
    return _pcall(
        body, name=name, grid=(s // ts,),
        in_specs=[blk, blk, pl.BlockSpec((ts, PROJ_W), lambda i: (i, 0))],
        out_specs=[blk, blk],
        out_shape=[jax.ShapeDtypeStruct((s, MLA_WIDTH), BF16), jax.ShapeDtypeStruct((s, MLA_WIDTH), BF16)],
        compiler_params=_params(("parallel",)),
    )(dy, o, proj)


def _mla_norm_bwd(dqn, dkvn, dkr_heads, proj, gq, gkv, tabs, name):
    s = proj.shape[0]
    ts = _pick(s, ROW_BLK)
    ck, s1k, s2k = tabs

    def rms_bwd(cv, dn_in, g, n_real):
        r, n = _rms(cv, n_real)
        dg = jnp.sum(dn_in * n, axis=0, keepdims=True)
        dn = dn_in * g
        dc = r * (dn - n * (jnp.sum(dn * n, axis=-1, keepdims=True) * (1.0 / n_real)))
        return dc, dg

    def body(dqn_ref, dkvn_ref, dkr_ref, cq_ref, ckv_ref, gq_ref, gkv_ref, c_ref, s1_ref, s2_ref,
             dp_ref, dgq_ref, dgkv_ref):
        @pl.when(pl.program_id(0) == 0)
        def _():
            dgq_ref[...] = jnp.zeros_like(dgq_ref)
            dgkv_ref[...] = jnp.zeros_like(dgkv_ref)

        dcq, dgq = rms_bwd(cq_ref[...], dqn_ref[...], gq_ref[...], Q_RANK)
        dckv, dgkv = rms_bwd(ckv_ref[...], dkvn_ref[...], gkv_ref[...], KV_RANK)
        dgq_ref[...] += dgq
        dgkv_ref[...] += dgkv
        dkr = dkr_ref[0]
        for h in range(1, HEADS):
            dkr = dkr + dkr_ref[h]
        dp_ref[:, PROJ_CQ:PROJ_CKV] = dcq.astype(BF16)
        dp_ref[:, PROJ_CKV:PROJ_KR] = dckv.astype(BF16)
        dp_ref[:, PROJ_KR:PROJ_Z] = _rot(dkr, c_ref[...], -s1_ref[...], -s2_ref[...]).astype(BF16)

    tab = pl.BlockSpec((ts, LANE), lambda i: (i, 0))
    return _pcall(
        body, name=name, grid=(s // ts,),
        in_specs=[pl.BlockSpec((ts, Q_RANK_PAD), lambda i: (i, 0)),
                  pl.BlockSpec((ts, KV_RANK), lambda i: (i, 0)),
                  pl.BlockSpec((HEADS, ts, LANE), lambda i: (0, i, 0)),
                  pl.BlockSpec((ts, Q_RANK_PAD), lambda i: (i, 0)),
                  pl.BlockSpec((ts, KV_RANK), lambda i: (i, PROJ_CKV // KV_RANK)),
                  _vec_spec(Q_RANK_PAD), _vec_spec(KV_RANK), tab, tab, tab],
        out_specs=[pl.BlockSpec((ts, PROJ_Z), lambda i: (i, 0)), _vec_spec(Q_RANK_PAD), _vec_spec(KV_RANK)],
        out_shape=[jax.ShapeDtypeStruct((s, PROJ_Z), BF16), jax.ShapeDtypeStruct((1, Q_RANK_PAD), F32),
                   jax.ShapeDtypeStruct((1, KV_RANK), F32)],
        compiler_params=_params(("arbitrary",)),
    )(dqn, dkvn, dkr_heads, proj, proj, gq, gkv, ck, s1k, s2k)


def _ada_mod(cond_raw, ada_w, bias_my, name):
    nl, d, ncol = ada_w.shape
    tk = _pick(d, 512)
    nk = d // tk

    def body(c_ref, w_ref, b_ref, o_ref, acc_ref):
        kk = pl.program_id(1)

        @pl.when(kk == 0)
        def _():
            acc_ref[...] = jnp.zeros_like(acc_ref)

        cv = c_ref[...]
        cond = (cv * _sigmoid(cv)).astype(BF16)
        acc_ref[...] += jnp.dot(cond, w_ref[...].astype(BF16), preferred_element_type=F32)

        @pl.when(kk == nk - 1)
        def _():
            o_ref[...] = acc_ref[...] + b_ref[...]

    return _pcall(
        body, name=name, grid=(nl, nk),
        in_specs=[pl.BlockSpec((N_DEV, tk), lambda l, kk: (0, kk)),
                  pl.BlockSpec((None, tk, ncol), lambda l, kk: (l, kk, 0)),
                  pl.BlockSpec((None, 1, ncol), lambda l, kk: (l, 0, 0))],
        out_specs=pl.BlockSpec((None, N_DEV, ncol), lambda l, kk: (l, 0, 0)),
        out_shape=jax.ShapeDtypeStruct((nl, N_DEV, ncol), F32),
        scratch_shapes=[pltpu.VMEM((N_DEV, ncol), F32)],
        compiler_params=_params(("parallel", "arbitrary")),
    )(cond_raw, ada_w, bias_my.reshape(nl, 1, ncol))


def _adam(w, g, m, v):
    m = ADAM_B1 * m + (1.0 - ADAM_B1) * g
    v = ADAM_B2 * v + (1.0 - ADAM_B2) * (g * g)
    m_hat = m / (1.0 - ADAM_B1 ** ADAM_STEP)
    v_hat = v / (1.0 - ADAM_B2 ** ADAM_STEP)
    delta = -ADAM_LR * (m_hat / (jnp.sqrt(v_hat) + ADAM_EPS) + ADAM_WD * w)
    return delta, m, v


def _ada_bwd_adam(cond_t, dmod_cols, w, m, v, name):
    nl, d, ncol = w.shape
    tk = _pick(d, 512)

    def body(c_ref, dm_ref, w_ref, m_ref, v_ref, g_ref, d_ref, mo_ref, vo_ref):
        cv = c_ref[...]
        cond = (cv * _sigmoid(cv)).astype(BF16)
        g = jnp.dot(cond, dm_ref[...].astype(BF16), preferred_element_type=F32)
        delta, m2, v2 = _adam(w_ref[...], g, m_ref[...], v_ref[...])
        g_ref[...] = g
        d_ref[...] = delta
        mo_ref[...] = m2
        vo_ref[...] = v2

    blk = pl.BlockSpec((None, tk, ncol), lambda l, kk: (l, kk, 0))
    shp = jax.ShapeDtypeStruct((nl, d, ncol), F32)
    return _pcall(
        body, name=name, grid=(nl, d // tk),
        in_specs=[pl.BlockSpec((tk, N_DEV), lambda l, kk: (kk, 0)),
                  pl.BlockSpec((None, N_DEV, ncol), lambda l, kk: (l, 0, 0)), blk, blk, blk],
        out_specs=[blk, blk, blk, blk], out_shape=[shp, shp, shp, shp],
        compiler_params=_params(("parallel", "parallel")),
    )(cond_t, dmod_cols, w, m, v)


def _adam_reduce(recv0, recv1, w, m, v, name):
    nl, r, c = w.shape
    tr = _pick(r, 128) if r % 128 == 0 else r
    tc = _pick(c, 1024)

    def body(r0_ref, r1_ref, w_ref, m_ref, v_ref, g_ref, d_ref, mo_ref, vo_ref):
        l = pl.program_id(0)

        def run(rr):
            g = rr[0].astype(F32)
            for sidx in range(1, N_DEV):
                g = g + rr[sidx].astype(F32)
            delta, m2, v2 = _adam(w_ref[...], g, m_ref[...], v_ref[...])
            g_ref[...] = g
            d_ref[...] = delta
            mo_ref[...] = m2
            vo_ref[...] = v2

        @pl.when(l == 0)
        def _():
            run(r0_ref)

        @pl.when(l == 1)
        def _():
            run(r1_ref)

    rblk = pl.BlockSpec((N_DEV, tr, tc), lambda l, i, j: (0, i, j))
    blk = pl.BlockSpec((None, tr, tc), lambda l, i, j: (l, i, j))
    shp = jax.ShapeDtypeStruct((nl, r, c), F32)
    return _pcall(
        body, name=name, grid=(nl, r // tr, c // tc),
        in_specs=[rblk, rblk, blk, blk, blk],
        out_specs=[blk, blk, blk, blk], out_shape=[shp, shp, shp, shp],
        compiler_params=_params(("arbitrary", "parallel", "parallel")),
    )(recv0.reshape(N_DEV, r, c), recv1.reshape(N_DEV, r, c), w, m, v)


def _adam_small(gathered, w, m, v, name):
    r = w.shape[0]
    tr = _pick(r, 512) if r % 512 == 0 else r

    def body(p_ref, w_ref, m_ref, v_ref, g_ref, d_ref, mo_ref, vo_ref):
        g = p_ref[0]
        for sidx in range(1, N_DEV):
            g = g + p_ref[sidx]
        delta, m2, v2 = _adam(w_ref[...], g, m_ref[...], v_ref[...])
        g_ref[...] = g
        d_ref[...] = delta
        mo_ref[...] = m2
        vo_ref[...] = v2

    blk = pl.BlockSpec((tr, LANE), lambda i: (i, 0))
    shp = jax.ShapeDtypeStruct((r, LANE), F32)
    return _pcall(
        body, name=name, grid=(r // tr,),
        in_specs=[pl.BlockSpec((N_DEV, tr, LANE), lambda i: (0, i, 0)), blk, blk, blk],
        out_specs=[blk, blk, blk, blk], out_shape=[shp, shp, shp, shp],
        compiler_params=_params(("parallel",)),
    )(gathered, w, m, v)


def _my_place():
    x, y, c = lax.axis_index("x"), lax.axis_index("y"), lax.axis_index("c")
    return x, y, c, 4 * x + 2 * y + c


def _peer(x, y, c, k):
    px = 1 - x if (k >> 2) & 1 else x
    py = 1 - y if (k >> 1) & 1 else y
    pc = 1 - c if k & 1 else c
    return (px, py, pc), 4 * px + 2 * py + pc


def _slab(ref, shape, kind, p):
    r, cd = shape
    if kind == "row":
        return ref.at[pl.ds(pl.multiple_of(p * r, SUBLANE), r), :]
    return ref.at[:, pl.ds(pl.multiple_of(p * cd, LANE), cd)]


def _exchange_layout(arrays, kinds, gather):
    shard_shapes, dst_kinds, out_shapes = [], [], []
    for a, kind in zip(arrays, kinds):
        r, cd = a.shape
        if gather:
            shard, dst_kind = (r, cd), kind
        else:
            shard, dst_kind = ((r // N_DEV, cd) if kind == "row" else (r, cd // N_DEV)), "row"
        shard_shapes.append(shard)
        dst_kinds.append(dst_kind)
        full = (shard[0] * N_DEV, shard[1]) if dst_kind == "row" else (shard[0], shard[1] * N_DEV)
        out_shapes.append(jax.ShapeDtypeStruct(full, a.dtype))
    return shard_shapes, dst_kinds, out_shapes


def _exchange_copies(ins, outs, send_sems, recv_sems, sem_of, layout, kinds, gather):
    shard_shapes, dst_kinds, _ = layout
    x, y, c, me = _my_place()

    def src_for(a, p):
        return ins[a] if gather else _slab(ins[a], shard_shapes[a], kinds[a], p)

    def dst_slot(a, p):
        return _slab(outs[a], shard_shapes[a], dst_kinds[a], p)

    def local(a, sem):
        return pltpu.make_async_copy(src_for(a, me), dst_slot(a, me), sem)

    def remote(a, k, slot):
        peer, pidx = _peer(x, y, c, k)
        return pltpu.make_async_remote_copy(
            src_ref=src_for(a, pidx), dst_ref=dst_slot(a, me if slot == "mine" else pidx),
            send_sem=send_sems.at[sem_of(a, k)], recv_sem=recv_sems.at[sem_of(a, k)],
            device_id=peer, device_id_type=MESH)

    return local, remote


def _landing_with_own_slab(arrays, kinds, gather, layout):
    _, _, _, me = _my_place()
    lands = []
    for a in range(len(arrays)):
        (r, cd), dst_kind, full = layout[0][a], layout[1][a], layout[2][a]
        if gather:
            piece = arrays[a]
        elif kinds[a] == "row":
            piece = lax.dynamic_slice_in_dim(arrays[a], me * r, r, axis=0)
        else:
            piece = lax.dynamic_slice_in_dim(arrays[a], me * cd, cd, axis=1)
        at = (me * r, 0) if dst_kind == "row" else (0, me * cd)
        lands.append(lax.dynamic_update_slice(lax.empty(full.shape, full.dtype), piece, at))
    return lands


def _exchange(arrays, kinds, gather, name):
    n = len(arrays)
    layout = _exchange_layout(arrays, kinds, gather)
    lands = _landing_with_own_slab(arrays, kinds, gather, layout)

    def body(*refs):
        ins, outs = refs[:n], refs[2 * n:3 * n]
        send_sems, recv_sems = refs[3 * n:]
        _, remote = _exchange_copies(ins, outs, send_sems, recv_sems,
                                     lambda a, k: a * (N_DEV - 1) + k - 1, layout, kinds, gather)
        for a in range(n):
            for k in range(1, N_DEV):
                remote(a, k, "mine").start()
        for a in range(n):
            for k in range(1, N_DEV):
                arrival = remote(a, k, "theirs")
                arrival.wait_send()
                arrival.wait_recv()

    anyspec = pl.BlockSpec(memory_space=pl.ANY)
    outs = _pcall(
        body, comm=True, name=name,
        in_specs=[anyspec] * (2 * n), out_specs=[anyspec] * n, out_shape=layout[2],
        input_output_aliases={n + a: a for a in range(n)},
        scratch_shapes=[pltpu.SemaphoreType.DMA((n * (N_DEV - 1),)), pltpu.SemaphoreType.DMA((n * (N_DEV - 1),))],
    )(*arrays, *lands)
    return list(outs)


HBM_SPEC = pl.BlockSpec(memory_space=pltpu.HBM)
SEM_SPEC = pl.BlockSpec(memory_space=pltpu.SEMAPHORE)
ANY_SPEC = pl.BlockSpec(memory_space=pl.ANY)
DATAFLOW = pltpu.SideEffectType.DATAFLOW_SIDE_EFFECTING


def _exchange_start(arrays, kinds, gather, name, after, carry=()):
    n, nc = len(arrays), len(carry)
    layout = _exchange_layout(arrays, kinds, gather)
    lands = _landing_with_own_slab(arrays, kinds, gather, layout)

    def body(*refs):
        ins, outs = refs[:n], refs[n:2 * n]
        send_sems, recv_sems = refs[2 * n + nc + 1], refs[2 * n + nc + 2]
        token = refs[2 * n + nc + 3 + 2 * n + nc]
        _, remote = _exchange_copies(ins, outs, send_sems, recv_sems, lambda a, k: a, layout, kinds, gather)
        for a in range(n):
            for k in range(1, N_DEV):
                remote(a, k, "mine").start()
        token[...] = jnp.zeros_like(token)

    passed = list(arrays) + lands + list(carry)
    res = pl.pallas_call(
        body, name=name,
        out_shape=(pltpu.SemaphoreType.DMA((n,)), pltpu.SemaphoreType.DMA((n,)),
                   *[pltpu.HBM(t.shape, t.dtype) for t in passed], jax.ShapeDtypeStruct((SUBLANE, LANE), F32)),
        in_specs=[HBM_SPEC] * (2 * n + nc) + [ANY_SPEC],
        out_specs=(SEM_SPEC, SEM_SPEC, *([HBM_SPEC] * (2 * n + nc)), pl.BlockSpec(memory_space=pltpu.VMEM)),
        input_output_aliases={i: 2 + i for i in range(2 * n + nc)},
        compiler_params=pltpu.CompilerParams(has_side_effects=DATAFLOW),
    )(*[pltpu.with_memory_space_constraint(t, pltpu.HBM) for t in passed], after)
    handle = (res[0], res[1], list(res[2:2 + n]), list(res[2 + n:2 + 2 * n]), tuple(kinds), gather)
    return handle, res[-1], list(res[2 + 2 * n:2 + 2 * n + nc])


def _exchange_wait(handle, name, after):
    send_sems, recv_sems, ins_thru, lands_thru, kinds, gather = handle
    n = len(ins_thru)
    layout = _exchange_layout(ins_thru, kinds, gather)

    def body(*refs):
        ins, outs = refs[:n], refs[n:2 * n]
        s_sems, r_sems = refs[2 * n], refs[2 * n + 1]
        _, remote = _exchange_copies(ins, outs, s_sems, r_sems, lambda a, k: a, layout, kinds, gather)
        for a in range(n):
            for k in range(1, N_DEV):
                arrival = remote(a, k, "theirs")
                arrival.wait_send()
                arrival.wait_recv()

    res = pl.pallas_call(
        body, name=name,
        out_shape=[pltpu.HBM(t.shape, t.dtype) for t in ins_thru + lands_thru],
        in_specs=[HBM_SPEC] * (2 * n) + [SEM_SPEC, SEM_SPEC, ANY_SPEC],
        out_specs=[HBM_SPEC] * (2 * n),
        input_output_aliases={i: i for i in range(2 * n)},
        compiler_params=pltpu.CompilerParams(has_side_effects=DATAFLOW),
    )(*ins_thru, *lands_thru, send_sems, recv_sems, after)
    return list(res[n:2 * n])


def _pad_mla_w_in(w):
    d = w.shape[0]
    z = lambda n: jnp.zeros((d, n), w.dtype)
    o1, o2, o3 = Q_RANK, Q_RANK + KV_RANK, Q_RANK + KV_RANK + ROPE
    return jnp.concatenate([w[:, :o1], z(Q_RANK_PAD - Q_RANK), w[:, o1:o2], w[:, o2:o3], z(LANE - ROPE), w[:, o3:]], axis=1)


def _unpad_mla_w_in(g):
    return jnp.concatenate([g[:, :Q_RANK], g[:, PROJ_CKV:PROJ_KR], g[:, PROJ_KR:PROJ_KR + ROPE], g[:, PROJ_Z:]], axis=1)


def _pad_w_uq(w):
    w3 = w.reshape(Q_RANK, HEADS, NOPE + ROPE)
    w3 = jnp.pad(w3, ((0, Q_RANK_PAD - Q_RANK), (0, 0), (0, HEAD_PAD - NOPE - ROPE)))
    return w3.reshape(Q_RANK_PAD, HEADS * HEAD_PAD)


def _unpad_w_uq(g):
    return g[:Q_RANK].reshape(Q_RANK, HEADS, HEAD_PAD)[:, :, :NOPE + ROPE].reshape(Q_RANK, HEADS * (NOPE + ROPE))


def _pack(pieces):
    flat = [p.reshape(-1).astype(F32) for p in pieces]
    tot = sum(f.shape[0] for f in flat)
    unit = SUBLANE * LANE
    padn = (-tot) % unit
    if padn:
        flat.append(jnp.zeros((padn,), F32))
    return jnp.concatenate(flat).reshape(-1, LANE)


def _unpack(packed, shapes):
    flat = packed.reshape(-1)
    out, off = [], 0
    for shp in shapes:
        nel = int(np.prod(shp))
        out.append(flat[off:off + nel].reshape(shp))
        off += nel
    return out


WEIGHTS = ['ada_w', 'ada_b', 'pre_g', 'post_g', 'sgu_w_in', 'sgu_norm_g', 'sgu_w_s', 'sgu_b_s', 'sgu_w_out',
           'mla_w_in', 'mla_q_norm_g', 'mla_kv_norm_g', 'mla_w_uq', 'mla_w_ukv', 'mla_w_out']
INPUTS = ['x', 'c'] + WEIGHTS + ['loss_target'] + ['m_' + n for n in WEIGHTS] + ['v_' + n for n in WEIGHTS]


def kernel(x, c, ada_w, ada_b, pre_g, post_g, sgu_w_in, sgu_norm_g, sgu_w_s, sgu_b_s, sgu_w_out, mla_w_in, mla_q_norm_g, mla_kv_norm_g, mla_w_uq, mla_w_ukv, mla_w_out, loss_target, m_ada_w, m_ada_b, m_pre_g, m_post_g, m_sgu_w_in, m_sgu_norm_g, m_sgu_w_s, m_sgu_b_s, m_sgu_w_out, m_mla_w_in, m_mla_q_norm_g, m_mla_kv_norm_g, m_mla_w_uq, m_mla_w_ukv, m_mla_w_out, v_ada_w, v_ada_b, v_pre_g, v_post_g, v_sgu_w_in, v_sgu_norm_g, v_sgu_w_s, v_sgu_b_s, v_sgu_w_out, v_mla_w_in, v_mla_q_norm_g, v_mla_kv_norm_g, v_mla_w_uq, v_mla_w_ukv, v_mla_w_out):
    given = locals()
    A = {name: given[name] for name in INPUTS}
    x0 = A['x'][0]
    tgt = A['loss_target'][0]
    s, d = x0.shape
    e = 2 * d
    ncol = 3 * d // N_DEV
    _, _, _, me = _my_place()
    ktabs, qtabs = _rope_tables(s)

    gains = jnp.zeros((SUBLANE, LANE), F32)
    gains = gains.at[0:2, :Q_RANK // N_DEV].set(A['mla_q_norm_g'])
    gains = gains.at[2:4, :KV_RANK // N_DEV].set(A['mla_kv_norm_g'])
    c8 = jnp.broadcast_to(A['c'], (SUBLANE, d))
    cg, gg = _exchange([c8, gains], ["row", "row"], True, "ag_cond")
    cond_raw = cg.reshape(N_DEV, SUBLANE, d)[:, 0, :]
    gg = gg.reshape(N_DEV, SUBLANE, LANE)
    gq_full = jnp.transpose(gg[:, 0:2, :Q_RANK // N_DEV], (1, 0, 2)).reshape(N_MIX, Q_RANK)
    gkv_full = jnp.transpose(gg[:, 2:4, :KV_RANK // N_DEV], (1, 0, 2)).reshape(N_MIX, KV_RANK)
    gq_pad = jnp.pad(gq_full, ((0, 0), (0, Q_RANK_PAD - Q_RANK)))

    bias_my = lax.dynamic_slice_in_dim(A['ada_b'], me * ncol, ncol, axis=1)
    mod_part = _ada_mod(cond_raw, A['ada_w'], bias_my, "ada_mod")
    send = jnp.pad(jnp.transpose(mod_part, (1, 0, 2)), ((0, 0), (0, SUBLANE - DEPTH), (0, 0)))
    (rb,) = _exchange([send.reshape(N_DEV * SUBLANE, ncol)], ["row"], False, "a2a_mod")

    token = rb
    gathers = []
    for i in range(DEPTH):
        j = i // N_MIX
        if i % N_MIX == 0:
            shards, kinds = [A['sgu_w_in'][j], A['sgu_w_out'][j]], ["col", "row"]
        else:
            shards = [A['mla_w_in'][j], A['mla_w_uq'][j], A['mla_w_ukv'][j], A['mla_w_out'][j]]
            kinds = ["col", "col", "col", "row"]
        handle, token, _ = _exchange_start([t.astype(BF16) for t in shards], kinds, True, f"ag_start_{i}", token)
        gathers.append(handle)

    mod = jnp.transpose(rb.reshape(N_DEV, SUBLANE, ncol)[:, :DEPTH, :], (1, 0, 2)).reshape(DEPTH, 3 * d) + token[0, 0]
    shift = [mod[i:i + 1, :d] for i in range(DEPTH)]
    scale = [mod[i:i + 1, d:2 * d] for i in range(DEPTH)]
    gate = [mod[i:i + 1, 2 * d:] for i in range(DEPTH)]

    saved = []
    x = x0
    for i in range(DEPTH):
        j = i // N_MIX
        h = _pre_fwd(x, A['pre_g'][i:i + 1], scale[i], shift[i], f"pre_fwd")
        if i % N_MIX == 0:
            w_in, w_out = _exchange_wait(gathers[i], f"ag_wait_{i}", h)
            uvz = _mm(h, w_in, "nn", F32, "sgu_in")
            bias_full = jnp.repeat(A['sgu_b_s'][j].T, e // SGU_GROUPS, axis=1)
            ng = A['sgu_norm_g'][j:j + 1]
            y = _sgu_mid_fwd(uvz, ng, A['sgu_w_s'][j], bias_full, "sgu_mid_fwd")
            out = _mm(y, w_out, "nn", F32, "sgu_out")
            saved.append(dict(x=x, h=h, uvz=uvz, y=y, out=out, w_in=w_in, w_out=w_out, bias=bias_full, ng=ng))
        else:
            w_in, w_uq, w_ukv, w_out = _exchange_wait(gathers[i], f"ag_wait_{i}", h)
            w_in = _pad_mla_w_in(w_in)
            w_uq = _pad_w_uq(w_uq)
            gq, gkv = gq_pad[j:j + 1], gkv_full[j:j + 1]
            proj = _mm(h, w_in, "nn", F32, "mla_in")
            qn, kvn, kr = _mla_norm_fwd(proj, gq, gkv, ktabs, "mla_norm_fwd")
            q = _mm(qn, w_uq, "nn", F32, "mla_uq")
            q_cat = _rope_heads(q, qtabs, 1.0, SOFTMAX_SCALE * LOG2_E, BF16, "rope_q_fwd")
            kv = _mm(kvn, w_ukv, "nn", BF16, "mla_ukv")
            o, y, lse = _attn_fwd(q_cat, kv, kr, proj, "attn_fwd")
            out = _mm(y, w_out, "nn", F32, "mla_out")
            saved.append(dict(x=x, h=h, proj=proj, qn=qn, kvn=kvn, kr=kr, q_cat=q_cat, kv=kv, o=o, y=y, lse=lse,
                              out=out, w_in=w_in, w_uq=w_uq, w_ukv=w_ukv, w_out=w_out, gq=gq, gkv=gkv))
        x = _post_fwd(x, out, gate[i], A['post_g'][i:i + 1], "post_fwd")

    dx, loss_row = _loss_head(x, tgt, "loss_head")
    loss = lax.psum(loss_row[0, 0], ("x", "y", "c"))

    d_shift, d_scale, d_gate = [None] * DEPTH, [None] * DEPTH, [None] * DEPTH
    d_pre, d_post = [None] * DEPTH, [None] * DEPTH
    scatters = [None] * DEPTH
    small = {}
    for i in reversed(range(DEPTH)):
        j = i // N_MIX
        sv = saved[i]
        dout, d_gate[i], d_post[i] = _post_bwd(dx, sv['out'], gate[i], A['post_g'][i:i + 1], "post_bwd")
        if i % N_MIX == 0:
            dy = _mm(dout, sv['w_out'], "nt", F32, "sgu_out_dx")
            g_w_out = _mm(sv['y'], dout, "tn", BF16, "sgu_out_dw")
            duvz, dws, dbs, dng = _sgu_mid_bwd(sv['uvz'], dy, sv['ng'], A['sgu_w_s'][j], sv['bias'], "sgu_mid_bwd")
            g_w_in = _mm(sv['h'], duvz, "tn", BF16, "sgu_in_dw")
            scatters[i], token, (duvz,) = _exchange_start([g_w_in, g_w_out], ["col", "row"], False, f"rs_start_{i}",
                                                          token, carry=[duvz])
            dh = _mm(duvz, sv['w_in'], "nt", F32, "sgu_in_dx")
            small[('sgu', j)] = (dws, dbs.reshape(SGU_GROUPS, SGU_BLOCK), dng)
        else:
            dy = _mm(dout, sv['w_out'], "nt", F32, "mla_out_dx")
            g_w_out = _mm(sv['y'], dout, "tn", BF16, "mla_out_dw")
            do, dz = _mla_gate_bwd(dy, sv['o'], sv['proj'], "mla_gate_bwd")
            dq, dkv, dkr_heads = _attn_bwd(sv['q_cat'], sv['kv'], sv['kr'], do, sv['o'], sv['lse'], "attn_bwd")
            dq_b = _rope_heads(dq, qtabs, -1.0, SOFTMAX_SCALE, BF16, "rope_q_bwd")
            dqn = _mm(dq_b, sv['w_uq'], "nt", F32, "mla_uq_dx")
            g_w_uq = _unpad_w_uq(_mm(sv['qn'], dq_b, "tn", BF16, "mla_uq_dw"))
            dkvn = _mm(dkv, sv['w_ukv'], "nt", F32, "mla_ukv_dx")
            g_w_ukv = _mm(sv['kvn'], dkv, "tn", BF16, "mla_ukv_dw")
            dpa, dgq, dgkv = _mla_norm_bwd(dqn, dkvn, dkr_heads, sv['proj'], sv['gq'], sv['gkv'], ktabs, "mla_norm_bwd")
            dproj = jnp.concatenate([dpa, dz], axis=1)
            g_w_in = _unpad_mla_w_in(_mm(sv['h'], dproj, "tn", BF16, "mla_in_dw"))
            scatters[i], token, (dproj,) = _exchange_start(
                [g_w_in, g_w_uq, g_w_ukv, g_w_out], ["col", "col", "col", "row"], False, f"rs_start_{i}", token,
                carry=[dproj])
            dh = _mm(dproj, sv['w_in'], "nt", F32, "mla_in_dx")
            small[('mla', j)] = (dgq[:, :Q_RANK], dgkv)
        dx, d_shift[i], d_scale[i], d_pre[i] = _pre_bwd(dh, sv['x'], dx, A['pre_g'][i:i + 1], scale[i], "pre_bwd")

    res = {}

    def big(name, recv0, recv1):
        res[name] = _adam_reduce(recv0, recv1, A[name], A['m_' + name], A['v_' + name], "adam_" + name)

    recv_mla = {j: _exchange_wait(scatters[N_MIX * j + 1], f"rs_wait_{N_MIX * j + 1}", dx) for j in reversed(range(N_MIX))}
    for idx_w, name in enumerate(['mla_w_in', 'mla_w_uq', 'mla_w_ukv', 'mla_w_out']):
        big(name, recv_mla[0][idx_w], recv_mla[1][idx_w])

    dmod = jnp.concatenate([jnp.concatenate([d_shift[i], d_scale[i], d_gate[i]], axis=1) for i in range(DEPTH)], axis=0)
    pieces = [dmod, jnp.concatenate(d_pre, axis=0), jnp.concatenate(d_post, axis=0),
              jnp.concatenate([small[('sgu', j)][2] for j in range(N_MIX)], axis=0),
              jnp.stack([small[('sgu', j)][0] for j in range(N_MIX)]),
              jnp.stack([small[('sgu', j)][1] for j in range(N_MIX)]),
              jnp.concatenate([small[('mla', j)][0] for j in range(N_MIX)], axis=0),
              jnp.concatenate([small[('mla', j)][1] for j in range(N_MIX)], axis=0)]
    small_names = ['ada_b', 'pre_g', 'post_g', 'sgu_norm_g', 'sgu_w_s', 'sgu_b_s', 'mla_q_norm_g', 'mla_kv_norm_g']
    full_shapes = [p.shape for p in pieces]
    packed = _pack(pieces)
    rows = packed.shape[0]
    (gathered,) = _exchange([packed], ["row"], True, "ag_small")
    gathered = gathered.reshape(N_DEV, rows, LANE)

    def full_size(prefix):
        out = []
        for name, shp in zip(small_names, full_shapes):
            t = A[prefix + name]
            if t.shape != tuple(shp):
                t = lax.dynamic_update_slice_in_dim(jnp.zeros(shp, F32), t, me * t.shape[1], axis=1)
            out.append(t)
        return _pack(out)

    sm = _adam_small(gathered, full_size(''), full_size('m_'), full_size('v_'), "adam_small")
    sm = [_unpack(t, full_shapes) for t in sm]
    for k_out in range(4):
        for name, val in zip(small_names, sm[k_out]):
            t = A[name]
            if t.shape != val.shape:
                val = lax.dynamic_slice_in_dim(val, me * t.shape[1], t.shape[1], axis=1)
            res.setdefault(name, [None] * 4)[k_out] = val

    nrow_dmod = DEPTH * 3 * d // LANE
    dmod_all = gathered[:, :nrow_dmod, :].reshape(N_DEV, DEPTH, 3 * d)
    dmod_cols = jnp.transpose(lax.dynamic_slice_in_dim(dmod_all, me * ncol, ncol, axis=2), (1, 0, 2))
    res['ada_w'] = _ada_bwd_adam(jnp.transpose(cond_raw), dmod_cols, A['ada_w'], A['m_ada_w'], A['v_ada_w'], "ada_bwd")

    recv_sgu = {j: _exchange_wait(scatters[N_MIX * j], f"rs_wait_{N_MIX * j}", res['ada_w'][0]) for j in reversed(range(N_MIX))}
    for idx_w, name in enumerate(['sgu_w_in', 'sgu_w_out']):
        big(name, recv_sgu[0][idx_w], recv_sgu[1][idx_w])

    outs = [loss, dx[None]]
    for k_out in range(4):
        outs += [res[n][k_out] for n in WEIGHTS]
    return tuple(outs)
```

```python
import functools
import math

import numpy as np
import jax
import jax.numpy as jnp
from jax import lax
from jax.experimental import pallas as pl
from jax.experimental.pallas import tpu as pltpu

F32 = jnp.float32
BF16 = jnp.bfloat16
MESH = pl.DeviceIdType.MESH

N_DEV = 8
DEPTH = 4
N_MIX = 2
NORM_EPS = 1e-6
CHUNK = 64
SGU_BLOCK = 128
SGU_GROUPS = 16
HEADS = 16
Q_RANK = 448
Q_RANK_PAD = 512
KV_RANK = 512
NOPE = 128
ROPE = 64
HALF = ROPE // 2
V_DIM = 128
HEAD_PAD = 256
ROPE_THETA = 10000.0
MLA_WIDTH = HEADS * V_DIM
LANE = 128
SUBLANE = 8
PROJ_CQ = 0
PROJ_CKV = Q_RANK_PAD
PROJ_KR = Q_RANK_PAD + KV_RANK
PROJ_Z = PROJ_KR + LANE
PROJ_W = PROJ_Z + MLA_WIDTH

ADAM_LR = 0.001
ADAM_B1 = 0.9
ADAM_B2 = 0.999
ADAM_EPS = 1e-08
ADAM_WD = 0.01
ADAM_STEP = 10

VMEM_LIMIT = 56 * 1024 * 1024
ATT_BLK = 512
ATT_SUB = 128
ROW_BLK = 256
MM_TM, MM_TN, MM_TK = 1024, 1024, 2048
MM_TILE_BYTES = 40 * 1024 * 1024
SOFTMAX_SCALE = (NOPE + ROPE) ** -0.5
LOG2_E = 1.0 / math.log(2.0)
INV_SQRT2 = 1.0 / math.sqrt(2.0)
INV_SQRT_2PI = 1.0 / math.sqrt(2.0 * math.pi)


def _pcall(body, comm=False, **kw):
    return pl.pallas_call(body, **kw)


def _params(sem=None):
    return pltpu.CompilerParams(dimension_semantics=sem, vmem_limit_bytes=VMEM_LIMIT)


def _pick(dim, pref):
    if dim <= pref:
        return dim
    t = (pref // LANE) * LANE
    while t >= LANE:
        if dim % t == 0:
            return t
        t -= LANE
    return dim


def _gelu(x):
    return 0.5 * x * (1.0 + lax.erf(x * INV_SQRT2))


def _gelu_grad(x):
    return 0.5 * (1.0 + lax.erf(x * INV_SQRT2)) + x * jnp.exp(-0.5 * x * x) * INV_SQRT_2PI


def _sigmoid(x):
    return 1.0 / (1.0 + jnp.exp(-x))


def _dot_nt(a, b):
    return lax.dot_general(a, b, (((1,), (1,)), ((), ())), preferred_element_type=F32)


def _dot_tn(a, b):
    return lax.dot_general(a, b, (((0,), (0,)), ((), ())), preferred_element_type=F32)


def _mm(a, b, dims, out_dtype, name):
    if dims == "nn":
        (m, k), (k2, n) = a.shape, b.shape
    elif dims == "nt":
        (m, k), (n, k2) = a.shape, b.shape
    else:
        (k, m), (k2, n) = a.shape, b.shape
    assert k == k2, (a.shape, b.shape, dims)
    tm, tn = _pick(m, MM_TM), _pick(n, MM_TN)
    out_bytes = 2 * tm * tn * jnp.dtype(out_dtype).itemsize
    whole_k = 2 * (tm + tn) * k * a.dtype.itemsize + out_bytes <= MM_TILE_BYTES
    tk = k if whole_k else _pick(k, MM_TK)
    nk = k // tk

    def body(a_ref, b_ref, o_ref, *scratch):
        if dims == "nn":
            p = jnp.dot(a_ref[...], b_ref[...], preferred_element_type=F32)
        elif dims == "nt":
            p = _dot_nt(a_ref[...], b_ref[...])
        else:
            p = _dot_tn(a_ref[...], b_ref[...])
        if nk == 1:
            o_ref[...] = p.astype(o_ref.dtype)
            return
        acc_ref, = scratch
        kk = pl.program_id(2)

        @pl.when(kk == 0)
        def _():
            acc_ref[...] = p

        @pl.when(kk > 0)
        def _():
            acc_ref[...] += p

        @pl.when(kk == nk - 1)
        def _():
            o_ref[...] = acc_ref[...].astype(o_ref.dtype)

    if dims == "tn":
        a_spec = pl.BlockSpec((tk, tm), lambda i, j, kk: (kk, i))
    else:
        a_spec = pl.BlockSpec((tm, tk), lambda i, j, kk: (i, kk))
    if dims == "nt":
        b_spec = pl.BlockSpec((tn, tk), lambda i, j, kk: (j, kk))
    else:
        b_spec = pl.BlockSpec((tk, tn), lambda i, j, kk: (kk, j))
    return _pcall(
        body, name=name,
        grid=(m // tm, n // tn, nk),
        in_specs=[a_spec, b_spec],
        out_specs=pl.BlockSpec((tm, tn), lambda i, j, kk: (i, j)),
        out_shape=jax.ShapeDtypeStruct((m, n), out_dtype),
        scratch_shapes=[pltpu.VMEM((tm, tn), F32)] if nk > 1 else [],
        compiler_params=_params(("parallel", "parallel", "arbitrary")),
    )(a, b)


def _row_spec(ts, d):
    return pl.BlockSpec((ts, d), lambda i: (i, 0))


def _vec_spec(d):
    return pl.BlockSpec((1, d), lambda i: (0, 0))


def _pre_fwd(x, g, scale, shift, name):
    s, d = x.shape
    ts = _pick(s, ROW_BLK)

    def body(x_ref, g_ref, sc_ref, sh_ref, h_ref):
        xv = x_ref[...]
        r = lax.rsqrt(jnp.mean(xv * xv, axis=-1, keepdims=True) + NORM_EPS)
        h_ref[...] = ((xv * r * g_ref[...]) * (1.0 + sc_ref[...]) + sh_ref[...]).astype(BF16)

    return _pcall(
        body, name=name, grid=(s // ts,),
        in_specs=[_row_spec(ts, d), _vec_spec(d), _vec_spec(d), _vec_spec(d)],
        out_specs=_row_spec(ts, d),
        out_shape=jax.ShapeDtypeStruct((s, d), BF16),
        compiler_params=_params(("parallel",)),
    )(x, g, scale, shift)


def _post_fwd(x, out, gate, g, name):
    s, d = x.shape
    ts = _pick(s, ROW_BLK)

    def body(x_ref, o_ref, gate_ref, g_ref, y_ref):
        o = o_ref[...]
        r = lax.rsqrt(jnp.mean(o * o, axis=-1, keepdims=True) + NORM_EPS)
        y_ref[...] = x_ref[...] + gate_ref[...] * (o * r * g_ref[...])

    return _pcall(
        body, name=name, grid=(s // ts,),
        in_specs=[_row_spec(ts, d), _row_spec(ts, d), _vec_spec(d), _vec_spec(d)],
        out_specs=_row_spec(ts, d),
        out_shape=jax.ShapeDtypeStruct((s, d), F32),
        compiler_params=_params(("parallel",)),
    )(x, out, gate, g)


def _loss_head(xf, tgt, name):
    s, d = xf.shape
    ts = _pick(s, ROW_BLK)
    ns = s // ts

    def body(x_ref, t_ref, dx_ref, loss_ref, acc_ref):
        i = pl.program_id(0)

        @pl.when(i == 0)
        def _():
            acc_ref[...] = jnp.zeros_like(acc_ref)

        e = x_ref[...] - t_ref[...]
        dx_ref[...] = e * (1.0 / d)
        acc_ref[...] += jnp.sum(e * e, axis=0, keepdims=True)

        @pl.when(i == ns - 1)
        def _():
            tot = jnp.sum(acc_ref[...], axis=1, keepdims=True) * (0.5 / d)
            loss_ref[...] = jnp.broadcast_to(tot, loss_ref.shape)

    return _pcall(
        body, name=name, grid=(ns,),
        in_specs=[_row_spec(ts, d), _row_spec(ts, d)],
        out_specs=[_row_spec(ts, d), pl.BlockSpec((1, LANE), lambda i: (0, 0))],
        out_shape=[jax.ShapeDtypeStruct((s, d), F32), jax.ShapeDtypeStruct((1, LANE), F32)],
        scratch_shapes=[pltpu.VMEM((1, d), F32)],
        compiler_params=_params(("arbitrary",)),
    )(xf, tgt)


def _post_bwd(dxo, out, gate, g, name):
    s, d = dxo.shape
    ts = _pick(s, ROW_BLK)

    def body(dx_ref, o_ref, gate_ref, g_ref, do_ref, dgate_ref, dg_ref):
        i = pl.program_id(0)

        @pl.when(i == 0)
        def _():
            dgate_ref[...] = jnp.zeros_like(dgate_ref)
            dg_ref[...] = jnp.zeros_like(dg_ref)

        o = o_ref[...]
        dx = dx_ref[...]
        gv = g_ref[...]
        r = lax.rsqrt(jnp.mean(o * o, axis=-1, keepdims=True) + NORM_EPS)
        n = o * r
        dyn = dx * gate_ref[...]
        dgate_ref[...] += jnp.sum(dx * (n * gv), axis=0, keepdims=True)
        dg_ref[...] += jnp.sum(dyn * n, axis=0, keepdims=True)
        dn = dyn * gv
        do_ref[...] = (r * (dn - n * jnp.mean(dn * n, axis=-1, keepdims=True))).astype(BF16)

    return _pcall(
        body, name=name, grid=(s // ts,),
        in_specs=[_row_spec(ts, d), _row_spec(ts, d), _vec_spec(d), _vec_spec(d)],
        out_specs=[_row_spec(ts, d), _vec_spec(d), _vec_spec(d)],
        out_shape=[jax.ShapeDtypeStruct((s, d), BF16), jax.ShapeDtypeStruct((1, d), F32),
                   jax.ShapeDtypeStruct((1, d), F32)],
        compiler_params=_params(("arbitrary",)),
    )(dxo, out, gate, g)


def _pre_bwd(dh, x, dxo, g, scale, name):
    s, d = x.shape
    ts = _pick(s, ROW_BLK)

    def body(dh_ref, x_ref, dxo_ref, g_ref, sc_ref, dx_ref, dsh_ref, dsc_ref, dg_ref):
        i = pl.program_id(0)

        @pl.when(i == 0)
        def _():
            dsh_ref[...] = jnp.zeros_like(dsh_ref)
            dsc_ref[...] = jnp.zeros_like(dsc_ref)
            dg_ref[...] = jnp.zeros_like(dg_ref)

        xv = x_ref[...]
        dhv = dh_ref[...]
        gv = g_ref[...]
        one_sc = 1.0 + sc_ref[...]
        r = lax.rsqrt(jnp.mean(xv * xv, axis=-1, keepdims=True) + NORM_EPS)
        n = xv * r
        dsh_ref[...] += jnp.sum(dhv, axis=0, keepdims=True)
        dsc_ref[...] += jnp.sum(dhv * (n * gv), axis=0, keepdims=True)
        dng = dhv * one_sc
        dg_ref[...] += jnp.sum(dng * n, axis=0, keepdims=True)
        dn = dng * gv
        dx_ref[...] = dxo_ref[...] + r * (dn - n * jnp.mean(dn * n, axis=-1, keepdims=True))

    return _pcall(
        body, name=name, grid=(s // ts,),
        in_specs=[_row_spec(ts, d), _row_spec(ts, d), _row_spec(ts, d), _vec_spec(d), _vec_spec(d)],
        out_specs=[_row_spec(ts, d), _vec_spec(d), _vec_spec(d), _vec_spec(d)],
        out_shape=[jax.ShapeDtypeStruct((s, d), F32)] + [jax.ShapeDtypeStruct((1, d), F32)] * 3,
        compiler_params=_params(("arbitrary",)),
    )(dh, x, dxo, g, scale)


def _sgu_mask():
    t = lax.broadcasted_iota(jnp.int32, (SGU_BLOCK, SGU_BLOCK), 0) // CHUNK
    s = lax.broadcasted_iota(jnp.int32, (SGU_BLOCK, SGU_BLOCK), 1) // CHUNK
    return s <= t


def _sgu_norm(v_pre, g):
    e = v_pre.shape[-1]
    vg = _gelu(v_pre)
    mu = jnp.sum(vg, axis=-1, keepdims=True) * (1.0 / e)
    dlt = vg - mu
    var = jnp.sum(dlt * dlt, axis=-1, keepdims=True) * (1.0 / e)
    rstd = lax.rsqrt(var + NORM_EPS)
    vhat = dlt * rstd
    return vhat, rstd, (vhat * g).astype(BF16)


def _sgu_mid_fwd(uvz, norm_g, w_s, bias_full, name):
    s, e3 = uvz.shape
    e = e3 // 3
    gd = e // SGU_GROUPS
    nb = s // SGU_BLOCK

    def body(uvz_ref, g_ref, w_ref, b_ref, y_ref, wsc):
        @pl.when(pl.program_id(0) == 0)
        def _():
            msk = _sgu_mask()
            for gi in range(SGU_GROUPS):
                wsc[gi] = jnp.where(msk, w_ref[gi], 0.0).astype(BF16)

        _, _, vb = _sgu_norm(uvz_ref[:, e:2 * e], g_ref[...])
        for gi in range(SGU_GROUPS):
            lo = gi * gd
            vm = jnp.dot(wsc[gi], vb[:, lo:lo + gd], preferred_element_type=F32) + b_ref[:, lo:lo + gd]
            zg = uvz_ref[:, 2 * e + lo:2 * e + lo + gd]
            y_ref[:, lo:lo + gd] = (_gelu(uvz_ref[:, lo:lo + gd]) * vm * (zg * _sigmoid(zg))).astype(BF16)

    return _pcall(
        body, name=name, grid=(nb,),
        in_specs=[pl.BlockSpec((SGU_BLOCK, e3), lambda n: (n, 0)),
                  pl.BlockSpec((1, e), lambda n: (0, 0)),
                  pl.BlockSpec((SGU_GROUPS, SGU_BLOCK, SGU_BLOCK), lambda n: (0, 0, 0)),
                  pl.BlockSpec((SGU_BLOCK, e), lambda n: (0, 0))],
        out_specs=pl.BlockSpec((SGU_BLOCK, e), lambda n: (n, 0)),
        out_shape=jax.ShapeDtypeStruct((s, e), BF16),
        scratch_shapes=[pltpu.VMEM((SGU_GROUPS, SGU_BLOCK, SGU_BLOCK), BF16)],
        compiler_params=_params(("arbitrary",)),
    )(uvz, norm_g, w_s, bias_full)


def _sgu_mid_bwd(uvz, dy, norm_g, w_s, bias_full, name):
    s, e3 = uvz.shape
    e = e3 // 3
    gd = e // SGU_GROUPS
    nb = s // SGU_BLOCK

    def body(uvz_ref, dy_ref, g_ref, w_ref, b_ref, d_ref, dw_ref, db_ref, dg_ref, wsc, wtsc, dvh_sc, dbacc):
        n = pl.program_id(0)

        @pl.when(n == 0)
        def _():
            msk = _sgu_mask()
            for gi in range(SGU_GROUPS):
                wm = jnp.where(msk, w_ref[gi], 0.0)
                wsc[gi] = wm.astype(BF16)
                wtsc[gi] = wm.T.astype(BF16)
            dw_ref[...] = jnp.zeros_like(dw_ref)
            dg_ref[...] = jnp.zeros_like(dg_ref)
            dbacc[...] = jnp.zeros_like(dbacc)

        v_pre = uvz_ref[:, e:2 * e]
        gv = g_ref[...]
        vhat, rstd, vb = _sgu_norm(v_pre, gv)
        s1 = jnp.zeros((SGU_BLOCK, 1), F32)
        s2 = jnp.zeros((SGU_BLOCK, 1), F32)
        for gi in range(SGU_GROUPS):
            lo = gi * gd
            u_pre = uvz_ref[:, lo:lo + gd]
            zg = uvz_ref[:, 2 * e + lo:2 * e + lo + gd]
            dyg = dy_ref[:, lo:lo + gd]
            ug = _gelu(u_pre)
            sig = _sigmoid(zg)
            vbg = vb[:, lo:lo + gd]
            vhg = vhat[:, lo:lo + gd]
            vm = jnp.dot(wsc[gi], vbg, preferred_element_type=F32) + b_ref[:, lo:lo + gd]
            t = dyg * (zg * sig)
            d_ref[:, lo:lo + gd] = (t * vm * _gelu_grad(u_pre)).astype(BF16)
            dvm = t * ug
            d_ref[:, 2 * e + lo:2 * e + lo + gd] = (dyg * ug * vm * (sig * (1.0 + zg * (1.0 - sig)))).astype(BF16)
            dvm_b = dvm.astype(BF16)
            dv = jnp.dot(wtsc[gi], dvm_b, preferred_element_type=F32)
            dw_ref[gi] += _dot_nt(dvm_b, vbg)
            dbacc[:, lo:lo + gd] += dvm
            dg_ref[:, lo:lo + gd] += jnp.sum(dv * vhg, axis=0, keepdims=True)
            dvh = dv * gv[:, lo:lo + gd]
            dvh_sc[:, lo:lo + gd] = dvh
            s1 = s1 + jnp.sum(dvh, axis=-1, keepdims=True)
            s2 = s2 + jnp.sum(dvh * vhg, axis=-1, keepdims=True)
        dvg = rstd * (dvh_sc[...] - s1 * (1.0 / e) - vhat * (s2 * (1.0 / e)))
        d_ref[:, e:2 * e] = (dvg * _gelu_grad(v_pre)).astype(BF16)

        @pl.when(n == nb - 1)
        def _():
            msk = _sgu_mask()
            for gi in range(SGU_GROUPS):
                dw_ref[gi] = jnp.where(msk, dw_ref[gi], 0.0)
                db_ref[gi] = jnp.sum(dbacc[:, gi * gd:(gi + 1) * gd], axis=1, keepdims=True)

    return _pcall(
        body, name=name, grid=(nb,),
        in_specs=[pl.BlockSpec((SGU_BLOCK, e3), lambda n: (n, 0)),
                  pl.BlockSpec((SGU_BLOCK, e), lambda n: (n, 0)),
                  pl.BlockSpec((1, e), lambda n: (0, 0)),
                  pl.BlockSpec((SGU_GROUPS, SGU_BLOCK, SGU_BLOCK), lambda n: (0, 0, 0)),
                  pl.BlockSpec((SGU_BLOCK, e), lambda n: (0, 0))],
        out_specs=[pl.BlockSpec((SGU_BLOCK, e3), lambda n: (n, 0)),
                   pl.BlockSpec((SGU_GROUPS, SGU_BLOCK, SGU_BLOCK), lambda n: (0, 0, 0)),
                   pl.BlockSpec((SGU_GROUPS, SGU_BLOCK, 1), lambda n: (0, 0, 0)),
                   pl.BlockSpec((1, e), lambda n: (0, 0))],
        out_shape=[jax.ShapeDtypeStruct((s, e3), BF16),
                   jax.ShapeDtypeStruct((SGU_GROUPS, SGU_BLOCK, SGU_BLOCK), F32),
                   jax.ShapeDtypeStruct((SGU_GROUPS, SGU_BLOCK, 1), F32),
                   jax.ShapeDtypeStruct((1, e), F32)],
        scratch_shapes=[pltpu.VMEM((SGU_GROUPS, SGU_BLOCK, SGU_BLOCK), BF16),
                        pltpu.VMEM((SGU_GROUPS, SGU_BLOCK, SGU_BLOCK), BF16),
                        pltpu.VMEM((SGU_BLOCK, e), F32),
                        pltpu.VMEM((SGU_BLOCK, e), F32)],
        compiler_params=_params(("arbitrary",)),
    )(uvz, dy, norm_g, w_s, bias_full)


def _rope_tables(s):
    pos = jnp.arange(s, dtype=F32)
    inv_freq = ROPE_THETA ** (-jnp.arange(0, ROPE, 2, dtype=F32) / ROPE)
    ang = pos[:, None] * inv_freq[None, :]
    cos, sin = jnp.cos(ang), jnp.sin(ang)
    z32 = jnp.zeros((s, HALF), F32)
    z64 = jnp.zeros((s, ROPE), F32)
    ck = jnp.concatenate([cos, cos, z64], axis=1)
    s1k = jnp.concatenate([-sin, z32, z64], axis=1)
    s2k = jnp.concatenate([z32, sin, z64], axis=1)
    one = jnp.ones((s, NOPE), F32)
    zn = jnp.zeros((s, NOPE), F32)
    return (ck, s1k, s2k), (jnp.concatenate([one, ck], axis=1), jnp.concatenate([zn, s1k], axis=1),
                            jnp.concatenate([zn, s2k], axis=1))


def _rot(x, c, s1, s2):
    w = x.shape[-1]
    return x * c + pltpu.roll(x, w - HALF, 1) * s1 + pltpu.roll(x, HALF, 1) * s2


def _rms(cv, n_real):
    r = lax.rsqrt(jnp.sum(cv * cv, axis=-1, keepdims=True) * (1.0 / n_real) + NORM_EPS)
    return r, cv * r


def _mla_norm_fwd(proj, gq, gkv, tabs, name):
    s = proj.shape[0]
    ts = _pick(s, ROW_BLK)
    ck, s1k, s2k = tabs

    def body(cq_ref, ckv_ref, kr_ref, gq_ref, gkv_ref, c_ref, s1_ref, s2_ref, qn_ref, kvn_ref, kro_ref):
        _, nq = _rms(cq_ref[...], Q_RANK)
        qn_ref[...] = (nq * gq_ref[...]).astype(BF16)
        _, nkv = _rms(ckv_ref[...], KV_RANK)
        kvn_ref[...] = (nkv * gkv_ref[...]).astype(BF16)
        kro_ref[...] = _rot(kr_ref[...], c_ref[...], s1_ref[...], s2_ref[...]).astype(BF16)

    tab = pl.BlockSpec((ts, LANE), lambda i: (i, 0))
    return _pcall(
        body, name=name, grid=(s // ts,),
        in_specs=[pl.BlockSpec((ts, Q_RANK_PAD), lambda i: (i, 0)),
                  pl.BlockSpec((ts, KV_RANK), lambda i: (i, PROJ_CKV // KV_RANK)),
                  pl.BlockSpec((ts, LANE), lambda i: (i, PROJ_KR // LANE)),
                  _vec_spec(Q_RANK_PAD), _vec_spec(KV_RANK), tab, tab, tab],
        out_specs=[pl.BlockSpec((ts, Q_RANK_PAD), lambda i: (i, 0)),
                   pl.BlockSpec((ts, KV_RANK), lambda i: (i, 0)), tab],
        out_shape=[jax.ShapeDtypeStruct((s, Q_RANK_PAD), BF16), jax.ShapeDtypeStruct((s, KV_RANK), BF16),
                   jax.ShapeDtypeStruct((s, LANE), BF16)],
        compiler_params=_params(("parallel",)),
    )(proj, proj, proj, gq, gkv, ck, s1k, s2k)


def _rope_heads(q, tabs, sign, mult, out_dtype, name):
    s, w = q.shape
    ts = _pick(s, ROW_BLK)
    c, s1, s2 = tabs

    def body(q_ref, c_ref, s1_ref, s2_ref, o_ref):
        cv, s1v, s2v = mult * c_ref[...], (sign * mult) * s1_ref[...], (sign * mult) * s2_ref[...]
        for h in range(w // HEAD_PAD):
            lo = h * HEAD_PAD
            o_ref[:, lo:lo + HEAD_PAD] = _rot(q_ref[:, lo:lo + HEAD_PAD].astype(F32), cv, s1v, s2v).astype(out_dtype)

    blk = pl.BlockSpec((ts, w), lambda i: (i, 0))
    tab = pl.BlockSpec((ts, HEAD_PAD), lambda i: (i, 0))
    return _pcall(
        body, name=name, grid=(s // ts,),
        in_specs=[blk, tab, tab, tab], out_specs=blk,
        out_shape=jax.ShapeDtypeStruct((s, w), out_dtype),
        compiler_params=_params(("parallel",)),
    )(q, c, s1, s2)


def _transpose_bf16(t):
    return t.astype(F32).T.astype(BF16)


def _diag_mask(tb, transposed):
    r = lax.broadcasted_iota(jnp.int32, (tb, tb), 0) // CHUNK
    c = lax.broadcasted_iota(jnp.int32, (tb, tb), 1) // CHUNK
    return (r <= c) if transposed else (c <= r)


def _attn_fwd(q_cat, kv, kr, proj, name):
    s = q_cat.shape[0]
    tb = _pick(s, ATT_BLK)
    nb = s // tb
    zcol = PROJ_Z // V_DIM

    def body(q_ref, kn_ref, v_ref, kr_ref, z_ref, o_ref, y_ref, lse_ref, kt_sc, vx_sc, m_sc, acc_sc, sa_sc, sb_sc):
        qi = pl.program_id(1)

        @pl.when(qi == 0)
        def _():
            for b in range(nb):
                rows = slice(b * tb, (b + 1) * tb)
                kt_sc[b] = _transpose_bf16(jnp.concatenate([kn_ref[rows, :], kr_ref[rows, :]], axis=1))
                vx_sc[b] = jnp.concatenate([v_ref[rows, :], jnp.ones((tb, V_DIM), BF16)], axis=1)

        m_sc[...] = jnp.full_like(m_sc, -1e30)
        acc_sc[...] = jnp.zeros_like(acc_sc)
        sub = min(tb, ATT_SUB)

        def scores(ki, s_ref):
            s_ref[...] = jnp.dot(q_ref[...], kt_sc[ki], preferred_element_type=F32)

        def step(ki, s_ref, masked):
            for r in range(tb // sub):
                rs = slice(r * sub, (r + 1) * sub)
                sc = s_ref[rs, :]
                if masked:
                    sc = jnp.where(_diag_mask(tb, False)[rs, :], sc, -1e30)
                m_prev = m_sc[rs, :]
                m_new = jnp.maximum(m_prev, jnp.max(sc, axis=-1, keepdims=True))
                p = jnp.exp2(sc - m_new).astype(BF16)
                acc_sc[rs, :] = (jnp.exp2(m_prev - m_new) * acc_sc[rs, :]
                                 + jnp.dot(p, vx_sc[ki], preferred_element_type=F32))
                m_sc[rs, :] = m_new

        def pair(t, carry):
            scores(2 * t + 1, sb_sc)
            step(2 * t, sa_sc, False)
            scores(2 * t + 2, sa_sc)
            step(2 * t + 1, sb_sc, False)
            return carry

        scores(0, sa_sc)
        lax.fori_loop(0, qi // 2, pair, 0)

        @pl.when(qi % 2 == 1)
        def _():
            scores(qi, sb_sc)
            step(qi - 1, sa_sc, False)
            step(qi, sb_sc, True)

        @pl.when(qi % 2 == 0)
        def _():
            step(qi, sa_sc, True)

        l = acc_sc[:, V_DIM:V_DIM + 1]
        o = acc_sc[:, :V_DIM] / l
        z = z_ref[...]
        o_ref[...] = o.astype(BF16)
        y_ref[...] = (o * (z * _sigmoid(z))).astype(BF16)
        lse_cols = jnp.broadcast_to(m_sc[...] + jnp.log2(l), (tb, LANE))
        lse_ref[...] = lse_cols.T[0:1, :]

    oblk = pl.BlockSpec((tb, V_DIM), lambda h, qi: (qi, h))
    return _pcall(
        body, name=name, grid=(HEADS, nb),
        in_specs=[pl.BlockSpec((tb, HEAD_PAD), lambda h, qi: (qi, h)),
                  pl.BlockSpec((s, NOPE), lambda h, qi: (0, 2 * h)),
                  pl.BlockSpec((s, V_DIM), lambda h, qi: (0, 2 * h + 1)),
                  pl.BlockSpec((s, LANE), lambda h, qi: (0, 0)),
                  pl.BlockSpec((tb, V_DIM), lambda h, qi: (qi, zcol + h))],
        out_specs=[oblk, oblk, pl.BlockSpec((None, None, 1, tb), lambda h, qi: (h, qi, 0, 0))],
        out_shape=[jax.ShapeDtypeStruct((s, MLA_WIDTH), BF16), jax.ShapeDtypeStruct((s, MLA_WIDTH), BF16),
                   jax.ShapeDtypeStruct((HEADS, nb, 1, tb), F32)],
        scratch_shapes=[pltpu.VMEM((nb, HEAD_PAD, tb), BF16), pltpu.VMEM((nb, tb, HEAD_PAD), BF16),
                        pltpu.VMEM((tb, 1), F32), pltpu.VMEM((tb, HEAD_PAD), F32),
                        pltpu.VMEM((tb, tb), F32), pltpu.VMEM((tb, tb), F32)],
        compiler_params=_params(("parallel", "arbitrary")),
    )(q_cat, kv, kv, kr, proj)


def _attn_bwd(q_cat, kv, kr, do, o, lse, name):
    s = q_cat.shape[0]
    tb = _pick(s, ATT_BLK)
    nb = s // tb
    ln2 = math.log(2.0)

    def body(q_ref, do_ref, o_ref, lse_ref, kn_ref, v_ref, kr_ref, dq_ref, dkv_ref, dkr_ref,
             qt_sc, dot_sc, delta_sc, dqt_sc, dk_sc, dv_sc):
        ki = pl.program_id(1)

        @pl.when(ki == 0)
        def _():
            for b in range(nb):
                rows = slice(b * tb, (b + 1) * tb)
                qt_sc[b] = _transpose_bf16(q_ref[rows, :])
                do_t = do_ref[rows, :].astype(F32).T
                dot_sc[b] = do_t.astype(BF16)
                delta_sc[b] = jnp.sum(do_t * o_ref[rows, :].astype(F32).T, axis=0, keepdims=True)
            dqt_sc[...] = jnp.zeros_like(dqt_sc)

        k = jnp.concatenate([kn_ref[...], kr_ref[...]], axis=1)
        kt = _transpose_bf16(k)
        vb = v_ref[...]
        dk_sc[...] = jnp.zeros_like(dk_sc)
        dv_sc[...] = jnp.zeros_like(dv_sc)

        def step(qi, masked):
            rows = pl.ds(pl.multiple_of(qi * tb, tb), tb)
            sc_t = jnp.dot(k, qt_sc[qi], preferred_element_type=F32)
            if masked:
                sc_t = jnp.where(_diag_mask(tb, True), sc_t, -1e30)
            p_t = jnp.exp2(sc_t - lse_ref[qi])
            dp_t = jnp.dot(vb, dot_sc[qi], preferred_element_type=F32)
            ds_t = (p_t * (dp_t - delta_sc[qi])).astype(BF16)
            dv_sc[...] += jnp.dot(p_t.astype(BF16), do_ref[rows, :], preferred_element_type=F32)
            dk_sc[...] += jnp.dot(ds_t, q_ref[rows, :], preferred_element_type=F32)
            dqt_sc[qi] += jnp.dot(kt, ds_t, preferred_element_type=F32)

        step(ki, True)

        def loop_body(qi, carry):
            step(qi, False)
            return carry

        lax.fori_loop(ki + 1, nb, loop_body, 0)

        dkv_ref[:, :NOPE] = (dk_sc[:, :NOPE] * ln2).astype(BF16)
        dkv_ref[:, NOPE:] = dv_sc[...].astype(BF16)
        dkr_ref[...] = dk_sc[:, NOPE:] * ln2

        @pl.when(ki == nb - 1)
        def _():
            for b in range(nb):
                dq_ref[b * tb:(b + 1) * tb, :] = dqt_sc[b].T

    return _pcall(
        body, name=name, grid=(HEADS, nb),
        in_specs=[pl.BlockSpec((s, HEAD_PAD), lambda h, ki: (0, h)),
                  pl.BlockSpec((s, V_DIM), lambda h, ki: (0, h)),
                  pl.BlockSpec((s, V_DIM), lambda h, ki: (0, h)),
                  pl.BlockSpec((None, nb, 1, tb), lambda h, ki: (h, 0, 0, 0)),
                  pl.BlockSpec((tb, NOPE), lambda h, ki: (ki, 2 * h)),
                  pl.BlockSpec((tb, V_DIM), lambda h, ki: (ki, 2 * h + 1)),
                  pl.BlockSpec((tb, LANE), lambda h, ki: (ki, 0))],
        out_specs=[pl.BlockSpec((s, HEAD_PAD), lambda h, ki: (0, h)),
                   pl.BlockSpec((tb, HEAD_PAD), lambda h, ki: (ki, h)),
                   pl.BlockSpec((None, tb, LANE), lambda h, ki: (h, ki, 0))],
        out_shape=[jax.ShapeDtypeStruct((s, HEADS * HEAD_PAD), F32),
                   jax.ShapeDtypeStruct((s, HEADS * HEAD_PAD), BF16),
                   jax.ShapeDtypeStruct((HEADS, s, LANE), F32)],
        scratch_shapes=[pltpu.VMEM((nb, HEAD_PAD, tb), BF16), pltpu.VMEM((nb, V_DIM, tb), BF16),
                        pltpu.VMEM((nb, 1, tb), F32), pltpu.VMEM((nb, HEAD_PAD, tb), F32),
                        pltpu.VMEM((tb, HEAD_PAD), F32), pltpu.VMEM((tb, V_DIM), F32)],
        compiler_params=_params(("parallel", "arbitrary")),
    )(q_cat, do, o, lse, kv, kv, kr)


def _mla_gate_bwd(dy, o, proj, name):
    s = dy.shape[0]
    ts = _pick(s, ROW_BLK)
    zcol = PROJ_Z // V_DIM

    def body(dy_ref, o_ref, p_ref, do_ref, dz_ref):
        z = p_ref[:, PROJ_Z:]
        dyv = dy_ref[...]
        sig = _sigmoid(z)
        do_ref[...] = (dyv * (z * sig)).astype(BF16)
        dz_ref[...] = (dyv * o_ref[...].astype(F32) * (sig * (1.0 + z * (1.0 - sig)))).astype(BF16)

    blk = pl.BlockSpec((ts, MLA_WIDTH), lambda i: (i, 0))
    return _pcall(
        body, name=name, grid=(s // ts,),
        in_specs=[blk, blk, pl.BlockSpec((ts, PROJ_W), lambda i: (i, 0))],
        out_specs=[blk, blk],
        out_shape=[jax.ShapeDtypeStruct((s, MLA_WIDTH), BF16), jax.ShapeDtypeStruct((s, MLA_WIDTH), BF16)],
        compiler_params=_params(("parallel",)),
    )(dy, o, proj)


def _mla_norm_bwd(dqn, dkvn, dkr_heads, proj, gq, gkv, tabs, name):
    s = proj.shape[0]
    ts = _pick(s, ROW_BLK)
    ck, s1k, s2k = tabs

    def rms_bwd(cv, dn_in, g, n_real):
        r, n = _rms(cv, n_real)
        dg = jnp.sum(dn_in * n, axis=0, keepdims=True)
        dn = dn_in * g
        dc = r * (dn - n * (jnp.sum(dn * n, axis=-1, keepdims=True) * (1.0 / n_real)))
        return dc, dg

    def body(dqn_ref, dkvn_ref, dkr_ref, cq_ref, ckv_ref, gq_ref, gkv_ref, c_ref, s1_ref, s2_ref,
             dp_ref, dgq_ref, dgkv_ref):
        @pl.when(pl.program_id(0) == 0)
        def _():
            dgq_ref[...] = jnp.zeros_like(dgq_ref)
            dgkv_ref[...] = jnp.zeros_like(dgkv_ref)

        dcq, dgq = rms_bwd(cq_ref[...], dqn_ref[...], gq_ref[...], Q_RANK)
        dckv, dgkv = rms_bwd(ckv_ref[...], dkvn_ref[...], gkv_ref[...], KV_RANK)
        dgq_ref[...] += dgq
        dgkv_ref[...] += dgkv
        dkr = dkr_ref[0]
        for h in range(1, HEADS):
            dkr = dkr + dkr_ref[h]
        dp_ref[:, PROJ_CQ:PROJ_CKV] = dcq.astype(BF16)
        dp_ref[:, PROJ_CKV:PROJ_KR] = dckv.astype(BF16)
        dp_ref[:, PROJ_KR:PROJ_Z] = _rot(dkr, c_ref[...], -s1_ref[...], -s2_ref[...]).astype(BF16)

    tab = pl.BlockSpec((ts, LANE), lambda i: (i, 0))
    return _pcall(
        body, name=name, grid=(s // ts,),
        in_specs=[pl.BlockSpec((ts, Q_RANK_PAD), lambda i: (i, 0)),
                  pl.BlockSpec((ts, KV_RANK), lambda i: (i, 0)),
                  pl.BlockSpec((HEADS, ts, LANE), lambda i: (0, i, 0)),
                  pl.BlockSpec((ts, Q_RANK_PAD), lambda i: (i, 0)),
                  pl.BlockSpec((ts, KV_RANK), lambda i: (i, PROJ_CKV // KV_RANK)),
                  _vec_spec(Q_RANK_PAD), _vec_spec(KV_RANK), tab, tab, tab],
        out_specs=[pl.BlockSpec((ts, PROJ_Z), lambda i: (i, 0)), _vec_spec(Q_RANK_PAD), _vec_spec(KV_RANK)],
        out_shape=[jax.ShapeDtypeStruct((s, PROJ_Z), BF16), jax.ShapeDtypeStruct((1, Q_RANK_PAD), F32),
                   jax.ShapeDtypeStruct((1, KV_RANK), F32)],
        compiler_params=_params(("arbitrary",)),
    )(dqn, dkvn, dkr_heads, proj, proj, gq, gkv, ck, s1k, s2k)


def _ada_mod(cond_raw, ada_w, bias_my, name):
    nl, d, ncol = ada_w.shape
    tk = _pick(d, 512)
    nk = d // tk

    def body(c_ref, w_ref, b_ref, o_ref, acc_ref):
        kk = pl.program_id(1)

        @pl.when(kk == 0)
        def _():
            acc_ref[...] = jnp.zeros_like(acc_ref)

        cv = c_ref[...]
        cond = (cv * _sigmoid(cv)).astype(BF16)
        acc_ref[...] += jnp.dot(cond, w_ref[...].astype(BF16), preferred_element_type=F32)

        @pl.when(kk == nk - 1)
        def _():
            o_ref[...] = acc_ref[...] + b_ref[...]

    return _pcall(
        body, name=name, grid=(nl, nk),
        in_specs=[pl.BlockSpec((N_DEV, tk), lambda l, kk: (0, kk)),
                  pl.BlockSpec((None, tk, ncol), lambda l, kk: (l, kk, 0)),
                  pl.BlockSpec((None, 1, ncol), lambda l, kk: (l, 0, 0))],
        out_specs=pl.BlockSpec((None, N_DEV, ncol), lambda l, kk: (l, 0, 0)),
        out_shape=jax.ShapeDtypeStruct((nl, N_DEV, ncol), F32),
        scratch_shapes=[pltpu.VMEM((N_DEV, ncol), F32)],
        compiler_params=_params(("parallel", "arbitrary")),
    )(cond_raw, ada_w, bias_my.reshape(nl, 1, ncol))


def _adam(w, g, m, v):
    m = ADAM_B1 * m + (1.0 - ADAM_B1) * g
    v = ADAM_B2 * v + (1.0 - ADAM_B2) * (g * g)
    m_hat = m / (1.0 - ADAM_B1 ** ADAM_STEP)
    v_hat = v / (1.0 - ADAM_B2 ** ADAM_STEP)
    delta = -ADAM_LR * (m_hat / (jnp.sqrt(v_hat) + ADAM_EPS) + ADAM_WD * w)
    return delta, m, v


def _ada_bwd_adam(cond_t, dmod_cols, w, m, v, name):
    nl, d, ncol = w.shape
    tk = _pick(d, 512)

    def body(c_ref, dm_ref, w_ref, m_ref, v_ref, g_ref, d_ref, mo_ref, vo_ref):
        cv = c_ref[...]
        cond = (cv * _sigmoid(cv)).astype(BF16)
        g = jnp.dot(cond, dm_ref[...].astype(BF16), preferred_element_type=F32)
        delta, m2, v2 = _adam(w_ref[...], g, m_ref[...], v_ref[...])
        g_ref[...] = g
        d_ref[...] = delta
        mo_ref[...] = m2
        vo_ref[...] = v2

    blk = pl.BlockSpec((None, tk, ncol), lambda l, kk: (l, kk, 0))
    shp = jax.ShapeDtypeStruct((nl, d, ncol), F32)
    return _pcall(
        body, name=name, grid=(nl, d // tk),
        in_specs=[pl.BlockSpec((tk, N_DEV), lambda l, kk: (kk, 0)),
                  pl.BlockSpec((None, N_DEV, ncol), lambda l, kk: (l, 0, 0)), blk, blk, blk],
        out_specs=[blk, blk, blk, blk], out_shape=[shp, shp, shp, shp],
        compiler_params=_params(("parallel", "parallel")),
    )(cond_t, dmod_cols, w, m, v)


def _adam_reduce(recv0, recv1, w, m, v, name):
    nl, r, c = w.shape
    tr = _pick(r, 128) if r % 128 == 0 else r
    tc = _pick(c, 1024)

    def body(r0_ref, r1_ref, w_ref, m_ref, v_ref, g_ref, d_ref, mo_ref, vo_ref):
        l = pl.program_id(0)

        def run(rr):
            g = rr[0].astype(F32)
            for sidx in range(1, N_DEV):
                g = g + rr[sidx].astype(F32)
            delta, m2, v2 = _adam(w_ref[...], g, m_ref[...], v_ref[...])
            g_ref[...] = g
            d_ref[...] = delta
            mo_ref[...] = m2
            vo_ref[...] = v2

        @pl.when(l == 0)
        def _():
            run(r0_ref)

        @pl.when(l == 1)
        def _():
            run(r1_ref)

    rblk = pl.BlockSpec((N_DEV, tr, tc), lambda l, i, j: (0, i, j))
    blk = pl.BlockSpec((None, tr, tc), lambda l, i, j: (l, i, j))
    shp = jax.ShapeDtypeStruct((nl, r, c), F32)
    return _pcall(
        body, name=name, grid=(nl, r // tr, c // tc),
        in_specs=[rblk, rblk, blk, blk, blk],
        out_specs=[blk, blk, blk, blk], out_shape=[shp, shp, shp, shp],
        compiler_params=_params(("arbitrary", "parallel", "parallel")),
    )(recv0.reshape(N_DEV, r, c), recv1.reshape(N_DEV, r, c), w, m, v)


def _adam_small(gathered, w, m, v, name):
    r = w.shape[0]
    tr = _pick(r, 512) if r % 512 == 0 else r

    def body(p_ref, w_ref, m_ref, v_ref, g_ref, d_ref, mo_ref, vo_ref):
        g = p_ref[0]
        for sidx in range(1, N_DEV):
            g = g + p_ref[sidx]
        delta, m2, v2 = _adam(w_ref[...], g, m_ref[...], v_ref[...])
        g_ref[...] = g
        d_ref[...] = delta
        mo_ref[...] = m2
        vo_ref[...] = v2

    blk = pl.BlockSpec((tr, LANE), lambda i: (i, 0))
    shp = jax.ShapeDtypeStruct((r, LANE), F32)
    return _pcall(
        body, name=name, grid=(r // tr,),
        in_specs=[pl.BlockSpec((N_DEV, tr, LANE), lambda i: (0, i, 0)), blk, blk, blk],
        out_specs=[blk, blk, blk, blk], out_shape=[shp, shp, shp, shp],
        compiler_params=_params(("parallel",)),
    )(gathered, w, m, v)


def _my_place():
    x, y, c = lax.axis_index("x"), lax.axis_index("y"), lax.axis_index("c")
    return x, y, c, 4 * x + 2 * y + c


def _peer(x, y, c, k):
    px = 1 - x if (k >> 2) & 1 else x
    py = 1 - y if (k >> 1) & 1 else y
    pc = 1 - c if k & 1 else c
    return (px, py, pc), 4 * px + 2 * py + pc


def _slab(ref, shape, kind, p):
    r, cd = shape
    if kind == "row":
        return ref.at[pl.ds(pl.multiple_of(p * r, SUBLANE), r), :]
    return ref.at[:, pl.ds(pl.multiple_of(p * cd, LANE), cd)]


def _exchange_layout(arrays, kinds, gather):
    shard_shapes, dst_kinds, out_shapes = [], [], []
    for a, kind in zip(arrays, kinds):
        r, cd = a.shape
        if gather:
            shard, dst_kind = (r, cd), kind
        else:
            shard, dst_kind = ((r // N_DEV, cd) if kind == "row" else (r, cd // N_DEV)), "row"
        shard_shapes.append(shard)
        dst_kinds.append(dst_kind)
        full = (shard[0] * N_DEV, shard[1]) if dst_kind == "row" else (shard[0], shard[1] * N_DEV)
        out_shapes.append(jax.ShapeDtypeStruct(full, a.dtype))
    return shard_shapes, dst_kinds, out_shapes


def _exchange_copies(ins, outs, send_sems, recv_sems, sem_of, layout, kinds, gather):
    shard_shapes, dst_kinds, _ = layout
    x, y, c, me = _my_place()

    def src_for(a, p):
        return ins[a] if gather else _slab(ins[a], shard_shapes[a], kinds[a], p)

    def dst_slot(a, p):
        return _slab(outs[a], shard_shapes[a], dst_kinds[a], p)

    def local(a, sem):
        return pltpu.make_async_copy(src_for(a, me), dst_slot(a, me), sem)

    def remote(a, k, slot):
        peer, pidx = _peer(x, y, c, k)
        return pltpu.make_async_remote_copy(
            src_ref=src_for(a, pidx), dst_ref=dst_slot(a, me if slot == "mine" else pidx),
            send_sem=send_sems.at[sem_of(a, k)], recv_sem=recv_sems.at[sem_of(a, k)],
            device_id=peer, device_id_type=MESH)

    return local, remote


def _place_own(src, src_kind, slab_shape, dst_kind, full, name):
    r, cd = slab_shape
    tr = _pick(r, 512)
    nr = r // tr
    _, _, _, me = _my_place()

    def body(me_ref, s_ref, o_ref):
        o_ref[...] = s_ref[...].astype(o_ref.dtype)

    def where(kind):
        if kind is None:
            return lambda i, me_ref: (i, 0)
        if kind == "row":
            return lambda i, me_ref: (me_ref[0] * nr + i, 0)
        return lambda i, me_ref: (i, me_ref[0])

    return _pcall(
        body, name=name,
        grid_spec=pltpu.PrefetchScalarGridSpec(
            num_scalar_prefetch=1, grid=(nr,),
            in_specs=[pl.BlockSpec((tr, cd), where(src_kind))],
            out_specs=pl.BlockSpec((tr, cd), where(dst_kind))),
        out_shape=jax.ShapeDtypeStruct(full.shape, full.dtype),
        compiler_params=_params(("arbitrary",)),
    )(jnp.reshape(me, (1,)).astype(jnp.int32), src)


def _landing_with_own_slab(arrays, kinds, gather, layout, name=None):
    _, _, _, me = _my_place()
    lands = []
    for a in range(len(arrays)):
        (r, cd), dst_kind, full = layout[0][a], layout[1][a], layout[2][a]
        if name is not None:
            lands.append(_place_own(arrays[a], None if gather else kinds[a], (r, cd), dst_kind, full, name))
            continue
        if gather:
            piece = arrays[a]
        elif kinds[a] == "row":
            piece = lax.dynamic_slice_in_dim(arrays[a], me * r, r, axis=0)
        else:
            piece = lax.dynamic_slice_in_dim(arrays[a], me * cd, cd, axis=1)
        at = (me * r, 0) if dst_kind == "row" else (0, me * cd)
        lands.append(lax.dynamic_update_slice(lax.empty(full.shape, full.dtype), piece, at))
    return lands


def _exchange(arrays, kinds, gather, name):
    n = len(arrays)
    layout = _exchange_layout(arrays, kinds, gather)
    lands = _landing_with_own_slab(arrays, kinds, gather, layout)

    def body(*refs):
        ins, outs = refs[:n], refs[2 * n:3 * n]
        send_sems, recv_sems = refs[3 * n:]
        _, remote = _exchange_copies(ins, outs, send_sems, recv_sems,
                                     lambda a, k: a * (N_DEV - 1) + k - 1, layout, kinds, gather)
        for a in range(n):
            for k in range(1, N_DEV):
                remote(a, k, "mine").start()
        for a in range(n):
            for k in range(1, N_DEV):
                arrival = remote(a, k, "theirs")
                arrival.wait_send()
                arrival.wait_recv()

    anyspec = pl.BlockSpec(memory_space=pl.ANY)
    outs = _pcall(
        body, comm=True, name=name,
        in_specs=[anyspec] * (2 * n), out_specs=[anyspec] * n, out_shape=layout[2],
        input_output_aliases={n + a: a for a in range(n)},
        scratch_shapes=[pltpu.SemaphoreType.DMA((n * (N_DEV - 1),)), pltpu.SemaphoreType.DMA((n * (N_DEV - 1),))],
    )(*arrays, *lands)
    return list(outs)


HBM_SPEC = pl.BlockSpec(memory_space=pltpu.HBM)
SEM_SPEC = pl.BlockSpec(memory_space=pltpu.SEMAPHORE)
ANY_SPEC = pl.BlockSpec(memory_space=pl.ANY)
DATAFLOW = pltpu.SideEffectType.DATAFLOW_SIDE_EFFECTING


def _exchange_start(arrays, kinds, gather, name, after, carry=()):
    n, nc = len(arrays), len(carry)
    layout = _exchange_layout(arrays, kinds, gather)
    lands = _landing_with_own_slab(arrays, kinds, gather, layout, "place_own")

    def body(*refs):
        ins, outs = refs[:n], refs[n:2 * n]
        send_sems, recv_sems = refs[2 * n + nc + 1], refs[2 * n + nc + 2]
        token = refs[2 * n + nc + 3 + 2 * n + nc]
        _, remote = _exchange_copies(ins, outs, send_sems, recv_sems, lambda a, k: a, layout, kinds, gather)
        for a in range(n):
            for k in range(1, N_DEV):
                remote(a, k, "mine").start()
        token[...] = jnp.zeros_like(token)

    passed = list(arrays) + lands + list(carry)
    res = pl.pallas_call(
        body, name=name,
        out_shape=(pltpu.SemaphoreType.DMA((n,)), pltpu.SemaphoreType.DMA((n,)),
                   *[pltpu.HBM(t.shape, t.dtype) for t in passed], jax.ShapeDtypeStruct((SUBLANE, LANE), F32)),
        in_specs=[HBM_SPEC] * (2 * n + nc) + [ANY_SPEC],
        out_specs=(SEM_SPEC, SEM_SPEC, *([HBM_SPEC] * (2 * n + nc)), pl.BlockSpec(memory_space=pltpu.VMEM)),
        input_output_aliases={i: 2 + i for i in range(2 * n + nc)},
        compiler_params=pltpu.CompilerParams(has_side_effects=DATAFLOW),
    )(*[pltpu.with_memory_space_constraint(t, pltpu.HBM) for t in passed], after)
    handle = (res[0], res[1], list(res[2:2 + n]), list(res[2 + n:2 + 2 * n]), tuple(kinds), gather)
    return handle, res[-1], list(res[2 + 2 * n:2 + 2 * n + nc])


def _exchange_wait(handle, name, after):
    send_sems, recv_sems, ins_thru, lands_thru, kinds, gather = handle
    n = len(ins_thru)
    layout = _exchange_layout(ins_thru, kinds, gather)

    def body(*refs):
        ins, outs = refs[:n], refs[n:2 * n]
        s_sems, r_sems = refs[2 * n], refs[2 * n + 1]
        _, remote = _exchange_copies(ins, outs, s_sems, r_sems, lambda a, k: a, layout, kinds, gather)
        for a in range(n):
            for k in range(1, N_DEV):
                arrival = remote(a, k, "theirs")
                arrival.wait_send()
                arrival.wait_recv()

    res = pl.pallas_call(
        body, name=name,
        out_shape=[pltpu.HBM(t.shape, t.dtype) for t in ins_thru + lands_thru],
        in_specs=[HBM_SPEC] * (2 * n) + [SEM_SPEC, SEM_SPEC, ANY_SPEC],
        out_specs=[HBM_SPEC] * (2 * n),
        input_output_aliases={i: i for i in range(2 * n)},
        compiler_params=pltpu.CompilerParams(has_side_effects=DATAFLOW),
    )(*ins_thru, *lands_thru, send_sems, recv_sems, after)
    return list(res[n:2 * n])


def _pad_mla_w_in(w):
    d = w.shape[0]
    z = lambda n: jnp.zeros((d, n), w.dtype)
    o1, o2, o3 = Q_RANK, Q_RANK + KV_RANK, Q_RANK + KV_RANK + ROPE
    return jnp.concatenate([w[:, :o1], z(Q_RANK_PAD - Q_RANK), w[:, o1:o2], w[:, o2:o3], z(LANE - ROPE), w[:, o3:]], axis=1)


def _unpad_mla_w_in(g):
    return jnp.concatenate([g[:, :Q_RANK], g[:, PROJ_CKV:PROJ_KR], g[:, PROJ_KR:PROJ_KR + ROPE], g[:, PROJ_Z:]], axis=1)


def _pad_w_uq(w):
    w3 = w.reshape(Q_RANK, HEADS, NOPE + ROPE)
    w3 = jnp.pad(w3, ((0, Q_RANK_PAD - Q_RANK), (0, 0), (0, HEAD_PAD - NOPE - ROPE)))
    return w3.reshape(Q_RANK_PAD, HEADS * HEAD_PAD)


def _unpad_w_uq(g):
    return g[:Q_RANK].reshape(Q_RANK, HEADS, HEAD_PAD)[:, :, :NOPE + ROPE].reshape(Q_RANK, HEADS * (NOPE + ROPE))


def _pack(pieces):
    flat = [p.reshape(-1).astype(F32) for p in pieces]
    tot = sum(f.shape[0] for f in flat)
    unit = SUBLANE * LANE
    padn = (-tot) % unit
    if padn:
        flat.append(jnp.zeros((padn,), F32))
    return jnp.concatenate(flat).reshape(-1, LANE)


def _unpack(packed, shapes):
    flat = packed.reshape(-1)
    out, off = [], 0
    for shp in shapes:
        nel = int(np.prod(shp))
        out.append(flat[off:off + nel].reshape(shp))
        off += nel
    return out


WEIGHTS = ['ada_w', 'ada_b', 'pre_g', 'post_g', 'sgu_w_in', 'sgu_norm_g', 'sgu_w_s', 'sgu_b_s', 'sgu_w_out',
           'mla_w_in', 'mla_q_norm_g', 'mla_kv_norm_g', 'mla_w_uq', 'mla_w_ukv', 'mla_w_out']
INPUTS = ['x', 'c'] + WEIGHTS + ['loss_target'] + ['m_' + n for n in WEIGHTS] + ['v_' + n for n in WEIGHTS]


def kernel(x, c, ada_w, ada_b, pre_g, post_g, sgu_w_in, sgu_norm_g, sgu_w_s, sgu_b_s, sgu_w_out, mla_w_in, mla_q_norm_g, mla_kv_norm_g, mla_w_uq, mla_w_ukv, mla_w_out, loss_target, m_ada_w, m_ada_b, m_pre_g, m_post_g, m_sgu_w_in, m_sgu_norm_g, m_sgu_w_s, m_sgu_b_s, m_sgu_w_out, m_mla_w_in, m_mla_q_norm_g, m_mla_kv_norm_g, m_mla_w_uq, m_mla_w_ukv, m_mla_w_out, v_ada_w, v_ada_b, v_pre_g, v_post_g, v_sgu_w_in, v_sgu_norm_g, v_sgu_w_s, v_sgu_b_s, v_sgu_w_out, v_mla_w_in, v_mla_q_norm_g, v_mla_kv_norm_g, v_mla_w_uq, v_mla_w_ukv, v_mla_w_out):
    given = locals()
    A = {name: given[name] for name in INPUTS}
    x0 = A['x'][0]
    tgt = A['loss_target'][0]
    s, d = x0.shape
    e = 2 * d
    ncol = 3 * d // N_DEV
    _, _, _, me = _my_place()
    ktabs, qtabs = _rope_tables(s)

    gains = jnp.zeros((SUBLANE, LANE), F32)
    gains = gains.at[0:2, :Q_RANK // N_DEV].set(A['mla_q_norm_g'])
    gains = gains.at[2:4, :KV_RANK // N_DEV].set(A['mla_kv_norm_g'])
    c8 = jnp.broadcast_to(A['c'], (SUBLANE, d))
    cg, gg = _exchange([c8, gains], ["row", "row"], True, "ag_cond")
    cond_raw = cg.reshape(N_DEV, SUBLANE, d)[:, 0, :]
    gg = gg.reshape(N_DEV, SUBLANE, LANE)
    gq_full = jnp.transpose(gg[:, 0:2, :Q_RANK // N_DEV], (1, 0, 2)).reshape(N_MIX, Q_RANK)
    gkv_full = jnp.transpose(gg[:, 2:4, :KV_RANK // N_DEV], (1, 0, 2)).reshape(N_MIX, KV_RANK)
    gq_pad = jnp.pad(gq_full, ((0, 0), (0, Q_RANK_PAD - Q_RANK)))

    bias_my = lax.dynamic_slice_in_dim(A['ada_b'], me * ncol, ncol, axis=1)
    mod_part = _ada_mod(cond_raw, A['ada_w'], bias_my, "ada_mod")
    send = jnp.pad(jnp.transpose(mod_part, (1, 0, 2)), ((0, 0), (0, SUBLANE - DEPTH), (0, 0)))
    (rb,) = _exchange([send.reshape(N_DEV * SUBLANE, ncol)], ["row"], False, "a2a_mod")

    token = rb
    gathers = []
    for i in range(DEPTH):
        j = i // N_MIX
        if i % N_MIX == 0:
            shards, kinds = [A['sgu_w_in'][j], A['sgu_w_out'][j]], ["col", "row"]
        else:
            shards = [A['mla_w_in'][j], A['mla_w_uq'][j], A['mla_w_ukv'][j], A['mla_w_out'][j]]
            kinds = ["col", "col", "col", "row"]
        handle, token, _ = _exchange_start([t.astype(BF16) for t in shards], kinds, True, f"ag_start_{i}", token)
        gathers.append(handle)

    mod = jnp.transpose(rb.reshape(N_DEV, SUBLANE, ncol)[:, :DEPTH, :], (1, 0, 2)).reshape(DEPTH, 3 * d) + token[0, 0]
    shift = [mod[i:i + 1, :d] for i in range(DEPTH)]
    scale = [mod[i:i + 1, d:2 * d] for i in range(DEPTH)]
    gate = [mod[i:i + 1, 2 * d:] for i in range(DEPTH)]

    saved = []
    x = x0
    for i in range(DEPTH):
        j = i // N_MIX
        h = _pre_fwd(x, A['pre_g'][i:i + 1], scale[i], shift[i], f"pre_fwd")
        if i % N_MIX == 0:
            w_in, w_out = _exchange_wait(gathers[i], f"ag_wait_{i}", h)
            uvz = _mm(h, w_in, "nn", F32, "sgu_in")
            bias_full = jnp.repeat(A['sgu_b_s'][j].T, e // SGU_GROUPS, axis=1)
            ng = A['sgu_norm_g'][j:j + 1]
            y = _sgu_mid_fwd(uvz, ng, A['sgu_w_s'][j], bias_full, "sgu_mid_fwd")
            out = _mm(y, w_out, "nn", F32, "sgu_out")
            saved.append(dict(x=x, h=h, uvz=uvz, y=y, out=out, w_in=w_in, w_out=w_out, bias=bias_full, ng=ng))
        else:
            w_in, w_uq, w_ukv, w_out = _exchange_wait(gathers[i], f"ag_wait_{i}", h)
            w_in = _pad_mla_w_in(w_in)
            w_uq = _pad_w_uq(w_uq)
            gq, gkv = gq_pad[j:j + 1], gkv_full[j:j + 1]
            proj = _mm(h, w_in, "nn", F32, "mla_in")
            qn, kvn, kr = _mla_norm_fwd(proj, gq, gkv, ktabs, "mla_norm_fwd")
            q = _mm(qn, w_uq, "nn", F32, "mla_uq")
            q_cat = _rope_heads(q, qtabs, 1.0, SOFTMAX_SCALE * LOG2_E, BF16, "rope_q_fwd")
            kv = _mm(kvn, w_ukv, "nn", BF16, "mla_ukv")
            o, y, lse = _attn_fwd(q_cat, kv, kr, proj, "attn_fwd")
            out = _mm(y, w_out, "nn", F32, "mla_out")
            saved.append(dict(x=x, h=h, proj=proj, qn=qn, kvn=kvn, kr=kr, q_cat=q_cat, kv=kv, o=o, y=y, lse=lse,
                              out=out, w_in=w_in, w_uq=w_uq, w_ukv=w_ukv, w_out=w_out, gq=gq, gkv=gkv))
        x = _post_fwd(x, out, gate[i], A['post_g'][i:i + 1], "post_fwd")

    dx, loss_row = _loss_head(x, tgt, "loss_head")
    loss = lax.psum(loss_row[0, 0], ("x", "y", "c"))

    d_shift, d_scale, d_gate = [None] * DEPTH, [None] * DEPTH, [None] * DEPTH
    d_pre, d_post = [None] * DEPTH, [None] * DEPTH
    scatters, scatters_out = [None] * DEPTH, [None] * DEPTH
    small = {}
    for i in reversed(range(DEPTH)):
        j = i // N_MIX
        sv = saved[i]
        dout, d_gate[i], d_post[i] = _post_bwd(dx, sv['out'], gate[i], A['post_g'][i:i + 1], "post_bwd")
        if i % N_MIX == 0:
            dy = _mm(dout, sv['w_out'], "nt", F32, "sgu_out_dx")
            g_w_out = _mm(sv['y'], dout, "tn", BF16, "sgu_out_dw")
            scatters_out[i], token, (dy,) = _exchange_start([g_w_out], ["row"], False, f"rs_out_start_{i}", token,
                                                            carry=[dy])
            duvz, dws, dbs, dng = _sgu_mid_bwd(sv['uvz'], dy, sv['ng'], A['sgu_w_s'][j], sv['bias'], "sgu_mid_bwd")
            g_w_in = _mm(sv['h'], duvz, "tn", BF16, "sgu_in_dw")
            scatters[i], token, (duvz,) = _exchange_start([g_w_in], ["col"], False, f"rs_start_{i}", token,
                                                          carry=[duvz])
            dh = _mm(duvz, sv['w_in'], "nt", F32, "sgu_in_dx")
            small[('sgu', j)] = (dws, dbs.reshape(SGU_GROUPS, SGU_BLOCK), dng)
        else:
            dy = _mm(dout, sv['w_out'], "nt", F32, "mla_out_dx")
            g_w_out = _mm(sv['y'], dout, "tn", BF16, "mla_out_dw")
            scatters_out[i], token, (dy,) = _exchange_start([g_w_out], ["row"], False, f"rs_out_start_{i}", token,
                                                            carry=[dy])
            do, dz = _mla_gate_bwd(dy, sv['o'], sv['proj'], "mla_gate_bwd")
            dq, dkv, dkr_heads = _attn_bwd(sv['q_cat'], sv['kv'], sv['kr'], do, sv['o'], sv['lse'], "attn_bwd")
            dq_b = _rope_heads(dq, qtabs, -1.0, SOFTMAX_SCALE, BF16, "rope_q_bwd")
            dqn = _mm(dq_b, sv['w_uq'], "nt", F32, "mla_uq_dx")
            g_w_uq = _unpad_w_uq(_mm(sv['qn'], dq_b, "tn", BF16, "mla_uq_dw"))
            dkvn = _mm(dkv, sv['w_ukv'], "nt", F32, "mla_ukv_dx")
            g_w_ukv = _mm(sv['kvn'], dkv, "tn", BF16, "mla_ukv_dw")
            dpa, dgq, dgkv = _mla_norm_bwd(dqn, dkvn, dkr_heads, sv['proj'], sv['gq'], sv['gkv'], ktabs, "mla_norm_bwd")
            dproj = jnp.concatenate([dpa, dz], axis=1)
            g_w_in = _unpad_mla_w_in(_mm(sv['h'], dproj, "tn", BF16, "mla_in_dw"))
            scatters[i], token, (dproj,) = _exchange_start(
                [g_w_in, g_w_uq, g_w_ukv], ["col", "col", "col"], False, f"rs_start_{i}", token, carry=[dproj])
            dh = _mm(dproj, sv['w_in'], "nt", F32, "mla_in_dx")
            small[('mla', j)] = (dgq[:, :Q_RANK], dgkv)
        dx, d_shift[i], d_scale[i], d_pre[i] = _pre_bwd(dh, sv['x'], dx, A['pre_g'][i:i + 1], scale[i], "pre_bwd")

    res = {}

    def big(name, recv0, recv1):
        res[name] = _adam_reduce(recv0, recv1, A[name], A['m_' + name], A['v_' + name], "adam_" + name)

    def finish(i, after):
        return (_exchange_wait(scatters[i], f"rs_wait_{i}", after)
                + _exchange_wait(scatters_out[i], f"rs_out_wait_{i}", after))

    recv_mla = {j: finish(N_MIX * j + 1, dx) for j in reversed(range(N_MIX))}
    for idx_w, name in enumerate(['mla_w_in', 'mla_w_uq', 'mla_w_ukv', 'mla_w_out']):
        big(name, recv_mla[0][idx_w], recv_mla[1][idx_w])

    dmod = jnp.concatenate([jnp.concatenate([d_shift[i], d_scale[i], d_gate[i]], axis=1) for i in range(DEPTH)], axis=0)
    pieces = [dmod, jnp.concatenate(d_pre, axis=0), jnp.concatenate(d_post, axis=0),
              jnp.concatenate([small[('sgu', j)][2] for j in range(N_MIX)], axis=0),
              jnp.stack([small[('sgu', j)][0] for j in range(N_MIX)]),
              jnp.stack([small[('sgu', j)][1] for j in range(N_MIX)]),
              jnp.concatenate([small[('mla', j)][0] for j in range(N_MIX)], axis=0),
              jnp.concatenate([small[('mla', j)][1] for j in range(N_MIX)], axis=0)]
    small_names = ['ada_b', 'pre_g', 'post_g', 'sgu_norm_g', 'sgu_w_s', 'sgu_b_s', 'mla_q_norm_g', 'mla_kv_norm_g']
    full_shapes = [p.shape for p in pieces]
    packed = _pack(pieces)
    rows = packed.shape[0]
    (gathered,) = _exchange([packed], ["row"], True, "ag_small")
    gathered = gathered.reshape(N_DEV, rows, LANE)

    def full_size(prefix):
        out = []
        for name, shp in zip(small_names, full_shapes):
            t = A[prefix + name]
            if t.shape != tuple(shp):
                t = lax.dynamic_update_slice_in_dim(jnp.zeros(shp, F32), t, me * t.shape[1], axis=1)
            out.append(t)
        return _pack(out)

    sm = _adam_small(gathered, full_size(''), full_size('m_'), full_size('v_'), "adam_small")
    sm = [_unpack(t, full_shapes) for t in sm]
    for k_out in range(4):
        for name, val in zip(small_names, sm[k_out]):
            t = A[name]
            if t.shape != val.shape:
                val = lax.dynamic_slice_in_dim(val, me * t.shape[1], t.shape[1], axis=1)
            res.setdefault(name, [None] * 4)[k_out] = val

    nrow_dmod = DEPTH * 3 * d // LANE
    dmod_all = gathered[:, :nrow_dmod, :].reshape(N_DEV, DEPTH, 3 * d)
    dmod_cols = jnp.transpose(lax.dynamic_slice_in_dim(dmod_all, me * ncol, ncol, axis=2), (1, 0, 2))
    res['ada_w'] = _ada_bwd_adam(jnp.transpose(cond_raw), dmod_cols, A['ada_w'], A['m_ada_w'], A['v_ada_w'], "ada_bwd")

    recv_sgu = {j: finish(N_MIX * j, res['ada_w'][0]) for j in reversed(range(N_MIX))}
    for idx_w, name in enumerate(['sgu_w_in', 'sgu_w_out']):
        big(name, recv_sgu[0][idx_w], recv_sgu[1][idx_w])

    outs = [loss, dx[None]]
    for k_out in range(4):
        outs += [res[n][k_out] for n in WEIGHTS]
    return tuple(outs)
```

```python
import functools
import math

import numpy as np
import jax
import jax.numpy as jnp
from jax import lax
from jax.experimental import pallas as pl
from jax.experimental.pallas import tpu as pltpu

F32 = jnp.float32
BF16 = jnp.bfloat16
MESH = pl.DeviceIdType.MESH

N_DEV = 8
DEPTH = 4
N_MIX = 2
NORM_EPS = 1e-6
CHUNK = 64
SGU_BLOCK = 128
SGU_GROUPS = 16
HEADS = 16
Q_RANK = 448
Q_RANK_PAD = 512
KV_RANK = 512
NOPE = 128
ROPE = 64
HALF = ROPE // 2
V_DIM = 128
HEAD_PAD = 256
ROPE_THETA = 10000.0
MLA_WIDTH = HEADS * V_DIM
LANE = 128
SUBLANE = 8
PROJ_CQ = 0
PROJ_CKV = Q_RANK_PAD
PROJ_KR = Q_RANK_PAD + KV_RANK
PROJ_Z = PROJ_KR + LANE
PROJ_W = PROJ_Z + MLA_WIDTH

ADAM_LR = 0.001
ADAM_B1 = 0.9
ADAM_B2 = 0.999
ADAM_EPS = 1e-08
ADAM_WD = 0.01
ADAM_STEP = 10

VMEM_LIMIT = 56 * 1024 * 1024
ATT_BLK = 512
ATT_SUB = 128
ROW_BLK = 256
MM_TM, MM_TN, MM_TK = 1024, 1024, 2048
MM_TILE_BYTES = 40 * 1024 * 1024
SOFTMAX_SCALE = (NOPE + ROPE) ** -0.5
LOG2_E = 1.0 / math.log(2.0)
INV_SQRT2 = 1.0 / math.sqrt(2.0)
INV_SQRT_2PI = 1.0 / math.sqrt(2.0 * math.pi)


def _pcall(body, comm=False, **kw):
    return pl.pallas_call(body, **kw)


def _params(sem=None):
    return pltpu.CompilerParams(dimension_semantics=sem, vmem_limit_bytes=VMEM_LIMIT)


def _pick(dim, pref):
    if dim <= pref:
        return dim
    t = (pref // LANE) * LANE
    while t >= LANE:
        if dim % t == 0:
            return t
        t -= LANE
    return dim


def _gelu(x):
    return 0.5 * x * (1.0 + lax.erf(x * INV_SQRT2))


def _gelu_grad(x):
    return 0.5 * (1.0 + lax.erf(x * INV_SQRT2)) + x * jnp.exp(-0.5 * x * x) * INV_SQRT_2PI


def _sigmoid(x):
    return 1.0 / (1.0 + jnp.exp(-x))


def _dot_nt(a, b):
    return lax.dot_general(a, b, (((1,), (1,)), ((), ())), preferred_element_type=F32)


def _dot_tn(a, b):
    return lax.dot_general(a, b, (((0,), (0,)), ((), ())), preferred_element_type=F32)


def _mm(a, b, dims, out_dtype, name):
    if dims == "nn":
        (m, k), (k2, n) = a.shape, b.shape
    elif dims == "nt":
        (m, k), (n, k2) = a.shape, b.shape
    else:
        (k, m), (k2, n) = a.shape, b.shape
    assert k == k2, (a.shape, b.shape, dims)
    tm, tn = _pick(m, MM_TM), _pick(n, MM_TN)
    out_bytes = 2 * tm * tn * jnp.dtype(out_dtype).itemsize
    whole_k = 2 * (tm + tn) * k * a.dtype.itemsize + out_bytes <= MM_TILE_BYTES
    tk = k if whole_k else _pick(k, MM_TK)
    nk = k // tk

    def body(a_ref, b_ref, o_ref, *scratch):
        if dims == "nn":
            p = jnp.dot(a_ref[...], b_ref[...], preferred_element_type=F32)
        elif dims == "nt":
            p = _dot_nt(a_ref[...], b_ref[...])
        else:
            p = _dot_tn(a_ref[...], b_ref[...])
        if nk == 1:
            o_ref[...] = p.astype(o_ref.dtype)
            return
        acc_ref, = scratch
        kk = pl.program_id(2)

        @pl.when(kk == 0)
        def _():
            acc_ref[...] = p

        @pl.when(kk > 0)
        def _():
            acc_ref[...] += p

        @pl.when(kk == nk - 1)
        def _():
            o_ref[...] = acc_ref[...].astype(o_ref.dtype)

    if dims == "tn":
        a_spec = pl.BlockSpec((tk, tm), lambda i, j, kk: (kk, i))
    else:
        a_spec = pl.BlockSpec((tm, tk), lambda i, j, kk: (i, kk))
    if dims == "nt":
        b_spec = pl.BlockSpec((tn, tk), lambda i, j, kk: (j, kk))
    else:
        b_spec = pl.BlockSpec((tk, tn), lambda i, j, kk: (kk, j))
    return _pcall(
        body, name=name,
        grid=(m // tm, n // tn, nk),
        in_specs=[a_spec, b_spec],
        out_specs=pl.BlockSpec((tm, tn), lambda i, j, kk: (i, j)),
        out_shape=jax.ShapeDtypeStruct((m, n), out_dtype),
        scratch_shapes=[pltpu.VMEM((tm, tn), F32)] if nk > 1 else [],
        compiler_params=_params(("parallel", "parallel", "arbitrary")),
    )(a, b)


def _row_spec(ts, d):
    return pl.BlockSpec((ts, d), lambda i: (i, 0))


def _vec_spec(d):
    return pl.BlockSpec((1, d), lambda i: (0, 0))


def _pre_fwd(x, g, scale, shift, name):
    s, d = x.shape
    ts = _pick(s, ROW_BLK)

    def body(x_ref, g_ref, sc_ref, sh_ref, h_ref):
        xv = x_ref[...]
        r = lax.rsqrt(jnp.mean(xv * xv, axis=-1, keepdims=True) + NORM_EPS)
        h_ref[...] = ((xv * r * g_ref[...]) * (1.0 + sc_ref[...]) + sh_ref[...]).astype(BF16)

    return _pcall(
        body, name=name, grid=(s // ts,),
        in_specs=[_row_spec(ts, d), _vec_spec(d), _vec_spec(d), _vec_spec(d)],
        out_specs=_row_spec(ts, d),
        out_shape=jax.ShapeDtypeStruct((s, d), BF16),
        compiler_params=_params(("parallel",)),
    )(x, g, scale, shift)


def _post_fwd(x, out, gate, g, name):
    s, d = x.shape
    ts = _pick(s, ROW_BLK)

    def body(x_ref, o_ref, gate_ref, g_ref, y_ref):
        o = o_ref[...]
        r = lax.rsqrt(jnp.mean(o * o, axis=-1, keepdims=True) + NORM_EPS)
        y_ref[...] = x_ref[...] + gate_ref[...] * (o * r * g_ref[...])

    return _pcall(
        body, name=name, grid=(s // ts,),
        in_specs=[_row_spec(ts, d), _row_spec(ts, d), _vec_spec(d), _vec_spec(d)],
        out_specs=_row_spec(ts, d),
        out_shape=jax.ShapeDtypeStruct((s, d), F32),
        compiler_params=_params(("parallel",)),
    )(x, out, gate, g)


def _loss_head(xf, tgt, name):
    s, d = xf.shape
    ts = _pick(s, ROW_BLK)
    ns = s // ts

    def body(x_ref, t_ref, dx_ref, loss_ref, acc_ref):
        i = pl.program_id(0)

        @pl.when(i == 0)
        def _():
            acc_ref[...] = jnp.zeros_like(acc_ref)

        e = x_ref[...] - t_ref[...]
        dx_ref[...] = e * (1.0 / d)
        acc_ref[...] += jnp.sum(e * e, axis=0, keepdims=True)

        @pl.when(i == ns - 1)
        def _():
            tot = jnp.sum(acc_ref[...], axis=1, keepdims=True) * (0.5 / d)
            loss_ref[...] = jnp.broadcast_to(tot, loss_ref.shape)

    return _pcall(
        body, name=name, grid=(ns,),
        in_specs=[_row_spec(ts, d), _row_spec(ts, d)],
        out_specs=[_row_spec(ts, d), pl.BlockSpec((1, LANE), lambda i: (0, 0))],
        out_shape=[jax.ShapeDtypeStruct((s, d), F32), jax.ShapeDtypeStruct((1, LANE), F32)],
        scratch_shapes=[pltpu.VMEM((1, d), F32)],
        compiler_params=_params(("arbitrary",)),
    )(xf, tgt)


def _post_bwd(dxo, out, gate, g, name):
    s, d = dxo.shape
    ts = _pick(s, ROW_BLK)

    def body(dx_ref, o_ref, gate_ref, g_ref, do_ref, dgate_ref, dg_ref):
        i = pl.program_id(0)

        @pl.when(i == 0)
        def _():
            dgate_ref[...] = jnp.zeros_like(dgate_ref)
            dg_ref[...] = jnp.zeros_like(dg_ref)

        o = o_ref[...]
        dx = dx_ref[...]
        gv = g_ref[...]
        r = lax.rsqrt(jnp.mean(o * o, axis=-1, keepdims=True) + NORM_EPS)
        n = o * r
        dyn = dx * gate_ref[...]
        dgate_ref[...] += jnp.sum(dx * (n * gv), axis=0, keepdims=True)
        dg_ref[...] += jnp.sum(dyn * n, axis=0, keepdims=True)
        dn = dyn * gv
        do_ref[...] = (r * (dn - n * jnp.mean(dn * n, axis=-1, keepdims=True))).astype(BF16)

    return _pcall(
        body, name=name, grid=(s // ts,),
        in_specs=[_row_spec(ts, d), _row_spec(ts, d), _vec_spec(d), _vec_spec(d)],
        out_specs=[_row_spec(ts, d), _vec_spec(d), _vec_spec(d)],
        out_shape=[jax.ShapeDtypeStruct((s, d), BF16), jax.ShapeDtypeStruct((1, d), F32),
                   jax.ShapeDtypeStruct((1, d), F32)],
        compiler_params=_params(("arbitrary",)),
    )(dxo, out, gate, g)


def _pre_bwd(dh, x, dxo, g, scale, name):
    s, d = x.shape
    ts = _pick(s, ROW_BLK)

    def body(dh_ref, x_ref, dxo_ref, g_ref, sc_ref, dx_ref, dsh_ref, dsc_ref, dg_ref):
        i = pl.program_id(0)

        @pl.when(i == 0)
        def _():
            dsh_ref[...] = jnp.zeros_like(dsh_ref)
            dsc_ref[...] = jnp.zeros_like(dsc_ref)
            dg_ref[...] = jnp.zeros_like(dg_ref)

        xv = x_ref[...]
        dhv = dh_ref[...]
        gv = g_ref[...]
        one_sc = 1.0 + sc_ref[...]
        r = lax.rsqrt(jnp.mean(xv * xv, axis=-1, keepdims=True) + NORM_EPS)
        n = xv * r
        dsh_ref[...] += jnp.sum(dhv, axis=0, keepdims=True)
        dsc_ref[...] += jnp.sum(dhv * (n * gv), axis=0, keepdims=True)
        dng = dhv * one_sc
        dg_ref[...] += jnp.sum(dng * n, axis=0, keepdims=True)
        dn = dng * gv
        dx_ref[...] = dxo_ref[...] + r * (dn - n * jnp.mean(dn * n, axis=-1, keepdims=True))

    return _pcall(
        body, name=name, grid=(s // ts,),
        in_specs=[_row_spec(ts, d), _row_spec(ts, d), _row_spec(ts, d), _vec_spec(d), _vec_spec(d)],
        out_specs=[_row_spec(ts, d), _vec_spec(d), _vec_spec(d), _vec_spec(d)],
        out_shape=[jax.ShapeDtypeStruct((s, d), F32)] + [jax.ShapeDtypeStruct((1, d), F32)] * 3,
        compiler_params=_params(("arbitrary",)),
    )(dh, x, dxo, g, scale)


def _sgu_mask():
    t = lax.broadcasted_iota(jnp.int32, (SGU_BLOCK, SGU_BLOCK), 0) // CHUNK
    s = lax.broadcasted_iota(jnp.int32, (SGU_BLOCK, SGU_BLOCK), 1) // CHUNK
    return s <= t


def _sgu_norm(v_pre, g):
    e = v_pre.shape[-1]
    vg = _gelu(v_pre)
    mu = jnp.sum(vg, axis=-1, keepdims=True) * (1.0 / e)
    dlt = vg - mu
    var = jnp.sum(dlt * dlt, axis=-1, keepdims=True) * (1.0 / e)
    rstd = lax.rsqrt(var + NORM_EPS)
    vhat = dlt * rstd
    return vhat, rstd, (vhat * g).astype(BF16)


def _sgu_mid_fwd(uvz, norm_g, w_s, bias_full, name):
    s, e3 = uvz.shape
    e = e3 // 3
    gd = e // SGU_GROUPS
    nb = s // SGU_BLOCK

    def body(uvz_ref, g_ref, w_ref, b_ref, y_ref, wsc):
        @pl.when(pl.program_id(0) == 0)
        def _():
            msk = _sgu_mask()
            for gi in range(SGU_GROUPS):
                wsc[gi] = jnp.where(msk, w_ref[gi], 0.0).astype(BF16)

        _, _, vb = _sgu_norm(uvz_ref[:, e:2 * e], g_ref[...])
        for gi in range(SGU_GROUPS):
            lo = gi * gd
            vm = jnp.dot(wsc[gi], vb[:, lo:lo + gd], preferred_element_type=F32) + b_ref[:, lo:lo + gd]
            zg = uvz_ref[:, 2 * e + lo:2 * e + lo + gd]
            y_ref[:, lo:lo + gd] = (_gelu(uvz_ref[:, lo:lo + gd]) * vm * (zg * _sigmoid(zg))).astype(BF16)

    return _pcall(
        body, name=name, grid=(nb,),
        in_specs=[pl.BlockSpec((SGU_BLOCK, e3), lambda n: (n, 0)),
                  pl.BlockSpec((1, e), lambda n: (0, 0)),
                  pl.BlockSpec((SGU_GROUPS, SGU_BLOCK, SGU_BLOCK), lambda n: (0, 0, 0)),
                  pl.BlockSpec((SGU_BLOCK, e), lambda n: (0, 0))],
        out_specs=pl.BlockSpec((SGU_BLOCK, e), lambda n: (n, 0)),
        out_shape=jax.ShapeDtypeStruct((s, e), BF16),
        scratch_shapes=[pltpu.VMEM((SGU_GROUPS, SGU_BLOCK, SGU_BLOCK), BF16)],
        compiler_params=_params(("arbitrary",)),
    )(uvz, norm_g, w_s, bias_full)


def _sgu_mid_bwd(uvz, dy, norm_g, w_s, bias_full, name):
    s, e3 = uvz.shape
    e = e3 // 3
    gd = e // SGU_GROUPS
    nb = s // SGU_BLOCK

    def body(uvz_ref, dy_ref, g_ref, w_ref, b_ref, d_ref, dw_ref, db_ref, dg_ref, wsc, wtsc, dvh_sc, dbacc):
        n = pl.program_id(0)

        @pl.when(n == 0)
        def _():
            msk = _sgu_mask()
            for gi in range(SGU_GROUPS):
                wm = jnp.where(msk, w_ref[gi], 0.0)
                wsc[gi] = wm.astype(BF16)
                wtsc[gi] = wm.T.astype(BF16)
            dw_ref[...] = jnp.zeros_like(dw_ref)
            dg_ref[...] = jnp.zeros_like(dg_ref)
            dbacc[...] = jnp.zeros_like(dbacc)

        v_pre = uvz_ref[:, e:2 * e]
        gv = g_ref[...]
        vhat, rstd, vb = _sgu_norm(v_pre, gv)
        s1 = jnp.zeros((SGU_BLOCK, 1), F32)
        s2 = jnp.zeros((SGU_BLOCK, 1), F32)
        for gi in range(SGU_GROUPS):
            lo = gi * gd
            u_pre = uvz_ref[:, lo:lo + gd]
            zg = uvz_ref[:, 2 * e + lo:2 * e + lo + gd]
            dyg = dy_ref[:, lo:lo + gd]
            ug = _gelu(u_pre)
            sig = _sigmoid(zg)
            vbg = vb[:, lo:lo + gd]
            vhg = vhat[:, lo:lo + gd]
            vm = jnp.dot(wsc[gi], vbg, preferred_element_type=F32) + b_ref[:, lo:lo + gd]
            t = dyg * (zg * sig)
            d_ref[:, lo:lo + gd] = (t * vm * _gelu_grad(u_pre)).astype(BF16)
            dvm = t * ug
            d_ref[:, 2 * e + lo:2 * e + lo + gd] = (dyg * ug * vm * (sig * (1.0 + zg * (1.0 - sig)))).astype(BF16)
            dvm_b = dvm.astype(BF16)
            dv = jnp.dot(wtsc[gi], dvm_b, preferred_element_type=F32)
            dw_ref[gi] += _dot_nt(dvm_b, vbg)
            dbacc[:, lo:lo + gd] += dvm
            dg_ref[:, lo:lo + gd] += jnp.sum(dv * vhg, axis=0, keepdims=True)
            dvh = dv * gv[:, lo:lo + gd]
            dvh_sc[:, lo:lo + gd] = dvh
            s1 = s1 + jnp.sum(dvh, axis=-1, keepdims=True)
            s2 = s2 + jnp.sum(dvh * vhg, axis=-1, keepdims=True)
        dvg = rstd * (dvh_sc[...] - s1 * (1.0 / e) - vhat * (s2 * (1.0 / e)))
        d_ref[:, e:2 * e] = (dvg * _gelu_grad(v_pre)).astype(BF16)

        @pl.when(n == nb - 1)
        def _():
            msk = _sgu_mask()
            for gi in range(SGU_GROUPS):
                dw_ref[gi] = jnp.where(msk, dw_ref[gi], 0.0)
                db_ref[gi] = jnp.sum(dbacc[:, gi * gd:(gi + 1) * gd], axis=1, keepdims=True)

    return _pcall(
        body, name=name, grid=(nb,),
        in_specs=[pl.BlockSpec((SGU_BLOCK, e3), lambda n: (n, 0)),
                  pl.BlockSpec((SGU_BLOCK, e), lambda n: (n, 0)),
                  pl.BlockSpec((1, e), lambda n: (0, 0)),
                  pl.BlockSpec((SGU_GROUPS, SGU_BLOCK, SGU_BLOCK), lambda n: (0, 0, 0)),
                  pl.BlockSpec((SGU_BLOCK, e), lambda n: (0, 0))],
        out_specs=[pl.BlockSpec((SGU_BLOCK, e3), lambda n: (n, 0)),
                   pl.BlockSpec((SGU_GROUPS, SGU_BLOCK, SGU_BLOCK), lambda n: (0, 0, 0)),
                   pl.BlockSpec((SGU_GROUPS, SGU_BLOCK, 1), lambda n: (0, 0, 0)),
                   pl.BlockSpec((1, e), lambda n: (0, 0))],
        out_shape=[jax.ShapeDtypeStruct((s, e3), BF16),
                   jax.ShapeDtypeStruct((SGU_GROUPS, SGU_BLOCK, SGU_BLOCK), F32),
                   jax.ShapeDtypeStruct((SGU_GROUPS, SGU_BLOCK, 1), F32),
                   jax.ShapeDtypeStruct((1, e), F32)],
        scratch_shapes=[pltpu.VMEM((SGU_GROUPS, SGU_BLOCK, SGU_BLOCK), BF16),
                        pltpu.VMEM((SGU_GROUPS, SGU_BLOCK, SGU_BLOCK), BF16),
                        pltpu.VMEM((SGU_BLOCK, e), F32),
                        pltpu.VMEM((SGU_BLOCK, e), F32)],
        compiler_params=_params(("arbitrary",)),
    )(uvz, dy, norm_g, w_s, bias_full)


def _rope_tables(s):
    pos = jnp.arange(s, dtype=F32)
    inv_freq = ROPE_THETA ** (-jnp.arange(0, ROPE, 2, dtype=F32) / ROPE)
    ang = pos[:, None] * inv_freq[None, :]
    cos, sin = jnp.cos(ang), jnp.sin(ang)
    z32 = jnp.zeros((s, HALF), F32)
    z64 = jnp.zeros((s, ROPE), F32)
    ck = jnp.concatenate([cos, cos, z64], axis=1)
    s1k = jnp.concatenate([-sin, z32, z64], axis=1)
    s2k = jnp.concatenate([z32, sin, z64], axis=1)
    one = jnp.ones((s, NOPE), F32)
    zn = jnp.zeros((s, NOPE), F32)
    return (ck, s1k, s2k), (jnp.concatenate([one, ck], axis=1), jnp.concatenate([zn, s1k], axis=1),
                            jnp.concatenate([zn, s2k], axis=1))


def _rot(x, c, s1, s2):
    w = x.shape[-1]
    return x * c + pltpu.roll(x, w - HALF, 1) * s1 + pltpu.roll(x, HALF, 1) * s2


def _rms(cv, n_real):
    r = lax.rsqrt(jnp.sum(cv * cv, axis=-1, keepdims=True) * (1.0 / n_real) + NORM_EPS)
    return r, cv * r


def _mla_norm_fwd(proj, gq, gkv, tabs, name):
    s = proj.shape[0]
    ts = _pick(s, ROW_BLK)
    ck, s1k, s2k = tabs

    def body(cq_ref, ckv_ref, kr_ref, gq_ref, gkv_ref, c_ref, s1_ref, s2_ref, qn_ref, kvn_ref, kro_ref):
        _, nq = _rms(cq_ref[...], Q_RANK)
        qn_ref[...] = (nq * gq_ref[...]).astype(BF16)
        _, nkv = _rms(ckv_ref[...], KV_RANK)
        kvn_ref[...] = (nkv * gkv_ref[...]).astype(BF16)
        kro_ref[...] = _rot(kr_ref[...], c_ref[...], s1_ref[...], s2_ref[...]).astype(BF16)

    tab = pl.BlockSpec((ts, LANE), lambda i: (i, 0))
    return _pcall(
        body, name=name, grid=(s // ts,),
        in_specs=[pl.BlockSpec((ts, Q_RANK_PAD), lambda i: (i, 0)),
                  pl.BlockSpec((ts, KV_RANK), lambda i: (i, PROJ_CKV // KV_RANK)),
                  pl.BlockSpec((ts, LANE), lambda i: (i, PROJ_KR // LANE)),
                  _vec_spec(Q_RANK_PAD), _vec_spec(KV_RANK), tab, tab, tab],
        out_specs=[pl.BlockSpec((ts, Q_RANK_PAD), lambda i: (i, 0)),
                   pl.BlockSpec((ts, KV_RANK), lambda i: (i, 0)), tab],
        out_shape=[jax.ShapeDtypeStruct((s, Q_RANK_PAD), BF16), jax.ShapeDtypeStruct((s, KV_RANK), BF16),
                   jax.ShapeDtypeStruct((s, LANE), BF16)],
        compiler_params=_params(("parallel",)),
    )(proj, proj, proj, gq, gkv, ck, s1k, s2k)


def _rope_heads(q, tabs, sign, mult, out_dtype, name):
    s, w = q.shape
    ts = _pick(s, ROW_BLK)
    c, s1, s2 = tabs

    def body(q_ref, c_ref, s1_ref, s2_ref, o_ref):
        cv, s1v, s2v = mult * c_ref[...], (sign * mult) * s1_ref[...], (sign * mult) * s2_ref[...]
        for h in range(w // HEAD_PAD):
            lo = h * HEAD_PAD
            o_ref[:, lo:lo + HEAD_PAD] = _rot(q_ref[:, lo:lo + HEAD_PAD].astype(F32), cv, s1v, s2v).astype(out_dtype)

    blk = pl.BlockSpec((ts, w), lambda i: (i, 0))
    tab = pl.BlockSpec((ts, HEAD_PAD), lambda i: (i, 0))
    return _pcall(
        body, name=name, grid=(s // ts,),
        in_specs=[blk, tab, tab, tab], out_specs=blk,
        out_shape=jax.ShapeDtypeStruct((s, w), out_dtype),
        compiler_params=_params(("parallel",)),
    )(q, c, s1, s2)


def _transpose_bf16(t):
    return t.astype(F32).T.astype(BF16)


def _diag_mask(tb, transposed):
    r = lax.broadcasted_iota(jnp.int32, (tb, tb), 0) // CHUNK
    c = lax.broadcasted_iota(jnp.int32, (tb, tb), 1) // CHUNK
    return (r <= c) if transposed else (c <= r)


def _attn_fwd(q_cat, kv, kr, proj, name):
    s = q_cat.shape[0]
    tb = _pick(s, ATT_BLK)
    nb = s // tb
    zcol = PROJ_Z // V_DIM

    def body(q_ref, kn_ref, v_ref, kr_ref, z_ref, o_ref, y_ref, lse_ref, kt_sc, vx_sc, m_sc, acc_sc, sa_sc, sb_sc):
        qi = pl.program_id(1)

        @pl.when(qi == 0)
        def _():
            for b in range(nb):
                rows = slice(b * tb, (b + 1) * tb)
                kt_sc[b] = _transpose_bf16(jnp.concatenate([kn_ref[rows, :], kr_ref[rows, :]], axis=1))
                vx_sc[b] = jnp.concatenate([v_ref[rows, :], jnp.ones((tb, V_DIM), BF16)], axis=1)

        m_sc[...] = jnp.full_like(m_sc, -1e30)
        acc_sc[...] = jnp.zeros_like(acc_sc)
        sub = min(tb, ATT_SUB)

        def scores(ki, s_ref):
            s_ref[...] = jnp.dot(q_ref[...], kt_sc[ki], preferred_element_type=F32)

        def step(ki, s_ref, masked):
            for r in range(tb // sub):
                rs = slice(r * sub, (r + 1) * sub)
                sc = s_ref[rs, :]
                if masked:
                    sc = jnp.where(_diag_mask(tb, False)[rs, :], sc, -1e30)
                m_prev = m_sc[rs, :]
                m_new = jnp.maximum(m_prev, jnp.max(sc, axis=-1, keepdims=True))
                p = jnp.exp2(sc - m_new).astype(BF16)
                acc_sc[rs, :] = (jnp.exp2(m_prev - m_new) * acc_sc[rs, :]
                                 + jnp.dot(p, vx_sc[ki], preferred_element_type=F32))
                m_sc[rs, :] = m_new

        def pair(t, carry):
            scores(2 * t + 1, sb_sc)
            step(2 * t, sa_sc, False)
            scores(2 * t + 2, sa_sc)
            step(2 * t + 1, sb_sc, False)
            return carry

        scores(0, sa_sc)
        lax.fori_loop(0, qi // 2, pair, 0)

        @pl.when(qi % 2 == 1)
        def _():
            scores(qi, sb_sc)
            step(qi - 1, sa_sc, False)
            step(qi, sb_sc, True)

        @pl.when(qi % 2 == 0)
        def _():
            step(qi, sa_sc, True)

        l = acc_sc[:, V_DIM:V_DIM + 1]
        o = acc_sc[:, :V_DIM] / l
        z = z_ref[...]
        o_ref[...] = o.astype(BF16)
        y_ref[...] = (o * (z * _sigmoid(z))).astype(BF16)
        lse_cols = jnp.broadcast_to(m_sc[...] + jnp.log2(l), (tb, LANE))
        lse_ref[...] = lse_cols.T[0:1, :]

    oblk = pl.BlockSpec((tb, V_DIM), lambda h, qi: (qi, h))
    return _pcall(
        body, name=name, grid=(HEADS, nb),
        in_specs=[pl.BlockSpec((tb, HEAD_PAD), lambda h, qi: (qi, h)),
                  pl.BlockSpec((s, NOPE), lambda h, qi: (0, 2 * h)),
                  pl.BlockSpec((s, V_DIM), lambda h, qi: (0, 2 * h + 1)),
                  pl.BlockSpec((s, LANE), lambda h, qi: (0, 0)),
                  pl.BlockSpec((tb, V_DIM), lambda h, qi: (qi, zcol + h))],
        out_specs=[oblk, oblk, pl.BlockSpec((None, None, 1, tb), lambda h, qi: (h, qi, 0, 0))],
        out_shape=[jax.ShapeDtypeStruct((s, MLA_WIDTH), BF16), jax.ShapeDtypeStruct((s, MLA_WIDTH), BF16),
                   jax.ShapeDtypeStruct((HEADS, nb, 1, tb), F32)],
        scratch_shapes=[pltpu.VMEM((nb, HEAD_PAD, tb), BF16), pltpu.VMEM((nb, tb, HEAD_PAD), BF16),
                        pltpu.VMEM((tb, 1), F32), pltpu.VMEM((tb, HEAD_PAD), F32),
                        pltpu.VMEM((tb, tb), F32), pltpu.VMEM((tb, tb), F32)],
        compiler_params=_params(("parallel", "arbitrary")),
    )(q_cat, kv, kv, kr, proj)


def _attn_bwd(q_cat, kv, kr, do, o, lse, name):
    s = q_cat.shape[0]
    tb = _pick(s, ATT_BLK)
    nb = s // tb
    ln2 = math.log(2.0)

    def body(q_ref, do_ref, o_ref, lse_ref, kn_ref, v_ref, kr_ref, dq_ref, dkv_ref, dkr_ref,
             qt_sc, dot_sc, delta_sc, dqt_sc, dk_sc, dv_sc):
        ki = pl.program_id(1)

        @pl.when(ki == 0)
        def _():
            for b in range(nb):
                rows = slice(b * tb, (b + 1) * tb)
                qt_sc[b] = _transpose_bf16(q_ref[rows, :])
                do_t = do_ref[rows, :].astype(F32).T
                dot_sc[b] = do_t.astype(BF16)
                delta_sc[b] = jnp.sum(do_t * o_ref[rows, :].astype(F32).T, axis=0, keepdims=True)
            dqt_sc[...] = jnp.zeros_like(dqt_sc)

        k = jnp.concatenate([kn_ref[...], kr_ref[...]], axis=1)
        kt = _transpose_bf16(k)
        vb = v_ref[...]
        dk_sc[...] = jnp.zeros_like(dk_sc)
        dv_sc[...] = jnp.zeros_like(dv_sc)

        def step(qi, masked):
            rows = pl.ds(pl.multiple_of(qi * tb, tb), tb)
            sc_t = jnp.dot(k, qt_sc[qi], preferred_element_type=F32)
            if masked:
                sc_t = jnp.where(_diag_mask(tb, True), sc_t, -1e30)
            p_t = jnp.exp2(sc_t - lse_ref[qi])
            dp_t = jnp.dot(vb, dot_sc[qi], preferred_element_type=F32)
            ds_t = (p_t * (dp_t - delta_sc[qi])).astype(BF16)
            dv_sc[...] += jnp.dot(p_t.astype(BF16), do_ref[rows, :], preferred_element_type=F32)
            dk_sc[...] += jnp.dot(ds_t, q_ref[rows, :], preferred_element_type=F32)
            dqt_sc[qi] += jnp.dot(kt, ds_t, preferred_element_type=F32)

        step(ki, True)

        def loop_body(qi, carry):
            step(qi, False)
            return carry

        lax.fori_loop(ki + 1, nb, loop_body, 0)

        dkv_ref[:, :NOPE] = (dk_sc[:, :NOPE] * ln2).astype(BF16)
        dkv_ref[:, NOPE:] = dv_sc[...].astype(BF16)
        dkr_ref[...] = dk_sc[:, NOPE:] * ln2

        @pl.when(ki == nb - 1)
        def _():
            for b in range(nb):
                dq_ref[b * tb:(b + 1) * tb, :] = dqt_sc[b].T

    return _pcall(
        body, name=name, grid=(HEADS, nb),
        in_specs=[pl.BlockSpec((s, HEAD_PAD), lambda h, ki: (0, h)),
                  pl.BlockSpec((s, V_DIM), lambda h, ki: (0, h)),
                  pl.BlockSpec((s, V_DIM), lambda h, ki: (0, h)),
                  pl.BlockSpec((None, nb, 1, tb), lambda h, ki: (h, 0, 0, 0)),
                  pl.BlockSpec((tb, NOPE), lambda h, ki: (ki, 2 * h)),
                  pl.BlockSpec((tb, V_DIM), lambda h, ki: (ki, 2 * h + 1)),
                  pl.BlockSpec((tb, LANE), lambda h, ki: (ki, 0))],
        out_specs=[pl.BlockSpec((s, HEAD_PAD), lambda h, ki: (0, h)),
                   pl.BlockSpec((tb, HEAD_PAD), lambda h, ki: (ki, h)),
                   pl.BlockSpec((None, tb, LANE), lambda h, ki: (h, ki, 0))],
        out_shape=[jax.ShapeDtypeStruct((s, HEADS * HEAD_PAD), F32),
                   jax.ShapeDtypeStruct((s, HEADS * HEAD_PAD), BF16),
                   jax.ShapeDtypeStruct((HEADS, s, LANE), F32)],
        scratch_shapes=[pltpu.VMEM((nb, HEAD_PAD, tb), BF16), pltpu.VMEM((nb, V_DIM, tb), BF16),
                        pltpu.VMEM((nb, 1, tb), F32), pltpu.VMEM((nb, HEAD_PAD, tb), F32),
                        pltpu.VMEM((tb, HEAD_PAD), F32), pltpu.VMEM((tb, V_DIM), F32)],
        compiler_params=_params(("parallel", "arbitrary")),
    )(q_cat, do, o, lse, kv, kv, kr)


def _mla_gate_bwd(dy, o, proj, name):
    s = dy.shape[0]
    ts = _pick(s, ROW_BLK)
    zcol = PROJ_Z // V_DIM

    def body(dy_ref, o_ref, p_ref, do_ref, dz_ref):
        z = p_ref[:, PROJ_Z:]
        dyv = dy_ref[...]
        sig = _sigmoid(z)
        do_ref[...] = (dyv * (z * sig)).astype(BF16)
        dz_ref[...] = (dyv * o_ref[...].astype(F32) * (sig * (1.0 + z * (1.0 - sig)))).astype(BF16)

    blk = pl.BlockSpec((ts, MLA_WIDTH), lambda i: (i, 0))
    return _pcall(
        body, name=name, grid=(s // ts,),
        in_specs=[blk, blk, pl.BlockSpec((ts, PROJ_W), lambda i: (i, 0))],
        out_specs=[blk, blk],
        out_shape=[jax.ShapeDtypeStruct((s, MLA_WIDTH), BF16), jax.ShapeDtypeStruct((s, MLA_WIDTH), BF16)],
        compiler_params=_params(("parallel",)),
    )(dy, o, proj)


def _mla_norm_bwd(dqn, dkvn, dkr_heads, proj, gq, gkv, tabs, name):
    s = proj.shape[0]
    ts = _pick(s, ROW_BLK)
    ck, s1k, s2k = tabs

    def rms_bwd(cv, dn_in, g, n_real):
        r, n = _rms(cv, n_real)
        dg = jnp.sum(dn_in * n, axis=0, keepdims=True)
        dn = dn_in * g
        dc = r * (dn - n * (jnp.sum(dn * n, axis=-1, keepdims=True) * (1.0 / n_real)))
        return dc, dg

    def body(dqn_ref, dkvn_ref, dkr_ref, cq_ref, ckv_ref, gq_ref, gkv_ref, c_ref, s1_ref, s2_ref,
             dp_ref, dgq_ref, dgkv_ref):
        @pl.when(pl.program_id(0) == 0)
        def _():
            dgq_ref[...] = jnp.zeros_like(dgq_ref)
            dgkv_ref[...] = jnp.zeros_like(dgkv_ref)

        dcq, dgq = rms_bwd(cq_ref[...], dqn_ref[...], gq_ref[...], Q_RANK)
        dckv, dgkv = rms_bwd(ckv_ref[...], dkvn_ref[...], gkv_ref[...], KV_RANK)
        dgq_ref[...] += dgq
        dgkv_ref[...] += dgkv
        dkr = dkr_ref[0]
        for h in range(1, HEADS):
            dkr = dkr + dkr_ref[h]
        dp_ref[:, PROJ_CQ:PROJ_CKV] = dcq.astype(BF16)
        dp_ref[:, PROJ_CKV:PROJ_KR] = dckv.astype(BF16)
        dp_ref[:, PROJ_KR:PROJ_Z] = _rot(dkr, c_ref[...], -s1_ref[...], -s2_ref[...]).astype(BF16)

    tab = pl.BlockSpec((ts, LANE), lambda i: (i, 0))
    return _pcall(
        body, name=name, grid=(s // ts,),
        in_specs=[pl.BlockSpec((ts, Q_RANK_PAD), lambda i: (i, 0)),
                  pl.BlockSpec((ts, KV_RANK), lambda i: (i, 0)),
                  pl.BlockSpec((HEADS, ts, LANE), lambda i: (0, i, 0)),
                  pl.BlockSpec((ts, Q_RANK_PAD), lambda i: (i, 0)),
                  pl.BlockSpec((ts, KV_RANK), lambda i: (i, PROJ_CKV // KV_RANK)),
                  _vec_spec(Q_RANK_PAD), _vec_spec(KV_RANK), tab, tab, tab],
        out_specs=[pl.BlockSpec((ts, PROJ_Z), lambda i: (i, 0)), _vec_spec(Q_RANK_PAD), _vec_spec(KV_RANK)],
        out_shape=[jax.ShapeDtypeStruct((s, PROJ_Z), BF16), jax.ShapeDtypeStruct((1, Q_RANK_PAD), F32),
                   jax.ShapeDtypeStruct((1, KV_RANK), F32)],
        compiler_params=_params(("arbitrary",)),
    )(dqn, dkvn, dkr_heads, proj, proj, gq, gkv, ck, s1k, s2k)


def _ada_mod(cond_raw, ada_w, bias_my, name):
    nl, d, ncol = ada_w.shape
    tk = _pick(d, 512)
    nk = d // tk

    def body(c_ref, w_ref, b_ref, o_ref, acc_ref):
        kk = pl.program_id(1)

        @pl.when(kk == 0)
        def _():
            acc_ref[...] = jnp.zeros_like(acc_ref)

        cv = c_ref[...]
        cond = (cv * _sigmoid(cv)).astype(BF16)
        acc_ref[...] += jnp.dot(cond, w_ref[...].astype(BF16), preferred_element_type=F32)

        @pl.when(kk == nk - 1)
        def _():
            o_ref[...] = acc_ref[...] + b_ref[...]

    return _pcall(
        body, name=name, grid=(nl, nk),
        in_specs=[pl.BlockSpec((N_DEV, tk), lambda l, kk: (0, kk)),
                  pl.BlockSpec((None, tk, ncol), lambda l, kk: (l, kk, 0)),
                  pl.BlockSpec((None, 1, ncol), lambda l, kk: (l, 0, 0))],
        out_specs=pl.BlockSpec((None, N_DEV, ncol), lambda l, kk: (l, 0, 0)),
        out_shape=jax.ShapeDtypeStruct((nl, N_DEV, ncol), F32),
        scratch_shapes=[pltpu.VMEM((N_DEV, ncol), F32)],
        compiler_params=_params(("parallel", "arbitrary")),
    )(cond_raw, ada_w, bias_my.reshape(nl, 1, ncol))


def _adam(w, g, m, v):
    m = ADAM_B1 * m + (1.0 - ADAM_B1) * g
    v = ADAM_B2 * v + (1.0 - ADAM_B2) * (g * g)
    m_hat = m / (1.0 - ADAM_B1 ** ADAM_STEP)
    v_hat = v / (1.0 - ADAM_B2 ** ADAM_STEP)
    delta = -ADAM_LR * (m_hat / (jnp.sqrt(v_hat) + ADAM_EPS) + ADAM_WD * w)
    return delta, m, v


def _ada_bwd_adam(cond_t, dmod_cols, w, m, v, name):
    nl, d, ncol = w.shape
    tk = _pick(d, 512)

    def body(c_ref, dm_ref, w_ref, m_ref, v_ref, g_ref, d_ref, mo_ref, vo_ref):
        cv = c_ref[...]
        cond = (cv * _sigmoid(cv)).astype(BF16)
        g = jnp.dot(cond, dm_ref[...].astype(BF16), preferred_element_type=F32)
        delta, m2, v2 = _adam(w_ref[...], g, m_ref[...], v_ref[...])
        g_ref[...] = g
        d_ref[...] = delta
        mo_ref[...] = m2
        vo_ref[...] = v2

    blk = pl.BlockSpec((None, tk, ncol), lambda l, kk: (l, kk, 0))
    shp = jax.ShapeDtypeStruct((nl, d, ncol), F32)
    return _pcall(
        body, name=name, grid=(nl, d // tk),
        in_specs=[pl.BlockSpec((tk, N_DEV), lambda l, kk: (kk, 0)),
                  pl.BlockSpec((None, N_DEV, ncol), lambda l, kk: (l, 0, 0)), blk, blk, blk],
        out_specs=[blk, blk, blk, blk], out_shape=[shp, shp, shp, shp],
        compiler_params=_params(("parallel", "parallel")),
    )(cond_t, dmod_cols, w, m, v)


def _adam_reduce(recv0, recv1, w, m, v, name):
    nl, r, c = w.shape
    tr = _pick(r, 128) if r % 128 == 0 else r
    tc = _pick(c, 1024)

    def body(r0_ref, r1_ref, w_ref, m_ref, v_ref, g_ref, d_ref, mo_ref, vo_ref):
        l = pl.program_id(0)

        def run(rr):
            g = rr[0].astype(F32)
            for sidx in range(1, N_DEV):
                g = g + rr[sidx].astype(F32)
            delta, m2, v2 = _adam(w_ref[...], g, m_ref[...], v_ref[...])
            g_ref[...] = g
            d_ref[...] = delta
            mo_ref[...] = m2
            vo_ref[...] = v2

        @pl.when(l == 0)
        def _():
            run(r0_ref)

        @pl.when(l == 1)
        def _():
            run(r1_ref)

    rblk = pl.BlockSpec((N_DEV, tr, tc), lambda l, i, j: (0, i, j))
    blk = pl.BlockSpec((None, tr, tc), lambda l, i, j: (l, i, j))
    shp = jax.ShapeDtypeStruct((nl, r, c), F32)
    return _pcall(
        body, name=name, grid=(nl, r // tr, c // tc),
        in_specs=[rblk, rblk, blk, blk, blk],
        out_specs=[blk, blk, blk, blk], out_shape=[shp, shp, shp, shp],
        compiler_params=_params(("arbitrary", "parallel", "parallel")),
    )(recv0.reshape(N_DEV, r, c), recv1.reshape(N_DEV, r, c), w, m, v)


def _adam_small(gathered, w, m, v, name):
    r = w.shape[0]
    tr = _pick(r, 512) if r % 512 == 0 else r

    def body(p_ref, w_ref, m_ref, v_ref, g_ref, d_ref, mo_ref, vo_ref):
        g = p_ref[0]
        for sidx in range(1, N_DEV):
            g = g + p_ref[sidx]
        delta, m2, v2 = _adam(w_ref[...], g, m_ref[...], v_ref[...])
        g_ref[...] = g
        d_ref[...] = delta
        mo_ref[...] = m2
        vo_ref[...] = v2

    blk = pl.BlockSpec((tr, LANE), lambda i: (i, 0))
    shp = jax.ShapeDtypeStruct((r, LANE), F32)
    return _pcall(
        body, name=name, grid=(r // tr,),
        in_specs=[pl.BlockSpec((N_DEV, tr, LANE), lambda i: (0, i, 0)), blk, blk, blk],
        out_specs=[blk, blk, blk, blk], out_shape=[shp, shp, shp, shp],
        compiler_params=_params(("parallel",)),
    )(gathered, w, m, v)


def _my_place():
    x, y, c = lax.axis_index("x"), lax.axis_index("y"), lax.axis_index("c")
    return x, y, c, 4 * x + 2 * y + c


def _peer(x, y, c, k):
    px = 1 - x if (k >> 2) & 1 else x
    py = 1 - y if (k >> 1) & 1 else y
    pc = 1 - c if k & 1 else c
    return (px, py, pc), 4 * px + 2 * py + pc


def _slab(ref, shape, kind, p):
    r, cd = shape
    if kind == "row":
        return ref.at[pl.ds(pl.multiple_of(p * r, SUBLANE), r), :]
    return ref.at[:, pl.ds(pl.multiple_of(p * cd, LANE), cd)]


def _exchange_layout(arrays, kinds, gather):
    shard_shapes, dst_kinds, out_shapes = [], [], []
    for a, kind in zip(arrays, kinds):
        r, cd = a.shape
        if gather:
            shard, dst_kind = (r, cd), kind
        else:
            shard, dst_kind = ((r // N_DEV, cd) if kind == "row" else (r, cd // N_DEV)), "row"
        shard_shapes.append(shard)
        dst_kinds.append(dst_kind)
        full = (shard[0] * N_DEV, shard[1]) if dst_kind == "row" else (shard[0], shard[1] * N_DEV)
        out_shapes.append(jax.ShapeDtypeStruct(full, a.dtype))
    return shard_shapes, dst_kinds, out_shapes


def _exchange_copies(ins, outs, send_sems, recv_sems, sem_of, layout, kinds, gather):
    shard_shapes, dst_kinds, _ = layout
    x, y, c, me = _my_place()

    def src_for(a, p):
        return ins[a] if gather else _slab(ins[a], shard_shapes[a], kinds[a], p)

    def dst_slot(a, p):
        return _slab(outs[a], shard_shapes[a], dst_kinds[a], p)

    def local(a, sem):
        return pltpu.make_async_copy(src_for(a, me), dst_slot(a, me), sem)

    def remote(a, k, slot):
        peer, pidx = _peer(x, y, c, k)
        return pltpu.make_async_remote_copy(
            src_ref=src_for(a, pidx), dst_ref=dst_slot(a, me if slot == "mine" else pidx),
            send_sem=send_sems.at[sem_of(a, k)], recv_sem=recv_sems.at[sem_of(a, k)],
            device_id=peer, device_id_type=MESH)

    return local, remote


def _place_own(src, src_kind, slab_shape, dst_kind, full, name):
    r, cd = slab_shape
    tr = _pick(r, 512)
    nr = r // tr
    _, _, _, me = _my_place()

    def body(me_ref, s_ref, o_ref):
        o_ref[...] = s_ref[...].astype(o_ref.dtype)

    def where(kind):
        if kind is None:
            return lambda i, me_ref: (i, 0)
        if kind == "row":
            return lambda i, me_ref: (me_ref[0] * nr + i, 0)
        return lambda i, me_ref: (i, me_ref[0])

    return _pcall(
        body, name=name,
        grid_spec=pltpu.PrefetchScalarGridSpec(
            num_scalar_prefetch=1, grid=(nr,),
            in_specs=[pl.BlockSpec((tr, cd), where(src_kind))],
            out_specs=pl.BlockSpec((tr, cd), where(dst_kind))),
        out_shape=jax.ShapeDtypeStruct(full.shape, full.dtype),
        compiler_params=_params(("arbitrary",)),
    )(jnp.reshape(me, (1,)).astype(jnp.int32), src)


def _landing_with_own_slab(arrays, kinds, gather, layout, name=None):
    _, _, _, me = _my_place()
    lands = []
    for a in range(len(arrays)):
        (r, cd), dst_kind, full = layout[0][a], layout[1][a], layout[2][a]
        if name is not None:
            lands.append(_place_own(arrays[a], None if gather else kinds[a], (r, cd), dst_kind, full, name))
            continue
        if gather:
            piece = arrays[a]
        elif kinds[a] == "row":
            piece = lax.dynamic_slice_in_dim(arrays[a], me * r, r, axis=0)
        else:
            piece = lax.dynamic_slice_in_dim(arrays[a], me * cd, cd, axis=1)
        at = (me * r, 0) if dst_kind == "row" else (0, me * cd)
        lands.append(lax.dynamic_update_slice(lax.empty(full.shape, full.dtype), piece, at))
    return lands


def _exchange(arrays, kinds, gather, name):
    n = len(arrays)
    layout = _exchange_layout(arrays, kinds, gather)
    lands = _landing_with_own_slab(arrays, kinds, gather, layout)

    def body(*refs):
        ins, outs = refs[:n], refs[2 * n:3 * n]
        send_sems, recv_sems = refs[3 * n:]
        _, remote = _exchange_copies(ins, outs, send_sems, recv_sems,
                                     lambda a, k: a * (N_DEV - 1) + k - 1, layout, kinds, gather)
        for a in range(n):
            for k in range(1, N_DEV):
                remote(a, k, "mine").start()
        for a in range(n):
            for k in range(1, N_DEV):
                arrival = remote(a, k, "theirs")
                arrival.wait_send()
                arrival.wait_recv()

    anyspec = pl.BlockSpec(memory_space=pl.ANY)
    outs = _pcall(
        body, comm=True, name=name,
        in_specs=[anyspec] * (2 * n), out_specs=[anyspec] * n, out_shape=layout[2],
        input_output_aliases={n + a: a for a in range(n)},
        scratch_shapes=[pltpu.SemaphoreType.DMA((n * (N_DEV - 1),)), pltpu.SemaphoreType.DMA((n * (N_DEV - 1),))],
    )(*arrays, *lands)
    return list(outs)


HBM_SPEC = pl.BlockSpec(memory_space=pltpu.HBM)
SEM_SPEC = pl.BlockSpec(memory_space=pltpu.SEMAPHORE)
ANY_SPEC = pl.BlockSpec(memory_space=pl.ANY)
DATAFLOW = pltpu.SideEffectType.DATAFLOW_SIDE_EFFECTING


def _exchange_start(arrays, kinds, gather, name, after, carry=()):
    n, nc = len(arrays), len(carry)
    layout = _exchange_layout(arrays, kinds, gather)
    lands = _landing_with_own_slab(arrays, kinds, gather, layout, "place_own")

    def body(*refs):
        ins, outs = refs[:n], refs[n:2 * n]
        send_sems, recv_sems = refs[2 * n + nc + 1], refs[2 * n + nc + 2]
        token = refs[2 * n + nc + 3 + 2 * n + nc]
        _, remote = _exchange_copies(ins, outs, send_sems, recv_sems, lambda a, k: a, layout, kinds, gather)
        for a in range(n):
            for k in range(1, N_DEV):
                remote(a, k, "mine").start()
        token[...] = jnp.zeros_like(token)

    passed = list(arrays) + lands + list(carry)
    res = pl.pallas_call(
        body, name=name,
        out_shape=(pltpu.SemaphoreType.DMA((n,)), pltpu.SemaphoreType.DMA((n,)),
                   *[pltpu.HBM(t.shape, t.dtype) for t in passed], jax.ShapeDtypeStruct((SUBLANE, LANE), F32)),
        in_specs=[HBM_SPEC] * (2 * n + nc) + [ANY_SPEC],
        out_specs=(SEM_SPEC, SEM_SPEC, *([HBM_SPEC] * (2 * n + nc)), pl.BlockSpec(memory_space=pltpu.VMEM)),
        input_output_aliases={i: 2 + i for i in range(2 * n + nc)},
        compiler_params=pltpu.CompilerParams(has_side_effects=DATAFLOW),
    )(*[pltpu.with_memory_space_constraint(t, pltpu.HBM) for t in passed], after)
    handle = (res[0], res[1], list(res[2:2 + n]), list(res[2 + n:2 + 2 * n]), tuple(kinds), gather)
    return handle, res[-1], list(res[2 + 2 * n:2 + 2 * n + nc])


def _exchange_wait(handle, name, after):
    send_sems, recv_sems, ins_thru, lands_thru, kinds, gather = handle
    n = len(ins_thru)
    layout = _exchange_layout(ins_thru, kinds, gather)

    def body(*refs):
        ins, outs = refs[:n], refs[n:2 * n]
        s_sems, r_sems = refs[2 * n], refs[2 * n + 1]
        _, remote = _exchange_copies(ins, outs, s_sems, r_sems, lambda a, k: a, layout, kinds, gather)
        for a in range(n):
            for k in range(1, N_DEV):
                arrival = remote(a, k, "theirs")
                arrival.wait_send()
                arrival.wait_recv()

    res = pl.pallas_call(
        body, name=name,
        out_shape=[pltpu.HBM(t.shape, t.dtype) for t in ins_thru + lands_thru],
        in_specs=[HBM_SPEC] * (2 * n) + [SEM_SPEC, SEM_SPEC, ANY_SPEC],
        out_specs=[HBM_SPEC] * (2 * n),
        input_output_aliases={i: i for i in range(2 * n)},
        compiler_params=pltpu.CompilerParams(has_side_effects=DATAFLOW),
    )(*ins_thru, *lands_thru, send_sems, recv_sems, after)
    return list(res[n:2 * n])


STAGE1_KS = (1, 2, 4, 6)
FORWARD_KS = (2, 4, 6)


def _gather2_copies(lands, shard_shapes, kinds):
    x, y, c, me = _my_place()

    def slab(a, p):
        return _slab(lands[a], shard_shapes[a], kinds[a], p)

    def stage1(a, k, sems, arriving):
        peer, pidx = _peer(x, y, c, k)
        s = slab(a, pidx if arriving else me)
        return pltpu.make_async_remote_copy(src_ref=s, dst_ref=s, send_sem=sems[0].at[a], recv_sem=sems[1].at[a],
                                            device_id=peer, device_id_type=MESH)

    def stage2(a, k, sems, arriving):
        sib, _ = _peer(x, y, c, 1)
        _, mine = _peer(x, y, c, k)
        _, theirs = _peer(x, y, 1 - c, k)
        s = slab(a, theirs if arriving else mine)
        return pltpu.make_async_remote_copy(src_ref=s, dst_ref=s, send_sem=sems[0].at[a], recv_sem=sems[1].at[a],
                                            device_id=sib, device_id_type=MESH)

    return stage1, stage2


def _gather2_call(lands, sems_in, name, after, make_body, returns_sems):
    n = len(lands)
    n_in = len(sems_in)

    def body(*refs):
        land_refs = refs[:n]
        in_sems = refs[n:n + n_in]
        rest = refs[n + n_in + 1:]
        out_sems = rest[:2] if returns_sems else ()
        make_body(land_refs, in_sems, out_sems)
        if returns_sems:
            token = rest[2 + n]
            token[...] = jnp.zeros_like(token)

    sem_shapes = (pltpu.SemaphoreType.DMA((n,)), pltpu.SemaphoreType.DMA((n,))) if returns_sems else ()
    tok_shape = (jax.ShapeDtypeStruct((SUBLANE, LANE), F32),) if returns_sems else ()
    n_sem_out = len(sem_shapes)
    res = pl.pallas_call(
        body, name=name,
        out_shape=(*sem_shapes, *[pltpu.HBM(t.shape, t.dtype) for t in lands], *tok_shape),
        in_specs=[HBM_SPEC] * n + [SEM_SPEC] * n_in + [ANY_SPEC],
        out_specs=(*([SEM_SPEC] * n_sem_out), *([HBM_SPEC] * n),
                   *([pl.BlockSpec(memory_space=pltpu.VMEM)] if returns_sems else [])),
        input_output_aliases={i: n_sem_out + i for i in range(n)},
        compiler_params=pltpu.CompilerParams(has_side_effects=DATAFLOW),
    )(*[pltpu.with_memory_space_constraint(t, pltpu.HBM) for t in lands], *sems_in, after)
    sems_out = tuple(res[:n_sem_out])
    lands_thru = list(res[n_sem_out:n_sem_out + n])
    return sems_out, lands_thru, (res[-1] if returns_sems else None)


def _gather2_start(shards, kinds, name, after):
    n = len(shards)
    layout = _exchange_layout(shards, kinds, True)
    lands = _landing_with_own_slab(shards, kinds, True, layout, "place_own")

    def make_body(land_refs, in_sems, out_sems):
        stage1, _ = _gather2_copies(land_refs, layout[0], kinds)
        for a in range(n):
            for k in STAGE1_KS:
                stage1(a, k, out_sems, False).start()

    sems, lands, token = _gather2_call(lands, (), name, after, make_body, True)
    return (sems, lands, layout[0], tuple(kinds)), token


def _gather2_forward(handle, name, after):
    sems1, lands, shard_shapes, kinds = handle
    n = len(lands)

    def make_body(land_refs, in_sems, out_sems):
        stage1, stage2 = _gather2_copies(land_refs, shard_shapes, kinds)
        for a in range(n):
            for k in STAGE1_KS:
                arrival = stage1(a, k, in_sems, True)
                arrival.wait_send()
                arrival.wait_recv()
        for a in range(n):
            for k in FORWARD_KS:
                stage2(a, k, out_sems, False).start()

    sems2, lands, token = _gather2_call(lands, sems1, name, after, make_body, True)
    return (sems2, lands, shard_shapes, kinds), token


def _gather2_wait(handle, name, after):
    sems2, lands, shard_shapes, kinds = handle
    n = len(lands)

    def make_body(land_refs, in_sems, out_sems):
        _, stage2 = _gather2_copies(land_refs, shard_shapes, kinds)
        for a in range(n):
            for k in FORWARD_KS:
                arrival = stage2(a, k, in_sems, True)
                arrival.wait_send()
                arrival.wait_recv()

    _, lands, _ = _gather2_call(lands, sems2, name, after, make_body, False)
    return lands


def _pad_mla_w_in(w):
    d = w.shape[0]
    z = lambda n: jnp.zeros((d, n), w.dtype)
    o1, o2, o3 = Q_RANK, Q_RANK + KV_RANK, Q_RANK + KV_RANK + ROPE
    return jnp.concatenate([w[:, :o1], z(Q_RANK_PAD - Q_RANK), w[:, o1:o2], w[:, o2:o3], z(LANE - ROPE), w[:, o3:]], axis=1)


def _unpad_mla_w_in(g):
    return jnp.concatenate([g[:, :Q_RANK], g[:, PROJ_CKV:PROJ_KR], g[:, PROJ_KR:PROJ_KR + ROPE], g[:, PROJ_Z:]], axis=1)


def _pad_w_uq(w):
    w3 = w.reshape(Q_RANK, HEADS, NOPE + ROPE)
    w3 = jnp.pad(w3, ((0, Q_RANK_PAD - Q_RANK), (0, 0), (0, HEAD_PAD - NOPE - ROPE)))
    return w3.reshape(Q_RANK_PAD, HEADS * HEAD_PAD)


def _unpad_w_uq(g):
    return g[:Q_RANK].reshape(Q_RANK, HEADS, HEAD_PAD)[:, :, :NOPE + ROPE].reshape(Q_RANK, HEADS * (NOPE + ROPE))


def _pack(pieces):
    flat = [p.reshape(-1).astype(F32) for p in pieces]
    tot = sum(f.shape[0] for f in flat)
    unit = SUBLANE * LANE
    padn = (-tot) % unit
    if padn:
        flat.append(jnp.zeros((padn,), F32))
    return jnp.concatenate(flat).reshape(-1, LANE)


def _unpack(packed, shapes):
    flat = packed.reshape(-1)
    out, off = [], 0
    for shp in shapes:
        nel = int(np.prod(shp))
        out.append(flat[off:off + nel].reshape(shp))
        off += nel
    return out


WEIGHTS = ['ada_w', 'ada_b', 'pre_g', 'post_g', 'sgu_w_in', 'sgu_norm_g', 'sgu_w_s', 'sgu_b_s', 'sgu_w_out',
           'mla_w_in', 'mla_q_norm_g', 'mla_kv_norm_g', 'mla_w_uq', 'mla_w_ukv', 'mla_w_out']
INPUTS = ['x', 'c'] + WEIGHTS + ['loss_target'] + ['m_' + n for n in WEIGHTS] + ['v_' + n for n in WEIGHTS]


def kernel(x, c, ada_w, ada_b, pre_g, post_g, sgu_w_in, sgu_norm_g, sgu_w_s, sgu_b_s, sgu_w_out, mla_w_in, mla_q_norm_g, mla_kv_norm_g, mla_w_uq, mla_w_ukv, mla_w_out, loss_target, m_ada_w, m_ada_b, m_pre_g, m_post_g, m_sgu_w_in, m_sgu_norm_g, m_sgu_w_s, m_sgu_b_s, m_sgu_w_out, m_mla_w_in, m_mla_q_norm_g, m_mla_kv_norm_g, m_mla_w_uq, m_mla_w_ukv, m_mla_w_out, v_ada_w, v_ada_b, v_pre_g, v_post_g, v_sgu_w_in, v_sgu_norm_g, v_sgu_w_s, v_sgu_b_s, v_sgu_w_out, v_mla_w_in, v_mla_q_norm_g, v_mla_kv_norm_g, v_mla_w_uq, v_mla_w_ukv, v_mla_w_out):
    given = locals()
    A = {name: given[name] for name in INPUTS}
    x0 = A['x'][0]
    tgt = A['loss_target'][0]
    s, d = x0.shape
    e = 2 * d
    ncol = 3 * d // N_DEV
    _, _, _, me = _my_place()
    ktabs, qtabs = _rope_tables(s)

    gains = jnp.zeros((SUBLANE, LANE), F32)
    gains = gains.at[0:2, :Q_RANK // N_DEV].set(A['mla_q_norm_g'])
    gains = gains.at[2:4, :KV_RANK // N_DEV].set(A['mla_kv_norm_g'])
    c8 = jnp.broadcast_to(A['c'], (SUBLANE, d))
    cg, gg = _exchange([c8, gains], ["row", "row"], True, "ag_cond")
    cond_raw = cg.reshape(N_DEV, SUBLANE, d)[:, 0, :]
    gg = gg.reshape(N_DEV, SUBLANE, LANE)
    gq_full = jnp.transpose(gg[:, 0:2, :Q_RANK // N_DEV], (1, 0, 2)).reshape(N_MIX, Q_RANK)
    gkv_full = jnp.transpose(gg[:, 2:4, :KV_RANK // N_DEV], (1, 0, 2)).reshape(N_MIX, KV_RANK)
    gq_pad = jnp.pad(gq_full, ((0, 0), (0, Q_RANK_PAD - Q_RANK)))

    bias_my = lax.dynamic_slice_in_dim(A['ada_b'], me * ncol, ncol, axis=1)
    mod_part = _ada_mod(cond_raw, A['ada_w'], bias_my, "ada_mod")
    send = jnp.pad(jnp.transpose(mod_part, (1, 0, 2)), ((0, 0), (0, SUBLANE - DEPTH), (0, 0)))
    (rb,) = _exchange([send.reshape(N_DEV * SUBLANE, ncol)], ["row"], False, "a2a_mod")

    token = rb
    gathers = {}
    for i in range(DEPTH):
        j = i // N_MIX
        if i % N_MIX == 0:
            parts = [("in", [A['sgu_w_in'][j]], ["col"]), ("out", [A['sgu_w_out'][j]], ["row"])]
        else:
            parts = [("all", [A['mla_w_in'][j], A['mla_w_uq'][j], A['mla_w_ukv'][j], A['mla_w_out'][j]],
                      ["col", "col", "col", "row"])]
        for part, shards, kinds in parts:
            gathers[(i, part)], token = _gather2_start([t.astype(BF16) for t in shards], kinds,
                                                       f"ag_start_{i}_{part}", token)

    def forward_gathers(i, after):
        for key in [k for k in gathers if k[0] == i]:
            gathers[key], _ = _gather2_forward(gathers[key], f"ag_forward_{key[0]}_{key[1]}", after)

    gathers[(0, "in")], token = _gather2_forward(gathers[(0, "in")], "ag_forward_0_in", token)
    mod = jnp.transpose(rb.reshape(N_DEV, SUBLANE, ncol)[:, :DEPTH, :], (1, 0, 2)).reshape(DEPTH, 3 * d) + token[0, 0]
    shift = [mod[i:i + 1, :d] for i in range(DEPTH)]
    scale = [mod[i:i + 1, d:2 * d] for i in range(DEPTH)]
    gate = [mod[i:i + 1, 2 * d:] for i in range(DEPTH)]

    saved = []
    x = x0
    for i in range(DEPTH):
        j = i // N_MIX
        h = _pre_fwd(x, A['pre_g'][i:i + 1], scale[i], shift[i], f"pre_fwd")
        if i % N_MIX == 0:
            (w_in,) = _gather2_wait(gathers[(i, "in")], f"ag_wait_{i}_in", h)
            uvz = _mm(h, w_in, "nn", F32, "sgu_in")
            if i == 0:
                gathers[(0, "out")], _ = _gather2_forward(gathers[(0, "out")], "ag_forward_0_out", uvz)
            bias_full = jnp.repeat(A['sgu_b_s'][j].T, e // SGU_GROUPS, axis=1)
            ng = A['sgu_norm_g'][j:j + 1]
            y = _sgu_mid_fwd(uvz, ng, A['sgu_w_s'][j], bias_full, "sgu_mid_fwd")
            forward_gathers(i + 1, y)
            (w_out,) = _gather2_wait(gathers[(i, "out")], f"ag_wait_{i}_out", y)
            out = _mm(y, w_out, "nn", F32, "sgu_out")
            saved.append(dict(x=x, h=h, uvz=uvz, y=y, out=out, w_in=w_in, w_out=w_out, bias=bias_full, ng=ng))
        else:
            w_in, w_uq, w_ukv, w_out = _gather2_wait(gathers[(i, "all")], f"ag_wait_{i}_all", h)
            w_in = _pad_mla_w_in(w_in)
            w_uq = _pad_w_uq(w_uq)
            gq, gkv = gq_pad[j:j + 1], gkv_full[j:j + 1]
            proj = _mm(h, w_in, "nn", F32, "mla_in")
            qn, kvn, kr = _mla_norm_fwd(proj, gq, gkv, ktabs, "mla_norm_fwd")
            q = _mm(qn, w_uq, "nn", F32, "mla_uq")
            q_cat = _rope_heads(q, qtabs, 1.0, SOFTMAX_SCALE * LOG2_E, BF16, "rope_q_fwd")
            kv = _mm(kvn, w_ukv, "nn", BF16, "mla_ukv")
            o, y, lse = _attn_fwd(q_cat, kv, kr, proj, "attn_fwd")
            forward_gathers(i + 1, y)
            out = _mm(y, w_out, "nn", F32, "mla_out")
            saved.append(dict(x=x, h=h, proj=proj, qn=qn, kvn=kvn, kr=kr, q_cat=q_cat, kv=kv, o=o, y=y, lse=lse,
                              out=out, w_in=w_in, w_uq=w_uq, w_ukv=w_ukv, w_out=w_out, gq=gq, gkv=gkv))
        x = _post_fwd(x, out, gate[i], A['post_g'][i:i + 1], "post_fwd")

    dx, loss_row = _loss_head(x, tgt, "loss_head")
    loss = lax.psum(loss_row[0, 0], ("x", "y", "c"))

    d_shift, d_scale, d_gate = [None] * DEPTH, [None] * DEPTH, [None] * DEPTH
    d_pre, d_post = [None] * DEPTH, [None] * DEPTH
    scatters, scatters_out = [None] * DEPTH, [None] * DEPTH
    small = {}
    for i in reversed(range(DEPTH)):
        j = i // N_MIX
        sv = saved[i]
        dout, d_gate[i], d_post[i] = _post_bwd(dx, sv['out'], gate[i], A['post_g'][i:i + 1], "post_bwd")
        if i % N_MIX == 0:
            dy = _mm(dout, sv['w_out'], "nt", F32, "sgu_out_dx")
            g_w_out = _mm(sv['y'], dout, "tn", BF16, "sgu_out_dw")
            scatters_out[i], token, (dy,) = _exchange_start([g_w_out], ["row"], False, f"rs_out_start_{i}", token,
                                                            carry=[dy])
            duvz, dws, dbs, dng = _sgu_mid_bwd(sv['uvz'], dy, sv['ng'], A['sgu_w_s'][j], sv['bias'], "sgu_mid_bwd")
            g_w_in = _mm(sv['h'], duvz, "tn", BF16, "sgu_in_dw")
            scatters[i], token, (duvz,) = _exchange_start([g_w_in], ["col"], False, f"rs_start_{i}", token,
                                                          carry=[duvz])
            dh = _mm(duvz, sv['w_in'], "nt", F32, "sgu_in_dx")
            small[('sgu', j)] = (dws, dbs.reshape(SGU_GROUPS, SGU_BLOCK), dng)
        else:
            dy = _mm(dout, sv['w_out'], "nt", F32, "mla_out_dx")
            g_w_out = _mm(sv['y'], dout, "tn", BF16, "mla_out_dw")
            scatters_out[i], token, (dy,) = _exchange_start([g_w_out], ["row"], False, f"rs_out_start_{i}", token,
                                                            carry=[dy])
            do, dz = _mla_gate_bwd(dy, sv['o'], sv['proj'], "mla_gate_bwd")
            dq, dkv, dkr_heads = _attn_bwd(sv['q_cat'], sv['kv'], sv['kr'], do, sv['o'], sv['lse'], "attn_bwd")
            dq_b = _rope_heads(dq, qtabs, -1.0, SOFTMAX_SCALE, BF16, "rope_q_bwd")
            dqn = _mm(dq_b, sv['w_uq'], "nt", F32, "mla_uq_dx")
            g_w_uq = _unpad_w_uq(_mm(sv['qn'], dq_b, "tn", BF16, "mla_uq_dw"))
            dkvn = _mm(dkv, sv['w_ukv'], "nt", F32, "mla_ukv_dx")
            g_w_ukv = _mm(sv['kvn'], dkv, "tn", BF16, "mla_ukv_dw")
            dpa, dgq, dgkv = _mla_norm_bwd(dqn, dkvn, dkr_heads, sv['proj'], sv['gq'], sv['gkv'], ktabs, "mla_norm_bwd")
            dproj = jnp.concatenate([dpa, dz], axis=1)
            g_w_in = _unpad_mla_w_in(_mm(sv['h'], dproj, "tn", BF16, "mla_in_dw"))
            scatters[i], token, (dproj,) = _exchange_start(
                [g_w_in, g_w_uq, g_w_ukv], ["col", "col", "col"], False, f"rs_start_{i}", token, carry=[dproj])
            dh = _mm(dproj, sv['w_in'], "nt", F32, "mla_in_dx")
            small[('mla', j)] = (dgq[:, :Q_RANK], dgkv)
        dx, d_shift[i], d_scale[i], d_pre[i] = _pre_bwd(dh, sv['x'], dx, A['pre_g'][i:i + 1], scale[i], "pre_bwd")

    res = {}

    def big(name, recv0, recv1):
        res[name] = _adam_reduce(recv0, recv1, A[name], A['m_' + name], A['v_' + name], "adam_" + name)

    def finish(i, after):
        return (_exchange_wait(scatters[i], f"rs_wait_{i}", after)
                + _exchange_wait(scatters_out[i], f"rs_out_wait_{i}", after))

    recv_mla = {j: finish(N_MIX * j + 1, dx) for j in reversed(range(N_MIX))}
    for idx_w, name in enumerate(['mla_w_in', 'mla_w_uq', 'mla_w_ukv', 'mla_w_out']):
        big(name, recv_mla[0][idx_w], recv_mla[1][idx_w])

    dmod = jnp.concatenate([jnp.concatenate([d_shift[i], d_scale[i], d_gate[i]], axis=1) for i in range(DEPTH)], axis=0)
    pieces = [dmod, jnp.concatenate(d_pre, axis=0), jnp.concatenate(d_post, axis=0),
              jnp.concatenate([small[('sgu', j)][2] for j in range(N_MIX)], axis=0),
              jnp.stack([small[('sgu', j)][0] for j in range(N_MIX)]),
              jnp.stack([small[('sgu', j)][1] for j in range(N_MIX)]),
              jnp.concatenate([small[('mla', j)][0] for j in range(N_MIX)], axis=0),
              jnp.concatenate([small[('mla', j)][1] for j in range(N_MIX)], axis=0)]
    small_names = ['ada_b', 'pre_g', 'post_g', 'sgu_norm_g', 'sgu_w_s', 'sgu_b_s', 'mla_q_norm_g', 'mla_kv_norm_g']
    full_shapes = [p.shape for p in pieces]
    packed = _pack(pieces)
    rows = packed.shape[0]
    (gathered,) = _exchange([packed], ["row"], True, "ag_small")
    gathered = gathered.reshape(N_DEV, rows, LANE)

    def full_size(prefix):
        out = []
        for name, shp in zip(small_names, full_shapes):
            t = A[prefix + name]
            if t.shape != tuple(shp):
                t = lax.dynamic_update_slice_in_dim(jnp.zeros(shp, F32), t, me * t.shape[1], axis=1)
            out.append(t)
        return _pack(out)

    sm = _adam_small(gathered, full_size(''), full_size('m_'), full_size('v_'), "adam_small")
    sm = [_unpack(t, full_shapes) for t in sm]
    for k_out in range(4):
        for name, val in zip(small_names, sm[k_out]):
            t = A[name]
            if t.shape != val.shape:
                val = lax.dynamic_slice_in_dim(val, me * t.shape[1], t.shape[1], axis=1)
            res.setdefault(name, [None] * 4)[k_out] = val

    nrow_dmod = DEPTH * 3 * d // LANE
    dmod_all = gathered[:, :nrow_dmod, :].reshape(N_DEV, DEPTH, 3 * d)
    dmod_cols = jnp.transpose(lax.dynamic_slice_in_dim(dmod_all, me * ncol, ncol, axis=2), (1, 0, 2))
    res['ada_w'] = _ada_bwd_adam(jnp.transpose(cond_raw), dmod_cols, A['ada_w'], A['m_ada_w'], A['v_ada_w'], "ada_bwd")

    recv_sgu = {j: finish(N_MIX * j, res['ada_w'][0]) for j in reversed(range(N_MIX))}
    for idx_w, name in enumerate(['sgu_w_in', 'sgu_w_out']):
        big(name, recv_sgu[0][idx_w], recv_sgu[1][idx_w])

    outs = [loss, dx[None]]
    for k_out in range(4):
        outs += [res[n][k_out] for n in WEIGHTS]
    return tuple(outs)
```

```python
import functools
import math

import numpy as np
import jax
import jax.numpy as jnp
from jax import lax
from jax.experimental import pallas as pl
from jax.experimental.pallas import tpu as pltpu

F32 = jnp.float32
BF16 = jnp.bfloat16
MESH = pl.DeviceIdType.MESH

N_DEV = 8
DEPTH = 4
N_MIX = 2
NORM_EPS = 1e-6
CHUNK = 64
SGU_BLOCK = 128
SGU_GROUPS = 16
HEADS = 16
Q_RANK = 448
Q_RANK_PAD = 512
KV_RANK = 512
NOPE = 128
ROPE = 64
HALF = ROPE // 2
V_DIM = 128
HEAD_PAD = 256
ROPE_THETA = 10000.0
MLA_WIDTH = HEADS * V_DIM
LANE = 128
SUBLANE = 8
PROJ_CQ = 0
PROJ_CKV = Q_RANK_PAD
PROJ_KR = Q_RANK_PAD + KV_RANK
PROJ_Z = PROJ_KR + LANE
PROJ_W = PROJ_Z + MLA_WIDTH

ADAM_LR = 0.001
ADAM_B1 = 0.9
ADAM_B2 = 0.999
ADAM_EPS = 1e-08
ADAM_WD = 0.01
ADAM_STEP = 10

VMEM_LIMIT = 56 * 1024 * 1024
ATT_BLK = 512
ATT_SUB = 128
ROW_BLK = 256
MM_TM, MM_TN, MM_TK = 1024, 1024, 2048
MM_TILE_BYTES = 40 * 1024 * 1024
SOFTMAX_SCALE = (NOPE + ROPE) ** -0.5
LOG2_E = 1.0 / math.log(2.0)
INV_SQRT2 = 1.0 / math.sqrt(2.0)
INV_SQRT_2PI = 1.0 / math.sqrt(2.0 * math.pi)


def _pcall(body, comm=False, **kw):
    return pl.pallas_call(body, **kw)


def _params(sem=None):
    return pltpu.CompilerParams(dimension_semantics=sem, vmem_limit_bytes=VMEM_LIMIT)


def _pick(dim, pref):
    if dim <= pref:
        return dim
    t = (pref // LANE) * LANE
    while t >= LANE:
        if dim % t == 0:
            return t
        t -= LANE
    return dim


def _gelu(x):
    return 0.5 * x * (1.0 + lax.erf(x * INV_SQRT2))


def _gelu_grad(x):
    return 0.5 * (1.0 + lax.erf(x * INV_SQRT2)) + x * jnp.exp(-0.5 * x * x) * INV_SQRT_2PI


def _sigmoid(x):
    return 1.0 / (1.0 + jnp.exp(-x))


def _dot_nt(a, b):
    return lax.dot_general(a, b, (((1,), (1,)), ((), ())), preferred_element_type=F32)


def _dot_tn(a, b):
    return lax.dot_general(a, b, (((0,), (0,)), ((), ())), preferred_element_type=F32)


def _mm(a, b, dims, out_dtype, name):
    if dims == "nn":
        (m, k), (k2, n) = a.shape, b.shape
    elif dims == "nt":
        (m, k), (n, k2) = a.shape, b.shape
    else:
        (k, m), (k2, n) = a.shape, b.shape
    assert k == k2, (a.shape, b.shape, dims)
    tm, tn = _pick(m, MM_TM), _pick(n, MM_TN)
    out_bytes = 2 * tm * tn * jnp.dtype(out_dtype).itemsize
    whole_k = 2 * (tm + tn) * k * a.dtype.itemsize + out_bytes <= MM_TILE_BYTES
    tk = k if whole_k else _pick(k, MM_TK)
    nk = k // tk

    def body(a_ref, b_ref, o_ref, *scratch):
        if dims == "nn":
            p = jnp.dot(a_ref[...], b_ref[...], preferred_element_type=F32)
        elif dims == "nt":
            p = _dot_nt(a_ref[...], b_ref[...])
        else:
            p = _dot_tn(a_ref[...], b_ref[...])
        if nk == 1:
            o_ref[...] = p.astype(o_ref.dtype)
            return
        acc_ref, = scratch
        kk = pl.program_id(2)

        @pl.when(kk == 0)
        def _():
            acc_ref[...] = p

        @pl.when(kk > 0)
        def _():
            acc_ref[...] += p

        @pl.when(kk == nk - 1)
        def _():
            o_ref[...] = acc_ref[...].astype(o_ref.dtype)

    if dims == "tn":
        a_spec = pl.BlockSpec((tk, tm), lambda i, j, kk: (kk, i))
    else:
        a_spec = pl.BlockSpec((tm, tk), lambda i, j, kk: (i, kk))
    if dims == "nt":
        b_spec = pl.BlockSpec((tn, tk), lambda i, j, kk: (j, kk))
    else:
        b_spec = pl.BlockSpec((tk, tn), lambda i, j, kk: (kk, j))
    return _pcall(
        body, name=name,
        grid=(m // tm, n // tn, nk),
        in_specs=[a_spec, b_spec],
        out_specs=pl.BlockSpec((tm, tn), lambda i, j, kk: (i, j)),
        out_shape=jax.ShapeDtypeStruct((m, n), out_dtype),
        scratch_shapes=[pltpu.VMEM((tm, tn), F32)] if nk > 1 else [],
        compiler_params=_params(("parallel", "parallel", "arbitrary")),
    )(a, b)


def _row_spec(ts, d):
    return pl.BlockSpec((ts, d), lambda i: (i, 0))


def _vec_spec(d):
    return pl.BlockSpec((1, d), lambda i: (0, 0))


def _pre_fwd(x, g, scale, shift, name):
    s, d = x.shape
    ts = _pick(s, ROW_BLK)

    def body(x_ref, g_ref, sc_ref, sh_ref, h_ref):
        xv = x_ref[...]
        r = lax.rsqrt(jnp.mean(xv * xv, axis=-1, keepdims=True) + NORM_EPS)
        h_ref[...] = ((xv * r * g_ref[...]) * (1.0 + sc_ref[...]) + sh_ref[...]).astype(BF16)

    return _pcall(
        body, name=name, grid=(s // ts,),
        in_specs=[_row_spec(ts, d), _vec_spec(d), _vec_spec(d), _vec_spec(d)],
        out_specs=_row_spec(ts, d),
        out_shape=jax.ShapeDtypeStruct((s, d), BF16),
        compiler_params=_params(("parallel",)),
    )(x, g, scale, shift)


def _post_fwd(x, out, gate, g, name):
    s, d = x.shape
    ts = _pick(s, ROW_BLK)

    def body(x_ref, o_ref, gate_ref, g_ref, y_ref):
        o = o_ref[...]
        r = lax.rsqrt(jnp.mean(o * o, axis=-1, keepdims=True) + NORM_EPS)
        y_ref[...] = x_ref[...] + gate_ref[...] * (o * r * g_ref[...])

    return _pcall(
        body, name=name, grid=(s // ts,),
        in_specs=[_row_spec(ts, d), _row_spec(ts, d), _vec_spec(d), _vec_spec(d)],
        out_specs=_row_spec(ts, d),
        out_shape=jax.ShapeDtypeStruct((s, d), F32),
        compiler_params=_params(("parallel",)),
    )(x, out, gate, g)


def _loss_head(xf, tgt, name):
    s, d = xf.shape
    ts = _pick(s, ROW_BLK)
    ns = s // ts

    def body(x_ref, t_ref, dx_ref, loss_ref, acc_ref):
        i = pl.program_id(0)

        @pl.when(i == 0)
        def _():
            acc_ref[...] = jnp.zeros_like(acc_ref)

        e = x_ref[...] - t_ref[...]
        dx_ref[...] = e * (1.0 / d)
        acc_ref[...] += jnp.sum(e * e, axis=0, keepdims=True)

        @pl.when(i == ns - 1)
        def _():
            tot = jnp.sum(acc_ref[...], axis=1, keepdims=True) * (0.5 / d)
            loss_ref[...] = jnp.broadcast_to(tot, loss_ref.shape)

    return _pcall(
        body, name=name, grid=(ns,),
        in_specs=[_row_spec(ts, d), _row_spec(ts, d)],
        out_specs=[_row_spec(ts, d), pl.BlockSpec((1, LANE), lambda i: (0, 0))],
        out_shape=[jax.ShapeDtypeStruct((s, d), F32), jax.ShapeDtypeStruct((1, LANE), F32)],
        scratch_shapes=[pltpu.VMEM((1, d), F32)],
        compiler_params=_params(("arbitrary",)),
    )(xf, tgt)


def _post_bwd(dxo, out, gate, g, name):
    s, d = dxo.shape
    ts = _pick(s, ROW_BLK)

    def body(dx_ref, o_ref, gate_ref, g_ref, do_ref, dgate_ref, dg_ref):
        i = pl.program_id(0)

        @pl.when(i == 0)
        def _():
            dgate_ref[...] = jnp.zeros_like(dgate_ref)
            dg_ref[...] = jnp.zeros_like(dg_ref)

        o = o_ref[...]
        dx = dx_ref[...]
        gv = g_ref[...]
        r = lax.rsqrt(jnp.mean(o * o, axis=-1, keepdims=True) + NORM_EPS)
        n = o * r
        dyn = dx * gate_ref[...]
        dgate_ref[...] += jnp.sum(dx * (n * gv), axis=0, keepdims=True)
        dg_ref[...] += jnp.sum(dyn * n, axis=0, keepdims=True)
        dn = dyn * gv
        do_ref[...] = (r * (dn - n * jnp.mean(dn * n, axis=-1, keepdims=True))).astype(BF16)

    return _pcall(
        body, name=name, grid=(s // ts,),
        in_specs=[_row_spec(ts, d), _row_spec(ts, d), _vec_spec(d), _vec_spec(d)],
        out_specs=[_row_spec(ts, d), _vec_spec(d), _vec_spec(d)],
        out_shape=[jax.ShapeDtypeStruct((s, d), BF16), jax.ShapeDtypeStruct((1, d), F32),
                   jax.ShapeDtypeStruct((1, d), F32)],
        compiler_params=_params(("arbitrary",)),
    )(dxo, out, gate, g)


def _pre_bwd(dh, x, dxo, g, scale, name):
    s, d = x.shape
    ts = _pick(s, ROW_BLK)

    def body(dh_ref, x_ref, dxo_ref, g_ref, sc_ref, dx_ref, dsh_ref, dsc_ref, dg_ref):
        i = pl.program_id(0)

        @pl.when(i == 0)
        def _():
            dsh_ref[...] = jnp.zeros_like(dsh_ref)
            dsc_ref[...] = jnp.zeros_like(dsc_ref)
            dg_ref[...] = jnp.zeros_like(dg_ref)

        xv = x_ref[...]
        dhv = dh_ref[...]
        gv = g_ref[...]
        one_sc = 1.0 + sc_ref[...]
        r = lax.rsqrt(jnp.mean(xv * xv, axis=-1, keepdims=True) + NORM_EPS)
        n = xv * r
        dsh_ref[...] += jnp.sum(dhv, axis=0, keepdims=True)
        dsc_ref[...] += jnp.sum(dhv * (n * gv), axis=0, keepdims=True)
        dng = dhv * one_sc
        dg_ref[...] += jnp.sum(dng * n, axis=0, keepdims=True)
        dn = dng * gv
        dx_ref[...] = dxo_ref[...] + r * (dn - n * jnp.mean(dn * n, axis=-1, keepdims=True))

    return _pcall(
        body, name=name, grid=(s // ts,),
        in_specs=[_row_spec(ts, d), _row_spec(ts, d), _row_spec(ts, d), _vec_spec(d), _vec_spec(d)],
        out_specs=[_row_spec(ts, d), _vec_spec(d), _vec_spec(d), _vec_spec(d)],
        out_shape=[jax.ShapeDtypeStruct((s, d), F32)] + [jax.ShapeDtypeStruct((1, d), F32)] * 3,
        compiler_params=_params(("arbitrary",)),
    )(dh, x, dxo, g, scale)


def _sgu_mask():
    t = lax.broadcasted_iota(jnp.int32, (SGU_BLOCK, SGU_BLOCK), 0) // CHUNK
    s = lax.broadcasted_iota(jnp.int32, (SGU_BLOCK, SGU_BLOCK), 1) // CHUNK
    return s <= t


def _sgu_norm(v_pre, g):
    e = v_pre.shape[-1]
    vg = _gelu(v_pre)
    mu = jnp.sum(vg, axis=-1, keepdims=True) * (1.0 / e)
    dlt = vg - mu
    var = jnp.sum(dlt * dlt, axis=-1, keepdims=True) * (1.0 / e)
    rstd = lax.rsqrt(var + NORM_EPS)
    vhat = dlt * rstd
    return vhat, rstd, (vhat * g).astype(BF16)


def _sgu_mid_fwd(uvz, norm_g, w_s, bias_full, name):
    s, e3 = uvz.shape
    e = e3 // 3
    gd = e // SGU_GROUPS
    nb = s // SGU_BLOCK

    def body(uvz_ref, g_ref, w_ref, b_ref, y_ref, wsc):
        @pl.when(pl.program_id(0) == 0)
        def _():
            msk = _sgu_mask()
            for gi in range(SGU_GROUPS):
                wsc[gi] = jnp.where(msk, w_ref[gi], 0.0).astype(BF16)

        _, _, vb = _sgu_norm(uvz_ref[:, e:2 * e], g_ref[...])
        for gi in range(SGU_GROUPS):
            lo = gi * gd
            vm = jnp.dot(wsc[gi], vb[:, lo:lo + gd], preferred_element_type=F32) + b_ref[:, lo:lo + gd]
            zg = uvz_ref[:, 2 * e + lo:2 * e + lo + gd]
            y_ref[:, lo:lo + gd] = (_gelu(uvz_ref[:, lo:lo + gd]) * vm * (zg * _sigmoid(zg))).astype(BF16)

    return _pcall(
        body, name=name, grid=(nb,),
        in_specs=[pl.BlockSpec((SGU_BLOCK, e3), lambda n: (n, 0)),
                  pl.BlockSpec((1, e), lambda n: (0, 0)),
                  pl.BlockSpec((SGU_GROUPS, SGU_BLOCK, SGU_BLOCK), lambda n: (0, 0, 0)),
                  pl.BlockSpec((SGU_BLOCK, e), lambda n: (0, 0))],
        out_specs=pl.BlockSpec((SGU_BLOCK, e), lambda n: (n, 0)),
        out_shape=jax.ShapeDtypeStruct((s, e), BF16),
        scratch_shapes=[pltpu.VMEM((SGU_GROUPS, SGU_BLOCK, SGU_BLOCK), BF16)],
        compiler_params=_params(("arbitrary",)),
    )(uvz, norm_g, w_s, bias_full)


def _sgu_mid_bwd(uvz, dy, norm_g, w_s, bias_full, name):
    s, e3 = uvz.shape
    e = e3 // 3
    gd = e // SGU_GROUPS
    nb = s // SGU_BLOCK

    def body(uvz_ref, dy_ref, g_ref, w_ref, b_ref, d_ref, dw_ref, db_ref, dg_ref, wsc, wtsc, dvh_sc, dbacc):
        n = pl.program_id(0)

        @pl.when(n == 0)
        def _():
            msk = _sgu_mask()
            for gi in range(SGU_GROUPS):
                wm = jnp.where(msk, w_ref[gi], 0.0)
                wsc[gi] = wm.astype(BF16)
                wtsc[gi] = wm.T.astype(BF16)
            dw_ref[...] = jnp.zeros_like(dw_ref)
            dg_ref[...] = jnp.zeros_like(dg_ref)
            dbacc[...] = jnp.zeros_like(dbacc)

        v_pre = uvz_ref[:, e:2 * e]
        gv = g_ref[...]
        vhat, rstd, vb = _sgu_norm(v_pre, gv)
        s1 = jnp.zeros((SGU_BLOCK, 1), F32)
        s2 = jnp.zeros((SGU_BLOCK, 1), F32)
        for gi in range(SGU_GROUPS):
            lo = gi * gd
            u_pre = uvz_ref[:, lo:lo + gd]
            zg = uvz_ref[:, 2 * e + lo:2 * e + lo + gd]
            dyg = dy_ref[:, lo:lo + gd]
            ug = _gelu(u_pre)
            sig = _sigmoid(zg)
            vbg = vb[:, lo:lo + gd]
            vhg = vhat[:, lo:lo + gd]
            vm = jnp.dot(wsc[gi], vbg, preferred_element_type=F32) + b_ref[:, lo:lo + gd]
            t = dyg * (zg * sig)
            d_ref[:, lo:lo + gd] = (t * vm * _gelu_grad(u_pre)).astype(BF16)
            dvm = t * ug
            d_ref[:, 2 * e + lo:2 * e + lo + gd] = (dyg * ug * vm * (sig * (1.0 + zg * (1.0 - sig)))).astype(BF16)
            dvm_b = dvm.astype(BF16)
            dv = jnp.dot(wtsc[gi], dvm_b, preferred_element_type=F32)
            dw_ref[gi] += _dot_nt(dvm_b, vbg)
            dbacc[:, lo:lo + gd] += dvm
            dg_ref[:, lo:lo + gd] += jnp.sum(dv * vhg, axis=0, keepdims=True)
            dvh = dv * gv[:, lo:lo + gd]
            dvh_sc[:, lo:lo + gd] = dvh
            s1 = s1 + jnp.sum(dvh, axis=-1, keepdims=True)
            s2 = s2 + jnp.sum(dvh * vhg, axis=-1, keepdims=True)
        dvg = rstd * (dvh_sc[...] - s1 * (1.0 / e) - vhat * (s2 * (1.0 / e)))
        d_ref[:, e:2 * e] = (dvg * _gelu_grad(v_pre)).astype(BF16)

        @pl.when(n == nb - 1)
        def _():
            msk = _sgu_mask()
            for gi in range(SGU_GROUPS):
                dw_ref[gi] = jnp.where(msk, dw_ref[gi], 0.0)
                db_ref[gi] = jnp.sum(dbacc[:, gi * gd:(gi + 1) * gd], axis=1, keepdims=True)

    return _pcall(
        body, name=name, grid=(nb,),
        in_specs=[pl.BlockSpec((SGU_BLOCK, e3), lambda n: (n, 0)),
                  pl.BlockSpec((SGU_BLOCK, e), lambda n: (n, 0)),
                  pl.BlockSpec((1, e), lambda n: (0, 0)),
                  pl.BlockSpec((SGU_GROUPS, SGU_BLOCK, SGU_BLOCK), lambda n: (0, 0, 0)),
                  pl.BlockSpec((SGU_BLOCK, e), lambda n: (0, 0))],
        out_specs=[pl.BlockSpec((SGU_BLOCK, e3), lambda n: (n, 0)),
                   pl.BlockSpec((SGU_GROUPS, SGU_BLOCK, SGU_BLOCK), lambda n: (0, 0, 0)),
                   pl.BlockSpec((SGU_GROUPS, SGU_BLOCK, 1), lambda n: (0, 0, 0)),
                   pl.BlockSpec((1, e), lambda n: (0, 0))],
        out_shape=[jax.ShapeDtypeStruct((s, e3), BF16),
                   jax.ShapeDtypeStruct((SGU_GROUPS, SGU_BLOCK, SGU_BLOCK), F32),
                   jax.ShapeDtypeStruct((SGU_GROUPS, SGU_BLOCK, 1), F32),
                   jax.ShapeDtypeStruct((1, e), F32)],
        scratch_shapes=[pltpu.VMEM((SGU_GROUPS, SGU_BLOCK, SGU_BLOCK), BF16),
                        pltpu.VMEM((SGU_GROUPS, SGU_BLOCK, SGU_BLOCK), BF16),
                        pltpu.VMEM((SGU_BLOCK, e), F32),
                        pltpu.VMEM((SGU_BLOCK, e), F32)],
        compiler_params=_params(("arbitrary",)),
    )(uvz, dy, norm_g, w_s, bias_full)


def _rope_tables(s):
    pos = jnp.arange(s, dtype=F32)
    inv_freq = ROPE_THETA ** (-jnp.arange(0, ROPE, 2, dtype=F32) / ROPE)
    ang = pos[:, None] * inv_freq[None, :]
    cos, sin = jnp.cos(ang), jnp.sin(ang)
    z32 = jnp.zeros((s, HALF), F32)
    z64 = jnp.zeros((s, ROPE), F32)
    ck = jnp.concatenate([cos, cos, z64], axis=1)
    s1k = jnp.concatenate([-sin, z32, z64], axis=1)
    s2k = jnp.concatenate([z32, sin, z64], axis=1)
    one = jnp.ones((s, NOPE), F32)
    zn = jnp.zeros((s, NOPE), F32)
    return (ck, s1k, s2k), (jnp.concatenate([one, ck], axis=1), jnp.concatenate([zn, s1k], axis=1),
                            jnp.concatenate([zn, s2k], axis=1))


def _rot(x, c, s1, s2):
    w = x.shape[-1]
    return x * c + pltpu.roll(x, w - HALF, 1) * s1 + pltpu.roll(x, HALF, 1) * s2


def _rms(cv, n_real):
    r = lax.rsqrt(jnp.sum(cv * cv, axis=-1, keepdims=True) * (1.0 / n_real) + NORM_EPS)
    return r, cv * r


def _mla_norm_fwd(proj, gq, gkv, tabs, name):
    s = proj.shape[0]
    ts = _pick(s, ROW_BLK)
    ck, s1k, s2k = tabs

    def body(cq_ref, ckv_ref, kr_ref, gq_ref, gkv_ref, c_ref, s1_ref, s2_ref, qn_ref, kvn_ref, kro_ref):
        _, nq = _rms(cq_ref[...], Q_RANK)
        qn_ref[...] = (nq * gq_ref[...]).astype(BF16)
        _, nkv = _rms(ckv_ref[...], KV_RANK)
        kvn_ref[...] = (nkv * gkv_ref[...]).astype(BF16)
        kro_ref[...] = _rot(kr_ref[...], c_ref[...], s1_ref[...], s2_ref[...]).astype(BF16)

    tab = pl.BlockSpec((ts, LANE), lambda i: (i, 0))
    return _pcall(
        body, name=name, grid=(s // ts,),
        in_specs=[pl.BlockSpec((ts, Q_RANK_PAD), lambda i: (i, 0)),
                  pl.BlockSpec((ts, KV_RANK), lambda i: (i, PROJ_CKV // KV_RANK)),
                  pl.BlockSpec((ts, LANE), lambda i: (i, PROJ_KR // LANE)),
                  _vec_spec(Q_RANK_PAD), _vec_spec(KV_RANK), tab, tab, tab],
        out_specs=[pl.BlockSpec((ts, Q_RANK_PAD), lambda i: (i, 0)),
                   pl.BlockSpec((ts, KV_RANK), lambda i: (i, 0)), tab],
        out_shape=[jax.ShapeDtypeStruct((s, Q_RANK_PAD), BF16), jax.ShapeDtypeStruct((s, KV_RANK), BF16),
                   jax.ShapeDtypeStruct((s, LANE), BF16)],
        compiler_params=_params(("parallel",)),
    )(proj, proj, proj, gq, gkv, ck, s1k, s2k)


def _rope_heads(q, tabs, sign, mult, out_dtype, name):
    s, w = q.shape
    ts = _pick(s, ROW_BLK)
    c, s1, s2 = tabs

    def body(q_ref, c_ref, s1_ref, s2_ref, o_ref):
        cv, s1v, s2v = mult * c_ref[...], (sign * mult) * s1_ref[...], (sign * mult) * s2_ref[...]
        for h in range(w // HEAD_PAD):
            lo = h * HEAD_PAD
            o_ref[:, lo:lo + HEAD_PAD] = _rot(q_ref[:, lo:lo + HEAD_PAD].astype(F32), cv, s1v, s2v).astype(out_dtype)

    blk = pl.BlockSpec((ts, w), lambda i: (i, 0))
    tab = pl.BlockSpec((ts, HEAD_PAD), lambda i: (i, 0))
    return _pcall(
        body, name=name, grid=(s // ts,),
        in_specs=[blk, tab, tab, tab], out_specs=blk,
        out_shape=jax.ShapeDtypeStruct((s, w), out_dtype),
        compiler_params=_params(("parallel",)),
    )(q, c, s1, s2)


def _transpose_bf16(t):
    return t.astype(F32).T.astype(BF16)


def _diag_mask(tb, transposed):
    r = lax.broadcasted_iota(jnp.int32, (tb, tb), 0) // CHUNK
    c = lax.broadcasted_iota(jnp.int32, (tb, tb), 1) // CHUNK
    return (r <= c) if transposed else (c <= r)


def _attn_fwd(q_cat, kv, kr, proj, name):
    s = q_cat.shape[0]
    tb = _pick(s, ATT_BLK)
    nb = s // tb
    zcol = PROJ_Z // V_DIM

    def body(q_ref, kn_ref, v_ref, kr_ref, z_ref, o_ref, y_ref, lse_ref, kt_sc, vx_sc, m_sc, acc_sc, sa_sc, sb_sc):
        qi = pl.program_id(1)

        @pl.when(qi == 0)
        def _():
            for b in range(nb):
                rows = slice(b * tb, (b + 1) * tb)
                kt_sc[b] = _transpose_bf16(jnp.concatenate([kn_ref[rows, :], kr_ref[rows, :]], axis=1))
                vx_sc[b] = jnp.concatenate([v_ref[rows, :], jnp.ones((tb, V_DIM), BF16)], axis=1)

        m_sc[...] = jnp.full_like(m_sc, -1e30)
        acc_sc[...] = jnp.zeros_like(acc_sc)
        sub = min(tb, ATT_SUB)

        def scores(ki, s_ref):
            s_ref[...] = jnp.dot(q_ref[...], kt_sc[ki], preferred_element_type=F32)

        def step(ki, s_ref, masked):
            for r in range(tb // sub):
                rs = slice(r * sub, (r + 1) * sub)
                sc = s_ref[rs, :]
                if masked:
                    sc = jnp.where(_diag_mask(tb, False)[rs, :], sc, -1e30)
                m_prev = m_sc[rs, :]
                m_new = jnp.maximum(m_prev, jnp.max(sc, axis=-1, keepdims=True))
                p = jnp.exp2(sc - m_new).astype(BF16)
                acc_sc[rs, :] = (jnp.exp2(m_prev - m_new) * acc_sc[rs, :]
                                 + jnp.dot(p, vx_sc[ki], preferred_element_type=F32))
                m_sc[rs, :] = m_new

        def pair(t, carry):
            scores(2 * t + 1, sb_sc)
            step(2 * t, sa_sc, False)
            scores(2 * t + 2, sa_sc)
            step(2 * t + 1, sb_sc, False)
            return carry

        scores(0, sa_sc)
        lax.fori_loop(0, qi // 2, pair, 0)

        @pl.when(qi % 2 == 1)
        def _():
            scores(qi, sb_sc)
            step(qi - 1, sa_sc, False)
            step(qi, sb_sc, True)

        @pl.when(qi % 2 == 0)
        def _():
            step(qi, sa_sc, True)

        l = acc_sc[:, V_DIM:V_DIM + 1]
        o = acc_sc[:, :V_DIM] / l
        z = z_ref[...]
        o_ref[...] = o.astype(BF16)
        y_ref[...] = (o * (z * _sigmoid(z))).astype(BF16)
        lse_cols = jnp.broadcast_to(m_sc[...] + jnp.log2(l), (tb, LANE))
        lse_ref[...] = lse_cols.T[0:1, :]

    oblk = pl.BlockSpec((tb, V_DIM), lambda h, qi: (qi, h))
    return _pcall(
        body, name=name, grid=(HEADS, nb),
        in_specs=[pl.BlockSpec((tb, HEAD_PAD), lambda h, qi: (qi, h)),
                  pl.BlockSpec((s, NOPE), lambda h, qi: (0, 2 * h)),
                  pl.BlockSpec((s, V_DIM), lambda h, qi: (0, 2 * h + 1)),
                  pl.BlockSpec((s, LANE), lambda h, qi: (0, 0)),
                  pl.BlockSpec((tb, V_DIM), lambda h, qi: (qi, zcol + h))],
        out_specs=[oblk, oblk, pl.BlockSpec((None, None, 1, tb), lambda h, qi: (h, qi, 0, 0))],
        out_shape=[jax.ShapeDtypeStruct((s, MLA_WIDTH), BF16), jax.ShapeDtypeStruct((s, MLA_WIDTH), BF16),
                   jax.ShapeDtypeStruct((HEADS, nb, 1, tb), F32)],
        scratch_shapes=[pltpu.VMEM((nb, HEAD_PAD, tb), BF16), pltpu.VMEM((nb, tb, HEAD_PAD), BF16),
                        pltpu.VMEM((tb, 1), F32), pltpu.VMEM((tb, HEAD_PAD), F32),
                        pltpu.VMEM((tb, tb), F32), pltpu.VMEM((tb, tb), F32)],
        compiler_params=_params(("parallel", "arbitrary")),
    )(q_cat, kv, kv, kr, proj)


def _attn_bwd(q_cat, kv, kr, do, o, lse, name):
    s = q_cat.shape[0]
    tb = _pick(s, ATT_BLK)
    nb = s // tb
    ln2 = math.log(2.0)

    def body(q_ref, do_ref, o_ref, lse_ref, kn_ref, v_ref, kr_ref, dq_ref, dkv_ref, dkr_ref,
             qt_sc, dot_sc, delta_sc, dqt_sc, dk_sc, dv_sc):
        ki = pl.program_id(1)

        @pl.when(ki == 0)
        def _():
            for b in range(nb):
                rows = slice(b * tb, (b + 1) * tb)
                qt_sc[b] = _transpose_bf16(q_ref[rows, :])
                do_t = do_ref[rows, :].astype(F32).T
                dot_sc[b] = do_t.astype(BF16)
                delta_sc[b] = jnp.sum(do_t * o_ref[rows, :].astype(F32).T, axis=0, keepdims=True)
            dqt_sc[...] = jnp.zeros_like(dqt_sc)

        k = jnp.concatenate([kn_ref[...], kr_ref[...]], axis=1)
        kt = _transpose_bf16(k)
        vb = v_ref[...]
        dk_sc[...] = jnp.zeros_like(dk_sc)
        dv_sc[...] = jnp.zeros_like(dv_sc)

        def step(qi, masked):
            rows = pl.ds(pl.multiple_of(qi * tb, tb), tb)
            sc_t = jnp.dot(k, qt_sc[qi], preferred_element_type=F32)
            if masked:
                sc_t = jnp.where(_diag_mask(tb, True), sc_t, -1e30)
            p_t = jnp.exp2(sc_t - lse_ref[qi])
            dp_t = jnp.dot(vb, dot_sc[qi], preferred_element_type=F32)
            ds_t = (p_t * (dp_t - delta_sc[qi])).astype(BF16)
            dv_sc[...] += jnp.dot(p_t.astype(BF16), do_ref[rows, :], preferred_element_type=F32)
            dk_sc[...] += jnp.dot(ds_t, q_ref[rows, :], preferred_element_type=F32)
            dqt_sc[qi] += jnp.dot(kt, ds_t, preferred_element_type=F32)

        step(ki, True)

        def loop_body(qi, carry):
            step(qi, False)
            return carry

        lax.fori_loop(ki + 1, nb, loop_body, 0)

        dkv_ref[:, :NOPE] = (dk_sc[:, :NOPE] * ln2).astype(BF16)
        dkv_ref[:, NOPE:] = dv_sc[...].astype(BF16)
        dkr_ref[...] = dk_sc[:, NOPE:] * ln2

        @pl.when(ki == nb - 1)
        def _():
            for b in range(nb):
                dq_ref[b * tb:(b + 1) * tb, :] = dqt_sc[b].T

    return _pcall(
        body, name=name, grid=(HEADS, nb),
        in_specs=[pl.BlockSpec((s, HEAD_PAD), lambda h, ki: (0, h)),
                  pl.BlockSpec((s, V_DIM), lambda h, ki: (0, h)),
                  pl.BlockSpec((s, V_DIM), lambda h, ki: (0, h)),
                  pl.BlockSpec((None, nb, 1, tb), lambda h, ki: (h, 0, 0, 0)),
                  pl.BlockSpec((tb, NOPE), lambda h, ki: (ki, 2 * h)),
                  pl.BlockSpec((tb, V_DIM), lambda h, ki: (ki, 2 * h + 1)),
                  pl.BlockSpec((tb, LANE), lambda h, ki: (ki, 0))],
        out_specs=[pl.BlockSpec((s, HEAD_PAD), lambda h, ki: (0, h)),
                   pl.BlockSpec((tb, HEAD_PAD), lambda h, ki: (ki, h)),
                   pl.BlockSpec((None, tb, LANE), lambda h, ki: (h, ki, 0))],
        out_shape=[jax.ShapeDtypeStruct((s, HEADS * HEAD_PAD), F32),
                   jax.ShapeDtypeStruct((s, HEADS * HEAD_PAD), BF16),
                   jax.ShapeDtypeStruct((HEADS, s, LANE), F32)],
        scratch_shapes=[pltpu.VMEM((nb, HEAD_PAD, tb), BF16), pltpu.VMEM((nb, V_DIM, tb), BF16),
                        pltpu.VMEM((nb, 1, tb), F32), pltpu.VMEM((nb, HEAD_PAD, tb), F32),
                        pltpu.VMEM((tb, HEAD_PAD), F32), pltpu.VMEM((tb, V_DIM), F32)],
        compiler_params=_params(("parallel", "arbitrary")),
    )(q_cat, do, o, lse, kv, kv, kr)


def _mla_gate_bwd(dy, o, proj, name):
    s = dy.shape[0]
    ts = _pick(s, ROW_BLK)
    zcol = PROJ_Z // V_DIM

    def body(dy_ref, o_ref, p_ref, do_ref, dz_ref):
        z = p_ref[:, PROJ_Z:]
        dyv = dy_ref[...]
        sig = _sigmoid(z)
        do_ref[...] = (dyv * (z * sig)).astype(BF16)
        dz_ref[...] = (dyv * o_ref[...].astype(F32) * (sig * (1.0 + z * (1.0 - sig)))).astype(BF16)

    blk = pl.BlockSpec((ts, MLA_WIDTH), lambda i: (i, 0))
    return _pcall(
        body, name=name, grid=(s // ts,),
        in_specs=[blk, blk, pl.BlockSpec((ts, PROJ_W), lambda i: (i, 0))],
        out_specs=[blk, blk],
        out_shape=[jax.ShapeDtypeStruct((s, MLA_WIDTH), BF16), jax.ShapeDtypeStruct((s, MLA_WIDTH), BF16)],
        compiler_params=_params(("parallel",)),
    )(dy, o, proj)


def _mla_norm_bwd(dqn, dkvn, dkr_heads, proj, gq, gkv, tabs, name):
    s = proj.shape[0]
    ts = _pick(s, ROW_BLK)
    ck, s1k, s2k = tabs

    def rms_bwd(cv, dn_in, g, n_real):
        r, n = _rms(cv, n_real)
        dg = jnp.sum(dn_in * n, axis=0, keepdims=True)
        dn = dn_in * g
        dc = r * (dn - n * (jnp.sum(dn * n, axis=-1, keepdims=True) * (1.0 / n_real)))
        return dc, dg

    def body(dqn_ref, dkvn_ref, dkr_ref, cq_ref, ckv_ref, gq_ref, gkv_ref, c_ref, s1_ref, s2_ref,
             dp_ref, dgq_ref, dgkv_ref):
        @pl.when(pl.program_id(0) == 0)
        def _():
            dgq_ref[...] = jnp.zeros_like(dgq_ref)
            dgkv_ref[...] = jnp.zeros_like(dgkv_ref)

        dcq, dgq = rms_bwd(cq_ref[...], dqn_ref[...], gq_ref[...], Q_RANK)
        dckv, dgkv = rms_bwd(ckv_ref[...], dkvn_ref[...], gkv_ref[...], KV_RANK)
        dgq_ref[...] += dgq
        dgkv_ref[...] += dgkv
        dkr = dkr_ref[0]
        for h in range(1, HEADS):
            dkr = dkr + dkr_ref[h]
        dp_ref[:, PROJ_CQ:PROJ_CKV] = dcq.astype(BF16)
        dp_ref[:, PROJ_CKV:PROJ_KR] = dckv.astype(BF16)
        dp_ref[:, PROJ_KR:PROJ_Z] = _rot(dkr, c_ref[...], -s1_ref[...], -s2_ref[...]).astype(BF16)

    tab = pl.BlockSpec((ts, LANE), lambda i: (i, 0))
    return _pcall(
        body, name=name, grid=(s // ts,),
        in_specs=[pl.BlockSpec((ts, Q_RANK_PAD), lambda i: (i, 0)),
                  pl.BlockSpec((ts, KV_RANK), lambda i: (i, 0)),
                  pl.BlockSpec((HEADS, ts, LANE), lambda i: (0, i, 0)),
                  pl.BlockSpec((ts, Q_RANK_PAD), lambda i: (i, 0)),
                  pl.BlockSpec((ts, KV_RANK), lambda i: (i, PROJ_CKV // KV_RANK)),
                  _vec_spec(Q_RANK_PAD), _vec_spec(KV_RANK), tab, tab, tab],
        out_specs=[pl.BlockSpec((ts, PROJ_Z), lambda i: (i, 0)), _vec_spec(Q_RANK_PAD), _vec_spec(KV_RANK)],
        out_shape=[jax.ShapeDtypeStruct((s, PROJ_Z), BF16), jax.ShapeDtypeStruct((1, Q_RANK_PAD), F32),
                   jax.ShapeDtypeStruct((1, KV_RANK), F32)],
        compiler_params=_params(("arbitrary",)),
    )(dqn, dkvn, dkr_heads, proj, proj, gq, gkv, ck, s1k, s2k)


def _ada_mod(cond_raw, ada_w, bias_my, name):
    nl, d, ncol = ada_w.shape
    tk = _pick(d, 512)
    nk = d // tk

    def body(c_ref, w_ref, b_ref, o_ref, acc_ref):
        kk = pl.program_id(1)

        @pl.when(kk == 0)
        def _():
            acc_ref[...] = jnp.zeros_like(acc_ref)

        cv = c_ref[...]
        cond = (cv * _sigmoid(cv)).astype(BF16)
        acc_ref[...] += jnp.dot(cond, w_ref[...].astype(BF16), preferred_element_type=F32)

        @pl.when(kk == nk - 1)
        def _():
            o_ref[...] = acc_ref[...] + b_ref[...]

    return _pcall(
        body, name=name, grid=(nl, nk),
        in_specs=[pl.BlockSpec((N_DEV, tk), lambda l, kk: (0, kk)),
                  pl.BlockSpec((None, tk, ncol), lambda l, kk: (l, kk, 0)),
                  pl.BlockSpec((None, 1, ncol), lambda l, kk: (l, 0, 0))],
        out_specs=pl.BlockSpec((None, N_DEV, ncol), lambda l, kk: (l, 0, 0)),
        out_shape=jax.ShapeDtypeStruct((nl, N_DEV, ncol), F32),
        scratch_shapes=[pltpu.VMEM((N_DEV, ncol), F32)],
        compiler_params=_params(("parallel", "arbitrary")),
    )(cond_raw, ada_w, bias_my.reshape(nl, 1, ncol))


def _adam(w, g, m, v):
    m = ADAM_B1 * m + (1.0 - ADAM_B1) * g
    v = ADAM_B2 * v + (1.0 - ADAM_B2) * (g * g)
    m_hat = m / (1.0 - ADAM_B1 ** ADAM_STEP)
    v_hat = v / (1.0 - ADAM_B2 ** ADAM_STEP)
    delta = -ADAM_LR * (m_hat / (jnp.sqrt(v_hat) + ADAM_EPS) + ADAM_WD * w)
    return delta, m, v


def _ada_bwd_adam(cond_t, dmod_cols, w, m, v, name):
    nl, d, ncol = w.shape
    tk = _pick(d, 512)

    def body(c_ref, dm_ref, w_ref, m_ref, v_ref, g_ref, d_ref, mo_ref, vo_ref):
        cv = c_ref[...]
        cond = (cv * _sigmoid(cv)).astype(BF16)
        g = jnp.dot(cond, dm_ref[...].astype(BF16), preferred_element_type=F32)
        delta, m2, v2 = _adam(w_ref[...], g, m_ref[...], v_ref[...])
        g_ref[...] = g
        d_ref[...] = delta
        mo_ref[...] = m2
        vo_ref[...] = v2

    blk = pl.BlockSpec((None, tk, ncol), lambda l, kk: (l, kk, 0))
    shp = jax.ShapeDtypeStruct((nl, d, ncol), F32)
    return _pcall(
        body, name=name, grid=(nl, d // tk),
        in_specs=[pl.BlockSpec((tk, N_DEV), lambda l, kk: (kk, 0)),
                  pl.BlockSpec((None, N_DEV, ncol), lambda l, kk: (l, 0, 0)), blk, blk, blk],
        out_specs=[blk, blk, blk, blk], out_shape=[shp, shp, shp, shp],
        compiler_params=_params(("parallel", "parallel")),
    )(cond_t, dmod_cols, w, m, v)


def _adam_reduce(recv0, recv1, w, m, v, name):
    nl, r, c = w.shape
    tr = _pick(r, 128) if r % 128 == 0 else r
    tc = _pick(c, 1024)

    def body(r0_ref, r1_ref, w_ref, m_ref, v_ref, g_ref, d_ref, mo_ref, vo_ref):
        l = pl.program_id(0)

        def run(rr):
            g = rr[0].astype(F32)
            for sidx in range(1, N_DEV):
                g = g + rr[sidx].astype(F32)
            delta, m2, v2 = _adam(w_ref[...], g, m_ref[...], v_ref[...])
            g_ref[...] = g
            d_ref[...] = delta
            mo_ref[...] = m2
            vo_ref[...] = v2

        @pl.when(l == 0)
        def _():
            run(r0_ref)

        @pl.when(l == 1)
        def _():
            run(r1_ref)

    rblk = pl.BlockSpec((N_DEV, tr, tc), lambda l, i, j: (0, i, j))
    blk = pl.BlockSpec((None, tr, tc), lambda l, i, j: (l, i, j))
    shp = jax.ShapeDtypeStruct((nl, r, c), F32)
    return _pcall(
        body, name=name, grid=(nl, r // tr, c // tc),
        in_specs=[rblk, rblk, blk, blk, blk],
        out_specs=[blk, blk, blk, blk], out_shape=[shp, shp, shp, shp],
        compiler_params=_params(("arbitrary", "parallel", "parallel")),
    )(recv0.reshape(N_DEV, r, c), recv1.reshape(N_DEV, r, c), w, m, v)


def _adam_small(gathered, w, m, v, name):
    r = w.shape[0]
    tr = _pick(r, 512) if r % 512 == 0 else r

    def body(p_ref, w_ref, m_ref, v_ref, g_ref, d_ref, mo_ref, vo_ref):
        g = p_ref[0]
        for sidx in range(1, N_DEV):
            g = g + p_ref[sidx]
        delta, m2, v2 = _adam(w_ref[...], g, m_ref[...], v_ref[...])
        g_ref[...] = g
        d_ref[...] = delta
        mo_ref[...] = m2
        vo_ref[...] = v2

    blk = pl.BlockSpec((tr, LANE), lambda i: (i, 0))
    shp = jax.ShapeDtypeStruct((r, LANE), F32)
    return _pcall(
        body, name=name, grid=(r // tr,),
        in_specs=[pl.BlockSpec((N_DEV, tr, LANE), lambda i: (0, i, 0)), blk, blk, blk],
        out_specs=[blk, blk, blk, blk], out_shape=[shp, shp, shp, shp],
        compiler_params=_params(("parallel",)),
    )(gathered, w, m, v)


def _my_place():
    x, y, c = lax.axis_index("x"), lax.axis_index("y"), lax.axis_index("c")
    return x, y, c, 4 * x + 2 * y + c


def _peer(x, y, c, k):
    px = 1 - x if (k >> 2) & 1 else x
    py = 1 - y if (k >> 1) & 1 else y
    pc = 1 - c if k & 1 else c
    return (px, py, pc), 4 * px + 2 * py + pc


def _slab(ref, shape, kind, p):
    r, cd = shape
    if kind == "row":
        return ref.at[pl.ds(pl.multiple_of(p * r, SUBLANE), r), :]
    return ref.at[:, pl.ds(pl.multiple_of(p * cd, LANE), cd)]


def _exchange_layout(arrays, kinds, gather):
    shard_shapes, dst_kinds, out_shapes = [], [], []
    for a, kind in zip(arrays, kinds):
        r, cd = a.shape
        if gather:
            shard, dst_kind = (r, cd), kind
        else:
            shard, dst_kind = ((r // N_DEV, cd) if kind == "row" else (r, cd // N_DEV)), "row"
        shard_shapes.append(shard)
        dst_kinds.append(dst_kind)
        full = (shard[0] * N_DEV, shard[1]) if dst_kind == "row" else (shard[0], shard[1] * N_DEV)
        out_shapes.append(jax.ShapeDtypeStruct(full, a.dtype))
    return shard_shapes, dst_kinds, out_shapes


def _exchange_copies(ins, outs, send_sems, recv_sems, sem_of, layout, kinds, gather):
    shard_shapes, dst_kinds, _ = layout
    x, y, c, me = _my_place()

    def src_for(a, p):
        return ins[a] if gather else _slab(ins[a], shard_shapes[a], kinds[a], p)

    def dst_slot(a, p):
        return _slab(outs[a], shard_shapes[a], dst_kinds[a], p)

    def local(a, sem):
        return pltpu.make_async_copy(src_for(a, me), dst_slot(a, me), sem)

    def remote(a, k, slot):
        peer, pidx = _peer(x, y, c, k)
        return pltpu.make_async_remote_copy(
            src_ref=src_for(a, pidx), dst_ref=dst_slot(a, me if slot == "mine" else pidx),
            send_sem=send_sems.at[sem_of(a, k)], recv_sem=recv_sems.at[sem_of(a, k)],
            device_id=peer, device_id_type=MESH)

    return local, remote


def _place_own(src, src_kind, slab_shape, dst_kind, full, name):
    r, cd = slab_shape
    tr = _pick(r, 512)
    nr = r // tr
    _, _, _, me = _my_place()

    def body(me_ref, s_ref, o_ref):
        o_ref[...] = s_ref[...].astype(o_ref.dtype)

    def where(kind):
        if kind is None:
            return lambda i, me_ref: (i, 0)
        if kind == "row":
            return lambda i, me_ref: (me_ref[0] * nr + i, 0)
        return lambda i, me_ref: (i, me_ref[0])

    return _pcall(
        body, name=name,
        grid_spec=pltpu.PrefetchScalarGridSpec(
            num_scalar_prefetch=1, grid=(nr,),
            in_specs=[pl.BlockSpec((tr, cd), where(src_kind))],
            out_specs=pl.BlockSpec((tr, cd), where(dst_kind))),
        out_shape=jax.ShapeDtypeStruct(full.shape, full.dtype),
        compiler_params=_params(("arbitrary",)),
    )(jnp.reshape(me, (1,)).astype(jnp.int32), src)


def _landing_with_own_slab(arrays, kinds, gather, layout, name=None):
    _, _, _, me = _my_place()
    lands = []
    for a in range(len(arrays)):
        (r, cd), dst_kind, full = layout[0][a], layout[1][a], layout[2][a]
        if name is not None:
            lands.append(_place_own(arrays[a], None if gather else kinds[a], (r, cd), dst_kind, full, name))
            continue
        if gather:
            piece = arrays[a]
        elif kinds[a] == "row":
            piece = lax.dynamic_slice_in_dim(arrays[a], me * r, r, axis=0)
        else:
            piece = lax.dynamic_slice_in_dim(arrays[a], me * cd, cd, axis=1)
        at = (me * r, 0) if dst_kind == "row" else (0, me * cd)
        lands.append(lax.dynamic_update_slice(lax.empty(full.shape, full.dtype), piece, at))
    return lands


def _exchange(arrays, kinds, gather, name, after=None):
    n = len(arrays)
    n_extra = 0 if after is None else 1
    layout = _exchange_layout(arrays, kinds, gather)
    lands = _landing_with_own_slab(arrays, kinds, gather, layout)

    def body(*refs):
        ins, outs = refs[:n], refs[2 * n + n_extra:3 * n + n_extra]
        send_sems, recv_sems = refs[3 * n + n_extra:]
        _, remote = _exchange_copies(ins, outs, send_sems, recv_sems,
                                     lambda a, k: a * (N_DEV - 1) + k - 1, layout, kinds, gather)
        for a in range(n):
            for k in range(1, N_DEV):
                remote(a, k, "mine").start()
        for a in range(n):
            for k in range(1, N_DEV):
                arrival = remote(a, k, "theirs")
                arrival.wait_send()
                arrival.wait_recv()

    anyspec = pl.BlockSpec(memory_space=pl.ANY)
    outs = _pcall(
        body, comm=True, name=name,
        in_specs=[anyspec] * (2 * n + n_extra), out_specs=[anyspec] * n, out_shape=layout[2],
        input_output_aliases={n + a: a for a in range(n)},
        scratch_shapes=[pltpu.SemaphoreType.DMA((n * (N_DEV - 1),)), pltpu.SemaphoreType.DMA((n * (N_DEV - 1),))],
    )(*arrays, *lands, *([] if after is None else [after]))
    return list(outs)


HBM_SPEC = pl.BlockSpec(memory_space=pltpu.HBM)
SEM_SPEC = pl.BlockSpec(memory_space=pltpu.SEMAPHORE)
ANY_SPEC = pl.BlockSpec(memory_space=pl.ANY)
DATAFLOW = pltpu.SideEffectType.DATAFLOW_SIDE_EFFECTING


def _exchange_start(arrays, kinds, gather, name, after, carry=()):
    n, nc = len(arrays), len(carry)
    layout = _exchange_layout(arrays, kinds, gather)
    lands = _landing_with_own_slab(arrays, kinds, gather, layout, "place_own")

    def body(*refs):
        ins, outs = refs[:n], refs[n:2 * n]
        send_sems, recv_sems = refs[2 * n + nc + 1], refs[2 * n + nc + 2]
        token = refs[2 * n + nc + 3 + 2 * n + nc]
        _, remote = _exchange_copies(ins, outs, send_sems, recv_sems, lambda a, k: a, layout, kinds, gather)
        for a in range(n):
            for k in range(1, N_DEV):
                remote(a, k, "mine").start()
        token[...] = jnp.zeros_like(token)

    passed = list(arrays) + lands + list(carry)
    res = pl.pallas_call(
        body, name=name,
        out_shape=(pltpu.SemaphoreType.DMA((n,)), pltpu.SemaphoreType.DMA((n,)),
                   *[pltpu.HBM(t.shape, t.dtype) for t in passed], jax.ShapeDtypeStruct((SUBLANE, LANE), F32)),
        in_specs=[HBM_SPEC] * (2 * n + nc) + [ANY_SPEC],
        out_specs=(SEM_SPEC, SEM_SPEC, *([HBM_SPEC] * (2 * n + nc)), pl.BlockSpec(memory_space=pltpu.VMEM)),
        input_output_aliases={i: 2 + i for i in range(2 * n + nc)},
        compiler_params=pltpu.CompilerParams(has_side_effects=DATAFLOW),
    )(*[pltpu.with_memory_space_constraint(t, pltpu.HBM) for t in passed], after)
    handle = (res[0], res[1], list(res[2:2 + n]), list(res[2 + n:2 + 2 * n]), tuple(kinds), gather)
    return handle, res[-1], list(res[2 + 2 * n:2 + 2 * n + nc])


def _exchange_wait(handle, name, after):
    send_sems, recv_sems, ins_thru, lands_thru, kinds, gather = handle
    n = len(ins_thru)
    layout = _exchange_layout(ins_thru, kinds, gather)

    def body(*refs):
        ins, outs = refs[:n], refs[n:2 * n]
        s_sems, r_sems = refs[2 * n], refs[2 * n + 1]
        _, remote = _exchange_copies(ins, outs, s_sems, r_sems, lambda a, k: a, layout, kinds, gather)
        for a in range(n):
            for k in range(1, N_DEV):
                arrival = remote(a, k, "theirs")
                arrival.wait_send()
                arrival.wait_recv()

    res = pl.pallas_call(
        body, name=name,
        out_shape=[pltpu.HBM(t.shape, t.dtype) for t in ins_thru + lands_thru],
        in_specs=[HBM_SPEC] * (2 * n) + [SEM_SPEC, SEM_SPEC, ANY_SPEC],
        out_specs=[HBM_SPEC] * (2 * n),
        input_output_aliases={i: i for i in range(2 * n)},
        compiler_params=pltpu.CompilerParams(has_side_effects=DATAFLOW),
    )(*ins_thru, *lands_thru, send_sems, recv_sems, after)
    return list(res[n:2 * n])


STAGE1_KS = (1, 2, 4, 6)
FORWARD_KS = (2, 4, 6)


def _gather2_copies(lands, shard_shapes, kinds):
    x, y, c, me = _my_place()

    def slab(a, p):
        return _slab(lands[a], shard_shapes[a], kinds[a], p)

    def stage1(a, k, sems, arriving):
        peer, pidx = _peer(x, y, c, k)
        s = slab(a, pidx if arriving else me)
        return pltpu.make_async_remote_copy(src_ref=s, dst_ref=s, send_sem=sems[0].at[a], recv_sem=sems[1].at[a],
                                            device_id=peer, device_id_type=MESH)

    def stage2(a, k, sems, arriving):
        sib, _ = _peer(x, y, c, 1)
        _, mine = _peer(x, y, c, k)
        _, theirs = _peer(x, y, 1 - c, k)
        s = slab(a, theirs if arriving else mine)
        return pltpu.make_async_remote_copy(src_ref=s, dst_ref=s, send_sem=sems[0].at[a], recv_sem=sems[1].at[a],
                                            device_id=sib, device_id_type=MESH)

    return stage1, stage2


def _gather2_call(lands, sems_in, name, after, make_body, returns_sems):
    n = len(lands)
    n_in = len(sems_in)

    def body(*refs):
        land_refs = refs[:n]
        in_sems = refs[n:n + n_in]
        rest = refs[n + n_in + 1:]
        out_sems = rest[:2] if returns_sems else ()
        make_body(land_refs, in_sems, out_sems)
        if returns_sems:
            token = rest[2 + n]
            token[...] = jnp.zeros_like(token)

    sem_shapes = (pltpu.SemaphoreType.DMA((n,)), pltpu.SemaphoreType.DMA((n,))) if returns_sems else ()
    tok_shape = (jax.ShapeDtypeStruct((SUBLANE, LANE), F32),) if returns_sems else ()
    n_sem_out = len(sem_shapes)
    res = pl.pallas_call(
        body, name=name,
        out_shape=(*sem_shapes, *[pltpu.HBM(t.shape, t.dtype) for t in lands], *tok_shape),
        in_specs=[HBM_SPEC] * n + [SEM_SPEC] * n_in + [ANY_SPEC],
        out_specs=(*([SEM_SPEC] * n_sem_out), *([HBM_SPEC] * n),
                   *([pl.BlockSpec(memory_space=pltpu.VMEM)] if returns_sems else [])),
        input_output_aliases={i: n_sem_out + i for i in range(n)},
        compiler_params=pltpu.CompilerParams(has_side_effects=DATAFLOW),
    )(*[pltpu.with_memory_space_constraint(t, pltpu.HBM) for t in lands], *sems_in, after)
    sems_out = tuple(res[:n_sem_out])
    lands_thru = list(res[n_sem_out:n_sem_out + n])
    return sems_out, lands_thru, (res[-1] if returns_sems else None)


def _gather2_start(shards, kinds, name, after):
    n = len(shards)
    layout = _exchange_layout(shards, kinds, True)
    lands = _landing_with_own_slab(shards, kinds, True, layout, "place_own")

    def make_body(land_refs, in_sems, out_sems):
        stage1, _ = _gather2_copies(land_refs, layout[0], kinds)
        for a in range(n):
            for k in STAGE1_KS:
                stage1(a, k, out_sems, False).start()

    sems, lands, token = _gather2_call(lands, (), name, after, make_body, True)
    return (sems, lands, layout[0], tuple(kinds)), token


def _gather2_forward(handle, name, after):
    sems1, lands, shard_shapes, kinds = handle
    n = len(lands)

    def make_body(land_refs, in_sems, out_sems):
        stage1, stage2 = _gather2_copies(land_refs, shard_shapes, kinds)
        for a in range(n):
            for k in STAGE1_KS:
                arrival = stage1(a, k, in_sems, True)
                arrival.wait_send()
                arrival.wait_recv()
        for a in range(n):
            for k in FORWARD_KS:
                stage2(a, k, out_sems, False).start()

    sems2, lands, token = _gather2_call(lands, sems1, name, after, make_body, True)
    return (sems2, lands, shard_shapes, kinds), token


def _gather2_wait(handle, name, after):
    sems2, lands, shard_shapes, kinds = handle
    n = len(lands)

    def make_body(land_refs, in_sems, out_sems):
        _, stage2 = _gather2_copies(land_refs, shard_shapes, kinds)
        for a in range(n):
            for k in FORWARD_KS:
                arrival = stage2(a, k, in_sems, True)
                arrival.wait_send()
                arrival.wait_recv()

    _, lands, _ = _gather2_call(lands, sems2, name, after, make_body, False)
    return lands


def _pad_mla_w_in(w):
    d = w.shape[0]
    z = lambda n: jnp.zeros((d, n), w.dtype)
    o1, o2, o3 = Q_RANK, Q_RANK + KV_RANK, Q_RANK + KV_RANK + ROPE
    return jnp.concatenate([w[:, :o1], z(Q_RANK_PAD - Q_RANK), w[:, o1:o2], w[:, o2:o3], z(LANE - ROPE), w[:, o3:]], axis=1)


def _unpad_mla_w_in(g):
    return jnp.concatenate([g[:, :Q_RANK], g[:, PROJ_CKV:PROJ_KR], g[:, PROJ_KR:PROJ_KR + ROPE], g[:, PROJ_Z:]], axis=1)


def _pad_w_uq(w):
    w3 = w.reshape(Q_RANK, HEADS, NOPE + ROPE)
    w3 = jnp.pad(w3, ((0, Q_RANK_PAD - Q_RANK), (0, 0), (0, HEAD_PAD - NOPE - ROPE)))
    return w3.reshape(Q_RANK_PAD, HEADS * HEAD_PAD)


def _unpad_w_uq(g):
    return g[:Q_RANK].reshape(Q_RANK, HEADS, HEAD_PAD)[:, :, :NOPE + ROPE].reshape(Q_RANK, HEADS * (NOPE + ROPE))


def _pack(pieces):
    flat = [p.reshape(-1).astype(F32) for p in pieces]
    tot = sum(f.shape[0] for f in flat)
    unit = SUBLANE * LANE
    padn = (-tot) % unit
    if padn:
        flat.append(jnp.zeros((padn,), F32))
    return jnp.concatenate(flat).reshape(-1, LANE)


def _unpack(packed, shapes, lead=()):
    flat = packed.reshape(tuple(lead) + (-1,))
    out, off = [], 0
    for shp in shapes:
        nel = int(np.prod(shp))
        out.append(flat[..., off:off + nel].reshape(tuple(lead) + tuple(shp)))
        off += nel
    return out


EARLY_GROUP = [('ada_b', lambda t: t[1:]), ('ada_b', lambda t: t[0:1, 2 * t.shape[1] // 3:]),
               ('pre_g', lambda t: t[1:]), ('post_g', lambda t: t), ('sgu_norm_g', lambda t: t),
               ('sgu_w_s', lambda t: t), ('sgu_b_s', lambda t: t), ('mla_q_norm_g', lambda t: t),
               ('mla_kv_norm_g', lambda t: t)]
LATE_GROUP = [('ada_b', lambda t: t[0:1, :2 * t.shape[1] // 3]), ('pre_g', lambda t: t[0:1])]


WEIGHTS = ['ada_w', 'ada_b', 'pre_g', 'post_g', 'sgu_w_in', 'sgu_norm_g', 'sgu_w_s', 'sgu_b_s', 'sgu_w_out',
           'mla_w_in', 'mla_q_norm_g', 'mla_kv_norm_g', 'mla_w_uq', 'mla_w_ukv', 'mla_w_out']
INPUTS = ['x', 'c'] + WEIGHTS + ['loss_target'] + ['m_' + n for n in WEIGHTS] + ['v_' + n for n in WEIGHTS]


def kernel(x, c, ada_w, ada_b, pre_g, post_g, sgu_w_in, sgu_norm_g, sgu_w_s, sgu_b_s, sgu_w_out, mla_w_in, mla_q_norm_g, mla_kv_norm_g, mla_w_uq, mla_w_ukv, mla_w_out, loss_target, m_ada_w, m_ada_b, m_pre_g, m_post_g, m_sgu_w_in, m_sgu_norm_g, m_sgu_w_s, m_sgu_b_s, m_sgu_w_out, m_mla_w_in, m_mla_q_norm_g, m_mla_kv_norm_g, m_mla_w_uq, m_mla_w_ukv, m_mla_w_out, v_ada_w, v_ada_b, v_pre_g, v_post_g, v_sgu_w_in, v_sgu_norm_g, v_sgu_w_s, v_sgu_b_s, v_sgu_w_out, v_mla_w_in, v_mla_q_norm_g, v_mla_kv_norm_g, v_mla_w_uq, v_mla_w_ukv, v_mla_w_out):
    given = locals()
    A = {name: given[name] for name in INPUTS}
    x0 = A['x'][0]
    tgt = A['loss_target'][0]
    s, d = x0.shape
    e = 2 * d
    ncol = 3 * d // N_DEV
    _, _, _, me = _my_place()
    ktabs, qtabs = _rope_tables(s)

    gains = jnp.zeros((SUBLANE, LANE), F32)
    gains = gains.at[0:2, :Q_RANK // N_DEV].set(A['mla_q_norm_g'])
    gains = gains.at[2:4, :KV_RANK // N_DEV].set(A['mla_kv_norm_g'])
    c8 = jnp.broadcast_to(A['c'], (SUBLANE, d))
    cg, gg = _exchange([c8, gains], ["row", "row"], True, "ag_cond")
    cond_raw = cg.reshape(N_DEV, SUBLANE, d)[:, 0, :]
    gg = gg.reshape(N_DEV, SUBLANE, LANE)
    gq_full = jnp.transpose(gg[:, 0:2, :Q_RANK // N_DEV], (1, 0, 2)).reshape(N_MIX, Q_RANK)
    gkv_full = jnp.transpose(gg[:, 2:4, :KV_RANK // N_DEV], (1, 0, 2)).reshape(N_MIX, KV_RANK)
    gq_pad = jnp.pad(gq_full, ((0, 0), (0, Q_RANK_PAD - Q_RANK)))

    bias_my = lax.dynamic_slice_in_dim(A['ada_b'], me * ncol, ncol, axis=1)
    mod_part = _ada_mod(cond_raw, A['ada_w'], bias_my, "ada_mod")
    send = jnp.pad(jnp.transpose(mod_part, (1, 0, 2)), ((0, 0), (0, SUBLANE - DEPTH), (0, 0)))
    (rb,) = _exchange([send.reshape(N_DEV * SUBLANE, ncol)], ["row"], False, "a2a_mod")

    token = rb
    gathers = {}
    for i in range(DEPTH):
        j = i // N_MIX
        if i % N_MIX == 0:
            parts = [("in", [A['sgu_w_in'][j]], ["col"]), ("out", [A['sgu_w_out'][j]], ["row"])]
        else:
            parts = [("all", [A['mla_w_in'][j], A['mla_w_uq'][j], A['mla_w_ukv'][j], A['mla_w_out'][j]],
                      ["col", "col", "col", "row"])]
        for part, shards, kinds in parts:
            gathers[(i, part)], token = _gather2_start([t.astype(BF16) for t in shards], kinds,
                                                       f"ag_start_{i}_{part}", token)

    def forward_gathers(i, after):
        for key in [k for k in gathers if k[0] == i]:
            gathers[key], _ = _gather2_forward(gathers[key], f"ag_forward_{key[0]}_{key[1]}", after)

    gathers[(0, "in")], token = _gather2_forward(gathers[(0, "in")], "ag_forward_0_in", token)
    mod = jnp.transpose(rb.reshape(N_DEV, SUBLANE, ncol)[:, :DEPTH, :], (1, 0, 2)).reshape(DEPTH, 3 * d) + token[0, 0]
    shift = [mod[i:i + 1, :d] for i in range(DEPTH)]
    scale = [mod[i:i + 1, d:2 * d] for i in range(DEPTH)]
    gate = [mod[i:i + 1, 2 * d:] for i in range(DEPTH)]

    saved = []
    x = x0
    for i in range(DEPTH):
        j = i // N_MIX
        h = _pre_fwd(x, A['pre_g'][i:i + 1], scale[i], shift[i], f"pre_fwd")
        if i % N_MIX == 0:
            (w_in,) = _gather2_wait(gathers[(i, "in")], f"ag_wait_{i}_in", h)
            uvz = _mm(h, w_in, "nn", F32, "sgu_in")
            if i == 0:
                gathers[(0, "out")], _ = _gather2_forward(gathers[(0, "out")], "ag_forward_0_out", uvz)
            bias_full = jnp.repeat(A['sgu_b_s'][j].T, e // SGU_GROUPS, axis=1)
            ng = A['sgu_norm_g'][j:j + 1]
            y = _sgu_mid_fwd(uvz, ng, A['sgu_w_s'][j], bias_full, "sgu_mid_fwd")
            forward_gathers(i + 1, y)
            (w_out,) = _gather2_wait(gathers[(i, "out")], f"ag_wait_{i}_out", y)
            out = _mm(y, w_out, "nn", F32, "sgu_out")
            saved.append(dict(x=x, h=h, uvz=uvz, y=y, out=out, w_in=w_in, w_out=w_out, bias=bias_full, ng=ng))
        else:
            w_in, w_uq, w_ukv, w_out = _gather2_wait(gathers[(i, "all")], f"ag_wait_{i}_all", h)
            w_in = _pad_mla_w_in(w_in)
            w_uq = _pad_w_uq(w_uq)
            gq, gkv = gq_pad[j:j + 1], gkv_full[j:j + 1]
            proj = _mm(h, w_in, "nn", F32, "mla_in")
            qn, kvn, kr = _mla_norm_fwd(proj, gq, gkv, ktabs, "mla_norm_fwd")
            q = _mm(qn, w_uq, "nn", F32, "mla_uq")
            q_cat = _rope_heads(q, qtabs, 1.0, SOFTMAX_SCALE * LOG2_E, BF16, "rope_q_fwd")
            kv = _mm(kvn, w_ukv, "nn", BF16, "mla_ukv")
            o, y, lse = _attn_fwd(q_cat, kv, kr, proj, "attn_fwd")
            forward_gathers(i + 1, y)
            out = _mm(y, w_out, "nn", F32, "mla_out")
            saved.append(dict(x=x, h=h, proj=proj, qn=qn, kvn=kvn, kr=kr, q_cat=q_cat, kv=kv, o=o, y=y, lse=lse,
                              out=out, w_in=w_in, w_uq=w_uq, w_ukv=w_ukv, w_out=w_out, gq=gq, gkv=gkv))
        x = _post_fwd(x, out, gate[i], A['post_g'][i:i + 1], "post_fwd")

    dx, loss_row = _loss_head(x, tgt, "loss_head")
    loss = lax.psum(loss_row[0, 0], ("x", "y", "c"))

    d_shift, d_scale, d_gate = [None] * DEPTH, [None] * DEPTH, [None] * DEPTH
    d_pre, d_post = [None] * DEPTH, [None] * DEPTH
    scatters, scatters_out = [None] * DEPTH, [None] * DEPTH
    small = {}
    for i in reversed(range(DEPTH)):
        j = i // N_MIX
        sv = saved[i]
        dout, d_gate[i], d_post[i] = _post_bwd(dx, sv['out'], gate[i], A['post_g'][i:i + 1], "post_bwd")
        if i % N_MIX == 0:
            dy = _mm(dout, sv['w_out'], "nt", F32, "sgu_out_dx")
            g_w_out = _mm(sv['y'], dout, "tn", BF16, "sgu_out_dw")
            scatters_out[i], token, (dy,) = _exchange_start([g_w_out], ["row"], False, f"rs_out_start_{i}", token,
                                                            carry=[dy])
            duvz, dws, dbs, dng = _sgu_mid_bwd(sv['uvz'], dy, sv['ng'], A['sgu_w_s'][j], sv['bias'], "sgu_mid_bwd")
            small[('sgu', j)] = (dws, dbs.reshape(SGU_GROUPS, SGU_BLOCK), dng)
            if i == 0:
                early_grads = [
                    jnp.concatenate([jnp.concatenate([d_shift[l], d_scale[l], d_gate[l]], axis=1)
                                     for l in range(1, DEPTH)], axis=0),
                    d_gate[0], jnp.concatenate(d_pre[1:], axis=0), jnp.concatenate(d_post, axis=0),
                    jnp.concatenate([small[('sgu', jj)][2] for jj in range(N_MIX)], axis=0),
                    jnp.stack([small[('sgu', jj)][0] for jj in range(N_MIX)]),
                    jnp.stack([small[('sgu', jj)][1] for jj in range(N_MIX)]),
                    jnp.concatenate([small[('mla', jj)][0] for jj in range(N_MIX)], axis=0),
                    jnp.concatenate([small[('mla', jj)][1] for jj in range(N_MIX)], axis=0)]
                early_small, token, _ = _exchange_start([_pack(early_grads)], ["row"], True, "ag_small_early_start",
                                                        token)
            g_w_in = _mm(sv['h'], duvz, "tn", BF16, "sgu_in_dw")
            scatters[i], token, (duvz,) = _exchange_start([g_w_in], ["col"], False, f"rs_start_{i}", token,
                                                          carry=[duvz])
            dh = _mm(duvz, sv['w_in'], "nt", F32, "sgu_in_dx")
        else:
            dy = _mm(dout, sv['w_out'], "nt", F32, "mla_out_dx")
            g_w_out = _mm(sv['y'], dout, "tn", BF16, "mla_out_dw")
            scatters_out[i], token, (dy,) = _exchange_start([g_w_out], ["row"], False, f"rs_out_start_{i}", token,
                                                            carry=[dy])
            do, dz = _mla_gate_bwd(dy, sv['o'], sv['proj'], "mla_gate_bwd")
            dq, dkv, dkr_heads = _attn_bwd(sv['q_cat'], sv['kv'], sv['kr'], do, sv['o'], sv['lse'], "attn_bwd")
            dq_b = _rope_heads(dq, qtabs, -1.0, SOFTMAX_SCALE, BF16, "rope_q_bwd")
            dqn = _mm(dq_b, sv['w_uq'], "nt", F32, "mla_uq_dx")
            g_w_uq = _unpad_w_uq(_mm(sv['qn'], dq_b, "tn", BF16, "mla_uq_dw"))
            dkvn = _mm(dkv, sv['w_ukv'], "nt", F32, "mla_ukv_dx")
            g_w_ukv = _mm(sv['kvn'], dkv, "tn", BF16, "mla_ukv_dw")
            dpa, dgq, dgkv = _mla_norm_bwd(dqn, dkvn, dkr_heads, sv['proj'], sv['gq'], sv['gkv'], ktabs, "mla_norm_bwd")
            dproj = jnp.concatenate([dpa, dz], axis=1)
            g_w_in = _unpad_mla_w_in(_mm(sv['h'], dproj, "tn", BF16, "mla_in_dw"))
            scatters[i], token, (dproj,) = _exchange_start(
                [g_w_in, g_w_uq, g_w_ukv], ["col", "col", "col"], False, f"rs_start_{i}", token, carry=[dproj])
            dh = _mm(dproj, sv['w_in'], "nt", F32, "mla_in_dx")
            small[('mla', j)] = (dgq[:, :Q_RANK], dgkv)
        dx, d_shift[i], d_scale[i], d_pre[i] = _pre_bwd(dh, sv['x'], dx, A['pre_g'][i:i + 1], scale[i], "pre_bwd")

    res = {}

    def big(name, recv0, recv1):
        res[name] = _adam_reduce(recv0, recv1, A[name], A['m_' + name], A['v_' + name], "adam_" + name)

    def small_full(name, prefix):
        t = A[prefix + name]
        if name in ('mla_q_norm_g', 'mla_kv_norm_g'):
            t = lax.dynamic_update_slice_in_dim(jnp.zeros((t.shape[0], t.shape[1] * N_DEV), F32), t,
                                                me * t.shape[1], axis=1)
        return t

    def finish(i, after):
        return (_exchange_wait(scatters[i], f"rs_wait_{i}", after)
                + _exchange_wait(scatters_out[i], f"rs_out_wait_{i}", after))

    recv_mla = {j: finish(N_MIX * j + 1, dx) for j in reversed(range(N_MIX))}
    for idx_w, name in enumerate(['mla_w_in', 'mla_w_uq', 'mla_w_ukv', 'mla_w_out']):
        big(name, recv_mla[0][idx_w], recv_mla[1][idx_w])

    recv_sgu = {j: finish(N_MIX * j, res['mla_w_out'][0]) for j in reversed(range(N_MIX))}
    for idx_w, name in enumerate(['sgu_w_in', 'sgu_w_out']):
        big(name, recv_sgu[0][idx_w], recv_sgu[1][idx_w])

    late_grads = [jnp.concatenate([d_shift[0], d_scale[0]], axis=1), d_pre[0]]
    (gathered_early,) = _exchange_wait(early_small, "ag_small_early_wait", res['sgu_w_out'][0])
    (gathered_late,) = _exchange([_pack(late_grads)], ["row"], True, "ag_small_late", after=res['sgu_w_out'][0])

    def small_update(group, grads, gathered, tag):
        shapes = [g.shape for g in grads]
        packs = [_pack([pick(small_full(name, prefix)) for name, pick in group]) for prefix in ('', 'm_', 'v_')]
        outs = _adam_small(gathered.reshape(N_DEV, -1, LANE), *packs, "adam_small_" + tag)
        return [_unpack(t, shapes) for t in outs], _unpack(gathered.reshape(N_DEV, -1), shapes, lead=(N_DEV,))

    upd_e, parts_e = small_update(EARLY_GROUP, early_grads, gathered_early, "early")
    upd_l, parts_l = small_update(LATE_GROUP, late_grads, gathered_late, "late")
    for k_out in range(4):
        ab_rest, ab_gate0, pre_rest, post_all, ng_all, ws_all, bs_all, gq_all, gkv_all = upd_e[k_out]
        ab_0, pre_0 = upd_l[k_out]
        res.setdefault('ada_b', [None] * 4)[k_out] = jnp.concatenate(
            [jnp.concatenate([ab_0, ab_gate0], axis=1), ab_rest], axis=0)
        res.setdefault('pre_g', [None] * 4)[k_out] = jnp.concatenate([pre_0, pre_rest], axis=0)
        res.setdefault('post_g', [None] * 4)[k_out] = post_all
        res.setdefault('sgu_norm_g', [None] * 4)[k_out] = ng_all
        res.setdefault('sgu_w_s', [None] * 4)[k_out] = ws_all
        res.setdefault('sgu_b_s', [None] * 4)[k_out] = bs_all
        for name, val in (('mla_q_norm_g', gq_all), ('mla_kv_norm_g', gkv_all)):
            wdt = A[name].shape[1]
            res.setdefault(name, [None] * 4)[k_out] = lax.dynamic_slice_in_dim(val, me * wdt, wdt, axis=1)

    dmod_all = jnp.concatenate([jnp.concatenate([parts_l[0], parts_e[1]], axis=2), parts_e[0]], axis=1)
    dmod_cols = jnp.transpose(lax.dynamic_slice_in_dim(dmod_all, me * ncol, ncol, axis=2), (1, 0, 2))
    res['ada_w'] = _ada_bwd_adam(jnp.transpose(cond_raw), dmod_cols, A['ada_w'], A['m_ada_w'], A['v_ada_w'], "ada_bwd")

    outs = [loss, dx[None]]
    for k_out in range(4):
        outs += [res[n][k_out] for n in WEIGHTS]
    return tuple(outs)
```

```python
import functools
import math

import numpy as np
import jax
import jax.numpy as jnp
from jax import lax
from jax.experimental import pallas as pl
from jax.experimental.pallas import tpu as pltpu

F32 = jnp.float32
BF16 = jnp.bfloat16
MESH = pl.DeviceIdType.MESH

N_DEV = 8
DEPTH = 4
N_MIX = 2
NORM_EPS = 1e-6
CHUNK = 64
SGU_BLOCK = 128
SGU_GROUPS = 16
HEADS = 16
Q_RANK = 448
Q_RANK_PAD = 512
KV_RANK = 512
NOPE = 128
ROPE = 64
HALF = ROPE // 2
V_DIM = 128
HEAD_PAD = 256
ROPE_THETA = 10000.0
MLA_WIDTH = HEADS * V_DIM
LANE = 128
SUBLANE = 8
PROJ_CQ = 0
PROJ_CKV = Q_RANK_PAD
PROJ_KR = Q_RANK_PAD + KV_RANK
PROJ_Z = PROJ_KR + LANE
PROJ_W = PROJ_Z + MLA_WIDTH

ADAM_LR = 0.001
ADAM_B1 = 0.9
ADAM_B2 = 0.999
ADAM_EPS = 1e-08
ADAM_WD = 0.01
ADAM_STEP = 10

VMEM_LIMIT = 56 * 1024 * 1024
ATT_BLK = 512
ATT_SUB = 128
ROW_BLK = 256
MM_TM, MM_TN, MM_TK = 1024, 1024, 2048
MM_TILE_BYTES = 40 * 1024 * 1024
SOFTMAX_SCALE = (NOPE + ROPE) ** -0.5
LOG2_E = 1.0 / math.log(2.0)
INV_SQRT2 = 1.0 / math.sqrt(2.0)
INV_SQRT_2PI = 1.0 / math.sqrt(2.0 * math.pi)


def _pcall(body, comm=False, **kw):
    return pl.pallas_call(body, **kw)


def _params(sem=None):
    return pltpu.CompilerParams(dimension_semantics=sem, vmem_limit_bytes=VMEM_LIMIT)


def _pick(dim, pref):
    if dim <= pref:
        return dim
    t = (pref // LANE) * LANE
    while t >= LANE:
        if dim % t == 0:
            return t
        t -= LANE
    return dim


def _gelu(x):
    return 0.5 * x * (1.0 + lax.erf(x * INV_SQRT2))


def _gelu_grad(x):
    return 0.5 * (1.0 + lax.erf(x * INV_SQRT2)) + x * jnp.exp(-0.5 * x * x) * INV_SQRT_2PI


def _sigmoid(x):
    return 1.0 / (1.0 + jnp.exp(-x))


def _dot_nt(a, b):
    return lax.dot_general(a, b, (((1,), (1,)), ((), ())), preferred_element_type=F32)


def _dot_tn(a, b):
    return lax.dot_general(a, b, (((0,), (0,)), ((), ())), preferred_element_type=F32)


def _mm(a, b, dims, out_dtype, name):
    if dims == "nn":
        (m, k), (k2, n) = a.shape, b.shape
    elif dims == "nt":
        (m, k), (n, k2) = a.shape, b.shape
    else:
        (k, m), (k2, n) = a.shape, b.shape
    assert k == k2, (a.shape, b.shape, dims)
    tm, tn = _pick(m, MM_TM), _pick(n, MM_TN)
    out_bytes = 2 * tm * tn * jnp.dtype(out_dtype).itemsize
    whole_k = 2 * (tm + tn) * k * a.dtype.itemsize + out_bytes <= MM_TILE_BYTES
    tk = k if whole_k else _pick(k, MM_TK)
    nk = k // tk

    def body(a_ref, b_ref, o_ref, *scratch):
        if dims == "nn":
            p = jnp.dot(a_ref[...], b_ref[...], preferred_element_type=F32)
        elif dims == "nt":
            p = _dot_nt(a_ref[...], b_ref[...])
        else:
            p = _dot_tn(a_ref[...], b_ref[...])
        if nk == 1:
            o_ref[...] = p.astype(o_ref.dtype)
            return
        acc_ref, = scratch
        kk = pl.program_id(2)

        @pl.when(kk == 0)
        def _():
            acc_ref[...] = p

        @pl.when(kk > 0)
        def _():
            acc_ref[...] += p

        @pl.when(kk == nk - 1)
        def _():
            o_ref[...] = acc_ref[...].astype(o_ref.dtype)

    if dims == "tn":
        a_spec = pl.BlockSpec((tk, tm), lambda i, j, kk: (kk, i))
    else:
        a_spec = pl.BlockSpec((tm, tk), lambda i, j, kk: (i, kk))
    if dims == "nt":
        b_spec = pl.BlockSpec((tn, tk), lambda i, j, kk: (j, kk))
    else:
        b_spec = pl.BlockSpec((tk, tn), lambda i, j, kk: (kk, j))
    return _pcall(
        body, name=name,
        grid=(m // tm, n // tn, nk),
        in_specs=[a_spec, b_spec],
        out_specs=pl.BlockSpec((tm, tn), lambda i, j, kk: (i, j)),
        out_shape=jax.ShapeDtypeStruct((m, n), out_dtype),
        scratch_shapes=[pltpu.VMEM((tm, tn), F32)] if nk > 1 else [],
        compiler_params=_params(("parallel", "parallel", "arbitrary")),
    )(a, b)


def _row_spec(ts, d):
    return pl.BlockSpec((ts, d), lambda i: (i, 0))


def _vec_spec(d):
    return pl.BlockSpec((1, d), lambda i: (0, 0))


def _pre_fwd(x, g, scale, shift, name):
    s, d = x.shape
    ts = _pick(s, ROW_BLK)

    def body(x_ref, g_ref, sc_ref, sh_ref, h_ref):
        xv = x_ref[...]
        r = lax.rsqrt(jnp.mean(xv * xv, axis=-1, keepdims=True) + NORM_EPS)
        h_ref[...] = ((xv * r * g_ref[...]) * (1.0 + sc_ref[...]) + sh_ref[...]).astype(BF16)

    return _pcall(
        body, name=name, grid=(s // ts,),
        in_specs=[_row_spec(ts, d), _vec_spec(d), _vec_spec(d), _vec_spec(d)],
        out_specs=_row_spec(ts, d),
        out_shape=jax.ShapeDtypeStruct((s, d), BF16),
        compiler_params=_params(("parallel",)),
    )(x, g, scale, shift)


def _post_fwd(x, out, gate, g, name):
    s, d = x.shape
    ts = _pick(s, ROW_BLK)

    def body(x_ref, o_ref, gate_ref, g_ref, y_ref):
        o = o_ref[...]
        r = lax.rsqrt(jnp.mean(o * o, axis=-1, keepdims=True) + NORM_EPS)
        y_ref[...] = x_ref[...] + gate_ref[...] * (o * r * g_ref[...])

    return _pcall(
        body, name=name, grid=(s // ts,),
        in_specs=[_row_spec(ts, d), _row_spec(ts, d), _vec_spec(d), _vec_spec(d)],
        out_specs=_row_spec(ts, d),
        out_shape=jax.ShapeDtypeStruct((s, d), F32),
        compiler_params=_params(("parallel",)),
    )(x, out, gate, g)


def _loss_head(xf, tgt, name):
    s, d = xf.shape
    ts = _pick(s, ROW_BLK)
    ns = s // ts

    def body(x_ref, t_ref, dx_ref, loss_ref, acc_ref):
        i = pl.program_id(0)

        @pl.when(i == 0)
        def _():
            acc_ref[...] = jnp.zeros_like(acc_ref)

        e = x_ref[...] - t_ref[...]
        dx_ref[...] = e * (1.0 / d)
        acc_ref[...] += jnp.sum(e * e, axis=0, keepdims=True)

        @pl.when(i == ns - 1)
        def _():
            tot = jnp.sum(acc_ref[...], axis=1, keepdims=True) * (0.5 / d)
            loss_ref[...] = jnp.broadcast_to(tot, loss_ref.shape)

    return _pcall(
        body, name=name, grid=(ns,),
        in_specs=[_row_spec(ts, d), _row_spec(ts, d)],
        out_specs=[_row_spec(ts, d), pl.BlockSpec((1, LANE), lambda i: (0, 0))],
        out_shape=[jax.ShapeDtypeStruct((s, d), F32), jax.ShapeDtypeStruct((1, LANE), F32)],
        scratch_shapes=[pltpu.VMEM((1, d), F32)],
        compiler_params=_params(("arbitrary",)),
    )(xf, tgt)


def _post_bwd(dxo, out, gate, g, name):
    s, d = dxo.shape
    ts = _pick(s, ROW_BLK)

    def body(dx_ref, o_ref, gate_ref, g_ref, do_ref, dgate_ref, dg_ref):
        i = pl.program_id(0)

        @pl.when(i == 0)
        def _():
            dgate_ref[...] = jnp.zeros_like(dgate_ref)
            dg_ref[...] = jnp.zeros_like(dg_ref)

        o = o_ref[...]
        dx = dx_ref[...]
        gv = g_ref[...]
        r = lax.rsqrt(jnp.mean(o * o, axis=-1, keepdims=True) + NORM_EPS)
        n = o * r
        dyn = dx * gate_ref[...]
        dgate_ref[...] += jnp.sum(dx * (n * gv), axis=0, keepdims=True)
        dg_ref[...] += jnp.sum(dyn * n, axis=0, keepdims=True)
        dn = dyn * gv
        do_ref[...] = (r * (dn - n * jnp.mean(dn * n, axis=-1, keepdims=True))).astype(BF16)

    return _pcall(
        body, name=name, grid=(s // ts,),
        in_specs=[_row_spec(ts, d), _row_spec(ts, d), _vec_spec(d), _vec_spec(d)],
        out_specs=[_row_spec(ts, d), _vec_spec(d), _vec_spec(d)],
        out_shape=[jax.ShapeDtypeStruct((s, d), BF16), jax.ShapeDtypeStruct((1, d), F32),
                   jax.ShapeDtypeStruct((1, d), F32)],
        compiler_params=_params(("arbitrary",)),
    )(dxo, out, gate, g)


def _pre_bwd(dh, x, dxo, g, scale, name):
    s, d = x.shape
    ts = _pick(s, ROW_BLK)

    def body(dh_ref, x_ref, dxo_ref, g_ref, sc_ref, dx_ref, dsh_ref, dsc_ref, dg_ref):
        i = pl.program_id(0)

        @pl.when(i == 0)
        def _():
            dsh_ref[...] = jnp.zeros_like(dsh_ref)
            dsc_ref[...] = jnp.zeros_like(dsc_ref)
            dg_ref[...] = jnp.zeros_like(dg_ref)

        xv = x_ref[...]
        dhv = dh_ref[...]
        gv = g_ref[...]
        one_sc = 1.0 + sc_ref[...]
        r = lax.rsqrt(jnp.mean(xv * xv, axis=-1, keepdims=True) + NORM_EPS)
        n = xv * r
        dsh_ref[...] += jnp.sum(dhv, axis=0, keepdims=True)
        dsc_ref[...] += jnp.sum(dhv * (n * gv), axis=0, keepdims=True)
        dng = dhv * one_sc
        dg_ref[...] += jnp.sum(dng * n, axis=0, keepdims=True)
        dn = dng * gv
        dx_ref[...] = dxo_ref[...] + r * (dn - n * jnp.mean(dn * n, axis=-1, keepdims=True))

    return _pcall(
        body, name=name, grid=(s // ts,),
        in_specs=[_row_spec(ts, d), _row_spec(ts, d), _row_spec(ts, d), _vec_spec(d), _vec_spec(d)],
        out_specs=[_row_spec(ts, d), _vec_spec(d), _vec_spec(d), _vec_spec(d)],
        out_shape=[jax.ShapeDtypeStruct((s, d), F32)] + [jax.ShapeDtypeStruct((1, d), F32)] * 3,
        compiler_params=_params(("arbitrary",)),
    )(dh, x, dxo, g, scale)


def _sgu_mask():
    t = lax.broadcasted_iota(jnp.int32, (SGU_BLOCK, SGU_BLOCK), 0) // CHUNK
    s = lax.broadcasted_iota(jnp.int32, (SGU_BLOCK, SGU_BLOCK), 1) // CHUNK
    return s <= t


def _sgu_norm(v_pre, g):
    e = v_pre.shape[-1]
    vg = _gelu(v_pre)
    mu = jnp.sum(vg, axis=-1, keepdims=True) * (1.0 / e)
    dlt = vg - mu
    var = jnp.sum(dlt * dlt, axis=-1, keepdims=True) * (1.0 / e)
    rstd = lax.rsqrt(var + NORM_EPS)
    vhat = dlt * rstd
    return vhat, rstd, (vhat * g).astype(BF16)


def _sgu_mid_fwd(uvz, norm_g, w_s, bias_full, name):
    s, e3 = uvz.shape
    e = e3 // 3
    gd = e // SGU_GROUPS
    nb = s // SGU_BLOCK

    def body(uvz_ref, g_ref, w_ref, b_ref, y_ref, wsc):
        @pl.when(pl.program_id(0) == 0)
        def _():
            msk = _sgu_mask()
            for gi in range(SGU_GROUPS):
                wsc[gi] = jnp.where(msk, w_ref[gi], 0.0).astype(BF16)

        _, _, vb = _sgu_norm(uvz_ref[:, e:2 * e], g_ref[...])
        for gi in range(SGU_GROUPS):
            lo = gi * gd
            vm = jnp.dot(wsc[gi], vb[:, lo:lo + gd], preferred_element_type=F32) + b_ref[:, lo:lo + gd]
            zg = uvz_ref[:, 2 * e + lo:2 * e + lo + gd]
            y_ref[:, lo:lo + gd] = (_gelu(uvz_ref[:, lo:lo + gd]) * vm * (zg * _sigmoid(zg))).astype(BF16)

    return _pcall(
        body, name=name, grid=(nb,),
        in_specs=[pl.BlockSpec((SGU_BLOCK, e3), lambda n: (n, 0)),
                  pl.BlockSpec((1, e), lambda n: (0, 0)),
                  pl.BlockSpec((SGU_GROUPS, SGU_BLOCK, SGU_BLOCK), lambda n: (0, 0, 0)),
                  pl.BlockSpec((SGU_BLOCK, e), lambda n: (0, 0))],
        out_specs=pl.BlockSpec((SGU_BLOCK, e), lambda n: (n, 0)),
        out_shape=jax.ShapeDtypeStruct((s, e), BF16),
        scratch_shapes=[pltpu.VMEM((SGU_GROUPS, SGU_BLOCK, SGU_BLOCK), BF16)],
        compiler_params=_params(("arbitrary",)),
    )(uvz, norm_g, w_s, bias_full)


def _sgu_mid_bwd(uvz, dy, norm_g, w_s, bias_full, name):
    s, e3 = uvz.shape
    e = e3 // 3
    gd = e // SGU_GROUPS
    nb = s // SGU_BLOCK

    def body(uvz_ref, dy_ref, g_ref, w_ref, b_ref, d_ref, dw_ref, db_ref, dg_ref, wsc, wtsc, dvh_sc, dbacc):
        n = pl.program_id(0)

        @pl.when(n == 0)
        def _():
            msk = _sgu_mask()
            for gi in range(SGU_GROUPS):
                wm = jnp.where(msk, w_ref[gi], 0.0)
                wsc[gi] = wm.astype(BF16)
                wtsc[gi] = wm.T.astype(BF16)
            dw_ref[...] = jnp.zeros_like(dw_ref)
            dg_ref[...] = jnp.zeros_like(dg_ref)
            dbacc[...] = jnp.zeros_like(dbacc)

        v_pre = uvz_ref[:, e:2 * e]
        gv = g_ref[...]
        vhat, rstd, vb = _sgu_norm(v_pre, gv)
        s1 = jnp.zeros((SGU_BLOCK, 1), F32)
        s2 = jnp.zeros((SGU_BLOCK, 1), F32)
        for gi in range(SGU_GROUPS):
            lo = gi * gd
            u_pre = uvz_ref[:, lo:lo + gd]
            zg = uvz_ref[:, 2 * e + lo:2 * e + lo + gd]
            dyg = dy_ref[:, lo:lo + gd]
            ug = _gelu(u_pre)
            sig = _sigmoid(zg)
            vbg = vb[:, lo:lo + gd]
            vhg = vhat[:, lo:lo + gd]
            vm = jnp.dot(wsc[gi], vbg, preferred_element_type=F32) + b_ref[:, lo:lo + gd]
            t = dyg * (zg * sig)
            d_ref[:, lo:lo + gd] = (t * vm * _gelu_grad(u_pre)).astype(BF16)
            dvm = t * ug
            d_ref[:, 2 * e + lo:2 * e + lo + gd] = (dyg * ug * vm * (sig * (1.0 + zg * (1.0 - sig)))).astype(BF16)
            dvm_b = dvm.astype(BF16)
            dv = jnp.dot(wtsc[gi], dvm_b, preferred_element_type=F32)
            dw_ref[gi] += _dot_nt(dvm_b, vbg)
            dbacc[:, lo:lo + gd] += dvm
            dg_ref[:, lo:lo + gd] += jnp.sum(dv * vhg, axis=0, keepdims=True)
            dvh = dv * gv[:, lo:lo + gd]
            dvh_sc[:, lo:lo + gd] = dvh
            s1 = s1 + jnp.sum(dvh, axis=-1, keepdims=True)
            s2 = s2 + jnp.sum(dvh * vhg, axis=-1, keepdims=True)
        dvg = rstd * (dvh_sc[...] - s1 * (1.0 / e) - vhat * (s2 * (1.0 / e)))
        d_ref[:, e:2 * e] = (dvg * _gelu_grad(v_pre)).astype(BF16)

        @pl.when(n == nb - 1)
        def _():
            msk = _sgu_mask()
            for gi in range(SGU_GROUPS):
                dw_ref[gi] = jnp.where(msk, dw_ref[gi], 0.0)
                db_ref[gi] = jnp.sum(dbacc[:, gi * gd:(gi + 1) * gd], axis=1, keepdims=True)

    return _pcall(
        body, name=name, grid=(nb,),
        in_specs=[pl.BlockSpec((SGU_BLOCK, e3), lambda n: (n, 0)),
                  pl.BlockSpec((SGU_BLOCK, e), lambda n: (n, 0)),
                  pl.BlockSpec((1, e), lambda n: (0, 0)),
                  pl.BlockSpec((SGU_GROUPS, SGU_BLOCK, SGU_BLOCK), lambda n: (0, 0, 0)),
                  pl.BlockSpec((SGU_BLOCK, e), lambda n: (0, 0))],
        out_specs=[pl.BlockSpec((SGU_BLOCK, e3), lambda n: (n, 0)),
                   pl.BlockSpec((SGU_GROUPS, SGU_BLOCK, SGU_BLOCK), lambda n: (0, 0, 0)),
                   pl.BlockSpec((SGU_GROUPS, SGU_BLOCK, 1), lambda n: (0, 0, 0)),
                   pl.BlockSpec((1, e), lambda n: (0, 0))],
        out_shape=[jax.ShapeDtypeStruct((s, e3), BF16),
                   jax.ShapeDtypeStruct((SGU_GROUPS, SGU_BLOCK, SGU_BLOCK), F32),
                   jax.ShapeDtypeStruct((SGU_GROUPS, SGU_BLOCK, 1), F32),
                   jax.ShapeDtypeStruct((1, e), F32)],
        scratch_shapes=[pltpu.VMEM((SGU_GROUPS, SGU_BLOCK, SGU_BLOCK), BF16),
                        pltpu.VMEM((SGU_GROUPS, SGU_BLOCK, SGU_BLOCK), BF16),
                        pltpu.VMEM((SGU_BLOCK, e), F32),
                        pltpu.VMEM((SGU_BLOCK, e), F32)],
        compiler_params=_params(("arbitrary",)),
    )(uvz, dy, norm_g, w_s, bias_full)


def _rope_tables(s):
    pos = jnp.arange(s, dtype=F32)
    inv_freq = ROPE_THETA ** (-jnp.arange(0, ROPE, 2, dtype=F32) / ROPE)
    ang = pos[:, None] * inv_freq[None, :]
    cos, sin = jnp.cos(ang), jnp.sin(ang)
    z32 = jnp.zeros((s, HALF), F32)
    z64 = jnp.zeros((s, ROPE), F32)
    ck = jnp.concatenate([cos, cos, z64], axis=1)
    s1k = jnp.concatenate([-sin, z32, z64], axis=1)
    s2k = jnp.concatenate([z32, sin, z64], axis=1)
    one = jnp.ones((s, NOPE), F32)
    zn = jnp.zeros((s, NOPE), F32)
    return (ck, s1k, s2k), (jnp.concatenate([one, ck], axis=1), jnp.concatenate([zn, s1k], axis=1),
                            jnp.concatenate([zn, s2k], axis=1))


def _rot(x, c, s1, s2):
    w = x.shape[-1]
    return x * c + pltpu.roll(x, w - HALF, 1) * s1 + pltpu.roll(x, HALF, 1) * s2


def _rms(cv, n_real):
    r = lax.rsqrt(jnp.sum(cv * cv, axis=-1, keepdims=True) * (1.0 / n_real) + NORM_EPS)
    return r, cv * r


def _mla_norm_fwd(proj, gq, gkv, tabs, name):
    s = proj.shape[0]
    ts = _pick(s, ROW_BLK)
    ck, s1k, s2k = tabs

    def body(cq_ref, ckv_ref, kr_ref, gq_ref, gkv_ref, c_ref, s1_ref, s2_ref, qn_ref, kvn_ref, kro_ref):
        _, nq = _rms(cq_ref[...], Q_RANK)
        qn_ref[...] = (nq * gq_ref[...]).astype(BF16)
        _, nkv = _rms(ckv_ref[...], KV_RANK)
        kvn_ref[...] = (nkv * gkv_ref[...]).astype(BF16)
        kro_ref[...] = _rot(kr_ref[...], c_ref[...], s1_ref[...], s2_ref[...]).astype(BF16)

    tab = pl.BlockSpec((ts, LANE), lambda i: (i, 0))
    return _pcall(
        body, name=name, grid=(s // ts,),
        in_specs=[pl.BlockSpec((ts, Q_RANK_PAD), lambda i: (i, 0)),
                  pl.BlockSpec((ts, KV_RANK), lambda i: (i, PROJ_CKV // KV_RANK)),
                  pl.BlockSpec((ts, LANE), lambda i: (i, PROJ_KR // LANE)),
                  _vec_spec(Q_RANK_PAD), _vec_spec(KV_RANK), tab, tab, tab],
        out_specs=[pl.BlockSpec((ts, Q_RANK_PAD), lambda i: (i, 0)),
                   pl.BlockSpec((ts, KV_RANK), lambda i: (i, 0)), tab],
        out_shape=[jax.ShapeDtypeStruct((s, Q_RANK_PAD), BF16), jax.ShapeDtypeStruct((s, KV_RANK), BF16),
                   jax.ShapeDtypeStruct((s, LANE), BF16)],
        compiler_params=_params(("parallel",)),
    )(proj, proj, proj, gq, gkv, ck, s1k, s2k)


def _rope_heads(q, tabs, sign, mult, out_dtype, name):
    s, w = q.shape
    ts = _pick(s, ROW_BLK)
    c, s1, s2 = tabs

    def body(q_ref, c_ref, s1_ref, s2_ref, o_ref):
        cv, s1v, s2v = mult * c_ref[...], (sign * mult) * s1_ref[...], (sign * mult) * s2_ref[...]
        for h in range(w // HEAD_PAD):
            lo = h * HEAD_PAD
            o_ref[:, lo:lo + HEAD_PAD] = _rot(q_ref[:, lo:lo + HEAD_PAD].astype(F32), cv, s1v, s2v).astype(out_dtype)

    blk = pl.BlockSpec((ts, w), lambda i: (i, 0))
    tab = pl.BlockSpec((ts, HEAD_PAD), lambda i: (i, 0))
    return _pcall(
        body, name=name, grid=(s // ts,),
        in_specs=[blk, tab, tab, tab], out_specs=blk,
        out_shape=jax.ShapeDtypeStruct((s, w), out_dtype),
        compiler_params=_params(("parallel",)),
    )(q, c, s1, s2)


def _transpose_bf16(t):
    return t.astype(F32).T.astype(BF16)


def _diag_mask(tb, transposed):
    r = lax.broadcasted_iota(jnp.int32, (tb, tb), 0) // CHUNK
    c = lax.broadcasted_iota(jnp.int32, (tb, tb), 1) // CHUNK
    return (r <= c) if transposed else (c <= r)


def _attn_fwd(q_cat, kv, kr, proj, name):
    s = q_cat.shape[0]
    tb = _pick(s, ATT_BLK)
    nb = s // tb
    zcol = PROJ_Z // V_DIM

    def body(q_ref, kn_ref, v_ref, kr_ref, z_ref, o_ref, y_ref, lse_ref, kt_sc, vx_sc, m_sc, acc_sc, sa_sc, sb_sc):
        qi = pl.program_id(1)

        @pl.when(qi == 0)
        def _():
            for b in range(nb):
                rows = slice(b * tb, (b + 1) * tb)
                kt_sc[b] = _transpose_bf16(jnp.concatenate([kn_ref[rows, :], kr_ref[rows, :]], axis=1))
                vx_sc[b] = jnp.concatenate([v_ref[rows, :], jnp.ones((tb, V_DIM), BF16)], axis=1)

        m_sc[...] = jnp.full_like(m_sc, -1e30)
        acc_sc[...] = jnp.zeros_like(acc_sc)
        sub = min(tb, ATT_SUB)

        def scores(ki, s_ref):
            s_ref[...] = jnp.dot(q_ref[...], kt_sc[ki], preferred_element_type=F32)

        def step(ki, s_ref, masked):
            for r in range(tb // sub):
                rs = slice(r * sub, (r + 1) * sub)
                sc = s_ref[rs, :]
                if masked:
                    sc = jnp.where(_diag_mask(tb, False)[rs, :], sc, -1e30)
                m_prev = m_sc[rs, :]
                m_new = jnp.maximum(m_prev, jnp.max(sc, axis=-1, keepdims=True))
                p = jnp.exp2(sc - m_new).astype(BF16)
                acc_sc[rs, :] = (jnp.exp2(m_prev - m_new) * acc_sc[rs, :]
                                 + jnp.dot(p, vx_sc[ki], preferred_element_type=F32))
                m_sc[rs, :] = m_new

        def pair(t, carry):
            scores(2 * t + 1, sb_sc)
            step(2 * t, sa_sc, False)
            scores(2 * t + 2, sa_sc)
            step(2 * t + 1, sb_sc, False)
            return carry

        scores(0, sa_sc)
        lax.fori_loop(0, qi // 2, pair, 0)

        @pl.when(qi % 2 == 1)
        def _():
            scores(qi, sb_sc)
            step(qi - 1, sa_sc, False)
            step(qi, sb_sc, True)

        @pl.when(qi % 2 == 0)
        def _():
            step(qi, sa_sc, True)

        l = acc_sc[:, V_DIM:V_DIM + 1]
        o = acc_sc[:, :V_DIM] / l
        z = z_ref[...]
        o_ref[...] = o.astype(BF16)
        y_ref[...] = (o * (z * _sigmoid(z))).astype(BF16)
        lse_cols = jnp.broadcast_to(m_sc[...] + jnp.log2(l), (tb, LANE))
        lse_ref[...] = lse_cols.T[0:1, :]

    oblk = pl.BlockSpec((tb, V_DIM), lambda h, qi: (qi, h))
    return _pcall(
        body, name=name, grid=(HEADS, nb),
        in_specs=[pl.BlockSpec((tb, HEAD_PAD), lambda h, qi: (qi, h)),
                  pl.BlockSpec((s, NOPE), lambda h, qi: (0, 2 * h)),
                  pl.BlockSpec((s, V_DIM), lambda h, qi: (0, 2 * h + 1)),
                  pl.BlockSpec((s, LANE), lambda h, qi: (0, 0)),
                  pl.BlockSpec((tb, V_DIM), lambda h, qi: (qi, zcol + h))],
        out_specs=[oblk, oblk, pl.BlockSpec((None, None, 1, tb), lambda h, qi: (h, qi, 0, 0))],
        out_shape=[jax.ShapeDtypeStruct((s, MLA_WIDTH), BF16), jax.ShapeDtypeStruct((s, MLA_WIDTH), BF16),
                   jax.ShapeDtypeStruct((HEADS, nb, 1, tb), F32)],
        scratch_shapes=[pltpu.VMEM((nb, HEAD_PAD, tb), BF16), pltpu.VMEM((nb, tb, HEAD_PAD), BF16),
                        pltpu.VMEM((tb, 1), F32), pltpu.VMEM((tb, HEAD_PAD), F32),
                        pltpu.VMEM((tb, tb), F32), pltpu.VMEM((tb, tb), F32)],
        compiler_params=_params(("parallel", "arbitrary")),
    )(q_cat, kv, kv, kr, proj)


def _attn_bwd(q_cat, kv, kr, do, o, lse, name):
    s = q_cat.shape[0]
    tb = _pick(s, ATT_BLK)
    nb = s // tb
    ln2 = math.log(2.0)

    def body(q_ref, do_ref, o_ref, lse_ref, kn_ref, v_ref, kr_ref, dq_ref, dkv_ref, dkr_ref,
             qt_sc, dot_sc, delta_sc, dqt_sc, dk_sc, dv_sc):
        ki = pl.program_id(1)

        @pl.when(ki == 0)
        def _():
            for b in range(nb):
                rows = slice(b * tb, (b + 1) * tb)
                qt_sc[b] = _transpose_bf16(q_ref[rows, :])
                do_t = do_ref[rows, :].astype(F32).T
                dot_sc[b] = do_t.astype(BF16)
                delta_sc[b] = jnp.sum(do_t * o_ref[rows, :].astype(F32).T, axis=0, keepdims=True)
            dqt_sc[...] = jnp.zeros_like(dqt_sc)

        k = jnp.concatenate([kn_ref[...], kr_ref[...]], axis=1)
        kt = _transpose_bf16(k)
        vb = v_ref[...]
        dk_sc[...] = jnp.zeros_like(dk_sc)
        dv_sc[...] = jnp.zeros_like(dv_sc)

        def step(qi, masked):
            rows = pl.ds(pl.multiple_of(qi * tb, tb), tb)
            sc_t = jnp.dot(k, qt_sc[qi], preferred_element_type=F32)
            if masked:
                sc_t = jnp.where(_diag_mask(tb, True), sc_t, -1e30)
            p_t = jnp.exp2(sc_t - lse_ref[qi])
            dp_t = jnp.dot(vb, dot_sc[qi], preferred_element_type=F32)
            ds_t = (p_t * (dp_t - delta_sc[qi])).astype(BF16)
            dv_sc[...] += jnp.dot(p_t.astype(BF16), do_ref[rows, :], preferred_element_type=F32)
            dk_sc[...] += jnp.dot(ds_t, q_ref[rows, :], preferred_element_type=F32)
            dqt_sc[qi] += jnp.dot(kt, ds_t, preferred_element_type=F32)

        step(ki, True)

        def loop_body(qi, carry):
            step(qi, False)
            return carry

        lax.fori_loop(ki + 1, nb, loop_body, 0)

        dkv_ref[:, :NOPE] = (dk_sc[:, :NOPE] * ln2).astype(BF16)
        dkv_ref[:, NOPE:] = dv_sc[...].astype(BF16)
        dkr_ref[...] = dk_sc[:, NOPE:] * ln2

        @pl.when(ki == nb - 1)
        def _():
            for b in range(nb):
                dq_ref[b * tb:(b + 1) * tb, :] = dqt_sc[b].T

    return _pcall(
        body, name=name, grid=(HEADS, nb),
        in_specs=[pl.BlockSpec((s, HEAD_PAD), lambda h, ki: (0, h)),
                  pl.BlockSpec((s, V_DIM), lambda h, ki: (0, h)),
                  pl.BlockSpec((s, V_DIM), lambda h, ki: (0, h)),
                  pl.BlockSpec((None, nb, 1, tb), lambda h, ki: (h, 0, 0, 0)),
                  pl.BlockSpec((tb, NOPE), lambda h, ki: (ki, 2 * h)),
                  pl.BlockSpec((tb, V_DIM), lambda h, ki: (ki, 2 * h + 1)),
                  pl.BlockSpec((tb, LANE), lambda h, ki: (ki, 0))],
        out_specs=[pl.BlockSpec((s, HEAD_PAD), lambda h, ki: (0, h)),
                   pl.BlockSpec((tb, HEAD_PAD), lambda h, ki: (ki, h)),
                   pl.BlockSpec((None, tb, LANE), lambda h, ki: (h, ki, 0))],
        out_shape=[jax.ShapeDtypeStruct((s, HEADS * HEAD_PAD), F32),
                   jax.ShapeDtypeStruct((s, HEADS * HEAD_PAD), BF16),
                   jax.ShapeDtypeStruct((HEADS, s, LANE), F32)],
        scratch_shapes=[pltpu.VMEM((nb, HEAD_PAD, tb), BF16), pltpu.VMEM((nb, V_DIM, tb), BF16),
                        pltpu.VMEM((nb, 1, tb), F32), pltpu.VMEM((nb, HEAD_PAD, tb), F32),
                        pltpu.VMEM((tb, HEAD_PAD), F32), pltpu.VMEM((tb, V_DIM), F32)],
        compiler_params=_params(("parallel", "arbitrary")),
    )(q_cat, do, o, lse, kv, kv, kr)


def _mla_gate_bwd(dy, o, proj, name):
    s = dy.shape[0]
    ts = _pick(s, ROW_BLK)
    zcol = PROJ_Z // V_DIM

    def body(dy_ref, o_ref, p_ref, do_ref, dz_ref):
        z = p_ref[:, PROJ_Z:]
        dyv = dy_ref[...]
        sig = _sigmoid(z)
        do_ref[...] = (dyv * (z * sig)).astype(BF16)
        dz_ref[...] = (dyv * o_ref[...].astype(F32) * (sig * (1.0 + z * (1.0 - sig)))).astype(BF16)

    blk = pl.BlockSpec((ts, MLA_WIDTH), lambda i: (i, 0))
    return _pcall(
        body, name=name, grid=(s // ts,),
        in_specs=[blk, blk, pl.BlockSpec((ts, PROJ_W), lambda i: (i, 0))],
        out_specs=[blk, blk],
        out_shape=[jax.ShapeDtypeStruct((s, MLA_WIDTH), BF16), jax.ShapeDtypeStruct((s, MLA_WIDTH), BF16)],
        compiler_params=_params(("parallel",)),
    )(dy, o, proj)


def _mla_norm_bwd(dqn, dkvn, dkr_heads, proj, gq, gkv, tabs, name):
    s = proj.shape[0]
    ts = _pick(s, ROW_BLK)
    ck, s1k, s2k = tabs

    def rms_bwd(cv, dn_in, g, n_real):
        r, n = _rms(cv, n_real)
        dg = jnp.sum(dn_in * n, axis=0, keepdims=True)
        dn = dn_in * g
        dc = r * (dn - n * (jnp.sum(dn * n, axis=-1, keepdims=True) * (1.0 / n_real)))
        return dc, dg

    def body(dqn_ref, dkvn_ref, dkr_ref, cq_ref, ckv_ref, gq_ref, gkv_ref, c_ref, s1_ref, s2_ref,
             dp_ref, dgq_ref, dgkv_ref):
        @pl.when(pl.program_id(0) == 0)
        def _():
            dgq_ref[...] = jnp.zeros_like(dgq_ref)
            dgkv_ref[...] = jnp.zeros_like(dgkv_ref)

        dcq, dgq = rms_bwd(cq_ref[...], dqn_ref[...], gq_ref[...], Q_RANK)
        dckv, dgkv = rms_bwd(ckv_ref[...], dkvn_ref[...], gkv_ref[...], KV_RANK)
        dgq_ref[...] += dgq
        dgkv_ref[...] += dgkv
        dkr = dkr_ref[0]
        for h in range(1, HEADS):
            dkr = dkr + dkr_ref[h]
        dp_ref[:, PROJ_CQ:PROJ_CKV] = dcq.astype(BF16)
        dp_ref[:, PROJ_CKV:PROJ_KR] = dckv.astype(BF16)
        dp_ref[:, PROJ_KR:PROJ_Z] = _rot(dkr, c_ref[...], -s1_ref[...], -s2_ref[...]).astype(BF16)

    tab = pl.BlockSpec((ts, LANE), lambda i: (i, 0))
    return _pcall(
        body, name=name, grid=(s // ts,),
        in_specs=[pl.BlockSpec((ts, Q_RANK_PAD), lambda i: (i, 0)),
                  pl.BlockSpec((ts, KV_RANK), lambda i: (i, 0)),
                  pl.BlockSpec((HEADS, ts, LANE), lambda i: (0, i, 0)),
                  pl.BlockSpec((ts, Q_RANK_PAD), lambda i: (i, 0)),
                  pl.BlockSpec((ts, KV_RANK), lambda i: (i, PROJ_CKV // KV_RANK)),
                  _vec_spec(Q_RANK_PAD), _vec_spec(KV_RANK), tab, tab, tab],
        out_specs=[pl.BlockSpec((ts, PROJ_Z), lambda i: (i, 0)), _vec_spec(Q_RANK_PAD), _vec_spec(KV_RANK)],
        out_shape=[jax.ShapeDtypeStruct((s, PROJ_Z), BF16), jax.ShapeDtypeStruct((1, Q_RANK_PAD), F32),
                   jax.ShapeDtypeStruct((1, KV_RANK), F32)],
        compiler_params=_params(("arbitrary",)),
    )(dqn, dkvn, dkr_heads, proj, proj, gq, gkv, ck, s1k, s2k)


def _ada_mod(cond_raw, ada_w, bias_my, name):
    nl, d, ncol = ada_w.shape
    tk = _pick(d, 512)
    nk = d // tk

    def body(c_ref, w_ref, b_ref, o_ref, acc_ref):
        kk = pl.program_id(1)

        @pl.when(kk == 0)
        def _():
            acc_ref[...] = jnp.zeros_like(acc_ref)

        cv = c_ref[...]
        cond = (cv * _sigmoid(cv)).astype(BF16)
        acc_ref[...] += jnp.dot(cond, w_ref[...].astype(BF16), preferred_element_type=F32)

        @pl.when(kk == nk - 1)
        def _():
            o_ref[...] = acc_ref[...] + b_ref[...]

    return _pcall(
        body, name=name, grid=(nl, nk),
        in_specs=[pl.BlockSpec((N_DEV, tk), lambda l, kk: (0, kk)),
                  pl.BlockSpec((None, tk, ncol), lambda l, kk: (l, kk, 0)),
                  pl.BlockSpec((None, 1, ncol), lambda l, kk: (l, 0, 0))],
        out_specs=pl.BlockSpec((None, N_DEV, ncol), lambda l, kk: (l, 0, 0)),
        out_shape=jax.ShapeDtypeStruct((nl, N_DEV, ncol), F32),
        scratch_shapes=[pltpu.VMEM((N_DEV, ncol), F32)],
        compiler_params=_params(("parallel", "arbitrary")),
    )(cond_raw, ada_w, bias_my.reshape(nl, 1, ncol))


def _adam(w, g, m, v):
    m = ADAM_B1 * m + (1.0 - ADAM_B1) * g
    v = ADAM_B2 * v + (1.0 - ADAM_B2) * (g * g)
    m_hat = m / (1.0 - ADAM_B1 ** ADAM_STEP)
    v_hat = v / (1.0 - ADAM_B2 ** ADAM_STEP)
    delta = -ADAM_LR * (m_hat / (jnp.sqrt(v_hat) + ADAM_EPS) + ADAM_WD * w)
    return delta, m, v


def _ada_bwd_adam(cond_t, dmod_cols, w, m, v, name):
    nl, d, ncol = w.shape
    tk = _pick(d, 512)

    def body(c_ref, dm_ref, w_ref, m_ref, v_ref, g_ref, d_ref, mo_ref, vo_ref):
        cv = c_ref[...]
        cond = (cv * _sigmoid(cv)).astype(BF16)
        g = jnp.dot(cond, dm_ref[...].astype(BF16), preferred_element_type=F32)
        delta, m2, v2 = _adam(w_ref[...], g, m_ref[...], v_ref[...])
        g_ref[...] = g
        d_ref[...] = delta
        mo_ref[...] = m2
        vo_ref[...] = v2

    blk = pl.BlockSpec((None, tk, ncol), lambda l, kk: (l, kk, 0))
    shp = jax.ShapeDtypeStruct((nl, d, ncol), F32)
    return _pcall(
        body, name=name, grid=(nl, d // tk),
        in_specs=[pl.BlockSpec((tk, N_DEV), lambda l, kk: (kk, 0)),
                  pl.BlockSpec((None, N_DEV, ncol), lambda l, kk: (l, 0, 0)), blk, blk, blk],
        out_specs=[blk, blk, blk, blk], out_shape=[shp, shp, shp, shp],
        compiler_params=_params(("parallel", "parallel")),
    )(cond_t, dmod_cols, w, m, v)


def _adam_reduce(recv0, recv1, w, m, v, name):
    nl, r, c = w.shape
    tr = _pick(r, 128) if r % 128 == 0 else r
    tc = _pick(c, 1024)
    n0, n1 = recv0.shape[0] // r, recv1.shape[0] // r

    def body(r0_ref, r1_ref, w_ref, m_ref, v_ref, g_ref, d_ref, mo_ref, vo_ref):
        l = pl.program_id(0)

        def run(rr):
            g = rr[0].astype(F32)
            for sidx in range(1, rr.shape[0]):
                g = g + rr[sidx].astype(F32)
            delta, m2, v2 = _adam(w_ref[...], g, m_ref[...], v_ref[...])
            g_ref[...] = g
            d_ref[...] = delta
            mo_ref[...] = m2
            vo_ref[...] = v2

        @pl.when(l == 0)
        def _():
            run(r0_ref)

        @pl.when(l == 1)
        def _():
            run(r1_ref)

    def rblk(n, layer):
        return pl.BlockSpec((n, tr, tc), lambda l, i, j: (0, jnp.where(l == layer, i, 0), jnp.where(l == layer, j, 0)))

    blk = pl.BlockSpec((None, tr, tc), lambda l, i, j: (l, i, j))
    shp = jax.ShapeDtypeStruct((nl, r, c), F32)
    return _pcall(
        body, name=name, grid=(nl, r // tr, c // tc),
        in_specs=[rblk(n0, 0), rblk(n1, 1), blk, blk, blk],
        out_specs=[blk, blk, blk, blk], out_shape=[shp, shp, shp, shp],
        compiler_params=_params(("arbitrary", "parallel", "parallel")),
    )(recv0.reshape(n0, r, c), recv1.reshape(n1, r, c), w, m, v)


def _adam_small(gathered, w, m, v, name):
    r = w.shape[0]
    tr = _pick(r, 512) if r % 512 == 0 else r

    def body(p_ref, w_ref, m_ref, v_ref, g_ref, d_ref, mo_ref, vo_ref):
        g = p_ref[0]
        for sidx in range(1, N_DEV):
            g = g + p_ref[sidx]
        delta, m2, v2 = _adam(w_ref[...], g, m_ref[...], v_ref[...])
        g_ref[...] = g
        d_ref[...] = delta
        mo_ref[...] = m2
        vo_ref[...] = v2

    blk = pl.BlockSpec((tr, LANE), lambda i: (i, 0))
    shp = jax.ShapeDtypeStruct((r, LANE), F32)
    return _pcall(
        body, name=name, grid=(r // tr,),
        in_specs=[pl.BlockSpec((N_DEV, tr, LANE), lambda i: (0, i, 0)), blk, blk, blk],
        out_specs=[blk, blk, blk, blk], out_shape=[shp, shp, shp, shp],
        compiler_params=_params(("parallel",)),
    )(gathered, w, m, v)


def _my_place():
    x, y, c = lax.axis_index("x"), lax.axis_index("y"), lax.axis_index("c")
    return x, y, c, 4 * x + 2 * y + c


def _peer(x, y, c, k):
    px = 1 - x if (k >> 2) & 1 else x
    py = 1 - y if (k >> 1) & 1 else y
    pc = 1 - c if k & 1 else c
    return (px, py, pc), 4 * px + 2 * py + pc


def _slab(ref, shape, kind, p):
    r, cd = shape
    if kind == "row":
        return ref.at[pl.ds(pl.multiple_of(p * r, SUBLANE), r), :]
    return ref.at[:, pl.ds(pl.multiple_of(p * cd, LANE), cd)]


def _exchange_layout(arrays, kinds, gather):
    shard_shapes, dst_kinds, out_shapes = [], [], []
    for a, kind in zip(arrays, kinds):
        r, cd = a.shape
        if gather:
            shard, dst_kind = (r, cd), kind
        else:
            shard, dst_kind = ((r // N_DEV, cd) if kind == "row" else (r, cd // N_DEV)), "row"
        shard_shapes.append(shard)
        dst_kinds.append(dst_kind)
        full = (shard[0] * N_DEV, shard[1]) if dst_kind == "row" else (shard[0], shard[1] * N_DEV)
        out_shapes.append(jax.ShapeDtypeStruct(full, a.dtype))
    return shard_shapes, dst_kinds, out_shapes


def _exchange_copies(ins, outs, send_sems, recv_sems, sem_of, layout, kinds, gather):
    shard_shapes, dst_kinds, _ = layout
    x, y, c, me = _my_place()

    def src_for(a, p):
        return ins[a] if gather else _slab(ins[a], shard_shapes[a], kinds[a], p)

    def dst_slot(a, p):
        return _slab(outs[a], shard_shapes[a], dst_kinds[a], p)

    def local(a, sem):
        return pltpu.make_async_copy(src_for(a, me), dst_slot(a, me), sem)

    def remote(a, k, slot):
        peer, pidx = _peer(x, y, c, k)
        return pltpu.make_async_remote_copy(
            src_ref=src_for(a, pidx), dst_ref=dst_slot(a, me if slot == "mine" else pidx),
            send_sem=send_sems.at[sem_of(a, k)], recv_sem=recv_sems.at[sem_of(a, k)],
            device_id=peer, device_id_type=MESH)

    return local, remote


def _place_own(src, src_kind, slab_shape, dst_kind, full, name, index=None):
    r, cd = slab_shape
    tr = _pick(r, 512)
    nr = r // tr
    me = _my_place()[3] if index is None else index

    def body(me_ref, s_ref, o_ref):
        o_ref[...] = s_ref[...].astype(o_ref.dtype)

    def where(kind):
        if kind is None:
            return lambda i, me_ref: (i, 0)
        if kind == "row":
            return lambda i, me_ref: (me_ref[0] * nr + i, 0)
        return lambda i, me_ref: (i, me_ref[0])

    return _pcall(
        body, name=name,
        grid_spec=pltpu.PrefetchScalarGridSpec(
            num_scalar_prefetch=1, grid=(nr,),
            in_specs=[pl.BlockSpec((tr, cd), where(src_kind))],
            out_specs=pl.BlockSpec((tr, cd), where(dst_kind))),
        out_shape=jax.ShapeDtypeStruct(full.shape, full.dtype),
        compiler_params=_params(("arbitrary",)),
    )(jnp.reshape(me, (1,)).astype(jnp.int32), src)


def _landing_with_own_slab(arrays, kinds, gather, layout, name=None):
    _, _, _, me = _my_place()
    lands = []
    for a in range(len(arrays)):
        (r, cd), dst_kind, full = layout[0][a], layout[1][a], layout[2][a]
        if name is not None:
            lands.append(_place_own(arrays[a], None if gather else kinds[a], (r, cd), dst_kind, full, name))
            continue
        if gather:
            piece = arrays[a]
        elif kinds[a] == "row":
            piece = lax.dynamic_slice_in_dim(arrays[a], me * r, r, axis=0)
        else:
            piece = lax.dynamic_slice_in_dim(arrays[a], me * cd, cd, axis=1)
        at = (me * r, 0) if dst_kind == "row" else (0, me * cd)
        lands.append(lax.dynamic_update_slice(lax.empty(full.shape, full.dtype), piece, at))
    return lands


def _exchange(arrays, kinds, gather, name, after=None):
    n = len(arrays)
    n_extra = 0 if after is None else 1
    layout = _exchange_layout(arrays, kinds, gather)
    lands = _landing_with_own_slab(arrays, kinds, gather, layout)

    def body(*refs):
        ins, outs = refs[:n], refs[2 * n + n_extra:3 * n + n_extra]
        send_sems, recv_sems = refs[3 * n + n_extra:]
        _, remote = _exchange_copies(ins, outs, send_sems, recv_sems,
                                     lambda a, k: a * (N_DEV - 1) + k - 1, layout, kinds, gather)
        for a in range(n):
            for k in range(1, N_DEV):
                remote(a, k, "mine").start()
        for a in range(n):
            for k in range(1, N_DEV):
                arrival = remote(a, k, "theirs")
                arrival.wait_send()
                arrival.wait_recv()

    anyspec = pl.BlockSpec(memory_space=pl.ANY)
    outs = _pcall(
        body, comm=True, name=name,
        in_specs=[anyspec] * (2 * n + n_extra), out_specs=[anyspec] * n, out_shape=layout[2],
        input_output_aliases={n + a: a for a in range(n)},
        scratch_shapes=[pltpu.SemaphoreType.DMA((n * (N_DEV - 1),)), pltpu.SemaphoreType.DMA((n * (N_DEV - 1),))],
    )(*arrays, *lands, *([] if after is None else [after]))
    return list(outs)


HBM_SPEC = pl.BlockSpec(memory_space=pltpu.HBM)
SEM_SPEC = pl.BlockSpec(memory_space=pltpu.SEMAPHORE)
ANY_SPEC = pl.BlockSpec(memory_space=pl.ANY)
DATAFLOW = pltpu.SideEffectType.DATAFLOW_SIDE_EFFECTING


def _exchange_start(arrays, kinds, gather, name, after, carry=()):
    n, nc = len(arrays), len(carry)
    layout = _exchange_layout(arrays, kinds, gather)
    lands = _landing_with_own_slab(arrays, kinds, gather, layout, "place_own")

    def body(*refs):
        ins, outs = refs[:n], refs[n:2 * n]
        send_sems, recv_sems = refs[2 * n + nc + 1], refs[2 * n + nc + 2]
        token = refs[2 * n + nc + 3 + 2 * n + nc]
        _, remote = _exchange_copies(ins, outs, send_sems, recv_sems, lambda a, k: a, layout, kinds, gather)
        for a in range(n):
            for k in range(1, N_DEV):
                remote(a, k, "mine").start()
        token[...] = jnp.zeros_like(token)

    passed = list(arrays) + lands + list(carry)
    res = pl.pallas_call(
        body, name=name,
        out_shape=(pltpu.SemaphoreType.DMA((n,)), pltpu.SemaphoreType.DMA((n,)),
                   *[pltpu.HBM(t.shape, t.dtype) for t in passed], jax.ShapeDtypeStruct((SUBLANE, LANE), F32)),
        in_specs=[HBM_SPEC] * (2 * n + nc) + [ANY_SPEC],
        out_specs=(SEM_SPEC, SEM_SPEC, *([HBM_SPEC] * (2 * n + nc)), pl.BlockSpec(memory_space=pltpu.VMEM)),
        input_output_aliases={i: 2 + i for i in range(2 * n + nc)},
        compiler_params=pltpu.CompilerParams(has_side_effects=DATAFLOW),
    )(*[pltpu.with_memory_space_constraint(t, pltpu.HBM) for t in passed], after)
    handle = (res[0], res[1], list(res[2:2 + n]), list(res[2 + n:2 + 2 * n]), tuple(kinds), gather)
    return handle, res[-1], list(res[2 + 2 * n:2 + 2 * n + nc])


def _exchange_wait(handle, name, after):
    send_sems, recv_sems, ins_thru, lands_thru, kinds, gather = handle
    n = len(ins_thru)
    layout = _exchange_layout(ins_thru, kinds, gather)

    def body(*refs):
        ins, outs = refs[:n], refs[n:2 * n]
        s_sems, r_sems = refs[2 * n], refs[2 * n + 1]
        _, remote = _exchange_copies(ins, outs, s_sems, r_sems, lambda a, k: a, layout, kinds, gather)
        for a in range(n):
            for k in range(1, N_DEV):
                arrival = remote(a, k, "theirs")
                arrival.wait_send()
                arrival.wait_recv()

    res = pl.pallas_call(
        body, name=name,
        out_shape=[pltpu.HBM(t.shape, t.dtype) for t in ins_thru + lands_thru],
        in_specs=[HBM_SPEC] * (2 * n) + [SEM_SPEC, SEM_SPEC, ANY_SPEC],
        out_specs=[HBM_SPEC] * (2 * n),
        input_output_aliases={i: i for i in range(2 * n)},
        compiler_params=pltpu.CompilerParams(has_side_effects=DATAFLOW),
    )(*ins_thru, *lands_thru, send_sems, recv_sems, after)
    return list(res[n:2 * n])


STAGE1_KS = (1, 2, 4, 6)
FORWARD_KS = (2, 4, 6)


def _gather2_copies(lands, shard_shapes, kinds):
    x, y, c, me = _my_place()

    def slab(a, p):
        return _slab(lands[a], shard_shapes[a], kinds[a], p)

    def stage1(a, k, sems, arriving):
        peer, pidx = _peer(x, y, c, k)
        s = slab(a, pidx if arriving else me)
        return pltpu.make_async_remote_copy(src_ref=s, dst_ref=s, send_sem=sems[0].at[a], recv_sem=sems[1].at[a],
                                            device_id=peer, device_id_type=MESH)

    def stage2(a, k, sems, arriving):
        sib, _ = _peer(x, y, c, 1)
        _, mine = _peer(x, y, c, k)
        _, theirs = _peer(x, y, 1 - c, k)
        s = slab(a, theirs if arriving else mine)
        return pltpu.make_async_remote_copy(src_ref=s, dst_ref=s, send_sem=sems[0].at[a], recv_sem=sems[1].at[a],
                                            device_id=sib, device_id_type=MESH)

    return stage1, stage2


def _gather2_call(lands, sems_in, name, after, make_body, returns_sems):
    n = len(lands)
    n_in = len(sems_in)

    def body(*refs):
        land_refs = refs[:n]
        in_sems = refs[n:n + n_in]
        rest = refs[n + n_in + 1:]
        out_sems = rest[:2] if returns_sems else ()
        make_body(land_refs, in_sems, out_sems)
        if returns_sems:
            token = rest[2 + n]
            token[...] = jnp.zeros_like(token)

    sem_shapes = (pltpu.SemaphoreType.DMA((n,)), pltpu.SemaphoreType.DMA((n,))) if returns_sems else ()
    tok_shape = (jax.ShapeDtypeStruct((SUBLANE, LANE), F32),) if returns_sems else ()
    n_sem_out = len(sem_shapes)
    res = pl.pallas_call(
        body, name=name,
        out_shape=(*sem_shapes, *[pltpu.HBM(t.shape, t.dtype) for t in lands], *tok_shape),
        in_specs=[HBM_SPEC] * n + [SEM_SPEC] * n_in + [ANY_SPEC],
        out_specs=(*([SEM_SPEC] * n_sem_out), *([HBM_SPEC] * n),
                   *([pl.BlockSpec(memory_space=pltpu.VMEM)] if returns_sems else [])),
        input_output_aliases={i: n_sem_out + i for i in range(n)},
        compiler_params=pltpu.CompilerParams(has_side_effects=DATAFLOW),
    )(*[pltpu.with_memory_space_constraint(t, pltpu.HBM) for t in lands], *sems_in, after)
    sems_out = tuple(res[:n_sem_out])
    lands_thru = list(res[n_sem_out:n_sem_out + n])
    return sems_out, lands_thru, (res[-1] if returns_sems else None)


def _gather2_start(shards, kinds, name, after):
    n = len(shards)
    layout = _exchange_layout(shards, kinds, True)
    lands = _landing_with_own_slab(shards, kinds, True, layout, "place_own")

    def make_body(land_refs, in_sems, out_sems):
        stage1, _ = _gather2_copies(land_refs, layout[0], kinds)
        for a in range(n):
            for k in STAGE1_KS:
                stage1(a, k, out_sems, False).start()

    sems, lands, token = _gather2_call(lands, (), name, after, make_body, True)
    return (sems, lands, layout[0], tuple(kinds)), token


def _gather2_forward(handle, name, after):
    sems1, lands, shard_shapes, kinds = handle
    n = len(lands)

    def make_body(land_refs, in_sems, out_sems):
        stage1, stage2 = _gather2_copies(land_refs, shard_shapes, kinds)
        for a in range(n):
            for k in STAGE1_KS:
                arrival = stage1(a, k, in_sems, True)
                arrival.wait_send()
                arrival.wait_recv()
        for a in range(n):
            for k in FORWARD_KS:
                stage2(a, k, out_sems, False).start()

    sems2, lands, token = _gather2_call(lands, sems1, name, after, make_body, True)
    return (sems2, lands, shard_shapes, kinds), token


def _gather2_wait(handle, name, after):
    sems2, lands, shard_shapes, kinds = handle
    n = len(lands)

    def make_body(land_refs, in_sems, out_sems):
        _, stage2 = _gather2_copies(land_refs, shard_shapes, kinds)
        for a in range(n):
            for k in FORWARD_KS:
                arrival = stage2(a, k, in_sems, True)
                arrival.wait_send()
                arrival.wait_recv()

    _, lands, _ = _gather2_call(lands, sems2, name, after, make_body, False)
    return lands


N_CHIP = N_DEV // 2


def _scatter2_pair(g, name):
    r, c8 = g.shape
    cd = c8 // N_DEV

    def body(g_ref, o_ref, send_sems, recv_sems):
        x, y, c, _ = _my_place()
        sib, _ = _peer(x, y, c, 1)

        def copy(ch):
            theirs = 2 * ch + (1 - c)
            return pltpu.make_async_remote_copy(
                src_ref=_slab(g_ref, (r, cd), "col", theirs), dst_ref=_slab(o_ref, (r, cd), "row", ch),
                send_sem=send_sems.at[ch], recv_sem=recv_sems.at[ch], device_id=sib, device_id_type=MESH)

        for ch in range(N_CHIP):
            copy(ch).start()
        for ch in range(N_CHIP):
            copy(ch).wait_send()
            copy(ch).wait_recv()

    return _pcall(
        body, comm=True, name=name,
        in_specs=[ANY_SPEC], out_specs=ANY_SPEC, out_shape=jax.ShapeDtypeStruct((N_CHIP * r, cd), g.dtype),
        scratch_shapes=[pltpu.SemaphoreType.DMA((N_CHIP,)), pltpu.SemaphoreType.DMA((N_CHIP,))],
    )(g)


def _scatter2_add(g, from_sibling, name):
    r, c8 = g.shape
    cd = c8 // N_DEV
    tr = _pick(r, 512)
    nr = r // tr
    _, _, core, _ = _my_place()

    def body(c_ref, g_ref, s_ref, o_ref):
        o_ref[...] = (g_ref[...].astype(F32) + s_ref[...].astype(F32)).astype(o_ref.dtype)

    return _pcall(
        body, name=name,
        grid_spec=pltpu.PrefetchScalarGridSpec(
            num_scalar_prefetch=1, grid=(N_CHIP, nr),
            in_specs=[pl.BlockSpec((tr, cd), lambda ch, i, c_ref: (i, 2 * ch + c_ref[0])),
                      pl.BlockSpec((tr, cd), lambda ch, i, c_ref: (ch * nr + i, 0))],
            out_specs=pl.BlockSpec((tr, cd), lambda ch, i, c_ref: (ch * nr + i, 0))),
        out_shape=jax.ShapeDtypeStruct((N_CHIP * r, cd), g.dtype),
        compiler_params=_params(("arbitrary", "arbitrary")),
    )(jnp.reshape(core, (1,)).astype(jnp.int32), g, from_sibling)


def _scatter2_copies(refs, r):
    part_ref, recv_ref = refs
    x, y, c, _ = _my_place()
    my_chip = 2 * x + y

    def rows(ref, ch):
        return ref.at[pl.ds(pl.multiple_of(ch * r, SUBLANE), r), :]

    def copy(k, sems, arriving):
        peer, _ = _peer(x, y, c, k)
        peer_chip = 2 * peer[0] + peer[1]
        return pltpu.make_async_remote_copy(
            src_ref=rows(part_ref, peer_chip), dst_ref=rows(recv_ref, peer_chip if arriving else my_chip),
            send_sem=sems[0].at[0], recv_sem=sems[1].at[0], device_id=peer, device_id_type=MESH)

    return copy


def _scatter2_start(g, name, after, carry=()):
    r = g.shape[0]
    x, y, _, _ = _my_place()
    from_sibling = _scatter2_pair(g, name + "_pair")
    partial = _scatter2_add(g, from_sibling, name + "_add")
    cd = partial.shape[1]
    recv = _place_own(partial, "row", (r, cd), "row", jax.ShapeDtypeStruct(partial.shape, partial.dtype),
                      "place_own", index=2 * x + y)

    def make_body(land_refs, in_sems, out_sems):
        copy = _scatter2_copies(land_refs[:2], r)
        for k in FORWARD_KS:
            copy(k, out_sems, False).start()

    sems, lands, token = _gather2_call([partial, recv] + list(carry), (), name, after, make_body, True)
    return (sems, lands[:2], r), token, lands[2:]


def _scatter2_wait(handle, name, after):
    sems, lands, r = handle

    def make_body(land_refs, in_sems, out_sems):
        copy = _scatter2_copies(land_refs, r)
        for k in FORWARD_KS:
            arrival = copy(k, in_sems, True)
            arrival.wait_send()
            arrival.wait_recv()

    _, lands, _ = _gather2_call(lands, sems, name, after, make_body, False)
    return lands[1]


def _pad_mla_w_in(w):
    d = w.shape[0]
    z = lambda n: jnp.zeros((d, n), w.dtype)
    o1, o2, o3 = Q_RANK, Q_RANK + KV_RANK, Q_RANK + KV_RANK + ROPE
    return jnp.concatenate([w[:, :o1], z(Q_RANK_PAD - Q_RANK), w[:, o1:o2], w[:, o2:o3], z(LANE - ROPE), w[:, o3:]], axis=1)


def _unpad_mla_w_in(g):
    return jnp.concatenate([g[:, :Q_RANK], g[:, PROJ_CKV:PROJ_KR], g[:, PROJ_KR:PROJ_KR + ROPE], g[:, PROJ_Z:]], axis=1)


def _pad_w_uq(w):
    w3 = w.reshape(Q_RANK, HEADS, NOPE + ROPE)
    w3 = jnp.pad(w3, ((0, Q_RANK_PAD - Q_RANK), (0, 0), (0, HEAD_PAD - NOPE - ROPE)))
    return w3.reshape(Q_RANK_PAD, HEADS * HEAD_PAD)


def _unpad_w_uq(g):
    return g[:Q_RANK].reshape(Q_RANK, HEADS, HEAD_PAD)[:, :, :NOPE + ROPE].reshape(Q_RANK, HEADS * (NOPE + ROPE))


def _pack(pieces):
    flat = [p.reshape(-1).astype(F32) for p in pieces]
    tot = sum(f.shape[0] for f in flat)
    unit = SUBLANE * LANE
    padn = (-tot) % unit
    if padn:
        flat.append(jnp.zeros((padn,), F32))
    return jnp.concatenate(flat).reshape(-1, LANE)


def _unpack(packed, shapes, lead=()):
    flat = packed.reshape(tuple(lead) + (-1,))
    out, off = [], 0
    for shp in shapes:
        nel = int(np.prod(shp))
        out.append(flat[..., off:off + nel].reshape(tuple(lead) + tuple(shp)))
        off += nel
    return out


EARLY_GROUP = [('ada_b', lambda t: t[1:]), ('ada_b', lambda t: t[0:1, 2 * t.shape[1] // 3:]),
               ('pre_g', lambda t: t[1:]), ('post_g', lambda t: t), ('sgu_norm_g', lambda t: t),
               ('sgu_w_s', lambda t: t), ('sgu_b_s', lambda t: t), ('mla_q_norm_g', lambda t: t),
               ('mla_kv_norm_g', lambda t: t)]
LATE_GROUP = [('ada_b', lambda t: t[0:1, :2 * t.shape[1] // 3]), ('pre_g', lambda t: t[0:1])]


WEIGHTS = ['ada_w', 'ada_b', 'pre_g', 'post_g', 'sgu_w_in', 'sgu_norm_g', 'sgu_w_s', 'sgu_b_s', 'sgu_w_out',
           'mla_w_in', 'mla_q_norm_g', 'mla_kv_norm_g', 'mla_w_uq', 'mla_w_ukv', 'mla_w_out']
INPUTS = ['x', 'c'] + WEIGHTS + ['loss_target'] + ['m_' + n for n in WEIGHTS] + ['v_' + n for n in WEIGHTS]


def kernel(x, c, ada_w, ada_b, pre_g, post_g, sgu_w_in, sgu_norm_g, sgu_w_s, sgu_b_s, sgu_w_out, mla_w_in, mla_q_norm_g, mla_kv_norm_g, mla_w_uq, mla_w_ukv, mla_w_out, loss_target, m_ada_w, m_ada_b, m_pre_g, m_post_g, m_sgu_w_in, m_sgu_norm_g, m_sgu_w_s, m_sgu_b_s, m_sgu_w_out, m_mla_w_in, m_mla_q_norm_g, m_mla_kv_norm_g, m_mla_w_uq, m_mla_w_ukv, m_mla_w_out, v_ada_w, v_ada_b, v_pre_g, v_post_g, v_sgu_w_in, v_sgu_norm_g, v_sgu_w_s, v_sgu_b_s, v_sgu_w_out, v_mla_w_in, v_mla_q_norm_g, v_mla_kv_norm_g, v_mla_w_uq, v_mla_w_ukv, v_mla_w_out):
    given = locals()
    A = {name: given[name] for name in INPUTS}
    x0 = A['x'][0]
    tgt = A['loss_target'][0]
    s, d = x0.shape
    e = 2 * d
    ncol = 3 * d // N_DEV
    _, _, _, me = _my_place()
    ktabs, qtabs = _rope_tables(s)

    gains = jnp.zeros((SUBLANE, LANE), F32)
    gains = gains.at[0:2, :Q_RANK // N_DEV].set(A['mla_q_norm_g'])
    gains = gains.at[2:4, :KV_RANK // N_DEV].set(A['mla_kv_norm_g'])
    c8 = jnp.broadcast_to(A['c'], (SUBLANE, d))
    cg, gg = _exchange([c8, gains], ["row", "row"], True, "ag_cond")
    cond_raw = cg.reshape(N_DEV, SUBLANE, d)[:, 0, :]
    gg = gg.reshape(N_DEV, SUBLANE, LANE)
    gq_full = jnp.transpose(gg[:, 0:2, :Q_RANK // N_DEV], (1, 0, 2)).reshape(N_MIX, Q_RANK)
    gkv_full = jnp.transpose(gg[:, 2:4, :KV_RANK // N_DEV], (1, 0, 2)).reshape(N_MIX, KV_RANK)
    gq_pad = jnp.pad(gq_full, ((0, 0), (0, Q_RANK_PAD - Q_RANK)))

    bias_my = lax.dynamic_slice_in_dim(A['ada_b'], me * ncol, ncol, axis=1)
    mod_part = _ada_mod(cond_raw, A['ada_w'], bias_my, "ada_mod")
    send = jnp.pad(jnp.transpose(mod_part, (1, 0, 2)), ((0, 0), (0, SUBLANE - DEPTH), (0, 0)))
    (rb,) = _exchange([send.reshape(N_DEV * SUBLANE, ncol)], ["row"], False, "a2a_mod")

    token = rb
    gathers = {}
    for i in range(DEPTH):
        j = i // N_MIX
        if i % N_MIX == 0:
            parts = [("in", [A['sgu_w_in'][j]], ["col"]), ("out", [A['sgu_w_out'][j]], ["row"])]
        else:
            parts = [("all", [A['mla_w_in'][j], A['mla_w_uq'][j], A['mla_w_ukv'][j], A['mla_w_out'][j]],
                      ["col", "col", "col", "row"])]
        for part, shards, kinds in parts:
            gathers[(i, part)], token = _gather2_start([t.astype(BF16) for t in shards], kinds,
                                                       f"ag_start_{i}_{part}", token)

    def forward_gathers(i, after):
        for key in [k for k in gathers if k[0] == i]:
            gathers[key], _ = _gather2_forward(gathers[key], f"ag_forward_{key[0]}_{key[1]}", after)

    gathers[(0, "in")], token = _gather2_forward(gathers[(0, "in")], "ag_forward_0_in", token)
    mod = jnp.transpose(rb.reshape(N_DEV, SUBLANE, ncol)[:, :DEPTH, :], (1, 0, 2)).reshape(DEPTH, 3 * d) + token[0, 0]
    shift = [mod[i:i + 1, :d] for i in range(DEPTH)]
    scale = [mod[i:i + 1, d:2 * d] for i in range(DEPTH)]
    gate = [mod[i:i + 1, 2 * d:] for i in range(DEPTH)]

    saved = []
    x = x0
    for i in range(DEPTH):
        j = i // N_MIX
        h = _pre_fwd(x, A['pre_g'][i:i + 1], scale[i], shift[i], f"pre_fwd")
        if i % N_MIX == 0:
            (w_in,) = _gather2_wait(gathers[(i, "in")], f"ag_wait_{i}_in", h)
            uvz = _mm(h, w_in, "nn", F32, "sgu_in")
            if i == 0:
                gathers[(0, "out")], _ = _gather2_forward(gathers[(0, "out")], "ag_forward_0_out", uvz)
            bias_full = jnp.repeat(A['sgu_b_s'][j].T, e // SGU_GROUPS, axis=1)
            ng = A['sgu_norm_g'][j:j + 1]
            y = _sgu_mid_fwd(uvz, ng, A['sgu_w_s'][j], bias_full, "sgu_mid_fwd")
            forward_gathers(i + 1, y)
            (w_out,) = _gather2_wait(gathers[(i, "out")], f"ag_wait_{i}_out", y)
            out = _mm(y, w_out, "nn", F32, "sgu_out")
            saved.append(dict(x=x, h=h, uvz=uvz, y=y, out=out, w_in=w_in, w_out=w_out, bias=bias_full, ng=ng))
        else:
            w_in, w_uq, w_ukv, w_out = _gather2_wait(gathers[(i, "all")], f"ag_wait_{i}_all", h)
            w_in = _pad_mla_w_in(w_in)
            w_uq = _pad_w_uq(w_uq)
            gq, gkv = gq_pad[j:j + 1], gkv_full[j:j + 1]
            proj = _mm(h, w_in, "nn", F32, "mla_in")
            qn, kvn, kr = _mla_norm_fwd(proj, gq, gkv, ktabs, "mla_norm_fwd")
            q = _mm(qn, w_uq, "nn", F32, "mla_uq")
            q_cat = _rope_heads(q, qtabs, 1.0, SOFTMAX_SCALE * LOG2_E, BF16, "rope_q_fwd")
            kv = _mm(kvn, w_ukv, "nn", BF16, "mla_ukv")
            o, y, lse = _attn_fwd(q_cat, kv, kr, proj, "attn_fwd")
            forward_gathers(i + 1, y)
            out = _mm(y, w_out, "nn", F32, "mla_out")
            saved.append(dict(x=x, h=h, proj=proj, qn=qn, kvn=kvn, kr=kr, q_cat=q_cat, kv=kv, o=o, y=y, lse=lse,
                              out=out, w_in=w_in, w_uq=w_uq, w_ukv=w_ukv, w_out=w_out, gq=gq, gkv=gkv))
        x = _post_fwd(x, out, gate[i], A['post_g'][i:i + 1], "post_fwd")

    dx, loss_row = _loss_head(x, tgt, "loss_head")
    loss = lax.psum(loss_row[0, 0], ("x", "y", "c"))

    d_shift, d_scale, d_gate = [None] * DEPTH, [None] * DEPTH, [None] * DEPTH
    d_pre, d_post = [None] * DEPTH, [None] * DEPTH
    scatters, scatters_out = [None] * DEPTH, [None] * DEPTH
    small = {}
    for i in reversed(range(DEPTH)):
        j = i // N_MIX
        sv = saved[i]
        dout, d_gate[i], d_post[i] = _post_bwd(dx, sv['out'], gate[i], A['post_g'][i:i + 1], "post_bwd")
        if i % N_MIX == 0:
            dy = _mm(dout, sv['w_out'], "nt", F32, "sgu_out_dx")
            g_w_out = _mm(sv['y'], dout, "tn", BF16, "sgu_out_dw")
            scatters_out[i], token, (dy,) = _exchange_start([g_w_out], ["row"], False, f"rs_out_start_{i}", token,
                                                            carry=[dy])
            duvz, dws, dbs, dng = _sgu_mid_bwd(sv['uvz'], dy, sv['ng'], A['sgu_w_s'][j], sv['bias'], "sgu_mid_bwd")
            small[('sgu', j)] = (dws, dbs.reshape(SGU_GROUPS, SGU_BLOCK), dng)
            if i == 0:
                early_grads = [
                    jnp.concatenate([jnp.concatenate([d_shift[l], d_scale[l], d_gate[l]], axis=1)
                                     for l in range(1, DEPTH)], axis=0),
                    d_gate[0], jnp.concatenate(d_pre[1:], axis=0), jnp.concatenate(d_post, axis=0),
                    jnp.concatenate([small[('sgu', jj)][2] for jj in range(N_MIX)], axis=0),
                    jnp.stack([small[('sgu', jj)][0] for jj in range(N_MIX)]),
                    jnp.stack([small[('sgu', jj)][1] for jj in range(N_MIX)]),
                    jnp.concatenate([small[('mla', jj)][0] for jj in range(N_MIX)], axis=0),
                    jnp.concatenate([small[('mla', jj)][1] for jj in range(N_MIX)], axis=0)]
                early_small, token, _ = _exchange_start([_pack(early_grads)], ["row"], True, "ag_small_early_start",
                                                        token)
            g_w_in = _mm(sv['h'], duvz, "tn", BF16, "sgu_in_dw")
            if i == 0:
                scatters[i], token, (duvz,) = _scatter2_start(g_w_in, "rs2_start_0", token, carry=[duvz])
            else:
                scatters[i], token, (duvz,) = _exchange_start([g_w_in], ["col"], False, f"rs_start_{i}", token,
                                                              carry=[duvz])
            dh = _mm(duvz, sv['w_in'], "nt", F32, "sgu_in_dx")
        else:
            dy = _mm(dout, sv['w_out'], "nt", F32, "mla_out_dx")
            g_w_out = _mm(sv['y'], dout, "tn", BF16, "mla_out_dw")
            scatters_out[i], token, (dy,) = _exchange_start([g_w_out], ["row"], False, f"rs_out_start_{i}", token,
                                                            carry=[dy])
            do, dz = _mla_gate_bwd(dy, sv['o'], sv['proj'], "mla_gate_bwd")
            dq, dkv, dkr_heads = _attn_bwd(sv['q_cat'], sv['kv'], sv['kr'], do, sv['o'], sv['lse'], "attn_bwd")
            dq_b = _rope_heads(dq, qtabs, -1.0, SOFTMAX_SCALE, BF16, "rope_q_bwd")
            dqn = _mm(dq_b, sv['w_uq'], "nt", F32, "mla_uq_dx")
            g_w_uq = _unpad_w_uq(_mm(sv['qn'], dq_b, "tn", BF16, "mla_uq_dw"))
            dkvn = _mm(dkv, sv['w_ukv'], "nt", F32, "mla_ukv_dx")
            g_w_ukv = _mm(sv['kvn'], dkv, "tn", BF16, "mla_ukv_dw")
            dpa, dgq, dgkv = _mla_norm_bwd(dqn, dkvn, dkr_heads, sv['proj'], sv['gq'], sv['gkv'], ktabs, "mla_norm_bwd")
            dproj = jnp.concatenate([dpa, dz], axis=1)
            g_w_in = _unpad_mla_w_in(_mm(sv['h'], dproj, "tn", BF16, "mla_in_dw"))
            scatters[i], token, (dproj,) = _exchange_start(
                [g_w_in, g_w_uq, g_w_ukv], ["col", "col", "col"], False, f"rs_start_{i}", token, carry=[dproj])
            dh = _mm(dproj, sv['w_in'], "nt", F32, "mla_in_dx")
            small[('mla', j)] = (dgq[:, :Q_RANK], dgkv)
        dx, d_shift[i], d_scale[i], d_pre[i] = _pre_bwd(dh, sv['x'], dx, A['pre_g'][i:i + 1], scale[i], "pre_bwd")

    res = {}

    def big(name, recv0, recv1):
        res[name] = _adam_reduce(recv0, recv1, A[name], A['m_' + name], A['v_' + name], "adam_" + name)

    def small_full(name, prefix):
        t = A[prefix + name]
        if name in ('mla_q_norm_g', 'mla_kv_norm_g'):
            t = lax.dynamic_update_slice_in_dim(jnp.zeros((t.shape[0], t.shape[1] * N_DEV), F32), t,
                                                me * t.shape[1], axis=1)
        return t

    def finish(i, after):
        first = ([_scatter2_wait(scatters[i], "rs2_wait_0", after)] if i == 0
                 else _exchange_wait(scatters[i], f"rs_wait_{i}", after))
        return first + _exchange_wait(scatters_out[i], f"rs_out_wait_{i}", after)

    recv_mla = {j: finish(N_MIX * j + 1, dx) for j in reversed(range(N_MIX))}
    for idx_w, name in enumerate(['mla_w_in', 'mla_w_uq', 'mla_w_ukv', 'mla_w_out']):
        big(name, recv_mla[0][idx_w], recv_mla[1][idx_w])

    recv_sgu = {j: finish(N_MIX * j, res['mla_w_out'][0]) for j in reversed(range(N_MIX))}
    for idx_w, name in enumerate(['sgu_w_in', 'sgu_w_out']):
        big(name, recv_sgu[0][idx_w], recv_sgu[1][idx_w])

    late_grads = [jnp.concatenate([d_shift[0], d_scale[0]], axis=1), d_pre[0]]
    (gathered_early,) = _exchange_wait(early_small, "ag_small_early_wait", res['sgu_w_in'][0])
    (gathered_late,) = _exchange([_pack(late_grads)], ["row"], True, "ag_small_late", after=res['sgu_w_in'][0])

    def small_update(group, grads, gathered, tag):
        shapes = [g.shape for g in grads]
        packs = [_pack([pick(small_full(name, prefix)) for name, pick in group]) for prefix in ('', 'm_', 'v_')]
        outs = _adam_small(gathered.reshape(N_DEV, -1, LANE), *packs, "adam_small_" + tag)
        return [_unpack(t, shapes) for t in outs], _unpack(gathered.reshape(N_DEV, -1), shapes, lead=(N_DEV,))

    upd_e, parts_e = small_update(EARLY_GROUP, early_grads, gathered_early, "early")
    upd_l, parts_l = small_update(LATE_GROUP, late_grads, gathered_late, "late")
    for k_out in range(4):
        ab_rest, ab_gate0, pre_rest, post_all, ng_all, ws_all, bs_all, gq_all, gkv_all = upd_e[k_out]
        ab_0, pre_0 = upd_l[k_out]
        res.setdefault('ada_b', [None] * 4)[k_out] = jnp.concatenate(
            [jnp.concatenate([ab_0, ab_gate0], axis=1), ab_rest], axis=0)
        res.setdefault('pre_g', [None] * 4)[k_out] = jnp.concatenate([pre_0, pre_rest], axis=0)
        res.setdefault('post_g', [None] * 4)[k_out] = post_all
        res.setdefault('sgu_norm_g', [None] * 4)[k_out] = ng_all
        res.setdefault('sgu_w_s', [None] * 4)[k_out] = ws_all
        res.setdefault('sgu_b_s', [None] * 4)[k_out] = bs_all
        for name, val in (('mla_q_norm_g', gq_all), ('mla_kv_norm_g', gkv_all)):
            wdt = A[name].shape[1]
            res.setdefault(name, [None] * 4)[k_out] = lax.dynamic_slice_in_dim(val, me * wdt, wdt, axis=1)

    dmod_all = jnp.concatenate([jnp.concatenate([parts_l[0], parts_e[1]], axis=2), parts_e[0]], axis=1)
    dmod_cols = jnp.transpose(lax.dynamic_slice_in_dim(dmod_all, me * ncol, ncol, axis=2), (1, 0, 2))
    res['ada_w'] = _ada_bwd_adam(jnp.transpose(cond_raw), dmod_cols, A['ada_w'], A['m_ada_w'], A['v_ada_w'], "ada_bwd")

    outs = [loss, dx[None]]
    for k_out in range(4):
        outs += [res[n][k_out] for n in WEIGHTS]
    return tuple(outs)
```

```python
import functools
import math

import numpy as np
import jax
import jax.numpy as jnp
from jax import lax
from jax.experimental import pallas as pl
from jax.experimental.pallas import tpu as pltpu

F32 = jnp.float32
BF16 = jnp.bfloat16
MESH = pl.DeviceIdType.MESH

N_DEV = 8
DEPTH = 4
N_MIX = 2
NORM_EPS = 1e-6
CHUNK = 64
SGU_BLOCK = 128
SGU_GROUPS = 16
HEADS = 16
Q_RANK = 448
Q_RANK_PAD = 512
KV_RANK = 512
NOPE = 128
ROPE = 64
HALF = ROPE // 2
V_DIM = 128
HEAD_PAD = 256
ROPE_THETA = 10000.0
MLA_WIDTH = HEADS * V_DIM
LANE = 128
SUBLANE = 8
PROJ_CQ = 0
PROJ_CKV = Q_RANK_PAD
PROJ_KR = Q_RANK_PAD + KV_RANK
PROJ_Z = PROJ_KR + LANE
PROJ_W = PROJ_Z + MLA_WIDTH

ADAM_LR = 0.001
ADAM_B1 = 0.9
ADAM_B2 = 0.999
ADAM_EPS = 1e-08
ADAM_WD = 0.01
ADAM_STEP = 10

VMEM_LIMIT = 56 * 1024 * 1024
ATT_BLK = 512
ATT_SUB = 128
ROW_BLK = 256
MM_TM, MM_TN, MM_TK = 1024, 1024, 2048
MM_TILE_BYTES = 40 * 1024 * 1024
SOFTMAX_SCALE = (NOPE + ROPE) ** -0.5
LOG2_E = 1.0 / math.log(2.0)
INV_SQRT2 = 1.0 / math.sqrt(2.0)
INV_SQRT_2PI = 1.0 / math.sqrt(2.0 * math.pi)


def _pcall(body, comm=False, **kw):
    return pl.pallas_call(body, **kw)


def _params(sem=None):
    return pltpu.CompilerParams(dimension_semantics=sem, vmem_limit_bytes=VMEM_LIMIT)


def _pick(dim, pref):
    if dim <= pref:
        return dim
    t = (pref // LANE) * LANE
    while t >= LANE:
        if dim % t == 0:
            return t
        t -= LANE
    return dim


def _gelu(x):
    return 0.5 * x * (1.0 + lax.erf(x * INV_SQRT2))


def _gelu_grad(x):
    return 0.5 * (1.0 + lax.erf(x * INV_SQRT2)) + x * jnp.exp(-0.5 * x * x) * INV_SQRT_2PI


def _sigmoid(x):
    return 1.0 / (1.0 + jnp.exp(-x))


def _dot_nt(a, b):
    return lax.dot_general(a, b, (((1,), (1,)), ((), ())), preferred_element_type=F32)


def _dot_tn(a, b):
    return lax.dot_general(a, b, (((0,), (0,)), ((), ())), preferred_element_type=F32)


def _mm(a, b, dims, out_dtype, name):
    if dims == "nn":
        (m, k), (k2, n) = a.shape, b.shape
    elif dims == "nt":
        (m, k), (n, k2) = a.shape, b.shape
    else:
        (k, m), (k2, n) = a.shape, b.shape
    assert k == k2, (a.shape, b.shape, dims)
    tm, tn = _pick(m, MM_TM), _pick(n, MM_TN)
    out_bytes = 2 * tm * tn * jnp.dtype(out_dtype).itemsize
    whole_k = 2 * (tm + tn) * k * a.dtype.itemsize + out_bytes <= MM_TILE_BYTES
    tk = k if whole_k else _pick(k, MM_TK)
    nk = k // tk

    def body(a_ref, b_ref, o_ref, *scratch):
        if dims == "nn":
            p = jnp.dot(a_ref[...], b_ref[...], preferred_element_type=F32)
        elif dims == "nt":
            p = _dot_nt(a_ref[...], b_ref[...])
        else:
            p = _dot_tn(a_ref[...], b_ref[...])
        if nk == 1:
            o_ref[...] = p.astype(o_ref.dtype)
            return
        acc_ref, = scratch
        kk = pl.program_id(2)

        @pl.when(kk == 0)
        def _():
            acc_ref[...] = p

        @pl.when(kk > 0)
        def _():
            acc_ref[...] += p

        @pl.when(kk == nk - 1)
        def _():
            o_ref[...] = acc_ref[...].astype(o_ref.dtype)

    if dims == "tn":
        a_spec = pl.BlockSpec((tk, tm), lambda i, j, kk: (kk, i))
    else:
        a_spec = pl.BlockSpec((tm, tk), lambda i, j, kk: (i, kk))
    if dims == "nt":
        b_spec = pl.BlockSpec((tn, tk), lambda i, j, kk: (j, kk))
    else:
        b_spec = pl.BlockSpec((tk, tn), lambda i, j, kk: (kk, j))
    return _pcall(
        body, name=name,
        grid=(m // tm, n // tn, nk),
        in_specs=[a_spec, b_spec],
        out_specs=pl.BlockSpec((tm, tn), lambda i, j, kk: (i, j)),
        out_shape=jax.ShapeDtypeStruct((m, n), out_dtype),
        scratch_shapes=[pltpu.VMEM((tm, tn), F32)] if nk > 1 else [],
        compiler_params=_params(("parallel", "parallel", "arbitrary")),
    )(a, b)


def _row_spec(ts, d):
    return pl.BlockSpec((ts, d), lambda i: (i, 0))


def _vec_spec(d):
    return pl.BlockSpec((1, d), lambda i: (0, 0))


def _pre_fwd(x, g, scale, shift, name):
    s, d = x.shape
    ts = _pick(s, ROW_BLK)

    def body(x_ref, g_ref, sc_ref, sh_ref, h_ref):
        xv = x_ref[...]
        r = lax.rsqrt(jnp.mean(xv * xv, axis=-1, keepdims=True) + NORM_EPS)
        h_ref[...] = ((xv * r * g_ref[...]) * (1.0 + sc_ref[...]) + sh_ref[...]).astype(BF16)

    return _pcall(
        body, name=name, grid=(s // ts,),
        in_specs=[_row_spec(ts, d), _vec_spec(d), _vec_spec(d), _vec_spec(d)],
        out_specs=_row_spec(ts, d),
        out_shape=jax.ShapeDtypeStruct((s, d), BF16),
        compiler_params=_params(("parallel",)),
    )(x, g, scale, shift)


def _post_fwd(x, out, gate, g, name):
    s, d = x.shape
    ts = _pick(s, ROW_BLK)

    def body(x_ref, o_ref, gate_ref, g_ref, y_ref):
        o = o_ref[...]
        r = lax.rsqrt(jnp.mean(o * o, axis=-1, keepdims=True) + NORM_EPS)
        y_ref[...] = x_ref[...] + gate_ref[...] * (o * r * g_ref[...])

    return _pcall(
        body, name=name, grid=(s // ts,),
        in_specs=[_row_spec(ts, d), _row_spec(ts, d), _vec_spec(d), _vec_spec(d)],
        out_specs=_row_spec(ts, d),
        out_shape=jax.ShapeDtypeStruct((s, d), F32),
        compiler_params=_params(("parallel",)),
    )(x, out, gate, g)


def _loss_head(xf, tgt, name):
    s, d = xf.shape
    ts = _pick(s, ROW_BLK)
    ns = s // ts

    def body(x_ref, t_ref, dx_ref, loss_ref, acc_ref):
        i = pl.program_id(0)

        @pl.when(i == 0)
        def _():
            acc_ref[...] = jnp.zeros_like(acc_ref)

        e = x_ref[...] - t_ref[...]
        dx_ref[...] = e * (1.0 / d)
        acc_ref[...] += jnp.sum(e * e, axis=0, keepdims=True)

        @pl.when(i == ns - 1)
        def _():
            tot = jnp.sum(acc_ref[...], axis=1, keepdims=True) * (0.5 / d)
            loss_ref[...] = jnp.broadcast_to(tot, loss_ref.shape)

    return _pcall(
        body, name=name, grid=(ns,),
        in_specs=[_row_spec(ts, d), _row_spec(ts, d)],
        out_specs=[_row_spec(ts, d), pl.BlockSpec((1, LANE), lambda i: (0, 0))],
        out_shape=[jax.ShapeDtypeStruct((s, d), F32), jax.ShapeDtypeStruct((1, LANE), F32)],
        scratch_shapes=[pltpu.VMEM((1, d), F32)],
        compiler_params=_params(("arbitrary",)),
    )(xf, tgt)


def _post_bwd(dxo, out, gate, g, name):
    s, d = dxo.shape
    ts = _pick(s, ROW_BLK)

    def body(dx_ref, o_ref, gate_ref, g_ref, do_ref, dgate_ref, dg_ref):
        i = pl.program_id(0)

        @pl.when(i == 0)
        def _():
            dgate_ref[...] = jnp.zeros_like(dgate_ref)
            dg_ref[...] = jnp.zeros_like(dg_ref)

        o = o_ref[...]
        dx = dx_ref[...]
        gv = g_ref[...]
        r = lax.rsqrt(jnp.mean(o * o, axis=-1, keepdims=True) + NORM_EPS)
        n = o * r
        dyn = dx * gate_ref[...]
        dgate_ref[...] += jnp.sum(dx * (n * gv), axis=0, keepdims=True)
        dg_ref[...] += jnp.sum(dyn * n, axis=0, keepdims=True)
        dn = dyn * gv
        do_ref[...] = (r * (dn - n * jnp.mean(dn * n, axis=-1, keepdims=True))).astype(BF16)

    return _pcall(
        body, name=name, grid=(s // ts,),
        in_specs=[_row_spec(ts, d), _row_spec(ts, d), _vec_spec(d), _vec_spec(d)],
        out_specs=[_row_spec(ts, d), _vec_spec(d), _vec_spec(d)],
        out_shape=[jax.ShapeDtypeStruct((s, d), BF16), jax.ShapeDtypeStruct((1, d), F32),
                   jax.ShapeDtypeStruct((1, d), F32)],
        compiler_params=_params(("arbitrary",)),
    )(dxo, out, gate, g)


def _pre_bwd(dh, x, dxo, g, scale, name):
    s, d = x.shape
    ts = _pick(s, ROW_BLK)

    def body(dh_ref, x_ref, dxo_ref, g_ref, sc_ref, dx_ref, dsh_ref, dsc_ref, dg_ref):
        i = pl.program_id(0)

        @pl.when(i == 0)
        def _():
            dsh_ref[...] = jnp.zeros_like(dsh_ref)
            dsc_ref[...] = jnp.zeros_like(dsc_ref)
            dg_ref[...] = jnp.zeros_like(dg_ref)

        xv = x_ref[...]
        dhv = dh_ref[...]
        gv = g_ref[...]
        one_sc = 1.0 + sc_ref[...]
        r = lax.rsqrt(jnp.mean(xv * xv, axis=-1, keepdims=True) + NORM_EPS)
        n = xv * r
        dsh_ref[...] += jnp.sum(dhv, axis=0, keepdims=True)
        dsc_ref[...] += jnp.sum(dhv * (n * gv), axis=0, keepdims=True)
        dng = dhv * one_sc
        dg_ref[...] += jnp.sum(dng * n, axis=0, keepdims=True)
        dn = dng * gv
        dx_ref[...] = dxo_ref[...] + r * (dn - n * jnp.mean(dn * n, axis=-1, keepdims=True))

    return _pcall(
        body, name=name, grid=(s // ts,),
        in_specs=[_row_spec(ts, d), _row_spec(ts, d), _row_spec(ts, d), _vec_spec(d), _vec_spec(d)],
        out_specs=[_row_spec(ts, d), _vec_spec(d), _vec_spec(d), _vec_spec(d)],
        out_shape=[jax.ShapeDtypeStruct((s, d), F32)] + [jax.ShapeDtypeStruct((1, d), F32)] * 3,
        compiler_params=_params(("arbitrary",)),
    )(dh, x, dxo, g, scale)


def _sgu_mask():
    t = lax.broadcasted_iota(jnp.int32, (SGU_BLOCK, SGU_BLOCK), 0) // CHUNK
    s = lax.broadcasted_iota(jnp.int32, (SGU_BLOCK, SGU_BLOCK), 1) // CHUNK
    return s <= t


def _sgu_norm(v_pre, g):
    e = v_pre.shape[-1]
    vg = _gelu(v_pre)
    mu = jnp.sum(vg, axis=-1, keepdims=True) * (1.0 / e)
    dlt = vg - mu
    var = jnp.sum(dlt * dlt, axis=-1, keepdims=True) * (1.0 / e)
    rstd = lax.rsqrt(var + NORM_EPS)
    vhat = dlt * rstd
    return vhat, rstd, (vhat * g).astype(BF16)


def _sgu_mid_fwd(uvz, norm_g, w_s, bias_full, name):
    s, e3 = uvz.shape
    e = e3 // 3
    gd = e // SGU_GROUPS
    nb = s // SGU_BLOCK

    def body(uvz_ref, g_ref, w_ref, b_ref, y_ref, wsc):
        @pl.when(pl.program_id(0) == 0)
        def _():
            msk = _sgu_mask()
            for gi in range(SGU_GROUPS):
                wsc[gi] = jnp.where(msk, w_ref[gi], 0.0).astype(BF16)

        _, _, vb = _sgu_norm(uvz_ref[:, e:2 * e], g_ref[...])
        for gi in range(SGU_GROUPS):
            lo = gi * gd
            vm = jnp.dot(wsc[gi], vb[:, lo:lo + gd], preferred_element_type=F32) + b_ref[:, lo:lo + gd]
            zg = uvz_ref[:, 2 * e + lo:2 * e + lo + gd]
            y_ref[:, lo:lo + gd] = (_gelu(uvz_ref[:, lo:lo + gd]) * vm * (zg * _sigmoid(zg))).astype(BF16)

    return _pcall(
        body, name=name, grid=(nb,),
        in_specs=[pl.BlockSpec((SGU_BLOCK, e3), lambda n: (n, 0)),
                  pl.BlockSpec((1, e), lambda n: (0, 0)),
                  pl.BlockSpec((SGU_GROUPS, SGU_BLOCK, SGU_BLOCK), lambda n: (0, 0, 0)),
                  pl.BlockSpec((SGU_BLOCK, e), lambda n: (0, 0))],
        out_specs=pl.BlockSpec((SGU_BLOCK, e), lambda n: (n, 0)),
        out_shape=jax.ShapeDtypeStruct((s, e), BF16),
        scratch_shapes=[pltpu.VMEM((SGU_GROUPS, SGU_BLOCK, SGU_BLOCK), BF16)],
        compiler_params=_params(("arbitrary",)),
    )(uvz, norm_g, w_s, bias_full)


def _sgu_mid_bwd(uvz, dy, norm_g, w_s, bias_full, name):
    s, e3 = uvz.shape
    e = e3 // 3
    gd = e // SGU_GROUPS
    nb = s // SGU_BLOCK

    def body(uvz_ref, dy_ref, g_ref, w_ref, b_ref, d_ref, dw_ref, db_ref, dg_ref, wsc, wtsc, dvh_sc, dbacc):
        n = pl.program_id(0)

        @pl.when(n == 0)
        def _():
            msk = _sgu_mask()
            for gi in range(SGU_GROUPS):
                wm = jnp.where(msk, w_ref[gi], 0.0)
                wsc[gi] = wm.astype(BF16)
                wtsc[gi] = wm.T.astype(BF16)
            dw_ref[...] = jnp.zeros_like(dw_ref)
            dg_ref[...] = jnp.zeros_like(dg_ref)
            dbacc[...] = jnp.zeros_like(dbacc)

        v_pre = uvz_ref[:, e:2 * e]
        gv = g_ref[...]
        vhat, rstd, vb = _sgu_norm(v_pre, gv)
        s1 = jnp.zeros((SGU_BLOCK, 1), F32)
        s2 = jnp.zeros((SGU_BLOCK, 1), F32)
        for gi in range(SGU_GROUPS):
            lo = gi * gd
            u_pre = uvz_ref[:, lo:lo + gd]
            zg = uvz_ref[:, 2 * e + lo:2 * e + lo + gd]
            dyg = dy_ref[:, lo:lo + gd]
            ug = _gelu(u_pre)
            sig = _sigmoid(zg)
            vbg = vb[:, lo:lo + gd]
            vhg = vhat[:, lo:lo + gd]
            vm = jnp.dot(wsc[gi], vbg, preferred_element_type=F32) + b_ref[:, lo:lo + gd]
            t = dyg * (zg * sig)
            d_ref[:, lo:lo + gd] = (t * vm * _gelu_grad(u_pre)).astype(BF16)
            dvm = t * ug
            d_ref[:, 2 * e + lo:2 * e + lo + gd] = (dyg * ug * vm * (sig * (1.0 + zg * (1.0 - sig)))).astype(BF16)
            dvm_b = dvm.astype(BF16)
            dv = jnp.dot(wtsc[gi], dvm_b, preferred_element_type=F32)
            dw_ref[gi] += _dot_nt(dvm_b, vbg)
            dbacc[:, lo:lo + gd] += dvm
            dg_ref[:, lo:lo + gd] += jnp.sum(dv * vhg, axis=0, keepdims=True)
            dvh = dv * gv[:, lo:lo + gd]
            dvh_sc[:, lo:lo + gd] = dvh
            s1 = s1 + jnp.sum(dvh, axis=-1, keepdims=True)
            s2 = s2 + jnp.sum(dvh * vhg, axis=-1, keepdims=True)
        dvg = rstd * (dvh_sc[...] - s1 * (1.0 / e) - vhat * (s2 * (1.0 / e)))
        d_ref[:, e:2 * e] = (dvg * _gelu_grad(v_pre)).astype(BF16)

        @pl.when(n == nb - 1)
        def _():
            msk = _sgu_mask()
            for gi in range(SGU_GROUPS):
                dw_ref[gi] = jnp.where(msk, dw_ref[gi], 0.0)
                db_ref[gi] = jnp.sum(dbacc[:, gi * gd:(gi + 1) * gd], axis=1, keepdims=True)

    return _pcall(
        body, name=name, grid=(nb,),
        in_specs=[pl.BlockSpec((SGU_BLOCK, e3), lambda n: (n, 0)),
                  pl.BlockSpec((SGU_BLOCK, e), lambda n: (n, 0)),
                  pl.BlockSpec((1, e), lambda n: (0, 0)),
                  pl.BlockSpec((SGU_GROUPS, SGU_BLOCK, SGU_BLOCK), lambda n: (0, 0, 0)),
                  pl.BlockSpec((SGU_BLOCK, e), lambda n: (0, 0))],
        out_specs=[pl.BlockSpec((SGU_BLOCK, e3), lambda n: (n, 0)),
                   pl.BlockSpec((SGU_GROUPS, SGU_BLOCK, SGU_BLOCK), lambda n: (0, 0, 0)),
                   pl.BlockSpec((SGU_GROUPS, SGU_BLOCK, 1), lambda n: (0, 0, 0)),
                   pl.BlockSpec((1, e), lambda n: (0, 0))],
        out_shape=[jax.ShapeDtypeStruct((s, e3), BF16),
                   jax.ShapeDtypeStruct((SGU_GROUPS, SGU_BLOCK, SGU_BLOCK), F32),
                   jax.ShapeDtypeStruct((SGU_GROUPS, SGU_BLOCK, 1), F32),
                   jax.ShapeDtypeStruct((1, e), F32)],
        scratch_shapes=[pltpu.VMEM((SGU_GROUPS, SGU_BLOCK, SGU_BLOCK), BF16),
                        pltpu.VMEM((SGU_GROUPS, SGU_BLOCK, SGU_BLOCK), BF16),
                        pltpu.VMEM((SGU_BLOCK, e), F32),
                        pltpu.VMEM((SGU_BLOCK, e), F32)],
        compiler_params=_params(("arbitrary",)),
    )(uvz, dy, norm_g, w_s, bias_full)


def _rope_tables(s):
    pos = jnp.arange(s, dtype=F32)
    inv_freq = ROPE_THETA ** (-jnp.arange(0, ROPE, 2, dtype=F32) / ROPE)
    ang = pos[:, None] * inv_freq[None, :]
    cos, sin = jnp.cos(ang), jnp.sin(ang)
    z32 = jnp.zeros((s, HALF), F32)
    z64 = jnp.zeros((s, ROPE), F32)
    ck = jnp.concatenate([cos, cos, z64], axis=1)
    s1k = jnp.concatenate([-sin, z32, z64], axis=1)
    s2k = jnp.concatenate([z32, sin, z64], axis=1)
    one = jnp.ones((s, NOPE), F32)
    zn = jnp.zeros((s, NOPE), F32)
    return (ck, s1k, s2k), (jnp.concatenate([one, ck], axis=1), jnp.concatenate([zn, s1k], axis=1),
                            jnp.concatenate([zn, s2k], axis=1))


def _rot(x, c, s1, s2):
    w = x.shape[-1]
    return x * c + pltpu.roll(x, w - HALF, 1) * s1 + pltpu.roll(x, HALF, 1) * s2


def _rms(cv, n_real):
    r = lax.rsqrt(jnp.sum(cv * cv, axis=-1, keepdims=True) * (1.0 / n_real) + NORM_EPS)
    return r, cv * r


def _mla_norm_fwd(proj, gq, gkv, tabs, name):
    s = proj.shape[0]
    ts = _pick(s, ROW_BLK)
    ck, s1k, s2k = tabs

    def body(cq_ref, ckv_ref, kr_ref, gq_ref, gkv_ref, c_ref, s1_ref, s2_ref, qn_ref, kvn_ref, kro_ref):
        _, nq = _rms(cq_ref[...], Q_RANK)
        qn_ref[...] = (nq * gq_ref[...]).astype(BF16)
        _, nkv = _rms(ckv_ref[...], KV_RANK)
        kvn_ref[...] = (nkv * gkv_ref[...]).astype(BF16)
        kro_ref[...] = _rot(kr_ref[...], c_ref[...], s1_ref[...], s2_ref[...]).astype(BF16)

    tab = pl.BlockSpec((ts, LANE), lambda i: (i, 0))
    return _pcall(
        body, name=name, grid=(s // ts,),
        in_specs=[pl.BlockSpec((ts, Q_RANK_PAD), lambda i: (i, 0)),
                  pl.BlockSpec((ts, KV_RANK), lambda i: (i, PROJ_CKV // KV_RANK)),
                  pl.BlockSpec((ts, LANE), lambda i: (i, PROJ_KR // LANE)),
                  _vec_spec(Q_RANK_PAD), _vec_spec(KV_RANK), tab, tab, tab],
        out_specs=[pl.BlockSpec((ts, Q_RANK_PAD), lambda i: (i, 0)),
                   pl.BlockSpec((ts, KV_RANK), lambda i: (i, 0)), tab],
        out_shape=[jax.ShapeDtypeStruct((s, Q_RANK_PAD), BF16), jax.ShapeDtypeStruct((s, KV_RANK), BF16),
                   jax.ShapeDtypeStruct((s, LANE), BF16)],
        compiler_params=_params(("parallel",)),
    )(proj, proj, proj, gq, gkv, ck, s1k, s2k)


def _rope_heads(q, tabs, sign, mult, out_dtype, name):
    s, w = q.shape
    ts = _pick(s, ROW_BLK)
    c, s1, s2 = tabs

    def body(q_ref, c_ref, s1_ref, s2_ref, o_ref):
        cv, s1v, s2v = mult * c_ref[...], (sign * mult) * s1_ref[...], (sign * mult) * s2_ref[...]
        for h in range(w // HEAD_PAD):
            lo = h * HEAD_PAD
            o_ref[:, lo:lo + HEAD_PAD] = _rot(q_ref[:, lo:lo + HEAD_PAD].astype(F32), cv, s1v, s2v).astype(out_dtype)

    blk = pl.BlockSpec((ts, w), lambda i: (i, 0))
    tab = pl.BlockSpec((ts, HEAD_PAD), lambda i: (i, 0))
    return _pcall(
        body, name=name, grid=(s // ts,),
        in_specs=[blk, tab, tab, tab], out_specs=blk,
        out_shape=jax.ShapeDtypeStruct((s, w), out_dtype),
        compiler_params=_params(("parallel",)),
    )(q, c, s1, s2)


def _transpose_bf16(t):
    return t.astype(F32).T.astype(BF16)


def _diag_mask(tb, transposed):
    r = lax.broadcasted_iota(jnp.int32, (tb, tb), 0) // CHUNK
    c = lax.broadcasted_iota(jnp.int32, (tb, tb), 1) // CHUNK
    return (r <= c) if transposed else (c <= r)


def _attn_fwd(q_cat, kv, kr, proj, name):
    s = q_cat.shape[0]
    tb = _pick(s, ATT_BLK)
    nb = s // tb
    zcol = PROJ_Z // V_DIM

    def body(q_ref, kn_ref, v_ref, kr_ref, z_ref, o_ref, y_ref, lse_ref, kt_sc, vx_sc, m_sc, acc_sc, sa_sc, sb_sc):
        qi = pl.program_id(1)

        @pl.when(qi == 0)
        def _():
            for b in range(nb):
                rows = slice(b * tb, (b + 1) * tb)
                kt_sc[b] = _transpose_bf16(jnp.concatenate([kn_ref[rows, :], kr_ref[rows, :]], axis=1))
                vx_sc[b] = jnp.concatenate([v_ref[rows, :], jnp.ones((tb, V_DIM), BF16)], axis=1)

        m_sc[...] = jnp.full_like(m_sc, -1e30)
        acc_sc[...] = jnp.zeros_like(acc_sc)
        sub = min(tb, ATT_SUB)

        def scores(ki, s_ref):
            s_ref[...] = jnp.dot(q_ref[...], kt_sc[ki], preferred_element_type=F32)

        def step(ki, s_ref, masked):
            for r in range(tb // sub):
                rs = slice(r * sub, (r + 1) * sub)
                sc = s_ref[rs, :]
                if masked:
                    sc = jnp.where(_diag_mask(tb, False)[rs, :], sc, -1e30)
                m_prev = m_sc[rs, :]
                m_new = jnp.maximum(m_prev, jnp.max(sc, axis=-1, keepdims=True))
                p = jnp.exp2(sc - m_new).astype(BF16)
                acc_sc[rs, :] = (jnp.exp2(m_prev - m_new) * acc_sc[rs, :]
                                 + jnp.dot(p, vx_sc[ki], preferred_element_type=F32))
                m_sc[rs, :] = m_new

        def pair(t, carry):
            scores(2 * t + 1, sb_sc)
            step(2 * t, sa_sc, False)
            scores(2 * t + 2, sa_sc)
            step(2 * t + 1, sb_sc, False)
            return carry

        scores(0, sa_sc)
        lax.fori_loop(0, qi // 2, pair, 0)

        @pl.when(qi % 2 == 1)
        def _():
            scores(qi, sb_sc)
            step(qi - 1, sa_sc, False)
            step(qi, sb_sc, True)

        @pl.when(qi % 2 == 0)
        def _():
            step(qi, sa_sc, True)

        l = acc_sc[:, V_DIM:V_DIM + 1]
        o = acc_sc[:, :V_DIM] / l
        z = z_ref[...]
        o_ref[...] = o.astype(BF16)
        y_ref[...] = (o * (z * _sigmoid(z))).astype(BF16)
        lse_cols = jnp.broadcast_to(m_sc[...] + jnp.log2(l), (tb, LANE))
        lse_ref[...] = lse_cols.T[0:1, :]

    oblk = pl.BlockSpec((tb, V_DIM), lambda h, qi: (qi, h))
    return _pcall(
        body, name=name, grid=(HEADS, nb),
        in_specs=[pl.BlockSpec((tb, HEAD_PAD), lambda h, qi: (qi, h)),
                  pl.BlockSpec((s, NOPE), lambda h, qi: (0, 2 * h)),
                  pl.BlockSpec((s, V_DIM), lambda h, qi: (0, 2 * h + 1)),
                  pl.BlockSpec((s, LANE), lambda h, qi: (0, 0)),
                  pl.BlockSpec((tb, V_DIM), lambda h, qi: (qi, zcol + h))],
        out_specs=[oblk, oblk, pl.BlockSpec((None, None, 1, tb), lambda h, qi: (h, qi, 0, 0))],
        out_shape=[jax.ShapeDtypeStruct((s, MLA_WIDTH), BF16), jax.ShapeDtypeStruct((s, MLA_WIDTH), BF16),
                   jax.ShapeDtypeStruct((HEADS, nb, 1, tb), F32)],
        scratch_shapes=[pltpu.VMEM((nb, HEAD_PAD, tb), BF16), pltpu.VMEM((nb, tb, HEAD_PAD), BF16),
                        pltpu.VMEM((tb, 1), F32), pltpu.VMEM((tb, HEAD_PAD), F32),
                        pltpu.VMEM((tb, tb), F32), pltpu.VMEM((tb, tb), F32)],
        compiler_params=_params(("parallel", "arbitrary")),
    )(q_cat, kv, kv, kr, proj)


def _attn_bwd(q_cat, kv, kr, do, o, lse, name):
    s = q_cat.shape[0]
    tb = _pick(s, ATT_BLK)
    nb = s // tb
    ln2 = math.log(2.0)

    def body(q_ref, do_ref, o_ref, lse_ref, kn_ref, v_ref, kr_ref, dq_ref, dkv_ref, dkr_ref,
             qt_sc, dot_sc, delta_sc, dqt_sc, dk_sc, dv_sc):
        ki = pl.program_id(1)

        @pl.when(ki == 0)
        def _():
            for b in range(nb):
                rows = slice(b * tb, (b + 1) * tb)
                qt_sc[b] = _transpose_bf16(q_ref[rows, :])
                do_t = do_ref[rows, :].astype(F32).T
                dot_sc[b] = do_t.astype(BF16)
                delta_sc[b] = jnp.sum(do_t * o_ref[rows, :].astype(F32).T, axis=0, keepdims=True)
            dqt_sc[...] = jnp.zeros_like(dqt_sc)

        k = jnp.concatenate([kn_ref[...], kr_ref[...]], axis=1)
        kt = _transpose_bf16(k)
        vb = v_ref[...]
        dk_sc[...] = jnp.zeros_like(dk_sc)
        dv_sc[...] = jnp.zeros_like(dv_sc)

        def step(qi, masked):
            rows = pl.ds(pl.multiple_of(qi * tb, tb), tb)
            sc_t = jnp.dot(k, qt_sc[qi], preferred_element_type=F32)
            if masked:
                sc_t = jnp.where(_diag_mask(tb, True), sc_t, -1e30)
            p_t = jnp.exp2(sc_t - lse_ref[qi])
            dp_t = jnp.dot(vb, dot_sc[qi], preferred_element_type=F32)
            ds_t = (p_t * (dp_t - delta_sc[qi])).astype(BF16)
            dv_sc[...] += jnp.dot(p_t.astype(BF16), do_ref[rows, :], preferred_element_type=F32)
            dk_sc[...] += jnp.dot(ds_t, q_ref[rows, :], preferred_element_type=F32)
            dqt_sc[qi] += jnp.dot(kt, ds_t, preferred_element_type=F32)

        step(ki, True)

        def loop_body(qi, carry):
            step(qi, False)
            return carry

        lax.fori_loop(ki + 1, nb, loop_body, 0)

        dkv_ref[:, :NOPE] = (dk_sc[:, :NOPE] * ln2).astype(BF16)
        dkv_ref[:, NOPE:] = dv_sc[...].astype(BF16)
        dkr_ref[...] = dk_sc[:, NOPE:] * ln2

        @pl.when(ki == nb - 1)
        def _():
            for b in range(nb):
                dq_ref[b * tb:(b + 1) * tb, :] = dqt_sc[b].T

    return _pcall(
        body, name=name, grid=(HEADS, nb),
        in_specs=[pl.BlockSpec((s, HEAD_PAD), lambda h, ki: (0, h)),
                  pl.BlockSpec((s, V_DIM), lambda h, ki: (0, h)),
                  pl.BlockSpec((s, V_DIM), lambda h, ki: (0, h)),
                  pl.BlockSpec((None, nb, 1, tb), lambda h, ki: (h, 0, 0, 0)),
                  pl.BlockSpec((tb, NOPE), lambda h, ki: (ki, 2 * h)),
                  pl.BlockSpec((tb, V_DIM), lambda h, ki: (ki, 2 * h + 1)),
                  pl.BlockSpec((tb, LANE), lambda h, ki: (ki, 0))],
        out_specs=[pl.BlockSpec((s, HEAD_PAD), lambda h, ki: (0, h)),
                   pl.BlockSpec((tb, HEAD_PAD), lambda h, ki: (ki, h)),
                   pl.BlockSpec((None, tb, LANE), lambda h, ki: (h, ki, 0))],
        out_shape=[jax.ShapeDtypeStruct((s, HEADS * HEAD_PAD), F32),
                   jax.ShapeDtypeStruct((s, HEADS * HEAD_PAD), BF16),
                   jax.ShapeDtypeStruct((HEADS, s, LANE), F32)],
        scratch_shapes=[pltpu.VMEM((nb, HEAD_PAD, tb), BF16), pltpu.VMEM((nb, V_DIM, tb), BF16),
                        pltpu.VMEM((nb, 1, tb), F32), pltpu.VMEM((nb, HEAD_PAD, tb), F32),
                        pltpu.VMEM((tb, HEAD_PAD), F32), pltpu.VMEM((tb, V_DIM), F32)],
        compiler_params=_params(("parallel", "arbitrary")),
    )(q_cat, do, o, lse, kv, kv, kr)


def _mla_gate_bwd(dy, o, proj, name):
    s = dy.shape[0]
    ts = _pick(s, ROW_BLK)
    zcol = PROJ_Z // V_DIM

    def body(dy_ref, o_ref, p_ref, do_ref, dz_ref):
        z = p_ref[:, PROJ_Z:]
        dyv = dy_ref[...]
        sig = _sigmoid(z)
        do_ref[...] = (dyv * (z * sig)).astype(BF16)
        dz_ref[...] = (dyv * o_ref[...].astype(F32) * (sig * (1.0 + z * (1.0 - sig)))).astype(BF16)

    blk = pl.BlockSpec((ts, MLA_WIDTH), lambda i: (i, 0))
    return _pcall(
        body, name=name, grid=(s // ts,),
        in_specs=[blk, blk, pl.BlockSpec((ts, PROJ_W), lambda i: (i, 0))],
        out_specs=[blk, blk],
        out_shape=[jax.ShapeDtypeStruct((s, MLA_WIDTH), BF16), jax.ShapeDtypeStruct((s, MLA_WIDTH), BF16)],
        compiler_params=_params(("parallel",)),
    )(dy, o, proj)


def _mla_norm_bwd(dqn, dkvn, dkr_heads, proj, gq, gkv, tabs, name):
    s = proj.shape[0]
    ts = _pick(s, ROW_BLK)
    ck, s1k, s2k = tabs

    def rms_bwd(cv, dn_in, g, n_real):
        r, n = _rms(cv, n_real)
        dg = jnp.sum(dn_in * n, axis=0, keepdims=True)
        dn = dn_in * g
        dc = r * (dn - n * (jnp.sum(dn * n, axis=-1, keepdims=True) * (1.0 / n_real)))
        return dc, dg

    def body(dqn_ref, dkvn_ref, dkr_ref, cq_ref, ckv_ref, gq_ref, gkv_ref, c_ref, s1_ref, s2_ref,
             dp_ref, dgq_ref, dgkv_ref):
        @pl.when(pl.program_id(0) == 0)
        def _():
            dgq_ref[...] = jnp.zeros_like(dgq_ref)
            dgkv_ref[...] = jnp.zeros_like(dgkv_ref)

        dcq, dgq = rms_bwd(cq_ref[...], dqn_ref[...], gq_ref[...], Q_RANK)
        dckv, dgkv = rms_bwd(ckv_ref[...], dkvn_ref[...], gkv_ref[...], KV_RANK)
        dgq_ref[...] += dgq
        dgkv_ref[...] += dgkv
        dkr = dkr_ref[0]
        for h in range(1, HEADS):
            dkr = dkr + dkr_ref[h]
        dp_ref[:, PROJ_CQ:PROJ_CKV] = dcq.astype(BF16)
        dp_ref[:, PROJ_CKV:PROJ_KR] = dckv.astype(BF16)
        dp_ref[:, PROJ_KR:PROJ_Z] = _rot(dkr, c_ref[...], -s1_ref[...], -s2_ref[...]).astype(BF16)

    tab = pl.BlockSpec((ts, LANE), lambda i: (i, 0))
    return _pcall(
        body, name=name, grid=(s // ts,),
        in_specs=[pl.BlockSpec((ts, Q_RANK_PAD), lambda i: (i, 0)),
                  pl.BlockSpec((ts, KV_RANK), lambda i: (i, 0)),
                  pl.BlockSpec((HEADS, ts, LANE), lambda i: (0, i, 0)),
                  pl.BlockSpec((ts, Q_RANK_PAD), lambda i: (i, 0)),
                  pl.BlockSpec((ts, KV_RANK), lambda i: (i, PROJ_CKV // KV_RANK)),
                  _vec_spec(Q_RANK_PAD), _vec_spec(KV_RANK), tab, tab, tab],
        out_specs=[pl.BlockSpec((ts, PROJ_Z), lambda i: (i, 0)), _vec_spec(Q_RANK_PAD), _vec_spec(KV_RANK)],
        out_shape=[jax.ShapeDtypeStruct((s, PROJ_Z), BF16), jax.ShapeDtypeStruct((1, Q_RANK_PAD), F32),
                   jax.ShapeDtypeStruct((1, KV_RANK), F32)],
        compiler_params=_params(("arbitrary",)),
    )(dqn, dkvn, dkr_heads, proj, proj, gq, gkv, ck, s1k, s2k)


def _ada_mod(cond_raw, ada_w, bias_my, name):
    nl, d, ncol = ada_w.shape
    tk = _pick(d, 512)
    nk = d // tk

    def body(c_ref, w_ref, b_ref, o_ref, acc_ref):
        kk = pl.program_id(1)

        @pl.when(kk == 0)
        def _():
            acc_ref[...] = jnp.zeros_like(acc_ref)

        cv = c_ref[...]
        cond = (cv * _sigmoid(cv)).astype(BF16)
        acc_ref[...] += jnp.dot(cond, w_ref[...].astype(BF16), preferred_element_type=F32)

        @pl.when(kk == nk - 1)
        def _():
            o_ref[...] = acc_ref[...] + b_ref[...]

    return _pcall(
        body, name=name, grid=(nl, nk),
        in_specs=[pl.BlockSpec((N_DEV, tk), lambda l, kk: (0, kk)),
                  pl.BlockSpec((None, tk, ncol), lambda l, kk: (l, kk, 0)),
                  pl.BlockSpec((None, 1, ncol), lambda l, kk: (l, 0, 0))],
        out_specs=pl.BlockSpec((None, N_DEV, ncol), lambda l, kk: (l, 0, 0)),
        out_shape=jax.ShapeDtypeStruct((nl, N_DEV, ncol), F32),
        scratch_shapes=[pltpu.VMEM((N_DEV, ncol), F32)],
        compiler_params=_params(("parallel", "arbitrary")),
    )(cond_raw, ada_w, bias_my.reshape(nl, 1, ncol))


def _adam(w, g, m, v):
    m = ADAM_B1 * m + (1.0 - ADAM_B1) * g
    v = ADAM_B2 * v + (1.0 - ADAM_B2) * (g * g)
    m_hat = m / (1.0 - ADAM_B1 ** ADAM_STEP)
    v_hat = v / (1.0 - ADAM_B2 ** ADAM_STEP)
    delta = -ADAM_LR * (m_hat / (jnp.sqrt(v_hat) + ADAM_EPS) + ADAM_WD * w)
    return delta, m, v


def _ada_bwd_adam(cond_t, dmod_cols, w, m, v, name):
    nl, d, ncol = w.shape
    tk = _pick(d, 512)

    def body(c_ref, dm_ref, w_ref, m_ref, v_ref, g_ref, d_ref, mo_ref, vo_ref):
        cv = c_ref[...]
        cond = (cv * _sigmoid(cv)).astype(BF16)
        g = jnp.dot(cond, dm_ref[...].astype(BF16), preferred_element_type=F32)
        delta, m2, v2 = _adam(w_ref[...], g, m_ref[...], v_ref[...])
        g_ref[...] = g
        d_ref[...] = delta
        mo_ref[...] = m2
        vo_ref[...] = v2

    blk = pl.BlockSpec((None, tk, ncol), lambda l, kk: (l, kk, 0))
    shp = jax.ShapeDtypeStruct((nl, d, ncol), F32)
    return _pcall(
        body, name=name, grid=(nl, d // tk),
        in_specs=[pl.BlockSpec((tk, N_DEV), lambda l, kk: (kk, 0)),
                  pl.BlockSpec((None, N_DEV, ncol), lambda l, kk: (l, 0, 0)), blk, blk, blk],
        out_specs=[blk, blk, blk, blk], out_shape=[shp, shp, shp, shp],
        compiler_params=_params(("parallel", "parallel")),
    )(cond_t, dmod_cols, w, m, v)


def _adam_reduce(recv0, recv1, w, m, v, name):
    nl, r, c = w.shape
    tr = _pick(r, 128) if r % 128 == 0 else r
    tc = _pick(c, 1024)
    n0, n1 = recv0.shape[0] // r, recv1.shape[0] // r

    def body(r0_ref, r1_ref, w_ref, m_ref, v_ref, g_ref, d_ref, mo_ref, vo_ref):
        l = pl.program_id(0)

        def run(rr):
            g = rr[0].astype(F32)
            for sidx in range(1, rr.shape[0]):
                g = g + rr[sidx].astype(F32)
            delta, m2, v2 = _adam(w_ref[...], g, m_ref[...], v_ref[...])
            g_ref[...] = g
            d_ref[...] = delta
            mo_ref[...] = m2
            vo_ref[...] = v2

        @pl.when(l == 0)
        def _():
            run(r0_ref)

        @pl.when(l == 1)
        def _():
            run(r1_ref)

    def rblk(n, layer):
        return pl.BlockSpec((n, tr, tc), lambda l, i, j: (0, jnp.where(l == layer, i, 0), jnp.where(l == layer, j, 0)))

    blk = pl.BlockSpec((None, tr, tc), lambda l, i, j: (l, i, j))
    shp = jax.ShapeDtypeStruct((nl, r, c), F32)
    return _pcall(
        body, name=name, grid=(nl, r // tr, c // tc),
        in_specs=[rblk(n0, 0), rblk(n1, 1), blk, blk, blk],
        out_specs=[blk, blk, blk, blk], out_shape=[shp, shp, shp, shp],
        compiler_params=_params(("arbitrary", "parallel", "parallel")),
    )(recv0.reshape(n0, r, c), recv1.reshape(n1, r, c), w, m, v)


def _adam_small(gathered, w, m, v, name):
    r = w.shape[0]
    tr = _pick(r, 512) if r % 512 == 0 else r

    def body(p_ref, w_ref, m_ref, v_ref, g_ref, d_ref, mo_ref, vo_ref):
        g = p_ref[0]
        for sidx in range(1, N_DEV):
            g = g + p_ref[sidx]
        delta, m2, v2 = _adam(w_ref[...], g, m_ref[...], v_ref[...])
        g_ref[...] = g
        d_ref[...] = delta
        mo_ref[...] = m2
        vo_ref[...] = v2

    blk = pl.BlockSpec((tr, LANE), lambda i: (i, 0))
    shp = jax.ShapeDtypeStruct((r, LANE), F32)
    return _pcall(
        body, name=name, grid=(r // tr,),
        in_specs=[pl.BlockSpec((N_DEV, tr, LANE), lambda i: (0, i, 0)), blk, blk, blk],
        out_specs=[blk, blk, blk, blk], out_shape=[shp, shp, shp, shp],
        compiler_params=_params(("parallel",)),
    )(gathered, w, m, v)


def _my_place():
    x, y, c = lax.axis_index("x"), lax.axis_index("y"), lax.axis_index("c")
    return x, y, c, 4 * x + 2 * y + c


def _peer(x, y, c, k):
    px = 1 - x if (k >> 2) & 1 else x
    py = 1 - y if (k >> 1) & 1 else y
    pc = 1 - c if k & 1 else c
    return (px, py, pc), 4 * px + 2 * py + pc


def _slab(ref, shape, kind, p):
    r, cd = shape
    if kind == "row":
        return ref.at[pl.ds(pl.multiple_of(p * r, SUBLANE), r), :]
    return ref.at[:, pl.ds(pl.multiple_of(p * cd, LANE), cd)]


def _exchange_layout(arrays, kinds, gather):
    shard_shapes, dst_kinds, out_shapes = [], [], []
    for a, kind in zip(arrays, kinds):
        r, cd = a.shape
        if gather:
            shard, dst_kind = (r, cd), kind
        else:
            shard, dst_kind = ((r // N_DEV, cd) if kind == "row" else (r, cd // N_DEV)), "row"
        shard_shapes.append(shard)
        dst_kinds.append(dst_kind)
        full = (shard[0] * N_DEV, shard[1]) if dst_kind == "row" else (shard[0], shard[1] * N_DEV)
        out_shapes.append(jax.ShapeDtypeStruct(full, a.dtype))
    return shard_shapes, dst_kinds, out_shapes


def _exchange_copies(ins, outs, send_sems, recv_sems, sem_of, layout, kinds, gather):
    shard_shapes, dst_kinds, _ = layout
    x, y, c, me = _my_place()

    def src_for(a, p):
        return ins[a] if gather else _slab(ins[a], shard_shapes[a], kinds[a], p)

    def dst_slot(a, p):
        return _slab(outs[a], shard_shapes[a], dst_kinds[a], p)

    def local(a, sem):
        return pltpu.make_async_copy(src_for(a, me), dst_slot(a, me), sem)

    def remote(a, k, slot):
        peer, pidx = _peer(x, y, c, k)
        return pltpu.make_async_remote_copy(
            src_ref=src_for(a, pidx), dst_ref=dst_slot(a, me if slot == "mine" else pidx),
            send_sem=send_sems.at[sem_of(a, k)], recv_sem=recv_sems.at[sem_of(a, k)],
            device_id=peer, device_id_type=MESH)

    return local, remote


def _place_own(src, src_kind, slab_shape, dst_kind, full, name, index=None):
    r, cd = slab_shape
    tr = _pick(r, 512)
    nr = r // tr
    me = _my_place()[3] if index is None else index

    def body(me_ref, s_ref, o_ref):
        o_ref[...] = s_ref[...].astype(o_ref.dtype)

    def where(kind):
        if kind is None:
            return lambda i, me_ref: (i, 0)
        if kind == "row":
            return lambda i, me_ref: (me_ref[0] * nr + i, 0)
        return lambda i, me_ref: (i, me_ref[0])

    return _pcall(
        body, name=name,
        grid_spec=pltpu.PrefetchScalarGridSpec(
            num_scalar_prefetch=1, grid=(nr,),
            in_specs=[pl.BlockSpec((tr, cd), where(src_kind))],
            out_specs=pl.BlockSpec((tr, cd), where(dst_kind))),
        out_shape=jax.ShapeDtypeStruct(full.shape, full.dtype),
        compiler_params=_params(("arbitrary",)),
    )(jnp.reshape(me, (1,)).astype(jnp.int32), src)


def _landing_with_own_slab(arrays, kinds, gather, layout, name=None, order_after=None):
    _, _, _, me = _my_place()
    index = None
    if order_after is not None:
        first = order_after.reshape(-1)[0].astype(jnp.int32)
        index = me + jnp.minimum(jnp.maximum(first, 0), 0)
    lands = []
    for a in range(len(arrays)):
        (r, cd), dst_kind, full = layout[0][a], layout[1][a], layout[2][a]
        if name is not None:
            lands.append(_place_own(arrays[a], None if gather else kinds[a], (r, cd), dst_kind, full, name,
                                    index=index))
            continue
        if gather:
            piece = arrays[a]
        elif kinds[a] == "row":
            piece = lax.dynamic_slice_in_dim(arrays[a], me * r, r, axis=0)
        else:
            piece = lax.dynamic_slice_in_dim(arrays[a], me * cd, cd, axis=1)
        at = (me * r, 0) if dst_kind == "row" else (0, me * cd)
        lands.append(lax.dynamic_update_slice(lax.empty(full.shape, full.dtype), piece, at))
    return lands


def _exchange(arrays, kinds, gather, name, after=None):
    n = len(arrays)
    n_extra = 0 if after is None else 1
    layout = _exchange_layout(arrays, kinds, gather)
    lands = _landing_with_own_slab(arrays, kinds, gather, layout)

    def body(*refs):
        ins, outs = refs[:n], refs[2 * n + n_extra:3 * n + n_extra]
        send_sems, recv_sems = refs[3 * n + n_extra:]
        _, remote = _exchange_copies(ins, outs, send_sems, recv_sems,
                                     lambda a, k: a * (N_DEV - 1) + k - 1, layout, kinds, gather)
        for a in range(n):
            for k in range(1, N_DEV):
                remote(a, k, "mine").start()
        for a in range(n):
            for k in range(1, N_DEV):
                arrival = remote(a, k, "theirs")
                arrival.wait_send()
                arrival.wait_recv()

    anyspec = pl.BlockSpec(memory_space=pl.ANY)
    outs = _pcall(
        body, comm=True, name=name,
        in_specs=[anyspec] * (2 * n + n_extra), out_specs=[anyspec] * n, out_shape=layout[2],
        input_output_aliases={n + a: a for a in range(n)},
        scratch_shapes=[pltpu.SemaphoreType.DMA((n * (N_DEV - 1),)), pltpu.SemaphoreType.DMA((n * (N_DEV - 1),))],
    )(*arrays, *lands, *([] if after is None else [after]))
    return list(outs)


HBM_SPEC = pl.BlockSpec(memory_space=pltpu.HBM)
SEM_SPEC = pl.BlockSpec(memory_space=pltpu.SEMAPHORE)
ANY_SPEC = pl.BlockSpec(memory_space=pl.ANY)
DATAFLOW = pltpu.SideEffectType.DATAFLOW_SIDE_EFFECTING


def _exchange_start(arrays, kinds, gather, name, after, carry=()):
    n, nc = len(arrays), len(carry)
    layout = _exchange_layout(arrays, kinds, gather)
    lands = _landing_with_own_slab(arrays, kinds, gather, layout, "place_own")

    def body(*refs):
        ins, outs = refs[:n], refs[n:2 * n]
        send_sems, recv_sems = refs[2 * n + nc + 1], refs[2 * n + nc + 2]
        token = refs[2 * n + nc + 3 + 2 * n + nc]
        _, remote = _exchange_copies(ins, outs, send_sems, recv_sems, lambda a, k: a, layout, kinds, gather)
        for a in range(n):
            for k in range(1, N_DEV):
                remote(a, k, "mine").start()
        token[...] = jnp.zeros_like(token)

    passed = list(arrays) + lands + list(carry)
    res = pl.pallas_call(
        body, name=name,
        out_shape=(pltpu.SemaphoreType.DMA((n,)), pltpu.SemaphoreType.DMA((n,)),
                   *[pltpu.HBM(t.shape, t.dtype) for t in passed], jax.ShapeDtypeStruct((SUBLANE, LANE), F32)),
        in_specs=[HBM_SPEC] * (2 * n + nc) + [ANY_SPEC],
        out_specs=(SEM_SPEC, SEM_SPEC, *([HBM_SPEC] * (2 * n + nc)), pl.BlockSpec(memory_space=pltpu.VMEM)),
        input_output_aliases={i: 2 + i for i in range(2 * n + nc)},
        compiler_params=pltpu.CompilerParams(has_side_effects=DATAFLOW),
    )(*[pltpu.with_memory_space_constraint(t, pltpu.HBM) for t in passed], after)
    handle = (res[0], res[1], list(res[2:2 + n]), list(res[2 + n:2 + 2 * n]), tuple(kinds), gather)
    return handle, res[-1], list(res[2 + 2 * n:2 + 2 * n + nc])


def _exchange_wait(handle, name, after):
    send_sems, recv_sems, ins_thru, lands_thru, kinds, gather = handle
    n = len(ins_thru)
    layout = _exchange_layout(ins_thru, kinds, gather)

    def body(*refs):
        ins, outs = refs[:n], refs[n:2 * n]
        s_sems, r_sems = refs[2 * n], refs[2 * n + 1]
        _, remote = _exchange_copies(ins, outs, s_sems, r_sems, lambda a, k: a, layout, kinds, gather)
        for a in range(n):
            for k in range(1, N_DEV):
                arrival = remote(a, k, "theirs")
                arrival.wait_send()
                arrival.wait_recv()

    res = pl.pallas_call(
        body, name=name,
        out_shape=[pltpu.HBM(t.shape, t.dtype) for t in ins_thru + lands_thru],
        in_specs=[HBM_SPEC] * (2 * n) + [SEM_SPEC, SEM_SPEC, ANY_SPEC],
        out_specs=[HBM_SPEC] * (2 * n),
        input_output_aliases={i: i for i in range(2 * n)},
        compiler_params=pltpu.CompilerParams(has_side_effects=DATAFLOW),
    )(*ins_thru, *lands_thru, send_sems, recv_sems, after)
    return list(res[n:2 * n])


STAGE1_KS = (1, 2, 4, 6)
FORWARD_KS = (2, 4, 6)


def _gather2_copies(lands, shard_shapes, kinds):
    x, y, c, me = _my_place()

    def slab(a, p):
        return _slab(lands[a], shard_shapes[a], kinds[a], p)

    def stage1(a, k, sems, arriving):
        peer, pidx = _peer(x, y, c, k)
        s = slab(a, pidx if arriving else me)
        return pltpu.make_async_remote_copy(src_ref=s, dst_ref=s, send_sem=sems[0].at[a], recv_sem=sems[1].at[a],
                                            device_id=peer, device_id_type=MESH)

    def stage2(a, k, sems, arriving):
        sib, _ = _peer(x, y, c, 1)
        _, mine = _peer(x, y, c, k)
        _, theirs = _peer(x, y, 1 - c, k)
        s = slab(a, theirs if arriving else mine)
        return pltpu.make_async_remote_copy(src_ref=s, dst_ref=s, send_sem=sems[0].at[a], recv_sem=sems[1].at[a],
                                            device_id=sib, device_id_type=MESH)

    return stage1, stage2


def _gather2_call(lands, sems_in, name, after, make_body, returns_sems):
    n = len(lands)
    n_in = len(sems_in)

    def body(*refs):
        land_refs = refs[:n]
        in_sems = refs[n:n + n_in]
        rest = refs[n + n_in + 1:]
        out_sems = rest[:2] if returns_sems else ()
        make_body(land_refs, in_sems, out_sems)
        if returns_sems:
            token = rest[2 + n]
            token[...] = jnp.zeros_like(token)

    sem_shapes = (pltpu.SemaphoreType.DMA((n,)), pltpu.SemaphoreType.DMA((n,))) if returns_sems else ()
    tok_shape = (jax.ShapeDtypeStruct((SUBLANE, LANE), F32),) if returns_sems else ()
    n_sem_out = len(sem_shapes)
    res = pl.pallas_call(
        body, name=name,
        out_shape=(*sem_shapes, *[pltpu.HBM(t.shape, t.dtype) for t in lands], *tok_shape),
        in_specs=[HBM_SPEC] * n + [SEM_SPEC] * n_in + [ANY_SPEC],
        out_specs=(*([SEM_SPEC] * n_sem_out), *([HBM_SPEC] * n),
                   *([pl.BlockSpec(memory_space=pltpu.VMEM)] if returns_sems else [])),
        input_output_aliases={i: n_sem_out + i for i in range(n)},
        compiler_params=pltpu.CompilerParams(has_side_effects=DATAFLOW),
    )(*[pltpu.with_memory_space_constraint(t, pltpu.HBM) for t in lands], *sems_in, after)
    sems_out = tuple(res[:n_sem_out])
    lands_thru = list(res[n_sem_out:n_sem_out + n])
    return sems_out, lands_thru, (res[-1] if returns_sems else None)


def _gather2_start(shards, kinds, name, after):
    n = len(shards)
    shard_shapes, dst_kinds, fulls = _exchange_layout(shards, kinds, True)
    layout = (shard_shapes, dst_kinds, [jax.ShapeDtypeStruct(f.shape, BF16) for f in fulls])
    lands = _landing_with_own_slab(shards, kinds, True, layout, "place_own", order_after=after)

    def make_body(land_refs, in_sems, out_sems):
        stage1, _ = _gather2_copies(land_refs, layout[0], kinds)
        for a in range(n):
            for k in STAGE1_KS:
                stage1(a, k, out_sems, False).start()

    sems, lands, token = _gather2_call(lands, (), name, after, make_body, True)
    return (sems, lands, layout[0], tuple(kinds)), token


def _gather2_forward(handle, name, after):
    sems1, lands, shard_shapes, kinds = handle
    n = len(lands)

    def make_body(land_refs, in_sems, out_sems):
        stage1, stage2 = _gather2_copies(land_refs, shard_shapes, kinds)
        for a in range(n):
            for k in STAGE1_KS:
                arrival = stage1(a, k, in_sems, True)
                arrival.wait_send()
                arrival.wait_recv()
        for a in range(n):
            for k in FORWARD_KS:
                stage2(a, k, out_sems, False).start()

    sems2, lands, token = _gather2_call(lands, sems1, name, after, make_body, True)
    return (sems2, lands, shard_shapes, kinds), token


def _gather2_wait(handle, name, after):
    sems2, lands, shard_shapes, kinds = handle
    n = len(lands)

    def make_body(land_refs, in_sems, out_sems):
        _, stage2 = _gather2_copies(land_refs, shard_shapes, kinds)
        for a in range(n):
            for k in FORWARD_KS:
                arrival = stage2(a, k, in_sems, True)
                arrival.wait_send()
                arrival.wait_recv()

    _, lands, _ = _gather2_call(lands, sems2, name, after, make_body, False)
    return lands


N_CHIP = N_DEV // 2


def _scatter2_pair(g, name):
    r, c8 = g.shape
    cd = c8 // N_DEV

    def body(g_ref, o_ref, send_sems, recv_sems):
        x, y, c, _ = _my_place()
        sib, _ = _peer(x, y, c, 1)

        def copy(ch):
            theirs = 2 * ch + (1 - c)
            return pltpu.make_async_remote_copy(
                src_ref=_slab(g_ref, (r, cd), "col", theirs), dst_ref=_slab(o_ref, (r, cd), "row", ch),
                send_sem=send_sems.at[ch], recv_sem=recv_sems.at[ch], device_id=sib, device_id_type=MESH)

        for ch in range(N_CHIP):
            copy(ch).start()
        for ch in range(N_CHIP):
            copy(ch).wait_send()
            copy(ch).wait_recv()

    return _pcall(
        body, comm=True, name=name,
        in_specs=[ANY_SPEC], out_specs=ANY_SPEC, out_shape=jax.ShapeDtypeStruct((N_CHIP * r, cd), g.dtype),
        scratch_shapes=[pltpu.SemaphoreType.DMA((N_CHIP,)), pltpu.SemaphoreType.DMA((N_CHIP,))],
    )(g)


def _scatter2_add(g, from_sibling, name):
    r, c8 = g.shape
    cd = c8 // N_DEV
    tr = _pick(r, 512)
    nr = r // tr
    _, _, core, _ = _my_place()

    def body(c_ref, g_ref, s_ref, o_ref):
        o_ref[...] = (g_ref[...].astype(F32) + s_ref[...].astype(F32)).astype(o_ref.dtype)

    return _pcall(
        body, name=name,
        grid_spec=pltpu.PrefetchScalarGridSpec(
            num_scalar_prefetch=1, grid=(N_CHIP, nr),
            in_specs=[pl.BlockSpec((tr, cd), lambda ch, i, c_ref: (i, 2 * ch + c_ref[0])),
                      pl.BlockSpec((tr, cd), lambda ch, i, c_ref: (ch * nr + i, 0))],
            out_specs=pl.BlockSpec((tr, cd), lambda ch, i, c_ref: (ch * nr + i, 0))),
        out_shape=jax.ShapeDtypeStruct((N_CHIP * r, cd), g.dtype),
        compiler_params=_params(("arbitrary", "arbitrary")),
    )(jnp.reshape(core, (1,)).astype(jnp.int32), g, from_sibling)


def _scatter2_copies(refs, r):
    part_ref, recv_ref = refs
    x, y, c, _ = _my_place()
    my_chip = 2 * x + y

    def rows(ref, ch):
        return ref.at[pl.ds(pl.multiple_of(ch * r, SUBLANE), r), :]

    def copy(k, sems, arriving):
        peer, _ = _peer(x, y, c, k)
        peer_chip = 2 * peer[0] + peer[1]
        return pltpu.make_async_remote_copy(
            src_ref=rows(part_ref, peer_chip), dst_ref=rows(recv_ref, peer_chip if arriving else my_chip),
            send_sem=sems[0].at[0], recv_sem=sems[1].at[0], device_id=peer, device_id_type=MESH)

    return copy


def _scatter2_start(g, name, after, carry=()):
    r = g.shape[0]
    x, y, _, _ = _my_place()
    from_sibling = _scatter2_pair(g, name + "_pair")
    partial = _scatter2_add(g, from_sibling, name + "_add")
    cd = partial.shape[1]
    recv = _place_own(partial, "row", (r, cd), "row", jax.ShapeDtypeStruct(partial.shape, partial.dtype),
                      "place_own", index=2 * x + y)

    def make_body(land_refs, in_sems, out_sems):
        copy = _scatter2_copies(land_refs[:2], r)
        for k in FORWARD_KS:
            copy(k, out_sems, False).start()

    sems, lands, token = _gather2_call([partial, recv] + list(carry), (), name, after, make_body, True)
    return (sems, lands[:2], r), token, lands[2:]


def _scatter2_wait(handle, name, after):
    sems, lands, r = handle

    def make_body(land_refs, in_sems, out_sems):
        copy = _scatter2_copies(land_refs, r)
        for k in FORWARD_KS:
            arrival = copy(k, in_sems, True)
            arrival.wait_send()
            arrival.wait_recv()

    _, lands, _ = _gather2_call(lands, sems, name, after, make_body, False)
    return lands[1]


def _pad_mla_w_in(w):
    d = w.shape[0]
    z = lambda n: jnp.zeros((d, n), w.dtype)
    o1, o2, o3 = Q_RANK, Q_RANK + KV_RANK, Q_RANK + KV_RANK + ROPE
    return jnp.concatenate([w[:, :o1], z(Q_RANK_PAD - Q_RANK), w[:, o1:o2], w[:, o2:o3], z(LANE - ROPE), w[:, o3:]], axis=1)


def _unpad_mla_w_in(g):
    return jnp.concatenate([g[:, :Q_RANK], g[:, PROJ_CKV:PROJ_KR], g[:, PROJ_KR:PROJ_KR + ROPE], g[:, PROJ_Z:]], axis=1)


def _pad_w_uq(w):
    w3 = w.reshape(Q_RANK, HEADS, NOPE + ROPE)
    w3 = jnp.pad(w3, ((0, Q_RANK_PAD - Q_RANK), (0, 0), (0, HEAD_PAD - NOPE - ROPE)))
    return w3.reshape(Q_RANK_PAD, HEADS * HEAD_PAD)


def _unpad_w_uq(g):
    return g[:Q_RANK].reshape(Q_RANK, HEADS, HEAD_PAD)[:, :, :NOPE + ROPE].reshape(Q_RANK, HEADS * (NOPE + ROPE))


def _pack(pieces):
    flat = [p.reshape(-1).astype(F32) for p in pieces]
    tot = sum(f.shape[0] for f in flat)
    unit = SUBLANE * LANE
    padn = (-tot) % unit
    if padn:
        flat.append(jnp.zeros((padn,), F32))
    return jnp.concatenate(flat).reshape(-1, LANE)


def _unpack(packed, shapes, lead=()):
    flat = packed.reshape(tuple(lead) + (-1,))
    out, off = [], 0
    for shp in shapes:
        nel = int(np.prod(shp))
        out.append(flat[..., off:off + nel].reshape(tuple(lead) + tuple(shp)))
        off += nel
    return out


SMALL_GROUPS = {
    "a": [('ada_b', lambda t: t[1:]), ('pre_g', lambda t: t[1:]), ('post_g', lambda t: t[1:]),
          ('sgu_norm_g', lambda t: t[1:]), ('sgu_w_s', lambda t: t[1:]), ('sgu_b_s', lambda t: t[1:]),
          ('mla_q_norm_g', lambda t: t), ('mla_kv_norm_g', lambda t: t)],
    "b": [('ada_b', lambda t: t[0:1, 2 * t.shape[1] // 3:]), ('post_g', lambda t: t[0:1]),
          ('sgu_norm_g', lambda t: t[0:1]), ('sgu_w_s', lambda t: t[0:1]), ('sgu_b_s', lambda t: t[0:1])],
    "c": [('ada_b', lambda t: t[0:1, :2 * t.shape[1] // 3]), ('pre_g', lambda t: t[0:1])],
}


WEIGHTS = ['ada_w', 'ada_b', 'pre_g', 'post_g', 'sgu_w_in', 'sgu_norm_g', 'sgu_w_s', 'sgu_b_s', 'sgu_w_out',
           'mla_w_in', 'mla_q_norm_g', 'mla_kv_norm_g', 'mla_w_uq', 'mla_w_ukv', 'mla_w_out']
INPUTS = ['x', 'c'] + WEIGHTS + ['loss_target'] + ['m_' + n for n in WEIGHTS] + ['v_' + n for n in WEIGHTS]


def kernel(x, c, ada_w, ada_b, pre_g, post_g, sgu_w_in, sgu_norm_g, sgu_w_s, sgu_b_s, sgu_w_out, mla_w_in, mla_q_norm_g, mla_kv_norm_g, mla_w_uq, mla_w_ukv, mla_w_out, loss_target, m_ada_w, m_ada_b, m_pre_g, m_post_g, m_sgu_w_in, m_sgu_norm_g, m_sgu_w_s, m_sgu_b_s, m_sgu_w_out, m_mla_w_in, m_mla_q_norm_g, m_mla_kv_norm_g, m_mla_w_uq, m_mla_w_ukv, m_mla_w_out, v_ada_w, v_ada_b, v_pre_g, v_post_g, v_sgu_w_in, v_sgu_norm_g, v_sgu_w_s, v_sgu_b_s, v_sgu_w_out, v_mla_w_in, v_mla_q_norm_g, v_mla_kv_norm_g, v_mla_w_uq, v_mla_w_ukv, v_mla_w_out):
    given = locals()
    A = {name: given[name] for name in INPUTS}
    x0 = A['x'][0]
    tgt = A['loss_target'][0]
    s, d = x0.shape
    e = 2 * d
    ncol = 3 * d // N_DEV
    _, _, _, me = _my_place()
    ktabs, qtabs = _rope_tables(s)

    gains = jnp.zeros((SUBLANE, LANE), F32)
    gains = gains.at[0:2, :Q_RANK // N_DEV].set(A['mla_q_norm_g'])
    gains = gains.at[2:4, :KV_RANK // N_DEV].set(A['mla_kv_norm_g'])
    c8 = jnp.broadcast_to(A['c'], (SUBLANE, d))
    cg, gg = _exchange([c8, gains], ["row", "row"], True, "ag_cond")
    cond_raw = cg.reshape(N_DEV, SUBLANE, d)[:, 0, :]
    gg = gg.reshape(N_DEV, SUBLANE, LANE)
    gq_full = jnp.transpose(gg[:, 0:2, :Q_RANK // N_DEV], (1, 0, 2)).reshape(N_MIX, Q_RANK)
    gkv_full = jnp.transpose(gg[:, 2:4, :KV_RANK // N_DEV], (1, 0, 2)).reshape(N_MIX, KV_RANK)
    gq_pad = jnp.pad(gq_full, ((0, 0), (0, Q_RANK_PAD - Q_RANK)))

    bias_my = lax.dynamic_slice_in_dim(A['ada_b'], me * ncol, ncol, axis=1)
    mod_part = _ada_mod(cond_raw, A['ada_w'], bias_my, "ada_mod")
    send = jnp.pad(jnp.transpose(mod_part, (1, 0, 2)), ((0, 0), (0, SUBLANE - DEPTH), (0, 0)))
    (rb,) = _exchange([send.reshape(N_DEV * SUBLANE, ncol)], ["row"], False, "a2a_mod")

    token = rb
    gathers = {}
    for i in range(DEPTH):
        j = i // N_MIX
        if i % N_MIX == 0:
            parts = [("in", [A['sgu_w_in'][j]], ["col"]), ("out", [A['sgu_w_out'][j]], ["row"])]
        else:
            parts = [("all", [A['mla_w_in'][j], A['mla_w_uq'][j], A['mla_w_ukv'][j], A['mla_w_out'][j]],
                      ["col", "col", "col", "row"])]
        for part, shards, kinds in parts:
            gathers[(i, part)], token = _gather2_start(shards, kinds, f"ag_start_{i}_{part}", token)

    def forward_gathers(i, after):
        for key in [k for k in gathers if k[0] == i]:
            gathers[key], _ = _gather2_forward(gathers[key], f"ag_forward_{key[0]}_{key[1]}", after)

    mod = jnp.transpose(rb.reshape(N_DEV, SUBLANE, ncol)[:, :DEPTH, :], (1, 0, 2)).reshape(DEPTH, 3 * d) + token[0, 0]
    shift = [mod[i:i + 1, :d] for i in range(DEPTH)]
    scale = [mod[i:i + 1, d:2 * d] for i in range(DEPTH)]
    gate = [mod[i:i + 1, 2 * d:] for i in range(DEPTH)]

    saved = []
    x = x0
    for i in range(DEPTH):
        j = i // N_MIX
        h = _pre_fwd(x, A['pre_g'][i:i + 1], scale[i], shift[i], f"pre_fwd")
        if i == 0:
            gathers[(0, "in")], _ = _gather2_forward(gathers[(0, "in")], "ag_forward_0_in", h)
        if i % N_MIX == 0:
            (w_in,) = _gather2_wait(gathers[(i, "in")], f"ag_wait_{i}_in", h)
            uvz = _mm(h, w_in, "nn", F32, "sgu_in")
            if i == 0:
                gathers[(0, "out")], _ = _gather2_forward(gathers[(0, "out")], "ag_forward_0_out", uvz)
            bias_full = jnp.repeat(A['sgu_b_s'][j].T, e // SGU_GROUPS, axis=1)
            ng = A['sgu_norm_g'][j:j + 1]
            y = _sgu_mid_fwd(uvz, ng, A['sgu_w_s'][j], bias_full, "sgu_mid_fwd")
            forward_gathers(i + 1, y)
            (w_out,) = _gather2_wait(gathers[(i, "out")], f"ag_wait_{i}_out", y)
            out = _mm(y, w_out, "nn", F32, "sgu_out")
            saved.append(dict(x=x, h=h, uvz=uvz, y=y, out=out, w_in=w_in, w_out=w_out, bias=bias_full, ng=ng))
        else:
            w_in, w_uq, w_ukv, w_out = _gather2_wait(gathers[(i, "all")], f"ag_wait_{i}_all", h)
            w_in = _pad_mla_w_in(w_in)
            w_uq = _pad_w_uq(w_uq)
            gq, gkv = gq_pad[j:j + 1], gkv_full[j:j + 1]
            proj = _mm(h, w_in, "nn", F32, "mla_in")
            qn, kvn, kr = _mla_norm_fwd(proj, gq, gkv, ktabs, "mla_norm_fwd")
            q = _mm(qn, w_uq, "nn", F32, "mla_uq")
            q_cat = _rope_heads(q, qtabs, 1.0, SOFTMAX_SCALE * LOG2_E, BF16, "rope_q_fwd")
            kv = _mm(kvn, w_ukv, "nn", BF16, "mla_ukv")
            o, y, lse = _attn_fwd(q_cat, kv, kr, proj, "attn_fwd")
            forward_gathers(i + 1, y)
            out = _mm(y, w_out, "nn", F32, "mla_out")
            saved.append(dict(x=x, h=h, proj=proj, qn=qn, kvn=kvn, kr=kr, q_cat=q_cat, kv=kv, o=o, y=y, lse=lse,
                              out=out, w_in=w_in, w_uq=w_uq, w_ukv=w_ukv, w_out=w_out, gq=gq, gkv=gkv))
        x = _post_fwd(x, out, gate[i], A['post_g'][i:i + 1], "post_fwd")

    dx, loss_row = _loss_head(x, tgt, "loss_head")
    loss = lax.psum(loss_row[0, 0], ("x", "y", "c"))

    d_shift, d_scale, d_gate = [None] * DEPTH, [None] * DEPTH, [None] * DEPTH
    d_pre, d_post = [None] * DEPTH, [None] * DEPTH
    scatters, scatters_out = [None] * DEPTH, [None] * DEPTH
    small, small_grads, small_handles = {}, {}, {}
    for i in reversed(range(DEPTH)):
        j = i // N_MIX
        sv = saved[i]
        dout, d_gate[i], d_post[i] = _post_bwd(dx, sv['out'], gate[i], A['post_g'][i:i + 1], "post_bwd")
        if i % N_MIX == 0:
            dy = _mm(dout, sv['w_out'], "nt", F32, "sgu_out_dx")
            g_w_out = _mm(sv['y'], dout, "tn", BF16, "sgu_out_dw")
            scatters_out[i], token, (dy,) = _exchange_start([g_w_out], ["row"], False, f"rs_out_start_{i}", token,
                                                            carry=[dy])
            duvz, dws, dbs, dng = _sgu_mid_bwd(sv['uvz'], dy, sv['ng'], A['sgu_w_s'][j], sv['bias'], "sgu_mid_bwd")
            small[('sgu', j)] = (dws, dbs.reshape(SGU_GROUPS, SGU_BLOCK), dng)
            g_w_in = _mm(sv['h'], duvz, "tn", BF16, "sgu_in_dw")
            if i == 0:
                scatters[i], token, (duvz,) = _scatter2_start(g_w_in, "rs2_start_0", token, carry=[duvz])
                small_grads["b"] = [d_gate[0], d_post[0], dng, dws[None], small[('sgu', 0)][1][None]]
                small_handles["b"], token, (duvz,) = _exchange_start(
                    [_pack(small_grads["b"])], ["row"], True, "ag_small_b_start", token, carry=[duvz])
            else:
                scatters[i], token, (duvz,) = _exchange_start([g_w_in], ["col"], False, f"rs_start_{i}", token,
                                                              carry=[duvz])
            dh = _mm(duvz, sv['w_in'], "nt", F32, "sgu_in_dx")
        else:
            dy = _mm(dout, sv['w_out'], "nt", F32, "mla_out_dx")
            g_w_out = _mm(sv['y'], dout, "tn", BF16, "mla_out_dw")
            scatters_out[i], token, (dy,) = _exchange_start([g_w_out], ["row"], False, f"rs_out_start_{i}", token,
                                                            carry=[dy])
            do, dz = _mla_gate_bwd(dy, sv['o'], sv['proj'], "mla_gate_bwd")
            dq, dkv, dkr_heads = _attn_bwd(sv['q_cat'], sv['kv'], sv['kr'], do, sv['o'], sv['lse'], "attn_bwd")
            dq_b = _rope_heads(dq, qtabs, -1.0, SOFTMAX_SCALE, BF16, "rope_q_bwd")
            dqn = _mm(dq_b, sv['w_uq'], "nt", F32, "mla_uq_dx")
            g_w_uq = _unpad_w_uq(_mm(sv['qn'], dq_b, "tn", BF16, "mla_uq_dw"))
            dkvn = _mm(dkv, sv['w_ukv'], "nt", F32, "mla_ukv_dx")
            g_w_ukv = _mm(sv['kvn'], dkv, "tn", BF16, "mla_ukv_dw")
            dpa, dgq, dgkv = _mla_norm_bwd(dqn, dkvn, dkr_heads, sv['proj'], sv['gq'], sv['gkv'], ktabs, "mla_norm_bwd")
            dproj = jnp.concatenate([dpa, dz], axis=1)
            g_w_in = _unpad_mla_w_in(_mm(sv['h'], dproj, "tn", BF16, "mla_in_dw"))
            scatters[i], token, (dproj,) = _exchange_start(
                [g_w_in, g_w_uq, g_w_ukv], ["col", "col", "col"], False, f"rs_start_{i}", token, carry=[dproj])
            dh = _mm(dproj, sv['w_in'], "nt", F32, "mla_in_dx")
            small[('mla', j)] = (dgq[:, :Q_RANK], dgkv)
        dx, d_shift[i], d_scale[i], d_pre[i] = _pre_bwd(dh, sv['x'], dx, A['pre_g'][i:i + 1], scale[i], "pre_bwd")
        if i == 1:
            small_grads["a"] = [
                jnp.concatenate([jnp.concatenate([d_shift[l], d_scale[l], d_gate[l]], axis=1)
                                 for l in range(1, DEPTH)], axis=0),
                jnp.concatenate(d_pre[1:], axis=0), jnp.concatenate(d_post[1:], axis=0),
                small[('sgu', 1)][2], small[('sgu', 1)][0][None], small[('sgu', 1)][1][None],
                jnp.concatenate([small[('mla', jj)][0] for jj in range(N_MIX)], axis=0),
                jnp.concatenate([small[('mla', jj)][1] for jj in range(N_MIX)], axis=0)]
            small_handles["a"], token, (dx,) = _exchange_start(
                [_pack(small_grads["a"])], ["row"], True, "ag_small_a_start", token, carry=[dx])

    res = {}

    def big(name, recv0, recv1):
        res[name] = _adam_reduce(recv0, recv1, A[name], A['m_' + name], A['v_' + name], "adam_" + name)

    def small_full(name, prefix):
        t = A[prefix + name]
        if name in ('mla_q_norm_g', 'mla_kv_norm_g'):
            t = lax.dynamic_update_slice_in_dim(jnp.zeros((t.shape[0], t.shape[1] * N_DEV), F32), t,
                                                me * t.shape[1], axis=1)
        return t

    def finish(i, after):
        first = ([_scatter2_wait(scatters[i], "rs2_wait_0", after)] if i == 0
                 else _exchange_wait(scatters[i], f"rs_wait_{i}", after))
        return first + _exchange_wait(scatters_out[i], f"rs_out_wait_{i}", after)

    recv_mla = {j: finish(N_MIX * j + 1, dx) for j in reversed(range(N_MIX))}
    for idx_w, name in enumerate(['mla_w_in', 'mla_w_uq', 'mla_w_ukv', 'mla_w_out']):
        big(name, recv_mla[0][idx_w], recv_mla[1][idx_w])

    recv_sgu = {j: finish(N_MIX * j, res['mla_w_out'][0]) for j in reversed(range(N_MIX))}
    for idx_w, name in enumerate(['sgu_w_in', 'sgu_w_out']):
        big(name, recv_sgu[0][idx_w], recv_sgu[1][idx_w])

    small_grads["c"] = [jnp.concatenate([d_shift[0], d_scale[0]], axis=1), d_pre[0]]
    gathered = {tag: _exchange_wait(small_handles[tag], f"ag_small_{tag}_wait", res['sgu_w_in'][0])[0]
                for tag in ("a", "b")}
    (gathered["c"],) = _exchange([_pack(small_grads["c"])], ["row"], True, "ag_small_c", after=res['sgu_w_in'][0])

    upd, parts = {}, {}
    for tag, group in SMALL_GROUPS.items():
        shapes = [g.shape for g in small_grads[tag]]
        packs = [_pack([pick(small_full(name, prefix)) for name, pick in group]) for prefix in ('', 'm_', 'v_')]
        outs4 = _adam_small(gathered[tag].reshape(N_DEV, -1, LANE), *packs, "adam_small_" + tag)
        upd[tag] = [_unpack(t, shapes) for t in outs4]
        parts[tag] = _unpack(gathered[tag].reshape(N_DEV, -1), shapes, lead=(N_DEV,))
    for k_out in range(4):
        ab_rest, pre_rest, post_rest, ng_1, ws_1, bs_1, gq_all, gkv_all = upd["a"][k_out]
        ab_gate0, post_0, ng_0, ws_0, bs_0 = upd["b"][k_out]
        ab_0, pre_0 = upd["c"][k_out]
        for name, val in (
                ('ada_b', jnp.concatenate([jnp.concatenate([ab_0, ab_gate0], axis=1), ab_rest], axis=0)),
                ('pre_g', jnp.concatenate([pre_0, pre_rest], axis=0)),
                ('post_g', jnp.concatenate([post_0, post_rest], axis=0)),
                ('sgu_norm_g', jnp.concatenate([ng_0, ng_1], axis=0)),
                ('sgu_w_s', jnp.concatenate([ws_0, ws_1], axis=0)),
                ('sgu_b_s', jnp.concatenate([bs_0, bs_1], axis=0)),
                ('mla_q_norm_g', gq_all), ('mla_kv_norm_g', gkv_all)):
            if name in ('mla_q_norm_g', 'mla_kv_norm_g'):
                wdt = A[name].shape[1]
                val = lax.dynamic_slice_in_dim(val, me * wdt, wdt, axis=1)
            res.setdefault(name, [None] * 4)[k_out] = val

    dmod_all = jnp.concatenate([jnp.concatenate([parts["c"][0], parts["b"][0]], axis=2), parts["a"][0]], axis=1)
    dmod_cols = jnp.transpose(lax.dynamic_slice_in_dim(dmod_all, me * ncol, ncol, axis=2), (1, 0, 2))
    res['ada_w'] = _ada_bwd_adam(jnp.transpose(cond_raw), dmod_cols, A['ada_w'], A['m_ada_w'], A['v_ada_w'], "ada_bwd")

    outs = [loss, dx[None]]
    for k_out in range(4):
        outs += [res[n][k_out] for n in WEIGHTS]
    return tuple(outs)
```

```python
import functools
import math

import numpy as np
import jax
import jax.numpy as jnp
from jax import lax
from jax.experimental import pallas as pl
from jax.experimental.pallas import tpu as pltpu

F32 = jnp.float32
BF16 = jnp.bfloat16
MESH = pl.DeviceIdType.MESH

N_DEV = 8
DEPTH = 4
N_MIX = 2
NORM_EPS = 1e-6
CHUNK = 64
SGU_BLOCK = 128
SGU_GROUPS = 16
HEADS = 16
Q_RANK = 448
Q_RANK_PAD = 512
KV_RANK = 512
NOPE = 128
ROPE = 64
HALF = ROPE // 2
V_DIM = 128
HEAD_PAD = 256
ROPE_THETA = 10000.0
MLA_WIDTH = HEADS * V_DIM
LANE = 128
SUBLANE = 8
PROJ_CQ = 0
PROJ_CKV = Q_RANK_PAD
PROJ_KR = Q_RANK_PAD + KV_RANK
PROJ_Z = PROJ_KR + LANE
PROJ_W = PROJ_Z + MLA_WIDTH

ADAM_LR = 0.001
ADAM_B1 = 0.9
ADAM_B2 = 0.999
ADAM_EPS = 1e-08
ADAM_WD = 0.01
ADAM_STEP = 10

VMEM_LIMIT = 56 * 1024 * 1024
ATT_BLK = 512
ATT_SUB = 128
ROW_BLK = 512
MM_TM, MM_TN, MM_TK = 1024, 1024, 2048
MM_TILE_BYTES = 40 * 1024 * 1024
SOFTMAX_SCALE = (NOPE + ROPE) ** -0.5
LOG2_E = 1.0 / math.log(2.0)
INV_SQRT2 = 1.0 / math.sqrt(2.0)
INV_SQRT_2PI = 1.0 / math.sqrt(2.0 * math.pi)


def _pcall(body, comm=False, **kw):
    return pl.pallas_call(body, **kw)


def _params(sem=None):
    return pltpu.CompilerParams(dimension_semantics=sem, vmem_limit_bytes=VMEM_LIMIT)


def _pick(dim, pref):
    if dim <= pref:
        return dim
    t = (pref // LANE) * LANE
    while t >= LANE:
        if dim % t == 0:
            return t
        t -= LANE
    return dim


def _gelu(x):
    return 0.5 * x * (1.0 + lax.erf(x * INV_SQRT2))


def _gelu_grad(x):
    return 0.5 * (1.0 + lax.erf(x * INV_SQRT2)) + x * jnp.exp(-0.5 * x * x) * INV_SQRT_2PI


def _sigmoid(x):
    return 1.0 / (1.0 + jnp.exp(-x))


def _dot_nt(a, b):
    return lax.dot_general(a, b, (((1,), (1,)), ((), ())), preferred_element_type=F32)


def _dot_tn(a, b):
    return lax.dot_general(a, b, (((0,), (0,)), ((), ())), preferred_element_type=F32)


def _mm(a, b, dims, out_dtype, name):
    if dims == "nn":
        (m, k), (k2, n) = a.shape, b.shape
    elif dims == "nt":
        (m, k), (n, k2) = a.shape, b.shape
    else:
        (k, m), (k2, n) = a.shape, b.shape
    assert k == k2, (a.shape, b.shape, dims)
    tm, tn = _pick(m, MM_TM), _pick(n, MM_TN)
    out_bytes = 2 * tm * tn * jnp.dtype(out_dtype).itemsize
    whole_k = 2 * (tm + tn) * k * a.dtype.itemsize + out_bytes <= MM_TILE_BYTES
    tk = k if whole_k else _pick(k, MM_TK)
    nk = k // tk

    def body(a_ref, b_ref, o_ref, *scratch):
        if dims == "nn":
            p = jnp.dot(a_ref[...], b_ref[...], preferred_element_type=F32)
        elif dims == "nt":
            p = _dot_nt(a_ref[...], b_ref[...])
        else:
            p = _dot_tn(a_ref[...], b_ref[...])
        if nk == 1:
            o_ref[...] = p.astype(o_ref.dtype)
            return
        acc_ref, = scratch
        kk = pl.program_id(2)

        @pl.when(kk == 0)
        def _():
            acc_ref[...] = p

        @pl.when(kk > 0)
        def _():
            acc_ref[...] += p

        @pl.when(kk == nk - 1)
        def _():
            o_ref[...] = acc_ref[...].astype(o_ref.dtype)

    if dims == "tn":
        a_spec = pl.BlockSpec((tk, tm), lambda i, j, kk: (kk, i))
    else:
        a_spec = pl.BlockSpec((tm, tk), lambda i, j, kk: (i, kk))
    if dims == "nt":
        b_spec = pl.BlockSpec((tn, tk), lambda i, j, kk: (j, kk))
    else:
        b_spec = pl.BlockSpec((tk, tn), lambda i, j, kk: (kk, j))
    return _pcall(
        body, name=name,
        grid=(m // tm, n // tn, nk),
        in_specs=[a_spec, b_spec],
        out_specs=pl.BlockSpec((tm, tn), lambda i, j, kk: (i, j)),
        out_shape=jax.ShapeDtypeStruct((m, n), out_dtype),
        scratch_shapes=[pltpu.VMEM((tm, tn), F32)] if nk > 1 else [],
        compiler_params=_params(("parallel", "parallel", "arbitrary")),
    )(a, b)


def _row_spec(ts, d):
    return pl.BlockSpec((ts, d), lambda i: (i, 0))


def _vec_spec(d):
    return pl.BlockSpec((1, d), lambda i: (0, 0))


def _pre_fwd(x, g, scale, shift, name):
    s, d = x.shape
    ts = _pick(s, ROW_BLK)

    def body(x_ref, g_ref, sc_ref, sh_ref, h_ref):
        xv = x_ref[...]
        r = lax.rsqrt(jnp.mean(xv * xv, axis=-1, keepdims=True) + NORM_EPS)
        h_ref[...] = ((xv * r * g_ref[...]) * (1.0 + sc_ref[...]) + sh_ref[...]).astype(BF16)

    return _pcall(
        body, name=name, grid=(s // ts,),
        in_specs=[_row_spec(ts, d), _vec_spec(d), _vec_spec(d), _vec_spec(d)],
        out_specs=_row_spec(ts, d),
        out_shape=jax.ShapeDtypeStruct((s, d), BF16),
        compiler_params=_params(("parallel",)),
    )(x, g, scale, shift)


def _post_fwd(x, out, gate, g, name):
    s, d = x.shape
    ts = _pick(s, ROW_BLK)

    def body(x_ref, o_ref, gate_ref, g_ref, y_ref):
        o = o_ref[...]
        r = lax.rsqrt(jnp.mean(o * o, axis=-1, keepdims=True) + NORM_EPS)
        y_ref[...] = x_ref[...] + gate_ref[...] * (o * r * g_ref[...])

    return _pcall(
        body, name=name, grid=(s // ts,),
        in_specs=[_row_spec(ts, d), _row_spec(ts, d), _vec_spec(d), _vec_spec(d)],
        out_specs=_row_spec(ts, d),
        out_shape=jax.ShapeDtypeStruct((s, d), F32),
        compiler_params=_params(("parallel",)),
    )(x, out, gate, g)


def _loss_head(xf, tgt, name):
    s, d = xf.shape
    ts = _pick(s, ROW_BLK)
    ns = s // ts

    def body(x_ref, t_ref, dx_ref, loss_ref, acc_ref):
        i = pl.program_id(0)

        @pl.when(i == 0)
        def _():
            acc_ref[...] = jnp.zeros_like(acc_ref)

        e = x_ref[...] - t_ref[...]
        dx_ref[...] = e * (1.0 / d)
        acc_ref[...] += jnp.sum(e * e, axis=0, keepdims=True)

        @pl.when(i == ns - 1)
        def _():
            tot = jnp.sum(acc_ref[...], axis=1, keepdims=True) * (0.5 / d)
            loss_ref[...] = jnp.broadcast_to(tot, loss_ref.shape)

    return _pcall(
        body, name=name, grid=(ns,),
        in_specs=[_row_spec(ts, d), _row_spec(ts, d)],
        out_specs=[_row_spec(ts, d), pl.BlockSpec((1, LANE), lambda i: (0, 0))],
        out_shape=[jax.ShapeDtypeStruct((s, d), F32), jax.ShapeDtypeStruct((1, LANE), F32)],
        scratch_shapes=[pltpu.VMEM((1, d), F32)],
        compiler_params=_params(("arbitrary",)),
    )(xf, tgt)


def _post_bwd(dxo, out, gate, g, name):
    s, d = dxo.shape
    ts = _pick(s, ROW_BLK)

    def body(dx_ref, o_ref, gate_ref, g_ref, do_ref, dgate_ref, dg_ref):
        i = pl.program_id(0)

        @pl.when(i == 0)
        def _():
            dgate_ref[...] = jnp.zeros_like(dgate_ref)
            dg_ref[...] = jnp.zeros_like(dg_ref)

        o = o_ref[...]
        dx = dx_ref[...]
        gv = g_ref[...]
        r = lax.rsqrt(jnp.mean(o * o, axis=-1, keepdims=True) + NORM_EPS)
        n = o * r
        dyn = dx * gate_ref[...]
        dgate_ref[...] += jnp.sum(dx * (n * gv), axis=0, keepdims=True)
        dg_ref[...] += jnp.sum(dyn * n, axis=0, keepdims=True)
        dn = dyn * gv
        do_ref[...] = (r * (dn - n * jnp.mean(dn * n, axis=-1, keepdims=True))).astype(BF16)

    return _pcall(
        body, name=name, grid=(s // ts,),
        in_specs=[_row_spec(ts, d), _row_spec(ts, d), _vec_spec(d), _vec_spec(d)],
        out_specs=[_row_spec(ts, d), _vec_spec(d), _vec_spec(d)],
        out_shape=[jax.ShapeDtypeStruct((s, d), BF16), jax.ShapeDtypeStruct((1, d), F32),
                   jax.ShapeDtypeStruct((1, d), F32)],
        compiler_params=_params(("arbitrary",)),
    )(dxo, out, gate, g)


def _pre_bwd(dh, x, dxo, g, scale, name):
    s, d = x.shape
    ts = _pick(s, ROW_BLK)

    def body(dh_ref, x_ref, dxo_ref, g_ref, sc_ref, dx_ref, dsh_ref, dsc_ref, dg_ref):
        i = pl.program_id(0)

        @pl.when(i == 0)
        def _():
            dsh_ref[...] = jnp.zeros_like(dsh_ref)
            dsc_ref[...] = jnp.zeros_like(dsc_ref)
            dg_ref[...] = jnp.zeros_like(dg_ref)

        xv = x_ref[...]
        dhv = dh_ref[...]
        gv = g_ref[...]
        one_sc = 1.0 + sc_ref[...]
        r = lax.rsqrt(jnp.mean(xv * xv, axis=-1, keepdims=True) + NORM_EPS)
        n = xv * r
        dsh_ref[...] += jnp.sum(dhv, axis=0, keepdims=True)
        dsc_ref[...] += jnp.sum(dhv * (n * gv), axis=0, keepdims=True)
        dng = dhv * one_sc
        dg_ref[...] += jnp.sum(dng * n, axis=0, keepdims=True)
        dn = dng * gv
        dx_ref[...] = dxo_ref[...] + r * (dn - n * jnp.mean(dn * n, axis=-1, keepdims=True))

    return _pcall(
        body, name=name, grid=(s // ts,),
        in_specs=[_row_spec(ts, d), _row_spec(ts, d), _row_spec(ts, d), _vec_spec(d), _vec_spec(d)],
        out_specs=[_row_spec(ts, d), _vec_spec(d), _vec_spec(d), _vec_spec(d)],
        out_shape=[jax.ShapeDtypeStruct((s, d), F32)] + [jax.ShapeDtypeStruct((1, d), F32)] * 3,
        compiler_params=_params(("arbitrary",)),
    )(dh, x, dxo, g, scale)


def _sgu_mask():
    t = lax.broadcasted_iota(jnp.int32, (SGU_BLOCK, SGU_BLOCK), 0) // CHUNK
    s = lax.broadcasted_iota(jnp.int32, (SGU_BLOCK, SGU_BLOCK), 1) // CHUNK
    return s <= t


def _sgu_norm(v_pre, g):
    e = v_pre.shape[-1]
    vg = _gelu(v_pre)
    mu = jnp.sum(vg, axis=-1, keepdims=True) * (1.0 / e)
    dlt = vg - mu
    var = jnp.sum(dlt * dlt, axis=-1, keepdims=True) * (1.0 / e)
    rstd = lax.rsqrt(var + NORM_EPS)
    vhat = dlt * rstd
    return vhat, rstd, (vhat * g).astype(BF16)


def _sgu_mid_fwd(uvz, norm_g, w_s, bias_full, name):
    s, e3 = uvz.shape
    e = e3 // 3
    gd = e // SGU_GROUPS
    nb = s // SGU_BLOCK

    def body(uvz_ref, g_ref, w_ref, b_ref, y_ref, wsc):
        @pl.when(pl.program_id(0) == 0)
        def _():
            msk = _sgu_mask()
            for gi in range(SGU_GROUPS):
                wsc[gi] = jnp.where(msk, w_ref[gi], 0.0).astype(BF16)

        _, _, vb = _sgu_norm(uvz_ref[:, e:2 * e], g_ref[...])
        for gi in range(SGU_GROUPS):
            lo = gi * gd
            vm = jnp.dot(wsc[gi], vb[:, lo:lo + gd], preferred_element_type=F32) + b_ref[:, lo:lo + gd]
            zg = uvz_ref[:, 2 * e + lo:2 * e + lo + gd]
            y_ref[:, lo:lo + gd] = (_gelu(uvz_ref[:, lo:lo + gd]) * vm * (zg * _sigmoid(zg))).astype(BF16)

    return _pcall(
        body, name=name, grid=(nb,),
        in_specs=[pl.BlockSpec((SGU_BLOCK, e3), lambda n: (n, 0)),
                  pl.BlockSpec((1, e), lambda n: (0, 0)),
                  pl.BlockSpec((SGU_GROUPS, SGU_BLOCK, SGU_BLOCK), lambda n: (0, 0, 0)),
                  pl.BlockSpec((SGU_BLOCK, e), lambda n: (0, 0))],
        out_specs=pl.BlockSpec((SGU_BLOCK, e), lambda n: (n, 0)),
        out_shape=jax.ShapeDtypeStruct((s, e), BF16),
        scratch_shapes=[pltpu.VMEM((SGU_GROUPS, SGU_BLOCK, SGU_BLOCK), BF16)],
        compiler_params=_params(("arbitrary",)),
    )(uvz, norm_g, w_s, bias_full)


def _sgu_mid_bwd(uvz, dy, norm_g, w_s, bias_full, name):
    s, e3 = uvz.shape
    e = e3 // 3
    gd = e // SGU_GROUPS
    nb = s // SGU_BLOCK

    def body(uvz_ref, dy_ref, g_ref, w_ref, b_ref, d_ref, dw_ref, db_ref, dg_ref, wsc, wtsc, dvh_sc, dbacc):
        n = pl.program_id(0)

        @pl.when(n == 0)
        def _():
            msk = _sgu_mask()
            for gi in range(SGU_GROUPS):
                wm = jnp.where(msk, w_ref[gi], 0.0)
                wsc[gi] = wm.astype(BF16)
                wtsc[gi] = wm.T.astype(BF16)
            dw_ref[...] = jnp.zeros_like(dw_ref)
            dg_ref[...] = jnp.zeros_like(dg_ref)
            dbacc[...] = jnp.zeros_like(dbacc)

        v_pre = uvz_ref[:, e:2 * e]
        gv = g_ref[...]
        vhat, rstd, vb = _sgu_norm(v_pre, gv)
        s1 = jnp.zeros((SGU_BLOCK, 1), F32)
        s2 = jnp.zeros((SGU_BLOCK, 1), F32)
        for gi in range(SGU_GROUPS):
            lo = gi * gd
            u_pre = uvz_ref[:, lo:lo + gd]
            zg = uvz_ref[:, 2 * e + lo:2 * e + lo + gd]
            dyg = dy_ref[:, lo:lo + gd]
            ug = _gelu(u_pre)
            sig = _sigmoid(zg)
            vbg = vb[:, lo:lo + gd]
            vhg = vhat[:, lo:lo + gd]
            vm = jnp.dot(wsc[gi], vbg, preferred_element_type=F32) + b_ref[:, lo:lo + gd]
            t = dyg * (zg * sig)
            d_ref[:, lo:lo + gd] = (t * vm * _gelu_grad(u_pre)).astype(BF16)
            dvm = t * ug
            d_ref[:, 2 * e + lo:2 * e + lo + gd] = (dyg * ug * vm * (sig * (1.0 + zg * (1.0 - sig)))).astype(BF16)
            dvm_b = dvm.astype(BF16)
            dv = jnp.dot(wtsc[gi], dvm_b, preferred_element_type=F32)
            dw_ref[gi] += _dot_nt(dvm_b, vbg)
            dbacc[:, lo:lo + gd] += dvm
            dg_ref[:, lo:lo + gd] += jnp.sum(dv * vhg, axis=0, keepdims=True)
            dvh = dv * gv[:, lo:lo + gd]
            dvh_sc[:, lo:lo + gd] = dvh
            s1 = s1 + jnp.sum(dvh, axis=-1, keepdims=True)
            s2 = s2 + jnp.sum(dvh * vhg, axis=-1, keepdims=True)
        dvg = rstd * (dvh_sc[...] - s1 * (1.0 / e) - vhat * (s2 * (1.0 / e)))
        d_ref[:, e:2 * e] = (dvg * _gelu_grad(v_pre)).astype(BF16)

        @pl.when(n == nb - 1)
        def _():
            msk = _sgu_mask()
            for gi in range(SGU_GROUPS):
                dw_ref[gi] = jnp.where(msk, dw_ref[gi], 0.0)
                db_ref[gi] = jnp.sum(dbacc[:, gi * gd:(gi + 1) * gd], axis=1, keepdims=True)

    return _pcall(
        body, name=name, grid=(nb,),
        in_specs=[pl.BlockSpec((SGU_BLOCK, e3), lambda n: (n, 0)),
                  pl.BlockSpec((SGU_BLOCK, e), lambda n: (n, 0)),
                  pl.BlockSpec((1, e), lambda n: (0, 0)),
                  pl.BlockSpec((SGU_GROUPS, SGU_BLOCK, SGU_BLOCK), lambda n: (0, 0, 0)),
                  pl.BlockSpec((SGU_BLOCK, e), lambda n: (0, 0))],
        out_specs=[pl.BlockSpec((SGU_BLOCK, e3), lambda n: (n, 0)),
                   pl.BlockSpec((SGU_GROUPS, SGU_BLOCK, SGU_BLOCK), lambda n: (0, 0, 0)),
                   pl.BlockSpec((SGU_GROUPS, SGU_BLOCK, 1), lambda n: (0, 0, 0)),
                   pl.BlockSpec((1, e), lambda n: (0, 0))],
        out_shape=[jax.ShapeDtypeStruct((s, e3), BF16),
                   jax.ShapeDtypeStruct((SGU_GROUPS, SGU_BLOCK, SGU_BLOCK), F32),
                   jax.ShapeDtypeStruct((SGU_GROUPS, SGU_BLOCK, 1), F32),
                   jax.ShapeDtypeStruct((1, e), F32)],
        scratch_shapes=[pltpu.VMEM((SGU_GROUPS, SGU_BLOCK, SGU_BLOCK), BF16),
                        pltpu.VMEM((SGU_GROUPS, SGU_BLOCK, SGU_BLOCK), BF16),
                        pltpu.VMEM((SGU_BLOCK, e), F32),
                        pltpu.VMEM((SGU_BLOCK, e), F32)],
        compiler_params=_params(("arbitrary",)),
    )(uvz, dy, norm_g, w_s, bias_full)


def _rope_tables(s):
    pos = jnp.arange(s, dtype=F32)
    inv_freq = ROPE_THETA ** (-jnp.arange(0, ROPE, 2, dtype=F32) / ROPE)
    ang = pos[:, None] * inv_freq[None, :]
    cos, sin = jnp.cos(ang), jnp.sin(ang)
    z32 = jnp.zeros((s, HALF), F32)
    z64 = jnp.zeros((s, ROPE), F32)
    ck = jnp.concatenate([cos, cos, z64], axis=1)
    s1k = jnp.concatenate([-sin, z32, z64], axis=1)
    s2k = jnp.concatenate([z32, sin, z64], axis=1)
    one = jnp.ones((s, NOPE), F32)
    zn = jnp.zeros((s, NOPE), F32)
    return (ck, s1k, s2k), (jnp.concatenate([one, ck], axis=1), jnp.concatenate([zn, s1k], axis=1),
                            jnp.concatenate([zn, s2k], axis=1))


def _rot(x, c, s1, s2):
    w = x.shape[-1]
    return x * c + pltpu.roll(x, w - HALF, 1) * s1 + pltpu.roll(x, HALF, 1) * s2


def _rms(cv, n_real):
    r = lax.rsqrt(jnp.sum(cv * cv, axis=-1, keepdims=True) * (1.0 / n_real) + NORM_EPS)
    return r, cv * r


def _mla_norm_fwd(proj, gq, gkv, tabs, name):
    s = proj.shape[0]
    ts = _pick(s, ROW_BLK)
    ck, s1k, s2k = tabs

    def body(cq_ref, ckv_ref, kr_ref, gq_ref, gkv_ref, c_ref, s1_ref, s2_ref, qn_ref, kvn_ref, kro_ref):
        _, nq = _rms(cq_ref[...], Q_RANK)
        qn_ref[...] = (nq * gq_ref[...]).astype(BF16)
        _, nkv = _rms(ckv_ref[...], KV_RANK)
        kvn_ref[...] = (nkv * gkv_ref[...]).astype(BF16)
        kro_ref[...] = _rot(kr_ref[...], c_ref[...], s1_ref[...], s2_ref[...]).astype(BF16)

    tab = pl.BlockSpec((ts, LANE), lambda i: (i, 0))
    return _pcall(
        body, name=name, grid=(s // ts,),
        in_specs=[pl.BlockSpec((ts, Q_RANK_PAD), lambda i: (i, 0)),
                  pl.BlockSpec((ts, KV_RANK), lambda i: (i, PROJ_CKV // KV_RANK)),
                  pl.BlockSpec((ts, LANE), lambda i: (i, PROJ_KR // LANE)),
                  _vec_spec(Q_RANK_PAD), _vec_spec(KV_RANK), tab, tab, tab],
        out_specs=[pl.BlockSpec((ts, Q_RANK_PAD), lambda i: (i, 0)),
                   pl.BlockSpec((ts, KV_RANK), lambda i: (i, 0)), tab],
        out_shape=[jax.ShapeDtypeStruct((s, Q_RANK_PAD), BF16), jax.ShapeDtypeStruct((s, KV_RANK), BF16),
                   jax.ShapeDtypeStruct((s, LANE), BF16)],
        compiler_params=_params(("parallel",)),
    )(proj, proj, proj, gq, gkv, ck, s1k, s2k)


def _rope_heads(q, tabs, sign, mult, out_dtype, name):
    s, w = q.shape
    ts = _pick(s, ROW_BLK)
    c, s1, s2 = tabs

    def body(q_ref, c_ref, s1_ref, s2_ref, o_ref):
        cv, s1v, s2v = mult * c_ref[...], (sign * mult) * s1_ref[...], (sign * mult) * s2_ref[...]
        for h in range(w // HEAD_PAD):
            lo = h * HEAD_PAD
            o_ref[:, lo:lo + HEAD_PAD] = _rot(q_ref[:, lo:lo + HEAD_PAD].astype(F32), cv, s1v, s2v).astype(out_dtype)

    blk = pl.BlockSpec((ts, w), lambda i: (i, 0))
    tab = pl.BlockSpec((ts, HEAD_PAD), lambda i: (i, 0))
    return _pcall(
        body, name=name, grid=(s // ts,),
        in_specs=[blk, tab, tab, tab], out_specs=blk,
        out_shape=jax.ShapeDtypeStruct((s, w), out_dtype),
        compiler_params=_params(("parallel",)),
    )(q, c, s1, s2)


def _transpose_bf16(t):
    return t.astype(F32).T.astype(BF16)


def _diag_mask(tb, transposed):
    r = lax.broadcasted_iota(jnp.int32, (tb, tb), 0) // CHUNK
    c = lax.broadcasted_iota(jnp.int32, (tb, tb), 1) // CHUNK
    return (r <= c) if transposed else (c <= r)


def _attn_fwd(q_cat, kv, kr, proj, name):
    s = q_cat.shape[0]
    tb = _pick(s, ATT_BLK)
    nb = s // tb
    zcol = PROJ_Z // V_DIM

    def body(q_ref, kn_ref, v_ref, kr_ref, z_ref, o_ref, y_ref, lse_ref, kt_sc, vx_sc, m_sc, acc_sc, sa_sc, sb_sc):
        qi = pl.program_id(1)

        @pl.when(qi == 0)
        def _():
            for b in range(nb):
                rows = slice(b * tb, (b + 1) * tb)
                kt_sc[b] = _transpose_bf16(jnp.concatenate([kn_ref[rows, :], kr_ref[rows, :]], axis=1))
                vx_sc[b] = jnp.concatenate([v_ref[rows, :], jnp.ones((tb, V_DIM), BF16)], axis=1)

        m_sc[...] = jnp.full_like(m_sc, -1e30)
        acc_sc[...] = jnp.zeros_like(acc_sc)
        sub = min(tb, ATT_SUB)

        def scores(ki, s_ref):
            s_ref[...] = jnp.dot(q_ref[...], kt_sc[ki], preferred_element_type=F32)

        def step(ki, s_ref, masked):
            for r in range(tb // sub):
                rs = slice(r * sub, (r + 1) * sub)
                sc = s_ref[rs, :]
                if masked:
                    sc = jnp.where(_diag_mask(tb, False)[rs, :], sc, -1e30)
                m_prev = m_sc[rs, :]
                m_new = jnp.maximum(m_prev, jnp.max(sc, axis=-1, keepdims=True))
                p = jnp.exp2(sc - m_new).astype(BF16)
                acc_sc[rs, :] = (jnp.exp2(m_prev - m_new) * acc_sc[rs, :]
                                 + jnp.dot(p, vx_sc[ki], preferred_element_type=F32))
                m_sc[rs, :] = m_new

        def pair(t, carry):
            scores(2 * t + 1, sb_sc)
            step(2 * t, sa_sc, False)
            scores(2 * t + 2, sa_sc)
            step(2 * t + 1, sb_sc, False)
            return carry

        scores(0, sa_sc)
        lax.fori_loop(0, qi // 2, pair, 0)

        @pl.when(qi % 2 == 1)
        def _():
            scores(qi, sb_sc)
            step(qi - 1, sa_sc, False)
            step(qi, sb_sc, True)

        @pl.when(qi % 2 == 0)
        def _():
            step(qi, sa_sc, True)

        l = acc_sc[:, V_DIM:V_DIM + 1]
        o = acc_sc[:, :V_DIM] / l
        z = z_ref[...]
        o_ref[...] = o.astype(BF16)
        y_ref[...] = (o * (z * _sigmoid(z))).astype(BF16)
        lse_cols = jnp.broadcast_to(m_sc[...] + jnp.log2(l), (tb, LANE))
        lse_ref[...] = lse_cols.T[0:1, :]

    oblk = pl.BlockSpec((tb, V_DIM), lambda h, qi: (qi, h))
    return _pcall(
        body, name=name, grid=(HEADS, nb),
        in_specs=[pl.BlockSpec((tb, HEAD_PAD), lambda h, qi: (qi, h)),
                  pl.BlockSpec((s, NOPE), lambda h, qi: (0, 2 * h)),
                  pl.BlockSpec((s, V_DIM), lambda h, qi: (0, 2 * h + 1)),
                  pl.BlockSpec((s, LANE), lambda h, qi: (0, 0)),
                  pl.BlockSpec((tb, V_DIM), lambda h, qi: (qi, zcol + h))],
        out_specs=[oblk, oblk, pl.BlockSpec((None, None, 1, tb), lambda h, qi: (h, qi, 0, 0))],
        out_shape=[jax.ShapeDtypeStruct((s, MLA_WIDTH), BF16), jax.ShapeDtypeStruct((s, MLA_WIDTH), BF16),
                   jax.ShapeDtypeStruct((HEADS, nb, 1, tb), F32)],
        scratch_shapes=[pltpu.VMEM((nb, HEAD_PAD, tb), BF16), pltpu.VMEM((nb, tb, HEAD_PAD), BF16),
                        pltpu.VMEM((tb, 1), F32), pltpu.VMEM((tb, HEAD_PAD), F32),
                        pltpu.VMEM((tb, tb), F32), pltpu.VMEM((tb, tb), F32)],
        compiler_params=_params(("parallel", "arbitrary")),
    )(q_cat, kv, kv, kr, proj)


def _attn_bwd(q_cat, kv, kr, do, o, lse, name):
    s = q_cat.shape[0]
    tb = _pick(s, ATT_BLK)
    nb = s // tb
    ln2 = math.log(2.0)

    def body(q_ref, do_ref, o_ref, lse_ref, kn_ref, v_ref, kr_ref, dq_ref, dkv_ref, dkr_ref,
             qt_sc, dot_sc, delta_sc, dqt_sc, dk_sc, dv_sc):
        ki = pl.program_id(1)

        @pl.when(ki == 0)
        def _():
            for b in range(nb):
                rows = slice(b * tb, (b + 1) * tb)
                qt_sc[b] = _transpose_bf16(q_ref[rows, :])
                do_t = do_ref[rows, :].astype(F32).T
                dot_sc[b] = do_t.astype(BF16)
                delta_sc[b] = jnp.sum(do_t * o_ref[rows, :].astype(F32).T, axis=0, keepdims=True)
            dqt_sc[...] = jnp.zeros_like(dqt_sc)

        k = jnp.concatenate([kn_ref[...], kr_ref[...]], axis=1)
        kt = _transpose_bf16(k)
        vb = v_ref[...]
        dk_sc[...] = jnp.zeros_like(dk_sc)
        dv_sc[...] = jnp.zeros_like(dv_sc)

        def step(qi, masked):
            rows = pl.ds(pl.multiple_of(qi * tb, tb), tb)
            sc_t = jnp.dot(k, qt_sc[qi], preferred_element_type=F32)
            if masked:
                sc_t = jnp.where(_diag_mask(tb, True), sc_t, -1e30)
            p_t = jnp.exp2(sc_t - lse_ref[qi])
            dp_t = jnp.dot(vb, dot_sc[qi], preferred_element_type=F32)
            ds_t = (p_t * (dp_t - delta_sc[qi])).astype(BF16)
            dv_sc[...] += jnp.dot(p_t.astype(BF16), do_ref[rows, :], preferred_element_type=F32)
            dk_sc[...] += jnp.dot(ds_t, q_ref[rows, :], preferred_element_type=F32)
            dqt_sc[qi] += jnp.dot(kt, ds_t, preferred_element_type=F32)

        step(ki, True)

        def loop_body(qi, carry):
            step(qi, False)
            return carry

        lax.fori_loop(ki + 1, nb, loop_body, 0)

        dkv_ref[:, :NOPE] = (dk_sc[:, :NOPE] * ln2).astype(BF16)
        dkv_ref[:, NOPE:] = dv_sc[...].astype(BF16)
        dkr_ref[...] = dk_sc[:, NOPE:] * ln2

        @pl.when(ki == nb - 1)
        def _():
            for b in range(nb):
                dq_ref[b * tb:(b + 1) * tb, :] = dqt_sc[b].T

    return _pcall(
        body, name=name, grid=(HEADS, nb),
        in_specs=[pl.BlockSpec((s, HEAD_PAD), lambda h, ki: (0, h)),
                  pl.BlockSpec((s, V_DIM), lambda h, ki: (0, h)),
                  pl.BlockSpec((s, V_DIM), lambda h, ki: (0, h)),
                  pl.BlockSpec((None, nb, 1, tb), lambda h, ki: (h, 0, 0, 0)),
                  pl.BlockSpec((tb, NOPE), lambda h, ki: (ki, 2 * h)),
                  pl.BlockSpec((tb, V_DIM), lambda h, ki: (ki, 2 * h + 1)),
                  pl.BlockSpec((tb, LANE), lambda h, ki: (ki, 0))],
        out_specs=[pl.BlockSpec((s, HEAD_PAD), lambda h, ki: (0, h)),
                   pl.BlockSpec((tb, HEAD_PAD), lambda h, ki: (ki, h)),
                   pl.BlockSpec((None, tb, LANE), lambda h, ki: (h, ki, 0))],
        out_shape=[jax.ShapeDtypeStruct((s, HEADS * HEAD_PAD), F32),
                   jax.ShapeDtypeStruct((s, HEADS * HEAD_PAD), BF16),
                   jax.ShapeDtypeStruct((HEADS, s, LANE), F32)],
        scratch_shapes=[pltpu.VMEM((nb, HEAD_PAD, tb), BF16), pltpu.VMEM((nb, V_DIM, tb), BF16),
                        pltpu.VMEM((nb, 1, tb), F32), pltpu.VMEM((nb, HEAD_PAD, tb), F32),
                        pltpu.VMEM((tb, HEAD_PAD), F32), pltpu.VMEM((tb, V_DIM), F32)],
        compiler_params=_params(("parallel", "arbitrary")),
    )(q_cat, do, o, lse, kv, kv, kr)


def _mla_gate_bwd(dy, o, proj, name):
    s = dy.shape[0]
    ts = _pick(s, ROW_BLK)
    zcol = PROJ_Z // V_DIM

    def body(dy_ref, o_ref, p_ref, do_ref, dz_ref):
        z = p_ref[:, PROJ_Z:]
        dyv = dy_ref[...]
        sig = _sigmoid(z)
        do_ref[...] = (dyv * (z * sig)).astype(BF16)
        dz_ref[...] = (dyv * o_ref[...].astype(F32) * (sig * (1.0 + z * (1.0 - sig)))).astype(BF16)

    blk = pl.BlockSpec((ts, MLA_WIDTH), lambda i: (i, 0))
    return _pcall(
        body, name=name, grid=(s // ts,),
        in_specs=[blk, blk, pl.BlockSpec((ts, PROJ_W), lambda i: (i, 0))],
        out_specs=[blk, blk],
        out_shape=[jax.ShapeDtypeStruct((s, MLA_WIDTH), BF16), jax.ShapeDtypeStruct((s, MLA_WIDTH), BF16)],
        compiler_params=_params(("parallel",)),
    )(dy, o, proj)


def _mla_norm_bwd(dqn, dkvn, dkr_heads, proj, gq, gkv, tabs, name):
    s = proj.shape[0]
    ts = _pick(s, ROW_BLK)
    ck, s1k, s2k = tabs

    def rms_bwd(cv, dn_in, g, n_real):
        r, n = _rms(cv, n_real)
        dg = jnp.sum(dn_in * n, axis=0, keepdims=True)
        dn = dn_in * g
        dc = r * (dn - n * (jnp.sum(dn * n, axis=-1, keepdims=True) * (1.0 / n_real)))
        return dc, dg

    def body(dqn_ref, dkvn_ref, dkr_ref, cq_ref, ckv_ref, gq_ref, gkv_ref, c_ref, s1_ref, s2_ref,
             dp_ref, dgq_ref, dgkv_ref):
        @pl.when(pl.program_id(0) == 0)
        def _():
            dgq_ref[...] = jnp.zeros_like(dgq_ref)
            dgkv_ref[...] = jnp.zeros_like(dgkv_ref)

        dcq, dgq = rms_bwd(cq_ref[...], dqn_ref[...], gq_ref[...], Q_RANK)
        dckv, dgkv = rms_bwd(ckv_ref[...], dkvn_ref[...], gkv_ref[...], KV_RANK)
        dgq_ref[...] += dgq
        dgkv_ref[...] += dgkv
        dkr = dkr_ref[0]
        for h in range(1, HEADS):
            dkr = dkr + dkr_ref[h]
        dp_ref[:, PROJ_CQ:PROJ_CKV] = dcq.astype(BF16)
        dp_ref[:, PROJ_CKV:PROJ_KR] = dckv.astype(BF16)
        dp_ref[:, PROJ_KR:PROJ_Z] = _rot(dkr, c_ref[...], -s1_ref[...], -s2_ref[...]).astype(BF16)

    tab = pl.BlockSpec((ts, LANE), lambda i: (i, 0))
    return _pcall(
        body, name=name, grid=(s // ts,),
        in_specs=[pl.BlockSpec((ts, Q_RANK_PAD), lambda i: (i, 0)),
                  pl.BlockSpec((ts, KV_RANK), lambda i: (i, 0)),
                  pl.BlockSpec((HEADS, ts, LANE), lambda i: (0, i, 0)),
                  pl.BlockSpec((ts, Q_RANK_PAD), lambda i: (i, 0)),
                  pl.BlockSpec((ts, KV_RANK), lambda i: (i, PROJ_CKV // KV_RANK)),
                  _vec_spec(Q_RANK_PAD), _vec_spec(KV_RANK), tab, tab, tab],
        out_specs=[pl.BlockSpec((ts, PROJ_Z), lambda i: (i, 0)), _vec_spec(Q_RANK_PAD), _vec_spec(KV_RANK)],
        out_shape=[jax.ShapeDtypeStruct((s, PROJ_Z), BF16), jax.ShapeDtypeStruct((1, Q_RANK_PAD), F32),
                   jax.ShapeDtypeStruct((1, KV_RANK), F32)],
        compiler_params=_params(("arbitrary",)),
    )(dqn, dkvn, dkr_heads, proj, proj, gq, gkv, ck, s1k, s2k)


def _ada_mod(cond_raw, ada_w, bias_my, name):
    nl, d, ncol = ada_w.shape
    tk = _pick(d, 512)
    nk = d // tk

    def body(c_ref, w_ref, b_ref, o_ref, acc_ref):
        kk = pl.program_id(1)

        @pl.when(kk == 0)
        def _():
            acc_ref[...] = jnp.zeros_like(acc_ref)

        cv = c_ref[...]
        cond = (cv * _sigmoid(cv)).astype(BF16)
        acc_ref[...] += jnp.dot(cond, w_ref[...].astype(BF16), preferred_element_type=F32)

        @pl.when(kk == nk - 1)
        def _():
            o_ref[...] = acc_ref[...] + b_ref[...]

    return _pcall(
        body, name=name, grid=(nl, nk),
        in_specs=[pl.BlockSpec((N_DEV, tk), lambda l, kk: (0, kk)),
                  pl.BlockSpec((None, tk, ncol), lambda l, kk: (l, kk, 0)),
                  pl.BlockSpec((None, 1, ncol), lambda l, kk: (l, 0, 0))],
        out_specs=pl.BlockSpec((None, N_DEV, ncol), lambda l, kk: (l, 0, 0)),
        out_shape=jax.ShapeDtypeStruct((nl, N_DEV, ncol), F32),
        scratch_shapes=[pltpu.VMEM((N_DEV, ncol), F32)],
        compiler_params=_params(("parallel", "arbitrary")),
    )(cond_raw, ada_w, bias_my.reshape(nl, 1, ncol))


def _adam(w, g, m, v):
    m = ADAM_B1 * m + (1.0 - ADAM_B1) * g
    v = ADAM_B2 * v + (1.0 - ADAM_B2) * (g * g)
    m_hat = m / (1.0 - ADAM_B1 ** ADAM_STEP)
    v_hat = v / (1.0 - ADAM_B2 ** ADAM_STEP)
    delta = -ADAM_LR * (m_hat / (jnp.sqrt(v_hat) + ADAM_EPS) + ADAM_WD * w)
    return delta, m, v


def _ada_bwd_adam(cond_t, dmod_cols, w, m, v, name):
    nl, d, ncol = w.shape
    tk = _pick(d, 512)

    def body(c_ref, dm_ref, w_ref, m_ref, v_ref, g_ref, d_ref, mo_ref, vo_ref):
        cv = c_ref[...]
        cond = (cv * _sigmoid(cv)).astype(BF16)
        g = jnp.dot(cond, dm_ref[...].astype(BF16), preferred_element_type=F32)
        delta, m2, v2 = _adam(w_ref[...], g, m_ref[...], v_ref[...])
        g_ref[...] = g
        d_ref[...] = delta
        mo_ref[...] = m2
        vo_ref[...] = v2

    blk = pl.BlockSpec((None, tk, ncol), lambda l, kk: (l, kk, 0))
    shp = jax.ShapeDtypeStruct((nl, d, ncol), F32)
    return _pcall(
        body, name=name, grid=(nl, d // tk),
        in_specs=[pl.BlockSpec((tk, N_DEV), lambda l, kk: (kk, 0)),
                  pl.BlockSpec((None, N_DEV, ncol), lambda l, kk: (l, 0, 0)), blk, blk, blk],
        out_specs=[blk, blk, blk, blk], out_shape=[shp, shp, shp, shp],
        compiler_params=_params(("parallel", "parallel")),
    )(cond_t, dmod_cols, w, m, v)


def _adam_reduce(recv0, recv1, w, m, v, name):
    nl, r, c = w.shape
    tr = _pick(r, 128) if r % 128 == 0 else r
    tc = _pick(c, 1024)
    n0, n1 = recv0.shape[0] // r, recv1.shape[0] // r

    def body(r0_ref, r1_ref, w_ref, m_ref, v_ref, g_ref, d_ref, mo_ref, vo_ref):
        l = pl.program_id(0)

        def run(rr):
            g = rr[0].astype(F32)
            for sidx in range(1, rr.shape[0]):
                g = g + rr[sidx].astype(F32)
            delta, m2, v2 = _adam(w_ref[...], g, m_ref[...], v_ref[...])
            g_ref[...] = g
            d_ref[...] = delta
            mo_ref[...] = m2
            vo_ref[...] = v2

        @pl.when(l == 0)
        def _():
            run(r0_ref)

        @pl.when(l == 1)
        def _():
            run(r1_ref)

    def rblk(n, layer):
        return pl.BlockSpec((n, tr, tc), lambda l, i, j: (0, jnp.where(l == layer, i, 0), jnp.where(l == layer, j, 0)))

    blk = pl.BlockSpec((None, tr, tc), lambda l, i, j: (l, i, j))
    shp = jax.ShapeDtypeStruct((nl, r, c), F32)
    return _pcall(
        body, name=name, grid=(nl, r // tr, c // tc),
        in_specs=[rblk(n0, 0), rblk(n1, 1), blk, blk, blk],
        out_specs=[blk, blk, blk, blk], out_shape=[shp, shp, shp, shp],
        compiler_params=_params(("arbitrary", "parallel", "parallel")),
    )(recv0.reshape(n0, r, c), recv1.reshape(n1, r, c), w, m, v)


def _adam_small(gathered, w, m, v, name):
    r = w.shape[0]
    tr = _pick(r, 512) if r % 512 == 0 else r

    def body(p_ref, w_ref, m_ref, v_ref, g_ref, d_ref, mo_ref, vo_ref):
        g = p_ref[0]
        for sidx in range(1, N_DEV):
            g = g + p_ref[sidx]
        delta, m2, v2 = _adam(w_ref[...], g, m_ref[...], v_ref[...])
        g_ref[...] = g
        d_ref[...] = delta
        mo_ref[...] = m2
        vo_ref[...] = v2

    blk = pl.BlockSpec((tr, LANE), lambda i: (i, 0))
    shp = jax.ShapeDtypeStruct((r, LANE), F32)
    return _pcall(
        body, name=name, grid=(r // tr,),
        in_specs=[pl.BlockSpec((N_DEV, tr, LANE), lambda i: (0, i, 0)), blk, blk, blk],
        out_specs=[blk, blk, blk, blk], out_shape=[shp, shp, shp, shp],
        compiler_params=_params(("parallel",)),
    )(gathered, w, m, v)


def _my_place():
    x, y, c = lax.axis_index("x"), lax.axis_index("y"), lax.axis_index("c")
    return x, y, c, 4 * x + 2 * y + c


def _peer(x, y, c, k):
    px = 1 - x if (k >> 2) & 1 else x
    py = 1 - y if (k >> 1) & 1 else y
    pc = 1 - c if k & 1 else c
    return (px, py, pc), 4 * px + 2 * py + pc


def _slab(ref, shape, kind, p):
    r, cd = shape
    if kind == "row":
        return ref.at[pl.ds(pl.multiple_of(p * r, SUBLANE), r), :]
    return ref.at[:, pl.ds(pl.multiple_of(p * cd, LANE), cd)]


def _exchange_layout(arrays, kinds, gather):
    shard_shapes, dst_kinds, out_shapes = [], [], []
    for a, kind in zip(arrays, kinds):
        r, cd = a.shape
        if gather:
            shard, dst_kind = (r, cd), kind
        else:
            shard, dst_kind = ((r // N_DEV, cd) if kind == "row" else (r, cd // N_DEV)), "row"
        shard_shapes.append(shard)
        dst_kinds.append(dst_kind)
        full = (shard[0] * N_DEV, shard[1]) if dst_kind == "row" else (shard[0], shard[1] * N_DEV)
        out_shapes.append(jax.ShapeDtypeStruct(full, a.dtype))
    return shard_shapes, dst_kinds, out_shapes


def _exchange_copies(ins, outs, send_sems, recv_sems, sem_of, layout, kinds, gather):
    shard_shapes, dst_kinds, _ = layout
    x, y, c, me = _my_place()

    def src_for(a, p):
        return ins[a] if gather else _slab(ins[a], shard_shapes[a], kinds[a], p)

    def dst_slot(a, p):
        return _slab(outs[a], shard_shapes[a], dst_kinds[a], p)

    def local(a, sem):
        return pltpu.make_async_copy(src_for(a, me), dst_slot(a, me), sem)

    def remote(a, k, slot):
        peer, pidx = _peer(x, y, c, k)
        return pltpu.make_async_remote_copy(
            src_ref=src_for(a, pidx), dst_ref=dst_slot(a, me if slot == "mine" else pidx),
            send_sem=send_sems.at[sem_of(a, k)], recv_sem=recv_sems.at[sem_of(a, k)],
            device_id=peer, device_id_type=MESH)

    return local, remote


def _place_own(src, src_kind, slab_shape, dst_kind, full, name, index=None):
    r, cd = slab_shape
    tr = _pick(r, 512)
    nr = r // tr
    me = _my_place()[3] if index is None else index

    def body(me_ref, s_ref, o_ref):
        o_ref[...] = s_ref[...].astype(o_ref.dtype)

    def where(kind):
        if kind is None:
            return lambda i, me_ref: (i, 0)
        if kind == "row":
            return lambda i, me_ref: (me_ref[0] * nr + i, 0)
        return lambda i, me_ref: (i, me_ref[0])

    return _pcall(
        body, name=name,
        grid_spec=pltpu.PrefetchScalarGridSpec(
            num_scalar_prefetch=1, grid=(nr,),
            in_specs=[pl.BlockSpec((tr, cd), where(src_kind))],
            out_specs=pl.BlockSpec((tr, cd), where(dst_kind))),
        out_shape=jax.ShapeDtypeStruct(full.shape, full.dtype),
        compiler_params=_params(("arbitrary",)),
    )(jnp.reshape(me, (1,)).astype(jnp.int32), src)


def _landing_with_own_slab(arrays, kinds, gather, layout, name=None, order_after=None):
    _, _, _, me = _my_place()
    index = None
    if order_after is not None:
        first = order_after.reshape(-1)[0].astype(jnp.int32)
        index = me + jnp.minimum(jnp.maximum(first, 0), 0)
    lands = []
    for a in range(len(arrays)):
        (r, cd), dst_kind, full = layout[0][a], layout[1][a], layout[2][a]
        if name is not None:
            lands.append(_place_own(arrays[a], None if gather else kinds[a], (r, cd), dst_kind, full, name,
                                    index=index))
            continue
        if gather:
            piece = arrays[a]
        elif kinds[a] == "row":
            piece = lax.dynamic_slice_in_dim(arrays[a], me * r, r, axis=0)
        else:
            piece = lax.dynamic_slice_in_dim(arrays[a], me * cd, cd, axis=1)
        at = (me * r, 0) if dst_kind == "row" else (0, me * cd)
        lands.append(lax.dynamic_update_slice(lax.empty(full.shape, full.dtype), piece, at))
    return lands


def _exchange(arrays, kinds, gather, name, after=None):
    n = len(arrays)
    n_extra = 0 if after is None else 1
    layout = _exchange_layout(arrays, kinds, gather)
    lands = _landing_with_own_slab(arrays, kinds, gather, layout)

    def body(*refs):
        ins, outs = refs[:n], refs[2 * n + n_extra:3 * n + n_extra]
        send_sems, recv_sems = refs[3 * n + n_extra:]
        _, remote = _exchange_copies(ins, outs, send_sems, recv_sems,
                                     lambda a, k: a * (N_DEV - 1) + k - 1, layout, kinds, gather)
        for a in range(n):
            for k in range(1, N_DEV):
                remote(a, k, "mine").start()
        for a in range(n):
            for k in range(1, N_DEV):
                arrival = remote(a, k, "theirs")
                arrival.wait_send()
                arrival.wait_recv()

    anyspec = pl.BlockSpec(memory_space=pl.ANY)
    outs = _pcall(
        body, comm=True, name=name,
        in_specs=[anyspec] * (2 * n + n_extra), out_specs=[anyspec] * n, out_shape=layout[2],
        input_output_aliases={n + a: a for a in range(n)},
        scratch_shapes=[pltpu.SemaphoreType.DMA((n * (N_DEV - 1),)), pltpu.SemaphoreType.DMA((n * (N_DEV - 1),))],
    )(*arrays, *lands, *([] if after is None else [after]))
    return list(outs)


HBM_SPEC = pl.BlockSpec(memory_space=pltpu.HBM)
SEM_SPEC = pl.BlockSpec(memory_space=pltpu.SEMAPHORE)
ANY_SPEC = pl.BlockSpec(memory_space=pl.ANY)
DATAFLOW = pltpu.SideEffectType.DATAFLOW_SIDE_EFFECTING


def _exchange_start(arrays, kinds, gather, name, after, carry=()):
    n, nc = len(arrays), len(carry)
    layout = _exchange_layout(arrays, kinds, gather)
    lands = _landing_with_own_slab(arrays, kinds, gather, layout, "place_own")

    def body(*refs):
        ins, outs = refs[:n], refs[n:2 * n]
        send_sems, recv_sems = refs[2 * n + nc + 1], refs[2 * n + nc + 2]
        token = refs[2 * n + nc + 3 + 2 * n + nc]
        _, remote = _exchange_copies(ins, outs, send_sems, recv_sems, lambda a, k: a, layout, kinds, gather)
        for a in range(n):
            for k in range(1, N_DEV):
                remote(a, k, "mine").start()
        token[...] = jnp.zeros_like(token)

    passed = list(arrays) + lands + list(carry)
    res = pl.pallas_call(
        body, name=name,
        out_shape=(pltpu.SemaphoreType.DMA((n,)), pltpu.SemaphoreType.DMA((n,)),
                   *[pltpu.HBM(t.shape, t.dtype) for t in passed], jax.ShapeDtypeStruct((SUBLANE, LANE), F32)),
        in_specs=[HBM_SPEC] * (2 * n + nc) + [ANY_SPEC],
        out_specs=(SEM_SPEC, SEM_SPEC, *([HBM_SPEC] * (2 * n + nc)), pl.BlockSpec(memory_space=pltpu.VMEM)),
        input_output_aliases={i: 2 + i for i in range(2 * n + nc)},
        compiler_params=pltpu.CompilerParams(has_side_effects=DATAFLOW),
    )(*[pltpu.with_memory_space_constraint(t, pltpu.HBM) for t in passed], after)
    handle = (res[0], res[1], list(res[2:2 + n]), list(res[2 + n:2 + 2 * n]), tuple(kinds), gather)
    return handle, res[-1], list(res[2 + 2 * n:2 + 2 * n + nc])


def _exchange_wait(handle, name, after):
    send_sems, recv_sems, ins_thru, lands_thru, kinds, gather = handle
    n = len(ins_thru)
    layout = _exchange_layout(ins_thru, kinds, gather)

    def body(*refs):
        ins, outs = refs[:n], refs[n:2 * n]
        s_sems, r_sems = refs[2 * n], refs[2 * n + 1]
        _, remote = _exchange_copies(ins, outs, s_sems, r_sems, lambda a, k: a, layout, kinds, gather)
        for a in range(n):
            for k in range(1, N_DEV):
                arrival = remote(a, k, "theirs")
                arrival.wait_send()
                arrival.wait_recv()

    res = pl.pallas_call(
        body, name=name,
        out_shape=[pltpu.HBM(t.shape, t.dtype) for t in ins_thru + lands_thru],
        in_specs=[HBM_SPEC] * (2 * n) + [SEM_SPEC, SEM_SPEC, ANY_SPEC],
        out_specs=[HBM_SPEC] * (2 * n),
        input_output_aliases={i: i for i in range(2 * n)},
        compiler_params=pltpu.CompilerParams(has_side_effects=DATAFLOW),
    )(*ins_thru, *lands_thru, send_sems, recv_sems, after)
    return list(res[n:2 * n])


STAGE1_KS = (1, 2, 4, 6)
FORWARD_KS = (2, 4, 6)


def _gather2_copies(lands, shard_shapes, kinds):
    x, y, c, me = _my_place()

    def slab(a, p):
        return _slab(lands[a], shard_shapes[a], kinds[a], p)

    def stage1(a, k, sems, arriving):
        peer, pidx = _peer(x, y, c, k)
        s = slab(a, pidx if arriving else me)
        return pltpu.make_async_remote_copy(src_ref=s, dst_ref=s, send_sem=sems[0].at[a], recv_sem=sems[1].at[a],
                                            device_id=peer, device_id_type=MESH)

    def stage2(a, k, sems, arriving):
        sib, _ = _peer(x, y, c, 1)
        _, mine = _peer(x, y, c, k)
        _, theirs = _peer(x, y, 1 - c, k)
        s = slab(a, theirs if arriving else mine)
        return pltpu.make_async_remote_copy(src_ref=s, dst_ref=s, send_sem=sems[0].at[a], recv_sem=sems[1].at[a],
                                            device_id=sib, device_id_type=MESH)

    return stage1, stage2


def _gather2_call(lands, sems_in, name, after, make_body, returns_sems):
    n = len(lands)
    n_in = len(sems_in)

    def body(*refs):
        land_refs = refs[:n]
        in_sems = refs[n:n + n_in]
        rest = refs[n + n_in + 1:]
        out_sems = rest[:2] if returns_sems else ()
        make_body(land_refs, in_sems, out_sems)
        if returns_sems:
            token = rest[2 + n]
            token[...] = jnp.zeros_like(token)

    sem_shapes = (pltpu.SemaphoreType.DMA((n,)), pltpu.SemaphoreType.DMA((n,))) if returns_sems else ()
    tok_shape = (jax.ShapeDtypeStruct((SUBLANE, LANE), F32),) if returns_sems else ()
    n_sem_out = len(sem_shapes)
    res = pl.pallas_call(
        body, name=name,
        out_shape=(*sem_shapes, *[pltpu.HBM(t.shape, t.dtype) for t in lands], *tok_shape),
        in_specs=[HBM_SPEC] * n + [SEM_SPEC] * n_in + [ANY_SPEC],
        out_specs=(*([SEM_SPEC] * n_sem_out), *([HBM_SPEC] * n),
                   *([pl.BlockSpec(memory_space=pltpu.VMEM)] if returns_sems else [])),
        input_output_aliases={i: n_sem_out + i for i in range(n)},
        compiler_params=pltpu.CompilerParams(has_side_effects=DATAFLOW),
    )(*[pltpu.with_memory_space_constraint(t, pltpu.HBM) for t in lands], *sems_in, after)
    sems_out = tuple(res[:n_sem_out])
    lands_thru = list(res[n_sem_out:n_sem_out + n])
    return sems_out, lands_thru, (res[-1] if returns_sems else None)


def _gather2_start(shards, kinds, name, after):
    n = len(shards)
    shard_shapes, dst_kinds, fulls = _exchange_layout(shards, kinds, True)
    layout = (shard_shapes, dst_kinds, [jax.ShapeDtypeStruct(f.shape, BF16) for f in fulls])
    lands = _landing_with_own_slab(shards, kinds, True, layout, "place_own", order_after=after)

    def make_body(land_refs, in_sems, out_sems):
        stage1, _ = _gather2_copies(land_refs, layout[0], kinds)
        for a in range(n):
            for k in STAGE1_KS:
                stage1(a, k, out_sems, False).start()

    sems, lands, token = _gather2_call(lands, (), name, after, make_body, True)
    return (sems, lands, layout[0], tuple(kinds)), token


def _gather2_forward(handle, name, after, carry=()):
    sems1, lands, shard_shapes, kinds = handle
    n = len(lands)

    def make_body(land_refs, in_sems, out_sems):
        stage1, stage2 = _gather2_copies(land_refs, shard_shapes, kinds)
        for a in range(n):
            for k in STAGE1_KS:
                arrival = stage1(a, k, in_sems, True)
                arrival.wait_send()
                arrival.wait_recv()
        for a in range(n):
            for k in FORWARD_KS:
                stage2(a, k, out_sems, False).start()

    sems2, passed, token = _gather2_call(list(lands) + list(carry), sems1, name, after, make_body, True)
    return (sems2, passed[:n], shard_shapes, kinds), token, passed[n:]


def _gather2_wait(handle, name, after):
    sems2, lands, shard_shapes, kinds = handle
    n = len(lands)

    def make_body(land_refs, in_sems, out_sems):
        _, stage2 = _gather2_copies(land_refs, shard_shapes, kinds)
        for a in range(n):
            for k in FORWARD_KS:
                arrival = stage2(a, k, in_sems, True)
                arrival.wait_send()
                arrival.wait_recv()

    _, lands, _ = _gather2_call(lands, sems2, name, after, make_body, False)
    return lands


N_CHIP = N_DEV // 2


def _scatter2_pair(g, name):
    r, c8 = g.shape
    cd = c8 // N_DEV

    def body(g_ref, o_ref, send_sems, recv_sems):
        x, y, c, _ = _my_place()
        sib, _ = _peer(x, y, c, 1)

        def copy(ch):
            theirs = 2 * ch + (1 - c)
            return pltpu.make_async_remote_copy(
                src_ref=_slab(g_ref, (r, cd), "col", theirs), dst_ref=_slab(o_ref, (r, cd), "row", ch),
                send_sem=send_sems.at[ch], recv_sem=recv_sems.at[ch], device_id=sib, device_id_type=MESH)

        for ch in range(N_CHIP):
            copy(ch).start()
        for ch in range(N_CHIP):
            copy(ch).wait_send()
            copy(ch).wait_recv()

    return _pcall(
        body, comm=True, name=name,
        in_specs=[ANY_SPEC], out_specs=ANY_SPEC, out_shape=jax.ShapeDtypeStruct((N_CHIP * r, cd), g.dtype),
        scratch_shapes=[pltpu.SemaphoreType.DMA((N_CHIP,)), pltpu.SemaphoreType.DMA((N_CHIP,))],
    )(g)


def _scatter2_add(g, from_sibling, name):
    r, c8 = g.shape
    cd = c8 // N_DEV
    tr = _pick(r, 512)
    nr = r // tr
    _, _, core, _ = _my_place()

    def body(c_ref, g_ref, s_ref, o_ref):
        o_ref[...] = (g_ref[...].astype(F32) + s_ref[...].astype(F32)).astype(o_ref.dtype)

    return _pcall(
        body, name=name,
        grid_spec=pltpu.PrefetchScalarGridSpec(
            num_scalar_prefetch=1, grid=(N_CHIP, nr),
            in_specs=[pl.BlockSpec((tr, cd), lambda ch, i, c_ref: (i, 2 * ch + c_ref[0])),
                      pl.BlockSpec((tr, cd), lambda ch, i, c_ref: (ch * nr + i, 0))],
            out_specs=pl.BlockSpec((tr, cd), lambda ch, i, c_ref: (ch * nr + i, 0))),
        out_shape=jax.ShapeDtypeStruct((N_CHIP * r, cd), g.dtype),
        compiler_params=_params(("arbitrary", "arbitrary")),
    )(jnp.reshape(core, (1,)).astype(jnp.int32), g, from_sibling)


def _scatter2_copies(refs, r):
    part_ref, recv_ref = refs
    x, y, c, _ = _my_place()
    my_chip = 2 * x + y

    def rows(ref, ch):
        return ref.at[pl.ds(pl.multiple_of(ch * r, SUBLANE), r), :]

    def copy(k, sems, arriving):
        peer, _ = _peer(x, y, c, k)
        peer_chip = 2 * peer[0] + peer[1]
        return pltpu.make_async_remote_copy(
            src_ref=rows(part_ref, peer_chip), dst_ref=rows(recv_ref, peer_chip if arriving else my_chip),
            send_sem=sems[0].at[0], recv_sem=sems[1].at[0], device_id=peer, device_id_type=MESH)

    return copy


def _scatter2_start(g, name, after, carry=()):
    r = g.shape[0]
    x, y, _, _ = _my_place()
    from_sibling = _scatter2_pair(g, name + "_pair")
    partial = _scatter2_add(g, from_sibling, name + "_add")
    cd = partial.shape[1]
    recv = _place_own(partial, "row", (r, cd), "row", jax.ShapeDtypeStruct(partial.shape, partial.dtype),
                      "place_own", index=2 * x + y)

    def make_body(land_refs, in_sems, out_sems):
        copy = _scatter2_copies(land_refs[:2], r)
        for k in FORWARD_KS:
            copy(k, out_sems, False).start()

    sems, lands, token = _gather2_call([partial, recv] + list(carry), (), name, after, make_body, True)
    return (sems, lands[:2], r), token, lands[2:]


def _scatter2_wait(handle, name, after):
    sems, lands, r = handle

    def make_body(land_refs, in_sems, out_sems):
        copy = _scatter2_copies(land_refs, r)
        for k in FORWARD_KS:
            arrival = copy(k, in_sems, True)
            arrival.wait_send()
            arrival.wait_recv()

    _, lands, _ = _gather2_call(lands, sems, name, after, make_body, False)
    return lands[1]


def _pad_mla_w_in(w):
    d = w.shape[0]
    z = lambda n: jnp.zeros((d, n), w.dtype)
    o1, o2, o3 = Q_RANK, Q_RANK + KV_RANK, Q_RANK + KV_RANK + ROPE
    return jnp.concatenate([w[:, :o1], z(Q_RANK_PAD - Q_RANK), w[:, o1:o2], w[:, o2:o3], z(LANE - ROPE), w[:, o3:]], axis=1)


def _unpad_mla_w_in(g):
    return jnp.concatenate([g[:, :Q_RANK], g[:, PROJ_CKV:PROJ_KR], g[:, PROJ_KR:PROJ_KR + ROPE], g[:, PROJ_Z:]], axis=1)


def _pad_w_uq(w):
    w3 = w.reshape(Q_RANK, HEADS, NOPE + ROPE)
    w3 = jnp.pad(w3, ((0, Q_RANK_PAD - Q_RANK), (0, 0), (0, HEAD_PAD - NOPE - ROPE)))
    return w3.reshape(Q_RANK_PAD, HEADS * HEAD_PAD)


def _unpad_w_uq(g):
    return g[:Q_RANK].reshape(Q_RANK, HEADS, HEAD_PAD)[:, :, :NOPE + ROPE].reshape(Q_RANK, HEADS * (NOPE + ROPE))


def _pack(pieces):
    flat = [p.reshape(-1).astype(F32) for p in pieces]
    tot = sum(f.shape[0] for f in flat)
    unit = SUBLANE * LANE
    padn = (-tot) % unit
    if padn:
        flat.append(jnp.zeros((padn,), F32))
    return jnp.concatenate(flat).reshape(-1, LANE)


def _unpack(packed, shapes, lead=()):
    flat = packed.reshape(tuple(lead) + (-1,))
    out, off = [], 0
    for shp in shapes:
        nel = int(np.prod(shp))
        out.append(flat[..., off:off + nel].reshape(tuple(lead) + tuple(shp)))
        off += nel
    return out


SMALL_GROUPS = {
    "a": [('ada_b', lambda t: t[1:]), ('pre_g', lambda t: t[1:]), ('post_g', lambda t: t[1:]),
          ('sgu_norm_g', lambda t: t[1:]), ('sgu_w_s', lambda t: t[1:]), ('sgu_b_s', lambda t: t[1:]),
          ('mla_q_norm_g', lambda t: t), ('mla_kv_norm_g', lambda t: t)],
    "b": [('ada_b', lambda t: t[0:1, 2 * t.shape[1] // 3:]), ('post_g', lambda t: t[0:1]),
          ('sgu_norm_g', lambda t: t[0:1]), ('sgu_w_s', lambda t: t[0:1]), ('sgu_b_s', lambda t: t[0:1])],
    "c": [('ada_b', lambda t: t[0:1, :2 * t.shape[1] // 3]), ('pre_g', lambda t: t[0:1])],
}


WEIGHTS = ['ada_w', 'ada_b', 'pre_g', 'post_g', 'sgu_w_in', 'sgu_norm_g', 'sgu_w_s', 'sgu_b_s', 'sgu_w_out',
           'mla_w_in', 'mla_q_norm_g', 'mla_kv_norm_g', 'mla_w_uq', 'mla_w_ukv', 'mla_w_out']
INPUTS = ['x', 'c'] + WEIGHTS + ['loss_target'] + ['m_' + n for n in WEIGHTS] + ['v_' + n for n in WEIGHTS]


def kernel(x, c, ada_w, ada_b, pre_g, post_g, sgu_w_in, sgu_norm_g, sgu_w_s, sgu_b_s, sgu_w_out, mla_w_in, mla_q_norm_g, mla_kv_norm_g, mla_w_uq, mla_w_ukv, mla_w_out, loss_target, m_ada_w, m_ada_b, m_pre_g, m_post_g, m_sgu_w_in, m_sgu_norm_g, m_sgu_w_s, m_sgu_b_s, m_sgu_w_out, m_mla_w_in, m_mla_q_norm_g, m_mla_kv_norm_g, m_mla_w_uq, m_mla_w_ukv, m_mla_w_out, v_ada_w, v_ada_b, v_pre_g, v_post_g, v_sgu_w_in, v_sgu_norm_g, v_sgu_w_s, v_sgu_b_s, v_sgu_w_out, v_mla_w_in, v_mla_q_norm_g, v_mla_kv_norm_g, v_mla_w_uq, v_mla_w_ukv, v_mla_w_out):
    given = locals()
    A = {name: given[name] for name in INPUTS}
    x0 = A['x'][0]
    tgt = A['loss_target'][0]
    s, d = x0.shape
    e = 2 * d
    ncol = 3 * d // N_DEV
    _, _, _, me = _my_place()
    ktabs, qtabs = _rope_tables(s)

    gains = jnp.zeros((SUBLANE, LANE), F32)
    gains = gains.at[0:2, :Q_RANK // N_DEV].set(A['mla_q_norm_g'])
    gains = gains.at[2:4, :KV_RANK // N_DEV].set(A['mla_kv_norm_g'])
    c8 = jnp.broadcast_to(A['c'], (SUBLANE, d))
    cg, gg = _exchange([c8, gains], ["row", "row"], True, "ag_cond")
    cond_raw = cg.reshape(N_DEV, SUBLANE, d)[:, 0, :]
    gg = gg.reshape(N_DEV, SUBLANE, LANE)
    gq_full = jnp.transpose(gg[:, 0:2, :Q_RANK // N_DEV], (1, 0, 2)).reshape(N_MIX, Q_RANK)
    gkv_full = jnp.transpose(gg[:, 2:4, :KV_RANK // N_DEV], (1, 0, 2)).reshape(N_MIX, KV_RANK)
    gq_pad = jnp.pad(gq_full, ((0, 0), (0, Q_RANK_PAD - Q_RANK)))

    bias_my = lax.dynamic_slice_in_dim(A['ada_b'], me * ncol, ncol, axis=1)
    mod_part = _ada_mod(cond_raw, A['ada_w'], bias_my, "ada_mod")
    send = jnp.pad(jnp.transpose(mod_part, (1, 0, 2)), ((0, 0), (0, SUBLANE - DEPTH), (0, 0)))
    (rb,) = _exchange([send.reshape(N_DEV * SUBLANE, ncol)], ["row"], False, "a2a_mod")

    token = rb
    gathers = {}
    for i in range(DEPTH):
        j = i // N_MIX
        if i % N_MIX == 0:
            parts = [("in", [A['sgu_w_in'][j]], ["col"]), ("out", [A['sgu_w_out'][j]], ["row"])]
        else:
            parts = [("all", [A['mla_w_in'][j], A['mla_w_uq'][j], A['mla_w_ukv'][j], A['mla_w_out'][j]],
                      ["col", "col", "col", "row"])]
        for part, shards, kinds in parts:
            gathers[(i, part)], token = _gather2_start(shards, kinds, f"ag_start_{i}_{part}", token)

    def forward_gathers(i, carried):
        for key in [k for k in gathers if k[0] == i]:
            gathers[key], _, (carried,) = _gather2_forward(gathers[key], f"ag_forward_{key[0]}_{key[1]}", token,
                                                           carry=[carried])
        return carried

    mod = jnp.transpose(rb.reshape(N_DEV, SUBLANE, ncol)[:, :DEPTH, :], (1, 0, 2)).reshape(DEPTH, 3 * d) + token[0, 0]
    shift = [mod[i:i + 1, :d] for i in range(DEPTH)]
    scale = [mod[i:i + 1, d:2 * d] for i in range(DEPTH)]
    gate = [mod[i:i + 1, 2 * d:] for i in range(DEPTH)]

    saved = []
    x = x0
    for i in range(DEPTH):
        j = i // N_MIX
        h = _pre_fwd(x, A['pre_g'][i:i + 1], scale[i], shift[i], f"pre_fwd")
        if i == 0:
            gathers[(0, "in")], _, (h,) = _gather2_forward(gathers[(0, "in")], "ag_forward_0_in", token, carry=[h])
        if i % N_MIX == 0:
            (w_in,) = _gather2_wait(gathers[(i, "in")], f"ag_wait_{i}_in", h)
            uvz = _mm(h, w_in, "nn", F32, "sgu_in")
            if i == 0:
                gathers[(0, "out")], _, (uvz,) = _gather2_forward(gathers[(0, "out")], "ag_forward_0_out", token,
                                                                  carry=[uvz])
            bias_full = jnp.repeat(A['sgu_b_s'][j].T, e // SGU_GROUPS, axis=1)
            ng = A['sgu_norm_g'][j:j + 1]
            y = _sgu_mid_fwd(uvz, ng, A['sgu_w_s'][j], bias_full, "sgu_mid_fwd")
            y = forward_gathers(i + 1, y)
            (w_out,) = _gather2_wait(gathers[(i, "out")], f"ag_wait_{i}_out", y)
            out = _mm(y, w_out, "nn", F32, "sgu_out")
            saved.append(dict(x=x, h=h, uvz=uvz, y=y, out=out, w_in=w_in, w_out=w_out, bias=bias_full, ng=ng))
        else:
            w_in, w_uq, w_ukv, w_out = _gather2_wait(gathers[(i, "all")], f"ag_wait_{i}_all", h)
            w_in = _pad_mla_w_in(w_in)
            w_uq = _pad_w_uq(w_uq)
            gq, gkv = gq_pad[j:j + 1], gkv_full[j:j + 1]
            proj = _mm(h, w_in, "nn", F32, "mla_in")
            qn, kvn, kr = _mla_norm_fwd(proj, gq, gkv, ktabs, "mla_norm_fwd")
            q = _mm(qn, w_uq, "nn", F32, "mla_uq")
            q_cat = _rope_heads(q, qtabs, 1.0, SOFTMAX_SCALE * LOG2_E, BF16, "rope_q_fwd")
            kv = _mm(kvn, w_ukv, "nn", BF16, "mla_ukv")
            o, y, lse = _attn_fwd(q_cat, kv, kr, proj, "attn_fwd")
            y = forward_gathers(i + 1, y)
            out = _mm(y, w_out, "nn", F32, "mla_out")
            saved.append(dict(x=x, h=h, proj=proj, qn=qn, kvn=kvn, kr=kr, q_cat=q_cat, kv=kv, o=o, y=y, lse=lse,
                              out=out, w_in=w_in, w_uq=w_uq, w_ukv=w_ukv, w_out=w_out, gq=gq, gkv=gkv))
        x = _post_fwd(x, out, gate[i], A['post_g'][i:i + 1], "post_fwd")

    dx, loss_row = _loss_head(x, tgt, "loss_head")
    loss = lax.psum(loss_row[0, 0], ("x", "y", "c"))

    d_shift, d_scale, d_gate = [None] * DEPTH, [None] * DEPTH, [None] * DEPTH
    d_pre, d_post = [None] * DEPTH, [None] * DEPTH
    scatters, scatters_out = [None] * DEPTH, [None] * DEPTH
    small, small_grads, small_handles = {}, {}, {}
    for i in reversed(range(DEPTH)):
        j = i // N_MIX
        sv = saved[i]
        dout, d_gate[i], d_post[i] = _post_bwd(dx, sv['out'], gate[i], A['post_g'][i:i + 1], "post_bwd")
        if i % N_MIX == 0:
            dy = _mm(dout, sv['w_out'], "nt", F32, "sgu_out_dx")
            g_w_out = _mm(sv['y'], dout, "tn", BF16, "sgu_out_dw")
            scatters_out[i], token, (dy,) = _exchange_start([g_w_out], ["row"], False, f"rs_out_start_{i}", token,
                                                            carry=[dy])
            duvz, dws, dbs, dng = _sgu_mid_bwd(sv['uvz'], dy, sv['ng'], A['sgu_w_s'][j], sv['bias'], "sgu_mid_bwd")
            small[('sgu', j)] = (dws, dbs.reshape(SGU_GROUPS, SGU_BLOCK), dng)
            g_w_in = _mm(sv['h'], duvz, "tn", BF16, "sgu_in_dw")
            if i == 0:
                scatters[i], token, (duvz,) = _scatter2_start(g_w_in, "rs2_start_0", token, carry=[duvz])
                small_grads["b"] = [d_gate[0], d_post[0], dng, dws[None], small[('sgu', 0)][1][None]]
                small_handles["b"], token, (duvz,) = _exchange_start(
                    [_pack(small_grads["b"])], ["row"], True, "ag_small_b_start", token, carry=[duvz])
            else:
                scatters[i], token, (duvz,) = _exchange_start([g_w_in], ["col"], False, f"rs_start_{i}", token,
                                                              carry=[duvz])
            dh = _mm(duvz, sv['w_in'], "nt", F32, "sgu_in_dx")
        else:
            dy = _mm(dout, sv['w_out'], "nt", F32, "mla_out_dx")
            g_w_out = _mm(sv['y'], dout, "tn", BF16, "mla_out_dw")
            scatters_out[i], token, (dy,) = _exchange_start([g_w_out], ["row"], False, f"rs_out_start_{i}", token,
                                                            carry=[dy])
            do, dz = _mla_gate_bwd(dy, sv['o'], sv['proj'], "mla_gate_bwd")
            dq, dkv, dkr_heads = _attn_bwd(sv['q_cat'], sv['kv'], sv['kr'], do, sv['o'], sv['lse'], "attn_bwd")
            dq_b = _rope_heads(dq, qtabs, -1.0, SOFTMAX_SCALE, BF16, "rope_q_bwd")
            dqn = _mm(dq_b, sv['w_uq'], "nt", F32, "mla_uq_dx")
            g_w_uq = _unpad_w_uq(_mm(sv['qn'], dq_b, "tn", BF16, "mla_uq_dw"))
            dkvn = _mm(dkv, sv['w_ukv'], "nt", F32, "mla_ukv_dx")
            g_w_ukv = _mm(sv['kvn'], dkv, "tn", BF16, "mla_ukv_dw")
            dpa, dgq, dgkv = _mla_norm_bwd(dqn, dkvn, dkr_heads, sv['proj'], sv['gq'], sv['gkv'], ktabs, "mla_norm_bwd")
            dproj = jnp.concatenate([dpa, dz], axis=1)
            g_w_in = _unpad_mla_w_in(_mm(sv['h'], dproj, "tn", BF16, "mla_in_dw"))
            scatters[i], token, (dproj,) = _exchange_start(
                [g_w_in, g_w_uq, g_w_ukv], ["col", "col", "col"], False, f"rs_start_{i}", token, carry=[dproj])
            dh = _mm(dproj, sv['w_in'], "nt", F32, "mla_in_dx")
            small[('mla', j)] = (dgq[:, :Q_RANK], dgkv)
        dx, d_shift[i], d_scale[i], d_pre[i] = _pre_bwd(dh, sv['x'], dx, A['pre_g'][i:i + 1], scale[i], "pre_bwd")
        if i == 1:
            small_grads["a"] = [
                jnp.concatenate([jnp.concatenate([d_shift[l], d_scale[l], d_gate[l]], axis=1)
                                 for l in range(1, DEPTH)], axis=0),
                jnp.concatenate(d_pre[1:], axis=0), jnp.concatenate(d_post[1:], axis=0),
                small[('sgu', 1)][2], small[('sgu', 1)][0][None], small[('sgu', 1)][1][None],
                jnp.concatenate([small[('mla', jj)][0] for jj in range(N_MIX)], axis=0),
                jnp.concatenate([small[('mla', jj)][1] for jj in range(N_MIX)], axis=0)]
            small_handles["a"], token, (dx,) = _exchange_start(
                [_pack(small_grads["a"])], ["row"], True, "ag_small_a_start", token, carry=[dx])

    res = {}

    def big(name, recv0, recv1):
        res[name] = _adam_reduce(recv0, recv1, A[name], A['m_' + name], A['v_' + name], "adam_" + name)

    def small_full(name, prefix):
        t = A[prefix + name]
        if name in ('mla_q_norm_g', 'mla_kv_norm_g'):
            t = lax.dynamic_update_slice_in_dim(jnp.zeros((t.shape[0], t.shape[1] * N_DEV), F32), t,
                                                me * t.shape[1], axis=1)
        return t

    def finish(i, after):
        first = ([_scatter2_wait(scatters[i], "rs2_wait_0", after)] if i == 0
                 else _exchange_wait(scatters[i], f"rs_wait_{i}", after))
        return first + _exchange_wait(scatters_out[i], f"rs_out_wait_{i}", after)

    recv_mla = {j: finish(N_MIX * j + 1, dx) for j in reversed(range(N_MIX))}
    for idx_w, name in enumerate(['mla_w_in', 'mla_w_uq', 'mla_w_ukv', 'mla_w_out']):
        big(name, recv_mla[0][idx_w], recv_mla[1][idx_w])

    recv_sgu = {j: finish(N_MIX * j, res['mla_w_out'][0]) for j in reversed(range(N_MIX))}
    for idx_w, name in enumerate(['sgu_w_in', 'sgu_w_out']):
        big(name, recv_sgu[0][idx_w], recv_sgu[1][idx_w])

    small_grads["c"] = [jnp.concatenate([d_shift[0], d_scale[0]], axis=1), d_pre[0]]
    gathered = {tag: _exchange_wait(small_handles[tag], f"ag_small_{tag}_wait", res['sgu_w_in'][0])[0]
                for tag in ("a", "b")}
    (gathered["c"],) = _exchange([_pack(small_grads["c"])], ["row"], True, "ag_small_c", after=res['sgu_w_in'][0])

    upd, parts = {}, {}
    for tag, group in SMALL_GROUPS.items():
        shapes = [g.shape for g in small_grads[tag]]
        packs = [_pack([pick(small_full(name, prefix)) for name, pick in group]) for prefix in ('', 'm_', 'v_')]
        outs4 = _adam_small(gathered[tag].reshape(N_DEV, -1, LANE), *packs, "adam_small_" + tag)
        upd[tag] = [_unpack(t, shapes) for t in outs4]
        parts[tag] = _unpack(gathered[tag].reshape(N_DEV, -1), shapes, lead=(N_DEV,))
    for k_out in range(4):
        ab_rest, pre_rest, post_rest, ng_1, ws_1, bs_1, gq_all, gkv_all = upd["a"][k_out]
        ab_gate0, post_0, ng_0, ws_0, bs_0 = upd["b"][k_out]
        ab_0, pre_0 = upd["c"][k_out]
        for name, val in (
                ('ada_b', jnp.concatenate([jnp.concatenate([ab_0, ab_gate0], axis=1), ab_rest], axis=0)),
                ('pre_g', jnp.concatenate([pre_0, pre_rest], axis=0)),
                ('post_g', jnp.concatenate([post_0, post_rest], axis=0)),
                ('sgu_norm_g', jnp.concatenate([ng_0, ng_1], axis=0)),
                ('sgu_w_s', jnp.concatenate([ws_0, ws_1], axis=0)),
                ('sgu_b_s', jnp.concatenate([bs_0, bs_1], axis=0)),
                ('mla_q_norm_g', gq_all), ('mla_kv_norm_g', gkv_all)):
            if name in ('mla_q_norm_g', 'mla_kv_norm_g'):
                wdt = A[name].shape[1]
                val = lax.dynamic_slice_in_dim(val, me * wdt, wdt, axis=1)
            res.setdefault(name, [None] * 4)[k_out] = val

    dmod_all = jnp.concatenate([jnp.concatenate([parts["c"][0], parts["b"][0]], axis=2), parts["a"][0]], axis=1)
    dmod_cols = jnp.transpose(lax.dynamic_slice_in_dim(dmod_all, me * ncol, ncol, axis=2), (1, 0, 2))
    res['ada_w'] = _ada_bwd_adam(jnp.transpose(cond_raw), dmod_cols, A['ada_w'], A['m_ada_w'], A['v_ada_w'], "ada_bwd")

    outs = [loss, dx[None]]
    for k_out in range(4):
        outs += [res[n][k_out] for n in WEIGHTS]
    return tuple(outs)
```

```python
import functools
import math

import numpy as np
import jax
import jax.numpy as jnp
from jax import lax
from jax.experimental import pallas as pl
from jax.experimental.pallas import tpu as pltpu

F32 = jnp.float32
BF16 = jnp.bfloat16
MESH = pl.DeviceIdType.MESH

N_DEV = 8
DEPTH = 4
N_MIX = 2
NORM_EPS = 1e-6
CHUNK = 64
SGU_BLOCK = 128
SGU_GROUPS = 16
HEADS = 16
Q_RANK = 448
Q_RANK_PAD = 512
KV_RANK = 512
NOPE = 128
ROPE = 64
HALF = ROPE // 2
V_DIM = 128
HEAD_PAD = 256
ROPE_THETA = 10000.0
MLA_WIDTH = HEADS * V_DIM
LANE = 128
SUBLANE = 8
PROJ_CQ = 0
PROJ_CKV = Q_RANK_PAD
PROJ_KR = Q_RANK_PAD + KV_RANK
PROJ_Z = PROJ_KR + LANE
PROJ_W = PROJ_Z + MLA_WIDTH

ADAM_LR = 0.001
ADAM_B1 = 0.9
ADAM_B2 = 0.999
ADAM_EPS = 1e-08
ADAM_WD = 0.01
ADAM_STEP = 10

VMEM_LIMIT = 56 * 1024 * 1024
ATT_BLK = 512
ATT_SUB = 128
ROW_BLK = 512
MM_TM, MM_TN, MM_TK = 1024, 1024, 2048
MM_TILE_BYTES = 40 * 1024 * 1024
SOFTMAX_SCALE = (NOPE + ROPE) ** -0.5
LOG2_E = 1.0 / math.log(2.0)
INV_SQRT2 = 1.0 / math.sqrt(2.0)
INV_SQRT_2PI = 1.0 / math.sqrt(2.0 * math.pi)


def _pcall(body, comm=False, **kw):
    return pl.pallas_call(body, **kw)


def _params(sem=None):
    return pltpu.CompilerParams(dimension_semantics=sem, vmem_limit_bytes=VMEM_LIMIT)


def _pick(dim, pref):
    if dim <= pref:
        return dim
    t = (pref // LANE) * LANE
    while t >= LANE:
        if dim % t == 0:
            return t
        t -= LANE
    return dim


def _gelu(x):
    return 0.5 * x * (1.0 + lax.erf(x * INV_SQRT2))


def _gelu_grad(x):
    return 0.5 * (1.0 + lax.erf(x * INV_SQRT2)) + x * jnp.exp(-0.5 * x * x) * INV_SQRT_2PI


def _sigmoid(x):
    return 1.0 / (1.0 + jnp.exp(-x))


def _dot_nt(a, b):
    return lax.dot_general(a, b, (((1,), (1,)), ((), ())), preferred_element_type=F32)


def _dot_tn(a, b):
    return lax.dot_general(a, b, (((0,), (0,)), ((), ())), preferred_element_type=F32)


def _mm(a, b, dims, out_dtype, name):
    if dims == "nn":
        (m, k), (k2, n) = a.shape, b.shape
    elif dims == "nt":
        (m, k), (n, k2) = a.shape, b.shape
    else:
        (k, m), (k2, n) = a.shape, b.shape
    assert k == k2, (a.shape, b.shape, dims)
    tm, tn = _pick(m, MM_TM), _pick(n, MM_TN)
    out_bytes = 2 * tm * tn * jnp.dtype(out_dtype).itemsize
    whole_k = 2 * (tm + tn) * k * a.dtype.itemsize + out_bytes <= MM_TILE_BYTES
    tk = k if whole_k else _pick(k, MM_TK)
    nk = k // tk

    def body(a_ref, b_ref, o_ref, *scratch):
        if dims == "nn":
            p = jnp.dot(a_ref[...], b_ref[...], preferred_element_type=F32)
        elif dims == "nt":
            p = _dot_nt(a_ref[...], b_ref[...])
        else:
            p = _dot_tn(a_ref[...], b_ref[...])
        if nk == 1:
            o_ref[...] = p.astype(o_ref.dtype)
            return
        acc_ref, = scratch
        kk = pl.program_id(2)

        @pl.when(kk == 0)
        def _():
            acc_ref[...] = p

        @pl.when(kk > 0)
        def _():
            acc_ref[...] += p

        @pl.when(kk == nk - 1)
        def _():
            o_ref[...] = acc_ref[...].astype(o_ref.dtype)

    if dims == "tn":
        a_spec = pl.BlockSpec((tk, tm), lambda i, j, kk: (kk, i))
    else:
        a_spec = pl.BlockSpec((tm, tk), lambda i, j, kk: (i, kk))
    if dims == "nt":
        b_spec = pl.BlockSpec((tn, tk), lambda i, j, kk: (j, kk))
    else:
        b_spec = pl.BlockSpec((tk, tn), lambda i, j, kk: (kk, j))
    return _pcall(
        body, name=name,
        grid=(m // tm, n // tn, nk),
        in_specs=[a_spec, b_spec],
        out_specs=pl.BlockSpec((tm, tn), lambda i, j, kk: (i, j)),
        out_shape=jax.ShapeDtypeStruct((m, n), out_dtype),
        scratch_shapes=[pltpu.VMEM((tm, tn), F32)] if nk > 1 else [],
        compiler_params=_params(("parallel", "parallel", "arbitrary")),
    )(a, b)


def _row_spec(ts, d):
    return pl.BlockSpec((ts, d), lambda i: (i, 0))


def _vec_spec(d):
    return pl.BlockSpec((1, d), lambda i: (0, 0))


def _pre_fwd(x, g, scale, shift, name):
    s, d = x.shape
    ts = _pick(s, ROW_BLK)

    def body(x_ref, g_ref, sc_ref, sh_ref, h_ref):
        xv = x_ref[...]
        r = lax.rsqrt(jnp.mean(xv * xv, axis=-1, keepdims=True) + NORM_EPS)
        h_ref[...] = ((xv * r * g_ref[...]) * (1.0 + sc_ref[...]) + sh_ref[...]).astype(BF16)

    return _pcall(
        body, name=name, grid=(s // ts,),
        in_specs=[_row_spec(ts, d), _vec_spec(d), _vec_spec(d), _vec_spec(d)],
        out_specs=_row_spec(ts, d),
        out_shape=jax.ShapeDtypeStruct((s, d), BF16),
        compiler_params=_params(("parallel",)),
    )(x, g, scale, shift)


def _post_fwd(x, out, gate, g, name):
    s, d = x.shape
    ts = _pick(s, ROW_BLK)

    def body(x_ref, o_ref, gate_ref, g_ref, y_ref):
        o = o_ref[...]
        r = lax.rsqrt(jnp.mean(o * o, axis=-1, keepdims=True) + NORM_EPS)
        y_ref[...] = x_ref[...] + gate_ref[...] * (o * r * g_ref[...])

    return _pcall(
        body, name=name, grid=(s // ts,),
        in_specs=[_row_spec(ts, d), _row_spec(ts, d), _vec_spec(d), _vec_spec(d)],
        out_specs=_row_spec(ts, d),
        out_shape=jax.ShapeDtypeStruct((s, d), F32),
        compiler_params=_params(("parallel",)),
    )(x, out, gate, g)


def _loss_head(xf, tgt, name):
    s, d = xf.shape
    ts = _pick(s, ROW_BLK)
    ns = s // ts

    def body(x_ref, t_ref, dx_ref, loss_ref, acc_ref):
        i = pl.program_id(0)

        @pl.when(i == 0)
        def _():
            acc_ref[...] = jnp.zeros_like(acc_ref)

        e = x_ref[...] - t_ref[...]
        dx_ref[...] = e * (1.0 / d)
        acc_ref[...] += jnp.sum(e * e, axis=0, keepdims=True)

        @pl.when(i == ns - 1)
        def _():
            tot = jnp.sum(acc_ref[...], axis=1, keepdims=True) * (0.5 / d)
            loss_ref[...] = jnp.broadcast_to(tot, loss_ref.shape)

    return _pcall(
        body, name=name, grid=(ns,),
        in_specs=[_row_spec(ts, d), _row_spec(ts, d)],
        out_specs=[_row_spec(ts, d), pl.BlockSpec((1, LANE), lambda i: (0, 0))],
        out_shape=[jax.ShapeDtypeStruct((s, d), F32), jax.ShapeDtypeStruct((1, LANE), F32)],
        scratch_shapes=[pltpu.VMEM((1, d), F32)],
        compiler_params=_params(("arbitrary",)),
    )(xf, tgt)


def _post_bwd(dxo, out, gate, g, name):
    s, d = dxo.shape
    ts = _pick(s, ROW_BLK)

    def body(dx_ref, o_ref, gate_ref, g_ref, do_ref, dgate_ref, dg_ref):
        i = pl.program_id(0)

        @pl.when(i == 0)
        def _():
            dgate_ref[...] = jnp.zeros_like(dgate_ref)
            dg_ref[...] = jnp.zeros_like(dg_ref)

        o = o_ref[...]
        dx = dx_ref[...]
        gv = g_ref[...]
        r = lax.rsqrt(jnp.mean(o * o, axis=-1, keepdims=True) + NORM_EPS)
        n = o * r
        dyn = dx * gate_ref[...]
        dgate_ref[...] += jnp.sum(dx * (n * gv), axis=0, keepdims=True)
        dg_ref[...] += jnp.sum(dyn * n, axis=0, keepdims=True)
        dn = dyn * gv
        do_ref[...] = (r * (dn - n * jnp.mean(dn * n, axis=-1, keepdims=True))).astype(BF16)

    return _pcall(
        body, name=name, grid=(s // ts,),
        in_specs=[_row_spec(ts, d), _row_spec(ts, d), _vec_spec(d), _vec_spec(d)],
        out_specs=[_row_spec(ts, d), _vec_spec(d), _vec_spec(d)],
        out_shape=[jax.ShapeDtypeStruct((s, d), BF16), jax.ShapeDtypeStruct((1, d), F32),
                   jax.ShapeDtypeStruct((1, d), F32)],
        compiler_params=_params(("arbitrary",)),
    )(dxo, out, gate, g)


def _pre_bwd(dh, x, dxo, g, scale, name):
    s, d = x.shape
    ts = _pick(s, ROW_BLK)

    def body(dh_ref, x_ref, dxo_ref, g_ref, sc_ref, dx_ref, dsh_ref, dsc_ref, dg_ref):
        i = pl.program_id(0)

        @pl.when(i == 0)
        def _():
            dsh_ref[...] = jnp.zeros_like(dsh_ref)
            dsc_ref[...] = jnp.zeros_like(dsc_ref)
            dg_ref[...] = jnp.zeros_like(dg_ref)

        xv = x_ref[...]
        dhv = dh_ref[...]
        gv = g_ref[...]
        one_sc = 1.0 + sc_ref[...]
        r = lax.rsqrt(jnp.mean(xv * xv, axis=-1, keepdims=True) + NORM_EPS)
        n = xv * r
        dsh_ref[...] += jnp.sum(dhv, axis=0, keepdims=True)
        dsc_ref[...] += jnp.sum(dhv * (n * gv), axis=0, keepdims=True)
        dng = dhv * one_sc
        dg_ref[...] += jnp.sum(dng * n, axis=0, keepdims=True)
        dn = dng * gv
        dx_ref[...] = dxo_ref[...] + r * (dn - n * jnp.mean(dn * n, axis=-1, keepdims=True))

    return _pcall(
        body, name=name, grid=(s // ts,),
        in_specs=[_row_spec(ts, d), _row_spec(ts, d), _row_spec(ts, d), _vec_spec(d), _vec_spec(d)],
        out_specs=[_row_spec(ts, d), _vec_spec(d), _vec_spec(d), _vec_spec(d)],
        out_shape=[jax.ShapeDtypeStruct((s, d), F32)] + [jax.ShapeDtypeStruct((1, d), F32)] * 3,
        compiler_params=_params(("arbitrary",)),
    )(dh, x, dxo, g, scale)


def _sgu_mask():
    t = lax.broadcasted_iota(jnp.int32, (SGU_BLOCK, SGU_BLOCK), 0) // CHUNK
    s = lax.broadcasted_iota(jnp.int32, (SGU_BLOCK, SGU_BLOCK), 1) // CHUNK
    return s <= t


def _sgu_norm(v_pre, g):
    e = v_pre.shape[-1]
    vg = _gelu(v_pre)
    mu = jnp.sum(vg, axis=-1, keepdims=True) * (1.0 / e)
    dlt = vg - mu
    var = jnp.sum(dlt * dlt, axis=-1, keepdims=True) * (1.0 / e)
    rstd = lax.rsqrt(var + NORM_EPS)
    vhat = dlt * rstd
    return vhat, rstd, (vhat * g).astype(BF16)


def _sgu_mid_fwd(uvz, norm_g, w_s, bias_full, name):
    s, e3 = uvz.shape
    e = e3 // 3
    gd = e // SGU_GROUPS
    nb = s // SGU_BLOCK

    def body(uvz_ref, g_ref, w_ref, b_ref, y_ref, wsc):
        @pl.when(pl.program_id(0) == 0)
        def _():
            msk = _sgu_mask()
            for gi in range(SGU_GROUPS):
                wsc[gi] = jnp.where(msk, w_ref[gi], 0.0).astype(BF16)

        _, _, vb = _sgu_norm(uvz_ref[:, e:2 * e], g_ref[...])
        for gi in range(SGU_GROUPS):
            lo = gi * gd
            vm = jnp.dot(wsc[gi], vb[:, lo:lo + gd], preferred_element_type=F32) + b_ref[:, lo:lo + gd]
            zg = uvz_ref[:, 2 * e + lo:2 * e + lo + gd]
            y_ref[:, lo:lo + gd] = (_gelu(uvz_ref[:, lo:lo + gd]) * vm * (zg * _sigmoid(zg))).astype(BF16)

    return _pcall(
        body, name=name, grid=(nb,),
        in_specs=[pl.BlockSpec((SGU_BLOCK, e3), lambda n: (n, 0)),
                  pl.BlockSpec((1, e), lambda n: (0, 0)),
                  pl.BlockSpec((SGU_GROUPS, SGU_BLOCK, SGU_BLOCK), lambda n: (0, 0, 0)),
                  pl.BlockSpec((SGU_BLOCK, e), lambda n: (0, 0))],
        out_specs=pl.BlockSpec((SGU_BLOCK, e), lambda n: (n, 0)),
        out_shape=jax.ShapeDtypeStruct((s, e), BF16),
        scratch_shapes=[pltpu.VMEM((SGU_GROUPS, SGU_BLOCK, SGU_BLOCK), BF16)],
        compiler_params=_params(("arbitrary",)),
    )(uvz, norm_g, w_s, bias_full)


def _sgu_mid_bwd(uvz, dy, norm_g, w_s, bias_full, name):
    s, e3 = uvz.shape
    e = e3 // 3
    gd = e // SGU_GROUPS
    nb = s // SGU_BLOCK

    def body(uvz_ref, dy_ref, g_ref, w_ref, b_ref, d_ref, dw_ref, db_ref, dg_ref, wsc, wtsc, dvh_sc, dbacc):
        n = pl.program_id(0)

        @pl.when(n == 0)
        def _():
            msk = _sgu_mask()
            for gi in range(SGU_GROUPS):
                wm = jnp.where(msk, w_ref[gi], 0.0)
                wsc[gi] = wm.astype(BF16)
                wtsc[gi] = wm.T.astype(BF16)
            dw_ref[...] = jnp.zeros_like(dw_ref)
            dg_ref[...] = jnp.zeros_like(dg_ref)
            dbacc[...] = jnp.zeros_like(dbacc)

        v_pre = uvz_ref[:, e:2 * e]
        gv = g_ref[...]
        vhat, rstd, vb = _sgu_norm(v_pre, gv)
        s1 = jnp.zeros((SGU_BLOCK, 1), F32)
        s2 = jnp.zeros((SGU_BLOCK, 1), F32)
        for gi in range(SGU_GROUPS):
            lo = gi * gd
            u_pre = uvz_ref[:, lo:lo + gd]
            zg = uvz_ref[:, 2 * e + lo:2 * e + lo + gd]
            dyg = dy_ref[:, lo:lo + gd]
            ug = _gelu(u_pre)
            sig = _sigmoid(zg)
            vbg = vb[:, lo:lo + gd]
            vhg = vhat[:, lo:lo + gd]
            vm = jnp.dot(wsc[gi], vbg, preferred_element_type=F32) + b_ref[:, lo:lo + gd]
            t = dyg * (zg * sig)
            d_ref[:, lo:lo + gd] = (t * vm * _gelu_grad(u_pre)).astype(BF16)
            dvm = t * ug
            d_ref[:, 2 * e + lo:2 * e + lo + gd] = (dyg * ug * vm * (sig * (1.0 + zg * (1.0 - sig)))).astype(BF16)
            dvm_b = dvm.astype(BF16)
            dv = jnp.dot(wtsc[gi], dvm_b, preferred_element_type=F32)
            dw_ref[gi] += _dot_nt(dvm_b, vbg)
            dbacc[:, lo:lo + gd] += dvm
            dg_ref[:, lo:lo + gd] += jnp.sum(dv * vhg, axis=0, keepdims=True)
            dvh = dv * gv[:, lo:lo + gd]
            dvh_sc[:, lo:lo + gd] = dvh
            s1 = s1 + jnp.sum(dvh, axis=-1, keepdims=True)
            s2 = s2 + jnp.sum(dvh * vhg, axis=-1, keepdims=True)
        dvg = rstd * (dvh_sc[...] - s1 * (1.0 / e) - vhat * (s2 * (1.0 / e)))
        d_ref[:, e:2 * e] = (dvg * _gelu_grad(v_pre)).astype(BF16)

        @pl.when(n == nb - 1)
        def _():
            msk = _sgu_mask()
            for gi in range(SGU_GROUPS):
                dw_ref[gi] = jnp.where(msk, dw_ref[gi], 0.0)
                db_ref[gi] = jnp.sum(dbacc[:, gi * gd:(gi + 1) * gd], axis=1, keepdims=True)

    return _pcall(
        body, name=name, grid=(nb,),
        in_specs=[pl.BlockSpec((SGU_BLOCK, e3), lambda n: (n, 0)),
                  pl.BlockSpec((SGU_BLOCK, e), lambda n: (n, 0)),
                  pl.BlockSpec((1, e), lambda n: (0, 0)),
                  pl.BlockSpec((SGU_GROUPS, SGU_BLOCK, SGU_BLOCK), lambda n: (0, 0, 0)),
                  pl.BlockSpec((SGU_BLOCK, e), lambda n: (0, 0))],
        out_specs=[pl.BlockSpec((SGU_BLOCK, e3), lambda n: (n, 0)),
                   pl.BlockSpec((SGU_GROUPS, SGU_BLOCK, SGU_BLOCK), lambda n: (0, 0, 0)),
                   pl.BlockSpec((SGU_GROUPS, SGU_BLOCK, 1), lambda n: (0, 0, 0)),
                   pl.BlockSpec((1, e), lambda n: (0, 0))],
        out_shape=[jax.ShapeDtypeStruct((s, e3), BF16),
                   jax.ShapeDtypeStruct((SGU_GROUPS, SGU_BLOCK, SGU_BLOCK), F32),
                   jax.ShapeDtypeStruct((SGU_GROUPS, SGU_BLOCK, 1), F32),
                   jax.ShapeDtypeStruct((1, e), F32)],
        scratch_shapes=[pltpu.VMEM((SGU_GROUPS, SGU_BLOCK, SGU_BLOCK), BF16),
                        pltpu.VMEM((SGU_GROUPS, SGU_BLOCK, SGU_BLOCK), BF16),
                        pltpu.VMEM((SGU_BLOCK, e), F32),
                        pltpu.VMEM((SGU_BLOCK, e), F32)],
        compiler_params=_params(("arbitrary",)),
    )(uvz, dy, norm_g, w_s, bias_full)


def _rope_tables(s):
    pos = jnp.arange(s, dtype=F32)
    inv_freq = ROPE_THETA ** (-jnp.arange(0, ROPE, 2, dtype=F32) / ROPE)
    ang = pos[:, None] * inv_freq[None, :]
    cos, sin = jnp.cos(ang), jnp.sin(ang)
    z32 = jnp.zeros((s, HALF), F32)
    z64 = jnp.zeros((s, ROPE), F32)
    ck = jnp.concatenate([cos, cos, z64], axis=1)
    s1k = jnp.concatenate([-sin, z32, z64], axis=1)
    s2k = jnp.concatenate([z32, sin, z64], axis=1)
    return ck, s1k, s2k


def _rot(x, c, s1, s2):
    w = x.shape[-1]
    return x * c + pltpu.roll(x, w - HALF, 1) * s1 + pltpu.roll(x, HALF, 1) * s2


def _rms(cv, n_real):
    r = lax.rsqrt(jnp.sum(cv * cv, axis=-1, keepdims=True) * (1.0 / n_real) + NORM_EPS)
    return r, cv * r


def _mla_norm_fwd(proj, gq, gkv, tabs, name):
    s = proj.shape[0]
    ts = _pick(s, ROW_BLK)
    ck, s1k, s2k = tabs

    def body(cq_ref, ckv_ref, kr_ref, gq_ref, gkv_ref, c_ref, s1_ref, s2_ref, qn_ref, kvn_ref, kro_ref):
        _, nq = _rms(cq_ref[...], Q_RANK)
        qn_ref[...] = (nq * gq_ref[...]).astype(BF16)
        _, nkv = _rms(ckv_ref[...], KV_RANK)
        kvn_ref[...] = (nkv * gkv_ref[...]).astype(BF16)
        kro_ref[...] = _rot(kr_ref[...], c_ref[...], s1_ref[...], s2_ref[...]).astype(BF16)

    tab = pl.BlockSpec((ts, LANE), lambda i: (i, 0))
    return _pcall(
        body, name=name, grid=(s // ts,),
        in_specs=[pl.BlockSpec((ts, Q_RANK_PAD), lambda i: (i, 0)),
                  pl.BlockSpec((ts, KV_RANK), lambda i: (i, PROJ_CKV // KV_RANK)),
                  pl.BlockSpec((ts, LANE), lambda i: (i, PROJ_KR // LANE)),
                  _vec_spec(Q_RANK_PAD), _vec_spec(KV_RANK), tab, tab, tab],
        out_specs=[pl.BlockSpec((ts, Q_RANK_PAD), lambda i: (i, 0)),
                   pl.BlockSpec((ts, KV_RANK), lambda i: (i, 0)), tab],
        out_shape=[jax.ShapeDtypeStruct((s, Q_RANK_PAD), BF16), jax.ShapeDtypeStruct((s, KV_RANK), BF16),
                   jax.ShapeDtypeStruct((s, LANE), BF16)],
        compiler_params=_params(("parallel",)),
    )(proj, proj, proj, gq, gkv, ck, s1k, s2k)


def _transpose_bf16(t):
    return t.astype(F32).T.astype(BF16)


def _diag_mask(tb, transposed):
    r = lax.broadcasted_iota(jnp.int32, (tb, tb), 0) // CHUNK
    c = lax.broadcasted_iota(jnp.int32, (tb, tb), 1) // CHUNK
    return (r <= c) if transposed else (c <= r)


def _attn_fwd(q, kv, kr, proj, tabs, name):
    s = q.shape[0]
    tb = _pick(s, ATT_BLK)
    nb = s // tb
    zcol = PROJ_Z // V_DIM
    mult = SOFTMAX_SCALE * LOG2_E

    def body(qf_ref, c_ref, s1_ref, s2_ref, kn_ref, v_ref, kr_ref, z_ref, o_ref, y_ref, lse_ref, q_ref,
             kt_sc, vx_sc, m_sc, acc_sc, sa_sc, sb_sc):
        qi = pl.program_id(1)
        q_ref[:, :NOPE] = (qf_ref[:, :NOPE] * mult).astype(BF16)
        q_ref[:, NOPE:] = (_rot(qf_ref[:, NOPE:], c_ref[...], s1_ref[...], s2_ref[...]) * mult).astype(BF16)

        @pl.when(qi == 0)
        def _():
            for b in range(nb):
                rows = slice(b * tb, (b + 1) * tb)
                kt_sc[b] = _transpose_bf16(jnp.concatenate([kn_ref[rows, :], kr_ref[rows, :]], axis=1))
                vx_sc[b] = jnp.concatenate([v_ref[rows, :], jnp.ones((tb, V_DIM), BF16)], axis=1)

        m_sc[...] = jnp.full_like(m_sc, -1e30)
        acc_sc[...] = jnp.zeros_like(acc_sc)
        sub = min(tb, ATT_SUB)

        def scores(ki, s_ref):
            s_ref[...] = jnp.dot(q_ref[...], kt_sc[ki], preferred_element_type=F32)

        def step(ki, s_ref, masked):
            for r in range(tb // sub):
                rs = slice(r * sub, (r + 1) * sub)
                sc = s_ref[rs, :]
                if masked:
                    sc = jnp.where(_diag_mask(tb, False)[rs, :], sc, -1e30)
                m_prev = m_sc[rs, :]
                m_new = jnp.maximum(m_prev, jnp.max(sc, axis=-1, keepdims=True))
                p = jnp.exp2(sc - m_new).astype(BF16)
                acc_sc[rs, :] = (jnp.exp2(m_prev - m_new) * acc_sc[rs, :]
                                 + jnp.dot(p, vx_sc[ki], preferred_element_type=F32))
                m_sc[rs, :] = m_new

        def pair(t, carry):
            scores(2 * t + 1, sb_sc)
            step(2 * t, sa_sc, False)
            scores(2 * t + 2, sa_sc)
            step(2 * t + 1, sb_sc, False)
            return carry

        scores(0, sa_sc)
        lax.fori_loop(0, qi // 2, pair, 0)

        @pl.when(qi % 2 == 1)
        def _():
            scores(qi, sb_sc)
            step(qi - 1, sa_sc, False)
            step(qi, sb_sc, True)

        @pl.when(qi % 2 == 0)
        def _():
            step(qi, sa_sc, True)

        l = acc_sc[:, V_DIM:V_DIM + 1]
        o = acc_sc[:, :V_DIM] / l
        z = z_ref[...]
        o_ref[...] = o.astype(BF16)
        y_ref[...] = (o * (z * _sigmoid(z))).astype(BF16)
        lse_cols = jnp.broadcast_to(m_sc[...] + jnp.log2(l), (tb, LANE))
        lse_ref[...] = lse_cols.T[0:1, :]

    oblk = pl.BlockSpec((tb, V_DIM), lambda h, qi: (qi, h))
    qblk = pl.BlockSpec((tb, HEAD_PAD), lambda h, qi: (qi, h))
    tab = pl.BlockSpec((tb, LANE), lambda h, qi: (qi, 0))
    return _pcall(
        body, name=name, grid=(HEADS, nb),
        in_specs=[qblk, tab, tab, tab,
                  pl.BlockSpec((s, NOPE), lambda h, qi: (0, 2 * h)),
                  pl.BlockSpec((s, V_DIM), lambda h, qi: (0, 2 * h + 1)),
                  pl.BlockSpec((s, LANE), lambda h, qi: (0, 0)),
                  pl.BlockSpec((tb, V_DIM), lambda h, qi: (qi, zcol + h))],
        out_specs=[oblk, oblk, pl.BlockSpec((None, None, 1, tb), lambda h, qi: (h, qi, 0, 0)), qblk],
        out_shape=[jax.ShapeDtypeStruct((s, MLA_WIDTH), BF16), jax.ShapeDtypeStruct((s, MLA_WIDTH), BF16),
                   jax.ShapeDtypeStruct((HEADS, nb, 1, tb), F32), jax.ShapeDtypeStruct((s, HEADS * HEAD_PAD), BF16)],
        scratch_shapes=[pltpu.VMEM((nb, HEAD_PAD, tb), BF16), pltpu.VMEM((nb, tb, HEAD_PAD), BF16),
                        pltpu.VMEM((tb, 1), F32), pltpu.VMEM((tb, HEAD_PAD), F32),
                        pltpu.VMEM((tb, tb), F32), pltpu.VMEM((tb, tb), F32)],
        compiler_params=_params(("parallel", "arbitrary")),
    )(q, *tabs, kv, kv, kr, proj)


def _attn_bwd(q_cat, kv, kr, do, o, lse, tabs, name):
    s = q_cat.shape[0]
    tb = _pick(s, ATT_BLK)
    nb = s // tb
    ln2 = math.log(2.0)

    def body(q_ref, do_ref, o_ref, lse_ref, kn_ref, v_ref, kr_ref, c_ref, s1_ref, s2_ref, dq_ref, dkv_ref, dkr_ref,
             qt_sc, dot_sc, delta_sc, dqt_sc, dk_sc, dv_sc):
        ki = pl.program_id(1)

        @pl.when(ki == 0)
        def _():
            for b in range(nb):
                rows = slice(b * tb, (b + 1) * tb)
                qt_sc[b] = _transpose_bf16(q_ref[rows, :])
                do_t = do_ref[rows, :].astype(F32).T
                dot_sc[b] = do_t.astype(BF16)
                delta_sc[b] = jnp.sum(do_t * o_ref[rows, :].astype(F32).T, axis=0, keepdims=True)
            dqt_sc[...] = jnp.zeros_like(dqt_sc)

        k = jnp.concatenate([kn_ref[...], kr_ref[...]], axis=1)
        kt = _transpose_bf16(k)
        vb = v_ref[...]
        dk_sc[...] = jnp.zeros_like(dk_sc)
        dv_sc[...] = jnp.zeros_like(dv_sc)

        def step(qi, masked):
            rows = pl.ds(pl.multiple_of(qi * tb, tb), tb)
            sc_t = jnp.dot(k, qt_sc[qi], preferred_element_type=F32)
            if masked:
                sc_t = jnp.where(_diag_mask(tb, True), sc_t, -1e30)
            p_t = jnp.exp2(sc_t - lse_ref[qi])
            dp_t = jnp.dot(vb, dot_sc[qi], preferred_element_type=F32)
            ds_t = (p_t * (dp_t - delta_sc[qi])).astype(BF16)
            dv_sc[...] += jnp.dot(p_t.astype(BF16), do_ref[rows, :], preferred_element_type=F32)
            dk_sc[...] += jnp.dot(ds_t, q_ref[rows, :], preferred_element_type=F32)
            dqt_sc[qi] += jnp.dot(kt, ds_t, preferred_element_type=F32)

        step(ki, True)

        def loop_body(qi, carry):
            step(qi, False)
            return carry

        lax.fori_loop(ki + 1, nb, loop_body, 0)

        dkv_ref[:, :NOPE] = (dk_sc[:, :NOPE] * ln2).astype(BF16)
        dkv_ref[:, NOPE:] = dv_sc[...].astype(BF16)
        dkr_ref[...] = dk_sc[:, NOPE:] * ln2

        @pl.when(ki == nb - 1)
        def _():
            for b in range(nb):
                rows = slice(b * tb, (b + 1) * tb)
                dq = dqt_sc[b].T
                dq_ref[rows, :NOPE] = (dq[:, :NOPE] * SOFTMAX_SCALE).astype(BF16)
                dq_ref[rows, NOPE:] = (_rot(dq[:, NOPE:], c_ref[rows, :], -s1_ref[rows, :], -s2_ref[rows, :])
                                       * SOFTMAX_SCALE).astype(BF16)

    tab = pl.BlockSpec((s, LANE), lambda h, ki: (0, 0))
    return _pcall(
        body, name=name, grid=(HEADS, nb),
        in_specs=[pl.BlockSpec((s, HEAD_PAD), lambda h, ki: (0, h)),
                  pl.BlockSpec((s, V_DIM), lambda h, ki: (0, h)),
                  pl.BlockSpec((s, V_DIM), lambda h, ki: (0, h)),
                  pl.BlockSpec((None, nb, 1, tb), lambda h, ki: (h, 0, 0, 0)),
                  pl.BlockSpec((tb, NOPE), lambda h, ki: (ki, 2 * h)),
                  pl.BlockSpec((tb, V_DIM), lambda h, ki: (ki, 2 * h + 1)),
                  pl.BlockSpec((tb, LANE), lambda h, ki: (ki, 0)), tab, tab, tab],
        out_specs=[pl.BlockSpec((s, HEAD_PAD), lambda h, ki: (0, h)),
                   pl.BlockSpec((tb, HEAD_PAD), lambda h, ki: (ki, h)),
                   pl.BlockSpec((None, tb, LANE), lambda h, ki: (h, ki, 0))],
        out_shape=[jax.ShapeDtypeStruct((s, HEADS * HEAD_PAD), BF16),
                   jax.ShapeDtypeStruct((s, HEADS * HEAD_PAD), BF16),
                   jax.ShapeDtypeStruct((HEADS, s, LANE), F32)],
        scratch_shapes=[pltpu.VMEM((nb, HEAD_PAD, tb), BF16), pltpu.VMEM((nb, V_DIM, tb), BF16),
                        pltpu.VMEM((nb, 1, tb), F32), pltpu.VMEM((nb, HEAD_PAD, tb), F32),
                        pltpu.VMEM((tb, HEAD_PAD), F32), pltpu.VMEM((tb, V_DIM), F32)],
        compiler_params=_params(("parallel", "arbitrary")),
    )(q_cat, do, o, lse, kv, kv, kr, *tabs)


def _mla_gate_bwd(dy, o, proj, name):
    s = dy.shape[0]
    ts = _pick(s, ROW_BLK)

    def body(dy_ref, o_ref, p_ref, do_ref, dp_ref):
        z = p_ref[:, PROJ_Z:]
        dyv = dy_ref[...]
        sig = _sigmoid(z)
        do_ref[...] = (dyv * (z * sig)).astype(BF16)
        dp_ref[:, :PROJ_Z] = jnp.zeros((ts, PROJ_Z), BF16)
        dp_ref[:, PROJ_Z:] = (dyv * o_ref[...].astype(F32) * (sig * (1.0 + z * (1.0 - sig)))).astype(BF16)

    blk = pl.BlockSpec((ts, MLA_WIDTH), lambda i: (i, 0))
    wide = pl.BlockSpec((ts, PROJ_W), lambda i: (i, 0))
    return _pcall(
        body, name=name, grid=(s // ts,),
        in_specs=[blk, blk, wide],
        out_specs=[blk, wide],
        out_shape=[jax.ShapeDtypeStruct((s, MLA_WIDTH), BF16), jax.ShapeDtypeStruct((s, PROJ_W), BF16)],
        compiler_params=_params(("parallel",)),
    )(dy, o, proj)


def _mla_norm_bwd(dqn, dkvn, dkr_heads, proj, gq, gkv, tabs, dproj, name):
    s = proj.shape[0]
    ts = _pick(s, ROW_BLK)
    ck, s1k, s2k = tabs

    def rms_bwd(cv, dn_in, g, n_real):
        r, n = _rms(cv, n_real)
        dg = jnp.sum(dn_in * n, axis=0, keepdims=True)
        dn = dn_in * g
        dc = r * (dn - n * (jnp.sum(dn * n, axis=-1, keepdims=True) * (1.0 / n_real)))
        return dc, dg

    def body(dqn_ref, dkvn_ref, dkr_ref, cq_ref, ckv_ref, gq_ref, gkv_ref, c_ref, s1_ref, s2_ref, dp_in_ref,
             dp_ref, dgq_ref, dgkv_ref):
        @pl.when(pl.program_id(0) == 0)
        def _():
            dgq_ref[...] = jnp.zeros_like(dgq_ref)
            dgkv_ref[...] = jnp.zeros_like(dgkv_ref)

        dcq, dgq = rms_bwd(cq_ref[...], dqn_ref[...], gq_ref[...], Q_RANK)
        dckv, dgkv = rms_bwd(ckv_ref[...], dkvn_ref[...], gkv_ref[...], KV_RANK)
        dgq_ref[...] += dgq
        dgkv_ref[...] += dgkv
        dkr = dkr_ref[0]
        for h in range(1, HEADS):
            dkr = dkr + dkr_ref[h]
        dp_ref[:, PROJ_CQ:PROJ_CKV] = dcq.astype(BF16)
        dp_ref[:, PROJ_CKV:PROJ_KR] = dckv.astype(BF16)
        dp_ref[:, PROJ_KR:PROJ_Z] = _rot(dkr, c_ref[...], -s1_ref[...], -s2_ref[...]).astype(BF16)

    tab = pl.BlockSpec((ts, LANE), lambda i: (i, 0))
    return _pcall(
        body, name=name, grid=(s // ts,),
        in_specs=[pl.BlockSpec((ts, Q_RANK_PAD), lambda i: (i, 0)),
                  pl.BlockSpec((ts, KV_RANK), lambda i: (i, 0)),
                  pl.BlockSpec((HEADS, ts, LANE), lambda i: (0, i, 0)),
                  pl.BlockSpec((ts, Q_RANK_PAD), lambda i: (i, 0)),
                  pl.BlockSpec((ts, KV_RANK), lambda i: (i, PROJ_CKV // KV_RANK)),
                  _vec_spec(Q_RANK_PAD), _vec_spec(KV_RANK), tab, tab, tab, pl.BlockSpec(memory_space=pl.ANY)],
        out_specs=[pl.BlockSpec((ts, PROJ_Z), lambda i: (i, 0)), _vec_spec(Q_RANK_PAD), _vec_spec(KV_RANK)],
        out_shape=[jax.ShapeDtypeStruct((s, PROJ_W), BF16), jax.ShapeDtypeStruct((1, Q_RANK_PAD), F32),
                   jax.ShapeDtypeStruct((1, KV_RANK), F32)],
        input_output_aliases={10: 0},
        compiler_params=_params(("arbitrary",)),
    )(dqn, dkvn, dkr_heads, proj, proj, gq, gkv, ck, s1k, s2k, dproj)


def _ada_mod(cond_raw, ada_w, bias_my, name):
    nl, d, ncol = ada_w.shape
    tk = _pick(d, 512)
    nk = d // tk

    def body(c_ref, w_ref, b_ref, o_ref, acc_ref):
        kk = pl.program_id(1)

        @pl.when(kk == 0)
        def _():
            acc_ref[...] = jnp.zeros_like(acc_ref)

        cv = c_ref[...]
        cond = (cv * _sigmoid(cv)).astype(BF16)
        acc_ref[...] += jnp.dot(cond, w_ref[...].astype(BF16), preferred_element_type=F32)

        @pl.when(kk == nk - 1)
        def _():
            o_ref[...] = acc_ref[...] + b_ref[...]

    return _pcall(
        body, name=name, grid=(nl, nk),
        in_specs=[pl.BlockSpec((N_DEV, tk), lambda l, kk: (0, kk)),
                  pl.BlockSpec((None, tk, ncol), lambda l, kk: (l, kk, 0)),
                  pl.BlockSpec((None, 1, ncol), lambda l, kk: (l, 0, 0))],
        out_specs=pl.BlockSpec((None, N_DEV, ncol), lambda l, kk: (l, 0, 0)),
        out_shape=jax.ShapeDtypeStruct((nl, N_DEV, ncol), F32),
        scratch_shapes=[pltpu.VMEM((N_DEV, ncol), F32)],
        compiler_params=_params(("parallel", "arbitrary")),
    )(cond_raw, ada_w, bias_my.reshape(nl, 1, ncol))


def _adam(w, g, m, v):
    m = ADAM_B1 * m + (1.0 - ADAM_B1) * g
    v = ADAM_B2 * v + (1.0 - ADAM_B2) * (g * g)
    m_hat = m / (1.0 - ADAM_B1 ** ADAM_STEP)
    v_hat = v / (1.0 - ADAM_B2 ** ADAM_STEP)
    delta = -ADAM_LR * (m_hat / (jnp.sqrt(v_hat) + ADAM_EPS) + ADAM_WD * w)
    return delta, m, v


def _ada_bwd_adam(cond_t, dmod_cols, w, m, v, name):
    nl, d, ncol = w.shape
    tk = _pick(d, 512)

    def body(c_ref, dm_ref, w_ref, m_ref, v_ref, g_ref, d_ref, mo_ref, vo_ref):
        cv = c_ref[...]
        cond = (cv * _sigmoid(cv)).astype(BF16)
        g = jnp.dot(cond, dm_ref[...].astype(BF16), preferred_element_type=F32)
        delta, m2, v2 = _adam(w_ref[...], g, m_ref[...], v_ref[...])
        g_ref[...] = g
        d_ref[...] = delta
        mo_ref[...] = m2
        vo_ref[...] = v2

    blk = pl.BlockSpec((None, tk, ncol), lambda l, kk: (l, kk, 0))
    shp = jax.ShapeDtypeStruct((nl, d, ncol), F32)
    return _pcall(
        body, name=name, grid=(nl, d // tk),
        in_specs=[pl.BlockSpec((tk, N_DEV), lambda l, kk: (kk, 0)),
                  pl.BlockSpec((None, N_DEV, ncol), lambda l, kk: (l, 0, 0)), blk, blk, blk],
        out_specs=[blk, blk, blk, blk], out_shape=[shp, shp, shp, shp],
        compiler_params=_params(("parallel", "parallel")),
    )(cond_t, dmod_cols, w, m, v)


def _adam_reduce(recv0, recv1, w, m, v, name):
    nl, r, c = w.shape
    tr = _pick(r, 128) if r % 128 == 0 else r
    tc = _pick(c, 1024)
    n0, n1 = recv0.shape[0] // r, recv1.shape[0] // r

    def body(r0_ref, r1_ref, w_ref, m_ref, v_ref, g_ref, d_ref, mo_ref, vo_ref):
        l = pl.program_id(0)

        def run(rr):
            g = rr[0].astype(F32)
            for sidx in range(1, rr.shape[0]):
                g = g + rr[sidx].astype(F32)
            delta, m2, v2 = _adam(w_ref[...], g, m_ref[...], v_ref[...])
            g_ref[...] = g
            d_ref[...] = delta
            mo_ref[...] = m2
            vo_ref[...] = v2

        @pl.when(l == 0)
        def _():
            run(r0_ref)

        @pl.when(l == 1)
        def _():
            run(r1_ref)

    def rblk(n, layer):
        return pl.BlockSpec((n, tr, tc), lambda l, i, j: (0, jnp.where(l == layer, i, 0), jnp.where(l == layer, j, 0)))

    blk = pl.BlockSpec((None, tr, tc), lambda l, i, j: (l, i, j))
    shp = jax.ShapeDtypeStruct((nl, r, c), F32)
    return _pcall(
        body, name=name, grid=(nl, r // tr, c // tc),
        in_specs=[rblk(n0, 0), rblk(n1, 1), blk, blk, blk],
        out_specs=[blk, blk, blk, blk], out_shape=[shp, shp, shp, shp],
        compiler_params=_params(("arbitrary", "parallel", "parallel")),
    )(recv0.reshape(n0, r, c), recv1.reshape(n1, r, c), w, m, v)


def _adam_small(gathered, w, m, v, name):
    r = w.shape[0]
    tr = _pick(r, 512) if r % 512 == 0 else r

    def body(p_ref, w_ref, m_ref, v_ref, g_ref, d_ref, mo_ref, vo_ref):
        g = p_ref[0]
        for sidx in range(1, N_DEV):
            g = g + p_ref[sidx]
        delta, m2, v2 = _adam(w_ref[...], g, m_ref[...], v_ref[...])
        g_ref[...] = g
        d_ref[...] = delta
        mo_ref[...] = m2
        vo_ref[...] = v2

    blk = pl.BlockSpec((tr, LANE), lambda i: (i, 0))
    shp = jax.ShapeDtypeStruct((r, LANE), F32)
    return _pcall(
        body, name=name, grid=(r // tr,),
        in_specs=[pl.BlockSpec((N_DEV, tr, LANE), lambda i: (0, i, 0)), blk, blk, blk],
        out_specs=[blk, blk, blk, blk], out_shape=[shp, shp, shp, shp],
        compiler_params=_params(("parallel",)),
    )(gathered, w, m, v)


def _my_place():
    x, y, c = lax.axis_index("x"), lax.axis_index("y"), lax.axis_index("c")
    return x, y, c, 4 * x + 2 * y + c


def _peer(x, y, c, k):
    px = 1 - x if (k >> 2) & 1 else x
    py = 1 - y if (k >> 1) & 1 else y
    pc = 1 - c if k & 1 else c
    return (px, py, pc), 4 * px + 2 * py + pc


def _slab(ref, shape, kind, p):
    r, cd = shape
    if kind == "row":
        return ref.at[pl.ds(pl.multiple_of(p * r, SUBLANE), r), :]
    return ref.at[:, pl.ds(pl.multiple_of(p * cd, LANE), cd)]


def _exchange_layout(arrays, kinds, gather):
    shard_shapes, dst_kinds, out_shapes = [], [], []
    for a, kind in zip(arrays, kinds):
        r, cd = a.shape
        if gather:
            shard, dst_kind = (r, cd), kind
        else:
            shard, dst_kind = ((r // N_DEV, cd) if kind == "row" else (r, cd // N_DEV)), "row"
        shard_shapes.append(shard)
        dst_kinds.append(dst_kind)
        full = (shard[0] * N_DEV, shard[1]) if dst_kind == "row" else (shard[0], shard[1] * N_DEV)
        out_shapes.append(jax.ShapeDtypeStruct(full, a.dtype))
    return shard_shapes, dst_kinds, out_shapes


def _exchange_copies(ins, outs, send_sems, recv_sems, sem_of, layout, kinds, gather):
    shard_shapes, dst_kinds, _ = layout
    x, y, c, me = _my_place()

    def src_for(a, p):
        return ins[a] if gather else _slab(ins[a], shard_shapes[a], kinds[a], p)

    def dst_slot(a, p):
        return _slab(outs[a], shard_shapes[a], dst_kinds[a], p)

    def local(a, sem):
        return pltpu.make_async_copy(src_for(a, me), dst_slot(a, me), sem)

    def remote(a, k, slot):
        peer, pidx = _peer(x, y, c, k)
        return pltpu.make_async_remote_copy(
            src_ref=src_for(a, pidx), dst_ref=dst_slot(a, me if slot == "mine" else pidx),
            send_sem=send_sems.at[sem_of(a, k)], recv_sem=recv_sems.at[sem_of(a, k)],
            device_id=peer, device_id_type=MESH)

    return local, remote


def _place_own(src, src_kind, slab_shape, dst_kind, full, name, index=None):
    r, cd = slab_shape
    tr = _pick(r, 512)
    nr = r // tr
    me = _my_place()[3] if index is None else index
    src, layer = src if isinstance(src, tuple) else (src, None)

    def body(me_ref, s_ref, o_ref):
        o_ref[...] = s_ref[...].astype(o_ref.dtype)

    def where(kind):
        if kind is None:
            return lambda i, me_ref: (i, 0)
        if kind == "row":
            return lambda i, me_ref: (me_ref[0] * nr + i, 0)
        return lambda i, me_ref: (i, me_ref[0])

    if layer is None:
        src_spec = pl.BlockSpec((tr, cd), where(src_kind))
    else:
        src_spec = pl.BlockSpec((None, tr, cd), lambda i, me_ref: (layer, *where(src_kind)(i, me_ref)))
    return _pcall(
        body, name=name,
        grid_spec=pltpu.PrefetchScalarGridSpec(
            num_scalar_prefetch=1, grid=(nr,),
            in_specs=[src_spec],
            out_specs=pl.BlockSpec((tr, cd), where(dst_kind))),
        out_shape=jax.ShapeDtypeStruct(full.shape, full.dtype),
        compiler_params=_params(("arbitrary",)),
    )(jnp.reshape(me, (1,)).astype(jnp.int32), src)


def _landing_with_own_slab(arrays, kinds, gather, layout, name=None, order_after=None):
    _, _, _, me = _my_place()
    index = None
    if order_after is not None:
        first = order_after.reshape(-1)[0].astype(jnp.int32)
        index = me + jnp.minimum(jnp.maximum(first, 0), 0)
    lands = []
    for a in range(len(arrays)):
        (r, cd), dst_kind, full = layout[0][a], layout[1][a], layout[2][a]
        if name is not None:
            lands.append(_place_own(arrays[a], None if gather else kinds[a], (r, cd), dst_kind, full, name,
                                    index=index))
            continue
        if gather:
            piece = arrays[a]
        elif kinds[a] == "row":
            piece = lax.dynamic_slice_in_dim(arrays[a], me * r, r, axis=0)
        else:
            piece = lax.dynamic_slice_in_dim(arrays[a], me * cd, cd, axis=1)
        at = (me * r, 0) if dst_kind == "row" else (0, me * cd)
        lands.append(lax.dynamic_update_slice(lax.empty(full.shape, full.dtype), piece, at))
    return lands


def _exchange(arrays, kinds, gather, name, after=None):
    n = len(arrays)
    n_extra = 0 if after is None else 1
    layout = _exchange_layout(arrays, kinds, gather)
    lands = _landing_with_own_slab(arrays, kinds, gather, layout)

    def body(*refs):
        ins, outs = refs[:n], refs[2 * n + n_extra:3 * n + n_extra]
        send_sems, recv_sems = refs[3 * n + n_extra:]
        _, remote = _exchange_copies(ins, outs, send_sems, recv_sems,
                                     lambda a, k: a * (N_DEV - 1) + k - 1, layout, kinds, gather)
        for a in range(n):
            for k in range(1, N_DEV):
                remote(a, k, "mine").start()
        for a in range(n):
            for k in range(1, N_DEV):
                arrival = remote(a, k, "theirs")
                arrival.wait_send()
                arrival.wait_recv()

    anyspec = pl.BlockSpec(memory_space=pl.ANY)
    outs = _pcall(
        body, comm=True, name=name,
        in_specs=[anyspec] * (2 * n + n_extra), out_specs=[anyspec] * n, out_shape=layout[2],
        input_output_aliases={n + a: a for a in range(n)},
        scratch_shapes=[pltpu.SemaphoreType.DMA((n * (N_DEV - 1),)), pltpu.SemaphoreType.DMA((n * (N_DEV - 1),))],
    )(*arrays, *lands, *([] if after is None else [after]))
    return list(outs)


HBM_SPEC = pl.BlockSpec(memory_space=pltpu.HBM)
SEM_SPEC = pl.BlockSpec(memory_space=pltpu.SEMAPHORE)
ANY_SPEC = pl.BlockSpec(memory_space=pl.ANY)
DATAFLOW = pltpu.SideEffectType.DATAFLOW_SIDE_EFFECTING


def _exchange_start(arrays, kinds, gather, name, after, carry=()):
    n, nc = len(arrays), len(carry)
    layout = _exchange_layout(arrays, kinds, gather)
    lands = _landing_with_own_slab(arrays, kinds, gather, layout, "place_own")

    def body(*refs):
        ins, outs = refs[:n], refs[n:2 * n]
        send_sems, recv_sems = refs[2 * n + nc + 1], refs[2 * n + nc + 2]
        token = refs[2 * n + nc + 3 + 2 * n + nc]
        _, remote = _exchange_copies(ins, outs, send_sems, recv_sems, lambda a, k: a, layout, kinds, gather)
        for a in range(n):
            for k in range(1, N_DEV):
                remote(a, k, "mine").start()
        token[...] = jnp.zeros_like(token)

    passed = list(arrays) + lands + list(carry)
    res = pl.pallas_call(
        body, name=name,
        out_shape=(pltpu.SemaphoreType.DMA((n,)), pltpu.SemaphoreType.DMA((n,)),
                   *[pltpu.HBM(t.shape, t.dtype) for t in passed], jax.ShapeDtypeStruct((SUBLANE, LANE), F32)),
        in_specs=[HBM_SPEC] * (2 * n + nc) + [ANY_SPEC],
        out_specs=(SEM_SPEC, SEM_SPEC, *([HBM_SPEC] * (2 * n + nc)), pl.BlockSpec(memory_space=pltpu.VMEM)),
        input_output_aliases={i: 2 + i for i in range(2 * n + nc)},
        compiler_params=pltpu.CompilerParams(has_side_effects=DATAFLOW),
    )(*[pltpu.with_memory_space_constraint(t, pltpu.HBM) for t in passed], after)
    handle = (res[0], res[1], list(res[2:2 + n]), list(res[2 + n:2 + 2 * n]), tuple(kinds), gather)
    return handle, res[-1], list(res[2 + 2 * n:2 + 2 * n + nc])


def _exchange_wait(handle, name, after):
    send_sems, recv_sems, ins_thru, lands_thru, kinds, gather = handle
    n = len(ins_thru)
    layout = _exchange_layout(ins_thru, kinds, gather)

    def body(*refs):
        ins, outs = refs[:n], refs[n:2 * n]
        s_sems, r_sems = refs[2 * n], refs[2 * n + 1]
        _, remote = _exchange_copies(ins, outs, s_sems, r_sems, lambda a, k: a, layout, kinds, gather)
        for a in range(n):
            for k in range(1, N_DEV):
                arrival = remote(a, k, "theirs")
                arrival.wait_send()
                arrival.wait_recv()

    res = pl.pallas_call(
        body, name=name,
        out_shape=[pltpu.HBM(t.shape, t.dtype) for t in ins_thru + lands_thru],
        in_specs=[HBM_SPEC] * (2 * n) + [SEM_SPEC, SEM_SPEC, ANY_SPEC],
        out_specs=[HBM_SPEC] * (2 * n),
        input_output_aliases={i: i for i in range(2 * n)},
        compiler_params=pltpu.CompilerParams(has_side_effects=DATAFLOW),
    )(*ins_thru, *lands_thru, send_sems, recv_sems, after)
    return list(res[n:2 * n])


STAGE1_KS = (1, 2, 4, 6)
FORWARD_KS = (2, 4, 6)


def _gather2_copies(lands, shard_shapes, kinds):
    x, y, c, me = _my_place()

    def slab(a, p):
        return _slab(lands[a], shard_shapes[a], kinds[a], p)

    def stage1(a, k, sems, arriving):
        peer, pidx = _peer(x, y, c, k)
        s = slab(a, pidx if arriving else me)
        return pltpu.make_async_remote_copy(src_ref=s, dst_ref=s, send_sem=sems[0].at[a], recv_sem=sems[1].at[a],
                                            device_id=peer, device_id_type=MESH)

    def stage2(a, k, sems, arriving):
        sib, _ = _peer(x, y, c, 1)
        _, mine = _peer(x, y, c, k)
        _, theirs = _peer(x, y, 1 - c, k)
        s = slab(a, theirs if arriving else mine)
        return pltpu.make_async_remote_copy(src_ref=s, dst_ref=s, send_sem=sems[0].at[a], recv_sem=sems[1].at[a],
                                            device_id=sib, device_id_type=MESH)

    return stage1, stage2


def _gather2_call(lands, sems_in, name, after, make_body, returns_sems):
    n = len(lands)
    n_in = len(sems_in)

    def body(*refs):
        land_refs = refs[:n]
        in_sems = refs[n:n + n_in]
        rest = refs[n + n_in + 1:]
        out_sems = rest[:2] if returns_sems else ()
        make_body(land_refs, in_sems, out_sems)
        if returns_sems:
            token = rest[2 + n]
            token[...] = jnp.zeros_like(token)

    sem_shapes = (pltpu.SemaphoreType.DMA((n,)), pltpu.SemaphoreType.DMA((n,))) if returns_sems else ()
    tok_shape = (jax.ShapeDtypeStruct((SUBLANE, LANE), F32),) if returns_sems else ()
    n_sem_out = len(sem_shapes)
    res = pl.pallas_call(
        body, name=name,
        out_shape=(*sem_shapes, *[pltpu.HBM(t.shape, t.dtype) for t in lands], *tok_shape),
        in_specs=[HBM_SPEC] * n + [SEM_SPEC] * n_in + [ANY_SPEC],
        out_specs=(*([SEM_SPEC] * n_sem_out), *([HBM_SPEC] * n),
                   *([pl.BlockSpec(memory_space=pltpu.VMEM)] if returns_sems else [])),
        input_output_aliases={i: n_sem_out + i for i in range(n)},
        compiler_params=pltpu.CompilerParams(has_side_effects=DATAFLOW),
    )(*[pltpu.with_memory_space_constraint(t, pltpu.HBM) for t in lands], *sems_in, after)
    sems_out = tuple(res[:n_sem_out])
    lands_thru = list(res[n_sem_out:n_sem_out + n])
    return sems_out, lands_thru, (res[-1] if returns_sems else None)


def _gather2_start(shards, kinds, name, after):
    n = len(shards)
    views = [jax.ShapeDtypeStruct(t.shape[1:], t.dtype) for t, _ in shards]
    shard_shapes, dst_kinds, fulls = _exchange_layout(views, kinds, True)
    layout = (shard_shapes, dst_kinds, [jax.ShapeDtypeStruct(f.shape, BF16) for f in fulls])
    lands = _landing_with_own_slab(shards, kinds, True, layout, "place_own", order_after=after)

    def make_body(land_refs, in_sems, out_sems):
        stage1, _ = _gather2_copies(land_refs, layout[0], kinds)
        for a in range(n):
            for k in STAGE1_KS:
                stage1(a, k, out_sems, False).start()

    sems, lands, token = _gather2_call(lands, (), name, after, make_body, True)
    return (sems, lands, layout[0], tuple(kinds)), token


def _gather2_forward(handle, name, after, carry=()):
    sems1, lands, shard_shapes, kinds = handle
    n = len(lands)

    def make_body(land_refs, in_sems, out_sems):
        stage1, stage2 = _gather2_copies(land_refs, shard_shapes, kinds)
        for a in range(n):
            for k in STAGE1_KS:
                arrival = stage1(a, k, in_sems, True)
                arrival.wait_send()
                arrival.wait_recv()
        for a in range(n):
            for k in FORWARD_KS:
                stage2(a, k, out_sems, False).start()

    sems2, passed, token = _gather2_call(list(lands) + list(carry), sems1, name, after, make_body, True)
    return (sems2, passed[:n], shard_shapes, kinds), token, passed[n:]


def _gather2_wait(handle, name, after):
    sems2, lands, shard_shapes, kinds = handle
    n = len(lands)

    def make_body(land_refs, in_sems, out_sems):
        _, stage2 = _gather2_copies(land_refs, shard_shapes, kinds)
        for a in range(n):
            for k in FORWARD_KS:
                arrival = stage2(a, k, in_sems, True)
                arrival.wait_send()
                arrival.wait_recv()

    _, lands, _ = _gather2_call(lands, sems2, name, after, make_body, False)
    return lands


N_CHIP = N_DEV // 2


def _scatter2_pair(g, name):
    r, c8 = g.shape
    cd = c8 // N_DEV

    def body(g_ref, o_ref, send_sems, recv_sems):
        x, y, c, _ = _my_place()
        sib, _ = _peer(x, y, c, 1)

        def copy(ch):
            theirs = 2 * ch + (1 - c)
            return pltpu.make_async_remote_copy(
                src_ref=_slab(g_ref, (r, cd), "col", theirs), dst_ref=_slab(o_ref, (r, cd), "row", ch),
                send_sem=send_sems.at[ch], recv_sem=recv_sems.at[ch], device_id=sib, device_id_type=MESH)

        for ch in range(N_CHIP):
            copy(ch).start()
        for ch in range(N_CHIP):
            copy(ch).wait_send()
            copy(ch).wait_recv()

    return _pcall(
        body, comm=True, name=name,
        in_specs=[ANY_SPEC], out_specs=ANY_SPEC, out_shape=jax.ShapeDtypeStruct((N_CHIP * r, cd), g.dtype),
        scratch_shapes=[pltpu.SemaphoreType.DMA((N_CHIP,)), pltpu.SemaphoreType.DMA((N_CHIP,))],
    )(g)


def _scatter2_add(g, from_sibling, name):
    r, c8 = g.shape
    cd = c8 // N_DEV
    tr = _pick(r, 512)
    nr = r // tr
    _, _, core, _ = _my_place()

    def body(c_ref, g_ref, s_ref, o_ref):
        o_ref[...] = (g_ref[...].astype(F32) + s_ref[...].astype(F32)).astype(o_ref.dtype)

    return _pcall(
        body, name=name,
        grid_spec=pltpu.PrefetchScalarGridSpec(
            num_scalar_prefetch=1, grid=(N_CHIP, nr),
            in_specs=[pl.BlockSpec((tr, cd), lambda ch, i, c_ref: (i, 2 * ch + c_ref[0])),
                      pl.BlockSpec((tr, cd), lambda ch, i, c_ref: (ch * nr + i, 0))],
            out_specs=pl.BlockSpec((tr, cd), lambda ch, i, c_ref: (ch * nr + i, 0))),
        out_shape=jax.ShapeDtypeStruct((N_CHIP * r, cd), g.dtype),
        compiler_params=_params(("arbitrary", "arbitrary")),
    )(jnp.reshape(core, (1,)).astype(jnp.int32), g, from_sibling)


def _scatter2_copies(refs, r):
    part_ref, recv_ref = refs
    x, y, c, _ = _my_place()
    my_chip = 2 * x + y

    def rows(ref, ch):
        return ref.at[pl.ds(pl.multiple_of(ch * r, SUBLANE), r), :]

    def copy(k, sems, arriving):
        peer, _ = _peer(x, y, c, k)
        peer_chip = 2 * peer[0] + peer[1]
        return pltpu.make_async_remote_copy(
            src_ref=rows(part_ref, peer_chip), dst_ref=rows(recv_ref, peer_chip if arriving else my_chip),
            send_sem=sems[0].at[0], recv_sem=sems[1].at[0], device_id=peer, device_id_type=MESH)

    return copy


def _scatter2_start(g, name, after, carry=()):
    r = g.shape[0]
    x, y, _, _ = _my_place()
    from_sibling = _scatter2_pair(g, name + "_pair")
    partial = _scatter2_add(g, from_sibling, name + "_add")
    cd = partial.shape[1]
    recv = _place_own(partial, "row", (r, cd), "row", jax.ShapeDtypeStruct(partial.shape, partial.dtype),
                      "place_own", index=2 * x + y)

    def make_body(land_refs, in_sems, out_sems):
        copy = _scatter2_copies(land_refs[:2], r)
        for k in FORWARD_KS:
            copy(k, out_sems, False).start()

    sems, lands, token = _gather2_call([partial, recv] + list(carry), (), name, after, make_body, True)
    return (sems, lands[:2], r), token, lands[2:]


def _scatter2_wait(handle, name, after):
    sems, lands, r = handle

    def make_body(land_refs, in_sems, out_sems):
        copy = _scatter2_copies(land_refs, r)
        for k in FORWARD_KS:
            arrival = copy(k, in_sems, True)
            arrival.wait_send()
            arrival.wait_recv()

    _, lands, _ = _gather2_call(lands, sems, name, after, make_body, False)
    return lands[1]


def _pad_mla_w_in(w):
    d = w.shape[0]
    z = lambda n: jnp.zeros((d, n), w.dtype)
    o1, o2, o3 = Q_RANK, Q_RANK + KV_RANK, Q_RANK + KV_RANK + ROPE
    return jnp.concatenate([w[:, :o1], z(Q_RANK_PAD - Q_RANK), w[:, o1:o2], w[:, o2:o3], z(LANE - ROPE), w[:, o3:]], axis=1)


def _unpad_mla_w_in(g):
    return jnp.concatenate([g[:, :Q_RANK], g[:, PROJ_CKV:PROJ_KR], g[:, PROJ_KR:PROJ_KR + ROPE], g[:, PROJ_Z:]], axis=1)


def _pad_w_uq(w):
    w3 = w.reshape(Q_RANK, HEADS, NOPE + ROPE)
    w3 = jnp.pad(w3, ((0, Q_RANK_PAD - Q_RANK), (0, 0), (0, HEAD_PAD - NOPE - ROPE)))
    return w3.reshape(Q_RANK_PAD, HEADS * HEAD_PAD)


def _unpad_w_uq(g):
    return g[:Q_RANK].reshape(Q_RANK, HEADS, HEAD_PAD)[:, :, :NOPE + ROPE].reshape(Q_RANK, HEADS * (NOPE + ROPE))


def _pack(pieces):
    flat = [p.reshape(-1).astype(F32) for p in pieces]
    tot = sum(f.shape[0] for f in flat)
    unit = SUBLANE * LANE
    padn = (-tot) % unit
    if padn:
        flat.append(jnp.zeros((padn,), F32))
    return jnp.concatenate(flat).reshape(-1, LANE)


def _unpack(packed, shapes, lead=()):
    flat = packed.reshape(tuple(lead) + (-1,))
    out, off = [], 0
    for shp in shapes:
        nel = int(np.prod(shp))
        out.append(flat[..., off:off + nel].reshape(tuple(lead) + tuple(shp)))
        off += nel
    return out


SMALL_GROUPS = {
    "a": [('ada_b', lambda t: t[1:]), ('pre_g', lambda t: t[1:]), ('post_g', lambda t: t[1:]),
          ('sgu_norm_g', lambda t: t[1:]), ('sgu_w_s', lambda t: t[1:]), ('sgu_b_s', lambda t: t[1:]),
          ('mla_q_norm_g', lambda t: t), ('mla_kv_norm_g', lambda t: t)],
    "b": [('ada_b', lambda t: t[0:1, 2 * t.shape[1] // 3:]), ('post_g', lambda t: t[0:1]),
          ('sgu_norm_g', lambda t: t[0:1]), ('sgu_w_s', lambda t: t[0:1]), ('sgu_b_s', lambda t: t[0:1])],
    "c": [('ada_b', lambda t: t[0:1, :2 * t.shape[1] // 3]), ('pre_g', lambda t: t[0:1])],
}


WEIGHTS = ['ada_w', 'ada_b', 'pre_g', 'post_g', 'sgu_w_in', 'sgu_norm_g', 'sgu_w_s', 'sgu_b_s', 'sgu_w_out',
           'mla_w_in', 'mla_q_norm_g', 'mla_kv_norm_g', 'mla_w_uq', 'mla_w_ukv', 'mla_w_out']
INPUTS = ['x', 'c'] + WEIGHTS + ['loss_target'] + ['m_' + n for n in WEIGHTS] + ['v_' + n for n in WEIGHTS]


def kernel(x, c, ada_w, ada_b, pre_g, post_g, sgu_w_in, sgu_norm_g, sgu_w_s, sgu_b_s, sgu_w_out, mla_w_in, mla_q_norm_g, mla_kv_norm_g, mla_w_uq, mla_w_ukv, mla_w_out, loss_target, m_ada_w, m_ada_b, m_pre_g, m_post_g, m_sgu_w_in, m_sgu_norm_g, m_sgu_w_s, m_sgu_b_s, m_sgu_w_out, m_mla_w_in, m_mla_q_norm_g, m_mla_kv_norm_g, m_mla_w_uq, m_mla_w_ukv, m_mla_w_out, v_ada_w, v_ada_b, v_pre_g, v_post_g, v_sgu_w_in, v_sgu_norm_g, v_sgu_w_s, v_sgu_b_s, v_sgu_w_out, v_mla_w_in, v_mla_q_norm_g, v_mla_kv_norm_g, v_mla_w_uq, v_mla_w_ukv, v_mla_w_out):
    given = locals()
    A = {name: given[name] for name in INPUTS}
    x0 = A['x'][0]
    tgt = A['loss_target'][0]
    s, d = x0.shape
    e = 2 * d
    ncol = 3 * d // N_DEV
    _, _, _, me = _my_place()
    ktabs = _rope_tables(s)

    gains = jnp.zeros((SUBLANE, LANE), F32)
    gains = gains.at[0:2, :Q_RANK // N_DEV].set(A['mla_q_norm_g'])
    gains = gains.at[2:4, :KV_RANK // N_DEV].set(A['mla_kv_norm_g'])
    c8 = jnp.broadcast_to(A['c'], (SUBLANE, d))
    cg, gg = _exchange([c8, gains], ["row", "row"], True, "ag_cond")
    cond_raw = cg.reshape(N_DEV, SUBLANE, d)[:, 0, :]
    gg = gg.reshape(N_DEV, SUBLANE, LANE)
    gq_full = jnp.transpose(gg[:, 0:2, :Q_RANK // N_DEV], (1, 0, 2)).reshape(N_MIX, Q_RANK)
    gkv_full = jnp.transpose(gg[:, 2:4, :KV_RANK // N_DEV], (1, 0, 2)).reshape(N_MIX, KV_RANK)
    gq_pad = jnp.pad(gq_full, ((0, 0), (0, Q_RANK_PAD - Q_RANK)))

    bias_my = lax.dynamic_slice_in_dim(A['ada_b'], me * ncol, ncol, axis=1)
    mod_part = _ada_mod(cond_raw, A['ada_w'], bias_my, "ada_mod")
    send = jnp.pad(jnp.transpose(mod_part, (1, 0, 2)), ((0, 0), (0, SUBLANE - DEPTH), (0, 0)))
    (rb,) = _exchange([send.reshape(N_DEV * SUBLANE, ncol)], ["row"], False, "a2a_mod")

    token = rb
    gathers = {}
    for i in range(DEPTH):
        j = i // N_MIX
        if i % N_MIX == 0:
            parts = [("in", [(A['sgu_w_in'], j)], ["col"]), ("out", [(A['sgu_w_out'], j)], ["row"])]
        else:
            parts = [("all", [(A['mla_w_in'], j), (A['mla_w_uq'], j), (A['mla_w_ukv'], j), (A['mla_w_out'], j)],
                      ["col", "col", "col", "row"])]
        for part, shards, kinds in parts:
            gathers[(i, part)], token = _gather2_start(shards, kinds, f"ag_start_{i}_{part}", token)

    def forward_gathers(i, carried):
        for key in [k for k in gathers if k[0] == i]:
            gathers[key], _, (carried,) = _gather2_forward(gathers[key], f"ag_forward_{key[0]}_{key[1]}", token,
                                                           carry=[carried])
        return carried

    mod = jnp.transpose(rb.reshape(N_DEV, SUBLANE, ncol)[:, :DEPTH, :], (1, 0, 2)).reshape(DEPTH, 3 * d) + token[0, 0]
    shift = [mod[i:i + 1, :d] for i in range(DEPTH)]
    scale = [mod[i:i + 1, d:2 * d] for i in range(DEPTH)]
    gate = [mod[i:i + 1, 2 * d:] for i in range(DEPTH)]

    saved = []
    x = x0
    for i in range(DEPTH):
        j = i // N_MIX
        h = _pre_fwd(x, A['pre_g'][i:i + 1], scale[i], shift[i], f"pre_fwd")
        if i == 0:
            gathers[(0, "in")], _, (h,) = _gather2_forward(gathers[(0, "in")], "ag_forward_0_in", token, carry=[h])
        if i % N_MIX == 0:
            (w_in,) = _gather2_wait(gathers[(i, "in")], f"ag_wait_{i}_in", h)
            uvz = _mm(h, w_in, "nn", F32, "sgu_in")
            if i == 0:
                gathers[(0, "out")], _, (uvz,) = _gather2_forward(gathers[(0, "out")], "ag_forward_0_out", token,
                                                                  carry=[uvz])
            bias_full = jnp.repeat(A['sgu_b_s'][j].T, e // SGU_GROUPS, axis=1)
            ng = A['sgu_norm_g'][j:j + 1]
            y = _sgu_mid_fwd(uvz, ng, A['sgu_w_s'][j], bias_full, "sgu_mid_fwd")
            y = forward_gathers(i + 1, y)
            (w_out,) = _gather2_wait(gathers[(i, "out")], f"ag_wait_{i}_out", y)
            out = _mm(y, w_out, "nn", F32, "sgu_out")
            saved.append(dict(x=x, h=h, uvz=uvz, y=y, out=out, w_in=w_in, w_out=w_out, bias=bias_full, ng=ng))
        else:
            w_in, w_uq, w_ukv, w_out = _gather2_wait(gathers[(i, "all")], f"ag_wait_{i}_all", h)
            w_in = _pad_mla_w_in(w_in)
            w_uq = _pad_w_uq(w_uq)
            gq, gkv = gq_pad[j:j + 1], gkv_full[j:j + 1]
            proj = _mm(h, w_in, "nn", F32, "mla_in")
            qn, kvn, kr = _mla_norm_fwd(proj, gq, gkv, ktabs, "mla_norm_fwd")
            q = _mm(qn, w_uq, "nn", F32, "mla_uq")
            kv = _mm(kvn, w_ukv, "nn", BF16, "mla_ukv")
            o, y, lse, q_cat = _attn_fwd(q, kv, kr, proj, ktabs, "attn_fwd")
            y = forward_gathers(i + 1, y)
            out = _mm(y, w_out, "nn", F32, "mla_out")
            saved.append(dict(x=x, h=h, proj=proj, qn=qn, kvn=kvn, kr=kr, q_cat=q_cat, kv=kv, o=o, y=y, lse=lse,
                              out=out, w_in=w_in, w_uq=w_uq, w_ukv=w_ukv, w_out=w_out, gq=gq, gkv=gkv))
        x = _post_fwd(x, out, gate[i], A['post_g'][i:i + 1], "post_fwd")

    dx, loss_row = _loss_head(x, tgt, "loss_head")
    loss = lax.psum(loss_row[0, 0], ("x", "y", "c"))

    d_shift, d_scale, d_gate = [None] * DEPTH, [None] * DEPTH, [None] * DEPTH
    d_pre, d_post = [None] * DEPTH, [None] * DEPTH
    scatters, scatters_out = [None] * DEPTH, [None] * DEPTH
    small, small_grads, small_handles = {}, {}, {}
    for i in reversed(range(DEPTH)):
        j = i // N_MIX
        sv = saved[i]
        dout, d_gate[i], d_post[i] = _post_bwd(dx, sv['out'], gate[i], A['post_g'][i:i + 1], "post_bwd")
        if i % N_MIX == 0:
            dy = _mm(dout, sv['w_out'], "nt", F32, "sgu_out_dx")
            g_w_out = _mm(sv['y'], dout, "tn", BF16, "sgu_out_dw")
            scatters_out[i], token, (dy,) = _exchange_start([g_w_out], ["row"], False, f"rs_out_start_{i}", token,
                                                            carry=[dy])
            duvz, dws, dbs, dng = _sgu_mid_bwd(sv['uvz'], dy, sv['ng'], A['sgu_w_s'][j], sv['bias'], "sgu_mid_bwd")
            small[('sgu', j)] = (dws, dbs.reshape(SGU_GROUPS, SGU_BLOCK), dng)
            g_w_in = _mm(sv['h'], duvz, "tn", BF16, "sgu_in_dw")
            if i == 0:
                scatters[i], token, (duvz,) = _scatter2_start(g_w_in, "rs2_start_0", token, carry=[duvz])
                small_grads["b"] = [d_gate[0], d_post[0], dng, dws[None], small[('sgu', 0)][1][None]]
                small_handles["b"], token, (duvz,) = _exchange_start(
                    [_pack(small_grads["b"])], ["row"], True, "ag_small_b_start", token, carry=[duvz])
            else:
                scatters[i], token, (duvz,) = _exchange_start([g_w_in], ["col"], False, f"rs_start_{i}", token,
                                                              carry=[duvz])
            dh = _mm(duvz, sv['w_in'], "nt", F32, "sgu_in_dx")
        else:
            dy = _mm(dout, sv['w_out'], "nt", F32, "mla_out_dx")
            g_w_out = _mm(sv['y'], dout, "tn", BF16, "mla_out_dw")
            scatters_out[i], token, (dy,) = _exchange_start([g_w_out], ["row"], False, f"rs_out_start_{i}", token,
                                                            carry=[dy])
            do, dproj = _mla_gate_bwd(dy, sv['o'], sv['proj'], "mla_gate_bwd")
            dq_b, dkv, dkr_heads = _attn_bwd(sv['q_cat'], sv['kv'], sv['kr'], do, sv['o'], sv['lse'], ktabs,
                                             "attn_bwd")
            dqn = _mm(dq_b, sv['w_uq'], "nt", F32, "mla_uq_dx")
            g_w_uq = _unpad_w_uq(_mm(sv['qn'], dq_b, "tn", BF16, "mla_uq_dw"))
            dkvn = _mm(dkv, sv['w_ukv'], "nt", F32, "mla_ukv_dx")
            g_w_ukv = _mm(sv['kvn'], dkv, "tn", BF16, "mla_ukv_dw")
            dproj, dgq, dgkv = _mla_norm_bwd(dqn, dkvn, dkr_heads, sv['proj'], sv['gq'], sv['gkv'], ktabs, dproj,
                                             "mla_norm_bwd")
            g_w_in = _unpad_mla_w_in(_mm(sv['h'], dproj, "tn", BF16, "mla_in_dw"))
            scatters[i], token, (dproj,) = _exchange_start(
                [g_w_in, g_w_uq, g_w_ukv], ["col", "col", "col"], False, f"rs_start_{i}", token, carry=[dproj])
            dh = _mm(dproj, sv['w_in'], "nt", F32, "mla_in_dx")
            small[('mla', j)] = (dgq[:, :Q_RANK], dgkv)
        dx, d_shift[i], d_scale[i], d_pre[i] = _pre_bwd(dh, sv['x'], dx, A['pre_g'][i:i + 1], scale[i], "pre_bwd")
        if i == 1:
            small_grads["a"] = [
                jnp.concatenate([jnp.concatenate([d_shift[l], d_scale[l], d_gate[l]], axis=1)
                                 for l in range(1, DEPTH)], axis=0),
                jnp.concatenate(d_pre[1:], axis=0), jnp.concatenate(d_post[1:], axis=0),
                small[('sgu', 1)][2], small[('sgu', 1)][0][None], small[('sgu', 1)][1][None],
                jnp.concatenate([small[('mla', jj)][0] for jj in range(N_MIX)], axis=0),
                jnp.concatenate([small[('mla', jj)][1] for jj in range(N_MIX)], axis=0)]
            small_handles["a"], token, (dx,) = _exchange_start(
                [_pack(small_grads["a"])], ["row"], True, "ag_small_a_start", token, carry=[dx])

    res = {}

    def big(name, recv0, recv1):
        res[name] = _adam_reduce(recv0, recv1, A[name], A['m_' + name], A['v_' + name], "adam_" + name)

    def small_full(name, prefix):
        t = A[prefix + name]
        if name in ('mla_q_norm_g', 'mla_kv_norm_g'):
            t = lax.dynamic_update_slice_in_dim(jnp.zeros((t.shape[0], t.shape[1] * N_DEV), F32), t,
                                                me * t.shape[1], axis=1)
        return t

    def finish(i, after):
        first = ([_scatter2_wait(scatters[i], "rs2_wait_0", after)] if i == 0
                 else _exchange_wait(scatters[i], f"rs_wait_{i}", after))
        return first + _exchange_wait(scatters_out[i], f"rs_out_wait_{i}", after)

    recv_mla = {j: finish(N_MIX * j + 1, dx) for j in reversed(range(N_MIX))}
    for idx_w, name in enumerate(['mla_w_in', 'mla_w_uq', 'mla_w_ukv', 'mla_w_out']):
        big(name, recv_mla[0][idx_w], recv_mla[1][idx_w])

    recv_sgu = {j: finish(N_MIX * j, res['mla_w_out'][0]) for j in reversed(range(N_MIX))}
    for idx_w, name in enumerate(['sgu_w_in', 'sgu_w_out']):
        big(name, recv_sgu[0][idx_w], recv_sgu[1][idx_w])

    small_grads["c"] = [jnp.concatenate([d_shift[0], d_scale[0]], axis=1), d_pre[0]]
    gathered = {tag: _exchange_wait(small_handles[tag], f"ag_small_{tag}_wait", res['sgu_w_in'][0])[0]
                for tag in ("a", "b")}
    (gathered["c"],) = _exchange([_pack(small_grads["c"])], ["row"], True, "ag_small_c", after=res['sgu_w_in'][0])

    upd, parts = {}, {}
    for tag, group in SMALL_GROUPS.items():
        shapes = [g.shape for g in small_grads[tag]]
        packs = [_pack([pick(small_full(name, prefix)) for name, pick in group]) for prefix in ('', 'm_', 'v_')]
        outs4 = _adam_small(gathered[tag].reshape(N_DEV, -1, LANE), *packs, "adam_small_" + tag)
        upd[tag] = [_unpack(t, shapes) for t in outs4]
        parts[tag] = _unpack(gathered[tag].reshape(N_DEV, -1), shapes, lead=(N_DEV,))
    for k_out in range(4):
        ab_rest, pre_rest, post_rest, ng_1, ws_1, bs_1, gq_all, gkv_all = upd["a"][k_out]
        ab_gate0, post_0, ng_0, ws_0, bs_0 = upd["b"][k_out]
        ab_0, pre_0 = upd["c"][k_out]
        for name, val in (
                ('ada_b', jnp.concatenate([jnp.concatenate([ab_0, ab_gate0], axis=1), ab_rest], axis=0)),
                ('pre_g', jnp.concatenate([pre_0, pre_rest], axis=0)),
                ('post_g', jnp.concatenate([post_0, post_rest], axis=0)),
                ('sgu_norm_g', jnp.concatenate([ng_0, ng_1], axis=0)),
                ('sgu_w_s', jnp.concatenate([ws_0, ws_1], axis=0)),
                ('sgu_b_s', jnp.concatenate([bs_0, bs_1], axis=0)),
                ('mla_q_norm_g', gq_all), ('mla_kv_norm_g', gkv_all)):
            if name in ('mla_q_norm_g', 'mla_kv_norm_g'):
                wdt = A[name].shape[1]
                val = lax.dynamic_slice_in_dim(val, me * wdt, wdt, axis=1)
            res.setdefault(name, [None] * 4)[k_out] = val

    dmod_all = jnp.concatenate([jnp.concatenate([parts["c"][0], parts["b"][0]], axis=2), parts["a"][0]], axis=1)
    dmod_cols = jnp.transpose(lax.dynamic_slice_in_dim(dmod_all, me * ncol, ncol, axis=2), (1, 0, 2))
    res['ada_w'] = _ada_bwd_adam(jnp.transpose(cond_raw), dmod_cols, A['ada_w'], A['m_ada_w'], A['v_ada_w'], "ada_bwd")

    outs = [loss, dx[None]]
    for k_out in range(4):
        outs += [res[n][k_out] for n in WEIGHTS]
    return tuple(outs)
```

```python
import functools
import math

import numpy as np
import jax
import jax.numpy as jnp
from jax import lax
from jax.experimental import pallas as pl
from jax.experimental.pallas import tpu as pltpu

F32 = jnp.float32
BF16 = jnp.bfloat16
MESH = pl.DeviceIdType.MESH

N_DEV = 8
DEPTH = 4
N_MIX = 2
NORM_EPS = 1e-6
CHUNK = 64
SGU_BLOCK = 128
SGU_GROUPS = 16
HEADS = 16
Q_RANK = 448
Q_RANK_PAD = 512
KV_RANK = 512
NOPE = 128
ROPE = 64
HALF = ROPE // 2
V_DIM = 128
HEAD_PAD = 256
ROPE_THETA = 10000.0
MLA_WIDTH = HEADS * V_DIM
LANE = 128
SUBLANE = 8
PROJ_CQ = 0
PROJ_CKV = Q_RANK_PAD
PROJ_KR = Q_RANK_PAD + KV_RANK
PROJ_Z = PROJ_KR + LANE
PROJ_W = PROJ_Z + MLA_WIDTH

ADAM_LR = 0.001
ADAM_B1 = 0.9
ADAM_B2 = 0.999
ADAM_EPS = 1e-08
ADAM_WD = 0.01
ADAM_STEP = 10

VMEM_LIMIT = 56 * 1024 * 1024
ATT_BLK = 512
ATT_SUB = 128
ROW_BLK = 512
MM_TM, MM_TN, MM_TK = 1024, 1024, 2048
MM_TILE_BYTES = 40 * 1024 * 1024
SOFTMAX_SCALE = (NOPE + ROPE) ** -0.5
LOG2_E = 1.0 / math.log(2.0)
INV_SQRT2 = 1.0 / math.sqrt(2.0)
INV_SQRT_2PI = 1.0 / math.sqrt(2.0 * math.pi)


def _pcall(body, comm=False, **kw):
    return pl.pallas_call(body, **kw)


def _params(sem=None):
    return pltpu.CompilerParams(dimension_semantics=sem, vmem_limit_bytes=VMEM_LIMIT)


def _pick(dim, pref):
    if dim <= pref:
        return dim
    t = (pref // LANE) * LANE
    while t >= LANE:
        if dim % t == 0:
            return t
        t -= LANE
    return dim


def _gelu(x):
    return 0.5 * x * (1.0 + lax.erf(x * INV_SQRT2))


def _gelu_grad(x):
    return 0.5 * (1.0 + lax.erf(x * INV_SQRT2)) + x * jnp.exp(-0.5 * x * x) * INV_SQRT_2PI


def _sigmoid(x):
    return 1.0 / (1.0 + jnp.exp(-x))


def _dot_nt(a, b):
    return lax.dot_general(a, b, (((1,), (1,)), ((), ())), preferred_element_type=F32)


def _dot_tn(a, b):
    return lax.dot_general(a, b, (((0,), (0,)), ((), ())), preferred_element_type=F32)


def _mm(a, b, dims, out_dtype, name):
    if dims == "nn":
        (m, k), (k2, n) = a.shape, b.shape
    elif dims == "nt":
        (m, k), (n, k2) = a.shape, b.shape
    else:
        (k, m), (k2, n) = a.shape, b.shape
    assert k == k2, (a.shape, b.shape, dims)
    tm, tn = _pick(m, MM_TM), _pick(n, MM_TN)
    out_bytes = 2 * tm * tn * jnp.dtype(out_dtype).itemsize
    whole_k = 2 * (tm + tn) * k * a.dtype.itemsize + out_bytes <= MM_TILE_BYTES
    tk = k if whole_k else _pick(k, MM_TK)
    nk = k // tk

    def body(a_ref, b_ref, o_ref, *scratch):
        if dims == "nn":
            p = jnp.dot(a_ref[...], b_ref[...], preferred_element_type=F32)
        elif dims == "nt":
            p = _dot_nt(a_ref[...], b_ref[...])
        else:
            p = _dot_tn(a_ref[...], b_ref[...])
        if nk == 1:
            o_ref[...] = p.astype(o_ref.dtype)
            return
        acc_ref, = scratch
        kk = pl.program_id(2)

        @pl.when(kk == 0)
        def _():
            acc_ref[...] = p

        @pl.when(kk > 0)
        def _():
            acc_ref[...] += p

        @pl.when(kk == nk - 1)
        def _():
            o_ref[...] = acc_ref[...].astype(o_ref.dtype)

    if dims == "tn":
        a_spec = pl.BlockSpec((tk, tm), lambda i, j, kk: (kk, i))
    else:
        a_spec = pl.BlockSpec((tm, tk), lambda i, j, kk: (i, kk))
    if dims == "nt":
        b_spec = pl.BlockSpec((tn, tk), lambda i, j, kk: (j, kk))
    else:
        b_spec = pl.BlockSpec((tk, tn), lambda i, j, kk: (kk, j))
    return _pcall(
        body, name=name,
        grid=(m // tm, n // tn, nk),
        in_specs=[a_spec, b_spec],
        out_specs=pl.BlockSpec((tm, tn), lambda i, j, kk: (i, j)),
        out_shape=jax.ShapeDtypeStruct((m, n), out_dtype),
        scratch_shapes=[pltpu.VMEM((tm, tn), F32)] if nk > 1 else [],
        compiler_params=_params(("parallel", "parallel", "arbitrary")),
    )(a, b)


def _row_spec(ts, d):
    return pl.BlockSpec((ts, d), lambda i: (i, 0))


def _vec_spec(d):
    return pl.BlockSpec((1, d), lambda i: (0, 0))


def _pre_fwd(x, g, scale, shift, name):
    s, d = x.shape
    ts = _pick(s, ROW_BLK)

    def body(x_ref, g_ref, sc_ref, sh_ref, h_ref):
        xv = x_ref[...]
        r = lax.rsqrt(jnp.mean(xv * xv, axis=-1, keepdims=True) + NORM_EPS)
        h_ref[...] = ((xv * r * g_ref[...]) * (1.0 + sc_ref[...]) + sh_ref[...]).astype(BF16)

    return _pcall(
        body, name=name, grid=(s // ts,),
        in_specs=[_row_spec(ts, d), _vec_spec(d), _vec_spec(d), _vec_spec(d)],
        out_specs=_row_spec(ts, d),
        out_shape=jax.ShapeDtypeStruct((s, d), BF16),
        compiler_params=_params(("parallel",)),
    )(x, g, scale, shift)


def _post_fwd(x, out, gate, g, name):
    s, d = x.shape
    ts = _pick(s, ROW_BLK)

    def body(x_ref, o_ref, gate_ref, g_ref, y_ref):
        o = o_ref[...]
        r = lax.rsqrt(jnp.mean(o * o, axis=-1, keepdims=True) + NORM_EPS)
        y_ref[...] = x_ref[...] + gate_ref[...] * (o * r * g_ref[...])

    return _pcall(
        body, name=name, grid=(s // ts,),
        in_specs=[_row_spec(ts, d), _row_spec(ts, d), _vec_spec(d), _vec_spec(d)],
        out_specs=_row_spec(ts, d),
        out_shape=jax.ShapeDtypeStruct((s, d), F32),
        compiler_params=_params(("parallel",)),
    )(x, out, gate, g)


def _loss_head(xf, tgt, name):
    s, d = xf.shape
    ts = _pick(s, ROW_BLK)
    ns = s // ts

    def body(x_ref, t_ref, dx_ref, loss_ref, acc_ref):
        i = pl.program_id(0)

        @pl.when(i == 0)
        def _():
            acc_ref[...] = jnp.zeros_like(acc_ref)

        e = x_ref[...] - t_ref[...]
        dx_ref[...] = e * (1.0 / d)
        acc_ref[...] += jnp.sum(e * e, axis=0, keepdims=True)

        @pl.when(i == ns - 1)
        def _():
            tot = jnp.sum(acc_ref[...], axis=1, keepdims=True) * (0.5 / d)
            loss_ref[...] = jnp.broadcast_to(tot, loss_ref.shape)

    return _pcall(
        body, name=name, grid=(ns,),
        in_specs=[_row_spec(ts, d), _row_spec(ts, d)],
        out_specs=[_row_spec(ts, d), pl.BlockSpec((1, LANE), lambda i: (0, 0))],
        out_shape=[jax.ShapeDtypeStruct((s, d), F32), jax.ShapeDtypeStruct((1, LANE), F32)],
        scratch_shapes=[pltpu.VMEM((1, d), F32)],
        compiler_params=_params(("arbitrary",)),
    )(xf, tgt)


def _post_bwd(dxo, out, gate, g, name):
    s, d = dxo.shape
    ts = _pick(s, ROW_BLK)

    def body(dx_ref, o_ref, gate_ref, g_ref, do_ref, dgate_ref, dg_ref):
        i = pl.program_id(0)

        @pl.when(i == 0)
        def _():
            dgate_ref[...] = jnp.zeros_like(dgate_ref)
            dg_ref[...] = jnp.zeros_like(dg_ref)

        o = o_ref[...]
        dx = dx_ref[...]
        gv = g_ref[...]
        r = lax.rsqrt(jnp.mean(o * o, axis=-1, keepdims=True) + NORM_EPS)
        n = o * r
        dyn = dx * gate_ref[...]
        dgate_ref[...] += jnp.sum(dx * (n * gv), axis=0, keepdims=True)
        dg_ref[...] += jnp.sum(dyn * n, axis=0, keepdims=True)
        dn = dyn * gv
        do_ref[...] = (r * (dn - n * jnp.mean(dn * n, axis=-1, keepdims=True))).astype(BF16)

    return _pcall(
        body, name=name, grid=(s // ts,),
        in_specs=[_row_spec(ts, d), _row_spec(ts, d), _vec_spec(d), _vec_spec(d)],
        out_specs=[_row_spec(ts, d), _vec_spec(d), _vec_spec(d)],
        out_shape=[jax.ShapeDtypeStruct((s, d), BF16), jax.ShapeDtypeStruct((1, d), F32),
                   jax.ShapeDtypeStruct((1, d), F32)],
        compiler_params=_params(("arbitrary",)),
    )(dxo, out, gate, g)


def _pre_bwd(dh, x, dxo, g, scale, name):
    s, d = x.shape
    ts = _pick(s, ROW_BLK)

    def body(dh_ref, x_ref, dxo_ref, g_ref, sc_ref, dx_ref, dsh_ref, dsc_ref, dg_ref):
        i = pl.program_id(0)

        @pl.when(i == 0)
        def _():
            dsh_ref[...] = jnp.zeros_like(dsh_ref)
            dsc_ref[...] = jnp.zeros_like(dsc_ref)
            dg_ref[...] = jnp.zeros_like(dg_ref)

        xv = x_ref[...]
        dhv = dh_ref[...]
        gv = g_ref[...]
        one_sc = 1.0 + sc_ref[...]
        r = lax.rsqrt(jnp.mean(xv * xv, axis=-1, keepdims=True) + NORM_EPS)
        n = xv * r
        dsh_ref[...] += jnp.sum(dhv, axis=0, keepdims=True)
        dsc_ref[...] += jnp.sum(dhv * (n * gv), axis=0, keepdims=True)
        dng = dhv * one_sc
        dg_ref[...] += jnp.sum(dng * n, axis=0, keepdims=True)
        dn = dng * gv
        dx_ref[...] = dxo_ref[...] + r * (dn - n * jnp.mean(dn * n, axis=-1, keepdims=True))

    return _pcall(
        body, name=name, grid=(s // ts,),
        in_specs=[_row_spec(ts, d), _row_spec(ts, d), _row_spec(ts, d), _vec_spec(d), _vec_spec(d)],
        out_specs=[_row_spec(ts, d), _vec_spec(d), _vec_spec(d), _vec_spec(d)],
        out_shape=[jax.ShapeDtypeStruct((s, d), F32)] + [jax.ShapeDtypeStruct((1, d), F32)] * 3,
        compiler_params=_params(("arbitrary",)),
    )(dh, x, dxo, g, scale)


def _sgu_mask():
    t = lax.broadcasted_iota(jnp.int32, (SGU_BLOCK, SGU_BLOCK), 0) // CHUNK
    s = lax.broadcasted_iota(jnp.int32, (SGU_BLOCK, SGU_BLOCK), 1) // CHUNK
    return s <= t


def _sgu_norm(v_pre, g):
    e = v_pre.shape[-1]
    vg = _gelu(v_pre)
    mu = jnp.sum(vg, axis=-1, keepdims=True) * (1.0 / e)
    dlt = vg - mu
    var = jnp.sum(dlt * dlt, axis=-1, keepdims=True) * (1.0 / e)
    rstd = lax.rsqrt(var + NORM_EPS)
    vhat = dlt * rstd
    return vhat, rstd, (vhat * g).astype(BF16)


def _sgu_mid_fwd(uvz, norm_g, w_s, bias_full, name):
    s, e3 = uvz.shape
    e = e3 // 3
    gd = e // SGU_GROUPS
    nb = s // SGU_BLOCK

    def body(uvz_ref, g_ref, w_ref, b_ref, y_ref, wsc):
        @pl.when(pl.program_id(0) == 0)
        def _():
            msk = _sgu_mask()
            for gi in range(SGU_GROUPS):
                wsc[gi] = jnp.where(msk, w_ref[gi], 0.0).astype(BF16)

        _, _, vb = _sgu_norm(uvz_ref[:, e:2 * e], g_ref[...])
        for gi in range(SGU_GROUPS):
            lo = gi * gd
            vm = jnp.dot(wsc[gi], vb[:, lo:lo + gd], preferred_element_type=F32) + b_ref[:, lo:lo + gd]
            zg = uvz_ref[:, 2 * e + lo:2 * e + lo + gd]
            y_ref[:, lo:lo + gd] = (_gelu(uvz_ref[:, lo:lo + gd]) * vm * (zg * _sigmoid(zg))).astype(BF16)

    return _pcall(
        body, name=name, grid=(nb,),
        in_specs=[pl.BlockSpec((SGU_BLOCK, e3), lambda n: (n, 0)),
                  pl.BlockSpec((1, e), lambda n: (0, 0)),
                  pl.BlockSpec((SGU_GROUPS, SGU_BLOCK, SGU_BLOCK), lambda n: (0, 0, 0)),
                  pl.BlockSpec((SGU_BLOCK, e), lambda n: (0, 0))],
        out_specs=pl.BlockSpec((SGU_BLOCK, e), lambda n: (n, 0)),
        out_shape=jax.ShapeDtypeStruct((s, e), BF16),
        scratch_shapes=[pltpu.VMEM((SGU_GROUPS, SGU_BLOCK, SGU_BLOCK), BF16)],
        compiler_params=_params(("arbitrary",)),
    )(uvz, norm_g, w_s, bias_full)


def _sgu_mid_bwd(uvz, dy, norm_g, w_s, bias_full, name):
    s, e3 = uvz.shape
    e = e3 // 3
    gd = e // SGU_GROUPS
    nb = s // SGU_BLOCK

    def body(uvz_ref, dy_ref, g_ref, w_ref, b_ref, d_ref, dw_ref, db_ref, dg_ref, wsc, wtsc, dvh_sc, dbacc):
        n = pl.program_id(0)

        @pl.when(n == 0)
        def _():
            msk = _sgu_mask()
            for gi in range(SGU_GROUPS):
                wm = jnp.where(msk, w_ref[gi], 0.0)
                wsc[gi] = wm.astype(BF16)
                wtsc[gi] = wm.T.astype(BF16)
            dw_ref[...] = jnp.zeros_like(dw_ref)
            dg_ref[...] = jnp.zeros_like(dg_ref)
            dbacc[...] = jnp.zeros_like(dbacc)

        v_pre = uvz_ref[:, e:2 * e]
        gv = g_ref[...]
        vhat, rstd, vb = _sgu_norm(v_pre, gv)
        s1 = jnp.zeros((SGU_BLOCK, 1), F32)
        s2 = jnp.zeros((SGU_BLOCK, 1), F32)
        for gi in range(SGU_GROUPS):
            lo = gi * gd
            u_pre = uvz_ref[:, lo:lo + gd]
            zg = uvz_ref[:, 2 * e + lo:2 * e + lo + gd]
            dyg = dy_ref[:, lo:lo + gd]
            ug = _gelu(u_pre)
            sig = _sigmoid(zg)
            vbg = vb[:, lo:lo + gd]
            vhg = vhat[:, lo:lo + gd]
            vm = jnp.dot(wsc[gi], vbg, preferred_element_type=F32) + b_ref[:, lo:lo + gd]
            t = dyg * (zg * sig)
            d_ref[:, lo:lo + gd] = (t * vm * _gelu_grad(u_pre)).astype(BF16)
            dvm = t * ug
            d_ref[:, 2 * e + lo:2 * e + lo + gd] = (dyg * ug * vm * (sig * (1.0 + zg * (1.0 - sig)))).astype(BF16)
            dvm_b = dvm.astype(BF16)
            dv = jnp.dot(wtsc[gi], dvm_b, preferred_element_type=F32)
            dw_ref[gi] += _dot_nt(dvm_b, vbg)
            dbacc[:, lo:lo + gd] += dvm
            dg_ref[:, lo:lo + gd] += jnp.sum(dv * vhg, axis=0, keepdims=True)
            dvh = dv * gv[:, lo:lo + gd]
            dvh_sc[:, lo:lo + gd] = dvh
            s1 = s1 + jnp.sum(dvh, axis=-1, keepdims=True)
            s2 = s2 + jnp.sum(dvh * vhg, axis=-1, keepdims=True)
        dvg = rstd * (dvh_sc[...] - s1 * (1.0 / e) - vhat * (s2 * (1.0 / e)))
        d_ref[:, e:2 * e] = (dvg * _gelu_grad(v_pre)).astype(BF16)

        @pl.when(n == nb - 1)
        def _():
            msk = _sgu_mask()
            for gi in range(SGU_GROUPS):
                dw_ref[gi] = jnp.where(msk, dw_ref[gi], 0.0)
                db_ref[gi] = jnp.sum(dbacc[:, gi * gd:(gi + 1) * gd], axis=1, keepdims=True)

    return _pcall(
        body, name=name, grid=(nb,),
        in_specs=[pl.BlockSpec((SGU_BLOCK, e3), lambda n: (n, 0)),
                  pl.BlockSpec((SGU_BLOCK, e), lambda n: (n, 0)),
                  pl.BlockSpec((1, e), lambda n: (0, 0)),
                  pl.BlockSpec((SGU_GROUPS, SGU_BLOCK, SGU_BLOCK), lambda n: (0, 0, 0)),
                  pl.BlockSpec((SGU_BLOCK, e), lambda n: (0, 0))],
        out_specs=[pl.BlockSpec((SGU_BLOCK, e3), lambda n: (n, 0)),
                   pl.BlockSpec((SGU_GROUPS, SGU_BLOCK, SGU_BLOCK), lambda n: (0, 0, 0)),
                   pl.BlockSpec((SGU_GROUPS, SGU_BLOCK, 1), lambda n: (0, 0, 0)),
                   pl.BlockSpec((1, e), lambda n: (0, 0))],
        out_shape=[jax.ShapeDtypeStruct((s, e3), BF16),
                   jax.ShapeDtypeStruct((SGU_GROUPS, SGU_BLOCK, SGU_BLOCK), F32),
                   jax.ShapeDtypeStruct((SGU_GROUPS, SGU_BLOCK, 1), F32),
                   jax.ShapeDtypeStruct((1, e), F32)],
        scratch_shapes=[pltpu.VMEM((SGU_GROUPS, SGU_BLOCK, SGU_BLOCK), BF16),
                        pltpu.VMEM((SGU_GROUPS, SGU_BLOCK, SGU_BLOCK), BF16),
                        pltpu.VMEM((SGU_BLOCK, e), F32),
                        pltpu.VMEM((SGU_BLOCK, e), F32)],
        compiler_params=_params(("arbitrary",)),
    )(uvz, dy, norm_g, w_s, bias_full)


def _rope_tables(s):
    pos = jnp.arange(s, dtype=F32)
    inv_freq = ROPE_THETA ** (-jnp.arange(0, ROPE, 2, dtype=F32) / ROPE)
    ang = pos[:, None] * inv_freq[None, :]
    cos, sin = jnp.cos(ang), jnp.sin(ang)
    z32 = jnp.zeros((s, HALF), F32)
    z64 = jnp.zeros((s, ROPE), F32)
    ck = jnp.concatenate([cos, cos, z64], axis=1)
    s1k = jnp.concatenate([-sin, z32, z64], axis=1)
    s2k = jnp.concatenate([z32, sin, z64], axis=1)
    return ck, s1k, s2k


def _rot(x, c, s1, s2):
    w = x.shape[-1]
    return x * c + pltpu.roll(x, w - HALF, 1) * s1 + pltpu.roll(x, HALF, 1) * s2


def _rms(cv, n_real):
    r = lax.rsqrt(jnp.sum(cv * cv, axis=-1, keepdims=True) * (1.0 / n_real) + NORM_EPS)
    return r, cv * r


def _mla_norm_fwd(proj, gq, gkv, tabs, name):
    s = proj.shape[0]
    ts = _pick(s, ROW_BLK)
    ck, s1k, s2k = tabs

    def body(cq_ref, ckv_ref, kr_ref, gq_ref, gkv_ref, c_ref, s1_ref, s2_ref, qn_ref, kvn_ref, kro_ref):
        _, nq = _rms(cq_ref[...], Q_RANK)
        qn_ref[...] = (nq * gq_ref[...]).astype(BF16)
        _, nkv = _rms(ckv_ref[...], KV_RANK)
        kvn_ref[...] = (nkv * gkv_ref[...]).astype(BF16)
        kro_ref[...] = _rot(kr_ref[...], c_ref[...], s1_ref[...], s2_ref[...]).astype(BF16)

    tab = pl.BlockSpec((ts, LANE), lambda i: (i, 0))
    return _pcall(
        body, name=name, grid=(s // ts,),
        in_specs=[pl.BlockSpec((ts, Q_RANK_PAD), lambda i: (i, 0)),
                  pl.BlockSpec((ts, KV_RANK), lambda i: (i, PROJ_CKV // KV_RANK)),
                  pl.BlockSpec((ts, LANE), lambda i: (i, PROJ_KR // LANE)),
                  _vec_spec(Q_RANK_PAD), _vec_spec(KV_RANK), tab, tab, tab],
        out_specs=[pl.BlockSpec((ts, Q_RANK_PAD), lambda i: (i, 0)),
                   pl.BlockSpec((ts, KV_RANK), lambda i: (i, 0)), tab],
        out_shape=[jax.ShapeDtypeStruct((s, Q_RANK_PAD), BF16), jax.ShapeDtypeStruct((s, KV_RANK), BF16),
                   jax.ShapeDtypeStruct((s, LANE), BF16)],
        compiler_params=_params(("parallel",)),
    )(proj, proj, proj, gq, gkv, ck, s1k, s2k)


def _transpose_bf16(t):
    return t.astype(F32).T.astype(BF16)


def _diag_mask(tb, transposed):
    r = lax.broadcasted_iota(jnp.int32, (tb, tb), 0) // CHUNK
    c = lax.broadcasted_iota(jnp.int32, (tb, tb), 1) // CHUNK
    return (r <= c) if transposed else (c <= r)


def _attn_fwd(q, kv, kr, proj, tabs, name):
    s = q.shape[0]
    tb = _pick(s, ATT_BLK)
    nb = s // tb
    zcol = PROJ_Z // V_DIM
    mult = SOFTMAX_SCALE * LOG2_E

    def body(qf_ref, c_ref, s1_ref, s2_ref, kn_ref, v_ref, kr_ref, z_ref, o_ref, y_ref, lse_ref, q_ref,
             kt_sc, vx_sc, m_sc, acc_sc, sa_sc, sb_sc):
        qi = pl.program_id(1)
        q_ref[:, :NOPE] = (qf_ref[:, :NOPE] * mult).astype(BF16)
        q_ref[:, NOPE:] = (_rot(qf_ref[:, NOPE:], c_ref[...], s1_ref[...], s2_ref[...]) * mult).astype(BF16)

        @pl.when(qi == 0)
        def _():
            for b in range(nb):
                rows = slice(b * tb, (b + 1) * tb)
                kt_sc[b] = _transpose_bf16(jnp.concatenate([kn_ref[rows, :], kr_ref[rows, :]], axis=1))
                vx_sc[b] = jnp.concatenate([v_ref[rows, :], jnp.ones((tb, V_DIM), BF16)], axis=1)

        m_sc[...] = jnp.full_like(m_sc, -1e30)
        acc_sc[...] = jnp.zeros_like(acc_sc)
        sub = min(tb, ATT_SUB)

        def scores(ki, s_ref):
            s_ref[...] = jnp.dot(q_ref[...], kt_sc[ki], preferred_element_type=F32)

        def step(ki, s_ref, masked):
            for r in range(tb // sub):
                rs = slice(r * sub, (r + 1) * sub)
                nk = (r + 1) * sub if masked else tb
                sc = s_ref[rs, :nk]
                if masked:
                    sc = jnp.where(_diag_mask(tb, False)[rs, :nk], sc, -1e30)
                m_prev = m_sc[rs, :]
                m_new = jnp.maximum(m_prev, jnp.max(sc, axis=-1, keepdims=True))
                p = jnp.exp2(sc - m_new).astype(BF16)
                acc_sc[rs, :] = (jnp.exp2(m_prev - m_new) * acc_sc[rs, :]
                                 + jnp.dot(p, vx_sc[ki, :nk, :], preferred_element_type=F32))
                m_sc[rs, :] = m_new

        def pair(t, carry):
            scores(2 * t + 1, sb_sc)
            step(2 * t, sa_sc, False)
            scores(2 * t + 2, sa_sc)
            step(2 * t + 1, sb_sc, False)
            return carry

        scores(0, sa_sc)
        lax.fori_loop(0, qi // 2, pair, 0)

        @pl.when(qi % 2 == 1)
        def _():
            scores(qi, sb_sc)
            step(qi - 1, sa_sc, False)
            step(qi, sb_sc, True)

        @pl.when(qi % 2 == 0)
        def _():
            step(qi, sa_sc, True)

        l = acc_sc[:, V_DIM:V_DIM + 1]
        o = acc_sc[:, :V_DIM] / l
        z = z_ref[...]
        o_ref[...] = o.astype(BF16)
        y_ref[...] = (o * (z * _sigmoid(z))).astype(BF16)
        lse_cols = jnp.broadcast_to(m_sc[...] + jnp.log2(l), (tb, LANE))
        lse_ref[...] = lse_cols.T[0:1, :]

    oblk = pl.BlockSpec((tb, V_DIM), lambda h, qi: (qi, h))
    qblk = pl.BlockSpec((tb, HEAD_PAD), lambda h, qi: (qi, h))
    tab = pl.BlockSpec((tb, LANE), lambda h, qi: (qi, 0))
    return _pcall(
        body, name=name, grid=(HEADS, nb),
        in_specs=[qblk, tab, tab, tab,
                  pl.BlockSpec((s, NOPE), lambda h, qi: (0, 2 * h)),
                  pl.BlockSpec((s, V_DIM), lambda h, qi: (0, 2 * h + 1)),
                  pl.BlockSpec((s, LANE), lambda h, qi: (0, 0)),
                  pl.BlockSpec((tb, V_DIM), lambda h, qi: (qi, zcol + h))],
        out_specs=[oblk, oblk, pl.BlockSpec((None, None, 1, tb), lambda h, qi: (h, qi, 0, 0)), qblk],
        out_shape=[jax.ShapeDtypeStruct((s, MLA_WIDTH), BF16), jax.ShapeDtypeStruct((s, MLA_WIDTH), BF16),
                   jax.ShapeDtypeStruct((HEADS, nb, 1, tb), F32), jax.ShapeDtypeStruct((s, HEADS * HEAD_PAD), BF16)],
        scratch_shapes=[pltpu.VMEM((nb, HEAD_PAD, tb), BF16), pltpu.VMEM((nb, tb, HEAD_PAD), BF16),
                        pltpu.VMEM((tb, 1), F32), pltpu.VMEM((tb, HEAD_PAD), F32),
                        pltpu.VMEM((tb, tb), F32), pltpu.VMEM((tb, tb), F32)],
        compiler_params=_params(("parallel", "arbitrary")),
    )(q, *tabs, kv, kv, kr, proj)


def _attn_bwd(q_cat, kv, kr, do, o, lse, tabs, name):
    s = q_cat.shape[0]
    tb = _pick(s, ATT_BLK)
    nb = s // tb
    ln2 = math.log(2.0)

    def body(q_ref, do_ref, o_ref, lse_ref, kn_ref, v_ref, kr_ref, c_ref, s1_ref, s2_ref, dq_ref, dkv_ref, dkr_ref,
             qt_sc, dot_sc, delta_sc, dqt_sc, dk_sc, dv_sc):
        ki = pl.program_id(1)

        @pl.when(ki == 0)
        def _():
            for b in range(nb):
                rows = slice(b * tb, (b + 1) * tb)
                qt_sc[b] = _transpose_bf16(q_ref[rows, :])
                do_t = do_ref[rows, :].astype(F32).T
                dot_sc[b] = do_t.astype(BF16)
                delta_sc[b] = jnp.sum(do_t * o_ref[rows, :].astype(F32).T, axis=0, keepdims=True)
            dqt_sc[...] = jnp.zeros_like(dqt_sc)

        k = jnp.concatenate([kn_ref[...], kr_ref[...]], axis=1)
        kt = _transpose_bf16(k)
        vb = v_ref[...]
        dk_sc[...] = jnp.zeros_like(dk_sc)
        dv_sc[...] = jnp.zeros_like(dv_sc)

        def step(qi, masked):
            rows = pl.ds(pl.multiple_of(qi * tb, tb), tb)
            sc_t = jnp.dot(k, qt_sc[qi], preferred_element_type=F32)
            if masked:
                sc_t = jnp.where(_diag_mask(tb, True), sc_t, -1e30)
            p_t = jnp.exp2(sc_t - lse_ref[qi])
            dp_t = jnp.dot(vb, dot_sc[qi], preferred_element_type=F32)
            ds_t = (p_t * (dp_t - delta_sc[qi])).astype(BF16)
            dv_sc[...] += jnp.dot(p_t.astype(BF16), do_ref[rows, :], preferred_element_type=F32)
            dk_sc[...] += jnp.dot(ds_t, q_ref[rows, :], preferred_element_type=F32)
            dqt_sc[qi] += jnp.dot(kt, ds_t, preferred_element_type=F32)

        step(ki, True)

        def loop_body(qi, carry):
            step(qi, False)
            return carry

        lax.fori_loop(ki + 1, nb, loop_body, 0)

        dkv_ref[:, :NOPE] = (dk_sc[:, :NOPE] * ln2).astype(BF16)
        dkv_ref[:, NOPE:] = dv_sc[...].astype(BF16)
        dkr_ref[...] = dk_sc[:, NOPE:] * ln2

        @pl.when(ki == nb - 1)
        def _():
            for b in range(nb):
                rows = slice(b * tb, (b + 1) * tb)
                dq = dqt_sc[b].T
                dq_ref[rows, :NOPE] = (dq[:, :NOPE] * SOFTMAX_SCALE).astype(BF16)
                dq_ref[rows, NOPE:] = (_rot(dq[:, NOPE:], c_ref[rows, :], -s1_ref[rows, :], -s2_ref[rows, :])
                                       * SOFTMAX_SCALE).astype(BF16)

    tab = pl.BlockSpec((s, LANE), lambda h, ki: (0, 0))
    return _pcall(
        body, name=name, grid=(HEADS, nb),
        in_specs=[pl.BlockSpec((s, HEAD_PAD), lambda h, ki: (0, h)),
                  pl.BlockSpec((s, V_DIM), lambda h, ki: (0, h)),
                  pl.BlockSpec((s, V_DIM), lambda h, ki: (0, h)),
                  pl.BlockSpec((None, nb, 1, tb), lambda h, ki: (h, 0, 0, 0)),
                  pl.BlockSpec((tb, NOPE), lambda h, ki: (ki, 2 * h)),
                  pl.BlockSpec((tb, V_DIM), lambda h, ki: (ki, 2 * h + 1)),
                  pl.BlockSpec((tb, LANE), lambda h, ki: (ki, 0)), tab, tab, tab],
        out_specs=[pl.BlockSpec((s, HEAD_PAD), lambda h, ki: (0, h)),
                   pl.BlockSpec((tb, HEAD_PAD), lambda h, ki: (ki, h)),
                   pl.BlockSpec((None, tb, LANE), lambda h, ki: (h, ki, 0))],
        out_shape=[jax.ShapeDtypeStruct((s, HEADS * HEAD_PAD), BF16),
                   jax.ShapeDtypeStruct((s, HEADS * HEAD_PAD), BF16),
                   jax.ShapeDtypeStruct((HEADS, s, LANE), F32)],
        scratch_shapes=[pltpu.VMEM((nb, HEAD_PAD, tb), BF16), pltpu.VMEM((nb, V_DIM, tb), BF16),
                        pltpu.VMEM((nb, 1, tb), F32), pltpu.VMEM((nb, HEAD_PAD, tb), F32),
                        pltpu.VMEM((tb, HEAD_PAD), F32), pltpu.VMEM((tb, V_DIM), F32)],
        compiler_params=_params(("parallel", "arbitrary")),
    )(q_cat, do, o, lse, kv, kv, kr, *tabs)


def _mla_gate_bwd(dy, o, proj, name):
    s = dy.shape[0]
    ts = _pick(s, ROW_BLK)

    def body(dy_ref, o_ref, p_ref, do_ref, dp_ref):
        z = p_ref[:, PROJ_Z:]
        dyv = dy_ref[...]
        sig = _sigmoid(z)
        do_ref[...] = (dyv * (z * sig)).astype(BF16)
        dp_ref[:, :PROJ_Z] = jnp.zeros((ts, PROJ_Z), BF16)
        dp_ref[:, PROJ_Z:] = (dyv * o_ref[...].astype(F32) * (sig * (1.0 + z * (1.0 - sig)))).astype(BF16)

    blk = pl.BlockSpec((ts, MLA_WIDTH), lambda i: (i, 0))
    wide = pl.BlockSpec((ts, PROJ_W), lambda i: (i, 0))
    return _pcall(
        body, name=name, grid=(s // ts,),
        in_specs=[blk, blk, wide],
        out_specs=[blk, wide],
        out_shape=[jax.ShapeDtypeStruct((s, MLA_WIDTH), BF16), jax.ShapeDtypeStruct((s, PROJ_W), BF16)],
        compiler_params=_params(("parallel",)),
    )(dy, o, proj)


def _mla_norm_bwd(dqn, dkvn, dkr_heads, proj, gq, gkv, tabs, dproj, name):
    s = proj.shape[0]
    ts = _pick(s, ROW_BLK)
    ck, s1k, s2k = tabs

    def rms_bwd(cv, dn_in, g, n_real):
        r, n = _rms(cv, n_real)
        dg = jnp.sum(dn_in * n, axis=0, keepdims=True)
        dn = dn_in * g
        dc = r * (dn - n * (jnp.sum(dn * n, axis=-1, keepdims=True) * (1.0 / n_real)))
        return dc, dg

    def body(dqn_ref, dkvn_ref, dkr_ref, cq_ref, ckv_ref, gq_ref, gkv_ref, c_ref, s1_ref, s2_ref, dp_in_ref,
             dp_ref, dgq_ref, dgkv_ref):
        @pl.when(pl.program_id(0) == 0)
        def _():
            dgq_ref[...] = jnp.zeros_like(dgq_ref)
            dgkv_ref[...] = jnp.zeros_like(dgkv_ref)

        dcq, dgq = rms_bwd(cq_ref[...], dqn_ref[...], gq_ref[...], Q_RANK)
        dckv, dgkv = rms_bwd(ckv_ref[...], dkvn_ref[...], gkv_ref[...], KV_RANK)
        dgq_ref[...] += dgq
        dgkv_ref[...] += dgkv
        dkr = dkr_ref[0]
        for h in range(1, HEADS):
            dkr = dkr + dkr_ref[h]
        dp_ref[:, PROJ_CQ:PROJ_CKV] = dcq.astype(BF16)
        dp_ref[:, PROJ_CKV:PROJ_KR] = dckv.astype(BF16)
        dp_ref[:, PROJ_KR:PROJ_Z] = _rot(dkr, c_ref[...], -s1_ref[...], -s2_ref[...]).astype(BF16)

    tab = pl.BlockSpec((ts, LANE), lambda i: (i, 0))
    return _pcall(
        body, name=name, grid=(s // ts,),
        in_specs=[pl.BlockSpec((ts, Q_RANK_PAD), lambda i: (i, 0)),
                  pl.BlockSpec((ts, KV_RANK), lambda i: (i, 0)),
                  pl.BlockSpec((HEADS, ts, LANE), lambda i: (0, i, 0)),
                  pl.BlockSpec((ts, Q_RANK_PAD), lambda i: (i, 0)),
                  pl.BlockSpec((ts, KV_RANK), lambda i: (i, PROJ_CKV // KV_RANK)),
                  _vec_spec(Q_RANK_PAD), _vec_spec(KV_RANK), tab, tab, tab, pl.BlockSpec(memory_space=pl.ANY)],
        out_specs=[pl.BlockSpec((ts, PROJ_Z), lambda i: (i, 0)), _vec_spec(Q_RANK_PAD), _vec_spec(KV_RANK)],
        out_shape=[jax.ShapeDtypeStruct((s, PROJ_W), BF16), jax.ShapeDtypeStruct((1, Q_RANK_PAD), F32),
                   jax.ShapeDtypeStruct((1, KV_RANK), F32)],
        input_output_aliases={10: 0},
        compiler_params=_params(("arbitrary",)),
    )(dqn, dkvn, dkr_heads, proj, proj, gq, gkv, ck, s1k, s2k, dproj)


def _ada_mod(cond_raw, ada_w, bias_my, name):
    nl, d, ncol = ada_w.shape
    tk = _pick(d, 512)
    nk = d // tk

    def body(c_ref, w_ref, b_ref, o_ref, acc_ref):
        kk = pl.program_id(1)

        @pl.when(kk == 0)
        def _():
            acc_ref[...] = jnp.zeros_like(acc_ref)

        cv = c_ref[...]
        cond = (cv * _sigmoid(cv)).astype(BF16)
        acc_ref[...] += jnp.dot(cond, w_ref[...].astype(BF16), preferred_element_type=F32)

        @pl.when(kk == nk - 1)
        def _():
            o_ref[...] = acc_ref[...] + b_ref[...]

    return _pcall(
        body, name=name, grid=(nl, nk),
        in_specs=[pl.BlockSpec((N_DEV, tk), lambda l, kk: (0, kk)),
                  pl.BlockSpec((None, tk, ncol), lambda l, kk: (l, kk, 0)),
                  pl.BlockSpec((None, 1, ncol), lambda l, kk: (l, 0, 0))],
        out_specs=pl.BlockSpec((None, N_DEV, ncol), lambda l, kk: (l, 0, 0)),
        out_shape=jax.ShapeDtypeStruct((nl, N_DEV, ncol), F32),
        scratch_shapes=[pltpu.VMEM((N_DEV, ncol), F32)],
        compiler_params=_params(("parallel", "arbitrary")),
    )(cond_raw, ada_w, bias_my.reshape(nl, 1, ncol))


def _adam(w, g, m, v):
    m = ADAM_B1 * m + (1.0 - ADAM_B1) * g
    v = ADAM_B2 * v + (1.0 - ADAM_B2) * (g * g)
    m_hat = m / (1.0 - ADAM_B1 ** ADAM_STEP)
    v_hat = v / (1.0 - ADAM_B2 ** ADAM_STEP)
    delta = -ADAM_LR * (m_hat / (jnp.sqrt(v_hat) + ADAM_EPS) + ADAM_WD * w)
    return delta, m, v


def _ada_bwd_adam(cond_t, dmod_cols, w, m, v, name):
    nl, d, ncol = w.shape
    tk = _pick(d, 512)

    def body(c_ref, dm_ref, w_ref, m_ref, v_ref, g_ref, d_ref, mo_ref, vo_ref):
        cv = c_ref[...]
        cond = (cv * _sigmoid(cv)).astype(BF16)
        g = jnp.dot(cond, dm_ref[...].astype(BF16), preferred_element_type=F32)
        delta, m2, v2 = _adam(w_ref[...], g, m_ref[...], v_ref[...])
        g_ref[...] = g
        d_ref[...] = delta
        mo_ref[...] = m2
        vo_ref[...] = v2

    blk = pl.BlockSpec((None, tk, ncol), lambda l, kk: (l, kk, 0))
    shp = jax.ShapeDtypeStruct((nl, d, ncol), F32)
    return _pcall(
        body, name=name, grid=(nl, d // tk),
        in_specs=[pl.BlockSpec((tk, N_DEV), lambda l, kk: (kk, 0)),
                  pl.BlockSpec((None, N_DEV, ncol), lambda l, kk: (l, 0, 0)), blk, blk, blk],
        out_specs=[blk, blk, blk, blk], out_shape=[shp, shp, shp, shp],
        compiler_params=_params(("parallel", "parallel")),
    )(cond_t, dmod_cols, w, m, v)


def _adam_reduce(recv0, recv1, w, m, v, name):
    nl, r, c = w.shape
    tr = _pick(r, 128) if r % 128 == 0 else r
    tc = _pick(c, 1024)
    n0, n1 = recv0.shape[0] // r, recv1.shape[0] // r

    def body(r0_ref, r1_ref, w_ref, m_ref, v_ref, g_ref, d_ref, mo_ref, vo_ref):
        l = pl.program_id(0)

        def run(rr):
            g = rr[0].astype(F32)
            for sidx in range(1, rr.shape[0]):
                g = g + rr[sidx].astype(F32)
            delta, m2, v2 = _adam(w_ref[...], g, m_ref[...], v_ref[...])
            g_ref[...] = g
            d_ref[...] = delta
            mo_ref[...] = m2
            vo_ref[...] = v2

        @pl.when(l == 0)
        def _():
            run(r0_ref)

        @pl.when(l == 1)
        def _():
            run(r1_ref)

    def rblk(n, layer):
        return pl.BlockSpec((n, tr, tc), lambda l, i, j: (0, jnp.where(l == layer, i, 0), jnp.where(l == layer, j, 0)))

    blk = pl.BlockSpec((None, tr, tc), lambda l, i, j: (l, i, j))
    shp = jax.ShapeDtypeStruct((nl, r, c), F32)
    return _pcall(
        body, name=name, grid=(nl, r // tr, c // tc),
        in_specs=[rblk(n0, 0), rblk(n1, 1), blk, blk, blk],
        out_specs=[blk, blk, blk, blk], out_shape=[shp, shp, shp, shp],
        compiler_params=_params(("arbitrary", "parallel", "parallel")),
    )(recv0.reshape(n0, r, c), recv1.reshape(n1, r, c), w, m, v)


def _adam_small(gathered, w, m, v, name):
    r = w.shape[0]
    tr = _pick(r, 512) if r % 512 == 0 else r

    def body(p_ref, w_ref, m_ref, v_ref, g_ref, d_ref, mo_ref, vo_ref):
        g = p_ref[0]
        for sidx in range(1, N_DEV):
            g = g + p_ref[sidx]
        delta, m2, v2 = _adam(w_ref[...], g, m_ref[...], v_ref[...])
        g_ref[...] = g
        d_ref[...] = delta
        mo_ref[...] = m2
        vo_ref[...] = v2

    blk = pl.BlockSpec((tr, LANE), lambda i: (i, 0))
    shp = jax.ShapeDtypeStruct((r, LANE), F32)
    return _pcall(
        body, name=name, grid=(r // tr,),
        in_specs=[pl.BlockSpec((N_DEV, tr, LANE), lambda i: (0, i, 0)), blk, blk, blk],
        out_specs=[blk, blk, blk, blk], out_shape=[shp, shp, shp, shp],
        compiler_params=_params(("parallel",)),
    )(gathered, w, m, v)


def _my_place():
    x, y, c = lax.axis_index("x"), lax.axis_index("y"), lax.axis_index("c")
    return x, y, c, 4 * x + 2 * y + c


def _peer(x, y, c, k):
    px = 1 - x if (k >> 2) & 1 else x
    py = 1 - y if (k >> 1) & 1 else y
    pc = 1 - c if k & 1 else c
    return (px, py, pc), 4 * px + 2 * py + pc


def _slab(ref, shape, kind, p):
    r, cd = shape
    if kind == "row":
        return ref.at[pl.ds(pl.multiple_of(p * r, SUBLANE), r), :]
    return ref.at[:, pl.ds(pl.multiple_of(p * cd, LANE), cd)]


def _exchange_layout(arrays, kinds, gather):
    shard_shapes, dst_kinds, out_shapes = [], [], []
    for a, kind in zip(arrays, kinds):
        r, cd = a.shape
        if gather:
            shard, dst_kind = (r, cd), kind
        else:
            shard, dst_kind = ((r // N_DEV, cd) if kind == "row" else (r, cd // N_DEV)), "row"
        shard_shapes.append(shard)
        dst_kinds.append(dst_kind)
        full = (shard[0] * N_DEV, shard[1]) if dst_kind == "row" else (shard[0], shard[1] * N_DEV)
        out_shapes.append(jax.ShapeDtypeStruct(full, a.dtype))
    return shard_shapes, dst_kinds, out_shapes


def _exchange_copies(ins, outs, send_sems, recv_sems, sem_of, layout, kinds, gather):
    shard_shapes, dst_kinds, _ = layout
    x, y, c, me = _my_place()

    def src_for(a, p):
        return ins[a] if gather else _slab(ins[a], shard_shapes[a], kinds[a], p)

    def dst_slot(a, p):
        return _slab(outs[a], shard_shapes[a], dst_kinds[a], p)

    def local(a, sem):
        return pltpu.make_async_copy(src_for(a, me), dst_slot(a, me), sem)

    def remote(a, k, slot):
        peer, pidx = _peer(x, y, c, k)
        return pltpu.make_async_remote_copy(
            src_ref=src_for(a, pidx), dst_ref=dst_slot(a, me if slot == "mine" else pidx),
            send_sem=send_sems.at[sem_of(a, k)], recv_sem=recv_sems.at[sem_of(a, k)],
            device_id=peer, device_id_type=MESH)

    return local, remote


def _place_own(src, src_kind, slab_shape, dst_kind, full, name, index=None):
    r, cd = slab_shape
    tr = _pick(r, 512)
    nr = r // tr
    me = _my_place()[3] if index is None else index
    src, layer = src if isinstance(src, tuple) else (src, None)

    def body(me_ref, s_ref, o_ref):
        o_ref[...] = s_ref[...].astype(o_ref.dtype)

    def where(kind):
        if kind is None:
            return lambda i, me_ref: (i, 0)
        if kind == "row":
            return lambda i, me_ref: (me_ref[0] * nr + i, 0)
        return lambda i, me_ref: (i, me_ref[0])

    if layer is None:
        src_spec = pl.BlockSpec((tr, cd), where(src_kind))
    else:
        src_spec = pl.BlockSpec((None, tr, cd), lambda i, me_ref: (layer, *where(src_kind)(i, me_ref)))
    return _pcall(
        body, name=name,
        grid_spec=pltpu.PrefetchScalarGridSpec(
            num_scalar_prefetch=1, grid=(nr,),
            in_specs=[src_spec],
            out_specs=pl.BlockSpec((tr, cd), where(dst_kind))),
        out_shape=jax.ShapeDtypeStruct(full.shape, full.dtype),
        compiler_params=_params(("arbitrary",)),
    )(jnp.reshape(me, (1,)).astype(jnp.int32), src)


def _landing_with_own_slab(arrays, kinds, gather, layout, name=None, order_after=None):
    _, _, _, me = _my_place()
    index = None
    if order_after is not None:
        first = order_after.reshape(-1)[0].astype(jnp.int32)
        index = me + jnp.minimum(jnp.maximum(first, 0), 0)
    lands = []
    for a in range(len(arrays)):
        (r, cd), dst_kind, full = layout[0][a], layout[1][a], layout[2][a]
        if name is not None:
            lands.append(_place_own(arrays[a], None if gather else kinds[a], (r, cd), dst_kind, full, name,
                                    index=index))
            continue
        if gather:
            piece = arrays[a]
        elif kinds[a] == "row":
            piece = lax.dynamic_slice_in_dim(arrays[a], me * r, r, axis=0)
        else:
            piece = lax.dynamic_slice_in_dim(arrays[a], me * cd, cd, axis=1)
        at = (me * r, 0) if dst_kind == "row" else (0, me * cd)
        lands.append(lax.dynamic_update_slice(lax.empty(full.shape, full.dtype), piece, at))
    return lands


def _exchange(arrays, kinds, gather, name, after=None):
    n = len(arrays)
    n_extra = 0 if after is None else 1
    layout = _exchange_layout(arrays, kinds, gather)
    lands = _landing_with_own_slab(arrays, kinds, gather, layout)

    def body(*refs):
        ins, outs = refs[:n], refs[2 * n + n_extra:3 * n + n_extra]
        send_sems, recv_sems = refs[3 * n + n_extra:]
        _, remote = _exchange_copies(ins, outs, send_sems, recv_sems,
                                     lambda a, k: a * (N_DEV - 1) + k - 1, layout, kinds, gather)
        for a in range(n):
            for k in range(1, N_DEV):
                remote(a, k, "mine").start()
        for a in range(n):
            for k in range(1, N_DEV):
                arrival = remote(a, k, "theirs")
                arrival.wait_send()
                arrival.wait_recv()

    anyspec = pl.BlockSpec(memory_space=pl.ANY)
    outs = _pcall(
        body, comm=True, name=name,
        in_specs=[anyspec] * (2 * n + n_extra), out_specs=[anyspec] * n, out_shape=layout[2],
        input_output_aliases={n + a: a for a in range(n)},
        scratch_shapes=[pltpu.SemaphoreType.DMA((n * (N_DEV - 1),)), pltpu.SemaphoreType.DMA((n * (N_DEV - 1),))],
    )(*arrays, *lands, *([] if after is None else [after]))
    return list(outs)


HBM_SPEC = pl.BlockSpec(memory_space=pltpu.HBM)
SEM_SPEC = pl.BlockSpec(memory_space=pltpu.SEMAPHORE)
ANY_SPEC = pl.BlockSpec(memory_space=pl.ANY)
DATAFLOW = pltpu.SideEffectType.DATAFLOW_SIDE_EFFECTING


def _exchange_start(arrays, kinds, gather, name, after, carry=()):
    n, nc = len(arrays), len(carry)
    layout = _exchange_layout(arrays, kinds, gather)
    lands = _landing_with_own_slab(arrays, kinds, gather, layout, "place_own")

    def body(*refs):
        ins, outs = refs[:n], refs[n:2 * n]
        send_sems, recv_sems = refs[2 * n + nc + 1], refs[2 * n + nc + 2]
        token = refs[2 * n + nc + 3 + 2 * n + nc]
        _, remote = _exchange_copies(ins, outs, send_sems, recv_sems, lambda a, k: a, layout, kinds, gather)
        for a in range(n):
            for k in range(1, N_DEV):
                remote(a, k, "mine").start()
        token[...] = jnp.zeros_like(token)

    passed = list(arrays) + lands + list(carry)
    res = pl.pallas_call(
        body, name=name,
        out_shape=(pltpu.SemaphoreType.DMA((n,)), pltpu.SemaphoreType.DMA((n,)),
                   *[pltpu.HBM(t.shape, t.dtype) for t in passed], jax.ShapeDtypeStruct((SUBLANE, LANE), F32)),
        in_specs=[HBM_SPEC] * (2 * n + nc) + [ANY_SPEC],
        out_specs=(SEM_SPEC, SEM_SPEC, *([HBM_SPEC] * (2 * n + nc)), pl.BlockSpec(memory_space=pltpu.VMEM)),
        input_output_aliases={i: 2 + i for i in range(2 * n + nc)},
        compiler_params=pltpu.CompilerParams(has_side_effects=DATAFLOW),
    )(*[pltpu.with_memory_space_constraint(t, pltpu.HBM) for t in passed], after)
    handle = (res[0], res[1], list(res[2:2 + n]), list(res[2 + n:2 + 2 * n]), tuple(kinds), gather)
    return handle, res[-1], list(res[2 + 2 * n:2 + 2 * n + nc])


def _exchange_wait(handle, name, after):
    send_sems, recv_sems, ins_thru, lands_thru, kinds, gather = handle
    n = len(ins_thru)
    layout = _exchange_layout(ins_thru, kinds, gather)

    def body(*refs):
        ins, outs = refs[:n], refs[n:2 * n]
        s_sems, r_sems = refs[2 * n], refs[2 * n + 1]
        _, remote = _exchange_copies(ins, outs, s_sems, r_sems, lambda a, k: a, layout, kinds, gather)
        for a in range(n):
            for k in range(1, N_DEV):
                arrival = remote(a, k, "theirs")
                arrival.wait_send()
                arrival.wait_recv()

    res = pl.pallas_call(
        body, name=name,
        out_shape=[pltpu.HBM(t.shape, t.dtype) for t in ins_thru + lands_thru],
        in_specs=[HBM_SPEC] * (2 * n) + [SEM_SPEC, SEM_SPEC, ANY_SPEC],
        out_specs=[HBM_SPEC] * (2 * n),
        input_output_aliases={i: i for i in range(2 * n)},
        compiler_params=pltpu.CompilerParams(has_side_effects=DATAFLOW),
    )(*ins_thru, *lands_thru, send_sems, recv_sems, after)
    return list(res[n:2 * n])


STAGE1_KS = (1, 2, 4, 6)
FORWARD_KS = (2, 4, 6)


def _gather2_copies(lands, shard_shapes, kinds):
    x, y, c, me = _my_place()

    def slab(a, p):
        return _slab(lands[a], shard_shapes[a], kinds[a], p)

    def stage1(a, k, sems, arriving):
        peer, pidx = _peer(x, y, c, k)
        s = slab(a, pidx if arriving else me)
        return pltpu.make_async_remote_copy(src_ref=s, dst_ref=s, send_sem=sems[0].at[a], recv_sem=sems[1].at[a],
                                            device_id=peer, device_id_type=MESH)

    def stage2(a, k, sems, arriving):
        sib, _ = _peer(x, y, c, 1)
        _, mine = _peer(x, y, c, k)
        _, theirs = _peer(x, y, 1 - c, k)
        s = slab(a, theirs if arriving else mine)
        return pltpu.make_async_remote_copy(src_ref=s, dst_ref=s, send_sem=sems[0].at[a], recv_sem=sems[1].at[a],
                                            device_id=sib, device_id_type=MESH)

    return stage1, stage2


def _gather2_call(lands, sems_in, name, after, make_body, returns_sems):
    n = len(lands)
    n_in = len(sems_in)

    def body(*refs):
        land_refs = refs[:n]
        in_sems = refs[n:n + n_in]
        rest = refs[n + n_in + 1:]
        out_sems = rest[:2] if returns_sems else ()
        make_body(land_refs, in_sems, out_sems)
        if returns_sems:
            token = rest[2 + n]
            token[...] = jnp.zeros_like(token)

    sem_shapes = (pltpu.SemaphoreType.DMA((n,)), pltpu.SemaphoreType.DMA((n,))) if returns_sems else ()
    tok_shape = (jax.ShapeDtypeStruct((SUBLANE, LANE), F32),) if returns_sems else ()
    n_sem_out = len(sem_shapes)
    res = pl.pallas_call(
        body, name=name,
        out_shape=(*sem_shapes, *[pltpu.HBM(t.shape, t.dtype) for t in lands], *tok_shape),
        in_specs=[HBM_SPEC] * n + [SEM_SPEC] * n_in + [ANY_SPEC],
        out_specs=(*([SEM_SPEC] * n_sem_out), *([HBM_SPEC] * n),
                   *([pl.BlockSpec(memory_space=pltpu.VMEM)] if returns_sems else [])),
        input_output_aliases={i: n_sem_out + i for i in range(n)},
        compiler_params=pltpu.CompilerParams(has_side_effects=DATAFLOW),
    )(*[pltpu.with_memory_space_constraint(t, pltpu.HBM) for t in lands], *sems_in, after)
    sems_out = tuple(res[:n_sem_out])
    lands_thru = list(res[n_sem_out:n_sem_out + n])
    return sems_out, lands_thru, (res[-1] if returns_sems else None)


def _gather2_start(shards, kinds, name, after):
    n = len(shards)
    views = [jax.ShapeDtypeStruct(t.shape[1:], t.dtype) for t, _ in shards]
    shard_shapes, dst_kinds, fulls = _exchange_layout(views, kinds, True)
    layout = (shard_shapes, dst_kinds, [jax.ShapeDtypeStruct(f.shape, BF16) for f in fulls])
    lands = _landing_with_own_slab(shards, kinds, True, layout, "place_own", order_after=after)

    def make_body(land_refs, in_sems, out_sems):
        stage1, _ = _gather2_copies(land_refs, layout[0], kinds)
        for a in range(n):
            for k in STAGE1_KS:
                stage1(a, k, out_sems, False).start()

    sems, lands, token = _gather2_call(lands, (), name, after, make_body, True)
    return (sems, lands, layout[0], tuple(kinds)), token


def _gather2_forward(handle, name, after, carry=()):
    sems1, lands, shard_shapes, kinds = handle
    n = len(lands)

    def make_body(land_refs, in_sems, out_sems):
        stage1, stage2 = _gather2_copies(land_refs, shard_shapes, kinds)
        for a in range(n):
            for k in STAGE1_KS:
                arrival = stage1(a, k, in_sems, True)
                arrival.wait_send()
                arrival.wait_recv()
        for a in range(n):
            for k in FORWARD_KS:
                stage2(a, k, out_sems, False).start()

    sems2, passed, token = _gather2_call(list(lands) + list(carry), sems1, name, after, make_body, True)
    return (sems2, passed[:n], shard_shapes, kinds), token, passed[n:]


def _gather2_wait(handle, name, after):
    sems2, lands, shard_shapes, kinds = handle
    n = len(lands)

    def make_body(land_refs, in_sems, out_sems):
        _, stage2 = _gather2_copies(land_refs, shard_shapes, kinds)
        for a in range(n):
            for k in FORWARD_KS:
                arrival = stage2(a, k, in_sems, True)
                arrival.wait_send()
                arrival.wait_recv()

    _, lands, _ = _gather2_call(lands, sems2, name, after, make_body, False)
    return lands


N_CHIP = N_DEV // 2


def _scatter2_pair(g, name):
    r, c8 = g.shape
    cd = c8 // N_DEV

    def body(g_ref, o_ref, send_sems, recv_sems):
        x, y, c, _ = _my_place()
        sib, _ = _peer(x, y, c, 1)

        def copy(ch):
            theirs = 2 * ch + (1 - c)
            return pltpu.make_async_remote_copy(
                src_ref=_slab(g_ref, (r, cd), "col", theirs), dst_ref=_slab(o_ref, (r, cd), "row", ch),
                send_sem=send_sems.at[ch], recv_sem=recv_sems.at[ch], device_id=sib, device_id_type=MESH)

        for ch in range(N_CHIP):
            copy(ch).start()
        for ch in range(N_CHIP):
            copy(ch).wait_send()
            copy(ch).wait_recv()

    return _pcall(
        body, comm=True, name=name,
        in_specs=[ANY_SPEC], out_specs=ANY_SPEC, out_shape=jax.ShapeDtypeStruct((N_CHIP * r, cd), g.dtype),
        scratch_shapes=[pltpu.SemaphoreType.DMA((N_CHIP,)), pltpu.SemaphoreType.DMA((N_CHIP,))],
    )(g)


def _scatter2_add(g, from_sibling, name):
    r, c8 = g.shape
    cd = c8 // N_DEV
    tr = _pick(r, 512)
    nr = r // tr
    _, _, core, _ = _my_place()

    def body(c_ref, g_ref, s_ref, o_ref):
        o_ref[...] = (g_ref[...].astype(F32) + s_ref[...].astype(F32)).astype(o_ref.dtype)

    return _pcall(
        body, name=name,
        grid_spec=pltpu.PrefetchScalarGridSpec(
            num_scalar_prefetch=1, grid=(N_CHIP, nr),
            in_specs=[pl.BlockSpec((tr, cd), lambda ch, i, c_ref: (i, 2 * ch + c_ref[0])),
                      pl.BlockSpec((tr, cd), lambda ch, i, c_ref: (ch * nr + i, 0))],
            out_specs=pl.BlockSpec((tr, cd), lambda ch, i, c_ref: (ch * nr + i, 0))),
        out_shape=jax.ShapeDtypeStruct((N_CHIP * r, cd), g.dtype),
        compiler_params=_params(("arbitrary", "arbitrary")),
    )(jnp.reshape(core, (1,)).astype(jnp.int32), g, from_sibling)


def _scatter2_copies(refs, r):
    part_ref, recv_ref = refs
    x, y, c, _ = _my_place()
    my_chip = 2 * x + y

    def rows(ref, ch):
        return ref.at[pl.ds(pl.multiple_of(ch * r, SUBLANE), r), :]

    def copy(k, sems, arriving):
        peer, _ = _peer(x, y, c, k)
        peer_chip = 2 * peer[0] + peer[1]
        return pltpu.make_async_remote_copy(
            src_ref=rows(part_ref, peer_chip), dst_ref=rows(recv_ref, peer_chip if arriving else my_chip),
            send_sem=sems[0].at[0], recv_sem=sems[1].at[0], device_id=peer, device_id_type=MESH)

    return copy


def _scatter2_start(g, name, after, carry=()):
    r = g.shape[0]
    x, y, _, _ = _my_place()
    from_sibling = _scatter2_pair(g, name + "_pair")
    partial = _scatter2_add(g, from_sibling, name + "_add")
    cd = partial.shape[1]
    recv = _place_own(partial, "row", (r, cd), "row", jax.ShapeDtypeStruct(partial.shape, partial.dtype),
                      "place_own", index=2 * x + y)

    def make_body(land_refs, in_sems, out_sems):
        copy = _scatter2_copies(land_refs[:2], r)
        for k in FORWARD_KS:
            copy(k, out_sems, False).start()

    sems, lands, token = _gather2_call([partial, recv] + list(carry), (), name, after, make_body, True)
    return (sems, lands[:2], r), token, lands[2:]


def _scatter2_wait(handle, name, after):
    sems, lands, r = handle

    def make_body(land_refs, in_sems, out_sems):
        copy = _scatter2_copies(land_refs, r)
        for k in FORWARD_KS:
            arrival = copy(k, in_sems, True)
            arrival.wait_send()
            arrival.wait_recv()

    _, lands, _ = _gather2_call(lands, sems, name, after, make_body, False)
    return lands[1]


def _pad_mla_w_in(w):
    d = w.shape[0]
    z = lambda n: jnp.zeros((d, n), w.dtype)
    o1, o2, o3 = Q_RANK, Q_RANK + KV_RANK, Q_RANK + KV_RANK + ROPE
    return jnp.concatenate([w[:, :o1], z(Q_RANK_PAD - Q_RANK), w[:, o1:o2], w[:, o2:o3], z(LANE - ROPE), w[:, o3:]], axis=1)


def _unpad_mla_w_in(g):
    return jnp.concatenate([g[:, :Q_RANK], g[:, PROJ_CKV:PROJ_KR], g[:, PROJ_KR:PROJ_KR + ROPE], g[:, PROJ_Z:]], axis=1)


def _pad_w_uq(w):
    w3 = w.reshape(Q_RANK, HEADS, NOPE + ROPE)
    w3 = jnp.pad(w3, ((0, Q_RANK_PAD - Q_RANK), (0, 0), (0, HEAD_PAD - NOPE - ROPE)))
    return w3.reshape(Q_RANK_PAD, HEADS * HEAD_PAD)


def _unpad_w_uq(g):
    return g[:Q_RANK].reshape(Q_RANK, HEADS, HEAD_PAD)[:, :, :NOPE + ROPE].reshape(Q_RANK, HEADS * (NOPE + ROPE))


def _pack(pieces):
    flat = [p.reshape(-1).astype(F32) for p in pieces]
    tot = sum(f.shape[0] for f in flat)
    unit = SUBLANE * LANE
    padn = (-tot) % unit
    if padn:
        flat.append(jnp.zeros((padn,), F32))
    return jnp.concatenate(flat).reshape(-1, LANE)


def _unpack(packed, shapes, lead=()):
    flat = packed.reshape(tuple(lead) + (-1,))
    out, off = [], 0
    for shp in shapes:
        nel = int(np.prod(shp))
        out.append(flat[..., off:off + nel].reshape(tuple(lead) + tuple(shp)))
        off += nel
    return out


SMALL_GROUPS = {
    "a": [('ada_b', lambda t: t[1:]), ('pre_g', lambda t: t[1:]), ('post_g', lambda t: t[1:]),
          ('sgu_norm_g', lambda t: t[1:]), ('sgu_w_s', lambda t: t[1:]), ('sgu_b_s', lambda t: t[1:]),
          ('mla_q_norm_g', lambda t: t), ('mla_kv_norm_g', lambda t: t)],
    "b": [('ada_b', lambda t: t[0:1, 2 * t.shape[1] // 3:]), ('post_g', lambda t: t[0:1]),
          ('sgu_norm_g', lambda t: t[0:1]), ('sgu_w_s', lambda t: t[0:1]), ('sgu_b_s', lambda t: t[0:1])],
    "c": [('ada_b', lambda t: t[0:1, :2 * t.shape[1] // 3]), ('pre_g', lambda t: t[0:1])],
}


WEIGHTS = ['ada_w', 'ada_b', 'pre_g', 'post_g', 'sgu_w_in', 'sgu_norm_g', 'sgu_w_s', 'sgu_b_s', 'sgu_w_out',
           'mla_w_in', 'mla_q_norm_g', 'mla_kv_norm_g', 'mla_w_uq', 'mla_w_ukv', 'mla_w_out']
INPUTS = ['x', 'c'] + WEIGHTS + ['loss_target'] + ['m_' + n for n in WEIGHTS] + ['v_' + n for n in WEIGHTS]


def kernel(x, c, ada_w, ada_b, pre_g, post_g, sgu_w_in, sgu_norm_g, sgu_w_s, sgu_b_s, sgu_w_out, mla_w_in, mla_q_norm_g, mla_kv_norm_g, mla_w_uq, mla_w_ukv, mla_w_out, loss_target, m_ada_w, m_ada_b, m_pre_g, m_post_g, m_sgu_w_in, m_sgu_norm_g, m_sgu_w_s, m_sgu_b_s, m_sgu_w_out, m_mla_w_in, m_mla_q_norm_g, m_mla_kv_norm_g, m_mla_w_uq, m_mla_w_ukv, m_mla_w_out, v_ada_w, v_ada_b, v_pre_g, v_post_g, v_sgu_w_in, v_sgu_norm_g, v_sgu_w_s, v_sgu_b_s, v_sgu_w_out, v_mla_w_in, v_mla_q_norm_g, v_mla_kv_norm_g, v_mla_w_uq, v_mla_w_ukv, v_mla_w_out):
    given = locals()
    A = {name: given[name] for name in INPUTS}
    x0 = A['x'][0]
    tgt = A['loss_target'][0]
    s, d = x0.shape
    e = 2 * d
    ncol = 3 * d // N_DEV
    _, _, _, me = _my_place()
    ktabs = _rope_tables(s)

    gains = jnp.zeros((SUBLANE, LANE), F32)
    gains = gains.at[0:2, :Q_RANK // N_DEV].set(A['mla_q_norm_g'])
    gains = gains.at[2:4, :KV_RANK // N_DEV].set(A['mla_kv_norm_g'])
    c8 = jnp.broadcast_to(A['c'], (SUBLANE, d))
    cg, gg = _exchange([c8, gains], ["row", "row"], True, "ag_cond")
    cond_raw = cg.reshape(N_DEV, SUBLANE, d)[:, 0, :]
    gg = gg.reshape(N_DEV, SUBLANE, LANE)
    gq_full = jnp.transpose(gg[:, 0:2, :Q_RANK // N_DEV], (1, 0, 2)).reshape(N_MIX, Q_RANK)
    gkv_full = jnp.transpose(gg[:, 2:4, :KV_RANK // N_DEV], (1, 0, 2)).reshape(N_MIX, KV_RANK)
    gq_pad = jnp.pad(gq_full, ((0, 0), (0, Q_RANK_PAD - Q_RANK)))

    bias_my = lax.dynamic_slice_in_dim(A['ada_b'], me * ncol, ncol, axis=1)
    mod_part = _ada_mod(cond_raw, A['ada_w'], bias_my, "ada_mod")
    send = jnp.pad(jnp.transpose(mod_part, (1, 0, 2)), ((0, 0), (0, SUBLANE - DEPTH), (0, 0)))
    (rb,) = _exchange([send.reshape(N_DEV * SUBLANE, ncol)], ["row"], False, "a2a_mod")

    token = rb
    gathers = {}
    for i in range(DEPTH):
        j = i // N_MIX
        if i % N_MIX == 0:
            parts = [("in", [(A['sgu_w_in'], j)], ["col"]), ("out", [(A['sgu_w_out'], j)], ["row"])]
        else:
            parts = [("all", [(A['mla_w_in'], j), (A['mla_w_uq'], j), (A['mla_w_ukv'], j), (A['mla_w_out'], j)],
                      ["col", "col", "col", "row"])]
        for part, shards, kinds in parts:
            gathers[(i, part)], token = _gather2_start(shards, kinds, f"ag_start_{i}_{part}", token)

    def forward_gathers(i, carried):
        for key in [k for k in gathers if k[0] == i]:
            gathers[key], _, (carried,) = _gather2_forward(gathers[key], f"ag_forward_{key[0]}_{key[1]}", token,
                                                           carry=[carried])
        return carried

    mod = jnp.transpose(rb.reshape(N_DEV, SUBLANE, ncol)[:, :DEPTH, :], (1, 0, 2)).reshape(DEPTH, 3 * d) + token[0, 0]
    shift = [mod[i:i + 1, :d] for i in range(DEPTH)]
    scale = [mod[i:i + 1, d:2 * d] for i in range(DEPTH)]
    gate = [mod[i:i + 1, 2 * d:] for i in range(DEPTH)]

    saved = []
    x = x0
    for i in range(DEPTH):
        j = i // N_MIX
        h = _pre_fwd(x, A['pre_g'][i:i + 1], scale[i], shift[i], f"pre_fwd")
        if i == 0:
            gathers[(0, "in")], _, (h,) = _gather2_forward(gathers[(0, "in")], "ag_forward_0_in", token, carry=[h])
        if i % N_MIX == 0:
            (w_in,) = _gather2_wait(gathers[(i, "in")], f"ag_wait_{i}_in", h)
            uvz = _mm(h, w_in, "nn", F32, "sgu_in")
            if i == 0:
                gathers[(0, "out")], _, (uvz,) = _gather2_forward(gathers[(0, "out")], "ag_forward_0_out", token,
                                                                  carry=[uvz])
            bias_full = jnp.repeat(A['sgu_b_s'][j].T, e // SGU_GROUPS, axis=1)
            ng = A['sgu_norm_g'][j:j + 1]
            y = _sgu_mid_fwd(uvz, ng, A['sgu_w_s'][j], bias_full, "sgu_mid_fwd")
            y = forward_gathers(i + 1, y)
            (w_out,) = _gather2_wait(gathers[(i, "out")], f"ag_wait_{i}_out", y)
            out = _mm(y, w_out, "nn", F32, "sgu_out")
            saved.append(dict(x=x, h=h, uvz=uvz, y=y, out=out, w_in=w_in, w_out=w_out, bias=bias_full, ng=ng))
        else:
            w_in, w_uq, w_ukv, w_out = _gather2_wait(gathers[(i, "all")], f"ag_wait_{i}_all", h)
            w_in = _pad_mla_w_in(w_in)
            w_uq = _pad_w_uq(w_uq)
            gq, gkv = gq_pad[j:j + 1], gkv_full[j:j + 1]
            proj = _mm(h, w_in, "nn", F32, "mla_in")
            qn, kvn, kr = _mla_norm_fwd(proj, gq, gkv, ktabs, "mla_norm_fwd")
            q = _mm(qn, w_uq, "nn", F32, "mla_uq")
            kv = _mm(kvn, w_ukv, "nn", BF16, "mla_ukv")
            o, y, lse, q_cat = _attn_fwd(q, kv, kr, proj, ktabs, "attn_fwd")
            y = forward_gathers(i + 1, y)
            out = _mm(y, w_out, "nn", F32, "mla_out")
            saved.append(dict(x=x, h=h, proj=proj, qn=qn, kvn=kvn, kr=kr, q_cat=q_cat, kv=kv, o=o, y=y, lse=lse,
                              out=out, w_in=w_in, w_uq=w_uq, w_ukv=w_ukv, w_out=w_out, gq=gq, gkv=gkv))
        x = _post_fwd(x, out, gate[i], A['post_g'][i:i + 1], "post_fwd")

    dx, loss_row = _loss_head(x, tgt, "loss_head")
    loss = lax.psum(loss_row[0, 0], ("x", "y", "c"))

    d_shift, d_scale, d_gate = [None] * DEPTH, [None] * DEPTH, [None] * DEPTH
    d_pre, d_post = [None] * DEPTH, [None] * DEPTH
    scatters, scatters_out = [None] * DEPTH, [None] * DEPTH
    small, small_grads, small_handles = {}, {}, {}
    for i in reversed(range(DEPTH)):
        j = i // N_MIX
        sv = saved[i]
        dout, d_gate[i], d_post[i] = _post_bwd(dx, sv['out'], gate[i], A['post_g'][i:i + 1], "post_bwd")
        if i % N_MIX == 0:
            dy = _mm(dout, sv['w_out'], "nt", F32, "sgu_out_dx")
            g_w_out = _mm(sv['y'], dout, "tn", BF16, "sgu_out_dw")
            scatters_out[i], token, (dy,) = _exchange_start([g_w_out], ["row"], False, f"rs_out_start_{i}", token,
                                                            carry=[dy])
            duvz, dws, dbs, dng = _sgu_mid_bwd(sv['uvz'], dy, sv['ng'], A['sgu_w_s'][j], sv['bias'], "sgu_mid_bwd")
            small[('sgu', j)] = (dws, dbs.reshape(SGU_GROUPS, SGU_BLOCK), dng)
            g_w_in = _mm(sv['h'], duvz, "tn", BF16, "sgu_in_dw")
            if i == 0:
                scatters[i], token, (duvz,) = _scatter2_start(g_w_in, "rs2_start_0", token, carry=[duvz])
                small_grads["b"] = [d_gate[0], d_post[0], dng, dws[None], small[('sgu', 0)][1][None]]
                small_handles["b"], token, (duvz,) = _exchange_start(
                    [_pack(small_grads["b"])], ["row"], True, "ag_small_b_start", token, carry=[duvz])
            else:
                scatters[i], token, (duvz,) = _exchange_start([g_w_in], ["col"], False, f"rs_start_{i}", token,
                                                              carry=[duvz])
            dh = _mm(duvz, sv['w_in'], "nt", F32, "sgu_in_dx")
        else:
            dy = _mm(dout, sv['w_out'], "nt", F32, "mla_out_dx")
            g_w_out = _mm(sv['y'], dout, "tn", BF16, "mla_out_dw")
            scatters_out[i], token, (dy,) = _exchange_start([g_w_out], ["row"], False, f"rs_out_start_{i}", token,
                                                            carry=[dy])
            do, dproj = _mla_gate_bwd(dy, sv['o'], sv['proj'], "mla_gate_bwd")
            dq_b, dkv, dkr_heads = _attn_bwd(sv['q_cat'], sv['kv'], sv['kr'], do, sv['o'], sv['lse'], ktabs,
                                             "attn_bwd")
            dqn = _mm(dq_b, sv['w_uq'], "nt", F32, "mla_uq_dx")
            g_w_uq = _unpad_w_uq(_mm(sv['qn'], dq_b, "tn", BF16, "mla_uq_dw"))
            dkvn = _mm(dkv, sv['w_ukv'], "nt", F32, "mla_ukv_dx")
            g_w_ukv = _mm(sv['kvn'], dkv, "tn", BF16, "mla_ukv_dw")
            dproj, dgq, dgkv = _mla_norm_bwd(dqn, dkvn, dkr_heads, sv['proj'], sv['gq'], sv['gkv'], ktabs, dproj,
                                             "mla_norm_bwd")
            g_w_in = _unpad_mla_w_in(_mm(sv['h'], dproj, "tn", BF16, "mla_in_dw"))
            scatters[i], token, (dproj,) = _exchange_start(
                [g_w_in, g_w_uq, g_w_ukv], ["col", "col", "col"], False, f"rs_start_{i}", token, carry=[dproj])
            dh = _mm(dproj, sv['w_in'], "nt", F32, "mla_in_dx")
            small[('mla', j)] = (dgq[:, :Q_RANK], dgkv)
        dx, d_shift[i], d_scale[i], d_pre[i] = _pre_bwd(dh, sv['x'], dx, A['pre_g'][i:i + 1], scale[i], "pre_bwd")
        if i == 1:
            small_grads["a"] = [
                jnp.concatenate([jnp.concatenate([d_shift[l], d_scale[l], d_gate[l]], axis=1)
                                 for l in range(1, DEPTH)], axis=0),
                jnp.concatenate(d_pre[1:], axis=0), jnp.concatenate(d_post[1:], axis=0),
                small[('sgu', 1)][2], small[('sgu', 1)][0][None], small[('sgu', 1)][1][None],
                jnp.concatenate([small[('mla', jj)][0] for jj in range(N_MIX)], axis=0),
                jnp.concatenate([small[('mla', jj)][1] for jj in range(N_MIX)], axis=0)]
            small_handles["a"], token, (dx,) = _exchange_start(
                [_pack(small_grads["a"])], ["row"], True, "ag_small_a_start", token, carry=[dx])

    res = {}

    def big(name, recv0, recv1):
        res[name] = _adam_reduce(recv0, recv1, A[name], A['m_' + name], A['v_' + name], "adam_" + name)

    def small_full(name, prefix):
        t = A[prefix + name]
        if name in ('mla_q_norm_g', 'mla_kv_norm_g'):
            t = lax.dynamic_update_slice_in_dim(jnp.zeros((t.shape[0], t.shape[1] * N_DEV), F32), t,
                                                me * t.shape[1], axis=1)
        return t

    def finish(i, after):
        first = ([_scatter2_wait(scatters[i], "rs2_wait_0", after)] if i == 0
                 else _exchange_wait(scatters[i], f"rs_wait_{i}", after))
        return first + _exchange_wait(scatters_out[i], f"rs_out_wait_{i}", after)

    recv_mla = {j: finish(N_MIX * j + 1, dx) for j in reversed(range(N_MIX))}
    for idx_w, name in enumerate(['mla_w_in', 'mla_w_uq', 'mla_w_ukv', 'mla_w_out']):
        big(name, recv_mla[0][idx_w], recv_mla[1][idx_w])

    recv_sgu = {j: finish(N_MIX * j, res['mla_w_out'][0]) for j in reversed(range(N_MIX))}
    for idx_w, name in enumerate(['sgu_w_in', 'sgu_w_out']):
        big(name, recv_sgu[0][idx_w], recv_sgu[1][idx_w])

    small_grads["c"] = [jnp.concatenate([d_shift[0], d_scale[0]], axis=1), d_pre[0]]
    gathered = {tag: _exchange_wait(small_handles[tag], f"ag_small_{tag}_wait", res['sgu_w_in'][0])[0]
                for tag in ("a", "b")}
    (gathered["c"],) = _exchange([_pack(small_grads["c"])], ["row"], True, "ag_small_c", after=res['sgu_w_in'][0])

    upd, parts = {}, {}
    for tag, group in SMALL_GROUPS.items():
        shapes = [g.shape for g in small_grads[tag]]
        packs = [_pack([pick(small_full(name, prefix)) for name, pick in group]) for prefix in ('', 'm_', 'v_')]
        outs4 = _adam_small(gathered[tag].reshape(N_DEV, -1, LANE), *packs, "adam_small_" + tag)
        upd[tag] = [_unpack(t, shapes) for t in outs4]
        parts[tag] = _unpack(gathered[tag].reshape(N_DEV, -1), shapes, lead=(N_DEV,))
    for k_out in range(4):
        ab_rest, pre_rest, post_rest, ng_1, ws_1, bs_1, gq_all, gkv_all = upd["a"][k_out]
        ab_gate0, post_0, ng_0, ws_0, bs_0 = upd["b"][k_out]
        ab_0, pre_0 = upd["c"][k_out]
        for name, val in (
                ('ada_b', jnp.concatenate([jnp.concatenate([ab_0, ab_gate0], axis=1), ab_rest], axis=0)),
                ('pre_g', jnp.concatenate([pre_0, pre_rest], axis=0)),
                ('post_g', jnp.concatenate([post_0, post_rest], axis=0)),
                ('sgu_norm_g', jnp.concatenate([ng_0, ng_1], axis=0)),
                ('sgu_w_s', jnp.concatenate([ws_0, ws_1], axis=0)),
                ('sgu_b_s', jnp.concatenate([bs_0, bs_1], axis=0)),
                ('mla_q_norm_g', gq_all), ('mla_kv_norm_g', gkv_all)):
            if name in ('mla_q_norm_g', 'mla_kv_norm_g'):
                wdt = A[name].shape[1]
                val = lax.dynamic_slice_in_dim(val, me * wdt, wdt, axis=1)
            res.setdefault(name, [None] * 4)[k_out] = val

    dmod_all = jnp.concatenate([jnp.concatenate([parts["c"][0], parts["b"][0]], axis=2), parts["a"][0]], axis=1)
    dmod_cols = jnp.transpose(lax.dynamic_slice_in_dim(dmod_all, me * ncol, ncol, axis=2), (1, 0, 2))
    res['ada_w'] = _ada_bwd_adam(jnp.transpose(cond_raw), dmod_cols, A['ada_w'], A['m_ada_w'], A['v_ada_w'], "ada_bwd")

    outs = [loss, dx[None]]
    for k_out in range(4):
        outs += [res[n][k_out] for n in WEIGHTS]
    return tuple(outs)
```

```python
import functools
import math

import numpy as np
import jax
import jax.numpy as jnp
from jax import lax
from jax.experimental import pallas as pl
from jax.experimental.pallas import tpu as pltpu

F32 = jnp.float32
BF16 = jnp.bfloat16
MESH = pl.DeviceIdType.MESH

N_DEV = 8
DEPTH = 4
N_MIX = 2
NORM_EPS = 1e-6
CHUNK = 64
SGU_BLOCK = 128
SGU_GROUPS = 16
HEADS = 16
Q_RANK = 448
Q_RANK_PAD = 512
KV_RANK = 512
NOPE = 128
ROPE = 64
HALF = ROPE // 2
V_DIM = 128
HEAD_PAD = 256
ROPE_THETA = 10000.0
MLA_WIDTH = HEADS * V_DIM
LANE = 128
SUBLANE = 8
PROJ_CQ = 0
PROJ_CKV = Q_RANK_PAD
PROJ_KR = Q_RANK_PAD + KV_RANK
PROJ_Z = PROJ_KR + LANE
PROJ_W = PROJ_Z + MLA_WIDTH

ADAM_LR = 0.001
ADAM_B1 = 0.9
ADAM_B2 = 0.999
ADAM_EPS = 1e-08
ADAM_WD = 0.01
ADAM_STEP = 10

VMEM_LIMIT = 56 * 1024 * 1024
ATT_BLK = 512
ATT_SUB = 128
ATT_HEADS_PER_STEP = 2
ROW_BLK = 512
MM_TM, MM_TN, MM_TK = 1024, 1024, 2048
MM_TILE_BYTES = 40 * 1024 * 1024
SOFTMAX_SCALE = (NOPE + ROPE) ** -0.5
LOG2_E = 1.0 / math.log(2.0)
INV_SQRT2 = 1.0 / math.sqrt(2.0)
INV_SQRT_2PI = 1.0 / math.sqrt(2.0 * math.pi)


def _pcall(body, comm=False, **kw):
    return pl.pallas_call(body, **kw)


def _params(sem=None):
    return pltpu.CompilerParams(dimension_semantics=sem, vmem_limit_bytes=VMEM_LIMIT)


def _pick(dim, pref):
    if dim <= pref:
        return dim
    t = (pref // LANE) * LANE
    while t >= LANE:
        if dim % t == 0:
            return t
        t -= LANE
    return dim


def _gelu(x):
    return 0.5 * x * (1.0 + lax.erf(x * INV_SQRT2))


def _gelu_grad(x):
    return 0.5 * (1.0 + lax.erf(x * INV_SQRT2)) + x * jnp.exp(-0.5 * x * x) * INV_SQRT_2PI


def _sigmoid(x):
    return 1.0 / (1.0 + jnp.exp(-x))


def _dot_nt(a, b):
    return lax.dot_general(a, b, (((1,), (1,)), ((), ())), preferred_element_type=F32)


def _dot_tn(a, b):
    return lax.dot_general(a, b, (((0,), (0,)), ((), ())), preferred_element_type=F32)


def _mm(a, b, dims, out_dtype, name):
    if dims == "nn":
        (m, k), (k2, n) = a.shape, b.shape
    elif dims == "nt":
        (m, k), (n, k2) = a.shape, b.shape
    else:
        (k, m), (k2, n) = a.shape, b.shape
    assert k == k2, (a.shape, b.shape, dims)
    tm, tn = _pick(m, MM_TM), _pick(n, MM_TN)
    out_bytes = 2 * tm * tn * jnp.dtype(out_dtype).itemsize
    whole_k = 2 * (tm + tn) * k * a.dtype.itemsize + out_bytes <= MM_TILE_BYTES
    tk = k if whole_k else _pick(k, MM_TK)
    nk = k // tk

    def body(a_ref, b_ref, o_ref, *scratch):
        if dims == "nn":
            p = jnp.dot(a_ref[...], b_ref[...], preferred_element_type=F32)
        elif dims == "nt":
            p = _dot_nt(a_ref[...], b_ref[...])
        else:
            p = _dot_tn(a_ref[...], b_ref[...])
        if nk == 1:
            o_ref[...] = p.astype(o_ref.dtype)
            return
        acc_ref, = scratch
        kk = pl.program_id(2)

        @pl.when(kk == 0)
        def _():
            acc_ref[...] = p

        @pl.when(kk > 0)
        def _():
            acc_ref[...] += p

        @pl.when(kk == nk - 1)
        def _():
            o_ref[...] = acc_ref[...].astype(o_ref.dtype)

    if dims == "tn":
        a_spec = pl.BlockSpec((tk, tm), lambda i, j, kk: (kk, i))
    else:
        a_spec = pl.BlockSpec((tm, tk), lambda i, j, kk: (i, kk))
    if dims == "nt":
        b_spec = pl.BlockSpec((tn, tk), lambda i, j, kk: (j, kk))
    else:
        b_spec = pl.BlockSpec((tk, tn), lambda i, j, kk: (kk, j))
    return _pcall(
        body, name=name,
        grid=(m // tm, n // tn, nk),
        in_specs=[a_spec, b_spec],
        out_specs=pl.BlockSpec((tm, tn), lambda i, j, kk: (i, j)),
        out_shape=jax.ShapeDtypeStruct((m, n), out_dtype),
        scratch_shapes=[pltpu.VMEM((tm, tn), F32)] if nk > 1 else [],
        compiler_params=_params(("parallel", "parallel", "arbitrary")),
    )(a, b)


def _row_spec(ts, d):
    return pl.BlockSpec((ts, d), lambda i: (i, 0))


def _vec_spec(d):
    return pl.BlockSpec((1, d), lambda i: (0, 0))


def _pre_fwd(x, g, scale, shift, name):
    s, d = x.shape
    ts = _pick(s, ROW_BLK)

    def body(x_ref, g_ref, sc_ref, sh_ref, h_ref):
        xv = x_ref[...]
        r = lax.rsqrt(jnp.mean(xv * xv, axis=-1, keepdims=True) + NORM_EPS)
        h_ref[...] = ((xv * r * g_ref[...]) * (1.0 + sc_ref[...]) + sh_ref[...]).astype(BF16)

    return _pcall(
        body, name=name, grid=(s // ts,),
        in_specs=[_row_spec(ts, d), _vec_spec(d), _vec_spec(d), _vec_spec(d)],
        out_specs=_row_spec(ts, d),
        out_shape=jax.ShapeDtypeStruct((s, d), BF16),
        compiler_params=_params(("parallel",)),
    )(x, g, scale, shift)


def _post_fwd(x, out, gate, g, name):
    s, d = x.shape
    ts = _pick(s, ROW_BLK)

    def body(x_ref, o_ref, gate_ref, g_ref, y_ref):
        o = o_ref[...]
        r = lax.rsqrt(jnp.mean(o * o, axis=-1, keepdims=True) + NORM_EPS)
        y_ref[...] = x_ref[...] + gate_ref[...] * (o * r * g_ref[...])

    return _pcall(
        body, name=name, grid=(s // ts,),
        in_specs=[_row_spec(ts, d), _row_spec(ts, d), _vec_spec(d), _vec_spec(d)],
        out_specs=_row_spec(ts, d),
        out_shape=jax.ShapeDtypeStruct((s, d), F32),
        compiler_params=_params(("parallel",)),
    )(x, out, gate, g)


def _loss_head(xf, tgt, name):
    s, d = xf.shape
    ts = _pick(s, ROW_BLK)
    ns = s // ts

    def body(x_ref, t_ref, dx_ref, loss_ref, acc_ref):
        i = pl.program_id(0)

        @pl.when(i == 0)
        def _():
            acc_ref[...] = jnp.zeros_like(acc_ref)

        e = x_ref[...] - t_ref[...]
        dx_ref[...] = e * (1.0 / d)
        acc_ref[...] += jnp.sum(e * e, axis=0, keepdims=True)

        @pl.when(i == ns - 1)
        def _():
            tot = jnp.sum(acc_ref[...], axis=1, keepdims=True) * (0.5 / d)
            loss_ref[...] = jnp.broadcast_to(tot, loss_ref.shape)

    return _pcall(
        body, name=name, grid=(ns,),
        in_specs=[_row_spec(ts, d), _row_spec(ts, d)],
        out_specs=[_row_spec(ts, d), pl.BlockSpec((1, LANE), lambda i: (0, 0))],
        out_shape=[jax.ShapeDtypeStruct((s, d), F32), jax.ShapeDtypeStruct((1, LANE), F32)],
        scratch_shapes=[pltpu.VMEM((1, d), F32)],
        compiler_params=_params(("arbitrary",)),
    )(xf, tgt)


def _post_bwd(dxo, out, gate, g, name):
    s, d = dxo.shape
    ts = _pick(s, ROW_BLK)

    def body(dx_ref, o_ref, gate_ref, g_ref, do_ref, dgate_ref, dg_ref):
        i = pl.program_id(0)

        @pl.when(i == 0)
        def _():
            dgate_ref[...] = jnp.zeros_like(dgate_ref)
            dg_ref[...] = jnp.zeros_like(dg_ref)

        o = o_ref[...]
        dx = dx_ref[...]
        gv = g_ref[...]
        r = lax.rsqrt(jnp.mean(o * o, axis=-1, keepdims=True) + NORM_EPS)
        n = o * r
        dyn = dx * gate_ref[...]
        dgate_ref[...] += jnp.sum(dx * (n * gv), axis=0, keepdims=True)
        dg_ref[...] += jnp.sum(dyn * n, axis=0, keepdims=True)
        dn = dyn * gv
        do_ref[...] = (r * (dn - n * jnp.mean(dn * n, axis=-1, keepdims=True))).astype(BF16)

    return _pcall(
        body, name=name, grid=(s // ts,),
        in_specs=[_row_spec(ts, d), _row_spec(ts, d), _vec_spec(d), _vec_spec(d)],
        out_specs=[_row_spec(ts, d), _vec_spec(d), _vec_spec(d)],
        out_shape=[jax.ShapeDtypeStruct((s, d), BF16), jax.ShapeDtypeStruct((1, d), F32),
                   jax.ShapeDtypeStruct((1, d), F32)],
        compiler_params=_params(("arbitrary",)),
    )(dxo, out, gate, g)


def _pre_bwd(dh, x, dxo, g, scale, name):
    s, d = x.shape
    ts = _pick(s, ROW_BLK)

    def body(dh_ref, x_ref, dxo_ref, g_ref, sc_ref, dx_ref, dsh_ref, dsc_ref, dg_ref):
        i = pl.program_id(0)

        @pl.when(i == 0)
        def _():
            dsh_ref[...] = jnp.zeros_like(dsh_ref)
            dsc_ref[...] = jnp.zeros_like(dsc_ref)
            dg_ref[...] = jnp.zeros_like(dg_ref)

        xv = x_ref[...]
        dhv = dh_ref[...]
        gv = g_ref[...]
        one_sc = 1.0 + sc_ref[...]
        r = lax.rsqrt(jnp.mean(xv * xv, axis=-1, keepdims=True) + NORM_EPS)
        n = xv * r
        dsh_ref[...] += jnp.sum(dhv, axis=0, keepdims=True)
        dsc_ref[...] += jnp.sum(dhv * (n * gv), axis=0, keepdims=True)
        dng = dhv * one_sc
        dg_ref[...] += jnp.sum(dng * n, axis=0, keepdims=True)
        dn = dng * gv
        dx_ref[...] = dxo_ref[...] + r * (dn - n * jnp.mean(dn * n, axis=-1, keepdims=True))

    return _pcall(
        body, name=name, grid=(s // ts,),
        in_specs=[_row_spec(ts, d), _row_spec(ts, d), _row_spec(ts, d), _vec_spec(d), _vec_spec(d)],
        out_specs=[_row_spec(ts, d), _vec_spec(d), _vec_spec(d), _vec_spec(d)],
        out_shape=[jax.ShapeDtypeStruct((s, d), F32)] + [jax.ShapeDtypeStruct((1, d), F32)] * 3,
        compiler_params=_params(("arbitrary",)),
    )(dh, x, dxo, g, scale)


def _sgu_mask():
    t = lax.broadcasted_iota(jnp.int32, (SGU_BLOCK, SGU_BLOCK), 0) // CHUNK
    s = lax.broadcasted_iota(jnp.int32, (SGU_BLOCK, SGU_BLOCK), 1) // CHUNK
    return s <= t


def _sgu_norm(v_pre, g):
    e = v_pre.shape[-1]
    vg = _gelu(v_pre)
    mu = jnp.sum(vg, axis=-1, keepdims=True) * (1.0 / e)
    dlt = vg - mu
    var = jnp.sum(dlt * dlt, axis=-1, keepdims=True) * (1.0 / e)
    rstd = lax.rsqrt(var + NORM_EPS)
    vhat = dlt * rstd
    return vhat, rstd, (vhat * g).astype(BF16)


def _sgu_mid_fwd(uvz, norm_g, w_s, bias_full, name):
    s, e3 = uvz.shape
    e = e3 // 3
    gd = e // SGU_GROUPS
    nb = s // SGU_BLOCK

    def body(uvz_ref, g_ref, w_ref, b_ref, y_ref, wsc):
        @pl.when(pl.program_id(0) == 0)
        def _():
            msk = _sgu_mask()
            for gi in range(SGU_GROUPS):
                wsc[gi] = jnp.where(msk, w_ref[gi], 0.0).astype(BF16)

        _, _, vb = _sgu_norm(uvz_ref[:, e:2 * e], g_ref[...])
        for gi in range(SGU_GROUPS):
            lo = gi * gd
            vm = jnp.dot(wsc[gi], vb[:, lo:lo + gd], preferred_element_type=F32) + b_ref[:, lo:lo + gd]
            zg = uvz_ref[:, 2 * e + lo:2 * e + lo + gd]
            y_ref[:, lo:lo + gd] = (_gelu(uvz_ref[:, lo:lo + gd]) * vm * (zg * _sigmoid(zg))).astype(BF16)

    return _pcall(
        body, name=name, grid=(nb,),
        in_specs=[pl.BlockSpec((SGU_BLOCK, e3), lambda n: (n, 0)),
                  pl.BlockSpec((1, e), lambda n: (0, 0)),
                  pl.BlockSpec((SGU_GROUPS, SGU_BLOCK, SGU_BLOCK), lambda n: (0, 0, 0)),
                  pl.BlockSpec((SGU_BLOCK, e), lambda n: (0, 0))],
        out_specs=pl.BlockSpec((SGU_BLOCK, e), lambda n: (n, 0)),
        out_shape=jax.ShapeDtypeStruct((s, e), BF16),
        scratch_shapes=[pltpu.VMEM((SGU_GROUPS, SGU_BLOCK, SGU_BLOCK), BF16)],
        compiler_params=_params(("arbitrary",)),
    )(uvz, norm_g, w_s, bias_full)


def _sgu_mid_bwd(uvz, dy, norm_g, w_s, bias_full, name):
    s, e3 = uvz.shape
    e = e3 // 3
    gd = e // SGU_GROUPS
    nb = s // SGU_BLOCK

    def body(uvz_ref, dy_ref, g_ref, w_ref, b_ref, d_ref, dw_ref, db_ref, dg_ref, wsc, wtsc, dvh_sc, dbacc):
        n = pl.program_id(0)

        @pl.when(n == 0)
        def _():
            msk = _sgu_mask()
            for gi in range(SGU_GROUPS):
                wm = jnp.where(msk, w_ref[gi], 0.0)
                wsc[gi] = wm.astype(BF16)
                wtsc[gi] = wm.T.astype(BF16)
            dw_ref[...] = jnp.zeros_like(dw_ref)
            dg_ref[...] = jnp.zeros_like(dg_ref)
            dbacc[...] = jnp.zeros_like(dbacc)

        v_pre = uvz_ref[:, e:2 * e]
        gv = g_ref[...]
        vhat, rstd, vb = _sgu_norm(v_pre, gv)
        s1 = jnp.zeros((SGU_BLOCK, 1), F32)
        s2 = jnp.zeros((SGU_BLOCK, 1), F32)
        for gi in range(SGU_GROUPS):
            lo = gi * gd
            u_pre = uvz_ref[:, lo:lo + gd]
            zg = uvz_ref[:, 2 * e + lo:2 * e + lo + gd]
            dyg = dy_ref[:, lo:lo + gd]
            ug = _gelu(u_pre)
            sig = _sigmoid(zg)
            vbg = vb[:, lo:lo + gd]
            vhg = vhat[:, lo:lo + gd]
            vm = jnp.dot(wsc[gi], vbg, preferred_element_type=F32) + b_ref[:, lo:lo + gd]
            t = dyg * (zg * sig)
            d_ref[:, lo:lo + gd] = (t * vm * _gelu_grad(u_pre)).astype(BF16)
            dvm = t * ug
            d_ref[:, 2 * e + lo:2 * e + lo + gd] = (dyg * ug * vm * (sig * (1.0 + zg * (1.0 - sig)))).astype(BF16)
            dvm_b = dvm.astype(BF16)
            dv = jnp.dot(wtsc[gi], dvm_b, preferred_element_type=F32)
            dw_ref[gi] += _dot_nt(dvm_b, vbg)
            dbacc[:, lo:lo + gd] += dvm
            dg_ref[:, lo:lo + gd] += jnp.sum(dv * vhg, axis=0, keepdims=True)
            dvh = dv * gv[:, lo:lo + gd]
            dvh_sc[:, lo:lo + gd] = dvh
            s1 = s1 + jnp.sum(dvh, axis=-1, keepdims=True)
            s2 = s2 + jnp.sum(dvh * vhg, axis=-1, keepdims=True)
        dvg = rstd * (dvh_sc[...] - s1 * (1.0 / e) - vhat * (s2 * (1.0 / e)))
        d_ref[:, e:2 * e] = (dvg * _gelu_grad(v_pre)).astype(BF16)

        @pl.when(n == nb - 1)
        def _():
            msk = _sgu_mask()
            for gi in range(SGU_GROUPS):
                dw_ref[gi] = jnp.where(msk, dw_ref[gi], 0.0)
                db_ref[gi] = jnp.sum(dbacc[:, gi * gd:(gi + 1) * gd], axis=1, keepdims=True)

    return _pcall(
        body, name=name, grid=(nb,),
        in_specs=[pl.BlockSpec((SGU_BLOCK, e3), lambda n: (n, 0)),
                  pl.BlockSpec((SGU_BLOCK, e), lambda n: (n, 0)),
                  pl.BlockSpec((1, e), lambda n: (0, 0)),
                  pl.BlockSpec((SGU_GROUPS, SGU_BLOCK, SGU_BLOCK), lambda n: (0, 0, 0)),
                  pl.BlockSpec((SGU_BLOCK, e), lambda n: (0, 0))],
        out_specs=[pl.BlockSpec((SGU_BLOCK, e3), lambda n: (n, 0)),
                   pl.BlockSpec((SGU_GROUPS, SGU_BLOCK, SGU_BLOCK), lambda n: (0, 0, 0)),
                   pl.BlockSpec((SGU_GROUPS, SGU_BLOCK, 1), lambda n: (0, 0, 0)),
                   pl.BlockSpec((1, e), lambda n: (0, 0))],
        out_shape=[jax.ShapeDtypeStruct((s, e3), BF16),
                   jax.ShapeDtypeStruct((SGU_GROUPS, SGU_BLOCK, SGU_BLOCK), F32),
                   jax.ShapeDtypeStruct((SGU_GROUPS, SGU_BLOCK, 1), F32),
                   jax.ShapeDtypeStruct((1, e), F32)],
        scratch_shapes=[pltpu.VMEM((SGU_GROUPS, SGU_BLOCK, SGU_BLOCK), BF16),
                        pltpu.VMEM((SGU_GROUPS, SGU_BLOCK, SGU_BLOCK), BF16),
                        pltpu.VMEM((SGU_BLOCK, e), F32),
                        pltpu.VMEM((SGU_BLOCK, e), F32)],
        compiler_params=_params(("arbitrary",)),
    )(uvz, dy, norm_g, w_s, bias_full)


def _rope_tables(s):
    pos = jnp.arange(s, dtype=F32)
    inv_freq = ROPE_THETA ** (-jnp.arange(0, ROPE, 2, dtype=F32) / ROPE)
    ang = pos[:, None] * inv_freq[None, :]
    cos, sin = jnp.cos(ang), jnp.sin(ang)
    z32 = jnp.zeros((s, HALF), F32)
    z64 = jnp.zeros((s, ROPE), F32)
    ck = jnp.concatenate([cos, cos, z64], axis=1)
    s1k = jnp.concatenate([-sin, z32, z64], axis=1)
    s2k = jnp.concatenate([z32, sin, z64], axis=1)
    return ck, s1k, s2k


def _rot(x, c, s1, s2):
    w = x.shape[-1]
    return x * c + pltpu.roll(x, w - HALF, 1) * s1 + pltpu.roll(x, HALF, 1) * s2


def _rms(cv, n_real):
    r = lax.rsqrt(jnp.sum(cv * cv, axis=-1, keepdims=True) * (1.0 / n_real) + NORM_EPS)
    return r, cv * r


def _mla_norm_fwd(proj, gq, gkv, tabs, name):
    s = proj.shape[0]
    ts = _pick(s, ROW_BLK)
    ck, s1k, s2k = tabs

    def body(cq_ref, ckv_ref, kr_ref, gq_ref, gkv_ref, c_ref, s1_ref, s2_ref, qn_ref, kvn_ref, kro_ref):
        _, nq = _rms(cq_ref[...], Q_RANK)
        qn_ref[...] = (nq * gq_ref[...]).astype(BF16)
        _, nkv = _rms(ckv_ref[...], KV_RANK)
        kvn_ref[...] = (nkv * gkv_ref[...]).astype(BF16)
        kro_ref[...] = _rot(kr_ref[...], c_ref[...], s1_ref[...], s2_ref[...]).astype(BF16)

    tab = pl.BlockSpec((ts, LANE), lambda i: (i, 0))
    return _pcall(
        body, name=name, grid=(s // ts,),
        in_specs=[pl.BlockSpec((ts, Q_RANK_PAD), lambda i: (i, 0)),
                  pl.BlockSpec((ts, KV_RANK), lambda i: (i, PROJ_CKV // KV_RANK)),
                  pl.BlockSpec((ts, LANE), lambda i: (i, PROJ_KR // LANE)),
                  _vec_spec(Q_RANK_PAD), _vec_spec(KV_RANK), tab, tab, tab],
        out_specs=[pl.BlockSpec((ts, Q_RANK_PAD), lambda i: (i, 0)),
                   pl.BlockSpec((ts, KV_RANK), lambda i: (i, 0)), tab],
        out_shape=[jax.ShapeDtypeStruct((s, Q_RANK_PAD), BF16), jax.ShapeDtypeStruct((s, KV_RANK), BF16),
                   jax.ShapeDtypeStruct((s, LANE), BF16)],
        compiler_params=_params(("parallel",)),
    )(proj, proj, proj, gq, gkv, ck, s1k, s2k)


def _transpose_bf16(t):
    return t.astype(F32).T.astype(BF16)


def _diag_mask(tb, transposed):
    r = lax.broadcasted_iota(jnp.int32, (tb, tb), 0) // CHUNK
    c = lax.broadcasted_iota(jnp.int32, (tb, tb), 1) // CHUNK
    return (r <= c) if transposed else (c <= r)


def _attn_fwd(q, kv, kr, proj, tabs, name):
    s = q.shape[0]
    tb = _pick(s, ATT_BLK)
    nb = s // tb
    zcol = PROJ_Z // V_DIM
    mult = SOFTMAX_SCALE * LOG2_E

    hp = ATT_HEADS_PER_STEP

    def body(qf_ref, c_ref, s1_ref, s2_ref, kv_ref, kr_ref, *rest):
        z_refs = rest[:hp]
        o_ref, y_ref, lse_ref, q_ref = rest[hp:hp + 4]
        scratch = rest[hp + 4:]
        kt_sc, vx_sc, m_sc, acc_sc, sa_sc, sb_sc = (scratch[i * hp:(i + 1) * hp] for i in range(6))
        qi = pl.program_id(1)
        heads = range(hp)
        for hh in heads:
            lo = hh * HEAD_PAD
            q_ref[:, lo:lo + NOPE] = (qf_ref[:, lo:lo + NOPE] * mult).astype(BF16)
            q_ref[:, lo + NOPE:lo + HEAD_PAD] = (
                _rot(qf_ref[:, lo + NOPE:lo + HEAD_PAD], c_ref[...], s1_ref[...], s2_ref[...]) * mult).astype(BF16)

        @pl.when(qi == 0)
        def _():
            for hh in heads:
                kcol = hh * (NOPE + V_DIM)
                for b in range(nb):
                    rows = slice(b * tb, (b + 1) * tb)
                    kt_sc[hh][b] = _transpose_bf16(
                        jnp.concatenate([kv_ref[rows, kcol:kcol + NOPE], kr_ref[rows, :]], axis=1))
                    vx_sc[hh][b] = jnp.concatenate(
                        [kv_ref[rows, kcol + NOPE:kcol + NOPE + V_DIM], jnp.ones((tb, V_DIM), BF16)], axis=1)

        for hh in heads:
            m_sc[hh][...] = jnp.full_like(m_sc[hh], -1e30)
            acc_sc[hh][...] = jnp.zeros_like(acc_sc[hh])
        sub = min(tb, ATT_SUB)

        def scores(ki, bufs):
            for hh in heads:
                bufs[hh][...] = jnp.dot(q_ref[:, hh * HEAD_PAD:(hh + 1) * HEAD_PAD], kt_sc[hh][ki],
                                        preferred_element_type=F32)

        def step(ki, bufs, masked):
            for r in range(tb // sub):
                rs = slice(r * sub, (r + 1) * sub)
                for hh in heads:
                    sc = bufs[hh][rs, :]
                    if masked:
                        sc = jnp.where(_diag_mask(tb, False)[rs, :], sc, -1e30)
                    m_prev = m_sc[hh][rs, :]
                    m_new = jnp.maximum(m_prev, jnp.max(sc, axis=-1, keepdims=True))
                    p = jnp.exp2(sc - m_new).astype(BF16)
                    acc_sc[hh][rs, :] = (jnp.exp2(m_prev - m_new) * acc_sc[hh][rs, :]
                                         + jnp.dot(p, vx_sc[hh][ki], preferred_element_type=F32))
                    m_sc[hh][rs, :] = m_new

        def pair(t, carry):
            scores(2 * t + 1, sb_sc)
            step(2 * t, sa_sc, False)
            scores(2 * t + 2, sa_sc)
            step(2 * t + 1, sb_sc, False)
            return carry

        scores(0, sa_sc)
        lax.fori_loop(0, qi // 2, pair, 0)

        @pl.when(qi % 2 == 1)
        def _():
            scores(qi, sb_sc)
            step(qi - 1, sa_sc, False)
            step(qi, sb_sc, True)

        @pl.when(qi % 2 == 0)
        def _():
            step(qi, sa_sc, True)

        for hh in heads:
            l = acc_sc[hh][:, V_DIM:V_DIM + 1]
            o = acc_sc[hh][:, :V_DIM] / l
            z = z_refs[hh][...]
            o_ref[:, hh * V_DIM:(hh + 1) * V_DIM] = o.astype(BF16)
            y_ref[:, hh * V_DIM:(hh + 1) * V_DIM] = (o * (z * _sigmoid(z))).astype(BF16)
            lse_cols = jnp.broadcast_to(m_sc[hh][...] + jnp.log2(l), (tb, LANE))
            lse_ref[hh] = lse_cols.T[0:1, :]

    oblk = pl.BlockSpec((tb, hp * V_DIM), lambda g, qi: (qi, g))
    qblk = pl.BlockSpec((tb, hp * HEAD_PAD), lambda g, qi: (qi, g))
    tab = pl.BlockSpec((tb, LANE), lambda g, qi: (qi, 0))
    per_head = lambda shape, dtype: [pltpu.VMEM(shape, dtype)] * hp
    return _pcall(
        body, name=name, grid=(HEADS // hp, nb),
        in_specs=[qblk, tab, tab, tab,
                  pl.BlockSpec((s, hp * (NOPE + V_DIM)), lambda g, qi: (0, g)),
                  pl.BlockSpec((s, LANE), lambda g, qi: (0, 0))]
                 + [pl.BlockSpec((tb, V_DIM), functools.partial(lambda g, qi, hh: (qi, zcol + hp * g + hh), hh=hh))
                    for hh in range(hp)],
        out_specs=[oblk, oblk, pl.BlockSpec((hp, None, 1, tb), lambda g, qi: (g, qi, 0, 0)), qblk],
        out_shape=[jax.ShapeDtypeStruct((s, MLA_WIDTH), BF16), jax.ShapeDtypeStruct((s, MLA_WIDTH), BF16),
                   jax.ShapeDtypeStruct((HEADS, nb, 1, tb), F32), jax.ShapeDtypeStruct((s, HEADS * HEAD_PAD), BF16)],
        scratch_shapes=(per_head((nb, HEAD_PAD, tb), BF16) + per_head((nb, tb, HEAD_PAD), BF16)
                        + per_head((tb, 1), F32) + per_head((tb, HEAD_PAD), F32)
                        + per_head((tb, tb), F32) + per_head((tb, tb), F32)),
        compiler_params=_params(("parallel", "arbitrary")),
    )(q, *tabs, kv, kr, *([proj] * hp))


def _attn_bwd(q_cat, kv, kr, do, o, lse, tabs, name):
    s = q_cat.shape[0]
    tb = _pick(s, ATT_BLK)
    nb = s // tb
    ln2 = math.log(2.0)

    def body(q_ref, do_ref, o_ref, lse_ref, kn_ref, v_ref, kr_ref, c_ref, s1_ref, s2_ref, dq_ref, dkv_ref, dkr_ref,
             qt_sc, dot_sc, delta_sc, dqt_sc, dk_sc, dv_sc):
        ki = pl.program_id(1)

        @pl.when(ki == 0)
        def _():
            for b in range(nb):
                rows = slice(b * tb, (b + 1) * tb)
                qt_sc[b] = _transpose_bf16(q_ref[rows, :])
                do_t = do_ref[rows, :].astype(F32).T
                dot_sc[b] = do_t.astype(BF16)
                delta_sc[b] = jnp.sum(do_t * o_ref[rows, :].astype(F32).T, axis=0, keepdims=True)
            dqt_sc[...] = jnp.zeros_like(dqt_sc)

        k = jnp.concatenate([kn_ref[...], kr_ref[...]], axis=1)
        kt = _transpose_bf16(k)
        vb = v_ref[...]
        dk_sc[...] = jnp.zeros_like(dk_sc)
        dv_sc[...] = jnp.zeros_like(dv_sc)

        def step(qi, masked):
            rows = pl.ds(pl.multiple_of(qi * tb, tb), tb)
            sc_t = jnp.dot(k, qt_sc[qi], preferred_element_type=F32)
            if masked:
                sc_t = jnp.where(_diag_mask(tb, True), sc_t, -1e30)
            p_t = jnp.exp2(sc_t - lse_ref[qi])
            dp_t = jnp.dot(vb, dot_sc[qi], preferred_element_type=F32)
            ds_t = (p_t * (dp_t - delta_sc[qi])).astype(BF16)
            dv_sc[...] += jnp.dot(p_t.astype(BF16), do_ref[rows, :], preferred_element_type=F32)
            dk_sc[...] += jnp.dot(ds_t, q_ref[rows, :], preferred_element_type=F32)
            dqt_sc[qi] += jnp.dot(kt, ds_t, preferred_element_type=F32)

        step(ki, True)

        def loop_body(qi, carry):
            step(qi, False)
            return carry

        lax.fori_loop(ki + 1, nb, loop_body, 0)

        dkv_ref[:, :NOPE] = (dk_sc[:, :NOPE] * ln2).astype(BF16)
        dkv_ref[:, NOPE:] = dv_sc[...].astype(BF16)
        dkr_ref[...] = dk_sc[:, NOPE:] * ln2

        @pl.when(ki == nb - 1)
        def _():
            for b in range(nb):
                rows = slice(b * tb, (b + 1) * tb)
                dq = dqt_sc[b].T
                dq_ref[rows, :NOPE] = (dq[:, :NOPE] * SOFTMAX_SCALE).astype(BF16)
                dq_ref[rows, NOPE:] = (_rot(dq[:, NOPE:], c_ref[rows, :], -s1_ref[rows, :], -s2_ref[rows, :])
                                       * SOFTMAX_SCALE).astype(BF16)

    tab = pl.BlockSpec((s, LANE), lambda h, ki: (0, 0))
    return _pcall(
        body, name=name, grid=(HEADS, nb),
        in_specs=[pl.BlockSpec((s, HEAD_PAD), lambda h, ki: (0, h)),
                  pl.BlockSpec((s, V_DIM), lambda h, ki: (0, h)),
                  pl.BlockSpec((s, V_DIM), lambda h, ki: (0, h)),
                  pl.BlockSpec((None, nb, 1, tb), lambda h, ki: (h, 0, 0, 0)),
                  pl.BlockSpec((tb, NOPE), lambda h, ki: (ki, 2 * h)),
                  pl.BlockSpec((tb, V_DIM), lambda h, ki: (ki, 2 * h + 1)),
                  pl.BlockSpec((tb, LANE), lambda h, ki: (ki, 0)), tab, tab, tab],
        out_specs=[pl.BlockSpec((s, HEAD_PAD), lambda h, ki: (0, h)),
                   pl.BlockSpec((tb, HEAD_PAD), lambda h, ki: (ki, h)),
                   pl.BlockSpec((None, tb, LANE), lambda h, ki: (h, ki, 0))],
        out_shape=[jax.ShapeDtypeStruct((s, HEADS * HEAD_PAD), BF16),
                   jax.ShapeDtypeStruct((s, HEADS * HEAD_PAD), BF16),
                   jax.ShapeDtypeStruct((HEADS, s, LANE), F32)],
        scratch_shapes=[pltpu.VMEM((nb, HEAD_PAD, tb), BF16), pltpu.VMEM((nb, V_DIM, tb), BF16),
                        pltpu.VMEM((nb, 1, tb), F32), pltpu.VMEM((nb, HEAD_PAD, tb), F32),
                        pltpu.VMEM((tb, HEAD_PAD), F32), pltpu.VMEM((tb, V_DIM), F32)],
        compiler_params=_params(("parallel", "arbitrary")),
    )(q_cat, do, o, lse, kv, kv, kr, *tabs)


def _mla_gate_bwd(dy, o, proj, name):
    s = dy.shape[0]
    ts = _pick(s, ROW_BLK)

    def body(dy_ref, o_ref, p_ref, do_ref, dp_ref):
        z = p_ref[:, PROJ_Z:]
        dyv = dy_ref[...]
        sig = _sigmoid(z)
        do_ref[...] = (dyv * (z * sig)).astype(BF16)
        dp_ref[:, :PROJ_Z] = jnp.zeros((ts, PROJ_Z), BF16)
        dp_ref[:, PROJ_Z:] = (dyv * o_ref[...].astype(F32) * (sig * (1.0 + z * (1.0 - sig)))).astype(BF16)

    blk = pl.BlockSpec((ts, MLA_WIDTH), lambda i: (i, 0))
    wide = pl.BlockSpec((ts, PROJ_W), lambda i: (i, 0))
    return _pcall(
        body, name=name, grid=(s // ts,),
        in_specs=[blk, blk, wide],
        out_specs=[blk, wide],
        out_shape=[jax.ShapeDtypeStruct((s, MLA_WIDTH), BF16), jax.ShapeDtypeStruct((s, PROJ_W), BF16)],
        compiler_params=_params(("parallel",)),
    )(dy, o, proj)


def _mla_norm_bwd(dqn, dkvn, dkr_heads, proj, gq, gkv, tabs, dproj, name):
    s = proj.shape[0]
    ts = _pick(s, ROW_BLK)
    ck, s1k, s2k = tabs

    def rms_bwd(cv, dn_in, g, n_real):
        r, n = _rms(cv, n_real)
        dg = jnp.sum(dn_in * n, axis=0, keepdims=True)
        dn = dn_in * g
        dc = r * (dn - n * (jnp.sum(dn * n, axis=-1, keepdims=True) * (1.0 / n_real)))
        return dc, dg

    def body(dqn_ref, dkvn_ref, dkr_ref, cq_ref, ckv_ref, gq_ref, gkv_ref, c_ref, s1_ref, s2_ref, dp_in_ref,
             dp_ref, dgq_ref, dgkv_ref):
        @pl.when(pl.program_id(0) == 0)
        def _():
            dgq_ref[...] = jnp.zeros_like(dgq_ref)
            dgkv_ref[...] = jnp.zeros_like(dgkv_ref)

        dcq, dgq = rms_bwd(cq_ref[...], dqn_ref[...], gq_ref[...], Q_RANK)
        dckv, dgkv = rms_bwd(ckv_ref[...], dkvn_ref[...], gkv_ref[...], KV_RANK)
        dgq_ref[...] += dgq
        dgkv_ref[...] += dgkv
        dkr = dkr_ref[0]
        for h in range(1, HEADS):
            dkr = dkr + dkr_ref[h]
        dp_ref[:, PROJ_CQ:PROJ_CKV] = dcq.astype(BF16)
        dp_ref[:, PROJ_CKV:PROJ_KR] = dckv.astype(BF16)
        dp_ref[:, PROJ_KR:PROJ_Z] = _rot(dkr, c_ref[...], -s1_ref[...], -s2_ref[...]).astype(BF16)

    tab = pl.BlockSpec((ts, LANE), lambda i: (i, 0))
    return _pcall(
        body, name=name, grid=(s // ts,),
        in_specs=[pl.BlockSpec((ts, Q_RANK_PAD), lambda i: (i, 0)),
                  pl.BlockSpec((ts, KV_RANK), lambda i: (i, 0)),
                  pl.BlockSpec((HEADS, ts, LANE), lambda i: (0, i, 0)),
                  pl.BlockSpec((ts, Q_RANK_PAD), lambda i: (i, 0)),
                  pl.BlockSpec((ts, KV_RANK), lambda i: (i, PROJ_CKV // KV_RANK)),
                  _vec_spec(Q_RANK_PAD), _vec_spec(KV_RANK), tab, tab, tab, pl.BlockSpec(memory_space=pl.ANY)],
        out_specs=[pl.BlockSpec((ts, PROJ_Z), lambda i: (i, 0)), _vec_spec(Q_RANK_PAD), _vec_spec(KV_RANK)],
        out_shape=[jax.ShapeDtypeStruct((s, PROJ_W), BF16), jax.ShapeDtypeStruct((1, Q_RANK_PAD), F32),
                   jax.ShapeDtypeStruct((1, KV_RANK), F32)],
        input_output_aliases={10: 0},
        compiler_params=_params(("arbitrary",)),
    )(dqn, dkvn, dkr_heads, proj, proj, gq, gkv, ck, s1k, s2k, dproj)


def _ada_mod(cond_raw, ada_w, bias_my, name):
    nl, d, ncol = ada_w.shape
    tk = _pick(d, 512)
    nk = d // tk

    def body(c_ref, w_ref, b_ref, o_ref, acc_ref):
        kk = pl.program_id(1)

        @pl.when(kk == 0)
        def _():
            acc_ref[...] = jnp.zeros_like(acc_ref)

        cv = c_ref[...]
        cond = (cv * _sigmoid(cv)).astype(BF16)
        acc_ref[...] += jnp.dot(cond, w_ref[...].astype(BF16), preferred_element_type=F32)

        @pl.when(kk == nk - 1)
        def _():
            o_ref[...] = acc_ref[...] + b_ref[...]

    return _pcall(
        body, name=name, grid=(nl, nk),
        in_specs=[pl.BlockSpec((N_DEV, tk), lambda l, kk: (0, kk)),
                  pl.BlockSpec((None, tk, ncol), lambda l, kk: (l, kk, 0)),
                  pl.BlockSpec((None, 1, ncol), lambda l, kk: (l, 0, 0))],
        out_specs=pl.BlockSpec((None, N_DEV, ncol), lambda l, kk: (l, 0, 0)),
        out_shape=jax.ShapeDtypeStruct((nl, N_DEV, ncol), F32),
        scratch_shapes=[pltpu.VMEM((N_DEV, ncol), F32)],
        compiler_params=_params(("parallel", "arbitrary")),
    )(cond_raw, ada_w, bias_my.reshape(nl, 1, ncol))


def _adam(w, g, m, v):
    m = ADAM_B1 * m + (1.0 - ADAM_B1) * g
    v = ADAM_B2 * v + (1.0 - ADAM_B2) * (g * g)
    m_hat = m / (1.0 - ADAM_B1 ** ADAM_STEP)
    v_hat = v / (1.0 - ADAM_B2 ** ADAM_STEP)
    delta = -ADAM_LR * (m_hat / (jnp.sqrt(v_hat) + ADAM_EPS) + ADAM_WD * w)
    return delta, m, v


def _ada_bwd_adam(cond_t, dmod_cols, w, m, v, name):
    nl, d, ncol = w.shape
    tk = _pick(d, 512)

    def body(c_ref, dm_ref, w_ref, m_ref, v_ref, g_ref, d_ref, mo_ref, vo_ref):
        cv = c_ref[...]
        cond = (cv * _sigmoid(cv)).astype(BF16)
        g = jnp.dot(cond, dm_ref[...].astype(BF16), preferred_element_type=F32)
        delta, m2, v2 = _adam(w_ref[...], g, m_ref[...], v_ref[...])
        g_ref[...] = g
        d_ref[...] = delta
        mo_ref[...] = m2
        vo_ref[...] = v2

    blk = pl.BlockSpec((None, tk, ncol), lambda l, kk: (l, kk, 0))
    shp = jax.ShapeDtypeStruct((nl, d, ncol), F32)
    return _pcall(
        body, name=name, grid=(nl, d // tk),
        in_specs=[pl.BlockSpec((tk, N_DEV), lambda l, kk: (kk, 0)),
                  pl.BlockSpec((None, N_DEV, ncol), lambda l, kk: (l, 0, 0)), blk, blk, blk],
        out_specs=[blk, blk, blk, blk], out_shape=[shp, shp, shp, shp],
        compiler_params=_params(("parallel", "parallel")),
    )(cond_t, dmod_cols, w, m, v)


def _adam_reduce(recv0, recv1, w, m, v, name):
    nl, r, c = w.shape
    tr = _pick(r, 128) if r % 128 == 0 else r
    tc = _pick(c, 1024)
    n0, n1 = recv0.shape[0] // r, recv1.shape[0] // r

    def body(r0_ref, r1_ref, w_ref, m_ref, v_ref, g_ref, d_ref, mo_ref, vo_ref):
        l = pl.program_id(0)

        def run(rr):
            g = rr[0].astype(F32)
            for sidx in range(1, rr.shape[0]):
                g = g + rr[sidx].astype(F32)
            delta, m2, v2 = _adam(w_ref[...], g, m_ref[...], v_ref[...])
            g_ref[...] = g
            d_ref[...] = delta
            mo_ref[...] = m2
            vo_ref[...] = v2

        @pl.when(l == 0)
        def _():
            run(r0_ref)

        @pl.when(l == 1)
        def _():
            run(r1_ref)

    def rblk(n, layer):
        return pl.BlockSpec((n, tr, tc), lambda l, i, j: (0, jnp.where(l == layer, i, 0), jnp.where(l == layer, j, 0)))

    blk = pl.BlockSpec((None, tr, tc), lambda l, i, j: (l, i, j))
    shp = jax.ShapeDtypeStruct((nl, r, c), F32)
    return _pcall(
        body, name=name, grid=(nl, r // tr, c // tc),
        in_specs=[rblk(n0, 0), rblk(n1, 1), blk, blk, blk],
        out_specs=[blk, blk, blk, blk], out_shape=[shp, shp, shp, shp],
        compiler_params=_params(("arbitrary", "parallel", "parallel")),
    )(recv0.reshape(n0, r, c), recv1.reshape(n1, r, c), w, m, v)


def _adam_small(gathered, w, m, v, name):
    r = w.shape[0]
    tr = _pick(r, 512) if r % 512 == 0 else r

    def body(p_ref, w_ref, m_ref, v_ref, g_ref, d_ref, mo_ref, vo_ref):
        g = p_ref[0]
        for sidx in range(1, N_DEV):
            g = g + p_ref[sidx]
        delta, m2, v2 = _adam(w_ref[...], g, m_ref[...], v_ref[...])
        g_ref[...] = g
        d_ref[...] = delta
        mo_ref[...] = m2
        vo_ref[...] = v2

    blk = pl.BlockSpec((tr, LANE), lambda i: (i, 0))
    shp = jax.ShapeDtypeStruct((r, LANE), F32)
    return _pcall(
        body, name=name, grid=(r // tr,),
        in_specs=[pl.BlockSpec((N_DEV, tr, LANE), lambda i: (0, i, 0)), blk, blk, blk],
        out_specs=[blk, blk, blk, blk], out_shape=[shp, shp, shp, shp],
        compiler_params=_params(("parallel",)),
    )(gathered, w, m, v)


def _my_place():
    x, y, c = lax.axis_index("x"), lax.axis_index("y"), lax.axis_index("c")
    return x, y, c, 4 * x + 2 * y + c


def _peer(x, y, c, k):
    px = 1 - x if (k >> 2) & 1 else x
    py = 1 - y if (k >> 1) & 1 else y
    pc = 1 - c if k & 1 else c
    return (px, py, pc), 4 * px + 2 * py + pc


def _slab(ref, shape, kind, p):
    r, cd = shape
    if kind == "row":
        return ref.at[pl.ds(pl.multiple_of(p * r, SUBLANE), r), :]
    return ref.at[:, pl.ds(pl.multiple_of(p * cd, LANE), cd)]


def _exchange_layout(arrays, kinds, gather):
    shard_shapes, dst_kinds, out_shapes = [], [], []
    for a, kind in zip(arrays, kinds):
        r, cd = a.shape
        if gather:
            shard, dst_kind = (r, cd), kind
        else:
            shard, dst_kind = ((r // N_DEV, cd) if kind == "row" else (r, cd // N_DEV)), "row"
        shard_shapes.append(shard)
        dst_kinds.append(dst_kind)
        full = (shard[0] * N_DEV, shard[1]) if dst_kind == "row" else (shard[0], shard[1] * N_DEV)
        out_shapes.append(jax.ShapeDtypeStruct(full, a.dtype))
    return shard_shapes, dst_kinds, out_shapes


def _exchange_copies(ins, outs, send_sems, recv_sems, sem_of, layout, kinds, gather):
    shard_shapes, dst_kinds, _ = layout
    x, y, c, me = _my_place()

    def src_for(a, p):
        return ins[a] if gather else _slab(ins[a], shard_shapes[a], kinds[a], p)

    def dst_slot(a, p):
        return _slab(outs[a], shard_shapes[a], dst_kinds[a], p)

    def local(a, sem):
        return pltpu.make_async_copy(src_for(a, me), dst_slot(a, me), sem)

    def remote(a, k, slot):
        peer, pidx = _peer(x, y, c, k)
        return pltpu.make_async_remote_copy(
            src_ref=src_for(a, pidx), dst_ref=dst_slot(a, me if slot == "mine" else pidx),
            send_sem=send_sems.at[sem_of(a, k)], recv_sem=recv_sems.at[sem_of(a, k)],
            device_id=peer, device_id_type=MESH)

    return local, remote


def _place_own(src, src_kind, slab_shape, dst_kind, full, name, index=None):
    r, cd = slab_shape
    tr = _pick(r, 512)
    nr = r // tr
    me = _my_place()[3] if index is None else index
    src, layer = src if isinstance(src, tuple) else (src, None)

    def body(me_ref, s_ref, o_ref):
        o_ref[...] = s_ref[...].astype(o_ref.dtype)

    def where(kind):
        if kind is None:
            return lambda i, me_ref: (i, 0)
        if kind == "row":
            return lambda i, me_ref: (me_ref[0] * nr + i, 0)
        return lambda i, me_ref: (i, me_ref[0])

    if layer is None:
        src_spec = pl.BlockSpec((tr, cd), where(src_kind))
    else:
        src_spec = pl.BlockSpec((None, tr, cd), lambda i, me_ref: (layer, *where(src_kind)(i, me_ref)))
    return _pcall(
        body, name=name,
        grid_spec=pltpu.PrefetchScalarGridSpec(
            num_scalar_prefetch=1, grid=(nr,),
            in_specs=[src_spec],
            out_specs=pl.BlockSpec((tr, cd), where(dst_kind))),
        out_shape=jax.ShapeDtypeStruct(full.shape, full.dtype),
        compiler_params=_params(("arbitrary",)),
    )(jnp.reshape(me, (1,)).astype(jnp.int32), src)


def _landing_with_own_slab(arrays, kinds, gather, layout, name=None, order_after=None):
    _, _, _, me = _my_place()
    index = None
    if order_after is not None:
        first = order_after.reshape(-1)[0].astype(jnp.int32)
        index = me + jnp.minimum(jnp.maximum(first, 0), 0)
    lands = []
    for a in range(len(arrays)):
        (r, cd), dst_kind, full = layout[0][a], layout[1][a], layout[2][a]
        if name is not None:
            lands.append(_place_own(arrays[a], None if gather else kinds[a], (r, cd), dst_kind, full, name,
                                    index=index))
            continue
        if gather:
            piece = arrays[a]
        elif kinds[a] == "row":
            piece = lax.dynamic_slice_in_dim(arrays[a], me * r, r, axis=0)
        else:
            piece = lax.dynamic_slice_in_dim(arrays[a], me * cd, cd, axis=1)
        at = (me * r, 0) if dst_kind == "row" else (0, me * cd)
        lands.append(lax.dynamic_update_slice(lax.empty(full.shape, full.dtype), piece, at))
    return lands


def _exchange(arrays, kinds, gather, name, after=None):
    n = len(arrays)
    n_extra = 0 if after is None else 1
    layout = _exchange_layout(arrays, kinds, gather)
    lands = _landing_with_own_slab(arrays, kinds, gather, layout)

    def body(*refs):
        ins, outs = refs[:n], refs[2 * n + n_extra:3 * n + n_extra]
        send_sems, recv_sems = refs[3 * n + n_extra:]
        _, remote = _exchange_copies(ins, outs, send_sems, recv_sems,
                                     lambda a, k: a * (N_DEV - 1) + k - 1, layout, kinds, gather)
        for a in range(n):
            for k in range(1, N_DEV):
                remote(a, k, "mine").start()
        for a in range(n):
            for k in range(1, N_DEV):
                arrival = remote(a, k, "theirs")
                arrival.wait_send()
                arrival.wait_recv()

    anyspec = pl.BlockSpec(memory_space=pl.ANY)
    outs = _pcall(
        body, comm=True, name=name,
        in_specs=[anyspec] * (2 * n + n_extra), out_specs=[anyspec] * n, out_shape=layout[2],
        input_output_aliases={n + a: a for a in range(n)},
        scratch_shapes=[pltpu.SemaphoreType.DMA((n * (N_DEV - 1),)), pltpu.SemaphoreType.DMA((n * (N_DEV - 1),))],
    )(*arrays, *lands, *([] if after is None else [after]))
    return list(outs)


HBM_SPEC = pl.BlockSpec(memory_space=pltpu.HBM)
SEM_SPEC = pl.BlockSpec(memory_space=pltpu.SEMAPHORE)
ANY_SPEC = pl.BlockSpec(memory_space=pl.ANY)
DATAFLOW = pltpu.SideEffectType.DATAFLOW_SIDE_EFFECTING


def _exchange_start(arrays, kinds, gather, name, after, carry=()):
    n, nc = len(arrays), len(carry)
    layout = _exchange_layout(arrays, kinds, gather)
    lands = _landing_with_own_slab(arrays, kinds, gather, layout, "place_own")

    def body(*refs):
        ins, outs = refs[:n], refs[n:2 * n]
        send_sems, recv_sems = refs[2 * n + nc + 1], refs[2 * n + nc + 2]
        token = refs[2 * n + nc + 3 + 2 * n + nc]
        _, remote = _exchange_copies(ins, outs, send_sems, recv_sems, lambda a, k: a, layout, kinds, gather)
        for a in range(n):
            for k in range(1, N_DEV):
                remote(a, k, "mine").start()
        token[...] = jnp.zeros_like(token)

    passed = list(arrays) + lands + list(carry)
    res = pl.pallas_call(
        body, name=name,
        out_shape=(pltpu.SemaphoreType.DMA((n,)), pltpu.SemaphoreType.DMA((n,)),
                   *[pltpu.HBM(t.shape, t.dtype) for t in passed], jax.ShapeDtypeStruct((SUBLANE, LANE), F32)),
        in_specs=[HBM_SPEC] * (2 * n + nc) + [ANY_SPEC],
        out_specs=(SEM_SPEC, SEM_SPEC, *([HBM_SPEC] * (2 * n + nc)), pl.BlockSpec(memory_space=pltpu.VMEM)),
        input_output_aliases={i: 2 + i for i in range(2 * n + nc)},
        compiler_params=pltpu.CompilerParams(has_side_effects=DATAFLOW),
    )(*[pltpu.with_memory_space_constraint(t, pltpu.HBM) for t in passed], after)
    handle = (res[0], res[1], list(res[2:2 + n]), list(res[2 + n:2 + 2 * n]), tuple(kinds), gather)
    return handle, res[-1], list(res[2 + 2 * n:2 + 2 * n + nc])


def _exchange_wait(handle, name, after):
    send_sems, recv_sems, ins_thru, lands_thru, kinds, gather = handle
    n = len(ins_thru)
    layout = _exchange_layout(ins_thru, kinds, gather)

    def body(*refs):
        ins, outs = refs[:n], refs[n:2 * n]
        s_sems, r_sems = refs[2 * n], refs[2 * n + 1]
        _, remote = _exchange_copies(ins, outs, s_sems, r_sems, lambda a, k: a, layout, kinds, gather)
        for a in range(n):
            for k in range(1, N_DEV):
                arrival = remote(a, k, "theirs")
                arrival.wait_send()
                arrival.wait_recv()

    res = pl.pallas_call(
        body, name=name,
        out_shape=[pltpu.HBM(t.shape, t.dtype) for t in ins_thru + lands_thru],
        in_specs=[HBM_SPEC] * (2 * n) + [SEM_SPEC, SEM_SPEC, ANY_SPEC],
        out_specs=[HBM_SPEC] * (2 * n),
        input_output_aliases={i: i for i in range(2 * n)},
        compiler_params=pltpu.CompilerParams(has_side_effects=DATAFLOW),
    )(*ins_thru, *lands_thru, send_sems, recv_sems, after)
    return list(res[n:2 * n])


STAGE1_KS = (1, 2, 4, 6)
FORWARD_KS = (2, 4, 6)


def _gather2_copies(lands, shard_shapes, kinds):
    x, y, c, me = _my_place()

    def slab(a, p):
        return _slab(lands[a], shard_shapes[a], kinds[a], p)

    def stage1(a, k, sems, arriving):
        peer, pidx = _peer(x, y, c, k)
        s = slab(a, pidx if arriving else me)
        return pltpu.make_async_remote_copy(src_ref=s, dst_ref=s, send_sem=sems[0].at[a], recv_sem=sems[1].at[a],
                                            device_id=peer, device_id_type=MESH)

    def stage2(a, k, sems, arriving):
        sib, _ = _peer(x, y, c, 1)
        _, mine = _peer(x, y, c, k)
        _, theirs = _peer(x, y, 1 - c, k)
        s = slab(a, theirs if arriving else mine)
        return pltpu.make_async_remote_copy(src_ref=s, dst_ref=s, send_sem=sems[0].at[a], recv_sem=sems[1].at[a],
                                            device_id=sib, device_id_type=MESH)

    return stage1, stage2


def _gather2_call(lands, sems_in, name, after, make_body, returns_sems):
    n = len(lands)
    n_in = len(sems_in)

    def body(*refs):
        land_refs = refs[:n]
        in_sems = refs[n:n + n_in]
        rest = refs[n + n_in + 1:]
        out_sems = rest[:2] if returns_sems else ()
        make_body(land_refs, in_sems, out_sems)
        if returns_sems:
            token = rest[2 + n]
            token[...] = jnp.zeros_like(token)

    sem_shapes = (pltpu.SemaphoreType.DMA((n,)), pltpu.SemaphoreType.DMA((n,))) if returns_sems else ()
    tok_shape = (jax.ShapeDtypeStruct((SUBLANE, LANE), F32),) if returns_sems else ()
    n_sem_out = len(sem_shapes)
    res = pl.pallas_call(
        body, name=name,
        out_shape=(*sem_shapes, *[pltpu.HBM(t.shape, t.dtype) for t in lands], *tok_shape),
        in_specs=[HBM_SPEC] * n + [SEM_SPEC] * n_in + [ANY_SPEC],
        out_specs=(*([SEM_SPEC] * n_sem_out), *([HBM_SPEC] * n),
                   *([pl.BlockSpec(memory_space=pltpu.VMEM)] if returns_sems else [])),
        input_output_aliases={i: n_sem_out + i for i in range(n)},
        compiler_params=pltpu.CompilerParams(has_side_effects=DATAFLOW),
    )(*[pltpu.with_memory_space_constraint(t, pltpu.HBM) for t in lands], *sems_in, after)
    sems_out = tuple(res[:n_sem_out])
    lands_thru = list(res[n_sem_out:n_sem_out + n])
    return sems_out, lands_thru, (res[-1] if returns_sems else None)


def _gather2_start(shards, kinds, name, after):
    n = len(shards)
    views = [jax.ShapeDtypeStruct(t.shape[1:], t.dtype) for t, _ in shards]
    shard_shapes, dst_kinds, fulls = _exchange_layout(views, kinds, True)
    layout = (shard_shapes, dst_kinds, [jax.ShapeDtypeStruct(f.shape, BF16) for f in fulls])
    lands = _landing_with_own_slab(shards, kinds, True, layout, "place_own", order_after=after)

    def make_body(land_refs, in_sems, out_sems):
        stage1, _ = _gather2_copies(land_refs, layout[0], kinds)
        for a in range(n):
            for k in STAGE1_KS:
                stage1(a, k, out_sems, False).start()

    sems, lands, token = _gather2_call(lands, (), name, after, make_body, True)
    return (sems, lands, layout[0], tuple(kinds)), token


def _gather2_forward(handle, name, after, carry=()):
    sems1, lands, shard_shapes, kinds = handle
    n = len(lands)

    def make_body(land_refs, in_sems, out_sems):
        stage1, stage2 = _gather2_copies(land_refs, shard_shapes, kinds)
        for a in range(n):
            for k in STAGE1_KS:
                arrival = stage1(a, k, in_sems, True)
                arrival.wait_send()
                arrival.wait_recv()
        for a in range(n):
            for k in FORWARD_KS:
                stage2(a, k, out_sems, False).start()

    sems2, passed, token = _gather2_call(list(lands) + list(carry), sems1, name, after, make_body, True)
    return (sems2, passed[:n], shard_shapes, kinds), token, passed[n:]


def _gather2_wait(handle, name, after):
    sems2, lands, shard_shapes, kinds = handle
    n = len(lands)

    def make_body(land_refs, in_sems, out_sems):
        _, stage2 = _gather2_copies(land_refs, shard_shapes, kinds)
        for a in range(n):
            for k in FORWARD_KS:
                arrival = stage2(a, k, in_sems, True)
                arrival.wait_send()
                arrival.wait_recv()

    _, lands, _ = _gather2_call(lands, sems2, name, after, make_body, False)
    return lands


N_CHIP = N_DEV // 2


def _scatter2_pair(g, name):
    r, c8 = g.shape
    cd = c8 // N_DEV

    def body(g_ref, o_ref, send_sems, recv_sems):
        x, y, c, _ = _my_place()
        sib, _ = _peer(x, y, c, 1)

        def copy(ch):
            theirs = 2 * ch + (1 - c)
            return pltpu.make_async_remote_copy(
                src_ref=_slab(g_ref, (r, cd), "col", theirs), dst_ref=_slab(o_ref, (r, cd), "row", ch),
                send_sem=send_sems.at[ch], recv_sem=recv_sems.at[ch], device_id=sib, device_id_type=MESH)

        for ch in range(N_CHIP):
            copy(ch).start()
        for ch in range(N_CHIP):
            copy(ch).wait_send()
            copy(ch).wait_recv()

    return _pcall(
        body, comm=True, name=name,
        in_specs=[ANY_SPEC], out_specs=ANY_SPEC, out_shape=jax.ShapeDtypeStruct((N_CHIP * r, cd), g.dtype),
        scratch_shapes=[pltpu.SemaphoreType.DMA((N_CHIP,)), pltpu.SemaphoreType.DMA((N_CHIP,))],
    )(g)


def _scatter2_add(g, from_sibling, name):
    r, c8 = g.shape
    cd = c8 // N_DEV
    tr = _pick(r, 512)
    nr = r // tr
    _, _, core, _ = _my_place()

    def body(c_ref, g_ref, s_ref, o_ref):
        o_ref[...] = (g_ref[...].astype(F32) + s_ref[...].astype(F32)).astype(o_ref.dtype)

    return _pcall(
        body, name=name,
        grid_spec=pltpu.PrefetchScalarGridSpec(
            num_scalar_prefetch=1, grid=(N_CHIP, nr),
            in_specs=[pl.BlockSpec((tr, cd), lambda ch, i, c_ref: (i, 2 * ch + c_ref[0])),
                      pl.BlockSpec((tr, cd), lambda ch, i, c_ref: (ch * nr + i, 0))],
            out_specs=pl.BlockSpec((tr, cd), lambda ch, i, c_ref: (ch * nr + i, 0))),
        out_shape=jax.ShapeDtypeStruct((N_CHIP * r, cd), g.dtype),
        compiler_params=_params(("arbitrary", "arbitrary")),
    )(jnp.reshape(core, (1,)).astype(jnp.int32), g, from_sibling)


def _scatter2_copies(refs, r):
    part_ref, recv_ref = refs
    x, y, c, _ = _my_place()
    my_chip = 2 * x + y

    def rows(ref, ch):
        return ref.at[pl.ds(pl.multiple_of(ch * r, SUBLANE), r), :]

    def copy(k, sems, arriving):
        peer, _ = _peer(x, y, c, k)
        peer_chip = 2 * peer[0] + peer[1]
        return pltpu.make_async_remote_copy(
            src_ref=rows(part_ref, peer_chip), dst_ref=rows(recv_ref, peer_chip if arriving else my_chip),
            send_sem=sems[0].at[0], recv_sem=sems[1].at[0], device_id=peer, device_id_type=MESH)

    return copy


def _scatter2_start(g, name, after, carry=()):
    r = g.shape[0]
    x, y, _, _ = _my_place()
    from_sibling = _scatter2_pair(g, name + "_pair")
    partial = _scatter2_add(g, from_sibling, name + "_add")
    cd = partial.shape[1]
    recv = _place_own(partial, "row", (r, cd), "row", jax.ShapeDtypeStruct(partial.shape, partial.dtype),
                      "place_own", index=2 * x + y)

    def make_body(land_refs, in_sems, out_sems):
        copy = _scatter2_copies(land_refs[:2], r)
        for k in FORWARD_KS:
            copy(k, out_sems, False).start()

    sems, lands, token = _gather2_call([partial, recv] + list(carry), (), name, after, make_body, True)
    return (sems, lands[:2], r), token, lands[2:]


def _scatter2_wait(handle, name, after):
    sems, lands, r = handle

    def make_body(land_refs, in_sems, out_sems):
        copy = _scatter2_copies(land_refs, r)
        for k in FORWARD_KS:
            arrival = copy(k, in_sems, True)
            arrival.wait_send()
            arrival.wait_recv()

    _, lands, _ = _gather2_call(lands, sems, name, after, make_body, False)
    return lands[1]


def _pad_mla_w_in(w):
    d = w.shape[0]
    z = lambda n: jnp.zeros((d, n), w.dtype)
    o1, o2, o3 = Q_RANK, Q_RANK + KV_RANK, Q_RANK + KV_RANK + ROPE
    return jnp.concatenate([w[:, :o1], z(Q_RANK_PAD - Q_RANK), w[:, o1:o2], w[:, o2:o3], z(LANE - ROPE), w[:, o3:]], axis=1)


def _unpad_mla_w_in(g):
    return jnp.concatenate([g[:, :Q_RANK], g[:, PROJ_CKV:PROJ_KR], g[:, PROJ_KR:PROJ_KR + ROPE], g[:, PROJ_Z:]], axis=1)


def _pad_w_uq(w):
    w3 = w.reshape(Q_RANK, HEADS, NOPE + ROPE)
    w3 = jnp.pad(w3, ((0, Q_RANK_PAD - Q_RANK), (0, 0), (0, HEAD_PAD - NOPE - ROPE)))
    return w3.reshape(Q_RANK_PAD, HEADS * HEAD_PAD)


def _unpad_w_uq(g):
    return g[:Q_RANK].reshape(Q_RANK, HEADS, HEAD_PAD)[:, :, :NOPE + ROPE].reshape(Q_RANK, HEADS * (NOPE + ROPE))


def _pack(pieces):
    flat = [p.reshape(-1).astype(F32) for p in pieces]
    tot = sum(f.shape[0] for f in flat)
    unit = SUBLANE * LANE
    padn = (-tot) % unit
    if padn:
        flat.append(jnp.zeros((padn,), F32))
    return jnp.concatenate(flat).reshape(-1, LANE)


def _unpack(packed, shapes, lead=()):
    flat = packed.reshape(tuple(lead) + (-1,))
    out, off = [], 0
    for shp in shapes:
        nel = int(np.prod(shp))
        out.append(flat[..., off:off + nel].reshape(tuple(lead) + tuple(shp)))
        off += nel
    return out


SMALL_GROUPS = {
    "a": [('ada_b', lambda t: t[1:]), ('pre_g', lambda t: t[1:]), ('post_g', lambda t: t[1:]),
          ('sgu_norm_g', lambda t: t[1:]), ('sgu_w_s', lambda t: t[1:]), ('sgu_b_s', lambda t: t[1:]),
          ('mla_q_norm_g', lambda t: t), ('mla_kv_norm_g', lambda t: t)],
    "b": [('ada_b', lambda t: t[0:1, 2 * t.shape[1] // 3:]), ('post_g', lambda t: t[0:1]),
          ('sgu_norm_g', lambda t: t[0:1]), ('sgu_w_s', lambda t: t[0:1]), ('sgu_b_s', lambda t: t[0:1])],
    "c": [('ada_b', lambda t: t[0:1, :2 * t.shape[1] // 3]), ('pre_g', lambda t: t[0:1])],
}


WEIGHTS = ['ada_w', 'ada_b', 'pre_g', 'post_g', 'sgu_w_in', 'sgu_norm_g', 'sgu_w_s', 'sgu_b_s', 'sgu_w_out',
           'mla_w_in', 'mla_q_norm_g', 'mla_kv_norm_g', 'mla_w_uq', 'mla_w_ukv', 'mla_w_out']
INPUTS = ['x', 'c'] + WEIGHTS + ['loss_target'] + ['m_' + n for n in WEIGHTS] + ['v_' + n for n in WEIGHTS]


def kernel(x, c, ada_w, ada_b, pre_g, post_g, sgu_w_in, sgu_norm_g, sgu_w_s, sgu_b_s, sgu_w_out, mla_w_in, mla_q_norm_g, mla_kv_norm_g, mla_w_uq, mla_w_ukv, mla_w_out, loss_target, m_ada_w, m_ada_b, m_pre_g, m_post_g, m_sgu_w_in, m_sgu_norm_g, m_sgu_w_s, m_sgu_b_s, m_sgu_w_out, m_mla_w_in, m_mla_q_norm_g, m_mla_kv_norm_g, m_mla_w_uq, m_mla_w_ukv, m_mla_w_out, v_ada_w, v_ada_b, v_pre_g, v_post_g, v_sgu_w_in, v_sgu_norm_g, v_sgu_w_s, v_sgu_b_s, v_sgu_w_out, v_mla_w_in, v_mla_q_norm_g, v_mla_kv_norm_g, v_mla_w_uq, v_mla_w_ukv, v_mla_w_out):
    given = locals()
    A = {name: given[name] for name in INPUTS}
    x0 = A['x'][0]
    tgt = A['loss_target'][0]
    s, d = x0.shape
    e = 2 * d
    ncol = 3 * d // N_DEV
    _, _, _, me = _my_place()
    ktabs = _rope_tables(s)

    gains = jnp.zeros((SUBLANE, LANE), F32)
    gains = gains.at[0:2, :Q_RANK // N_DEV].set(A['mla_q_norm_g'])
    gains = gains.at[2:4, :KV_RANK // N_DEV].set(A['mla_kv_norm_g'])
    c8 = jnp.broadcast_to(A['c'], (SUBLANE, d))
    cg, gg = _exchange([c8, gains], ["row", "row"], True, "ag_cond")
    cond_raw = cg.reshape(N_DEV, SUBLANE, d)[:, 0, :]
    gg = gg.reshape(N_DEV, SUBLANE, LANE)
    gq_full = jnp.transpose(gg[:, 0:2, :Q_RANK // N_DEV], (1, 0, 2)).reshape(N_MIX, Q_RANK)
    gkv_full = jnp.transpose(gg[:, 2:4, :KV_RANK // N_DEV], (1, 0, 2)).reshape(N_MIX, KV_RANK)
    gq_pad = jnp.pad(gq_full, ((0, 0), (0, Q_RANK_PAD - Q_RANK)))

    bias_my = lax.dynamic_slice_in_dim(A['ada_b'], me * ncol, ncol, axis=1)
    mod_part = _ada_mod(cond_raw, A['ada_w'], bias_my, "ada_mod")
    send = jnp.pad(jnp.transpose(mod_part, (1, 0, 2)), ((0, 0), (0, SUBLANE - DEPTH), (0, 0)))
    (rb,) = _exchange([send.reshape(N_DEV * SUBLANE, ncol)], ["row"], False, "a2a_mod")

    token = rb
    gathers = {}
    for i in range(DEPTH):
        j = i // N_MIX
        if i % N_MIX == 0:
            parts = [("in", [(A['sgu_w_in'], j)], ["col"]), ("out", [(A['sgu_w_out'], j)], ["row"])]
        else:
            parts = [("all", [(A['mla_w_in'], j), (A['mla_w_uq'], j), (A['mla_w_ukv'], j), (A['mla_w_out'], j)],
                      ["col", "col", "col", "row"])]
        for part, shards, kinds in parts:
            gathers[(i, part)], token = _gather2_start(shards, kinds, f"ag_start_{i}_{part}", token)

    def forward_gathers(i, carried):
        for key in [k for k in gathers if k[0] == i]:
            gathers[key], _, (carried,) = _gather2_forward(gathers[key], f"ag_forward_{key[0]}_{key[1]}", token,
                                                           carry=[carried])
        return carried

    mod = jnp.transpose(rb.reshape(N_DEV, SUBLANE, ncol)[:, :DEPTH, :], (1, 0, 2)).reshape(DEPTH, 3 * d) + token[0, 0]
    shift = [mod[i:i + 1, :d] for i in range(DEPTH)]
    scale = [mod[i:i + 1, d:2 * d] for i in range(DEPTH)]
    gate = [mod[i:i + 1, 2 * d:] for i in range(DEPTH)]

    saved = []
    x = x0
    for i in range(DEPTH):
        j = i // N_MIX
        h = _pre_fwd(x, A['pre_g'][i:i + 1], scale[i], shift[i], f"pre_fwd")
        if i == 0:
            gathers[(0, "in")], _, (h,) = _gather2_forward(gathers[(0, "in")], "ag_forward_0_in", token, carry=[h])
        if i % N_MIX == 0:
            (w_in,) = _gather2_wait(gathers[(i, "in")], f"ag_wait_{i}_in", h)
            uvz = _mm(h, w_in, "nn", F32, "sgu_in")
            if i == 0:
                gathers[(0, "out")], _, (uvz,) = _gather2_forward(gathers[(0, "out")], "ag_forward_0_out", token,
                                                                  carry=[uvz])
            bias_full = jnp.repeat(A['sgu_b_s'][j].T, e // SGU_GROUPS, axis=1)
            ng = A['sgu_norm_g'][j:j + 1]
            y = _sgu_mid_fwd(uvz, ng, A['sgu_w_s'][j], bias_full, "sgu_mid_fwd")
            y = forward_gathers(i + 1, y)
            (w_out,) = _gather2_wait(gathers[(i, "out")], f"ag_wait_{i}_out", y)
            out = _mm(y, w_out, "nn", F32, "sgu_out")
            saved.append(dict(x=x, h=h, uvz=uvz, y=y, out=out, w_in=w_in, w_out=w_out, bias=bias_full, ng=ng))
        else:
            w_in, w_uq, w_ukv, w_out = _gather2_wait(gathers[(i, "all")], f"ag_wait_{i}_all", h)
            w_in = _pad_mla_w_in(w_in)
            w_uq = _pad_w_uq(w_uq)
            gq, gkv = gq_pad[j:j + 1], gkv_full[j:j + 1]
            proj = _mm(h, w_in, "nn", F32, "mla_in")
            qn, kvn, kr = _mla_norm_fwd(proj, gq, gkv, ktabs, "mla_norm_fwd")
            q = _mm(qn, w_uq, "nn", F32, "mla_uq")
            kv = _mm(kvn, w_ukv, "nn", BF16, "mla_ukv")
            o, y, lse, q_cat = _attn_fwd(q, kv, kr, proj, ktabs, "attn_fwd")
            y = forward_gathers(i + 1, y)
            out = _mm(y, w_out, "nn", F32, "mla_out")
            saved.append(dict(x=x, h=h, proj=proj, qn=qn, kvn=kvn, kr=kr, q_cat=q_cat, kv=kv, o=o, y=y, lse=lse,
                              out=out, w_in=w_in, w_uq=w_uq, w_ukv=w_ukv, w_out=w_out, gq=gq, gkv=gkv))
        x = _post_fwd(x, out, gate[i], A['post_g'][i:i + 1], "post_fwd")

    dx, loss_row = _loss_head(x, tgt, "loss_head")
    loss = lax.psum(loss_row[0, 0], ("x", "y", "c"))

    d_shift, d_scale, d_gate = [None] * DEPTH, [None] * DEPTH, [None] * DEPTH
    d_pre, d_post = [None] * DEPTH, [None] * DEPTH
    scatters, scatters_out = [None] * DEPTH, [None] * DEPTH
    small, small_grads, small_handles = {}, {}, {}
    for i in reversed(range(DEPTH)):
        j = i // N_MIX
        sv = saved[i]
        dout, d_gate[i], d_post[i] = _post_bwd(dx, sv['out'], gate[i], A['post_g'][i:i + 1], "post_bwd")
        if i % N_MIX == 0:
            dy = _mm(dout, sv['w_out'], "nt", F32, "sgu_out_dx")
            g_w_out = _mm(sv['y'], dout, "tn", BF16, "sgu_out_dw")
            scatters_out[i], token, (dy,) = _exchange_start([g_w_out], ["row"], False, f"rs_out_start_{i}", token,
                                                            carry=[dy])
            duvz, dws, dbs, dng = _sgu_mid_bwd(sv['uvz'], dy, sv['ng'], A['sgu_w_s'][j], sv['bias'], "sgu_mid_bwd")
            small[('sgu', j)] = (dws, dbs.reshape(SGU_GROUPS, SGU_BLOCK), dng)
            g_w_in = _mm(sv['h'], duvz, "tn", BF16, "sgu_in_dw")
            if i == 0:
                scatters[i], token, (duvz,) = _scatter2_start(g_w_in, "rs2_start_0", token, carry=[duvz])
                small_grads["b"] = [d_gate[0], d_post[0], dng, dws[None], small[('sgu', 0)][1][None]]
                small_handles["b"], token, (duvz,) = _exchange_start(
                    [_pack(small_grads["b"])], ["row"], True, "ag_small_b_start", token, carry=[duvz])
            else:
                scatters[i], token, (duvz,) = _exchange_start([g_w_in], ["col"], False, f"rs_start_{i}", token,
                                                              carry=[duvz])
            dh = _mm(duvz, sv['w_in'], "nt", F32, "sgu_in_dx")
        else:
            dy = _mm(dout, sv['w_out'], "nt", F32, "mla_out_dx")
            g_w_out = _mm(sv['y'], dout, "tn", BF16, "mla_out_dw")
            scatters_out[i], token, (dy,) = _exchange_start([g_w_out], ["row"], False, f"rs_out_start_{i}", token,
                                                            carry=[dy])
            do, dproj = _mla_gate_bwd(dy, sv['o'], sv['proj'], "mla_gate_bwd")
            dq_b, dkv, dkr_heads = _attn_bwd(sv['q_cat'], sv['kv'], sv['kr'], do, sv['o'], sv['lse'], ktabs,
                                             "attn_bwd")
            dqn = _mm(dq_b, sv['w_uq'], "nt", F32, "mla_uq_dx")
            g_w_uq = _unpad_w_uq(_mm(sv['qn'], dq_b, "tn", BF16, "mla_uq_dw"))
            dkvn = _mm(dkv, sv['w_ukv'], "nt", F32, "mla_ukv_dx")
            g_w_ukv = _mm(sv['kvn'], dkv, "tn", BF16, "mla_ukv_dw")
            dproj, dgq, dgkv = _mla_norm_bwd(dqn, dkvn, dkr_heads, sv['proj'], sv['gq'], sv['gkv'], ktabs, dproj,
                                             "mla_norm_bwd")
            g_w_in = _unpad_mla_w_in(_mm(sv['h'], dproj, "tn", BF16, "mla_in_dw"))
            scatters[i], token, (dproj,) = _exchange_start(
                [g_w_in, g_w_uq, g_w_ukv], ["col", "col", "col"], False, f"rs_start_{i}", token, carry=[dproj])
            dh = _mm(dproj, sv['w_in'], "nt", F32, "mla_in_dx")
            small[('mla', j)] = (dgq[:, :Q_RANK], dgkv)
        dx, d_shift[i], d_scale[i], d_pre[i] = _pre_bwd(dh, sv['x'], dx, A['pre_g'][i:i + 1], scale[i], "pre_bwd")
        if i == 1:
            small_grads["a"] = [
                jnp.concatenate([jnp.concatenate([d_shift[l], d_scale[l], d_gate[l]], axis=1)
                                 for l in range(1, DEPTH)], axis=0),
                jnp.concatenate(d_pre[1:], axis=0), jnp.concatenate(d_post[1:], axis=0),
                small[('sgu', 1)][2], small[('sgu', 1)][0][None], small[('sgu', 1)][1][None],
                jnp.concatenate([small[('mla', jj)][0] for jj in range(N_MIX)], axis=0),
                jnp.concatenate([small[('mla', jj)][1] for jj in range(N_MIX)], axis=0)]
            small_handles["a"], token, (dx,) = _exchange_start(
                [_pack(small_grads["a"])], ["row"], True, "ag_small_a_start", token, carry=[dx])

    res = {}

    def big(name, recv0, recv1):
        res[name] = _adam_reduce(recv0, recv1, A[name], A['m_' + name], A['v_' + name], "adam_" + name)

    def small_full(name, prefix):
        t = A[prefix + name]
        if name in ('mla_q_norm_g', 'mla_kv_norm_g'):
            t = lax.dynamic_update_slice_in_dim(jnp.zeros((t.shape[0], t.shape[1] * N_DEV), F32), t,
                                                me * t.shape[1], axis=1)
        return t

    def finish(i, after):
        first = ([_scatter2_wait(scatters[i], "rs2_wait_0", after)] if i == 0
                 else _exchange_wait(scatters[i], f"rs_wait_{i}", after))
        return first + _exchange_wait(scatters_out[i], f"rs_out_wait_{i}", after)

    recv_mla = {j: finish(N_MIX * j + 1, dx) for j in reversed(range(N_MIX))}
    for idx_w, name in enumerate(['mla_w_in', 'mla_w_uq', 'mla_w_ukv', 'mla_w_out']):
        big(name, recv_mla[0][idx_w], recv_mla[1][idx_w])

    recv_sgu = {j: finish(N_MIX * j, res['mla_w_out'][0]) for j in reversed(range(N_MIX))}
    for idx_w, name in enumerate(['sgu_w_in', 'sgu_w_out']):
        big(name, recv_sgu[0][idx_w], recv_sgu[1][idx_w])

    small_grads["c"] = [jnp.concatenate([d_shift[0], d_scale[0]], axis=1), d_pre[0]]
    gathered = {tag: _exchange_wait(small_handles[tag], f"ag_small_{tag}_wait", res['sgu_w_in'][0])[0]
                for tag in ("a", "b")}
    (gathered["c"],) = _exchange([_pack(small_grads["c"])], ["row"], True, "ag_small_c", after=res['sgu_w_in'][0])

    upd, parts = {}, {}
    for tag, group in SMALL_GROUPS.items():
        shapes = [g.shape for g in small_grads[tag]]
        packs = [_pack([pick(small_full(name, prefix)) for name, pick in group]) for prefix in ('', 'm_', 'v_')]
        outs4 = _adam_small(gathered[tag].reshape(N_DEV, -1, LANE), *packs, "adam_small_" + tag)
        upd[tag] = [_unpack(t, shapes) for t in outs4]
        parts[tag] = _unpack(gathered[tag].reshape(N_DEV, -1), shapes, lead=(N_DEV,))
    for k_out in range(4):
        ab_rest, pre_rest, post_rest, ng_1, ws_1, bs_1, gq_all, gkv_all = upd["a"][k_out]
        ab_gate0, post_0, ng_0, ws_0, bs_0 = upd["b"][k_out]
        ab_0, pre_0 = upd["c"][k_out]
        for name, val in (
                ('ada_b', jnp.concatenate([jnp.concatenate([ab_0, ab_gate0], axis=1), ab_rest], axis=0)),
                ('pre_g', jnp.concatenate([pre_0, pre_rest], axis=0)),
                ('post_g', jnp.concatenate([post_0, post_rest], axis=0)),
                ('sgu_norm_g', jnp.concatenate([ng_0, ng_1], axis=0)),
                ('sgu_w_s', jnp.concatenate([ws_0, ws_1], axis=0)),
                ('sgu_b_s', jnp.concatenate([bs_0, bs_1], axis=0)),
                ('mla_q_norm_g', gq_all), ('mla_kv_norm_g', gkv_all)):
            if name in ('mla_q_norm_g', 'mla_kv_norm_g'):
                wdt = A[name].shape[1]
                val = lax.dynamic_slice_in_dim(val, me * wdt, wdt, axis=1)
            res.setdefault(name, [None] * 4)[k_out] = val

    dmod_all = jnp.concatenate([jnp.concatenate([parts["c"][0], parts["b"][0]], axis=2), parts["a"][0]], axis=1)
    dmod_cols = jnp.transpose(lax.dynamic_slice_in_dim(dmod_all, me * ncol, ncol, axis=2), (1, 0, 2))
    res['ada_w'] = _ada_bwd_adam(jnp.transpose(cond_raw), dmod_cols, A['ada_w'], A['m_ada_w'], A['v_ada_w'], "ada_bwd")

    outs = [loss, dx[None]]
    for k_out in range(4):
        outs += [res[n][k_out] for n in WEIGHTS]
    return tuple(outs)
```

```python
import functools
import math

import numpy as np
import jax
import jax.numpy as jnp
from jax import lax
from jax.experimental import pallas as pl
from jax.experimental.pallas import tpu as pltpu

F32 = jnp.float32
BF16 = jnp.bfloat16
MESH = pl.DeviceIdType.MESH

N_DEV = 8
DEPTH = 4
N_MIX = 2
NORM_EPS = 1e-6
CHUNK = 64
SGU_BLOCK = 128
SGU_GROUPS = 16
HEADS = 16
Q_RANK = 448
Q_RANK_PAD = 512
KV_RANK = 512
NOPE = 128
ROPE = 64
HALF = ROPE // 2
V_DIM = 128
HEAD_PAD = 256
ROPE_THETA = 10000.0
MLA_WIDTH = HEADS * V_DIM
LANE = 128
SUBLANE = 8
PROJ_CQ = 0
PROJ_CKV = Q_RANK_PAD
PROJ_KR = Q_RANK_PAD + KV_RANK
PROJ_Z = PROJ_KR + LANE
PROJ_W = PROJ_Z + MLA_WIDTH

ADAM_LR = 0.001
ADAM_B1 = 0.9
ADAM_B2 = 0.999
ADAM_EPS = 1e-08
ADAM_WD = 0.01
ADAM_STEP = 10

VMEM_LIMIT = 56 * 1024 * 1024
ATT_BLK = 512
ATT_SUB = 128
ATT_HEADS_PER_STEP = 2
ROW_BLK = 512
MM_TM, MM_TN, MM_TK = 1024, 1024, 2048
MM_TILE_BYTES = 40 * 1024 * 1024
SOFTMAX_SCALE = (NOPE + ROPE) ** -0.5
LOG2_E = 1.0 / math.log(2.0)
INV_SQRT2 = 1.0 / math.sqrt(2.0)
INV_SQRT_2PI = 1.0 / math.sqrt(2.0 * math.pi)


def _pcall(body, comm=False, **kw):
    return pl.pallas_call(body, **kw)


def _params(sem=None):
    return pltpu.CompilerParams(dimension_semantics=sem, vmem_limit_bytes=VMEM_LIMIT)


def _pick(dim, pref):
    if dim <= pref:
        return dim
    t = (pref // LANE) * LANE
    while t >= LANE:
        if dim % t == 0:
            return t
        t -= LANE
    return dim


def _gelu(x):
    return 0.5 * x * (1.0 + lax.erf(x * INV_SQRT2))


def _gelu_grad(x):
    return 0.5 * (1.0 + lax.erf(x * INV_SQRT2)) + x * jnp.exp(-0.5 * x * x) * INV_SQRT_2PI


def _sigmoid(x):
    return 1.0 / (1.0 + jnp.exp(-x))


def _dot_nt(a, b):
    return lax.dot_general(a, b, (((1,), (1,)), ((), ())), preferred_element_type=F32)


def _dot_tn(a, b):
    return lax.dot_general(a, b, (((0,), (0,)), ((), ())), preferred_element_type=F32)


def _mm(a, b, dims, out_dtype, name):
    if dims == "nn":
        (m, k), (k2, n) = a.shape, b.shape
    elif dims == "nt":
        (m, k), (n, k2) = a.shape, b.shape
    else:
        (k, m), (k2, n) = a.shape, b.shape
    assert k == k2, (a.shape, b.shape, dims)
    tm, tn = _pick(m, MM_TM), _pick(n, MM_TN)
    out_bytes = 2 * tm * tn * jnp.dtype(out_dtype).itemsize
    whole_k = 2 * (tm + tn) * k * a.dtype.itemsize + out_bytes <= MM_TILE_BYTES
    tk = k if whole_k else _pick(k, MM_TK)
    nk = k // tk

    def body(a_ref, b_ref, o_ref, *scratch):
        if dims == "nn":
            p = jnp.dot(a_ref[...], b_ref[...], preferred_element_type=F32)
        elif dims == "nt":
            p = _dot_nt(a_ref[...], b_ref[...])
        else:
            p = _dot_tn(a_ref[...], b_ref[...])
        if nk == 1:
            o_ref[...] = p.astype(o_ref.dtype)
            return
        acc_ref, = scratch
        kk = pl.program_id(2)

        @pl.when(kk == 0)
        def _():
            acc_ref[...] = p

        @pl.when(kk > 0)
        def _():
            acc_ref[...] += p

        @pl.when(kk == nk - 1)
        def _():
            o_ref[...] = acc_ref[...].astype(o_ref.dtype)

    if dims == "tn":
        a_spec = pl.BlockSpec((tk, tm), lambda i, j, kk: (kk, i))
    else:
        a_spec = pl.BlockSpec((tm, tk), lambda i, j, kk: (i, kk))
    if dims == "nt":
        b_spec = pl.BlockSpec((tn, tk), lambda i, j, kk: (j, kk))
    else:
        b_spec = pl.BlockSpec((tk, tn), lambda i, j, kk: (kk, j))
    return _pcall(
        body, name=name,
        grid=(m // tm, n // tn, nk),
        in_specs=[a_spec, b_spec],
        out_specs=pl.BlockSpec((tm, tn), lambda i, j, kk: (i, j)),
        out_shape=jax.ShapeDtypeStruct((m, n), out_dtype),
        scratch_shapes=[pltpu.VMEM((tm, tn), F32)] if nk > 1 else [],
        compiler_params=_params(("parallel", "parallel", "arbitrary")),
    )(a, b)


def _row_spec(ts, d):
    return pl.BlockSpec((ts, d), lambda i: (i, 0))


def _vec_spec(d):
    return pl.BlockSpec((1, d), lambda i: (0, 0))


def _pre_fwd(x, g, scale, shift, name):
    s, d = x.shape
    ts = _pick(s, ROW_BLK)

    def body(x_ref, g_ref, sc_ref, sh_ref, h_ref):
        xv = x_ref[...]
        r = lax.rsqrt(jnp.mean(xv * xv, axis=-1, keepdims=True) + NORM_EPS)
        h_ref[...] = ((xv * r * g_ref[...]) * (1.0 + sc_ref[...]) + sh_ref[...]).astype(BF16)

    return _pcall(
        body, name=name, grid=(s // ts,),
        in_specs=[_row_spec(ts, d), _vec_spec(d), _vec_spec(d), _vec_spec(d)],
        out_specs=_row_spec(ts, d),
        out_shape=jax.ShapeDtypeStruct((s, d), BF16),
        compiler_params=_params(("parallel",)),
    )(x, g, scale, shift)


def _post_fwd(x, out, gate, g, name):
    s, d = x.shape
    ts = _pick(s, ROW_BLK)

    def body(x_ref, o_ref, gate_ref, g_ref, y_ref):
        o = o_ref[...]
        r = lax.rsqrt(jnp.mean(o * o, axis=-1, keepdims=True) + NORM_EPS)
        y_ref[...] = x_ref[...] + gate_ref[...] * (o * r * g_ref[...])

    return _pcall(
        body, name=name, grid=(s // ts,),
        in_specs=[_row_spec(ts, d), _row_spec(ts, d), _vec_spec(d), _vec_spec(d)],
        out_specs=_row_spec(ts, d),
        out_shape=jax.ShapeDtypeStruct((s, d), F32),
        compiler_params=_params(("parallel",)),
    )(x, out, gate, g)


def _loss_head(xf, tgt, name):
    s, d = xf.shape
    ts = _pick(s, ROW_BLK)
    ns = s // ts

    def body(x_ref, t_ref, dx_ref, loss_ref, acc_ref):
        i = pl.program_id(0)

        @pl.when(i == 0)
        def _():
            acc_ref[...] = jnp.zeros_like(acc_ref)

        e = x_ref[...] - t_ref[...]
        dx_ref[...] = e * (1.0 / d)
        acc_ref[...] += jnp.sum(e * e, axis=0, keepdims=True)

        @pl.when(i == ns - 1)
        def _():
            tot = jnp.sum(acc_ref[...], axis=1, keepdims=True) * (0.5 / d)
            loss_ref[...] = jnp.broadcast_to(tot, loss_ref.shape)

    return _pcall(
        body, name=name, grid=(ns,),
        in_specs=[_row_spec(ts, d), _row_spec(ts, d)],
        out_specs=[_row_spec(ts, d), pl.BlockSpec((1, LANE), lambda i: (0, 0))],
        out_shape=[jax.ShapeDtypeStruct((s, d), F32), jax.ShapeDtypeStruct((1, LANE), F32)],
        scratch_shapes=[pltpu.VMEM((1, d), F32)],
        compiler_params=_params(("arbitrary",)),
    )(xf, tgt)


def _post_bwd(dxo, out, gate, g, name):
    s, d = dxo.shape
    ts = _pick(s, ROW_BLK)

    def body(dx_ref, o_ref, gate_ref, g_ref, do_ref, dgate_ref, dg_ref):
        i = pl.program_id(0)

        @pl.when(i == 0)
        def _():
            dgate_ref[...] = jnp.zeros_like(dgate_ref)
            dg_ref[...] = jnp.zeros_like(dg_ref)

        o = o_ref[...]
        dx = dx_ref[...]
        gv = g_ref[...]
        r = lax.rsqrt(jnp.mean(o * o, axis=-1, keepdims=True) + NORM_EPS)
        n = o * r
        dyn = dx * gate_ref[...]
        dgate_ref[...] += jnp.sum(dx * (n * gv), axis=0, keepdims=True)
        dg_ref[...] += jnp.sum(dyn * n, axis=0, keepdims=True)
        dn = dyn * gv
        do_ref[...] = (r * (dn - n * jnp.mean(dn * n, axis=-1, keepdims=True))).astype(BF16)

    return _pcall(
        body, name=name, grid=(s // ts,),
        in_specs=[_row_spec(ts, d), _row_spec(ts, d), _vec_spec(d), _vec_spec(d)],
        out_specs=[_row_spec(ts, d), _vec_spec(d), _vec_spec(d)],
        out_shape=[jax.ShapeDtypeStruct((s, d), BF16), jax.ShapeDtypeStruct((1, d), F32),
                   jax.ShapeDtypeStruct((1, d), F32)],
        compiler_params=_params(("arbitrary",)),
    )(dxo, out, gate, g)


def _pre_bwd(dh, x, dxo, g, scale, name):
    s, d = x.shape
    ts = _pick(s, ROW_BLK)

    def body(dh_ref, x_ref, dxo_ref, g_ref, sc_ref, dx_ref, dsh_ref, dsc_ref, dg_ref):
        i = pl.program_id(0)

        @pl.when(i == 0)
        def _():
            dsh_ref[...] = jnp.zeros_like(dsh_ref)
            dsc_ref[...] = jnp.zeros_like(dsc_ref)
            dg_ref[...] = jnp.zeros_like(dg_ref)

        xv = x_ref[...]
        dhv = dh_ref[...]
        gv = g_ref[...]
        one_sc = 1.0 + sc_ref[...]
        r = lax.rsqrt(jnp.mean(xv * xv, axis=-1, keepdims=True) + NORM_EPS)
        n = xv * r
        dsh_ref[...] += jnp.sum(dhv, axis=0, keepdims=True)
        dsc_ref[...] += jnp.sum(dhv * (n * gv), axis=0, keepdims=True)
        dng = dhv * one_sc
        dg_ref[...] += jnp.sum(dng * n, axis=0, keepdims=True)
        dn = dng * gv
        dx_ref[...] = dxo_ref[...] + r * (dn - n * jnp.mean(dn * n, axis=-1, keepdims=True))

    return _pcall(
        body, name=name, grid=(s // ts,),
        in_specs=[_row_spec(ts, d), _row_spec(ts, d), _row_spec(ts, d), _vec_spec(d), _vec_spec(d)],
        out_specs=[_row_spec(ts, d), _vec_spec(d), _vec_spec(d), _vec_spec(d)],
        out_shape=[jax.ShapeDtypeStruct((s, d), F32)] + [jax.ShapeDtypeStruct((1, d), F32)] * 3,
        compiler_params=_params(("arbitrary",)),
    )(dh, x, dxo, g, scale)


def _sgu_mask():
    t = lax.broadcasted_iota(jnp.int32, (SGU_BLOCK, SGU_BLOCK), 0) // CHUNK
    s = lax.broadcasted_iota(jnp.int32, (SGU_BLOCK, SGU_BLOCK), 1) // CHUNK
    return s <= t


def _sgu_norm(v_pre, g):
    e = v_pre.shape[-1]
    vg = _gelu(v_pre)
    mu = jnp.sum(vg, axis=-1, keepdims=True) * (1.0 / e)
    dlt = vg - mu
    var = jnp.sum(dlt * dlt, axis=-1, keepdims=True) * (1.0 / e)
    rstd = lax.rsqrt(var + NORM_EPS)
    vhat = dlt * rstd
    return vhat, rstd, (vhat * g).astype(BF16)


def _sgu_mid_fwd(uvz, norm_g, w_s, bias_full, name):
    s, e3 = uvz.shape
    e = e3 // 3
    gd = e // SGU_GROUPS
    nb = s // SGU_BLOCK

    def body(uvz_ref, g_ref, w_ref, b_ref, y_ref, wsc):
        @pl.when(pl.program_id(0) == 0)
        def _():
            msk = _sgu_mask()
            for gi in range(SGU_GROUPS):
                wsc[gi] = jnp.where(msk, w_ref[gi], 0.0).astype(BF16)

        _, _, vb = _sgu_norm(uvz_ref[:, e:2 * e], g_ref[...])
        for gi in range(SGU_GROUPS):
            lo = gi * gd
            vm = jnp.dot(wsc[gi], vb[:, lo:lo + gd], preferred_element_type=F32) + b_ref[:, lo:lo + gd]
            zg = uvz_ref[:, 2 * e + lo:2 * e + lo + gd]
            y_ref[:, lo:lo + gd] = (_gelu(uvz_ref[:, lo:lo + gd]) * vm * (zg * _sigmoid(zg))).astype(BF16)

    return _pcall(
        body, name=name, grid=(nb,),
        in_specs=[pl.BlockSpec((SGU_BLOCK, e3), lambda n: (n, 0)),
                  pl.BlockSpec((1, e), lambda n: (0, 0)),
                  pl.BlockSpec((SGU_GROUPS, SGU_BLOCK, SGU_BLOCK), lambda n: (0, 0, 0)),
                  pl.BlockSpec((SGU_BLOCK, e), lambda n: (0, 0))],
        out_specs=pl.BlockSpec((SGU_BLOCK, e), lambda n: (n, 0)),
        out_shape=jax.ShapeDtypeStruct((s, e), BF16),
        scratch_shapes=[pltpu.VMEM((SGU_GROUPS, SGU_BLOCK, SGU_BLOCK), BF16)],
        compiler_params=_params(("arbitrary",)),
    )(uvz, norm_g, w_s, bias_full)


def _sgu_mid_bwd(uvz, dy, norm_g, w_s, bias_full, name):
    s, e3 = uvz.shape
    e = e3 // 3
    gd = e // SGU_GROUPS
    nb = s // SGU_BLOCK

    def body(uvz_ref, dy_ref, g_ref, w_ref, b_ref, d_ref, dw_ref, db_ref, dg_ref, wsc, wtsc, dvh_sc, dbacc):
        n = pl.program_id(0)

        @pl.when(n == 0)
        def _():
            msk = _sgu_mask()
            for gi in range(SGU_GROUPS):
                wm = jnp.where(msk, w_ref[gi], 0.0)
                wsc[gi] = wm.astype(BF16)
                wtsc[gi] = wm.T.astype(BF16)
            dw_ref[...] = jnp.zeros_like(dw_ref)
            dg_ref[...] = jnp.zeros_like(dg_ref)
            dbacc[...] = jnp.zeros_like(dbacc)

        v_pre = uvz_ref[:, e:2 * e]
        gv = g_ref[...]
        vhat, rstd, vb = _sgu_norm(v_pre, gv)
        s1 = jnp.zeros((SGU_BLOCK, 1), F32)
        s2 = jnp.zeros((SGU_BLOCK, 1), F32)
        for gi in range(SGU_GROUPS):
            lo = gi * gd
            u_pre = uvz_ref[:, lo:lo + gd]
            zg = uvz_ref[:, 2 * e + lo:2 * e + lo + gd]
            dyg = dy_ref[:, lo:lo + gd]
            ug = _gelu(u_pre)
            sig = _sigmoid(zg)
            vbg = vb[:, lo:lo + gd]
            vhg = vhat[:, lo:lo + gd]
            vm = jnp.dot(wsc[gi], vbg, preferred_element_type=F32) + b_ref[:, lo:lo + gd]
            t = dyg * (zg * sig)
            d_ref[:, lo:lo + gd] = (t * vm * _gelu_grad(u_pre)).astype(BF16)
            dvm = t * ug
            d_ref[:, 2 * e + lo:2 * e + lo + gd] = (dyg * ug * vm * (sig * (1.0 + zg * (1.0 - sig)))).astype(BF16)
            dvm_b = dvm.astype(BF16)
            dv = jnp.dot(wtsc[gi], dvm_b, preferred_element_type=F32)
            dw_ref[gi] += _dot_nt(dvm_b, vbg)
            dbacc[:, lo:lo + gd] += dvm
            dg_ref[:, lo:lo + gd] += jnp.sum(dv * vhg, axis=0, keepdims=True)
            dvh = dv * gv[:, lo:lo + gd]
            dvh_sc[:, lo:lo + gd] = dvh
            s1 = s1 + jnp.sum(dvh, axis=-1, keepdims=True)
            s2 = s2 + jnp.sum(dvh * vhg, axis=-1, keepdims=True)
        dvg = rstd * (dvh_sc[...] - s1 * (1.0 / e) - vhat * (s2 * (1.0 / e)))
        d_ref[:, e:2 * e] = (dvg * _gelu_grad(v_pre)).astype(BF16)

        @pl.when(n == nb - 1)
        def _():
            msk = _sgu_mask()
            for gi in range(SGU_GROUPS):
                dw_ref[gi] = jnp.where(msk, dw_ref[gi], 0.0)
                db_ref[gi] = jnp.sum(dbacc[:, gi * gd:(gi + 1) * gd], axis=1, keepdims=True)

    return _pcall(
        body, name=name, grid=(nb,),
        in_specs=[pl.BlockSpec((SGU_BLOCK, e3), lambda n: (n, 0)),
                  pl.BlockSpec((SGU_BLOCK, e), lambda n: (n, 0)),
                  pl.BlockSpec((1, e), lambda n: (0, 0)),
                  pl.BlockSpec((SGU_GROUPS, SGU_BLOCK, SGU_BLOCK), lambda n: (0, 0, 0)),
                  pl.BlockSpec((SGU_BLOCK, e), lambda n: (0, 0))],
        out_specs=[pl.BlockSpec((SGU_BLOCK, e3), lambda n: (n, 0)),
                   pl.BlockSpec((SGU_GROUPS, SGU_BLOCK, SGU_BLOCK), lambda n: (0, 0, 0)),
                   pl.BlockSpec((SGU_GROUPS, SGU_BLOCK, 1), lambda n: (0, 0, 0)),
                   pl.BlockSpec((1, e), lambda n: (0, 0))],
        out_shape=[jax.ShapeDtypeStruct((s, e3), BF16),
                   jax.ShapeDtypeStruct((SGU_GROUPS, SGU_BLOCK, SGU_BLOCK), F32),
                   jax.ShapeDtypeStruct((SGU_GROUPS, SGU_BLOCK, 1), F32),
                   jax.ShapeDtypeStruct((1, e), F32)],
        scratch_shapes=[pltpu.VMEM((SGU_GROUPS, SGU_BLOCK, SGU_BLOCK), BF16),
                        pltpu.VMEM((SGU_GROUPS, SGU_BLOCK, SGU_BLOCK), BF16),
                        pltpu.VMEM((SGU_BLOCK, e), F32),
                        pltpu.VMEM((SGU_BLOCK, e), F32)],
        compiler_params=_params(("arbitrary",)),
    )(uvz, dy, norm_g, w_s, bias_full)


def _rope_tables(s):
    pos = jnp.arange(s, dtype=F32)
    inv_freq = ROPE_THETA ** (-jnp.arange(0, ROPE, 2, dtype=F32) / ROPE)
    ang = pos[:, None] * inv_freq[None, :]
    cos, sin = jnp.cos(ang), jnp.sin(ang)
    z32 = jnp.zeros((s, HALF), F32)
    z64 = jnp.zeros((s, ROPE), F32)
    ck = jnp.concatenate([cos, cos, z64], axis=1)
    s1k = jnp.concatenate([-sin, z32, z64], axis=1)
    s2k = jnp.concatenate([z32, sin, z64], axis=1)
    return ck, s1k, s2k


def _rot(x, c, s1, s2):
    w = x.shape[-1]
    return x * c + pltpu.roll(x, w - HALF, 1) * s1 + pltpu.roll(x, HALF, 1) * s2


def _rms(cv, n_real):
    r = lax.rsqrt(jnp.sum(cv * cv, axis=-1, keepdims=True) * (1.0 / n_real) + NORM_EPS)
    return r, cv * r


def _mla_norm_fwd(proj, gq, gkv, tabs, name):
    s = proj.shape[0]
    ts = _pick(s, ROW_BLK)
    ck, s1k, s2k = tabs

    def body(cq_ref, ckv_ref, kr_ref, gq_ref, gkv_ref, c_ref, s1_ref, s2_ref, qn_ref, kvn_ref, kro_ref):
        _, nq = _rms(cq_ref[...], Q_RANK)
        qn_ref[...] = (nq * gq_ref[...]).astype(BF16)
        _, nkv = _rms(ckv_ref[...], KV_RANK)
        kvn_ref[...] = (nkv * gkv_ref[...]).astype(BF16)
        kro_ref[...] = _rot(kr_ref[...], c_ref[...], s1_ref[...], s2_ref[...]).astype(BF16)

    tab = pl.BlockSpec((ts, LANE), lambda i: (i, 0))
    return _pcall(
        body, name=name, grid=(s // ts,),
        in_specs=[pl.BlockSpec((ts, Q_RANK_PAD), lambda i: (i, 0)),
                  pl.BlockSpec((ts, KV_RANK), lambda i: (i, PROJ_CKV // KV_RANK)),
                  pl.BlockSpec((ts, LANE), lambda i: (i, PROJ_KR // LANE)),
                  _vec_spec(Q_RANK_PAD), _vec_spec(KV_RANK), tab, tab, tab],
        out_specs=[pl.BlockSpec((ts, Q_RANK_PAD), lambda i: (i, 0)),
                   pl.BlockSpec((ts, KV_RANK), lambda i: (i, 0)), tab],
        out_shape=[jax.ShapeDtypeStruct((s, Q_RANK_PAD), BF16), jax.ShapeDtypeStruct((s, KV_RANK), BF16),
                   jax.ShapeDtypeStruct((s, LANE), BF16)],
        compiler_params=_params(("parallel",)),
    )(proj, proj, proj, gq, gkv, ck, s1k, s2k)


def _transpose_bf16(t):
    return t.astype(F32).T.astype(BF16)


def _diag_mask(tb, transposed):
    r = lax.broadcasted_iota(jnp.int32, (tb, tb), 0) // CHUNK
    c = lax.broadcasted_iota(jnp.int32, (tb, tb), 1) // CHUNK
    return (r <= c) if transposed else (c <= r)


def _attn_fwd(q, kv, kr, proj, tabs, name):
    s = q.shape[0]
    tb = _pick(s, ATT_BLK)
    nb = s // tb
    zcol = PROJ_Z // V_DIM
    mult = SOFTMAX_SCALE * LOG2_E

    hp = ATT_HEADS_PER_STEP

    def body(qf_ref, c_ref, s1_ref, s2_ref, kv_ref, kr_ref, *rest):
        z_refs = rest[:hp]
        o_ref, y_ref, lse_ref, q_ref = rest[hp:hp + 4]
        scratch = rest[hp + 4:]
        kt_sc, vx_sc, m_sc, acc_sc, sa_sc, sb_sc = (scratch[i * hp:(i + 1) * hp] for i in range(6))
        qi = pl.program_id(1)
        heads = range(hp)
        for hh in heads:
            lo = hh * HEAD_PAD
            q_ref[:, lo:lo + NOPE] = (qf_ref[:, lo:lo + NOPE] * mult).astype(BF16)
            q_ref[:, lo + NOPE:lo + HEAD_PAD] = (
                _rot(qf_ref[:, lo + NOPE:lo + HEAD_PAD], c_ref[...], s1_ref[...], s2_ref[...]) * mult).astype(BF16)

        @pl.when(qi == 0)
        def _():
            for hh in heads:
                kcol = hh * (NOPE + V_DIM)
                for b in range(nb):
                    rows = slice(b * tb, (b + 1) * tb)
                    kt_sc[hh][b] = _transpose_bf16(
                        jnp.concatenate([kv_ref[rows, kcol:kcol + NOPE], kr_ref[rows, :]], axis=1))
                    vx_sc[hh][b] = jnp.concatenate(
                        [kv_ref[rows, kcol + NOPE:kcol + NOPE + V_DIM], jnp.ones((tb, V_DIM), BF16)], axis=1)

        for hh in heads:
            m_sc[hh][...] = jnp.full_like(m_sc[hh], -1e30)
            acc_sc[hh][...] = jnp.zeros_like(acc_sc[hh])
        sub = min(tb, ATT_SUB)

        def scores(ki, bufs):
            for hh in heads:
                bufs[hh][...] = jnp.dot(q_ref[:, hh * HEAD_PAD:(hh + 1) * HEAD_PAD], kt_sc[hh][ki],
                                        preferred_element_type=F32)

        def step(ki, bufs, masked):
            for r in range(tb // sub):
                rs = slice(r * sub, (r + 1) * sub)
                for hh in heads:
                    sc = bufs[hh][rs, :]
                    if masked:
                        sc = jnp.where(_diag_mask(tb, False)[rs, :], sc, -1e30)
                    m_prev = m_sc[hh][rs, :]
                    m_new = jnp.maximum(m_prev, jnp.max(sc, axis=-1, keepdims=True))
                    p = jnp.exp2(sc - m_new).astype(BF16)
                    acc_sc[hh][rs, :] = (jnp.exp2(m_prev - m_new) * acc_sc[hh][rs, :]
                                         + jnp.dot(p, vx_sc[hh][ki], preferred_element_type=F32))
                    m_sc[hh][rs, :] = m_new

        def pair(t, carry):
            scores(2 * t + 1, sb_sc)
            step(2 * t, sa_sc, False)
            scores(2 * t + 2, sa_sc)
            step(2 * t + 1, sb_sc, False)
            return carry

        scores(0, sa_sc)
        lax.fori_loop(0, qi // 2, pair, 0)

        @pl.when(qi % 2 == 1)
        def _():
            scores(qi, sb_sc)
            step(qi - 1, sa_sc, False)
            step(qi, sb_sc, True)

        @pl.when(qi % 2 == 0)
        def _():
            step(qi, sa_sc, True)

        for hh in heads:
            l = acc_sc[hh][:, V_DIM:V_DIM + 1]
            o = acc_sc[hh][:, :V_DIM] / l
            z = z_refs[hh][...]
            o_ref[:, hh * V_DIM:(hh + 1) * V_DIM] = o.astype(BF16)
            y_ref[:, hh * V_DIM:(hh + 1) * V_DIM] = (o * (z * _sigmoid(z))).astype(BF16)
            lse_cols = jnp.broadcast_to(m_sc[hh][...] + jnp.log2(l), (tb, LANE))
            lse_ref[hh] = lse_cols.T[0:1, :]

    oblk = pl.BlockSpec((tb, hp * V_DIM), lambda g, qi: (qi, g))
    qblk = pl.BlockSpec((tb, hp * HEAD_PAD), lambda g, qi: (qi, g))
    tab = pl.BlockSpec((tb, LANE), lambda g, qi: (qi, 0))
    per_head = lambda shape, dtype: [pltpu.VMEM(shape, dtype)] * hp
    return _pcall(
        body, name=name, grid=(HEADS // hp, nb),
        in_specs=[qblk, tab, tab, tab,
                  pl.BlockSpec((s, hp * (NOPE + V_DIM)), lambda g, qi: (0, g)),
                  pl.BlockSpec((s, LANE), lambda g, qi: (0, 0))]
                 + [pl.BlockSpec((tb, V_DIM), functools.partial(lambda g, qi, hh: (qi, zcol + hp * g + hh), hh=hh))
                    for hh in range(hp)],
        out_specs=[oblk, oblk, pl.BlockSpec((hp, None, 1, tb), lambda g, qi: (g, qi, 0, 0)), qblk],
        out_shape=[jax.ShapeDtypeStruct((s, MLA_WIDTH), BF16), jax.ShapeDtypeStruct((s, MLA_WIDTH), BF16),
                   jax.ShapeDtypeStruct((HEADS, nb, 1, tb), F32), jax.ShapeDtypeStruct((s, HEADS * HEAD_PAD), BF16)],
        scratch_shapes=(per_head((nb, HEAD_PAD, tb), BF16) + per_head((nb, tb, HEAD_PAD), BF16)
                        + per_head((tb, 1), F32) + per_head((tb, HEAD_PAD), F32)
                        + per_head((tb, tb), F32) + per_head((tb, tb), F32)),
        compiler_params=_params(("parallel", "arbitrary")),
    )(q, *tabs, kv, kr, *([proj] * hp))


def _attn_bwd(q_cat, kv, kr, do, o, lse, tabs, name):
    s = q_cat.shape[0]
    tb = _pick(s, ATT_BLK)
    nb = s // tb
    ln2 = math.log(2.0)

    hp = ATT_HEADS_PER_STEP
    heads = range(hp)

    def body(q_ref, do_ref, o_ref, lse_ref, kv_ref, kr_ref, c_ref, s1_ref, s2_ref, dq_ref, dkv_ref, dkr_ref, *scratch):
        qt_sc, dot_sc, delta_sc, dqt_sc, dk_sc, dv_sc = (scratch[i * hp:(i + 1) * hp] for i in range(6))
        ki = pl.program_id(1)
        qcols = lambda hh: slice(hh * HEAD_PAD, (hh + 1) * HEAD_PAD)
        vcols = lambda hh: slice(hh * V_DIM, (hh + 1) * V_DIM)

        @pl.when(ki == 0)
        def _():
            for hh in heads:
                for b in range(nb):
                    rows = slice(b * tb, (b + 1) * tb)
                    qt_sc[hh][b] = _transpose_bf16(q_ref[rows, qcols(hh)])
                    do_t = do_ref[rows, vcols(hh)].astype(F32).T
                    dot_sc[hh][b] = do_t.astype(BF16)
                    delta_sc[hh][b] = jnp.sum(do_t * o_ref[rows, vcols(hh)].astype(F32).T, axis=0, keepdims=True)
                dqt_sc[hh][...] = jnp.zeros_like(dqt_sc[hh])

        k, kt, vb = [], [], []
        for hh in heads:
            kcol = hh * (NOPE + V_DIM)
            k.append(jnp.concatenate([kv_ref[:, kcol:kcol + NOPE], kr_ref[...]], axis=1))
            kt.append(_transpose_bf16(k[hh]))
            vb.append(kv_ref[:, kcol + NOPE:kcol + NOPE + V_DIM])
            dk_sc[hh][...] = jnp.zeros_like(dk_sc[hh])
            dv_sc[hh][...] = jnp.zeros_like(dv_sc[hh])

        def step(qi, masked):
            rows = pl.ds(pl.multiple_of(qi * tb, tb), tb)
            for hh in heads:
                sc_t = jnp.dot(k[hh], qt_sc[hh][qi], preferred_element_type=F32)
                if masked:
                    sc_t = jnp.where(_diag_mask(tb, True), sc_t, -1e30)
                p_t = jnp.exp2(sc_t - lse_ref[hh, qi])
                dp_t = jnp.dot(vb[hh], dot_sc[hh][qi], preferred_element_type=F32)
                ds_t = (p_t * (dp_t - delta_sc[hh][qi])).astype(BF16)
                dv_sc[hh][...] += jnp.dot(p_t.astype(BF16), do_ref[rows, vcols(hh)], preferred_element_type=F32)
                dk_sc[hh][...] += jnp.dot(ds_t, q_ref[rows, qcols(hh)], preferred_element_type=F32)
                dqt_sc[hh][qi] += jnp.dot(kt[hh], ds_t, preferred_element_type=F32)

        step(ki, True)

        def loop_body(qi, carry):
            step(qi, False)
            return carry

        lax.fori_loop(ki + 1, nb, loop_body, 0)

        for hh in heads:
            lo = hh * HEAD_PAD
            dkv_ref[:, lo:lo + NOPE] = (dk_sc[hh][:, :NOPE] * ln2).astype(BF16)
            dkv_ref[:, lo + NOPE:lo + HEAD_PAD] = dv_sc[hh][...].astype(BF16)
            dkr_ref[hh] = dk_sc[hh][:, NOPE:] * ln2

        @pl.when(ki == nb - 1)
        def _():
            for hh in heads:
                lo = hh * HEAD_PAD
                for b in range(nb):
                    rows = slice(b * tb, (b + 1) * tb)
                    dq = dqt_sc[hh][b].T
                    dq_ref[rows, lo:lo + NOPE] = (dq[:, :NOPE] * SOFTMAX_SCALE).astype(BF16)
                    dq_ref[rows, lo + NOPE:lo + HEAD_PAD] = (
                        _rot(dq[:, NOPE:], c_ref[rows, :], -s1_ref[rows, :], -s2_ref[rows, :])
                        * SOFTMAX_SCALE).astype(BF16)

    tab = pl.BlockSpec((s, LANE), lambda g, ki: (0, 0), pipeline_mode=pl.Buffered(1))
    per_head = lambda shape, dtype: [pltpu.VMEM(shape, dtype)] * hp
    return _pcall(
        body, name=name, grid=(HEADS // hp, nb),
        in_specs=[pl.BlockSpec((s, hp * HEAD_PAD), lambda g, ki: (0, g)),
                  pl.BlockSpec((s, hp * V_DIM), lambda g, ki: (0, g)),
                  pl.BlockSpec((s, hp * V_DIM), lambda g, ki: (0, g)),
                  pl.BlockSpec((hp, nb, 1, tb), lambda g, ki: (g, 0, 0, 0)),
                  pl.BlockSpec((tb, hp * (NOPE + V_DIM)), lambda g, ki: (ki, g)),
                  pl.BlockSpec((tb, LANE), lambda g, ki: (ki, 0)), tab, tab, tab],
        out_specs=[pl.BlockSpec((s, hp * HEAD_PAD), lambda g, ki: (0, g)),
                   pl.BlockSpec((tb, hp * HEAD_PAD), lambda g, ki: (ki, g)),
                   pl.BlockSpec((hp, tb, LANE), lambda g, ki: (g, ki, 0))],
        out_shape=[jax.ShapeDtypeStruct((s, HEADS * HEAD_PAD), BF16),
                   jax.ShapeDtypeStruct((s, HEADS * HEAD_PAD), BF16),
                   jax.ShapeDtypeStruct((HEADS, s, LANE), F32)],
        scratch_shapes=(per_head((nb, HEAD_PAD, tb), BF16) + per_head((nb, V_DIM, tb), BF16)
                        + per_head((nb, 1, tb), F32) + per_head((nb, HEAD_PAD, tb), F32)
                        + per_head((tb, HEAD_PAD), F32) + per_head((tb, V_DIM), F32)),
        compiler_params=_params(("parallel", "arbitrary")),
    )(q_cat, do, o, lse, kv, kr, *tabs)


def _mla_gate_bwd(dy, o, proj, name):
    s = dy.shape[0]
    ts = _pick(s, ROW_BLK)

    def body(dy_ref, o_ref, p_ref, do_ref, dp_ref):
        z = p_ref[:, PROJ_Z:]
        dyv = dy_ref[...]
        sig = _sigmoid(z)
        do_ref[...] = (dyv * (z * sig)).astype(BF16)
        dp_ref[:, :PROJ_Z] = jnp.zeros((ts, PROJ_Z), BF16)
        dp_ref[:, PROJ_Z:] = (dyv * o_ref[...].astype(F32) * (sig * (1.0 + z * (1.0 - sig)))).astype(BF16)

    blk = pl.BlockSpec((ts, MLA_WIDTH), lambda i: (i, 0))
    wide = pl.BlockSpec((ts, PROJ_W), lambda i: (i, 0))
    return _pcall(
        body, name=name, grid=(s // ts,),
        in_specs=[blk, blk, wide],
        out_specs=[blk, wide],
        out_shape=[jax.ShapeDtypeStruct((s, MLA_WIDTH), BF16), jax.ShapeDtypeStruct((s, PROJ_W), BF16)],
        compiler_params=_params(("parallel",)),
    )(dy, o, proj)


def _mla_norm_bwd(dqn, dkvn, dkr_heads, proj, gq, gkv, tabs, dproj, name):
    s = proj.shape[0]
    ts = _pick(s, ROW_BLK)
    ck, s1k, s2k = tabs

    def rms_bwd(cv, dn_in, g, n_real):
        r, n = _rms(cv, n_real)
        dg = jnp.sum(dn_in * n, axis=0, keepdims=True)
        dn = dn_in * g
        dc = r * (dn - n * (jnp.sum(dn * n, axis=-1, keepdims=True) * (1.0 / n_real)))
        return dc, dg

    def body(dqn_ref, dkvn_ref, dkr_ref, cq_ref, ckv_ref, gq_ref, gkv_ref, c_ref, s1_ref, s2_ref, dp_in_ref,
             dp_ref, dgq_ref, dgkv_ref):
        @pl.when(pl.program_id(0) == 0)
        def _():
            dgq_ref[...] = jnp.zeros_like(dgq_ref)
            dgkv_ref[...] = jnp.zeros_like(dgkv_ref)

        dcq, dgq = rms_bwd(cq_ref[...], dqn_ref[...], gq_ref[...], Q_RANK)
        dckv, dgkv = rms_bwd(ckv_ref[...], dkvn_ref[...], gkv_ref[...], KV_RANK)
        dgq_ref[...] += dgq
        dgkv_ref[...] += dgkv
        dkr = dkr_ref[0]
        for h in range(1, HEADS):
            dkr = dkr + dkr_ref[h]
        dp_ref[:, PROJ_CQ:PROJ_CKV] = dcq.astype(BF16)
        dp_ref[:, PROJ_CKV:PROJ_KR] = dckv.astype(BF16)
        dp_ref[:, PROJ_KR:PROJ_Z] = _rot(dkr, c_ref[...], -s1_ref[...], -s2_ref[...]).astype(BF16)

    tab = pl.BlockSpec((ts, LANE), lambda i: (i, 0))
    return _pcall(
        body, name=name, grid=(s // ts,),
        in_specs=[pl.BlockSpec((ts, Q_RANK_PAD), lambda i: (i, 0)),
                  pl.BlockSpec((ts, KV_RANK), lambda i: (i, 0)),
                  pl.BlockSpec((HEADS, ts, LANE), lambda i: (0, i, 0)),
                  pl.BlockSpec((ts, Q_RANK_PAD), lambda i: (i, 0)),
                  pl.BlockSpec((ts, KV_RANK), lambda i: (i, PROJ_CKV // KV_RANK)),
                  _vec_spec(Q_RANK_PAD), _vec_spec(KV_RANK), tab, tab, tab, pl.BlockSpec(memory_space=pl.ANY)],
        out_specs=[pl.BlockSpec((ts, PROJ_Z), lambda i: (i, 0)), _vec_spec(Q_RANK_PAD), _vec_spec(KV_RANK)],
        out_shape=[jax.ShapeDtypeStruct((s, PROJ_W), BF16), jax.ShapeDtypeStruct((1, Q_RANK_PAD), F32),
                   jax.ShapeDtypeStruct((1, KV_RANK), F32)],
        input_output_aliases={10: 0},
        compiler_params=_params(("arbitrary",)),
    )(dqn, dkvn, dkr_heads, proj, proj, gq, gkv, ck, s1k, s2k, dproj)


def _ada_mod(cond_raw, ada_w, bias_my, name):
    nl, d, ncol = ada_w.shape
    tk = _pick(d, 512)
    nk = d // tk

    def body(c_ref, w_ref, b_ref, o_ref, acc_ref):
        kk = pl.program_id(1)

        @pl.when(kk == 0)
        def _():
            acc_ref[...] = jnp.zeros_like(acc_ref)

        cv = c_ref[...]
        cond = (cv * _sigmoid(cv)).astype(BF16)
        acc_ref[...] += jnp.dot(cond, w_ref[...].astype(BF16), preferred_element_type=F32)

        @pl.when(kk == nk - 1)
        def _():
            o_ref[...] = acc_ref[...] + b_ref[...]

    return _pcall(
        body, name=name, grid=(nl, nk),
        in_specs=[pl.BlockSpec((N_DEV, tk), lambda l, kk: (0, kk)),
                  pl.BlockSpec((None, tk, ncol), lambda l, kk: (l, kk, 0)),
                  pl.BlockSpec((None, 1, ncol), lambda l, kk: (l, 0, 0))],
        out_specs=pl.BlockSpec((None, N_DEV, ncol), lambda l, kk: (l, 0, 0)),
        out_shape=jax.ShapeDtypeStruct((nl, N_DEV, ncol), F32),
        scratch_shapes=[pltpu.VMEM((N_DEV, ncol), F32)],
        compiler_params=_params(("parallel", "arbitrary")),
    )(cond_raw, ada_w, bias_my.reshape(nl, 1, ncol))


def _adam(w, g, m, v):
    m = ADAM_B1 * m + (1.0 - ADAM_B1) * g
    v = ADAM_B2 * v + (1.0 - ADAM_B2) * (g * g)
    m_hat = m / (1.0 - ADAM_B1 ** ADAM_STEP)
    v_hat = v / (1.0 - ADAM_B2 ** ADAM_STEP)
    delta = -ADAM_LR * (m_hat / (jnp.sqrt(v_hat) + ADAM_EPS) + ADAM_WD * w)
    return delta, m, v


def _ada_bwd_adam(cond_t, dmod_cols, w, m, v, name):
    nl, d, ncol = w.shape
    tk = _pick(d, 512)

    def body(c_ref, dm_ref, w_ref, m_ref, v_ref, g_ref, d_ref, mo_ref, vo_ref):
        cv = c_ref[...]
        cond = (cv * _sigmoid(cv)).astype(BF16)
        g = jnp.dot(cond, dm_ref[...].astype(BF16), preferred_element_type=F32)
        delta, m2, v2 = _adam(w_ref[...], g, m_ref[...], v_ref[...])
        g_ref[...] = g
        d_ref[...] = delta
        mo_ref[...] = m2
        vo_ref[...] = v2

    blk = pl.BlockSpec((None, tk, ncol), lambda l, kk: (l, kk, 0))
    shp = jax.ShapeDtypeStruct((nl, d, ncol), F32)
    return _pcall(
        body, name=name, grid=(nl, d // tk),
        in_specs=[pl.BlockSpec((tk, N_DEV), lambda l, kk: (kk, 0)),
                  pl.BlockSpec((None, N_DEV, ncol), lambda l, kk: (l, 0, 0)), blk, blk, blk],
        out_specs=[blk, blk, blk, blk], out_shape=[shp, shp, shp, shp],
        compiler_params=_params(("parallel", "parallel")),
    )(cond_t, dmod_cols, w, m, v)


def _adam_reduce(recv0, recv1, w, m, v, name):
    nl, r, c = w.shape
    tr = _pick(r, 128) if r % 128 == 0 else r
    tc = _pick(c, 1024)
    n0, n1 = recv0.shape[0] // r, recv1.shape[0] // r

    def body(r0_ref, r1_ref, w_ref, m_ref, v_ref, g_ref, d_ref, mo_ref, vo_ref):
        l = pl.program_id(0)

        def run(rr):
            g = rr[0].astype(F32)
            for sidx in range(1, rr.shape[0]):
                g = g + rr[sidx].astype(F32)
            delta, m2, v2 = _adam(w_ref[...], g, m_ref[...], v_ref[...])
            g_ref[...] = g
            d_ref[...] = delta
            mo_ref[...] = m2
            vo_ref[...] = v2

        @pl.when(l == 0)
        def _():
            run(r0_ref)

        @pl.when(l == 1)
        def _():
            run(r1_ref)

    def rblk(n, layer):
        return pl.BlockSpec((n, tr, tc), lambda l, i, j: (0, jnp.where(l == layer, i, 0), jnp.where(l == layer, j, 0)))

    blk = pl.BlockSpec((None, tr, tc), lambda l, i, j: (l, i, j))
    shp = jax.ShapeDtypeStruct((nl, r, c), F32)
    return _pcall(
        body, name=name, grid=(nl, r // tr, c // tc),
        in_specs=[rblk(n0, 0), rblk(n1, 1), blk, blk, blk],
        out_specs=[blk, blk, blk, blk], out_shape=[shp, shp, shp, shp],
        compiler_params=_params(("arbitrary", "parallel", "parallel")),
    )(recv0.reshape(n0, r, c), recv1.reshape(n1, r, c), w, m, v)


def _adam_small(gathered, w, m, v, name):
    r = w.shape[0]
    tr = _pick(r, 512) if r % 512 == 0 else r

    def body(p_ref, w_ref, m_ref, v_ref, g_ref, d_ref, mo_ref, vo_ref):
        g = p_ref[0]
        for sidx in range(1, N_DEV):
            g = g + p_ref[sidx]
        delta, m2, v2 = _adam(w_ref[...], g, m_ref[...], v_ref[...])
        g_ref[...] = g
        d_ref[...] = delta
        mo_ref[...] = m2
        vo_ref[...] = v2

    blk = pl.BlockSpec((tr, LANE), lambda i: (i, 0))
    shp = jax.ShapeDtypeStruct((r, LANE), F32)
    return _pcall(
        body, name=name, grid=(r // tr,),
        in_specs=[pl.BlockSpec((N_DEV, tr, LANE), lambda i: (0, i, 0)), blk, blk, blk],
        out_specs=[blk, blk, blk, blk], out_shape=[shp, shp, shp, shp],
        compiler_params=_params(("parallel",)),
    )(gathered, w, m, v)


def _my_place():
    x, y, c = lax.axis_index("x"), lax.axis_index("y"), lax.axis_index("c")
    return x, y, c, 4 * x + 2 * y + c


def _peer(x, y, c, k):
    px = 1 - x if (k >> 2) & 1 else x
    py = 1 - y if (k >> 1) & 1 else y
    pc = 1 - c if k & 1 else c
    return (px, py, pc), 4 * px + 2 * py + pc


def _slab(ref, shape, kind, p):
    r, cd = shape
    if kind == "row":
        return ref.at[pl.ds(pl.multiple_of(p * r, SUBLANE), r), :]
    return ref.at[:, pl.ds(pl.multiple_of(p * cd, LANE), cd)]


def _exchange_layout(arrays, kinds, gather):
    shard_shapes, dst_kinds, out_shapes = [], [], []
    for a, kind in zip(arrays, kinds):
        r, cd = a.shape
        if gather:
            shard, dst_kind = (r, cd), kind
        else:
            shard, dst_kind = ((r // N_DEV, cd) if kind == "row" else (r, cd // N_DEV)), "row"
        shard_shapes.append(shard)
        dst_kinds.append(dst_kind)
        full = (shard[0] * N_DEV, shard[1]) if dst_kind == "row" else (shard[0], shard[1] * N_DEV)
        out_shapes.append(jax.ShapeDtypeStruct(full, a.dtype))
    return shard_shapes, dst_kinds, out_shapes


def _exchange_copies(ins, outs, send_sems, recv_sems, sem_of, layout, kinds, gather):
    shard_shapes, dst_kinds, _ = layout
    x, y, c, me = _my_place()

    def src_for(a, p):
        return ins[a] if gather else _slab(ins[a], shard_shapes[a], kinds[a], p)

    def dst_slot(a, p):
        return _slab(outs[a], shard_shapes[a], dst_kinds[a], p)

    def local(a, sem):
        return pltpu.make_async_copy(src_for(a, me), dst_slot(a, me), sem)

    def remote(a, k, slot):
        peer, pidx = _peer(x, y, c, k)
        return pltpu.make_async_remote_copy(
            src_ref=src_for(a, pidx), dst_ref=dst_slot(a, me if slot == "mine" else pidx),
            send_sem=send_sems.at[sem_of(a, k)], recv_sem=recv_sems.at[sem_of(a, k)],
            device_id=peer, device_id_type=MESH)

    return local, remote


def _place_own(src, src_kind, slab_shape, dst_kind, full, name, index=None):
    r, cd = slab_shape
    tr = _pick(r, 512)
    nr = r // tr
    me = _my_place()[3] if index is None else index
    src, layer = src if isinstance(src, tuple) else (src, None)

    def body(me_ref, s_ref, o_ref):
        o_ref[...] = s_ref[...].astype(o_ref.dtype)

    def where(kind):
        if kind is None:
            return lambda i, me_ref: (i, 0)
        if kind == "row":
            return lambda i, me_ref: (me_ref[0] * nr + i, 0)
        return lambda i, me_ref: (i, me_ref[0])

    if layer is None:
        src_spec = pl.BlockSpec((tr, cd), where(src_kind))
    else:
        src_spec = pl.BlockSpec((None, tr, cd), lambda i, me_ref: (layer, *where(src_kind)(i, me_ref)))
    return _pcall(
        body, name=name,
        grid_spec=pltpu.PrefetchScalarGridSpec(
            num_scalar_prefetch=1, grid=(nr,),
            in_specs=[src_spec],
            out_specs=pl.BlockSpec((tr, cd), where(dst_kind))),
        out_shape=jax.ShapeDtypeStruct(full.shape, full.dtype),
        compiler_params=_params(("arbitrary",)),
    )(jnp.reshape(me, (1,)).astype(jnp.int32), src)


def _landing_with_own_slab(arrays, kinds, gather, layout, name=None, order_after=None):
    _, _, _, me = _my_place()
    index = None
    if order_after is not None:
        first = order_after.reshape(-1)[0].astype(jnp.int32)
        index = me + jnp.minimum(jnp.maximum(first, 0), 0)
    lands = []
    for a in range(len(arrays)):
        (r, cd), dst_kind, full = layout[0][a], layout[1][a], layout[2][a]
        if name is not None:
            lands.append(_place_own(arrays[a], None if gather else kinds[a], (r, cd), dst_kind, full, name,
                                    index=index))
            continue
        if gather:
            piece = arrays[a]
        elif kinds[a] == "row":
            piece = lax.dynamic_slice_in_dim(arrays[a], me * r, r, axis=0)
        else:
            piece = lax.dynamic_slice_in_dim(arrays[a], me * cd, cd, axis=1)
        at = (me * r, 0) if dst_kind == "row" else (0, me * cd)
        lands.append(lax.dynamic_update_slice(lax.empty(full.shape, full.dtype), piece, at))
    return lands


def _exchange(arrays, kinds, gather, name, after=None):
    n = len(arrays)
    n_extra = 0 if after is None else 1
    layout = _exchange_layout(arrays, kinds, gather)
    lands = _landing_with_own_slab(arrays, kinds, gather, layout)

    def body(*refs):
        ins, outs = refs[:n], refs[2 * n + n_extra:3 * n + n_extra]
        send_sems, recv_sems = refs[3 * n + n_extra:]
        _, remote = _exchange_copies(ins, outs, send_sems, recv_sems,
                                     lambda a, k: a * (N_DEV - 1) + k - 1, layout, kinds, gather)
        for a in range(n):
            for k in range(1, N_DEV):
                remote(a, k, "mine").start()
        for a in range(n):
            for k in range(1, N_DEV):
                arrival = remote(a, k, "theirs")
                arrival.wait_send()
                arrival.wait_recv()

    anyspec = pl.BlockSpec(memory_space=pl.ANY)
    outs = _pcall(
        body, comm=True, name=name,
        in_specs=[anyspec] * (2 * n + n_extra), out_specs=[anyspec] * n, out_shape=layout[2],
        input_output_aliases={n + a: a for a in range(n)},
        scratch_shapes=[pltpu.SemaphoreType.DMA((n * (N_DEV - 1),)), pltpu.SemaphoreType.DMA((n * (N_DEV - 1),))],
    )(*arrays, *lands, *([] if after is None else [after]))
    return list(outs)


HBM_SPEC = pl.BlockSpec(memory_space=pltpu.HBM)
SEM_SPEC = pl.BlockSpec(memory_space=pltpu.SEMAPHORE)
ANY_SPEC = pl.BlockSpec(memory_space=pl.ANY)
DATAFLOW = pltpu.SideEffectType.DATAFLOW_SIDE_EFFECTING


def _exchange_start(arrays, kinds, gather, name, after, carry=()):
    n, nc = len(arrays), len(carry)
    layout = _exchange_layout(arrays, kinds, gather)
    lands = _landing_with_own_slab(arrays, kinds, gather, layout, "place_own")

    def body(*refs):
        ins, outs = refs[:n], refs[n:2 * n]
        send_sems, recv_sems = refs[2 * n + nc + 1], refs[2 * n + nc + 2]
        token = refs[2 * n + nc + 3 + 2 * n + nc]
        _, remote = _exchange_copies(ins, outs, send_sems, recv_sems, lambda a, k: a, layout, kinds, gather)
        for a in range(n):
            for k in range(1, N_DEV):
                remote(a, k, "mine").start()
        token[...] = jnp.zeros_like(token)

    passed = list(arrays) + lands + list(carry)
    res = pl.pallas_call(
        body, name=name,
        out_shape=(pltpu.SemaphoreType.DMA((n,)), pltpu.SemaphoreType.DMA((n,)),
                   *[pltpu.HBM(t.shape, t.dtype) for t in passed], jax.ShapeDtypeStruct((SUBLANE, LANE), F32)),
        in_specs=[HBM_SPEC] * (2 * n + nc) + [ANY_SPEC],
        out_specs=(SEM_SPEC, SEM_SPEC, *([HBM_SPEC] * (2 * n + nc)), pl.BlockSpec(memory_space=pltpu.VMEM)),
        input_output_aliases={i: 2 + i for i in range(2 * n + nc)},
        compiler_params=pltpu.CompilerParams(has_side_effects=DATAFLOW),
    )(*[pltpu.with_memory_space_constraint(t, pltpu.HBM) for t in passed], after)
    handle = (res[0], res[1], list(res[2:2 + n]), list(res[2 + n:2 + 2 * n]), tuple(kinds), gather)
    return handle, res[-1], list(res[2 + 2 * n:2 + 2 * n + nc])


def _exchange_wait(handle, name, after):
    send_sems, recv_sems, ins_thru, lands_thru, kinds, gather = handle
    n = len(ins_thru)
    layout = _exchange_layout(ins_thru, kinds, gather)

    def body(*refs):
        ins, outs = refs[:n], refs[n:2 * n]
        s_sems, r_sems = refs[2 * n], refs[2 * n + 1]
        _, remote = _exchange_copies(ins, outs, s_sems, r_sems, lambda a, k: a, layout, kinds, gather)
        for a in range(n):
            for k in range(1, N_DEV):
                arrival = remote(a, k, "theirs")
                arrival.wait_send()
                arrival.wait_recv()

    res = pl.pallas_call(
        body, name=name,
        out_shape=[pltpu.HBM(t.shape, t.dtype) for t in ins_thru + lands_thru],
        in_specs=[HBM_SPEC] * (2 * n) + [SEM_SPEC, SEM_SPEC, ANY_SPEC],
        out_specs=[HBM_SPEC] * (2 * n),
        input_output_aliases={i: i for i in range(2 * n)},
        compiler_params=pltpu.CompilerParams(has_side_effects=DATAFLOW),
    )(*ins_thru, *lands_thru, send_sems, recv_sems, after)
    return list(res[n:2 * n])


STAGE1_KS = (1, 2, 4, 6)
FORWARD_KS = (2, 4, 6)


def _gather2_copies(lands, shard_shapes, kinds):
    x, y, c, me = _my_place()

    def slab(a, p):
        return _slab(lands[a], shard_shapes[a], kinds[a], p)

    def stage1(a, k, sems, arriving):
        peer, pidx = _peer(x, y, c, k)
        s = slab(a, pidx if arriving else me)
        return pltpu.make_async_remote_copy(src_ref=s, dst_ref=s, send_sem=sems[0].at[a], recv_sem=sems[1].at[a],
                                            device_id=peer, device_id_type=MESH)

    def stage2(a, k, sems, arriving):
        sib, _ = _peer(x, y, c, 1)
        _, mine = _peer(x, y, c, k)
        _, theirs = _peer(x, y, 1 - c, k)
        s = slab(a, theirs if arriving else mine)
        return pltpu.make_async_remote_copy(src_ref=s, dst_ref=s, send_sem=sems[0].at[a], recv_sem=sems[1].at[a],
                                            device_id=sib, device_id_type=MESH)

    return stage1, stage2


def _gather2_call(lands, sems_in, name, after, make_body, returns_sems):
    n = len(lands)
    n_in = len(sems_in)

    def body(*refs):
        land_refs = refs[:n]
        in_sems = refs[n:n + n_in]
        rest = refs[n + n_in + 1:]
        out_sems = rest[:2] if returns_sems else ()
        make_body(land_refs, in_sems, out_sems)
        if returns_sems:
            token = rest[2 + n]
            token[...] = jnp.zeros_like(token)

    sem_shapes = (pltpu.SemaphoreType.DMA((n,)), pltpu.SemaphoreType.DMA((n,))) if returns_sems else ()
    tok_shape = (jax.ShapeDtypeStruct((SUBLANE, LANE), F32),) if returns_sems else ()
    n_sem_out = len(sem_shapes)
    res = pl.pallas_call(
        body, name=name,
        out_shape=(*sem_shapes, *[pltpu.HBM(t.shape, t.dtype) for t in lands], *tok_shape),
        in_specs=[HBM_SPEC] * n + [SEM_SPEC] * n_in + [ANY_SPEC],
        out_specs=(*([SEM_SPEC] * n_sem_out), *([HBM_SPEC] * n),
                   *([pl.BlockSpec(memory_space=pltpu.VMEM)] if returns_sems else [])),
        input_output_aliases={i: n_sem_out + i for i in range(n)},
        compiler_params=pltpu.CompilerParams(has_side_effects=DATAFLOW),
    )(*[pltpu.with_memory_space_constraint(t, pltpu.HBM) for t in lands], *sems_in, after)
    sems_out = tuple(res[:n_sem_out])
    lands_thru = list(res[n_sem_out:n_sem_out + n])
    return sems_out, lands_thru, (res[-1] if returns_sems else None)


def _gather2_start(shards, kinds, name, after):
    n = len(shards)
    views = [jax.ShapeDtypeStruct(t.shape[1:], t.dtype) for t, _ in shards]
    shard_shapes, dst_kinds, fulls = _exchange_layout(views, kinds, True)
    layout = (shard_shapes, dst_kinds, [jax.ShapeDtypeStruct(f.shape, BF16) for f in fulls])
    lands = _landing_with_own_slab(shards, kinds, True, layout, "place_own", order_after=after)

    def make_body(land_refs, in_sems, out_sems):
        stage1, _ = _gather2_copies(land_refs, layout[0], kinds)
        for a in range(n):
            for k in STAGE1_KS:
                stage1(a, k, out_sems, False).start()

    sems, lands, token = _gather2_call(lands, (), name, after, make_body, True)
    return (sems, lands, layout[0], tuple(kinds)), token


def _gather2_forward(handle, name, after, carry=()):
    sems1, lands, shard_shapes, kinds = handle
    n = len(lands)

    def make_body(land_refs, in_sems, out_sems):
        stage1, stage2 = _gather2_copies(land_refs, shard_shapes, kinds)
        for a in range(n):
            for k in STAGE1_KS:
                arrival = stage1(a, k, in_sems, True)
                arrival.wait_send()
                arrival.wait_recv()
        for a in range(n):
            for k in FORWARD_KS:
                stage2(a, k, out_sems, False).start()

    sems2, passed, token = _gather2_call(list(lands) + list(carry), sems1, name, after, make_body, True)
    return (sems2, passed[:n], shard_shapes, kinds), token, passed[n:]


def _gather2_wait(handle, name, after):
    sems2, lands, shard_shapes, kinds = handle
    n = len(lands)

    def make_body(land_refs, in_sems, out_sems):
        _, stage2 = _gather2_copies(land_refs, shard_shapes, kinds)
        for a in range(n):
            for k in FORWARD_KS:
                arrival = stage2(a, k, in_sems, True)
                arrival.wait_send()
                arrival.wait_recv()

    _, lands, _ = _gather2_call(lands, sems2, name, after, make_body, False)
    return lands


N_CHIP = N_DEV // 2


def _scatter2_pair(g, name):
    r, c8 = g.shape
    cd = c8 // N_DEV

    def body(g_ref, o_ref, send_sems, recv_sems):
        x, y, c, _ = _my_place()
        sib, _ = _peer(x, y, c, 1)

        def copy(ch):
            theirs = 2 * ch + (1 - c)
            return pltpu.make_async_remote_copy(
                src_ref=_slab(g_ref, (r, cd), "col", theirs), dst_ref=_slab(o_ref, (r, cd), "row", ch),
                send_sem=send_sems.at[ch], recv_sem=recv_sems.at[ch], device_id=sib, device_id_type=MESH)

        for ch in range(N_CHIP):
            copy(ch).start()
        for ch in range(N_CHIP):
            copy(ch).wait_send()
            copy(ch).wait_recv()

    return _pcall(
        body, comm=True, name=name,
        in_specs=[ANY_SPEC], out_specs=ANY_SPEC, out_shape=jax.ShapeDtypeStruct((N_CHIP * r, cd), g.dtype),
        scratch_shapes=[pltpu.SemaphoreType.DMA((N_CHIP,)), pltpu.SemaphoreType.DMA((N_CHIP,))],
    )(g)


def _scatter2_add(g, from_sibling, name):
    r, c8 = g.shape
    cd = c8 // N_DEV
    tr = _pick(r, 512)
    nr = r // tr
    _, _, core, _ = _my_place()

    def body(c_ref, g_ref, s_ref, o_ref):
        o_ref[...] = (g_ref[...].astype(F32) + s_ref[...].astype(F32)).astype(o_ref.dtype)

    return _pcall(
        body, name=name,
        grid_spec=pltpu.PrefetchScalarGridSpec(
            num_scalar_prefetch=1, grid=(N_CHIP, nr),
            in_specs=[pl.BlockSpec((tr, cd), lambda ch, i, c_ref: (i, 2 * ch + c_ref[0])),
                      pl.BlockSpec((tr, cd), lambda ch, i, c_ref: (ch * nr + i, 0))],
            out_specs=pl.BlockSpec((tr, cd), lambda ch, i, c_ref: (ch * nr + i, 0))),
        out_shape=jax.ShapeDtypeStruct((N_CHIP * r, cd), g.dtype),
        compiler_params=_params(("arbitrary", "arbitrary")),
    )(jnp.reshape(core, (1,)).astype(jnp.int32), g, from_sibling)


def _scatter2_copies(refs, r):
    part_ref, recv_ref = refs
    x, y, c, _ = _my_place()
    my_chip = 2 * x + y

    def rows(ref, ch):
        return ref.at[pl.ds(pl.multiple_of(ch * r, SUBLANE), r), :]

    def copy(k, sems, arriving):
        peer, _ = _peer(x, y, c, k)
        peer_chip = 2 * peer[0] + peer[1]
        return pltpu.make_async_remote_copy(
            src_ref=rows(part_ref, peer_chip), dst_ref=rows(recv_ref, peer_chip if arriving else my_chip),
            send_sem=sems[0].at[0], recv_sem=sems[1].at[0], device_id=peer, device_id_type=MESH)

    return copy


def _scatter2_start(g, name, after, carry=()):
    r = g.shape[0]
    x, y, _, _ = _my_place()
    from_sibling = _scatter2_pair(g, name + "_pair")
    partial = _scatter2_add(g, from_sibling, name + "_add")
    cd = partial.shape[1]
    recv = _place_own(partial, "row", (r, cd), "row", jax.ShapeDtypeStruct(partial.shape, partial.dtype),
                      "place_own", index=2 * x + y)

    def make_body(land_refs, in_sems, out_sems):
        copy = _scatter2_copies(land_refs[:2], r)
        for k in FORWARD_KS:
            copy(k, out_sems, False).start()

    sems, lands, token = _gather2_call([partial, recv] + list(carry), (), name, after, make_body, True)
    return (sems, lands[:2], r), token, lands[2:]


def _scatter2_wait(handle, name, after):
    sems, lands, r = handle

    def make_body(land_refs, in_sems, out_sems):
        copy = _scatter2_copies(land_refs, r)
        for k in FORWARD_KS:
            arrival = copy(k, in_sems, True)
            arrival.wait_send()
            arrival.wait_recv()

    _, lands, _ = _gather2_call(lands, sems, name, after, make_body, False)
    return lands[1]


def _pad_mla_w_in(w):
    d = w.shape[0]
    z = lambda n: jnp.zeros((d, n), w.dtype)
    o1, o2, o3 = Q_RANK, Q_RANK + KV_RANK, Q_RANK + KV_RANK + ROPE
    return jnp.concatenate([w[:, :o1], z(Q_RANK_PAD - Q_RANK), w[:, o1:o2], w[:, o2:o3], z(LANE - ROPE), w[:, o3:]], axis=1)


def _unpad_mla_w_in(g):
    return jnp.concatenate([g[:, :Q_RANK], g[:, PROJ_CKV:PROJ_KR], g[:, PROJ_KR:PROJ_KR + ROPE], g[:, PROJ_Z:]], axis=1)


def _pad_w_uq(w):
    w3 = w.reshape(Q_RANK, HEADS, NOPE + ROPE)
    w3 = jnp.pad(w3, ((0, Q_RANK_PAD - Q_RANK), (0, 0), (0, HEAD_PAD - NOPE - ROPE)))
    return w3.reshape(Q_RANK_PAD, HEADS * HEAD_PAD)


def _unpad_w_uq(g):
    return g[:Q_RANK].reshape(Q_RANK, HEADS, HEAD_PAD)[:, :, :NOPE + ROPE].reshape(Q_RANK, HEADS * (NOPE + ROPE))


def _pack(pieces):
    flat = [p.reshape(-1).astype(F32) for p in pieces]
    tot = sum(f.shape[0] for f in flat)
    unit = SUBLANE * LANE
    padn = (-tot) % unit
    if padn:
        flat.append(jnp.zeros((padn,), F32))
    return jnp.concatenate(flat).reshape(-1, LANE)


def _unpack(packed, shapes, lead=()):
    flat = packed.reshape(tuple(lead) + (-1,))
    out, off = [], 0
    for shp in shapes:
        nel = int(np.prod(shp))
        out.append(flat[..., off:off + nel].reshape(tuple(lead) + tuple(shp)))
        off += nel
    return out


SMALL_GROUPS = {
    "a": [('ada_b', lambda t: t[1:]), ('pre_g', lambda t: t[1:]), ('post_g', lambda t: t[1:]),
          ('sgu_norm_g', lambda t: t[1:]), ('sgu_w_s', lambda t: t[1:]), ('sgu_b_s', lambda t: t[1:]),
          ('mla_q_norm_g', lambda t: t), ('mla_kv_norm_g', lambda t: t)],
    "b": [('ada_b', lambda t: t[0:1, 2 * t.shape[1] // 3:]), ('post_g', lambda t: t[0:1]),
          ('sgu_norm_g', lambda t: t[0:1]), ('sgu_w_s', lambda t: t[0:1]), ('sgu_b_s', lambda t: t[0:1])],
    "c": [('ada_b', lambda t: t[0:1, :2 * t.shape[1] // 3]), ('pre_g', lambda t: t[0:1])],
}


WEIGHTS = ['ada_w', 'ada_b', 'pre_g', 'post_g', 'sgu_w_in', 'sgu_norm_g', 'sgu_w_s', 'sgu_b_s', 'sgu_w_out',
           'mla_w_in', 'mla_q_norm_g', 'mla_kv_norm_g', 'mla_w_uq', 'mla_w_ukv', 'mla_w_out']
INPUTS = ['x', 'c'] + WEIGHTS + ['loss_target'] + ['m_' + n for n in WEIGHTS] + ['v_' + n for n in WEIGHTS]


def kernel(x, c, ada_w, ada_b, pre_g, post_g, sgu_w_in, sgu_norm_g, sgu_w_s, sgu_b_s, sgu_w_out, mla_w_in, mla_q_norm_g, mla_kv_norm_g, mla_w_uq, mla_w_ukv, mla_w_out, loss_target, m_ada_w, m_ada_b, m_pre_g, m_post_g, m_sgu_w_in, m_sgu_norm_g, m_sgu_w_s, m_sgu_b_s, m_sgu_w_out, m_mla_w_in, m_mla_q_norm_g, m_mla_kv_norm_g, m_mla_w_uq, m_mla_w_ukv, m_mla_w_out, v_ada_w, v_ada_b, v_pre_g, v_post_g, v_sgu_w_in, v_sgu_norm_g, v_sgu_w_s, v_sgu_b_s, v_sgu_w_out, v_mla_w_in, v_mla_q_norm_g, v_mla_kv_norm_g, v_mla_w_uq, v_mla_w_ukv, v_mla_w_out):
    given = locals()
    A = {name: given[name] for name in INPUTS}
    x0 = A['x'][0]
    tgt = A['loss_target'][0]
    s, d = x0.shape
    e = 2 * d
    ncol = 3 * d // N_DEV
    _, _, _, me = _my_place()
    ktabs = _rope_tables(s)

    gains = jnp.zeros((SUBLANE, LANE), F32)
    gains = gains.at[0:2, :Q_RANK // N_DEV].set(A['mla_q_norm_g'])
    gains = gains.at[2:4, :KV_RANK // N_DEV].set(A['mla_kv_norm_g'])
    c8 = jnp.broadcast_to(A['c'], (SUBLANE, d))
    cg, gg = _exchange([c8, gains], ["row", "row"], True, "ag_cond")
    cond_raw = cg.reshape(N_DEV, SUBLANE, d)[:, 0, :]
    gg = gg.reshape(N_DEV, SUBLANE, LANE)
    gq_full = jnp.transpose(gg[:, 0:2, :Q_RANK // N_DEV], (1, 0, 2)).reshape(N_MIX, Q_RANK)
    gkv_full = jnp.transpose(gg[:, 2:4, :KV_RANK // N_DEV], (1, 0, 2)).reshape(N_MIX, KV_RANK)
    gq_pad = jnp.pad(gq_full, ((0, 0), (0, Q_RANK_PAD - Q_RANK)))

    bias_my = lax.dynamic_slice_in_dim(A['ada_b'], me * ncol, ncol, axis=1)
    mod_part = _ada_mod(cond_raw, A['ada_w'], bias_my, "ada_mod")
    send = jnp.pad(jnp.transpose(mod_part, (1, 0, 2)), ((0, 0), (0, SUBLANE - DEPTH), (0, 0)))
    (rb,) = _exchange([send.reshape(N_DEV * SUBLANE, ncol)], ["row"], False, "a2a_mod")

    token = rb
    gathers = {}
    for i in range(DEPTH):
        j = i // N_MIX
        if i % N_MIX == 0:
            parts = [("in", [(A['sgu_w_in'], j)], ["col"]), ("out", [(A['sgu_w_out'], j)], ["row"])]
        else:
            parts = [("all", [(A['mla_w_in'], j), (A['mla_w_uq'], j), (A['mla_w_ukv'], j), (A['mla_w_out'], j)],
                      ["col", "col", "col", "row"])]
        for part, shards, kinds in parts:
            gathers[(i, part)], token = _gather2_start(shards, kinds, f"ag_start_{i}_{part}", token)

    def forward_gathers(i, carried):
        for key in [k for k in gathers if k[0] == i]:
            gathers[key], _, (carried,) = _gather2_forward(gathers[key], f"ag_forward_{key[0]}_{key[1]}", token,
                                                           carry=[carried])
        return carried

    mod = jnp.transpose(rb.reshape(N_DEV, SUBLANE, ncol)[:, :DEPTH, :], (1, 0, 2)).reshape(DEPTH, 3 * d) + token[0, 0]
    shift = [mod[i:i + 1, :d] for i in range(DEPTH)]
    scale = [mod[i:i + 1, d:2 * d] for i in range(DEPTH)]
    gate = [mod[i:i + 1, 2 * d:] for i in range(DEPTH)]

    saved = []
    x = x0
    for i in range(DEPTH):
        j = i // N_MIX
        h = _pre_fwd(x, A['pre_g'][i:i + 1], scale[i], shift[i], f"pre_fwd")
        if i == 0:
            gathers[(0, "in")], _, (h,) = _gather2_forward(gathers[(0, "in")], "ag_forward_0_in", token, carry=[h])
        if i % N_MIX == 0:
            (w_in,) = _gather2_wait(gathers[(i, "in")], f"ag_wait_{i}_in", h)
            uvz = _mm(h, w_in, "nn", F32, "sgu_in")
            if i == 0:
                gathers[(0, "out")], _, (uvz,) = _gather2_forward(gathers[(0, "out")], "ag_forward_0_out", token,
                                                                  carry=[uvz])
            bias_full = jnp.repeat(A['sgu_b_s'][j].T, e // SGU_GROUPS, axis=1)
            ng = A['sgu_norm_g'][j:j + 1]
            y = _sgu_mid_fwd(uvz, ng, A['sgu_w_s'][j], bias_full, "sgu_mid_fwd")
            y = forward_gathers(i + 1, y)
            (w_out,) = _gather2_wait(gathers[(i, "out")], f"ag_wait_{i}_out", y)
            out = _mm(y, w_out, "nn", F32, "sgu_out")
            saved.append(dict(x=x, h=h, uvz=uvz, y=y, out=out, w_in=w_in, w_out=w_out, bias=bias_full, ng=ng))
        else:
            w_in, w_uq, w_ukv, w_out = _gather2_wait(gathers[(i, "all")], f"ag_wait_{i}_all", h)
            w_in = _pad_mla_w_in(w_in)
            w_uq = _pad_w_uq(w_uq)
            gq, gkv = gq_pad[j:j + 1], gkv_full[j:j + 1]
            proj = _mm(h, w_in, "nn", F32, "mla_in")
            qn, kvn, kr = _mla_norm_fwd(proj, gq, gkv, ktabs, "mla_norm_fwd")
            q = _mm(qn, w_uq, "nn", F32, "mla_uq")
            kv = _mm(kvn, w_ukv, "nn", BF16, "mla_ukv")
            o, y, lse, q_cat = _attn_fwd(q, kv, kr, proj, ktabs, "attn_fwd")
            y = forward_gathers(i + 1, y)
            out = _mm(y, w_out, "nn", F32, "mla_out")
            saved.append(dict(x=x, h=h, proj=proj, qn=qn, kvn=kvn, kr=kr, q_cat=q_cat, kv=kv, o=o, y=y, lse=lse,
                              out=out, w_in=w_in, w_uq=w_uq, w_ukv=w_ukv, w_out=w_out, gq=gq, gkv=gkv))
        x = _post_fwd(x, out, gate[i], A['post_g'][i:i + 1], "post_fwd")

    dx, loss_row = _loss_head(x, tgt, "loss_head")
    loss = lax.psum(loss_row[0, 0], ("x", "y", "c"))

    d_shift, d_scale, d_gate = [None] * DEPTH, [None] * DEPTH, [None] * DEPTH
    d_pre, d_post = [None] * DEPTH, [None] * DEPTH
    scatters, scatters_out = [None] * DEPTH, [None] * DEPTH
    small, small_grads, small_handles = {}, {}, {}
    for i in reversed(range(DEPTH)):
        j = i // N_MIX
        sv = saved[i]
        dout, d_gate[i], d_post[i] = _post_bwd(dx, sv['out'], gate[i], A['post_g'][i:i + 1], "post_bwd")
        if i % N_MIX == 0:
            dy = _mm(dout, sv['w_out'], "nt", F32, "sgu_out_dx")
            g_w_out = _mm(sv['y'], dout, "tn", BF16, "sgu_out_dw")
            scatters_out[i], token, (dy,) = _exchange_start([g_w_out], ["row"], False, f"rs_out_start_{i}", token,
                                                            carry=[dy])
            duvz, dws, dbs, dng = _sgu_mid_bwd(sv['uvz'], dy, sv['ng'], A['sgu_w_s'][j], sv['bias'], "sgu_mid_bwd")
            small[('sgu', j)] = (dws, dbs.reshape(SGU_GROUPS, SGU_BLOCK), dng)
            g_w_in = _mm(sv['h'], duvz, "tn", BF16, "sgu_in_dw")
            if i == 0:
                scatters[i], token, (duvz,) = _scatter2_start(g_w_in, "rs2_start_0", token, carry=[duvz])
                small_grads["b"] = [d_gate[0], d_post[0], dng, dws[None], small[('sgu', 0)][1][None]]
                small_handles["b"], token, (duvz,) = _exchange_start(
                    [_pack(small_grads["b"])], ["row"], True, "ag_small_b_start", token, carry=[duvz])
            else:
                scatters[i], token, (duvz,) = _exchange_start([g_w_in], ["col"], False, f"rs_start_{i}", token,
                                                              carry=[duvz])
            dh = _mm(duvz, sv['w_in'], "nt", F32, "sgu_in_dx")
        else:
            dy = _mm(dout, sv['w_out'], "nt", F32, "mla_out_dx")
            g_w_out = _mm(sv['y'], dout, "tn", BF16, "mla_out_dw")
            scatters_out[i], token, (dy,) = _exchange_start([g_w_out], ["row"], False, f"rs_out_start_{i}", token,
                                                            carry=[dy])
            do, dproj = _mla_gate_bwd(dy, sv['o'], sv['proj'], "mla_gate_bwd")
            dq_b, dkv, dkr_heads = _attn_bwd(sv['q_cat'], sv['kv'], sv['kr'], do, sv['o'], sv['lse'], ktabs,
                                             "attn_bwd")
            dqn = _mm(dq_b, sv['w_uq'], "nt", F32, "mla_uq_dx")
            g_w_uq = _unpad_w_uq(_mm(sv['qn'], dq_b, "tn", BF16, "mla_uq_dw"))
            dkvn = _mm(dkv, sv['w_ukv'], "nt", F32, "mla_ukv_dx")
            g_w_ukv = _mm(sv['kvn'], dkv, "tn", BF16, "mla_ukv_dw")
            dproj, dgq, dgkv = _mla_norm_bwd(dqn, dkvn, dkr_heads, sv['proj'], sv['gq'], sv['gkv'], ktabs, dproj,
                                             "mla_norm_bwd")
            g_w_in = _unpad_mla_w_in(_mm(sv['h'], dproj, "tn", BF16, "mla_in_dw"))
            scatters[i], token, (dproj,) = _exchange_start(
                [g_w_in, g_w_uq, g_w_ukv], ["col", "col", "col"], False, f"rs_start_{i}", token, carry=[dproj])
            dh = _mm(dproj, sv['w_in'], "nt", F32, "mla_in_dx")
            small[('mla', j)] = (dgq[:, :Q_RANK], dgkv)
        dx, d_shift[i], d_scale[i], d_pre[i] = _pre_bwd(dh, sv['x'], dx, A['pre_g'][i:i + 1], scale[i], "pre_bwd")
        if i == 1:
            small_grads["a"] = [
                jnp.concatenate([jnp.concatenate([d_shift[l], d_scale[l], d_gate[l]], axis=1)
                                 for l in range(1, DEPTH)], axis=0),
                jnp.concatenate(d_pre[1:], axis=0), jnp.concatenate(d_post[1:], axis=0),
                small[('sgu', 1)][2], small[('sgu', 1)][0][None], small[('sgu', 1)][1][None],
                jnp.concatenate([small[('mla', jj)][0] for jj in range(N_MIX)], axis=0),
                jnp.concatenate([small[('mla', jj)][1] for jj in range(N_MIX)], axis=0)]
            small_handles["a"], token, (dx,) = _exchange_start(
                [_pack(small_grads["a"])], ["row"], True, "ag_small_a_start", token, carry=[dx])

    res = {}

    def big(name, recv0, recv1):
        res[name] = _adam_reduce(recv0, recv1, A[name], A['m_' + name], A['v_' + name], "adam_" + name)

    def small_full(name, prefix):
        t = A[prefix + name]
        if name in ('mla_q_norm_g', 'mla_kv_norm_g'):
            t = lax.dynamic_update_slice_in_dim(jnp.zeros((t.shape[0], t.shape[1] * N_DEV), F32), t,
                                                me * t.shape[1], axis=1)
        return t

    def finish(i, after):
        first = ([_scatter2_wait(scatters[i], "rs2_wait_0", after)] if i == 0
                 else _exchange_wait(scatters[i], f"rs_wait_{i}", after))
        return first + _exchange_wait(scatters_out[i], f"rs_out_wait_{i}", after)

    recv_mla = {j: finish(N_MIX * j + 1, dx) for j in reversed(range(N_MIX))}
    for idx_w, name in enumerate(['mla_w_in', 'mla_w_uq', 'mla_w_ukv', 'mla_w_out']):
        big(name, recv_mla[0][idx_w], recv_mla[1][idx_w])

    recv_sgu = {j: finish(N_MIX * j, res['mla_w_out'][0]) for j in reversed(range(N_MIX))}
    for idx_w, name in enumerate(['sgu_w_in', 'sgu_w_out']):
        big(name, recv_sgu[0][idx_w], recv_sgu[1][idx_w])

    small_grads["c"] = [jnp.concatenate([d_shift[0], d_scale[0]], axis=1), d_pre[0]]
    gathered = {tag: _exchange_wait(small_handles[tag], f"ag_small_{tag}_wait", res['sgu_w_in'][0])[0]
                for tag in ("a", "b")}
    (gathered["c"],) = _exchange([_pack(small_grads["c"])], ["row"], True, "ag_small_c", after=res['sgu_w_in'][0])

    upd, parts = {}, {}
    for tag, group in SMALL_GROUPS.items():
        shapes = [g.shape for g in small_grads[tag]]
        packs = [_pack([pick(small_full(name, prefix)) for name, pick in group]) for prefix in ('', 'm_', 'v_')]
        outs4 = _adam_small(gathered[tag].reshape(N_DEV, -1, LANE), *packs, "adam_small_" + tag)
        upd[tag] = [_unpack(t, shapes) for t in outs4]
        parts[tag] = _unpack(gathered[tag].reshape(N_DEV, -1), shapes, lead=(N_DEV,))
    for k_out in range(4):
        ab_rest, pre_rest, post_rest, ng_1, ws_1, bs_1, gq_all, gkv_all = upd["a"][k_out]
        ab_gate0, post_0, ng_0, ws_0, bs_0 = upd["b"][k_out]
        ab_0, pre_0 = upd["c"][k_out]
        for name, val in (
                ('ada_b', jnp.concatenate([jnp.concatenate([ab_0, ab_gate0], axis=1), ab_rest], axis=0)),
                ('pre_g', jnp.concatenate([pre_0, pre_rest], axis=0)),
                ('post_g', jnp.concatenate([post_0, post_rest], axis=0)),
                ('sgu_norm_g', jnp.concatenate([ng_0, ng_1], axis=0)),
                ('sgu_w_s', jnp.concatenate([ws_0, ws_1], axis=0)),
                ('sgu_b_s', jnp.concatenate([bs_0, bs_1], axis=0)),
                ('mla_q_norm_g', gq_all), ('mla_kv_norm_g', gkv_all)):
            if name in ('mla_q_norm_g', 'mla_kv_norm_g'):
                wdt = A[name].shape[1]
                val = lax.dynamic_slice_in_dim(val, me * wdt, wdt, axis=1)
            res.setdefault(name, [None] * 4)[k_out] = val

    dmod_all = jnp.concatenate([jnp.concatenate([parts["c"][0], parts["b"][0]], axis=2), parts["a"][0]], axis=1)
    dmod_cols = jnp.transpose(lax.dynamic_slice_in_dim(dmod_all, me * ncol, ncol, axis=2), (1, 0, 2))
    res['ada_w'] = _ada_bwd_adam(jnp.transpose(cond_raw), dmod_cols, A['ada_w'], A['m_ada_w'], A['v_ada_w'], "ada_bwd")

    outs = [loss, dx[None]]
    for k_out in range(4):
        outs += [res[n][k_out] for n in WEIGHTS]
    return tuple(outs)
```

```python
import functools
import math

import numpy as np
import jax
import jax.numpy as jnp
from jax import lax
from jax.experimental import pallas as pl
from jax.experimental.pallas import tpu as pltpu

F32 = jnp.float32
BF16 = jnp.bfloat16
MESH = pl.DeviceIdType.MESH

N_DEV = 8
DEPTH = 4
N_MIX = 2
NORM_EPS = 1e-6
CHUNK = 64
SGU_BLOCK = 128
SGU_GROUPS = 16
HEADS = 16
Q_RANK = 448
Q_RANK_PAD = 512
KV_RANK = 512
NOPE = 128
ROPE = 64
HALF = ROPE // 2
V_DIM = 128
HEAD_PAD = 256
ROPE_THETA = 10000.0
MLA_WIDTH = HEADS * V_DIM
LANE = 128
SUBLANE = 8
PROJ_CQ = 0
PROJ_CKV = Q_RANK_PAD
PROJ_KR = Q_RANK_PAD + KV_RANK
PROJ_Z = PROJ_KR + LANE
PROJ_W = PROJ_Z + MLA_WIDTH

ADAM_LR = 0.001
ADAM_B1 = 0.9
ADAM_B2 = 0.999
ADAM_EPS = 1e-08
ADAM_WD = 0.01
ADAM_STEP = 10

VMEM_LIMIT = 56 * 1024 * 1024
ATT_BLK = 512
ATT_SUB = 256
ATT_HEADS_PER_STEP = 2
ROW_BLK = 512
MM_TM, MM_TN, MM_TK = 1024, 1024, 2048
MM_TILE_BYTES = 40 * 1024 * 1024
SOFTMAX_SCALE = (NOPE + ROPE) ** -0.5
LOG2_E = 1.0 / math.log(2.0)
INV_SQRT2 = 1.0 / math.sqrt(2.0)
INV_SQRT_2PI = 1.0 / math.sqrt(2.0 * math.pi)


def _pcall(body, comm=False, **kw):
    return pl.pallas_call(body, **kw)


def _params(sem=None):
    return pltpu.CompilerParams(dimension_semantics=sem, vmem_limit_bytes=VMEM_LIMIT)


def _pick(dim, pref):
    if dim <= pref:
        return dim
    t = (pref // LANE) * LANE
    while t >= LANE:
        if dim % t == 0:
            return t
        t -= LANE
    return dim


def _gelu(x):
    return 0.5 * x * (1.0 + lax.erf(x * INV_SQRT2))


def _gelu_grad(x):
    return 0.5 * (1.0 + lax.erf(x * INV_SQRT2)) + x * jnp.exp(-0.5 * x * x) * INV_SQRT_2PI


def _sigmoid(x):
    return 1.0 / (1.0 + jnp.exp(-x))


def _dot_nt(a, b):
    return lax.dot_general(a, b, (((1,), (1,)), ((), ())), preferred_element_type=F32)


def _dot_tn(a, b):
    return lax.dot_general(a, b, (((0,), (0,)), ((), ())), preferred_element_type=F32)


def _mm(a, b, dims, out_dtype, name):
    if dims == "nn":
        (m, k), (k2, n) = a.shape, b.shape
    elif dims == "nt":
        (m, k), (n, k2) = a.shape, b.shape
    else:
        (k, m), (k2, n) = a.shape, b.shape
    assert k == k2, (a.shape, b.shape, dims)
    tm, tn = _pick(m, MM_TM), _pick(n, MM_TN)
    out_bytes = 2 * tm * tn * jnp.dtype(out_dtype).itemsize
    whole_k = 2 * (tm + tn) * k * a.dtype.itemsize + out_bytes <= MM_TILE_BYTES
    tk = k if whole_k else _pick(k, MM_TK)
    nk = k // tk

    def body(a_ref, b_ref, o_ref, *scratch):
        if dims == "nn":
            p = jnp.dot(a_ref[...], b_ref[...], preferred_element_type=F32)
        elif dims == "nt":
            p = _dot_nt(a_ref[...], b_ref[...])
        else:
            p = _dot_tn(a_ref[...], b_ref[...])
        if nk == 1:
            o_ref[...] = p.astype(o_ref.dtype)
            return
        acc_ref, = scratch
        kk = pl.program_id(2)

        @pl.when(kk == 0)
        def _():
            acc_ref[...] = p

        @pl.when(kk > 0)
        def _():
            acc_ref[...] += p

        @pl.when(kk == nk - 1)
        def _():
            o_ref[...] = acc_ref[...].astype(o_ref.dtype)

    if dims == "tn":
        a_spec = pl.BlockSpec((tk, tm), lambda i, j, kk: (kk, i))
    else:
        a_spec = pl.BlockSpec((tm, tk), lambda i, j, kk: (i, kk))
    if dims == "nt":
        b_spec = pl.BlockSpec((tn, tk), lambda i, j, kk: (j, kk))
    else:
        b_spec = pl.BlockSpec((tk, tn), lambda i, j, kk: (kk, j))
    return _pcall(
        body, name=name,
        grid=(m // tm, n // tn, nk),
        in_specs=[a_spec, b_spec],
        out_specs=pl.BlockSpec((tm, tn), lambda i, j, kk: (i, j)),
        out_shape=jax.ShapeDtypeStruct((m, n), out_dtype),
        scratch_shapes=[pltpu.VMEM((tm, tn), F32)] if nk > 1 else [],
        compiler_params=_params(("parallel", "parallel", "arbitrary")),
    )(a, b)


def _row_spec(ts, d):
    return pl.BlockSpec((ts, d), lambda i: (i, 0))


def _vec_spec(d):
    return pl.BlockSpec((1, d), lambda i: (0, 0))


def _pre_fwd(x, g, scale, shift, name):
    s, d = x.shape
    ts = _pick(s, ROW_BLK)

    def body(x_ref, g_ref, sc_ref, sh_ref, h_ref):
        xv = x_ref[...]
        r = lax.rsqrt(jnp.mean(xv * xv, axis=-1, keepdims=True) + NORM_EPS)
        h_ref[...] = ((xv * r * g_ref[...]) * (1.0 + sc_ref[...]) + sh_ref[...]).astype(BF16)

    return _pcall(
        body, name=name, grid=(s // ts,),
        in_specs=[_row_spec(ts, d), _vec_spec(d), _vec_spec(d), _vec_spec(d)],
        out_specs=_row_spec(ts, d),
        out_shape=jax.ShapeDtypeStruct((s, d), BF16),
        compiler_params=_params(("parallel",)),
    )(x, g, scale, shift)


def _post_fwd(x, out, gate, g, name):
    s, d = x.shape
    ts = _pick(s, ROW_BLK)

    def body(x_ref, o_ref, gate_ref, g_ref, y_ref):
        o = o_ref[...]
        r = lax.rsqrt(jnp.mean(o * o, axis=-1, keepdims=True) + NORM_EPS)
        y_ref[...] = x_ref[...] + gate_ref[...] * (o * r * g_ref[...])

    return _pcall(
        body, name=name, grid=(s // ts,),
        in_specs=[_row_spec(ts, d), _row_spec(ts, d), _vec_spec(d), _vec_spec(d)],
        out_specs=_row_spec(ts, d),
        out_shape=jax.ShapeDtypeStruct((s, d), F32),
        compiler_params=_params(("parallel",)),
    )(x, out, gate, g)


def _loss_head(xf, tgt, name):
    s, d = xf.shape
    ts = _pick(s, ROW_BLK)
    ns = s // ts

    def body(x_ref, t_ref, dx_ref, loss_ref, acc_ref):
        i = pl.program_id(0)

        @pl.when(i == 0)
        def _():
            acc_ref[...] = jnp.zeros_like(acc_ref)

        e = x_ref[...] - t_ref[...]
        dx_ref[...] = e * (1.0 / d)
        acc_ref[...] += jnp.sum(e * e, axis=0, keepdims=True)

        @pl.when(i == ns - 1)
        def _():
            tot = jnp.sum(acc_ref[...], axis=1, keepdims=True) * (0.5 / d)
            loss_ref[...] = jnp.broadcast_to(tot, loss_ref.shape)

    return _pcall(
        body, name=name, grid=(ns,),
        in_specs=[_row_spec(ts, d), _row_spec(ts, d)],
        out_specs=[_row_spec(ts, d), pl.BlockSpec((1, LANE), lambda i: (0, 0))],
        out_shape=[jax.ShapeDtypeStruct((s, d), F32), jax.ShapeDtypeStruct((1, LANE), F32)],
        scratch_shapes=[pltpu.VMEM((1, d), F32)],
        compiler_params=_params(("arbitrary",)),
    )(xf, tgt)


def _post_bwd(dxo, out, gate, g, name):
    s, d = dxo.shape
    ts = _pick(s, ROW_BLK)

    def body(dx_ref, o_ref, gate_ref, g_ref, do_ref, dgate_ref, dg_ref):
        i = pl.program_id(0)

        @pl.when(i == 0)
        def _():
            dgate_ref[...] = jnp.zeros_like(dgate_ref)
            dg_ref[...] = jnp.zeros_like(dg_ref)

        o = o_ref[...]
        dx = dx_ref[...]
        gv = g_ref[...]
        r = lax.rsqrt(jnp.mean(o * o, axis=-1, keepdims=True) + NORM_EPS)
        n = o * r
        dyn = dx * gate_ref[...]
        dgate_ref[...] += jnp.sum(dx * (n * gv), axis=0, keepdims=True)
        dg_ref[...] += jnp.sum(dyn * n, axis=0, keepdims=True)
        dn = dyn * gv
        do_ref[...] = (r * (dn - n * jnp.mean(dn * n, axis=-1, keepdims=True))).astype(BF16)

    return _pcall(
        body, name=name, grid=(s // ts,),
        in_specs=[_row_spec(ts, d), _row_spec(ts, d), _vec_spec(d), _vec_spec(d)],
        out_specs=[_row_spec(ts, d), _vec_spec(d), _vec_spec(d)],
        out_shape=[jax.ShapeDtypeStruct((s, d), BF16), jax.ShapeDtypeStruct((1, d), F32),
                   jax.ShapeDtypeStruct((1, d), F32)],
        compiler_params=_params(("arbitrary",)),
    )(dxo, out, gate, g)


def _pre_bwd(dh, x, dxo, g, scale, name):
    s, d = x.shape
    ts = _pick(s, ROW_BLK)

    def body(dh_ref, x_ref, dxo_ref, g_ref, sc_ref, dx_ref, dsh_ref, dsc_ref, dg_ref):
        i = pl.program_id(0)

        @pl.when(i == 0)
        def _():
            dsh_ref[...] = jnp.zeros_like(dsh_ref)
            dsc_ref[...] = jnp.zeros_like(dsc_ref)
            dg_ref[...] = jnp.zeros_like(dg_ref)

        xv = x_ref[...]
        dhv = dh_ref[...]
        gv = g_ref[...]
        one_sc = 1.0 + sc_ref[...]
        r = lax.rsqrt(jnp.mean(xv * xv, axis=-1, keepdims=True) + NORM_EPS)
        n = xv * r
        dsh_ref[...] += jnp.sum(dhv, axis=0, keepdims=True)
        dsc_ref[...] += jnp.sum(dhv * (n * gv), axis=0, keepdims=True)
        dng = dhv * one_sc
        dg_ref[...] += jnp.sum(dng * n, axis=0, keepdims=True)
        dn = dng * gv
        dx_ref[...] = dxo_ref[...] + r * (dn - n * jnp.mean(dn * n, axis=-1, keepdims=True))

    return _pcall(
        body, name=name, grid=(s // ts,),
        in_specs=[_row_spec(ts, d), _row_spec(ts, d), _row_spec(ts, d), _vec_spec(d), _vec_spec(d)],
        out_specs=[_row_spec(ts, d), _vec_spec(d), _vec_spec(d), _vec_spec(d)],
        out_shape=[jax.ShapeDtypeStruct((s, d), F32)] + [jax.ShapeDtypeStruct((1, d), F32)] * 3,
        compiler_params=_params(("arbitrary",)),
    )(dh, x, dxo, g, scale)


def _sgu_mask():
    t = lax.broadcasted_iota(jnp.int32, (SGU_BLOCK, SGU_BLOCK), 0) // CHUNK
    s = lax.broadcasted_iota(jnp.int32, (SGU_BLOCK, SGU_BLOCK), 1) // CHUNK
    return s <= t


def _sgu_norm(v_pre, g):
    e = v_pre.shape[-1]
    vg = _gelu(v_pre)
    mu = jnp.sum(vg, axis=-1, keepdims=True) * (1.0 / e)
    dlt = vg - mu
    var = jnp.sum(dlt * dlt, axis=-1, keepdims=True) * (1.0 / e)
    rstd = lax.rsqrt(var + NORM_EPS)
    vhat = dlt * rstd
    return vhat, rstd, (vhat * g).astype(BF16)


def _sgu_mid_fwd(uvz, norm_g, w_s, bias_full, name):
    s, e3 = uvz.shape
    e = e3 // 3
    gd = e // SGU_GROUPS
    nb = s // SGU_BLOCK

    def body(uvz_ref, g_ref, w_ref, b_ref, y_ref, wsc):
        @pl.when(pl.program_id(0) == 0)
        def _():
            msk = _sgu_mask()
            for gi in range(SGU_GROUPS):
                wsc[gi] = jnp.where(msk, w_ref[gi], 0.0).astype(BF16)

        _, _, vb = _sgu_norm(uvz_ref[:, e:2 * e], g_ref[...])
        for gi in range(SGU_GROUPS):
            lo = gi * gd
            vm = jnp.dot(wsc[gi], vb[:, lo:lo + gd], preferred_element_type=F32) + b_ref[:, lo:lo + gd]
            zg = uvz_ref[:, 2 * e + lo:2 * e + lo + gd]
            y_ref[:, lo:lo + gd] = (_gelu(uvz_ref[:, lo:lo + gd]) * vm * (zg * _sigmoid(zg))).astype(BF16)

    return _pcall(
        body, name=name, grid=(nb,),
        in_specs=[pl.BlockSpec((SGU_BLOCK, e3), lambda n: (n, 0)),
                  pl.BlockSpec((1, e), lambda n: (0, 0)),
                  pl.BlockSpec((SGU_GROUPS, SGU_BLOCK, SGU_BLOCK), lambda n: (0, 0, 0)),
                  pl.BlockSpec((SGU_BLOCK, e), lambda n: (0, 0))],
        out_specs=pl.BlockSpec((SGU_BLOCK, e), lambda n: (n, 0)),
        out_shape=jax.ShapeDtypeStruct((s, e), BF16),
        scratch_shapes=[pltpu.VMEM((SGU_GROUPS, SGU_BLOCK, SGU_BLOCK), BF16)],
        compiler_params=_params(("arbitrary",)),
    )(uvz, norm_g, w_s, bias_full)


def _sgu_mid_bwd(uvz, dy, norm_g, w_s, bias_full, name):
    s, e3 = uvz.shape
    e = e3 // 3
    gd = e // SGU_GROUPS
    nb = s // SGU_BLOCK

    def body(uvz_ref, dy_ref, g_ref, w_ref, b_ref, d_ref, dw_ref, db_ref, dg_ref, wsc, wtsc, dvh_sc, dbacc):
        n = pl.program_id(0)

        @pl.when(n == 0)
        def _():
            msk = _sgu_mask()
            for gi in range(SGU_GROUPS):
                wm = jnp.where(msk, w_ref[gi], 0.0)
                wsc[gi] = wm.astype(BF16)
                wtsc[gi] = wm.T.astype(BF16)
            dw_ref[...] = jnp.zeros_like(dw_ref)
            dg_ref[...] = jnp.zeros_like(dg_ref)
            dbacc[...] = jnp.zeros_like(dbacc)

        v_pre = uvz_ref[:, e:2 * e]
        gv = g_ref[...]
        vhat, rstd, vb = _sgu_norm(v_pre, gv)
        s1 = jnp.zeros((SGU_BLOCK, 1), F32)
        s2 = jnp.zeros((SGU_BLOCK, 1), F32)
        for gi in range(SGU_GROUPS):
            lo = gi * gd
            u_pre = uvz_ref[:, lo:lo + gd]
            zg = uvz_ref[:, 2 * e + lo:2 * e + lo + gd]
            dyg = dy_ref[:, lo:lo + gd]
            ug = _gelu(u_pre)
            sig = _sigmoid(zg)
            vbg = vb[:, lo:lo + gd]
            vhg = vhat[:, lo:lo + gd]
            vm = jnp.dot(wsc[gi], vbg, preferred_element_type=F32) + b_ref[:, lo:lo + gd]
            t = dyg * (zg * sig)
            d_ref[:, lo:lo + gd] = (t * vm * _gelu_grad(u_pre)).astype(BF16)
            dvm = t * ug
            d_ref[:, 2 * e + lo:2 * e + lo + gd] = (dyg * ug * vm * (sig * (1.0 + zg * (1.0 - sig)))).astype(BF16)
            dvm_b = dvm.astype(BF16)
            dv = jnp.dot(wtsc[gi], dvm_b, preferred_element_type=F32)
            dw_ref[gi] += _dot_nt(dvm_b, vbg)
            dbacc[:, lo:lo + gd] += dvm
            dg_ref[:, lo:lo + gd] += jnp.sum(dv * vhg, axis=0, keepdims=True)
            dvh = dv * gv[:, lo:lo + gd]
            dvh_sc[:, lo:lo + gd] = dvh
            s1 = s1 + jnp.sum(dvh, axis=-1, keepdims=True)
            s2 = s2 + jnp.sum(dvh * vhg, axis=-1, keepdims=True)
        dvg = rstd * (dvh_sc[...] - s1 * (1.0 / e) - vhat * (s2 * (1.0 / e)))
        d_ref[:, e:2 * e] = (dvg * _gelu_grad(v_pre)).astype(BF16)

        @pl.when(n == nb - 1)
        def _():
            msk = _sgu_mask()
            for gi in range(SGU_GROUPS):
                dw_ref[gi] = jnp.where(msk, dw_ref[gi], 0.0)
                db_ref[gi] = jnp.sum(dbacc[:, gi * gd:(gi + 1) * gd], axis=1, keepdims=True)

    return _pcall(
        body, name=name, grid=(nb,),
        in_specs=[pl.BlockSpec((SGU_BLOCK, e3), lambda n: (n, 0)),
                  pl.BlockSpec((SGU_BLOCK, e), lambda n: (n, 0)),
                  pl.BlockSpec((1, e), lambda n: (0, 0)),
                  pl.BlockSpec((SGU_GROUPS, SGU_BLOCK, SGU_BLOCK), lambda n: (0, 0, 0)),
                  pl.BlockSpec((SGU_BLOCK, e), lambda n: (0, 0))],
        out_specs=[pl.BlockSpec((SGU_BLOCK, e3), lambda n: (n, 0)),
                   pl.BlockSpec((SGU_GROUPS, SGU_BLOCK, SGU_BLOCK), lambda n: (0, 0, 0)),
                   pl.BlockSpec((SGU_GROUPS, SGU_BLOCK, 1), lambda n: (0, 0, 0)),
                   pl.BlockSpec((1, e), lambda n: (0, 0))],
        out_shape=[jax.ShapeDtypeStruct((s, e3), BF16),
                   jax.ShapeDtypeStruct((SGU_GROUPS, SGU_BLOCK, SGU_BLOCK), F32),
                   jax.ShapeDtypeStruct((SGU_GROUPS, SGU_BLOCK, 1), F32),
                   jax.ShapeDtypeStruct((1, e), F32)],
        scratch_shapes=[pltpu.VMEM((SGU_GROUPS, SGU_BLOCK, SGU_BLOCK), BF16),
                        pltpu.VMEM((SGU_GROUPS, SGU_BLOCK, SGU_BLOCK), BF16),
                        pltpu.VMEM((SGU_BLOCK, e), F32),
                        pltpu.VMEM((SGU_BLOCK, e), F32)],
        compiler_params=_params(("arbitrary",)),
    )(uvz, dy, norm_g, w_s, bias_full)


def _rope_tables(s):
    pos = jnp.arange(s, dtype=F32)
    inv_freq = ROPE_THETA ** (-jnp.arange(0, ROPE, 2, dtype=F32) / ROPE)
    ang = pos[:, None] * inv_freq[None, :]
    cos, sin = jnp.cos(ang), jnp.sin(ang)
    z32 = jnp.zeros((s, HALF), F32)
    z64 = jnp.zeros((s, ROPE), F32)
    ck = jnp.concatenate([cos, cos, z64], axis=1)
    s1k = jnp.concatenate([-sin, z32, z64], axis=1)
    s2k = jnp.concatenate([z32, sin, z64], axis=1)
    return ck, s1k, s2k


def _rot(x, c, s1, s2):
    w = x.shape[-1]
    return x * c + pltpu.roll(x, w - HALF, 1) * s1 + pltpu.roll(x, HALF, 1) * s2


def _rms(cv, n_real):
    r = lax.rsqrt(jnp.sum(cv * cv, axis=-1, keepdims=True) * (1.0 / n_real) + NORM_EPS)
    return r, cv * r


def _mla_norm_fwd(proj, gq, gkv, tabs, name):
    s = proj.shape[0]
    ts = _pick(s, ROW_BLK)
    ck, s1k, s2k = tabs

    def body(cq_ref, ckv_ref, kr_ref, gq_ref, gkv_ref, c_ref, s1_ref, s2_ref, qn_ref, kvn_ref, kro_ref):
        _, nq = _rms(cq_ref[...], Q_RANK)
        qn_ref[...] = (nq * gq_ref[...]).astype(BF16)
        _, nkv = _rms(ckv_ref[...], KV_RANK)
        kvn_ref[...] = (nkv * gkv_ref[...]).astype(BF16)
        kro_ref[...] = _rot(kr_ref[...], c_ref[...], s1_ref[...], s2_ref[...]).astype(BF16)

    tab = pl.BlockSpec((ts, LANE), lambda i: (i, 0))
    return _pcall(
        body, name=name, grid=(s // ts,),
        in_specs=[pl.BlockSpec((ts, Q_RANK_PAD), lambda i: (i, 0)),
                  pl.BlockSpec((ts, KV_RANK), lambda i: (i, PROJ_CKV // KV_RANK)),
                  pl.BlockSpec((ts, LANE), lambda i: (i, PROJ_KR // LANE)),
                  _vec_spec(Q_RANK_PAD), _vec_spec(KV_RANK), tab, tab, tab],
        out_specs=[pl.BlockSpec((ts, Q_RANK_PAD), lambda i: (i, 0)),
                   pl.BlockSpec((ts, KV_RANK), lambda i: (i, 0)), tab],
        out_shape=[jax.ShapeDtypeStruct((s, Q_RANK_PAD), BF16), jax.ShapeDtypeStruct((s, KV_RANK), BF16),
                   jax.ShapeDtypeStruct((s, LANE), BF16)],
        compiler_params=_params(("parallel",)),
    )(proj, proj, proj, gq, gkv, ck, s1k, s2k)


def _transpose_bf16(t):
    return t.astype(F32).T.astype(BF16)


def _diag_mask(tb, transposed):
    r = lax.broadcasted_iota(jnp.int32, (tb, tb), 0) // CHUNK
    c = lax.broadcasted_iota(jnp.int32, (tb, tb), 1) // CHUNK
    return (r <= c) if transposed else (c <= r)


def _attn_fwd(q, kv, kr, proj, tabs, name):
    s = q.shape[0]
    tb = _pick(s, ATT_BLK)
    nb = s // tb
    zcol = PROJ_Z // V_DIM
    mult = SOFTMAX_SCALE * LOG2_E

    hp = ATT_HEADS_PER_STEP

    def body(qf_ref, c_ref, s1_ref, s2_ref, kv_ref, kr_ref, *rest):
        z_refs = rest[:hp]
        o_ref, y_ref, lse_ref, q_ref = rest[hp:hp + 4]
        scratch = rest[hp + 4:]
        kt_sc, vx_sc, m_sc, acc_sc, sa_sc, sb_sc = (scratch[i * hp:(i + 1) * hp] for i in range(6))
        qi = pl.program_id(1)
        heads = range(hp)
        for hh in heads:
            lo = hh * HEAD_PAD
            q_ref[:, lo:lo + NOPE] = (qf_ref[:, lo:lo + NOPE] * mult).astype(BF16)
            q_ref[:, lo + NOPE:lo + HEAD_PAD] = (
                _rot(qf_ref[:, lo + NOPE:lo + HEAD_PAD], c_ref[...], s1_ref[...], s2_ref[...]) * mult).astype(BF16)

        @pl.when(qi == 0)
        def _():
            for hh in heads:
                kcol = hh * (NOPE + V_DIM)
                for b in range(nb):
                    rows = slice(b * tb, (b + 1) * tb)
                    kt_sc[hh][b] = _transpose_bf16(
                        jnp.concatenate([kv_ref[rows, kcol:kcol + NOPE], kr_ref[rows, :]], axis=1))
                    vx_sc[hh][b] = jnp.concatenate(
                        [kv_ref[rows, kcol + NOPE:kcol + NOPE + V_DIM], jnp.ones((tb, V_DIM), BF16)], axis=1)

        for hh in heads:
            m_sc[hh][...] = jnp.full_like(m_sc[hh], -1e30)
            acc_sc[hh][...] = jnp.zeros_like(acc_sc[hh])
        sub = min(tb, ATT_SUB)

        def scores(ki, bufs):
            for hh in heads:
                bufs[hh][...] = jnp.dot(q_ref[:, hh * HEAD_PAD:(hh + 1) * HEAD_PAD], kt_sc[hh][ki],
                                        preferred_element_type=F32)

        def step(ki, bufs, masked):
            for r in range(tb // sub):
                rs = slice(r * sub, (r + 1) * sub)
                for hh in heads:
                    sc = bufs[hh][rs, :]
                    if masked:
                        sc = jnp.where(_diag_mask(tb, False)[rs, :], sc, -1e30)
                    m_prev = m_sc[hh][rs, :]
                    m_new = jnp.maximum(m_prev, jnp.max(sc, axis=-1, keepdims=True))
                    p = jnp.exp2(sc - m_new).astype(BF16)
                    acc_sc[hh][rs, :] = (jnp.exp2(m_prev - m_new) * acc_sc[hh][rs, :]
                                         + jnp.dot(p, vx_sc[hh][ki], preferred_element_type=F32))
                    m_sc[hh][rs, :] = m_new

        def pair(t, carry):
            scores(2 * t + 1, sb_sc)
            step(2 * t, sa_sc, False)
            scores(2 * t + 2, sa_sc)
            step(2 * t + 1, sb_sc, False)
            return carry

        scores(0, sa_sc)
        lax.fori_loop(0, qi // 2, pair, 0)

        @pl.when(qi % 2 == 1)
        def _():
            scores(qi, sb_sc)
            step(qi - 1, sa_sc, False)
            step(qi, sb_sc, True)

        @pl.when(qi % 2 == 0)
        def _():
            step(qi, sa_sc, True)

        for hh in heads:
            l = acc_sc[hh][:, V_DIM:V_DIM + 1]
            o = acc_sc[hh][:, :V_DIM] / l
            z = z_refs[hh][...]
            o_ref[:, hh * V_DIM:(hh + 1) * V_DIM] = o.astype(BF16)
            y_ref[:, hh * V_DIM:(hh + 1) * V_DIM] = (o * (z * _sigmoid(z))).astype(BF16)
            lse_cols = jnp.broadcast_to(m_sc[hh][...] + jnp.log2(l), (tb, LANE))
            lse_ref[hh] = lse_cols.T[0:1, :]

    oblk = pl.BlockSpec((tb, hp * V_DIM), lambda g, qi: (qi, g))
    qblk = pl.BlockSpec((tb, hp * HEAD_PAD), lambda g, qi: (qi, g))
    tab = pl.BlockSpec((tb, LANE), lambda g, qi: (qi, 0))
    per_head = lambda shape, dtype: [pltpu.VMEM(shape, dtype)] * hp
    return _pcall(
        body, name=name, grid=(HEADS // hp, nb),
        in_specs=[qblk, tab, tab, tab,
                  pl.BlockSpec((s, hp * (NOPE + V_DIM)), lambda g, qi: (0, g)),
                  pl.BlockSpec((s, LANE), lambda g, qi: (0, 0))]
                 + [pl.BlockSpec((tb, V_DIM), functools.partial(lambda g, qi, hh: (qi, zcol + hp * g + hh), hh=hh))
                    for hh in range(hp)],
        out_specs=[oblk, oblk, pl.BlockSpec((hp, None, 1, tb), lambda g, qi: (g, qi, 0, 0)), qblk],
        out_shape=[jax.ShapeDtypeStruct((s, MLA_WIDTH), BF16), jax.ShapeDtypeStruct((s, MLA_WIDTH), BF16),
                   jax.ShapeDtypeStruct((HEADS, nb, 1, tb), F32), jax.ShapeDtypeStruct((s, HEADS * HEAD_PAD), BF16)],
        scratch_shapes=(per_head((nb, HEAD_PAD, tb), BF16) + per_head((nb, tb, HEAD_PAD), BF16)
                        + per_head((tb, 1), F32) + per_head((tb, HEAD_PAD), F32)
                        + per_head((tb, tb), F32) + per_head((tb, tb), F32)),
        compiler_params=_params(("parallel", "arbitrary")),
    )(q, *tabs, kv, kr, *([proj] * hp))


def _attn_bwd(q_cat, kv, kr, do, o, lse, tabs, name):
    s = q_cat.shape[0]
    tb = _pick(s, ATT_BLK)
    nb = s // tb
    ln2 = math.log(2.0)

    hp = ATT_HEADS_PER_STEP
    heads = range(hp)

    def body(q_ref, do_ref, o_ref, lse_ref, kv_ref, kr_ref, c_ref, s1_ref, s2_ref, dq_ref, dkv_ref, dkr_ref, *scratch):
        qt_sc, dot_sc, delta_sc, dqt_sc, dk_sc, dv_sc = (scratch[i * hp:(i + 1) * hp] for i in range(6))
        ki = pl.program_id(1)
        qcols = lambda hh: slice(hh * HEAD_PAD, (hh + 1) * HEAD_PAD)
        vcols = lambda hh: slice(hh * V_DIM, (hh + 1) * V_DIM)

        @pl.when(ki == 0)
        def _():
            for hh in heads:
                for b in range(nb):
                    rows = slice(b * tb, (b + 1) * tb)
                    qt_sc[hh][b] = _transpose_bf16(q_ref[rows, qcols(hh)])
                    do_t = do_ref[rows, vcols(hh)].astype(F32).T
                    dot_sc[hh][b] = do_t.astype(BF16)
                    delta_sc[hh][b] = jnp.sum(do_t * o_ref[rows, vcols(hh)].astype(F32).T, axis=0, keepdims=True)
                dqt_sc[hh][...] = jnp.zeros_like(dqt_sc[hh])

        k, kt, vb = [], [], []
        for hh in heads:
            kcol = hh * (NOPE + V_DIM)
            k.append(jnp.concatenate([kv_ref[:, kcol:kcol + NOPE], kr_ref[...]], axis=1))
            kt.append(_transpose_bf16(k[hh]))
            vb.append(kv_ref[:, kcol + NOPE:kcol + NOPE + V_DIM])
            dk_sc[hh][...] = jnp.zeros_like(dk_sc[hh])
            dv_sc[hh][...] = jnp.zeros_like(dv_sc[hh])

        def step(qi, masked):
            rows = pl.ds(pl.multiple_of(qi * tb, tb), tb)
            for hh in heads:
                sc_t = jnp.dot(k[hh], qt_sc[hh][qi], preferred_element_type=F32)
                if masked:
                    sc_t = jnp.where(_diag_mask(tb, True), sc_t, -1e30)
                p_t = jnp.exp2(sc_t - lse_ref[hh, qi])
                dp_t = jnp.dot(vb[hh], dot_sc[hh][qi], preferred_element_type=F32)
                ds_t = (p_t * (dp_t - delta_sc[hh][qi])).astype(BF16)
                dv_sc[hh][...] += jnp.dot(p_t.astype(BF16), do_ref[rows, vcols(hh)], preferred_element_type=F32)
                dk_sc[hh][...] += jnp.dot(ds_t, q_ref[rows, qcols(hh)], preferred_element_type=F32)
                dqt_sc[hh][qi] += jnp.dot(kt[hh], ds_t, preferred_element_type=F32)

        step(ki, True)

        def loop_body(qi, carry):
            step(qi, False)
            return carry

        lax.fori_loop(ki + 1, nb, loop_body, 0)

        for hh in heads:
            lo = hh * HEAD_PAD
            dkv_ref[:, lo:lo + NOPE] = (dk_sc[hh][:, :NOPE] * ln2).astype(BF16)
            dkv_ref[:, lo + NOPE:lo + HEAD_PAD] = dv_sc[hh][...].astype(BF16)
            dkr_ref[hh] = dk_sc[hh][:, NOPE:] * ln2

        @pl.when(ki == nb - 1)
        def _():
            for hh in heads:
                lo = hh * HEAD_PAD
                for b in range(nb):
                    rows = slice(b * tb, (b + 1) * tb)
                    dq = dqt_sc[hh][b].T
                    dq_ref[rows, lo:lo + NOPE] = (dq[:, :NOPE] * SOFTMAX_SCALE).astype(BF16)
                    dq_ref[rows, lo + NOPE:lo + HEAD_PAD] = (
                        _rot(dq[:, NOPE:], c_ref[rows, :], -s1_ref[rows, :], -s2_ref[rows, :])
                        * SOFTMAX_SCALE).astype(BF16)

    tab = pl.BlockSpec((s, LANE), lambda g, ki: (0, 0), pipeline_mode=pl.Buffered(1))
    per_head = lambda shape, dtype: [pltpu.VMEM(shape, dtype)] * hp
    return _pcall(
        body, name=name, grid=(HEADS // hp, nb),
        in_specs=[pl.BlockSpec((s, hp * HEAD_PAD), lambda g, ki: (0, g)),
                  pl.BlockSpec((s, hp * V_DIM), lambda g, ki: (0, g)),
                  pl.BlockSpec((s, hp * V_DIM), lambda g, ki: (0, g)),
                  pl.BlockSpec((hp, nb, 1, tb), lambda g, ki: (g, 0, 0, 0)),
                  pl.BlockSpec((tb, hp * (NOPE + V_DIM)), lambda g, ki: (ki, g)),
                  pl.BlockSpec((tb, LANE), lambda g, ki: (ki, 0)), tab, tab, tab],
        out_specs=[pl.BlockSpec((s, hp * HEAD_PAD), lambda g, ki: (0, g)),
                   pl.BlockSpec((tb, hp * HEAD_PAD), lambda g, ki: (ki, g)),
                   pl.BlockSpec((hp, tb, LANE), lambda g, ki: (g, ki, 0))],
        out_shape=[jax.ShapeDtypeStruct((s, HEADS * HEAD_PAD), BF16),
                   jax.ShapeDtypeStruct((s, HEADS * HEAD_PAD), BF16),
                   jax.ShapeDtypeStruct((HEADS, s, LANE), F32)],
        scratch_shapes=(per_head((nb, HEAD_PAD, tb), BF16) + per_head((nb, V_DIM, tb), BF16)
                        + per_head((nb, 1, tb), F32) + per_head((nb, HEAD_PAD, tb), F32)
                        + per_head((tb, HEAD_PAD), F32) + per_head((tb, V_DIM), F32)),
        compiler_params=_params(("parallel", "arbitrary")),
    )(q_cat, do, o, lse, kv, kr, *tabs)


def _mla_gate_bwd(dy, o, proj, name):
    s = dy.shape[0]
    ts = _pick(s, ROW_BLK)

    def body(dy_ref, o_ref, p_ref, do_ref, dp_ref):
        z = p_ref[:, PROJ_Z:]
        dyv = dy_ref[...]
        sig = _sigmoid(z)
        do_ref[...] = (dyv * (z * sig)).astype(BF16)
        dp_ref[:, :PROJ_Z] = jnp.zeros((ts, PROJ_Z), BF16)
        dp_ref[:, PROJ_Z:] = (dyv * o_ref[...].astype(F32) * (sig * (1.0 + z * (1.0 - sig)))).astype(BF16)

    blk = pl.BlockSpec((ts, MLA_WIDTH), lambda i: (i, 0))
    wide = pl.BlockSpec((ts, PROJ_W), lambda i: (i, 0))
    return _pcall(
        body, name=name, grid=(s // ts,),
        in_specs=[blk, blk, wide],
        out_specs=[blk, wide],
        out_shape=[jax.ShapeDtypeStruct((s, MLA_WIDTH), BF16), jax.ShapeDtypeStruct((s, PROJ_W), BF16)],
        compiler_params=_params(("parallel",)),
    )(dy, o, proj)


def _mla_norm_bwd(dqn, dkvn, dkr_heads, proj, gq, gkv, tabs, dproj, name):
    s = proj.shape[0]
    ts = _pick(s, ROW_BLK)
    ck, s1k, s2k = tabs

    def rms_bwd(cv, dn_in, g, n_real):
        r, n = _rms(cv, n_real)
        dg = jnp.sum(dn_in * n, axis=0, keepdims=True)
        dn = dn_in * g
        dc = r * (dn - n * (jnp.sum(dn * n, axis=-1, keepdims=True) * (1.0 / n_real)))
        return dc, dg

    def body(dqn_ref, dkvn_ref, dkr_ref, cq_ref, ckv_ref, gq_ref, gkv_ref, c_ref, s1_ref, s2_ref, dp_in_ref,
             dp_ref, dgq_ref, dgkv_ref):
        @pl.when(pl.program_id(0) == 0)
        def _():
            dgq_ref[...] = jnp.zeros_like(dgq_ref)
            dgkv_ref[...] = jnp.zeros_like(dgkv_ref)

        dcq, dgq = rms_bwd(cq_ref[...], dqn_ref[...], gq_ref[...], Q_RANK)
        dckv, dgkv = rms_bwd(ckv_ref[...], dkvn_ref[...], gkv_ref[...], KV_RANK)
        dgq_ref[...] += dgq
        dgkv_ref[...] += dgkv
        dkr = dkr_ref[0]
        for h in range(1, HEADS):
            dkr = dkr + dkr_ref[h]
        dp_ref[:, PROJ_CQ:PROJ_CKV] = dcq.astype(BF16)
        dp_ref[:, PROJ_CKV:PROJ_KR] = dckv.astype(BF16)
        dp_ref[:, PROJ_KR:PROJ_Z] = _rot(dkr, c_ref[...], -s1_ref[...], -s2_ref[...]).astype(BF16)

    tab = pl.BlockSpec((ts, LANE), lambda i: (i, 0))
    return _pcall(
        body, name=name, grid=(s // ts,),
        in_specs=[pl.BlockSpec((ts, Q_RANK_PAD), lambda i: (i, 0)),
                  pl.BlockSpec((ts, KV_RANK), lambda i: (i, 0)),
                  pl.BlockSpec((HEADS, ts, LANE), lambda i: (0, i, 0)),
                  pl.BlockSpec((ts, Q_RANK_PAD), lambda i: (i, 0)),
                  pl.BlockSpec((ts, KV_RANK), lambda i: (i, PROJ_CKV // KV_RANK)),
                  _vec_spec(Q_RANK_PAD), _vec_spec(KV_RANK), tab, tab, tab, pl.BlockSpec(memory_space=pl.ANY)],
        out_specs=[pl.BlockSpec((ts, PROJ_Z), lambda i: (i, 0)), _vec_spec(Q_RANK_PAD), _vec_spec(KV_RANK)],
        out_shape=[jax.ShapeDtypeStruct((s, PROJ_W), BF16), jax.ShapeDtypeStruct((1, Q_RANK_PAD), F32),
                   jax.ShapeDtypeStruct((1, KV_RANK), F32)],
        input_output_aliases={10: 0},
        compiler_params=_params(("arbitrary",)),
    )(dqn, dkvn, dkr_heads, proj, proj, gq, gkv, ck, s1k, s2k, dproj)


def _ada_mod(cond_raw, ada_w, bias_my, name):
    nl, d, ncol = ada_w.shape
    tk = _pick(d, 512)
    nk = d // tk

    def body(c_ref, w_ref, b_ref, o_ref, acc_ref):
        kk = pl.program_id(1)

        @pl.when(kk == 0)
        def _():
            acc_ref[...] = jnp.zeros_like(acc_ref)

        cv = c_ref[...]
        cond = (cv * _sigmoid(cv)).astype(BF16)
        acc_ref[...] += jnp.dot(cond, w_ref[...].astype(BF16), preferred_element_type=F32)

        @pl.when(kk == nk - 1)
        def _():
            o_ref[...] = acc_ref[...] + b_ref[...]

    return _pcall(
        body, name=name, grid=(nl, nk),
        in_specs=[pl.BlockSpec((N_DEV, tk), lambda l, kk: (0, kk)),
                  pl.BlockSpec((None, tk, ncol), lambda l, kk: (l, kk, 0)),
                  pl.BlockSpec((None, 1, ncol), lambda l, kk: (l, 0, 0))],
        out_specs=pl.BlockSpec((None, N_DEV, ncol), lambda l, kk: (l, 0, 0)),
        out_shape=jax.ShapeDtypeStruct((nl, N_DEV, ncol), F32),
        scratch_shapes=[pltpu.VMEM((N_DEV, ncol), F32)],
        compiler_params=_params(("parallel", "arbitrary")),
    )(cond_raw, ada_w, bias_my.reshape(nl, 1, ncol))


def _adam(w, g, m, v):
    m = ADAM_B1 * m + (1.0 - ADAM_B1) * g
    v = ADAM_B2 * v + (1.0 - ADAM_B2) * (g * g)
    m_hat = m / (1.0 - ADAM_B1 ** ADAM_STEP)
    v_hat = v / (1.0 - ADAM_B2 ** ADAM_STEP)
    delta = -ADAM_LR * (m_hat / (jnp.sqrt(v_hat) + ADAM_EPS) + ADAM_WD * w)
    return delta, m, v


def _ada_bwd_adam(cond_t, dmod_cols, w, m, v, name):
    nl, d, ncol = w.shape
    tk = _pick(d, 512)

    def body(c_ref, dm_ref, w_ref, m_ref, v_ref, g_ref, d_ref, mo_ref, vo_ref):
        cv = c_ref[...]
        cond = (cv * _sigmoid(cv)).astype(BF16)
        g = jnp.dot(cond, dm_ref[...].astype(BF16), preferred_element_type=F32)
        delta, m2, v2 = _adam(w_ref[...], g, m_ref[...], v_ref[...])
        g_ref[...] = g
        d_ref[...] = delta
        mo_ref[...] = m2
        vo_ref[...] = v2

    blk = pl.BlockSpec((None, tk, ncol), lambda l, kk: (l, kk, 0))
    shp = jax.ShapeDtypeStruct((nl, d, ncol), F32)
    return _pcall(
        body, name=name, grid=(nl, d // tk),
        in_specs=[pl.BlockSpec((tk, N_DEV), lambda l, kk: (kk, 0)),
                  pl.BlockSpec((None, N_DEV, ncol), lambda l, kk: (l, 0, 0)), blk, blk, blk],
        out_specs=[blk, blk, blk, blk], out_shape=[shp, shp, shp, shp],
        compiler_params=_params(("parallel", "parallel")),
    )(cond_t, dmod_cols, w, m, v)


def _adam_reduce(recv0, recv1, w, m, v, name):
    nl, r, c = w.shape
    tr = _pick(r, 128) if r % 128 == 0 else r
    tc = _pick(c, 1024)
    n0, n1 = recv0.shape[0] // r, recv1.shape[0] // r

    def body(r0_ref, r1_ref, w_ref, m_ref, v_ref, g_ref, d_ref, mo_ref, vo_ref):
        l = pl.program_id(0)

        def run(rr):
            g = rr[0].astype(F32)
            for sidx in range(1, rr.shape[0]):
                g = g + rr[sidx].astype(F32)
            delta, m2, v2 = _adam(w_ref[...], g, m_ref[...], v_ref[...])
            g_ref[...] = g
            d_ref[...] = delta
            mo_ref[...] = m2
            vo_ref[...] = v2

        @pl.when(l == 0)
        def _():
            run(r0_ref)

        @pl.when(l == 1)
        def _():
            run(r1_ref)

    def rblk(n, layer):
        return pl.BlockSpec((n, tr, tc), lambda l, i, j: (0, jnp.where(l == layer, i, 0), jnp.where(l == layer, j, 0)))

    blk = pl.BlockSpec((None, tr, tc), lambda l, i, j: (l, i, j))
    shp = jax.ShapeDtypeStruct((nl, r, c), F32)
    return _pcall(
        body, name=name, grid=(nl, r // tr, c // tc),
        in_specs=[rblk(n0, 0), rblk(n1, 1), blk, blk, blk],
        out_specs=[blk, blk, blk, blk], out_shape=[shp, shp, shp, shp],
        compiler_params=_params(("arbitrary", "parallel", "parallel")),
    )(recv0.reshape(n0, r, c), recv1.reshape(n1, r, c), w, m, v)


def _adam_small(gathered, w, m, v, name):
    r = w.shape[0]
    tr = _pick(r, 512) if r % 512 == 0 else r

    def body(p_ref, w_ref, m_ref, v_ref, g_ref, d_ref, mo_ref, vo_ref):
        g = p_ref[0]
        for sidx in range(1, N_DEV):
            g = g + p_ref[sidx]
        delta, m2, v2 = _adam(w_ref[...], g, m_ref[...], v_ref[...])
        g_ref[...] = g
        d_ref[...] = delta
        mo_ref[...] = m2
        vo_ref[...] = v2

    blk = pl.BlockSpec((tr, LANE), lambda i: (i, 0))
    shp = jax.ShapeDtypeStruct((r, LANE), F32)
    return _pcall(
        body, name=name, grid=(r // tr,),
        in_specs=[pl.BlockSpec((N_DEV, tr, LANE), lambda i: (0, i, 0)), blk, blk, blk],
        out_specs=[blk, blk, blk, blk], out_shape=[shp, shp, shp, shp],
        compiler_params=_params(("parallel",)),
    )(gathered, w, m, v)


def _my_place():
    x, y, c = lax.axis_index("x"), lax.axis_index("y"), lax.axis_index("c")
    return x, y, c, 4 * x + 2 * y + c


def _peer(x, y, c, k):
    px = 1 - x if (k >> 2) & 1 else x
    py = 1 - y if (k >> 1) & 1 else y
    pc = 1 - c if k & 1 else c
    return (px, py, pc), 4 * px + 2 * py + pc


def _slab(ref, shape, kind, p):
    r, cd = shape
    if kind == "row":
        return ref.at[pl.ds(pl.multiple_of(p * r, SUBLANE), r), :]
    return ref.at[:, pl.ds(pl.multiple_of(p * cd, LANE), cd)]


def _exchange_layout(arrays, kinds, gather):
    shard_shapes, dst_kinds, out_shapes = [], [], []
    for a, kind in zip(arrays, kinds):
        r, cd = a.shape
        if gather:
            shard, dst_kind = (r, cd), kind
        else:
            shard, dst_kind = ((r // N_DEV, cd) if kind == "row" else (r, cd // N_DEV)), "row"
        shard_shapes.append(shard)
        dst_kinds.append(dst_kind)
        full = (shard[0] * N_DEV, shard[1]) if dst_kind == "row" else (shard[0], shard[1] * N_DEV)
        out_shapes.append(jax.ShapeDtypeStruct(full, a.dtype))
    return shard_shapes, dst_kinds, out_shapes


def _exchange_copies(ins, outs, send_sems, recv_sems, sem_of, layout, kinds, gather):
    shard_shapes, dst_kinds, _ = layout
    x, y, c, me = _my_place()

    def src_for(a, p):
        return ins[a] if gather else _slab(ins[a], shard_shapes[a], kinds[a], p)

    def dst_slot(a, p):
        return _slab(outs[a], shard_shapes[a], dst_kinds[a], p)

    def remote(a, k, slot):
        peer, pidx = _peer(x, y, c, k)
        return pltpu.make_async_remote_copy(
            src_ref=src_for(a, pidx), dst_ref=dst_slot(a, me if slot == "mine" else pidx),
            send_sem=send_sems.at[sem_of(a, k)], recv_sem=recv_sems.at[sem_of(a, k)],
            device_id=peer, device_id_type=MESH)

    return remote


def _place_own(src, src_kind, slab_shape, dst_kind, full, name, index=None):
    r, cd = slab_shape
    tr = _pick(r, 512)
    nr = r // tr
    me = _my_place()[3] if index is None else index
    src, layer = src if isinstance(src, tuple) else (src, None)

    def body(me_ref, s_ref, o_ref):
        o_ref[...] = s_ref[...].astype(o_ref.dtype)

    def where(kind):
        if kind is None:
            return lambda i, me_ref: (i, 0)
        if kind == "row":
            return lambda i, me_ref: (me_ref[0] * nr + i, 0)
        return lambda i, me_ref: (i, me_ref[0])

    if layer is None:
        src_spec = pl.BlockSpec((tr, cd), where(src_kind))
    else:
        src_spec = pl.BlockSpec((None, tr, cd), lambda i, me_ref: (layer, *where(src_kind)(i, me_ref)))
    return _pcall(
        body, name=name,
        grid_spec=pltpu.PrefetchScalarGridSpec(
            num_scalar_prefetch=1, grid=(nr,),
            in_specs=[src_spec],
            out_specs=pl.BlockSpec((tr, cd), where(dst_kind))),
        out_shape=jax.ShapeDtypeStruct(full.shape, full.dtype),
        compiler_params=_params(("arbitrary",)),
    )(jnp.reshape(me, (1,)).astype(jnp.int32), src)


def _landing_with_own_slab(arrays, kinds, gather, layout, name=None, order_after=None):
    _, _, _, me = _my_place()
    index = None
    if order_after is not None:
        first = order_after.reshape(-1)[0].astype(jnp.int32)
        index = me + jnp.minimum(jnp.maximum(first, 0), 0)
    lands = []
    for a in range(len(arrays)):
        (r, cd), dst_kind, full = layout[0][a], layout[1][a], layout[2][a]
        if name is not None:
            lands.append(_place_own(arrays[a], None if gather else kinds[a], (r, cd), dst_kind, full, name,
                                    index=index))
            continue
        if gather:
            piece = arrays[a]
        elif kinds[a] == "row":
            piece = lax.dynamic_slice_in_dim(arrays[a], me * r, r, axis=0)
        else:
            piece = lax.dynamic_slice_in_dim(arrays[a], me * cd, cd, axis=1)
        at = (me * r, 0) if dst_kind == "row" else (0, me * cd)
        lands.append(lax.dynamic_update_slice(lax.empty(full.shape, full.dtype), piece, at))
    return lands


def _exchange(arrays, kinds, gather, name, after=None):
    n = len(arrays)
    n_extra = 0 if after is None else 1
    layout = _exchange_layout(arrays, kinds, gather)
    lands = _landing_with_own_slab(arrays, kinds, gather, layout)

    def body(*refs):
        ins, outs = refs[:n], refs[2 * n + n_extra:3 * n + n_extra]
        send_sems, recv_sems = refs[3 * n + n_extra:]
        remote = _exchange_copies(ins, outs, send_sems, recv_sems,
                                  lambda a, k: a * (N_DEV - 1) + k - 1, layout, kinds, gather)
        for a in range(n):
            for k in range(1, N_DEV):
                remote(a, k, "mine").start()
        for a in range(n):
            for k in range(1, N_DEV):
                arrival = remote(a, k, "theirs")
                arrival.wait_send()
                arrival.wait_recv()

    anyspec = pl.BlockSpec(memory_space=pl.ANY)
    outs = _pcall(
        body, comm=True, name=name,
        in_specs=[anyspec] * (2 * n + n_extra), out_specs=[anyspec] * n, out_shape=layout[2],
        input_output_aliases={n + a: a for a in range(n)},
        scratch_shapes=[pltpu.SemaphoreType.DMA((n * (N_DEV - 1),)), pltpu.SemaphoreType.DMA((n * (N_DEV - 1),))],
    )(*arrays, *lands, *([] if after is None else [after]))
    return list(outs)


HBM_SPEC = pl.BlockSpec(memory_space=pltpu.HBM)
SEM_SPEC = pl.BlockSpec(memory_space=pltpu.SEMAPHORE)
ANY_SPEC = pl.BlockSpec(memory_space=pl.ANY)
DATAFLOW = pltpu.SideEffectType.DATAFLOW_SIDE_EFFECTING


def _exchange_start(arrays, kinds, gather, name, after, carry=()):
    n, nc = len(arrays), len(carry)
    layout = _exchange_layout(arrays, kinds, gather)
    lands = _landing_with_own_slab(arrays, kinds, gather, layout, "place_own")

    def body(*refs):
        ins, outs = refs[:n], refs[n:2 * n]
        send_sems, recv_sems = refs[2 * n + nc + 1], refs[2 * n + nc + 2]
        token = refs[2 * n + nc + 3 + 2 * n + nc]
        remote = _exchange_copies(ins, outs, send_sems, recv_sems, lambda a, k: a, layout, kinds, gather)
        for a in range(n):
            for k in range(1, N_DEV):
                remote(a, k, "mine").start()
        token[...] = jnp.zeros_like(token)

    passed = list(arrays) + lands + list(carry)
    res = pl.pallas_call(
        body, name=name,
        out_shape=(pltpu.SemaphoreType.DMA((n,)), pltpu.SemaphoreType.DMA((n,)),
                   *[pltpu.HBM(t.shape, t.dtype) for t in passed], jax.ShapeDtypeStruct((SUBLANE, LANE), F32)),
        in_specs=[HBM_SPEC] * (2 * n + nc) + [ANY_SPEC],
        out_specs=(SEM_SPEC, SEM_SPEC, *([HBM_SPEC] * (2 * n + nc)), pl.BlockSpec(memory_space=pltpu.VMEM)),
        input_output_aliases={i: 2 + i for i in range(2 * n + nc)},
        compiler_params=pltpu.CompilerParams(has_side_effects=DATAFLOW),
    )(*[pltpu.with_memory_space_constraint(t, pltpu.HBM) for t in passed], after)
    handle = (res[0], res[1], list(res[2:2 + n]), list(res[2 + n:2 + 2 * n]), tuple(kinds), gather)
    return handle, res[-1], list(res[2 + 2 * n:2 + 2 * n + nc])


def _exchange_wait(handle, name, after):
    send_sems, recv_sems, ins_thru, lands_thru, kinds, gather = handle
    n = len(ins_thru)
    layout = _exchange_layout(ins_thru, kinds, gather)

    def body(*refs):
        ins, outs = refs[:n], refs[n:2 * n]
        s_sems, r_sems = refs[2 * n], refs[2 * n + 1]
        remote = _exchange_copies(ins, outs, s_sems, r_sems, lambda a, k: a, layout, kinds, gather)
        for a in range(n):
            for k in range(1, N_DEV):
                arrival = remote(a, k, "theirs")
                arrival.wait_send()
                arrival.wait_recv()

    res = pl.pallas_call(
        body, name=name,
        out_shape=[pltpu.HBM(t.shape, t.dtype) for t in ins_thru + lands_thru],
        in_specs=[HBM_SPEC] * (2 * n) + [SEM_SPEC, SEM_SPEC, ANY_SPEC],
        out_specs=[HBM_SPEC] * (2 * n),
        input_output_aliases={i: i for i in range(2 * n)},
        compiler_params=pltpu.CompilerParams(has_side_effects=DATAFLOW),
    )(*ins_thru, *lands_thru, send_sems, recv_sems, after)
    return list(res[n:2 * n])


STAGE1_KS = (1, 2, 4, 6)
FORWARD_KS = (2, 4, 6)


def _gather2_copies(lands, shard_shapes, kinds):
    x, y, c, me = _my_place()

    def slab(a, p):
        return _slab(lands[a], shard_shapes[a], kinds[a], p)

    def stage1(a, k, sems, arriving):
        peer, pidx = _peer(x, y, c, k)
        s = slab(a, pidx if arriving else me)
        return pltpu.make_async_remote_copy(src_ref=s, dst_ref=s, send_sem=sems[0].at[a], recv_sem=sems[1].at[a],
                                            device_id=peer, device_id_type=MESH)

    def stage2(a, k, sems, arriving):
        sib, _ = _peer(x, y, c, 1)
        _, mine = _peer(x, y, c, k)
        _, theirs = _peer(x, y, 1 - c, k)
        s = slab(a, theirs if arriving else mine)
        return pltpu.make_async_remote_copy(src_ref=s, dst_ref=s, send_sem=sems[0].at[a], recv_sem=sems[1].at[a],
                                            device_id=sib, device_id_type=MESH)

    return stage1, stage2


def _gather2_call(lands, sems_in, name, after, make_body, returns_sems):
    n = len(lands)
    n_in = len(sems_in)

    def body(*refs):
        land_refs = refs[:n]
        in_sems = refs[n:n + n_in]
        rest = refs[n + n_in + 1:]
        out_sems = rest[:2] if returns_sems else ()
        make_body(land_refs, in_sems, out_sems)
        if returns_sems:
            token = rest[2 + n]
            token[...] = jnp.zeros_like(token)

    sem_shapes = (pltpu.SemaphoreType.DMA((n,)), pltpu.SemaphoreType.DMA((n,))) if returns_sems else ()
    tok_shape = (jax.ShapeDtypeStruct((SUBLANE, LANE), F32),) if returns_sems else ()
    n_sem_out = len(sem_shapes)
    res = pl.pallas_call(
        body, name=name,
        out_shape=(*sem_shapes, *[pltpu.HBM(t.shape, t.dtype) for t in lands], *tok_shape),
        in_specs=[HBM_SPEC] * n + [SEM_SPEC] * n_in + [ANY_SPEC],
        out_specs=(*([SEM_SPEC] * n_sem_out), *([HBM_SPEC] * n),
                   *([pl.BlockSpec(memory_space=pltpu.VMEM)] if returns_sems else [])),
        input_output_aliases={i: n_sem_out + i for i in range(n)},
        compiler_params=pltpu.CompilerParams(has_side_effects=DATAFLOW),
    )(*[pltpu.with_memory_space_constraint(t, pltpu.HBM) for t in lands], *sems_in, after)
    sems_out = tuple(res[:n_sem_out])
    lands_thru = list(res[n_sem_out:n_sem_out + n])
    return sems_out, lands_thru, (res[-1] if returns_sems else None)


def _gather2_start(shards, kinds, name, after):
    n = len(shards)
    views = [jax.ShapeDtypeStruct(t.shape[1:], t.dtype) for t, _ in shards]
    shard_shapes, dst_kinds, fulls = _exchange_layout(views, kinds, True)
    layout = (shard_shapes, dst_kinds, [jax.ShapeDtypeStruct(f.shape, BF16) for f in fulls])
    lands = _landing_with_own_slab(shards, kinds, True, layout, "place_own", order_after=after)

    def make_body(land_refs, in_sems, out_sems):
        stage1, _ = _gather2_copies(land_refs, layout[0], kinds)
        for a in range(n):
            for k in STAGE1_KS:
                stage1(a, k, out_sems, False).start()

    sems, lands, token = _gather2_call(lands, (), name, after, make_body, True)
    return (sems, lands, layout[0], tuple(kinds)), token


def _gather2_forward(handle, name, after, carry=()):
    sems1, lands, shard_shapes, kinds = handle
    n = len(lands)

    def make_body(land_refs, in_sems, out_sems):
        stage1, stage2 = _gather2_copies(land_refs, shard_shapes, kinds)
        for a in range(n):
            for k in STAGE1_KS:
                arrival = stage1(a, k, in_sems, True)
                arrival.wait_send()
                arrival.wait_recv()
        for a in range(n):
            for k in FORWARD_KS:
                stage2(a, k, out_sems, False).start()

    sems2, passed, token = _gather2_call(list(lands) + list(carry), sems1, name, after, make_body, True)
    return (sems2, passed[:n], shard_shapes, kinds), token, passed[n:]


def _gather2_wait(handle, name, after):
    sems2, lands, shard_shapes, kinds = handle
    n = len(lands)

    def make_body(land_refs, in_sems, out_sems):
        _, stage2 = _gather2_copies(land_refs, shard_shapes, kinds)
        for a in range(n):
            for k in FORWARD_KS:
                arrival = stage2(a, k, in_sems, True)
                arrival.wait_send()
                arrival.wait_recv()

    _, lands, _ = _gather2_call(lands, sems2, name, after, make_body, False)
    return lands


N_CHIP = N_DEV // 2


def _scatter2_pair(g, name):
    r, c8 = g.shape
    cd = c8 // N_DEV

    def body(g_ref, o_ref, send_sems, recv_sems):
        x, y, c, _ = _my_place()
        sib, _ = _peer(x, y, c, 1)

        def copy(ch):
            theirs = 2 * ch + (1 - c)
            return pltpu.make_async_remote_copy(
                src_ref=_slab(g_ref, (r, cd), "col", theirs), dst_ref=_slab(o_ref, (r, cd), "row", ch),
                send_sem=send_sems.at[ch], recv_sem=recv_sems.at[ch], device_id=sib, device_id_type=MESH)

        for ch in range(N_CHIP):
            copy(ch).start()
        for ch in range(N_CHIP):
            copy(ch).wait_send()
            copy(ch).wait_recv()

    return _pcall(
        body, comm=True, name=name,
        in_specs=[ANY_SPEC], out_specs=ANY_SPEC, out_shape=jax.ShapeDtypeStruct((N_CHIP * r, cd), g.dtype),
        scratch_shapes=[pltpu.SemaphoreType.DMA((N_CHIP,)), pltpu.SemaphoreType.DMA((N_CHIP,))],
    )(g)


def _scatter2_add(g, from_sibling, name):
    r, c8 = g.shape
    cd = c8 // N_DEV
    tr = _pick(r, 512)
    nr = r // tr
    _, _, core, _ = _my_place()

    def body(c_ref, g_ref, s_ref, o_ref):
        o_ref[...] = (g_ref[...].astype(F32) + s_ref[...].astype(F32)).astype(o_ref.dtype)

    return _pcall(
        body, name=name,
        grid_spec=pltpu.PrefetchScalarGridSpec(
            num_scalar_prefetch=1, grid=(N_CHIP, nr),
            in_specs=[pl.BlockSpec((tr, cd), lambda ch, i, c_ref: (i, 2 * ch + c_ref[0])),
                      pl.BlockSpec((tr, cd), lambda ch, i, c_ref: (ch * nr + i, 0))],
            out_specs=pl.BlockSpec((tr, cd), lambda ch, i, c_ref: (ch * nr + i, 0))),
        out_shape=jax.ShapeDtypeStruct((N_CHIP * r, cd), g.dtype),
        compiler_params=_params(("arbitrary", "arbitrary")),
    )(jnp.reshape(core, (1,)).astype(jnp.int32), g, from_sibling)


def _scatter2_copies(refs, r):
    part_ref, recv_ref = refs
    x, y, c, _ = _my_place()
    my_chip = 2 * x + y

    def rows(ref, ch):
        return ref.at[pl.ds(pl.multiple_of(ch * r, SUBLANE), r), :]

    def copy(k, sems, arriving):
        peer, _ = _peer(x, y, c, k)
        peer_chip = 2 * peer[0] + peer[1]
        return pltpu.make_async_remote_copy(
            src_ref=rows(part_ref, peer_chip), dst_ref=rows(recv_ref, peer_chip if arriving else my_chip),
            send_sem=sems[0].at[0], recv_sem=sems[1].at[0], device_id=peer, device_id_type=MESH)

    return copy


def _scatter2_start(g, name, after, carry=()):
    r = g.shape[0]
    x, y, _, _ = _my_place()
    from_sibling = _scatter2_pair(g, name + "_pair")
    partial = _scatter2_add(g, from_sibling, name + "_add")
    cd = partial.shape[1]
    recv = _place_own(partial, "row", (r, cd), "row", jax.ShapeDtypeStruct(partial.shape, partial.dtype),
                      "place_own", index=2 * x + y)

    def make_body(land_refs, in_sems, out_sems):
        copy = _scatter2_copies(land_refs[:2], r)
        for k in FORWARD_KS:
            copy(k, out_sems, False).start()

    sems, lands, token = _gather2_call([partial, recv] + list(carry), (), name, after, make_body, True)
    return (sems, lands[:2], r), token, lands[2:]


def _scatter2_wait(handle, name, after):
    sems, lands, r = handle

    def make_body(land_refs, in_sems, out_sems):
        copy = _scatter2_copies(land_refs, r)
        for k in FORWARD_KS:
            arrival = copy(k, in_sems, True)
            arrival.wait_send()
            arrival.wait_recv()

    _, lands, _ = _gather2_call(lands, sems, name, after, make_body, False)
    return lands[1]


def _pad_mla_w_in(w):
    d = w.shape[0]
    z = lambda n: jnp.zeros((d, n), w.dtype)
    o1, o2, o3 = Q_RANK, Q_RANK + KV_RANK, Q_RANK + KV_RANK + ROPE
    return jnp.concatenate([w[:, :o1], z(Q_RANK_PAD - Q_RANK), w[:, o1:o2], w[:, o2:o3], z(LANE - ROPE), w[:, o3:]], axis=1)


def _unpad_mla_w_in(g):
    return jnp.concatenate([g[:, :Q_RANK], g[:, PROJ_CKV:PROJ_KR], g[:, PROJ_KR:PROJ_KR + ROPE], g[:, PROJ_Z:]], axis=1)


def _pad_w_uq(w):
    w3 = w.reshape(Q_RANK, HEADS, NOPE + ROPE)
    w3 = jnp.pad(w3, ((0, Q_RANK_PAD - Q_RANK), (0, 0), (0, HEAD_PAD - NOPE - ROPE)))
    return w3.reshape(Q_RANK_PAD, HEADS * HEAD_PAD)


def _unpad_w_uq(g):
    return g[:Q_RANK].reshape(Q_RANK, HEADS, HEAD_PAD)[:, :, :NOPE + ROPE].reshape(Q_RANK, HEADS * (NOPE + ROPE))


def _pack(pieces):
    flat = [p.reshape(-1).astype(F32) for p in pieces]
    tot = sum(f.shape[0] for f in flat)
    unit = SUBLANE * LANE
    padn = (-tot) % unit
    if padn:
        flat.append(jnp.zeros((padn,), F32))
    return jnp.concatenate(flat).reshape(-1, LANE)


def _unpack(packed, shapes, lead=()):
    flat = packed.reshape(tuple(lead) + (-1,))
    out, off = [], 0
    for shp in shapes:
        nel = int(np.prod(shp))
        out.append(flat[..., off:off + nel].reshape(tuple(lead) + tuple(shp)))
        off += nel
    return out


SMALL_GROUPS = {
    "a": [('ada_b', lambda t: t[1:]), ('pre_g', lambda t: t[1:]), ('post_g', lambda t: t[1:]),
          ('sgu_norm_g', lambda t: t[1:]), ('sgu_w_s', lambda t: t[1:]), ('sgu_b_s', lambda t: t[1:]),
          ('mla_q_norm_g', lambda t: t), ('mla_kv_norm_g', lambda t: t)],
    "b": [('ada_b', lambda t: t[0:1, 2 * t.shape[1] // 3:]), ('post_g', lambda t: t[0:1]),
          ('sgu_norm_g', lambda t: t[0:1]), ('sgu_w_s', lambda t: t[0:1]), ('sgu_b_s', lambda t: t[0:1])],
    "c": [('ada_b', lambda t: t[0:1, :2 * t.shape[1] // 3]), ('pre_g', lambda t: t[0:1])],
}


WEIGHTS = ['ada_w', 'ada_b', 'pre_g', 'post_g', 'sgu_w_in', 'sgu_norm_g', 'sgu_w_s', 'sgu_b_s', 'sgu_w_out',
           'mla_w_in', 'mla_q_norm_g', 'mla_kv_norm_g', 'mla_w_uq', 'mla_w_ukv', 'mla_w_out']
INPUTS = ['x', 'c'] + WEIGHTS + ['loss_target'] + ['m_' + n for n in WEIGHTS] + ['v_' + n for n in WEIGHTS]


def kernel(x, c, ada_w, ada_b, pre_g, post_g, sgu_w_in, sgu_norm_g, sgu_w_s, sgu_b_s, sgu_w_out, mla_w_in, mla_q_norm_g, mla_kv_norm_g, mla_w_uq, mla_w_ukv, mla_w_out, loss_target, m_ada_w, m_ada_b, m_pre_g, m_post_g, m_sgu_w_in, m_sgu_norm_g, m_sgu_w_s, m_sgu_b_s, m_sgu_w_out, m_mla_w_in, m_mla_q_norm_g, m_mla_kv_norm_g, m_mla_w_uq, m_mla_w_ukv, m_mla_w_out, v_ada_w, v_ada_b, v_pre_g, v_post_g, v_sgu_w_in, v_sgu_norm_g, v_sgu_w_s, v_sgu_b_s, v_sgu_w_out, v_mla_w_in, v_mla_q_norm_g, v_mla_kv_norm_g, v_mla_w_uq, v_mla_w_ukv, v_mla_w_out):
    given = locals()
    A = {name: given[name] for name in INPUTS}
    x0 = A['x'][0]
    tgt = A['loss_target'][0]
    s, d = x0.shape
    e = 2 * d
    ncol = 3 * d // N_DEV
    _, _, _, me = _my_place()
    ktabs = _rope_tables(s)

    gains = jnp.zeros((SUBLANE, LANE), F32)
    gains = gains.at[0:2, :Q_RANK // N_DEV].set(A['mla_q_norm_g'])
    gains = gains.at[2:4, :KV_RANK // N_DEV].set(A['mla_kv_norm_g'])
    c8 = jnp.broadcast_to(A['c'], (SUBLANE, d))
    cg, gg = _exchange([c8, gains], ["row", "row"], True, "ag_cond")
    cond_raw = cg.reshape(N_DEV, SUBLANE, d)[:, 0, :]
    gg = gg.reshape(N_DEV, SUBLANE, LANE)
    gq_full = jnp.transpose(gg[:, 0:2, :Q_RANK // N_DEV], (1, 0, 2)).reshape(N_MIX, Q_RANK)
    gkv_full = jnp.transpose(gg[:, 2:4, :KV_RANK // N_DEV], (1, 0, 2)).reshape(N_MIX, KV_RANK)
    gq_pad = jnp.pad(gq_full, ((0, 0), (0, Q_RANK_PAD - Q_RANK)))

    bias_my = lax.dynamic_slice_in_dim(A['ada_b'], me * ncol, ncol, axis=1)
    mod_part = _ada_mod(cond_raw, A['ada_w'], bias_my, "ada_mod")
    send = jnp.pad(jnp.transpose(mod_part, (1, 0, 2)), ((0, 0), (0, SUBLANE - DEPTH), (0, 0)))
    (rb,) = _exchange([send.reshape(N_DEV * SUBLANE, ncol)], ["row"], False, "a2a_mod")

    token = rb
    gathers = {}
    for i in range(DEPTH):
        j = i // N_MIX
        if i % N_MIX == 0:
            parts = [("in", [(A['sgu_w_in'], j)], ["col"]), ("out", [(A['sgu_w_out'], j)], ["row"])]
        else:
            parts = [("all", [(A['mla_w_in'], j), (A['mla_w_uq'], j), (A['mla_w_ukv'], j), (A['mla_w_out'], j)],
                      ["col", "col", "col", "row"])]
        for part, shards, kinds in parts:
            gathers[(i, part)], token = _gather2_start(shards, kinds, f"ag_start_{i}_{part}", token)

    def forward_gathers(i, carried):
        for key in [k for k in gathers if k[0] == i]:
            gathers[key], _, (carried,) = _gather2_forward(gathers[key], f"ag_forward_{key[0]}_{key[1]}", token,
                                                           carry=[carried])
        return carried

    mod = jnp.transpose(rb.reshape(N_DEV, SUBLANE, ncol)[:, :DEPTH, :], (1, 0, 2)).reshape(DEPTH, 3 * d) + token[0, 0]
    shift = [mod[i:i + 1, :d] for i in range(DEPTH)]
    scale = [mod[i:i + 1, d:2 * d] for i in range(DEPTH)]
    gate = [mod[i:i + 1, 2 * d:] for i in range(DEPTH)]

    saved = []
    x = x0
    for i in range(DEPTH):
        j = i // N_MIX
        h = _pre_fwd(x, A['pre_g'][i:i + 1], scale[i], shift[i], f"pre_fwd")
        if i == 0:
            gathers[(0, "in")], _, (h,) = _gather2_forward(gathers[(0, "in")], "ag_forward_0_in", token, carry=[h])
        if i % N_MIX == 0:
            (w_in,) = _gather2_wait(gathers[(i, "in")], f"ag_wait_{i}_in", h)
            uvz = _mm(h, w_in, "nn", F32, "sgu_in")
            if i == 0:
                gathers[(0, "out")], _, (uvz,) = _gather2_forward(gathers[(0, "out")], "ag_forward_0_out", token,
                                                                  carry=[uvz])
            bias_full = jnp.repeat(A['sgu_b_s'][j].T, e // SGU_GROUPS, axis=1)
            ng = A['sgu_norm_g'][j:j + 1]
            y = _sgu_mid_fwd(uvz, ng, A['sgu_w_s'][j], bias_full, "sgu_mid_fwd")
            y = forward_gathers(i + 1, y)
            (w_out,) = _gather2_wait(gathers[(i, "out")], f"ag_wait_{i}_out", y)
            out = _mm(y, w_out, "nn", F32, "sgu_out")
            saved.append(dict(x=x, h=h, uvz=uvz, y=y, out=out, w_in=w_in, w_out=w_out, bias=bias_full, ng=ng))
        else:
            w_in, w_uq, w_ukv, w_out = _gather2_wait(gathers[(i, "all")], f"ag_wait_{i}_all", h)
            w_in = _pad_mla_w_in(w_in)
            w_uq = _pad_w_uq(w_uq)
            gq, gkv = gq_pad[j:j + 1], gkv_full[j:j + 1]
            proj = _mm(h, w_in, "nn", F32, "mla_in")
            qn, kvn, kr = _mla_norm_fwd(proj, gq, gkv, ktabs, "mla_norm_fwd")
            q = _mm(qn, w_uq, "nn", F32, "mla_uq")
            kv = _mm(kvn, w_ukv, "nn", BF16, "mla_ukv")
            o, y, lse, q_cat = _attn_fwd(q, kv, kr, proj, ktabs, "attn_fwd")
            y = forward_gathers(i + 1, y)
            out = _mm(y, w_out, "nn", F32, "mla_out")
            saved.append(dict(x=x, h=h, proj=proj, qn=qn, kvn=kvn, kr=kr, q_cat=q_cat, kv=kv, o=o, y=y, lse=lse,
                              out=out, w_in=w_in, w_uq=w_uq, w_ukv=w_ukv, w_out=w_out, gq=gq, gkv=gkv))
        x = _post_fwd(x, out, gate[i], A['post_g'][i:i + 1], "post_fwd")

    dx, loss_row = _loss_head(x, tgt, "loss_head")
    loss = lax.psum(loss_row[0, 0], ("x", "y", "c"))

    d_shift, d_scale, d_gate = [None] * DEPTH, [None] * DEPTH, [None] * DEPTH
    d_pre, d_post = [None] * DEPTH, [None] * DEPTH
    scatters, scatters_out = [None] * DEPTH, [None] * DEPTH
    small, small_grads, small_handles = {}, {}, {}
    for i in reversed(range(DEPTH)):
        j = i // N_MIX
        sv = saved[i]
        dout, d_gate[i], d_post[i] = _post_bwd(dx, sv['out'], gate[i], A['post_g'][i:i + 1], "post_bwd")
        if i % N_MIX == 0:
            dy = _mm(dout, sv['w_out'], "nt", F32, "sgu_out_dx")
            g_w_out = _mm(sv['y'], dout, "tn", BF16, "sgu_out_dw")
            scatters_out[i], token, (dy,) = _exchange_start([g_w_out], ["row"], False, f"rs_out_start_{i}", token,
                                                            carry=[dy])
            duvz, dws, dbs, dng = _sgu_mid_bwd(sv['uvz'], dy, sv['ng'], A['sgu_w_s'][j], sv['bias'], "sgu_mid_bwd")
            small[('sgu', j)] = (dws, dbs.reshape(SGU_GROUPS, SGU_BLOCK), dng)
            g_w_in = _mm(sv['h'], duvz, "tn", BF16, "sgu_in_dw")
            if i == 0:
                scatters[i], token, (duvz,) = _scatter2_start(g_w_in, "rs2_start_0", token, carry=[duvz])
                small_grads["b"] = [d_gate[0], d_post[0], dng, dws[None], small[('sgu', 0)][1][None]]
                small_handles["b"], token, (duvz,) = _exchange_start(
                    [_pack(small_grads["b"])], ["row"], True, "ag_small_b_start", token, carry=[duvz])
            else:
                scatters[i], token, (duvz,) = _exchange_start([g_w_in], ["col"], False, f"rs_start_{i}", token,
                                                              carry=[duvz])
            dh = _mm(duvz, sv['w_in'], "nt", F32, "sgu_in_dx")
        else:
            dy = _mm(dout, sv['w_out'], "nt", F32, "mla_out_dx")
            g_w_out = _mm(sv['y'], dout, "tn", BF16, "mla_out_dw")
            scatters_out[i], token, (dy,) = _exchange_start([g_w_out], ["row"], False, f"rs_out_start_{i}", token,
                                                            carry=[dy])
            do, dproj = _mla_gate_bwd(dy, sv['o'], sv['proj'], "mla_gate_bwd")
            dq_b, dkv, dkr_heads = _attn_bwd(sv['q_cat'], sv['kv'], sv['kr'], do, sv['o'], sv['lse'], ktabs,
                                             "attn_bwd")
            dqn = _mm(dq_b, sv['w_uq'], "nt", F32, "mla_uq_dx")
            g_w_uq = _unpad_w_uq(_mm(sv['qn'], dq_b, "tn", BF16, "mla_uq_dw"))
            dkvn = _mm(dkv, sv['w_ukv'], "nt", F32, "mla_ukv_dx")
            g_w_ukv = _mm(sv['kvn'], dkv, "tn", BF16, "mla_ukv_dw")
            dproj, dgq, dgkv = _mla_norm_bwd(dqn, dkvn, dkr_heads, sv['proj'], sv['gq'], sv['gkv'], ktabs, dproj,
                                             "mla_norm_bwd")
            g_w_in = _unpad_mla_w_in(_mm(sv['h'], dproj, "tn", BF16, "mla_in_dw"))
            scatters[i], token, (dproj,) = _exchange_start(
                [g_w_in, g_w_uq, g_w_ukv], ["col", "col", "col"], False, f"rs_start_{i}", token, carry=[dproj])
            dh = _mm(dproj, sv['w_in'], "nt", F32, "mla_in_dx")
            small[('mla', j)] = (dgq[:, :Q_RANK], dgkv)
        dx, d_shift[i], d_scale[i], d_pre[i] = _pre_bwd(dh, sv['x'], dx, A['pre_g'][i:i + 1], scale[i], "pre_bwd")
        if i == 1:
            small_grads["a"] = [
                jnp.concatenate([jnp.concatenate([d_shift[l], d_scale[l], d_gate[l]], axis=1)
                                 for l in range(1, DEPTH)], axis=0),
                jnp.concatenate(d_pre[1:], axis=0), jnp.concatenate(d_post[1:], axis=0),
                small[('sgu', 1)][2], small[('sgu', 1)][0][None], small[('sgu', 1)][1][None],
                jnp.concatenate([small[('mla', jj)][0] for jj in range(N_MIX)], axis=0),
                jnp.concatenate([small[('mla', jj)][1] for jj in range(N_MIX)], axis=0)]
            small_handles["a"], token, (dx,) = _exchange_start(
                [_pack(small_grads["a"])], ["row"], True, "ag_small_a_start", token, carry=[dx])

    res = {}

    def big(name, recv0, recv1):
        res[name] = _adam_reduce(recv0, recv1, A[name], A['m_' + name], A['v_' + name], "adam_" + name)

    def small_full(name, prefix):
        t = A[prefix + name]
        if name in ('mla_q_norm_g', 'mla_kv_norm_g'):
            t = lax.dynamic_update_slice_in_dim(jnp.zeros((t.shape[0], t.shape[1] * N_DEV), F32), t,
                                                me * t.shape[1], axis=1)
        return t

    def finish(i, after):
        first = ([_scatter2_wait(scatters[i], "rs2_wait_0", after)] if i == 0
                 else _exchange_wait(scatters[i], f"rs_wait_{i}", after))
        return first + _exchange_wait(scatters_out[i], f"rs_out_wait_{i}", after)

    recv_mla = {j: finish(N_MIX * j + 1, dx) for j in reversed(range(N_MIX))}
    for idx_w, name in enumerate(['mla_w_in', 'mla_w_uq', 'mla_w_ukv', 'mla_w_out']):
        big(name, recv_mla[0][idx_w], recv_mla[1][idx_w])

    recv_sgu = {j: finish(N_MIX * j, res['mla_w_out'][0]) for j in reversed(range(N_MIX))}
    for idx_w, name in enumerate(['sgu_w_in', 'sgu_w_out']):
        big(name, recv_sgu[0][idx_w], recv_sgu[1][idx_w])

    small_grads["c"] = [jnp.concatenate([d_shift[0], d_scale[0]], axis=1), d_pre[0]]
    gathered = {tag: _exchange_wait(small_handles[tag], f"ag_small_{tag}_wait", res['sgu_w_in'][0])[0]
                for tag in ("a", "b")}
    (gathered["c"],) = _exchange([_pack(small_grads["c"])], ["row"], True, "ag_small_c", after=res['sgu_w_in'][0])

    upd, parts = {}, {}
    for tag, group in SMALL_GROUPS.items():
        shapes = [g.shape for g in small_grads[tag]]
        packs = [_pack([pick(small_full(name, prefix)) for name, pick in group]) for prefix in ('', 'm_', 'v_')]
        outs4 = _adam_small(gathered[tag].reshape(N_DEV, -1, LANE), *packs, "adam_small_" + tag)
        upd[tag] = [_unpack(t, shapes) for t in outs4]
        parts[tag] = _unpack(gathered[tag].reshape(N_DEV, -1), shapes, lead=(N_DEV,))
    for k_out in range(4):
        ab_rest, pre_rest, post_rest, ng_1, ws_1, bs_1, gq_all, gkv_all = upd["a"][k_out]
        ab_gate0, post_0, ng_0, ws_0, bs_0 = upd["b"][k_out]
        ab_0, pre_0 = upd["c"][k_out]
        for name, val in (
                ('ada_b', jnp.concatenate([jnp.concatenate([ab_0, ab_gate0], axis=1), ab_rest], axis=0)),
                ('pre_g', jnp.concatenate([pre_0, pre_rest], axis=0)),
                ('post_g', jnp.concatenate([post_0, post_rest], axis=0)),
                ('sgu_norm_g', jnp.concatenate([ng_0, ng_1], axis=0)),
                ('sgu_w_s', jnp.concatenate([ws_0, ws_1], axis=0)),
                ('sgu_b_s', jnp.concatenate([bs_0, bs_1], axis=0)),
                ('mla_q_norm_g', gq_all), ('mla_kv_norm_g', gkv_all)):
            if name in ('mla_q_norm_g', 'mla_kv_norm_g'):
                wdt = A[name].shape[1]
                val = lax.dynamic_slice_in_dim(val, me * wdt, wdt, axis=1)
            res.setdefault(name, [None] * 4)[k_out] = val

    dmod_all = jnp.concatenate([jnp.concatenate([parts["c"][0], parts["b"][0]], axis=2), parts["a"][0]], axis=1)
    dmod_cols = jnp.transpose(lax.dynamic_slice_in_dim(dmod_all, me * ncol, ncol, axis=2), (1, 0, 2))
    res['ada_w'] = _ada_bwd_adam(jnp.transpose(cond_raw), dmod_cols, A['ada_w'], A['m_ada_w'], A['v_ada_w'], "ada_bwd")

    outs = [loss, dx[None]]
    for k_out in range(4):
        outs += [res[n][k_out] for n in WEIGHTS]
    return tuple(outs)
```

```python
import functools
import math

import numpy as np
import jax
import jax.numpy as jnp
from jax import lax
from jax.experimental import pallas as pl
from jax.experimental.pallas import tpu as pltpu

F32 = jnp.float32
BF16 = jnp.bfloat16
MESH = pl.DeviceIdType.MESH

N_DEV = 8
DEPTH = 4
N_MIX = 2
NORM_EPS = 1e-6
CHUNK = 64
SGU_BLOCK = 128
SGU_GROUPS = 16
HEADS = 16
Q_RANK = 448
Q_RANK_PAD = 512
KV_RANK = 512
NOPE = 128
ROPE = 64
HALF = ROPE // 2
V_DIM = 128
HEAD_PAD = 256
ROPE_THETA = 10000.0
MLA_WIDTH = HEADS * V_DIM
LANE = 128
SUBLANE = 8
PROJ_CQ = 0
PROJ_CKV = Q_RANK_PAD
PROJ_KR = Q_RANK_PAD + KV_RANK
PROJ_Z = PROJ_KR + LANE
PROJ_W = PROJ_Z + MLA_WIDTH

ADAM_LR = 0.001
ADAM_B1 = 0.9
ADAM_B2 = 0.999
ADAM_EPS = 1e-08
ADAM_WD = 0.01
ADAM_STEP = 10

VMEM_LIMIT = 56 * 1024 * 1024
ATT_BLK = 512
ATT_SUB = 256
ATT_HEADS_PER_STEP = 2
ROW_BLK = 512
MM_TM, MM_TN, MM_TK = 1024, 1024, 2048
MM_TILE_BYTES = 40 * 1024 * 1024
SOFTMAX_SCALE = (NOPE + ROPE) ** -0.5
LOG2_E = 1.0 / math.log(2.0)
INV_SQRT2 = 1.0 / math.sqrt(2.0)
INV_SQRT_2PI = 1.0 / math.sqrt(2.0 * math.pi)


def _pcall(body, comm=False, **kw):
    return pl.pallas_call(body, **kw)


def _params(sem=None):
    return pltpu.CompilerParams(dimension_semantics=sem, vmem_limit_bytes=VMEM_LIMIT)


def _pick(dim, pref):
    if dim <= pref:
        return dim
    t = (pref // LANE) * LANE
    while t >= LANE:
        if dim % t == 0:
            return t
        t -= LANE
    return dim


def _gelu(x):
    return 0.5 * x * (1.0 + lax.erf(x * INV_SQRT2))


def _gelu_grad(x):
    return 0.5 * (1.0 + lax.erf(x * INV_SQRT2)) + x * jnp.exp(-0.5 * x * x) * INV_SQRT_2PI


def _sigmoid(x):
    return 1.0 / (1.0 + jnp.exp(-x))


def _dot_nt(a, b):
    return lax.dot_general(a, b, (((1,), (1,)), ((), ())), preferred_element_type=F32)


def _dot_tn(a, b):
    return lax.dot_general(a, b, (((0,), (0,)), ((), ())), preferred_element_type=F32)


def _mm(a, b, dims, out_dtype, name):
    if dims == "nn":
        (m, k), (k2, n) = a.shape, b.shape
    elif dims == "nt":
        (m, k), (n, k2) = a.shape, b.shape
    else:
        (k, m), (k2, n) = a.shape, b.shape
    assert k == k2, (a.shape, b.shape, dims)
    tm, tn = _pick(m, MM_TM), _pick(n, MM_TN)
    out_bytes = 2 * tm * tn * jnp.dtype(out_dtype).itemsize
    whole_k = 2 * (tm + tn) * k * a.dtype.itemsize + out_bytes <= MM_TILE_BYTES
    tk = k if whole_k else _pick(k, MM_TK)
    nk = k // tk

    def body(a_ref, b_ref, o_ref, *scratch):
        if dims == "nn":
            p = jnp.dot(a_ref[...], b_ref[...], preferred_element_type=F32)
        elif dims == "nt":
            p = _dot_nt(a_ref[...], b_ref[...])
        else:
            p = _dot_tn(a_ref[...], b_ref[...])
        if nk == 1:
            o_ref[...] = p.astype(o_ref.dtype)
            return
        acc_ref, = scratch
        kk = pl.program_id(2)

        @pl.when(kk == 0)
        def _():
            acc_ref[...] = p

        @pl.when(kk > 0)
        def _():
            acc_ref[...] += p

        @pl.when(kk == nk - 1)
        def _():
            o_ref[...] = acc_ref[...].astype(o_ref.dtype)

    if dims == "tn":
        a_spec = pl.BlockSpec((tk, tm), lambda i, j, kk: (kk, i))
    else:
        a_spec = pl.BlockSpec((tm, tk), lambda i, j, kk: (i, kk))
    if dims == "nt":
        b_spec = pl.BlockSpec((tn, tk), lambda i, j, kk: (j, kk))
    else:
        b_spec = pl.BlockSpec((tk, tn), lambda i, j, kk: (kk, j))
    return _pcall(
        body, name=name,
        grid=(m // tm, n // tn, nk),
        in_specs=[a_spec, b_spec],
        out_specs=pl.BlockSpec((tm, tn), lambda i, j, kk: (i, j)),
        out_shape=jax.ShapeDtypeStruct((m, n), out_dtype),
        scratch_shapes=[pltpu.VMEM((tm, tn), F32)] if nk > 1 else [],
        compiler_params=_params(("parallel", "parallel", "arbitrary")),
    )(a, b)


def _row_spec(ts, d):
    return pl.BlockSpec((ts, d), lambda i: (i, 0))


def _vec_spec(d):
    return pl.BlockSpec((1, d), lambda i: (0, 0))


def _pre_fwd(x, g, scale, shift, name):
    s, d = x.shape
    ts = _pick(s, ROW_BLK)

    def body(x_ref, g_ref, sc_ref, sh_ref, h_ref):
        xv = x_ref[...]
        r = lax.rsqrt(jnp.mean(xv * xv, axis=-1, keepdims=True) + NORM_EPS)
        h_ref[...] = ((xv * r * g_ref[...]) * (1.0 + sc_ref[...]) + sh_ref[...]).astype(BF16)

    return _pcall(
        body, name=name, grid=(s // ts,),
        in_specs=[_row_spec(ts, d), _vec_spec(d), _vec_spec(d), _vec_spec(d)],
        out_specs=_row_spec(ts, d),
        out_shape=jax.ShapeDtypeStruct((s, d), BF16),
        compiler_params=_params(("parallel",)),
    )(x, g, scale, shift)


def _post_fwd(x, out, gate, g, name):
    s, d = x.shape
    ts = _pick(s, ROW_BLK)

    def body(x_ref, o_ref, gate_ref, g_ref, y_ref):
        o = o_ref[...]
        r = lax.rsqrt(jnp.mean(o * o, axis=-1, keepdims=True) + NORM_EPS)
        y_ref[...] = x_ref[...] + gate_ref[...] * (o * r * g_ref[...])

    return _pcall(
        body, name=name, grid=(s // ts,),
        in_specs=[_row_spec(ts, d), _row_spec(ts, d), _vec_spec(d), _vec_spec(d)],
        out_specs=_row_spec(ts, d),
        out_shape=jax.ShapeDtypeStruct((s, d), F32),
        compiler_params=_params(("parallel",)),
    )(x, out, gate, g)


def _loss_head(xf, tgt, name):
    s, d = xf.shape
    ts = _pick(s, ROW_BLK)
    ns = s // ts

    def body(x_ref, t_ref, dx_ref, loss_ref, acc_ref):
        i = pl.program_id(0)

        @pl.when(i == 0)
        def _():
            acc_ref[...] = jnp.zeros_like(acc_ref)

        e = x_ref[...] - t_ref[...]
        dx_ref[...] = e * (1.0 / d)
        acc_ref[...] += jnp.sum(e * e, axis=0, keepdims=True)

        @pl.when(i == ns - 1)
        def _():
            tot = jnp.sum(acc_ref[...], axis=1, keepdims=True) * (0.5 / d)
            loss_ref[...] = jnp.broadcast_to(tot, loss_ref.shape)

    return _pcall(
        body, name=name, grid=(ns,),
        in_specs=[_row_spec(ts, d), _row_spec(ts, d)],
        out_specs=[_row_spec(ts, d), pl.BlockSpec((1, LANE), lambda i: (0, 0))],
        out_shape=[jax.ShapeDtypeStruct((s, d), F32), jax.ShapeDtypeStruct((1, LANE), F32)],
        scratch_shapes=[pltpu.VMEM((1, d), F32)],
        compiler_params=_params(("arbitrary",)),
    )(xf, tgt)


def _post_bwd(dxo, out, gate, g, name):
    s, d = dxo.shape
    ts = _pick(s, ROW_BLK)

    def body(dx_ref, o_ref, gate_ref, g_ref, do_ref, dgate_ref, dg_ref):
        i = pl.program_id(0)

        @pl.when(i == 0)
        def _():
            dgate_ref[...] = jnp.zeros_like(dgate_ref)
            dg_ref[...] = jnp.zeros_like(dg_ref)

        o = o_ref[...]
        dx = dx_ref[...]
        gv = g_ref[...]
        r = lax.rsqrt(jnp.mean(o * o, axis=-1, keepdims=True) + NORM_EPS)
        n = o * r
        dyn = dx * gate_ref[...]
        dgate_ref[...] += jnp.sum(dx * (n * gv), axis=0, keepdims=True)
        dg_ref[...] += jnp.sum(dyn * n, axis=0, keepdims=True)
        dn = dyn * gv
        do_ref[...] = (r * (dn - n * jnp.mean(dn * n, axis=-1, keepdims=True))).astype(BF16)

    return _pcall(
        body, name=name, grid=(s // ts,),
        in_specs=[_row_spec(ts, d), _row_spec(ts, d), _vec_spec(d), _vec_spec(d)],
        out_specs=[_row_spec(ts, d), _vec_spec(d), _vec_spec(d)],
        out_shape=[jax.ShapeDtypeStruct((s, d), BF16), jax.ShapeDtypeStruct((1, d), F32),
                   jax.ShapeDtypeStruct((1, d), F32)],
        compiler_params=_params(("arbitrary",)),
    )(dxo, out, gate, g)


def _pre_bwd(dh, x, dxo, g, scale, name):
    s, d = x.shape
    ts = _pick(s, ROW_BLK)

    def body(dh_ref, x_ref, dxo_ref, g_ref, sc_ref, dx_ref, dsh_ref, dsc_ref, dg_ref):
        i = pl.program_id(0)

        @pl.when(i == 0)
        def _():
            dsh_ref[...] = jnp.zeros_like(dsh_ref)
            dsc_ref[...] = jnp.zeros_like(dsc_ref)
            dg_ref[...] = jnp.zeros_like(dg_ref)

        xv = x_ref[...]
        dhv = dh_ref[...]
        gv = g_ref[...]
        one_sc = 1.0 + sc_ref[...]
        r = lax.rsqrt(jnp.mean(xv * xv, axis=-1, keepdims=True) + NORM_EPS)
        n = xv * r
        dsh_ref[...] += jnp.sum(dhv, axis=0, keepdims=True)
        dsc_ref[...] += jnp.sum(dhv * (n * gv), axis=0, keepdims=True)
        dng = dhv * one_sc
        dg_ref[...] += jnp.sum(dng * n, axis=0, keepdims=True)
        dn = dng * gv
        dx_ref[...] = dxo_ref[...] + r * (dn - n * jnp.mean(dn * n, axis=-1, keepdims=True))

    return _pcall(
        body, name=name, grid=(s // ts,),
        in_specs=[_row_spec(ts, d), _row_spec(ts, d), _row_spec(ts, d), _vec_spec(d), _vec_spec(d)],
        out_specs=[_row_spec(ts, d), _vec_spec(d), _vec_spec(d), _vec_spec(d)],
        out_shape=[jax.ShapeDtypeStruct((s, d), F32)] + [jax.ShapeDtypeStruct((1, d), F32)] * 3,
        compiler_params=_params(("arbitrary",)),
    )(dh, x, dxo, g, scale)


def _post_pre_fwd(x, out, gate, post_g, pre_g, scale, shift, name):
    s, d = x.shape
    ts = _pick(s, ROW_BLK // 2)

    def body(x_ref, o_ref, gate_ref, pg_ref, g_ref, sc_ref, sh_ref, y_ref, h_ref):
        o = o_ref[...]
        r = lax.rsqrt(jnp.mean(o * o, axis=-1, keepdims=True) + NORM_EPS)
        xn = x_ref[...] + gate_ref[...] * (o * r * pg_ref[...])
        y_ref[...] = xn
        r2 = lax.rsqrt(jnp.mean(xn * xn, axis=-1, keepdims=True) + NORM_EPS)
        h_ref[...] = ((xn * r2 * g_ref[...]) * (1.0 + sc_ref[...]) + sh_ref[...]).astype(BF16)

    return _pcall(
        body, name=name, grid=(s // ts,),
        in_specs=[_row_spec(ts, d), _row_spec(ts, d)] + [_vec_spec(d)] * 5,
        out_specs=[_row_spec(ts, d), _row_spec(ts, d)],
        out_shape=[jax.ShapeDtypeStruct((s, d), F32), jax.ShapeDtypeStruct((s, d), BF16)],
        compiler_params=_params(("parallel",)),
    )(x, out, gate, post_g, pre_g, scale, shift)


def _pre_post_bwd(dh, x, dxo, pre_g, scale, out, gate, post_g, name):
    s, d = x.shape
    ts = _pick(s, ROW_BLK // 2)

    def body(dh_ref, x_ref, dxo_ref, g_ref, sc_ref, o_ref, gate_ref, pg_ref,
             dx_ref, dsh_ref, dsc_ref, dg_ref, do_ref, dgate_ref, dpg_ref):
        i = pl.program_id(0)

        @pl.when(i == 0)
        def _():
            for ref in (dsh_ref, dsc_ref, dg_ref, dgate_ref, dpg_ref):
                ref[...] = jnp.zeros_like(ref)

        xv = x_ref[...]
        dhv = dh_ref[...]
        gv = g_ref[...]
        one_sc = 1.0 + sc_ref[...]
        r = lax.rsqrt(jnp.mean(xv * xv, axis=-1, keepdims=True) + NORM_EPS)
        n = xv * r
        dsh_ref[...] += jnp.sum(dhv, axis=0, keepdims=True)
        dsc_ref[...] += jnp.sum(dhv * (n * gv), axis=0, keepdims=True)
        dng = dhv * one_sc
        dg_ref[...] += jnp.sum(dng * n, axis=0, keepdims=True)
        dn = dng * gv
        dx = dxo_ref[...] + r * (dn - n * jnp.mean(dn * n, axis=-1, keepdims=True))
        dx_ref[...] = dx

        o = o_ref[...]
        pgv = pg_ref[...]
        ro = lax.rsqrt(jnp.mean(o * o, axis=-1, keepdims=True) + NORM_EPS)
        no = o * ro
        dyn = dx * gate_ref[...]
        dgate_ref[...] += jnp.sum(dx * (no * pgv), axis=0, keepdims=True)
        dpg_ref[...] += jnp.sum(dyn * no, axis=0, keepdims=True)
        dno = dyn * pgv
        do_ref[...] = (ro * (dno - no * jnp.mean(dno * no, axis=-1, keepdims=True))).astype(BF16)

    vec = jax.ShapeDtypeStruct((1, d), F32)
    return _pcall(
        body, name=name, grid=(s // ts,),
        in_specs=[_row_spec(ts, d)] * 3 + [_vec_spec(d)] * 2 + [_row_spec(ts, d)] + [_vec_spec(d)] * 2,
        out_specs=[_row_spec(ts, d)] + [_vec_spec(d)] * 3 + [_row_spec(ts, d)] + [_vec_spec(d)] * 2,
        out_shape=[jax.ShapeDtypeStruct((s, d), F32), vec, vec, vec, jax.ShapeDtypeStruct((s, d), BF16), vec, vec],
        compiler_params=_params(("arbitrary",)),
    )(dh, x, dxo, pre_g, scale, out, gate, post_g)


def _sgu_mask():
    t = lax.broadcasted_iota(jnp.int32, (SGU_BLOCK, SGU_BLOCK), 0) // CHUNK
    s = lax.broadcasted_iota(jnp.int32, (SGU_BLOCK, SGU_BLOCK), 1) // CHUNK
    return s <= t


def _sgu_norm(v_pre, g):
    e = v_pre.shape[-1]
    vg = _gelu(v_pre)
    mu = jnp.sum(vg, axis=-1, keepdims=True) * (1.0 / e)
    dlt = vg - mu
    var = jnp.sum(dlt * dlt, axis=-1, keepdims=True) * (1.0 / e)
    rstd = lax.rsqrt(var + NORM_EPS)
    vhat = dlt * rstd
    return vhat, rstd, (vhat * g).astype(BF16)


def _sgu_mid_fwd(uvz, norm_g, w_s, bias_full, name):
    s, e3 = uvz.shape
    e = e3 // 3
    gd = e // SGU_GROUPS
    nb = s // SGU_BLOCK

    def body(uvz_ref, g_ref, w_ref, b_ref, y_ref, wsc):
        @pl.when(pl.program_id(0) == 0)
        def _():
            msk = _sgu_mask()
            for gi in range(SGU_GROUPS):
                wsc[gi] = jnp.where(msk, w_ref[gi], 0.0).astype(BF16)

        _, _, vb = _sgu_norm(uvz_ref[:, e:2 * e], g_ref[...])
        for gi in range(SGU_GROUPS):
            lo = gi * gd
            vm = jnp.dot(wsc[gi], vb[:, lo:lo + gd], preferred_element_type=F32) + b_ref[:, lo:lo + gd]
            zg = uvz_ref[:, 2 * e + lo:2 * e + lo + gd]
            y_ref[:, lo:lo + gd] = (_gelu(uvz_ref[:, lo:lo + gd]) * vm * (zg * _sigmoid(zg))).astype(BF16)

    return _pcall(
        body, name=name, grid=(nb,),
        in_specs=[pl.BlockSpec((SGU_BLOCK, e3), lambda n: (n, 0)),
                  pl.BlockSpec((1, e), lambda n: (0, 0)),
                  pl.BlockSpec((SGU_GROUPS, SGU_BLOCK, SGU_BLOCK), lambda n: (0, 0, 0)),
                  pl.BlockSpec((SGU_BLOCK, e), lambda n: (0, 0))],
        out_specs=pl.BlockSpec((SGU_BLOCK, e), lambda n: (n, 0)),
        out_shape=jax.ShapeDtypeStruct((s, e), BF16),
        scratch_shapes=[pltpu.VMEM((SGU_GROUPS, SGU_BLOCK, SGU_BLOCK), BF16)],
        compiler_params=_params(("arbitrary",)),
    )(uvz, norm_g, w_s, bias_full)


def _sgu_mid_bwd(uvz, dy, norm_g, w_s, bias_full, name):
    s, e3 = uvz.shape
    e = e3 // 3
    gd = e // SGU_GROUPS
    nb = s // SGU_BLOCK

    def body(uvz_ref, dy_ref, g_ref, w_ref, b_ref, d_ref, dw_ref, db_ref, dg_ref, wsc, wtsc, dvh_sc, dbacc):
        n = pl.program_id(0)

        @pl.when(n == 0)
        def _():
            msk = _sgu_mask()
            for gi in range(SGU_GROUPS):
                wm = jnp.where(msk, w_ref[gi], 0.0)
                wsc[gi] = wm.astype(BF16)
                wtsc[gi] = wm.T.astype(BF16)
            dw_ref[...] = jnp.zeros_like(dw_ref)
            dg_ref[...] = jnp.zeros_like(dg_ref)
            dbacc[...] = jnp.zeros_like(dbacc)

        v_pre = uvz_ref[:, e:2 * e]
        gv = g_ref[...]
        vhat, rstd, vb = _sgu_norm(v_pre, gv)
        s1 = jnp.zeros((SGU_BLOCK, 1), F32)
        s2 = jnp.zeros((SGU_BLOCK, 1), F32)
        for gi in range(SGU_GROUPS):
            lo = gi * gd
            u_pre = uvz_ref[:, lo:lo + gd]
            zg = uvz_ref[:, 2 * e + lo:2 * e + lo + gd]
            dyg = dy_ref[:, lo:lo + gd]
            ug = _gelu(u_pre)
            sig = _sigmoid(zg)
            vbg = vb[:, lo:lo + gd]
            vhg = vhat[:, lo:lo + gd]
            vm = jnp.dot(wsc[gi], vbg, preferred_element_type=F32) + b_ref[:, lo:lo + gd]
            t = dyg * (zg * sig)
            d_ref[:, lo:lo + gd] = (t * vm * _gelu_grad(u_pre)).astype(BF16)
            dvm = t * ug
            d_ref[:, 2 * e + lo:2 * e + lo + gd] = (dyg * ug * vm * (sig * (1.0 + zg * (1.0 - sig)))).astype(BF16)
            dvm_b = dvm.astype(BF16)
            dv = jnp.dot(wtsc[gi], dvm_b, preferred_element_type=F32)
            dw_ref[gi] += _dot_nt(dvm_b, vbg)
            dbacc[:, lo:lo + gd] += dvm
            dg_ref[:, lo:lo + gd] += jnp.sum(dv * vhg, axis=0, keepdims=True)
            dvh = dv * gv[:, lo:lo + gd]
            dvh_sc[:, lo:lo + gd] = dvh
            s1 = s1 + jnp.sum(dvh, axis=-1, keepdims=True)
            s2 = s2 + jnp.sum(dvh * vhg, axis=-1, keepdims=True)
        dvg = rstd * (dvh_sc[...] - s1 * (1.0 / e) - vhat * (s2 * (1.0 / e)))
        d_ref[:, e:2 * e] = (dvg * _gelu_grad(v_pre)).astype(BF16)

        @pl.when(n == nb - 1)
        def _():
            msk = _sgu_mask()
            for gi in range(SGU_GROUPS):
                dw_ref[gi] = jnp.where(msk, dw_ref[gi], 0.0)
                db_ref[gi] = jnp.sum(dbacc[:, gi * gd:(gi + 1) * gd], axis=1, keepdims=True)

    return _pcall(
        body, name=name, grid=(nb,),
        in_specs=[pl.BlockSpec((SGU_BLOCK, e3), lambda n: (n, 0)),
                  pl.BlockSpec((SGU_BLOCK, e), lambda n: (n, 0)),
                  pl.BlockSpec((1, e), lambda n: (0, 0)),
                  pl.BlockSpec((SGU_GROUPS, SGU_BLOCK, SGU_BLOCK), lambda n: (0, 0, 0)),
                  pl.BlockSpec((SGU_BLOCK, e), lambda n: (0, 0))],
        out_specs=[pl.BlockSpec((SGU_BLOCK, e3), lambda n: (n, 0)),
                   pl.BlockSpec((SGU_GROUPS, SGU_BLOCK, SGU_BLOCK), lambda n: (0, 0, 0)),
                   pl.BlockSpec((SGU_GROUPS, SGU_BLOCK, 1), lambda n: (0, 0, 0)),
                   pl.BlockSpec((1, e), lambda n: (0, 0))],
        out_shape=[jax.ShapeDtypeStruct((s, e3), BF16),
                   jax.ShapeDtypeStruct((SGU_GROUPS, SGU_BLOCK, SGU_BLOCK), F32),
                   jax.ShapeDtypeStruct((SGU_GROUPS, SGU_BLOCK, 1), F32),
                   jax.ShapeDtypeStruct((1, e), F32)],
        scratch_shapes=[pltpu.VMEM((SGU_GROUPS, SGU_BLOCK, SGU_BLOCK), BF16),
                        pltpu.VMEM((SGU_GROUPS, SGU_BLOCK, SGU_BLOCK), BF16),
                        pltpu.VMEM((SGU_BLOCK, e), F32),
                        pltpu.VMEM((SGU_BLOCK, e), F32)],
        compiler_params=_params(("arbitrary",)),
    )(uvz, dy, norm_g, w_s, bias_full)


def _rope_tables(s):
    pos = jnp.arange(s, dtype=F32)
    inv_freq = ROPE_THETA ** (-jnp.arange(0, ROPE, 2, dtype=F32) / ROPE)
    ang = pos[:, None] * inv_freq[None, :]
    cos, sin = jnp.cos(ang), jnp.sin(ang)
    z32 = jnp.zeros((s, HALF), F32)
    z64 = jnp.zeros((s, ROPE), F32)
    ck = jnp.concatenate([cos, cos, z64], axis=1)
    s1k = jnp.concatenate([-sin, z32, z64], axis=1)
    s2k = jnp.concatenate([z32, sin, z64], axis=1)
    return ck, s1k, s2k


def _rot(x, c, s1, s2):
    w = x.shape[-1]
    return x * c + pltpu.roll(x, w - HALF, 1) * s1 + pltpu.roll(x, HALF, 1) * s2


def _rms(cv, n_real):
    r = lax.rsqrt(jnp.sum(cv * cv, axis=-1, keepdims=True) * (1.0 / n_real) + NORM_EPS)
    return r, cv * r


def _mla_norm_fwd(proj, gq, gkv, tabs, name):
    s = proj.shape[0]
    ts = _pick(s, ROW_BLK)
    ck, s1k, s2k = tabs

    def body(cq_ref, ckv_ref, kr_ref, gq_ref, gkv_ref, c_ref, s1_ref, s2_ref, qn_ref, kvn_ref, kro_ref):
        _, nq = _rms(cq_ref[...], Q_RANK)
        qn_ref[...] = (nq * gq_ref[...]).astype(BF16)
        _, nkv = _rms(ckv_ref[...], KV_RANK)
        kvn_ref[...] = (nkv * gkv_ref[...]).astype(BF16)
        kro_ref[...] = _rot(kr_ref[...], c_ref[...], s1_ref[...], s2_ref[...]).astype(BF16)

    tab = pl.BlockSpec((ts, LANE), lambda i: (i, 0))
    return _pcall(
        body, name=name, grid=(s // ts,),
        in_specs=[pl.BlockSpec((ts, Q_RANK_PAD), lambda i: (i, 0)),
                  pl.BlockSpec((ts, KV_RANK), lambda i: (i, PROJ_CKV // KV_RANK)),
                  pl.BlockSpec((ts, LANE), lambda i: (i, PROJ_KR // LANE)),
                  _vec_spec(Q_RANK_PAD), _vec_spec(KV_RANK), tab, tab, tab],
        out_specs=[pl.BlockSpec((ts, Q_RANK_PAD), lambda i: (i, 0)),
                   pl.BlockSpec((ts, KV_RANK), lambda i: (i, 0)), tab],
        out_shape=[jax.ShapeDtypeStruct((s, Q_RANK_PAD), BF16), jax.ShapeDtypeStruct((s, KV_RANK), BF16),
                   jax.ShapeDtypeStruct((s, LANE), BF16)],
        compiler_params=_params(("parallel",)),
    )(proj, proj, proj, gq, gkv, ck, s1k, s2k)


def _transpose_bf16(t):
    return t.astype(F32).T.astype(BF16)


def _diag_mask(tb, transposed):
    r = lax.broadcasted_iota(jnp.int32, (tb, tb), 0) // CHUNK
    c = lax.broadcasted_iota(jnp.int32, (tb, tb), 1) // CHUNK
    return (r <= c) if transposed else (c <= r)


def _attn_fwd(q, kv, kr, proj, tabs, name):
    s = q.shape[0]
    tb = _pick(s, ATT_BLK)
    nb = s // tb
    zcol = PROJ_Z // V_DIM
    mult = SOFTMAX_SCALE * LOG2_E

    hp = ATT_HEADS_PER_STEP

    def body(qf_ref, c_ref, s1_ref, s2_ref, kv_ref, kr_ref, *rest):
        z_refs = rest[:hp]
        o_ref, y_ref, lse_ref, q_ref = rest[hp:hp + 4]
        scratch = rest[hp + 4:]
        kt_sc, vx_sc, m_sc, acc_sc, sa_sc, sb_sc = (scratch[i * hp:(i + 1) * hp] for i in range(6))
        qi = pl.program_id(1)
        heads = range(hp)
        for hh in heads:
            lo = hh * HEAD_PAD
            q_ref[:, lo:lo + NOPE] = (qf_ref[:, lo:lo + NOPE] * mult).astype(BF16)
            q_ref[:, lo + NOPE:lo + HEAD_PAD] = (
                _rot(qf_ref[:, lo + NOPE:lo + HEAD_PAD], c_ref[...], s1_ref[...], s2_ref[...]) * mult).astype(BF16)

        @pl.when(qi == 0)
        def _():
            for hh in heads:
                kcol = hh * (NOPE + V_DIM)
                for b in range(nb):
                    rows = slice(b * tb, (b + 1) * tb)
                    kt_sc[hh][b] = _transpose_bf16(
                        jnp.concatenate([kv_ref[rows, kcol:kcol + NOPE], kr_ref[rows, :]], axis=1))
                    vx_sc[hh][b] = jnp.concatenate(
                        [kv_ref[rows, kcol + NOPE:kcol + NOPE + V_DIM], jnp.ones((tb, V_DIM), BF16)], axis=1)

        for hh in heads:
            m_sc[hh][...] = jnp.full_like(m_sc[hh], -1e30)
            acc_sc[hh][...] = jnp.zeros_like(acc_sc[hh])
        sub = min(tb, ATT_SUB)

        def scores(ki, bufs):
            for hh in heads:
                bufs[hh][...] = jnp.dot(q_ref[:, hh * HEAD_PAD:(hh + 1) * HEAD_PAD], kt_sc[hh][ki],
                                        preferred_element_type=F32)

        def step(ki, bufs, masked):
            for r in range(tb // sub):
                rs = slice(r * sub, (r + 1) * sub)
                for hh in heads:
                    sc = bufs[hh][rs, :]
                    if masked:
                        sc = jnp.where(_diag_mask(tb, False)[rs, :], sc, -1e30)
                    m_prev = m_sc[hh][rs, :]
                    m_new = jnp.maximum(m_prev, jnp.max(sc, axis=-1, keepdims=True))
                    p = jnp.exp2(sc - m_new).astype(BF16)
                    acc_sc[hh][rs, :] = (jnp.exp2(m_prev - m_new) * acc_sc[hh][rs, :]
                                         + jnp.dot(p, vx_sc[hh][ki], preferred_element_type=F32))
                    m_sc[hh][rs, :] = m_new

        def pair(t, carry):
            scores(2 * t + 1, sb_sc)
            step(2 * t, sa_sc, False)
            scores(2 * t + 2, sa_sc)
            step(2 * t + 1, sb_sc, False)
            return carry

        scores(0, sa_sc)
        lax.fori_loop(0, qi // 2, pair, 0)

        @pl.when(qi % 2 == 1)
        def _():
            scores(qi, sb_sc)
            step(qi - 1, sa_sc, False)
            step(qi, sb_sc, True)

        @pl.when(qi % 2 == 0)
        def _():
            step(qi, sa_sc, True)

        for hh in heads:
            l = acc_sc[hh][:, V_DIM:V_DIM + 1]
            o = acc_sc[hh][:, :V_DIM] / l
            z = z_refs[hh][...]
            o_ref[:, hh * V_DIM:(hh + 1) * V_DIM] = o.astype(BF16)
            y_ref[:, hh * V_DIM:(hh + 1) * V_DIM] = (o * (z * _sigmoid(z))).astype(BF16)
            lse_cols = jnp.broadcast_to(m_sc[hh][...] + jnp.log2(l), (tb, LANE))
            lse_ref[hh] = lse_cols.T[0:1, :]

    oblk = pl.BlockSpec((tb, hp * V_DIM), lambda g, qi: (qi, g))
    qblk = pl.BlockSpec((tb, hp * HEAD_PAD), lambda g, qi: (qi, g))
    tab = pl.BlockSpec((tb, LANE), lambda g, qi: (qi, 0))
    per_head = lambda shape, dtype: [pltpu.VMEM(shape, dtype)] * hp
    return _pcall(
        body, name=name, grid=(HEADS // hp, nb),
        in_specs=[qblk, tab, tab, tab,
                  pl.BlockSpec((s, hp * (NOPE + V_DIM)), lambda g, qi: (0, g)),
                  pl.BlockSpec((s, LANE), lambda g, qi: (0, 0))]
                 + [pl.BlockSpec((tb, V_DIM), functools.partial(lambda g, qi, hh: (qi, zcol + hp * g + hh), hh=hh))
                    for hh in range(hp)],
        out_specs=[oblk, oblk, pl.BlockSpec((hp, None, 1, tb), lambda g, qi: (g, qi, 0, 0)), qblk],
        out_shape=[jax.ShapeDtypeStruct((s, MLA_WIDTH), BF16), jax.ShapeDtypeStruct((s, MLA_WIDTH), BF16),
                   jax.ShapeDtypeStruct((HEADS, nb, 1, tb), F32), jax.ShapeDtypeStruct((s, HEADS * HEAD_PAD), BF16)],
        scratch_shapes=(per_head((nb, HEAD_PAD, tb), BF16) + per_head((nb, tb, HEAD_PAD), BF16)
                        + per_head((tb, 1), F32) + per_head((tb, HEAD_PAD), F32)
                        + per_head((tb, tb), F32) + per_head((tb, tb), F32)),
        compiler_params=_params(("parallel", "arbitrary")),
    )(q, *tabs, kv, kr, *([proj] * hp))


def _attn_bwd(q_cat, kv, kr, do, o, lse, tabs, name):
    s = q_cat.shape[0]
    tb = _pick(s, ATT_BLK)
    nb = s // tb
    ln2 = math.log(2.0)

    hp = ATT_HEADS_PER_STEP
    heads = range(hp)

    def body(q_ref, do_ref, o_ref, lse_ref, kv_ref, kr_ref, c_ref, s1_ref, s2_ref, dq_ref, dkv_ref, dkr_ref, *scratch):
        qt_sc, dot_sc, delta_sc, dqt_sc, dk_sc, dv_sc = (scratch[i * hp:(i + 1) * hp] for i in range(6))
        ki = pl.program_id(1)
        qcols = lambda hh: slice(hh * HEAD_PAD, (hh + 1) * HEAD_PAD)
        vcols = lambda hh: slice(hh * V_DIM, (hh + 1) * V_DIM)

        @pl.when(ki == 0)
        def _():
            for hh in heads:
                for b in range(nb):
                    rows = slice(b * tb, (b + 1) * tb)
                    qt_sc[hh][b] = _transpose_bf16(q_ref[rows, qcols(hh)])
                    do_t = do_ref[rows, vcols(hh)].astype(F32).T
                    dot_sc[hh][b] = do_t.astype(BF16)
                    delta_sc[hh][b] = jnp.sum(do_t * o_ref[rows, vcols(hh)].astype(F32).T, axis=0, keepdims=True)
                dqt_sc[hh][...] = jnp.zeros_like(dqt_sc[hh])

        k, kt, vb = [], [], []
        for hh in heads:
            kcol = hh * (NOPE + V_DIM)
            k.append(jnp.concatenate([kv_ref[:, kcol:kcol + NOPE], kr_ref[...]], axis=1))
            kt.append(_transpose_bf16(k[hh]))
            vb.append(kv_ref[:, kcol + NOPE:kcol + NOPE + V_DIM])
            dk_sc[hh][...] = jnp.zeros_like(dk_sc[hh])
            dv_sc[hh][...] = jnp.zeros_like(dv_sc[hh])

        def step(qi, masked):
            rows = pl.ds(pl.multiple_of(qi * tb, tb), tb)
            for hh in heads:
                sc_t = jnp.dot(k[hh], qt_sc[hh][qi], preferred_element_type=F32)
                if masked:
                    sc_t = jnp.where(_diag_mask(tb, True), sc_t, -1e30)
                p_t = jnp.exp2(sc_t - lse_ref[hh, qi])
                dp_t = jnp.dot(vb[hh], dot_sc[hh][qi], preferred_element_type=F32)
                ds_t = (p_t * (dp_t - delta_sc[hh][qi])).astype(BF16)
                dv_sc[hh][...] += jnp.dot(p_t.astype(BF16), do_ref[rows, vcols(hh)], preferred_element_type=F32)
                dk_sc[hh][...] += jnp.dot(ds_t, q_ref[rows, qcols(hh)], preferred_element_type=F32)
                dqt_sc[hh][qi] += jnp.dot(kt[hh], ds_t, preferred_element_type=F32)

        step(ki, True)

        def loop_body(qi, carry):
            step(qi, False)
            return carry

        lax.fori_loop(ki + 1, nb, loop_body, 0)

        for hh in heads:
            lo = hh * HEAD_PAD
            dkv_ref[:, lo:lo + NOPE] = (dk_sc[hh][:, :NOPE] * ln2).astype(BF16)
            dkv_ref[:, lo + NOPE:lo + HEAD_PAD] = dv_sc[hh][...].astype(BF16)
            dkr_ref[hh] = dk_sc[hh][:, NOPE:] * ln2

        @pl.when(ki == nb - 1)
        def _():
            for hh in heads:
                lo = hh * HEAD_PAD
                for b in range(nb):
                    rows = slice(b * tb, (b + 1) * tb)
                    dq = dqt_sc[hh][b].T
                    dq_ref[rows, lo:lo + NOPE] = (dq[:, :NOPE] * SOFTMAX_SCALE).astype(BF16)
                    dq_ref[rows, lo + NOPE:lo + HEAD_PAD] = (
                        _rot(dq[:, NOPE:], c_ref[rows, :], -s1_ref[rows, :], -s2_ref[rows, :])
                        * SOFTMAX_SCALE).astype(BF16)

    tab = pl.BlockSpec((s, LANE), lambda g, ki: (0, 0), pipeline_mode=pl.Buffered(1))
    per_head = lambda shape, dtype: [pltpu.VMEM(shape, dtype)] * hp
    return _pcall(
        body, name=name, grid=(HEADS // hp, nb),
        in_specs=[pl.BlockSpec((s, hp * HEAD_PAD), lambda g, ki: (0, g)),
                  pl.BlockSpec((s, hp * V_DIM), lambda g, ki: (0, g)),
                  pl.BlockSpec((s, hp * V_DIM), lambda g, ki: (0, g)),
                  pl.BlockSpec((hp, nb, 1, tb), lambda g, ki: (g, 0, 0, 0)),
                  pl.BlockSpec((tb, hp * (NOPE + V_DIM)), lambda g, ki: (ki, g)),
                  pl.BlockSpec((tb, LANE), lambda g, ki: (ki, 0)), tab, tab, tab],
        out_specs=[pl.BlockSpec((s, hp * HEAD_PAD), lambda g, ki: (0, g)),
                   pl.BlockSpec((tb, hp * HEAD_PAD), lambda g, ki: (ki, g)),
                   pl.BlockSpec((hp, tb, LANE), lambda g, ki: (g, ki, 0))],
        out_shape=[jax.ShapeDtypeStruct((s, HEADS * HEAD_PAD), BF16),
                   jax.ShapeDtypeStruct((s, HEADS * HEAD_PAD), BF16),
                   jax.ShapeDtypeStruct((HEADS, s, LANE), F32)],
        scratch_shapes=(per_head((nb, HEAD_PAD, tb), BF16) + per_head((nb, V_DIM, tb), BF16)
                        + per_head((nb, 1, tb), F32) + per_head((nb, HEAD_PAD, tb), F32)
                        + per_head((tb, HEAD_PAD), F32) + per_head((tb, V_DIM), F32)),
        compiler_params=_params(("parallel", "arbitrary")),
    )(q_cat, do, o, lse, kv, kr, *tabs)


def _mla_gate_bwd(dy, o, proj, name):
    s = dy.shape[0]
    ts = _pick(s, ROW_BLK)

    def body(dy_ref, o_ref, p_ref, do_ref, dp_ref):
        z = p_ref[:, PROJ_Z:]
        dyv = dy_ref[...]
        sig = _sigmoid(z)
        do_ref[...] = (dyv * (z * sig)).astype(BF16)
        dp_ref[:, :PROJ_Z] = jnp.zeros((ts, PROJ_Z), BF16)
        dp_ref[:, PROJ_Z:] = (dyv * o_ref[...].astype(F32) * (sig * (1.0 + z * (1.0 - sig)))).astype(BF16)

    blk = pl.BlockSpec((ts, MLA_WIDTH), lambda i: (i, 0))
    wide = pl.BlockSpec((ts, PROJ_W), lambda i: (i, 0))
    return _pcall(
        body, name=name, grid=(s // ts,),
        in_specs=[blk, blk, wide],
        out_specs=[blk, wide],
        out_shape=[jax.ShapeDtypeStruct((s, MLA_WIDTH), BF16), jax.ShapeDtypeStruct((s, PROJ_W), BF16)],
        compiler_params=_params(("parallel",)),
    )(dy, o, proj)


def _mla_norm_bwd(dqn, dkvn, dkr_heads, proj, gq, gkv, tabs, dproj, name):
    s = proj.shape[0]
    ts = _pick(s, ROW_BLK)
    ck, s1k, s2k = tabs

    def rms_bwd(cv, dn_in, g, n_real):
        r, n = _rms(cv, n_real)
        dg = jnp.sum(dn_in * n, axis=0, keepdims=True)
        dn = dn_in * g
        dc = r * (dn - n * (jnp.sum(dn * n, axis=-1, keepdims=True) * (1.0 / n_real)))
        return dc, dg

    def body(dqn_ref, dkvn_ref, dkr_ref, cq_ref, ckv_ref, gq_ref, gkv_ref, c_ref, s1_ref, s2_ref, dp_in_ref,
             dp_ref, dgq_ref, dgkv_ref):
        @pl.when(pl.program_id(0) == 0)
        def _():
            dgq_ref[...] = jnp.zeros_like(dgq_ref)
            dgkv_ref[...] = jnp.zeros_like(dgkv_ref)

        dcq, dgq = rms_bwd(cq_ref[...], dqn_ref[...], gq_ref[...], Q_RANK)
        dckv, dgkv = rms_bwd(ckv_ref[...], dkvn_ref[...], gkv_ref[...], KV_RANK)
        dgq_ref[...] += dgq
        dgkv_ref[...] += dgkv
        dkr = dkr_ref[0]
        for h in range(1, HEADS):
            dkr = dkr + dkr_ref[h]
        dp_ref[:, PROJ_CQ:PROJ_CKV] = dcq.astype(BF16)
        dp_ref[:, PROJ_CKV:PROJ_KR] = dckv.astype(BF16)
        dp_ref[:, PROJ_KR:PROJ_Z] = _rot(dkr, c_ref[...], -s1_ref[...], -s2_ref[...]).astype(BF16)

    tab = pl.BlockSpec((ts, LANE), lambda i: (i, 0))
    return _pcall(
        body, name=name, grid=(s // ts,),
        in_specs=[pl.BlockSpec((ts, Q_RANK_PAD), lambda i: (i, 0)),
                  pl.BlockSpec((ts, KV_RANK), lambda i: (i, 0)),
                  pl.BlockSpec((HEADS, ts, LANE), lambda i: (0, i, 0)),
                  pl.BlockSpec((ts, Q_RANK_PAD), lambda i: (i, 0)),
                  pl.BlockSpec((ts, KV_RANK), lambda i: (i, PROJ_CKV // KV_RANK)),
                  _vec_spec(Q_RANK_PAD), _vec_spec(KV_RANK), tab, tab, tab, pl.BlockSpec(memory_space=pl.ANY)],
        out_specs=[pl.BlockSpec((ts, PROJ_Z), lambda i: (i, 0)), _vec_spec(Q_RANK_PAD), _vec_spec(KV_RANK)],
        out_shape=[jax.ShapeDtypeStruct((s, PROJ_W), BF16), jax.ShapeDtypeStruct((1, Q_RANK_PAD), F32),
                   jax.ShapeDtypeStruct((1, KV_RANK), F32)],
        input_output_aliases={10: 0},
        compiler_params=_params(("arbitrary",)),
    )(dqn, dkvn, dkr_heads, proj, proj, gq, gkv, ck, s1k, s2k, dproj)


def _ada_mod(cond_raw, ada_w, bias_my, name):
    nl, d, ncol = ada_w.shape
    tk = _pick(d, 512)
    nk = d // tk

    def body(c_ref, w_ref, b_ref, o_ref, acc_ref):
        kk = pl.program_id(1)

        @pl.when(kk == 0)
        def _():
            acc_ref[...] = jnp.zeros_like(acc_ref)

        cv = c_ref[...]
        cond = (cv * _sigmoid(cv)).astype(BF16)
        acc_ref[...] += jnp.dot(cond, w_ref[...].astype(BF16), preferred_element_type=F32)

        @pl.when(kk == nk - 1)
        def _():
            o_ref[...] = acc_ref[...] + b_ref[...]

    return _pcall(
        body, name=name, grid=(nl, nk),
        in_specs=[pl.BlockSpec((N_DEV, tk), lambda l, kk: (0, kk)),
                  pl.BlockSpec((None, tk, ncol), lambda l, kk: (l, kk, 0)),
                  pl.BlockSpec((None, 1, ncol), lambda l, kk: (l, 0, 0))],
        out_specs=pl.BlockSpec((None, N_DEV, ncol), lambda l, kk: (l, 0, 0)),
        out_shape=jax.ShapeDtypeStruct((nl, N_DEV, ncol), F32),
        scratch_shapes=[pltpu.VMEM((N_DEV, ncol), F32)],
        compiler_params=_params(("parallel", "arbitrary")),
    )(cond_raw, ada_w, bias_my.reshape(nl, 1, ncol))


def _adam(w, g, m, v):
    m = ADAM_B1 * m + (1.0 - ADAM_B1) * g
    v = ADAM_B2 * v + (1.0 - ADAM_B2) * (g * g)
    m_hat = m / (1.0 - ADAM_B1 ** ADAM_STEP)
    v_hat = v / (1.0 - ADAM_B2 ** ADAM_STEP)
    delta = -ADAM_LR * (m_hat / (jnp.sqrt(v_hat) + ADAM_EPS) + ADAM_WD * w)
    return delta, m, v


def _ada_bwd_adam(cond_t, dmod_cols, w, m, v, name):
    nl, d, ncol = w.shape
    tk = _pick(d, 512)

    def body(c_ref, dm_ref, w_ref, m_ref, v_ref, g_ref, d_ref, mo_ref, vo_ref):
        cv = c_ref[...]
        cond = (cv * _sigmoid(cv)).astype(BF16)
        g = jnp.dot(cond, dm_ref[...].astype(BF16), preferred_element_type=F32)
        delta, m2, v2 = _adam(w_ref[...], g, m_ref[...], v_ref[...])
        g_ref[...] = g
        d_ref[...] = delta
        mo_ref[...] = m2
        vo_ref[...] = v2

    blk = pl.BlockSpec((None, tk, ncol), lambda l, kk: (l, kk, 0))
    shp = jax.ShapeDtypeStruct((nl, d, ncol), F32)
    return _pcall(
        body, name=name, grid=(nl, d // tk),
        in_specs=[pl.BlockSpec((tk, N_DEV), lambda l, kk: (kk, 0)),
                  pl.BlockSpec((None, N_DEV, ncol), lambda l, kk: (l, 0, 0)), blk, blk, blk],
        out_specs=[blk, blk, blk, blk], out_shape=[shp, shp, shp, shp],
        compiler_params=_params(("parallel", "parallel")),
    )(cond_t, dmod_cols, w, m, v)


def _adam_reduce(recv0, recv1, w, m, v, name):
    nl, r, c = w.shape
    tr = _pick(r, 128) if r % 128 == 0 else r
    tc = _pick(c, 1024)
    n0, n1 = recv0.shape[0] // r, recv1.shape[0] // r

    def body(r0_ref, r1_ref, w_ref, m_ref, v_ref, g_ref, d_ref, mo_ref, vo_ref):
        l = pl.program_id(0)

        def run(rr):
            g = rr[0].astype(F32)
            for sidx in range(1, rr.shape[0]):
                g = g + rr[sidx].astype(F32)
            delta, m2, v2 = _adam(w_ref[...], g, m_ref[...], v_ref[...])
            g_ref[...] = g
            d_ref[...] = delta
            mo_ref[...] = m2
            vo_ref[...] = v2

        @pl.when(l == 0)
        def _():
            run(r0_ref)

        @pl.when(l == 1)
        def _():
            run(r1_ref)

    def rblk(n, layer):
        return pl.BlockSpec((n, tr, tc), lambda l, i, j: (0, jnp.where(l == layer, i, 0), jnp.where(l == layer, j, 0)))

    blk = pl.BlockSpec((None, tr, tc), lambda l, i, j: (l, i, j))
    shp = jax.ShapeDtypeStruct((nl, r, c), F32)
    return _pcall(
        body, name=name, grid=(nl, r // tr, c // tc),
        in_specs=[rblk(n0, 0), rblk(n1, 1), blk, blk, blk],
        out_specs=[blk, blk, blk, blk], out_shape=[shp, shp, shp, shp],
        compiler_params=_params(("arbitrary", "parallel", "parallel")),
    )(recv0.reshape(n0, r, c), recv1.reshape(n1, r, c), w, m, v)


def _adam_small(gathered, w, m, v, name):
    r = w.shape[0]
    tr = _pick(r, 512) if r % 512 == 0 else r

    def body(p_ref, w_ref, m_ref, v_ref, g_ref, d_ref, mo_ref, vo_ref):
        g = p_ref[0]
        for sidx in range(1, N_DEV):
            g = g + p_ref[sidx]
        delta, m2, v2 = _adam(w_ref[...], g, m_ref[...], v_ref[...])
        g_ref[...] = g
        d_ref[...] = delta
        mo_ref[...] = m2
        vo_ref[...] = v2

    blk = pl.BlockSpec((tr, LANE), lambda i: (i, 0))
    shp = jax.ShapeDtypeStruct((r, LANE), F32)
    return _pcall(
        body, name=name, grid=(r // tr,),
        in_specs=[pl.BlockSpec((N_DEV, tr, LANE), lambda i: (0, i, 0)), blk, blk, blk],
        out_specs=[blk, blk, blk, blk], out_shape=[shp, shp, shp, shp],
        compiler_params=_params(("parallel",)),
    )(gathered, w, m, v)


def _my_place():
    x, y, c = lax.axis_index("x"), lax.axis_index("y"), lax.axis_index("c")
    return x, y, c, 4 * x + 2 * y + c


def _peer(x, y, c, k):
    px = 1 - x if (k >> 2) & 1 else x
    py = 1 - y if (k >> 1) & 1 else y
    pc = 1 - c if k & 1 else c
    return (px, py, pc), 4 * px + 2 * py + pc


def _slab(ref, shape, kind, p):
    r, cd = shape
    if kind == "row":
        return ref.at[pl.ds(pl.multiple_of(p * r, SUBLANE), r), :]
    return ref.at[:, pl.ds(pl.multiple_of(p * cd, LANE), cd)]


def _exchange_layout(arrays, kinds, gather):
    shard_shapes, dst_kinds, out_shapes = [], [], []
    for a, kind in zip(arrays, kinds):
        r, cd = a.shape
        if gather:
            shard, dst_kind = (r, cd), kind
        else:
            shard, dst_kind = ((r // N_DEV, cd) if kind == "row" else (r, cd // N_DEV)), "row"
        shard_shapes.append(shard)
        dst_kinds.append(dst_kind)
        full = (shard[0] * N_DEV, shard[1]) if dst_kind == "row" else (shard[0], shard[1] * N_DEV)
        out_shapes.append(jax.ShapeDtypeStruct(full, a.dtype))
    return shard_shapes, dst_kinds, out_shapes


def _exchange_copies(ins, outs, send_sems, recv_sems, sem_of, layout, kinds, gather):
    shard_shapes, dst_kinds, _ = layout
    x, y, c, me = _my_place()

    def src_for(a, p):
        return ins[a] if gather else _slab(ins[a], shard_shapes[a], kinds[a], p)

    def dst_slot(a, p):
        return _slab(outs[a], shard_shapes[a], dst_kinds[a], p)

    def remote(a, k, slot):
        peer, pidx = _peer(x, y, c, k)
        return pltpu.make_async_remote_copy(
            src_ref=src_for(a, pidx), dst_ref=dst_slot(a, me if slot == "mine" else pidx),
            send_sem=send_sems.at[sem_of(a, k)], recv_sem=recv_sems.at[sem_of(a, k)],
            device_id=peer, device_id_type=MESH)

    return remote


def _place_own(src, src_kind, slab_shape, dst_kind, full, name, index=None):
    r, cd = slab_shape
    tr = _pick(r, 512)
    nr = r // tr
    me = _my_place()[3] if index is None else index
    src, layer = src if isinstance(src, tuple) else (src, None)

    def body(me_ref, s_ref, o_ref):
        o_ref[...] = s_ref[...].astype(o_ref.dtype)

    def where(kind):
        if kind is None:
            return lambda i, me_ref: (i, 0)
        if kind == "row":
            return lambda i, me_ref: (me_ref[0] * nr + i, 0)
        return lambda i, me_ref: (i, me_ref[0])

    if layer is None:
        src_spec = pl.BlockSpec((tr, cd), where(src_kind))
    else:
        src_spec = pl.BlockSpec((None, tr, cd), lambda i, me_ref: (layer, *where(src_kind)(i, me_ref)))
    return _pcall(
        body, name=name,
        grid_spec=pltpu.PrefetchScalarGridSpec(
            num_scalar_prefetch=1, grid=(nr,),
            in_specs=[src_spec],
            out_specs=pl.BlockSpec((tr, cd), where(dst_kind))),
        out_shape=jax.ShapeDtypeStruct(full.shape, full.dtype),
        compiler_params=_params(("arbitrary",)),
    )(jnp.reshape(me, (1,)).astype(jnp.int32), src)


def _landing_with_own_slab(arrays, kinds, gather, layout, name=None, order_after=None):
    _, _, _, me = _my_place()
    index = None
    if order_after is not None:
        first = order_after.reshape(-1)[0].astype(jnp.int32)
        index = me + jnp.minimum(jnp.maximum(first, 0), 0)
    lands = []
    for a in range(len(arrays)):
        (r, cd), dst_kind, full = layout[0][a], layout[1][a], layout[2][a]
        if name is not None:
            lands.append(_place_own(arrays[a], None if gather else kinds[a], (r, cd), dst_kind, full, name,
                                    index=index))
            continue
        if gather:
            piece = arrays[a]
        elif kinds[a] == "row":
            piece = lax.dynamic_slice_in_dim(arrays[a], me * r, r, axis=0)
        else:
            piece = lax.dynamic_slice_in_dim(arrays[a], me * cd, cd, axis=1)
        at = (me * r, 0) if dst_kind == "row" else (0, me * cd)
        lands.append(lax.dynamic_update_slice(lax.empty(full.shape, full.dtype), piece, at))
    return lands


def _exchange(arrays, kinds, gather, name, after=None):
    n = len(arrays)
    n_extra = 0 if after is None else 1
    layout = _exchange_layout(arrays, kinds, gather)
    lands = _landing_with_own_slab(arrays, kinds, gather, layout)

    def body(*refs):
        ins, outs = refs[:n], refs[2 * n + n_extra:3 * n + n_extra]
        send_sems, recv_sems = refs[3 * n + n_extra:]
        remote = _exchange_copies(ins, outs, send_sems, recv_sems,
                                  lambda a, k: a * (N_DEV - 1) + k - 1, layout, kinds, gather)
        for a in range(n):
            for k in range(1, N_DEV):
                remote(a, k, "mine").start()
        for a in range(n):
            for k in range(1, N_DEV):
                arrival = remote(a, k, "theirs")
                arrival.wait_send()
                arrival.wait_recv()

    anyspec = pl.BlockSpec(memory_space=pl.ANY)
    outs = _pcall(
        body, comm=True, name=name,
        in_specs=[anyspec] * (2 * n + n_extra), out_specs=[anyspec] * n, out_shape=layout[2],
        input_output_aliases={n + a: a for a in range(n)},
        scratch_shapes=[pltpu.SemaphoreType.DMA((n * (N_DEV - 1),)), pltpu.SemaphoreType.DMA((n * (N_DEV - 1),))],
    )(*arrays, *lands, *([] if after is None else [after]))
    return list(outs)


HBM_SPEC = pl.BlockSpec(memory_space=pltpu.HBM)
SEM_SPEC = pl.BlockSpec(memory_space=pltpu.SEMAPHORE)
ANY_SPEC = pl.BlockSpec(memory_space=pl.ANY)
DATAFLOW = pltpu.SideEffectType.DATAFLOW_SIDE_EFFECTING


def _exchange_start(arrays, kinds, gather, name, after, carry=()):
    n, nc = len(arrays), len(carry)
    layout = _exchange_layout(arrays, kinds, gather)
    lands = _landing_with_own_slab(arrays, kinds, gather, layout, "place_own")

    def body(*refs):
        ins, outs = refs[:n], refs[n:2 * n]
        send_sems, recv_sems = refs[2 * n + nc + 1], refs[2 * n + nc + 2]
        token = refs[2 * n + nc + 3 + 2 * n + nc]
        remote = _exchange_copies(ins, outs, send_sems, recv_sems, lambda a, k: a, layout, kinds, gather)
        for a in range(n):
            for k in range(1, N_DEV):
                remote(a, k, "mine").start()
        token[...] = jnp.zeros_like(token)

    passed = list(arrays) + lands + list(carry)
    res = pl.pallas_call(
        body, name=name,
        out_shape=(pltpu.SemaphoreType.DMA((n,)), pltpu.SemaphoreType.DMA((n,)),
                   *[pltpu.HBM(t.shape, t.dtype) for t in passed], jax.ShapeDtypeStruct((SUBLANE, LANE), F32)),
        in_specs=[HBM_SPEC] * (2 * n + nc) + [ANY_SPEC],
        out_specs=(SEM_SPEC, SEM_SPEC, *([HBM_SPEC] * (2 * n + nc)), pl.BlockSpec(memory_space=pltpu.VMEM)),
        input_output_aliases={i: 2 + i for i in range(2 * n + nc)},
        compiler_params=pltpu.CompilerParams(has_side_effects=DATAFLOW),
    )(*[pltpu.with_memory_space_constraint(t, pltpu.HBM) for t in passed], after)
    handle = (res[0], res[1], list(res[2:2 + n]), list(res[2 + n:2 + 2 * n]), tuple(kinds), gather)
    return handle, res[-1], list(res[2 + 2 * n:2 + 2 * n + nc])


def _exchange_wait(handle, name, after):
    send_sems, recv_sems, ins_thru, lands_thru, kinds, gather = handle
    n = len(ins_thru)
    layout = _exchange_layout(ins_thru, kinds, gather)

    def body(*refs):
        ins, outs = refs[:n], refs[n:2 * n]
        s_sems, r_sems = refs[2 * n], refs[2 * n + 1]
        remote = _exchange_copies(ins, outs, s_sems, r_sems, lambda a, k: a, layout, kinds, gather)
        for a in range(n):
            for k in range(1, N_DEV):
                arrival = remote(a, k, "theirs")
                arrival.wait_send()
                arrival.wait_recv()

    res = pl.pallas_call(
        body, name=name,
        out_shape=[pltpu.HBM(t.shape, t.dtype) for t in ins_thru + lands_thru],
        in_specs=[HBM_SPEC] * (2 * n) + [SEM_SPEC, SEM_SPEC, ANY_SPEC],
        out_specs=[HBM_SPEC] * (2 * n),
        input_output_aliases={i: i for i in range(2 * n)},
        compiler_params=pltpu.CompilerParams(has_side_effects=DATAFLOW),
    )(*ins_thru, *lands_thru, send_sems, recv_sems, after)
    return list(res[n:2 * n])


STAGE1_KS = (1, 2, 4, 6)
FORWARD_KS = (2, 4, 6)


def _gather2_copies(lands, shard_shapes, kinds):
    x, y, c, me = _my_place()

    def slab(a, p):
        return _slab(lands[a], shard_shapes[a], kinds[a], p)

    def stage1(a, k, sems, arriving):
        peer, pidx = _peer(x, y, c, k)
        s = slab(a, pidx if arriving else me)
        return pltpu.make_async_remote_copy(src_ref=s, dst_ref=s, send_sem=sems[0].at[a], recv_sem=sems[1].at[a],
                                            device_id=peer, device_id_type=MESH)

    def stage2(a, k, sems, arriving):
        sib, _ = _peer(x, y, c, 1)
        _, mine = _peer(x, y, c, k)
        _, theirs = _peer(x, y, 1 - c, k)
        s = slab(a, theirs if arriving else mine)
        return pltpu.make_async_remote_copy(src_ref=s, dst_ref=s, send_sem=sems[0].at[a], recv_sem=sems[1].at[a],
                                            device_id=sib, device_id_type=MESH)

    return stage1, stage2


def _gather2_call(lands, sems_in, name, after, make_body, returns_sems):
    n = len(lands)
    n_in = len(sems_in)

    def body(*refs):
        land_refs = refs[:n]
        in_sems = refs[n:n + n_in]
        rest = refs[n + n_in + 1:]
        out_sems = rest[:2] if returns_sems else ()
        make_body(land_refs, in_sems, out_sems)
        if returns_sems:
            token = rest[2 + n]
            token[...] = jnp.zeros_like(token)

    sem_shapes = (pltpu.SemaphoreType.DMA((n,)), pltpu.SemaphoreType.DMA((n,))) if returns_sems else ()
    tok_shape = (jax.ShapeDtypeStruct((SUBLANE, LANE), F32),) if returns_sems else ()
    n_sem_out = len(sem_shapes)
    res = pl.pallas_call(
        body, name=name,
        out_shape=(*sem_shapes, *[pltpu.HBM(t.shape, t.dtype) for t in lands], *tok_shape),
        in_specs=[HBM_SPEC] * n + [SEM_SPEC] * n_in + [ANY_SPEC],
        out_specs=(*([SEM_SPEC] * n_sem_out), *([HBM_SPEC] * n),
                   *([pl.BlockSpec(memory_space=pltpu.VMEM)] if returns_sems else [])),
        input_output_aliases={i: n_sem_out + i for i in range(n)},
        compiler_params=pltpu.CompilerParams(has_side_effects=DATAFLOW),
    )(*[pltpu.with_memory_space_constraint(t, pltpu.HBM) for t in lands], *sems_in, after)
    sems_out = tuple(res[:n_sem_out])
    lands_thru = list(res[n_sem_out:n_sem_out + n])
    return sems_out, lands_thru, (res[-1] if returns_sems else None)


def _gather2_start(shards, kinds, name, after):
    n = len(shards)
    views = [jax.ShapeDtypeStruct(t.shape[1:], t.dtype) for t, _ in shards]
    shard_shapes, dst_kinds, fulls = _exchange_layout(views, kinds, True)
    layout = (shard_shapes, dst_kinds, [jax.ShapeDtypeStruct(f.shape, BF16) for f in fulls])
    lands = _landing_with_own_slab(shards, kinds, True, layout, "place_own", order_after=after)

    def make_body(land_refs, in_sems, out_sems):
        stage1, _ = _gather2_copies(land_refs, layout[0], kinds)
        for a in range(n):
            for k in STAGE1_KS:
                stage1(a, k, out_sems, False).start()

    sems, lands, token = _gather2_call(lands, (), name, after, make_body, True)
    return (sems, lands, layout[0], tuple(kinds)), token


def _gather2_forward(handle, name, after, carry=()):
    sems1, lands, shard_shapes, kinds = handle
    n = len(lands)

    def make_body(land_refs, in_sems, out_sems):
        stage1, stage2 = _gather2_copies(land_refs, shard_shapes, kinds)
        for a in range(n):
            for k in STAGE1_KS:
                arrival = stage1(a, k, in_sems, True)
                arrival.wait_send()
                arrival.wait_recv()
        for a in range(n):
            for k in FORWARD_KS:
                stage2(a, k, out_sems, False).start()

    sems2, passed, token = _gather2_call(list(lands) + list(carry), sems1, name, after, make_body, True)
    return (sems2, passed[:n], shard_shapes, kinds), token, passed[n:]


def _gather2_wait(handle, name, after):
    sems2, lands, shard_shapes, kinds = handle
    n = len(lands)

    def make_body(land_refs, in_sems, out_sems):
        _, stage2 = _gather2_copies(land_refs, shard_shapes, kinds)
        for a in range(n):
            for k in FORWARD_KS:
                arrival = stage2(a, k, in_sems, True)
                arrival.wait_send()
                arrival.wait_recv()

    _, lands, _ = _gather2_call(lands, sems2, name, after, make_body, False)
    return lands


N_CHIP = N_DEV // 2


def _scatter2_pair(g, name):
    r, c8 = g.shape
    cd = c8 // N_DEV

    def body(g_ref, o_ref, send_sems, recv_sems):
        x, y, c, _ = _my_place()
        sib, _ = _peer(x, y, c, 1)

        def copy(ch):
            theirs = 2 * ch + (1 - c)
            return pltpu.make_async_remote_copy(
                src_ref=_slab(g_ref, (r, cd), "col", theirs), dst_ref=_slab(o_ref, (r, cd), "row", ch),
                send_sem=send_sems.at[ch], recv_sem=recv_sems.at[ch], device_id=sib, device_id_type=MESH)

        for ch in range(N_CHIP):
            copy(ch).start()
        for ch in range(N_CHIP):
            copy(ch).wait_send()
            copy(ch).wait_recv()

    return _pcall(
        body, comm=True, name=name,
        in_specs=[ANY_SPEC], out_specs=ANY_SPEC, out_shape=jax.ShapeDtypeStruct((N_CHIP * r, cd), g.dtype),
        scratch_shapes=[pltpu.SemaphoreType.DMA((N_CHIP,)), pltpu.SemaphoreType.DMA((N_CHIP,))],
    )(g)


def _scatter2_add(g, from_sibling, name):
    r, c8 = g.shape
    cd = c8 // N_DEV
    tr = _pick(r, 512)
    nr = r // tr
    _, _, core, _ = _my_place()

    def body(c_ref, g_ref, s_ref, o_ref):
        o_ref[...] = (g_ref[...].astype(F32) + s_ref[...].astype(F32)).astype(o_ref.dtype)

    return _pcall(
        body, name=name,
        grid_spec=pltpu.PrefetchScalarGridSpec(
            num_scalar_prefetch=1, grid=(N_CHIP, nr),
            in_specs=[pl.BlockSpec((tr, cd), lambda ch, i, c_ref: (i, 2 * ch + c_ref[0])),
                      pl.BlockSpec((tr, cd), lambda ch, i, c_ref: (ch * nr + i, 0))],
            out_specs=pl.BlockSpec((tr, cd), lambda ch, i, c_ref: (ch * nr + i, 0))),
        out_shape=jax.ShapeDtypeStruct((N_CHIP * r, cd), g.dtype),
        compiler_params=_params(("arbitrary", "arbitrary")),
    )(jnp.reshape(core, (1,)).astype(jnp.int32), g, from_sibling)


def _scatter2_copies(refs, r):
    part_ref, recv_ref = refs
    x, y, c, _ = _my_place()
    my_chip = 2 * x + y

    def rows(ref, ch):
        return ref.at[pl.ds(pl.multiple_of(ch * r, SUBLANE), r), :]

    def copy(k, sems, arriving):
        peer, _ = _peer(x, y, c, k)
        peer_chip = 2 * peer[0] + peer[1]
        return pltpu.make_async_remote_copy(
            src_ref=rows(part_ref, peer_chip), dst_ref=rows(recv_ref, peer_chip if arriving else my_chip),
            send_sem=sems[0].at[0], recv_sem=sems[1].at[0], device_id=peer, device_id_type=MESH)

    return copy


def _scatter2_start(g, name, after, carry=()):
    r = g.shape[0]
    x, y, _, _ = _my_place()
    from_sibling = _scatter2_pair(g, name + "_pair")
    partial = _scatter2_add(g, from_sibling, name + "_add")
    cd = partial.shape[1]
    recv = _place_own(partial, "row", (r, cd), "row", jax.ShapeDtypeStruct(partial.shape, partial.dtype),
                      "place_own", index=2 * x + y)

    def make_body(land_refs, in_sems, out_sems):
        copy = _scatter2_copies(land_refs[:2], r)
        for k in FORWARD_KS:
            copy(k, out_sems, False).start()

    sems, lands, token = _gather2_call([partial, recv] + list(carry), (), name, after, make_body, True)
    return (sems, lands[:2], r), token, lands[2:]


def _scatter2_wait(handle, name, after):
    sems, lands, r = handle

    def make_body(land_refs, in_sems, out_sems):
        copy = _scatter2_copies(land_refs, r)
        for k in FORWARD_KS:
            arrival = copy(k, in_sems, True)
            arrival.wait_send()
            arrival.wait_recv()

    _, lands, _ = _gather2_call(lands, sems, name, after, make_body, False)
    return lands[1]


def _pad_mla_w_in(w):
    d = w.shape[0]
    z = lambda n: jnp.zeros((d, n), w.dtype)
    o1, o2, o3 = Q_RANK, Q_RANK + KV_RANK, Q_RANK + KV_RANK + ROPE
    return jnp.concatenate([w[:, :o1], z(Q_RANK_PAD - Q_RANK), w[:, o1:o2], w[:, o2:o3], z(LANE - ROPE), w[:, o3:]], axis=1)


def _unpad_mla_w_in(g):
    return jnp.concatenate([g[:, :Q_RANK], g[:, PROJ_CKV:PROJ_KR], g[:, PROJ_KR:PROJ_KR + ROPE], g[:, PROJ_Z:]], axis=1)


def _pad_w_uq(w):
    w3 = w.reshape(Q_RANK, HEADS, NOPE + ROPE)
    w3 = jnp.pad(w3, ((0, Q_RANK_PAD - Q_RANK), (0, 0), (0, HEAD_PAD - NOPE - ROPE)))
    return w3.reshape(Q_RANK_PAD, HEADS * HEAD_PAD)


def _unpad_w_uq(g):
    return g[:Q_RANK].reshape(Q_RANK, HEADS, HEAD_PAD)[:, :, :NOPE + ROPE].reshape(Q_RANK, HEADS * (NOPE + ROPE))


def _pack(pieces):
    flat = [p.reshape(-1).astype(F32) for p in pieces]
    tot = sum(f.shape[0] for f in flat)
    unit = SUBLANE * LANE
    padn = (-tot) % unit
    if padn:
        flat.append(jnp.zeros((padn,), F32))
    return jnp.concatenate(flat).reshape(-1, LANE)


def _unpack(packed, shapes, lead=()):
    flat = packed.reshape(tuple(lead) + (-1,))
    out, off = [], 0
    for shp in shapes:
        nel = int(np.prod(shp))
        out.append(flat[..., off:off + nel].reshape(tuple(lead) + tuple(shp)))
        off += nel
    return out


SMALL_GROUPS = {
    "a": [('ada_b', lambda t: t[1:]), ('pre_g', lambda t: t[1:]), ('post_g', lambda t: t[1:]),
          ('sgu_norm_g', lambda t: t[1:]), ('sgu_w_s', lambda t: t[1:]), ('sgu_b_s', lambda t: t[1:]),
          ('mla_q_norm_g', lambda t: t), ('mla_kv_norm_g', lambda t: t)],
    "b": [('ada_b', lambda t: t[0:1, 2 * t.shape[1] // 3:]), ('post_g', lambda t: t[0:1]),
          ('sgu_norm_g', lambda t: t[0:1]), ('sgu_w_s', lambda t: t[0:1]), ('sgu_b_s', lambda t: t[0:1])],
    "c": [('ada_b', lambda t: t[0:1, :2 * t.shape[1] // 3]), ('pre_g', lambda t: t[0:1])],
}


WEIGHTS = ['ada_w', 'ada_b', 'pre_g', 'post_g', 'sgu_w_in', 'sgu_norm_g', 'sgu_w_s', 'sgu_b_s', 'sgu_w_out',
           'mla_w_in', 'mla_q_norm_g', 'mla_kv_norm_g', 'mla_w_uq', 'mla_w_ukv', 'mla_w_out']
INPUTS = ['x', 'c'] + WEIGHTS + ['loss_target'] + ['m_' + n for n in WEIGHTS] + ['v_' + n for n in WEIGHTS]


def kernel(x, c, ada_w, ada_b, pre_g, post_g, sgu_w_in, sgu_norm_g, sgu_w_s, sgu_b_s, sgu_w_out, mla_w_in, mla_q_norm_g, mla_kv_norm_g, mla_w_uq, mla_w_ukv, mla_w_out, loss_target, m_ada_w, m_ada_b, m_pre_g, m_post_g, m_sgu_w_in, m_sgu_norm_g, m_sgu_w_s, m_sgu_b_s, m_sgu_w_out, m_mla_w_in, m_mla_q_norm_g, m_mla_kv_norm_g, m_mla_w_uq, m_mla_w_ukv, m_mla_w_out, v_ada_w, v_ada_b, v_pre_g, v_post_g, v_sgu_w_in, v_sgu_norm_g, v_sgu_w_s, v_sgu_b_s, v_sgu_w_out, v_mla_w_in, v_mla_q_norm_g, v_mla_kv_norm_g, v_mla_w_uq, v_mla_w_ukv, v_mla_w_out):
    given = locals()
    A = {name: given[name] for name in INPUTS}
    x0 = A['x'][0]
    tgt = A['loss_target'][0]
    s, d = x0.shape
    e = 2 * d
    ncol = 3 * d // N_DEV
    _, _, _, me = _my_place()
    ktabs = _rope_tables(s)

    gains = jnp.zeros((SUBLANE, LANE), F32)
    gains = gains.at[0:2, :Q_RANK // N_DEV].set(A['mla_q_norm_g'])
    gains = gains.at[2:4, :KV_RANK // N_DEV].set(A['mla_kv_norm_g'])
    c8 = jnp.broadcast_to(A['c'], (SUBLANE, d))
    cg, gg = _exchange([c8, gains], ["row", "row"], True, "ag_cond")
    cond_raw = cg.reshape(N_DEV, SUBLANE, d)[:, 0, :]
    gg = gg.reshape(N_DEV, SUBLANE, LANE)
    gq_full = jnp.transpose(gg[:, 0:2, :Q_RANK // N_DEV], (1, 0, 2)).reshape(N_MIX, Q_RANK)
    gkv_full = jnp.transpose(gg[:, 2:4, :KV_RANK // N_DEV], (1, 0, 2)).reshape(N_MIX, KV_RANK)
    gq_pad = jnp.pad(gq_full, ((0, 0), (0, Q_RANK_PAD - Q_RANK)))

    bias_my = lax.dynamic_slice_in_dim(A['ada_b'], me * ncol, ncol, axis=1)
    mod_part = _ada_mod(cond_raw, A['ada_w'], bias_my, "ada_mod")
    send = jnp.pad(jnp.transpose(mod_part, (1, 0, 2)), ((0, 0), (0, SUBLANE - DEPTH), (0, 0)))
    (rb,) = _exchange([send.reshape(N_DEV * SUBLANE, ncol)], ["row"], False, "a2a_mod")

    token = rb
    gathers = {}
    for i in range(DEPTH):
        j = i // N_MIX
        if i % N_MIX == 0:
            parts = [("in", [(A['sgu_w_in'], j)], ["col"]), ("out", [(A['sgu_w_out'], j)], ["row"])]
        else:
            parts = [("all", [(A['mla_w_in'], j), (A['mla_w_uq'], j), (A['mla_w_ukv'], j), (A['mla_w_out'], j)],
                      ["col", "col", "col", "row"])]
        for part, shards, kinds in parts:
            gathers[(i, part)], token = _gather2_start(shards, kinds, f"ag_start_{i}_{part}", token)

    def forward_gathers(i, carried):
        for key in [k for k in gathers if k[0] == i]:
            gathers[key], _, (carried,) = _gather2_forward(gathers[key], f"ag_forward_{key[0]}_{key[1]}", token,
                                                           carry=[carried])
        return carried

    mod = jnp.transpose(rb.reshape(N_DEV, SUBLANE, ncol)[:, :DEPTH, :], (1, 0, 2)).reshape(DEPTH, 3 * d) + token[0, 0]
    shift = [mod[i:i + 1, :d] for i in range(DEPTH)]
    scale = [mod[i:i + 1, d:2 * d] for i in range(DEPTH)]
    gate = [mod[i:i + 1, 2 * d:] for i in range(DEPTH)]

    saved = []
    x = x0
    h = _pre_fwd(x, A['pre_g'][0:1], scale[0], shift[0], "pre_fwd")
    for i in range(DEPTH):
        j = i // N_MIX
        if i == 0:
            gathers[(0, "in")], _, (h,) = _gather2_forward(gathers[(0, "in")], "ag_forward_0_in", token, carry=[h])
        if i % N_MIX == 0:
            (w_in,) = _gather2_wait(gathers[(i, "in")], f"ag_wait_{i}_in", h)
            uvz = _mm(h, w_in, "nn", F32, "sgu_in")
            if i == 0:
                gathers[(0, "out")], _, (uvz,) = _gather2_forward(gathers[(0, "out")], "ag_forward_0_out", token,
                                                                  carry=[uvz])
            bias_full = jnp.repeat(A['sgu_b_s'][j].T, e // SGU_GROUPS, axis=1)
            ng = A['sgu_norm_g'][j:j + 1]
            y = _sgu_mid_fwd(uvz, ng, A['sgu_w_s'][j], bias_full, "sgu_mid_fwd")
            y = forward_gathers(i + 1, y)
            (w_out,) = _gather2_wait(gathers[(i, "out")], f"ag_wait_{i}_out", y)
            out = _mm(y, w_out, "nn", F32, "sgu_out")
            saved.append(dict(x=x, h=h, uvz=uvz, y=y, out=out, w_in=w_in, w_out=w_out, bias=bias_full, ng=ng))
        else:
            w_in, w_uq, w_ukv, w_out = _gather2_wait(gathers[(i, "all")], f"ag_wait_{i}_all", h)
            w_in = _pad_mla_w_in(w_in)
            w_uq = _pad_w_uq(w_uq)
            gq, gkv = gq_pad[j:j + 1], gkv_full[j:j + 1]
            proj = _mm(h, w_in, "nn", F32, "mla_in")
            qn, kvn, kr = _mla_norm_fwd(proj, gq, gkv, ktabs, "mla_norm_fwd")
            q = _mm(qn, w_uq, "nn", F32, "mla_uq")
            kv = _mm(kvn, w_ukv, "nn", BF16, "mla_ukv")
            o, y, lse, q_cat = _attn_fwd(q, kv, kr, proj, ktabs, "attn_fwd")
            y = forward_gathers(i + 1, y)
            out = _mm(y, w_out, "nn", F32, "mla_out")
            saved.append(dict(x=x, h=h, proj=proj, qn=qn, kvn=kvn, kr=kr, q_cat=q_cat, kv=kv, o=o, y=y, lse=lse,
                              out=out, w_in=w_in, w_uq=w_uq, w_ukv=w_ukv, w_out=w_out, gq=gq, gkv=gkv))
        if i + 1 < DEPTH:
            x, h = _post_pre_fwd(x, out, gate[i], A['post_g'][i:i + 1], A['pre_g'][i + 1:i + 2], scale[i + 1],
                                 shift[i + 1], "post_pre_fwd")
        else:
            x = _post_fwd(x, out, gate[i], A['post_g'][i:i + 1], "post_fwd")

    dx, loss_row = _loss_head(x, tgt, "loss_head")
    loss = lax.psum(loss_row[0, 0], ("x", "y", "c"))

    d_shift, d_scale, d_gate = [None] * DEPTH, [None] * DEPTH, [None] * DEPTH
    d_pre, d_post = [None] * DEPTH, [None] * DEPTH
    scatters, scatters_out = [None] * DEPTH, [None] * DEPTH
    small, small_grads, small_handles = {}, {}, {}
    last = DEPTH - 1
    dout, d_gate[last], d_post[last] = _post_bwd(dx, saved[last]['out'], gate[last], A['post_g'][last:], "post_bwd")
    for i in reversed(range(DEPTH)):
        j = i // N_MIX
        sv = saved[i]
        if i % N_MIX == 0:
            dy = _mm(dout, sv['w_out'], "nt", F32, "sgu_out_dx")
            g_w_out = _mm(sv['y'], dout, "tn", BF16, "sgu_out_dw")
            scatters_out[i], token, (dy,) = _exchange_start([g_w_out], ["row"], False, f"rs_out_start_{i}", token,
                                                            carry=[dy])
            duvz, dws, dbs, dng = _sgu_mid_bwd(sv['uvz'], dy, sv['ng'], A['sgu_w_s'][j], sv['bias'], "sgu_mid_bwd")
            small[('sgu', j)] = (dws, dbs.reshape(SGU_GROUPS, SGU_BLOCK), dng)
            g_w_in = _mm(sv['h'], duvz, "tn", BF16, "sgu_in_dw")
            if i == 0:
                scatters[i], token, (duvz,) = _scatter2_start(g_w_in, "rs2_start_0", token, carry=[duvz])
                small_grads["b"] = [d_gate[0], d_post[0], dng, dws[None], small[('sgu', 0)][1][None]]
                small_handles["b"], token, (duvz,) = _exchange_start(
                    [_pack(small_grads["b"])], ["row"], True, "ag_small_b_start", token, carry=[duvz])
            else:
                scatters[i], token, (duvz,) = _exchange_start([g_w_in], ["col"], False, f"rs_start_{i}", token,
                                                              carry=[duvz])
            dh = _mm(duvz, sv['w_in'], "nt", F32, "sgu_in_dx")
        else:
            dy = _mm(dout, sv['w_out'], "nt", F32, "mla_out_dx")
            g_w_out = _mm(sv['y'], dout, "tn", BF16, "mla_out_dw")
            scatters_out[i], token, (dy,) = _exchange_start([g_w_out], ["row"], False, f"rs_out_start_{i}", token,
                                                            carry=[dy])
            do, dproj = _mla_gate_bwd(dy, sv['o'], sv['proj'], "mla_gate_bwd")
            dq_b, dkv, dkr_heads = _attn_bwd(sv['q_cat'], sv['kv'], sv['kr'], do, sv['o'], sv['lse'], ktabs,
                                             "attn_bwd")
            dqn = _mm(dq_b, sv['w_uq'], "nt", F32, "mla_uq_dx")
            g_w_uq = _unpad_w_uq(_mm(sv['qn'], dq_b, "tn", BF16, "mla_uq_dw"))
            dkvn = _mm(dkv, sv['w_ukv'], "nt", F32, "mla_ukv_dx")
            g_w_ukv = _mm(sv['kvn'], dkv, "tn", BF16, "mla_ukv_dw")
            dproj, dgq, dgkv = _mla_norm_bwd(dqn, dkvn, dkr_heads, sv['proj'], sv['gq'], sv['gkv'], ktabs, dproj,
                                             "mla_norm_bwd")
            g_w_in = _unpad_mla_w_in(_mm(sv['h'], dproj, "tn", BF16, "mla_in_dw"))
            scatters[i], token, (dproj,) = _exchange_start(
                [g_w_in, g_w_uq, g_w_ukv], ["col", "col", "col"], False, f"rs_start_{i}", token, carry=[dproj])
            dh = _mm(dproj, sv['w_in'], "nt", F32, "mla_in_dx")
            small[('mla', j)] = (dgq[:, :Q_RANK], dgkv)
        if i > 0:
            (dx, d_shift[i], d_scale[i], d_pre[i], dout, d_gate[i - 1], d_post[i - 1]) = _pre_post_bwd(
                dh, sv['x'], dx, A['pre_g'][i:i + 1], scale[i], saved[i - 1]['out'], gate[i - 1],
                A['post_g'][i - 1:i], "pre_post_bwd")
        else:
            dx, d_shift[i], d_scale[i], d_pre[i] = _pre_bwd(dh, sv['x'], dx, A['pre_g'][i:i + 1], scale[i], "pre_bwd")
        if i == 1:
            small_grads["a"] = [
                jnp.concatenate([jnp.concatenate([d_shift[l], d_scale[l], d_gate[l]], axis=1)
                                 for l in range(1, DEPTH)], axis=0),
                jnp.concatenate(d_pre[1:], axis=0), jnp.concatenate(d_post[1:], axis=0),
                small[('sgu', 1)][2], small[('sgu', 1)][0][None], small[('sgu', 1)][1][None],
                jnp.concatenate([small[('mla', jj)][0] for jj in range(N_MIX)], axis=0),
                jnp.concatenate([small[('mla', jj)][1] for jj in range(N_MIX)], axis=0)]
            small_handles["a"], token, (dx,) = _exchange_start(
                [_pack(small_grads["a"])], ["row"], True, "ag_small_a_start", token, carry=[dx])

    res = {}

    def big(name, recv0, recv1):
        res[name] = _adam_reduce(recv0, recv1, A[name], A['m_' + name], A['v_' + name], "adam_" + name)

    def small_full(name, prefix):
        t = A[prefix + name]
        if name in ('mla_q_norm_g', 'mla_kv_norm_g'):
            t = lax.dynamic_update_slice_in_dim(jnp.zeros((t.shape[0], t.shape[1] * N_DEV), F32), t,
                                                me * t.shape[1], axis=1)
        return t

    def finish(i, after):
        first = ([_scatter2_wait(scatters[i], "rs2_wait_0", after)] if i == 0
                 else _exchange_wait(scatters[i], f"rs_wait_{i}", after))
        return first + _exchange_wait(scatters_out[i], f"rs_out_wait_{i}", after)

    recv_mla = {j: finish(N_MIX * j + 1, dx) for j in reversed(range(N_MIX))}
    for idx_w, name in enumerate(['mla_w_in', 'mla_w_uq', 'mla_w_ukv', 'mla_w_out']):
        big(name, recv_mla[0][idx_w], recv_mla[1][idx_w])

    recv_sgu = {j: finish(N_MIX * j, res['mla_w_out'][0]) for j in reversed(range(N_MIX))}
    for idx_w, name in enumerate(['sgu_w_in', 'sgu_w_out']):
        big(name, recv_sgu[0][idx_w], recv_sgu[1][idx_w])

    small_grads["c"] = [jnp.concatenate([d_shift[0], d_scale[0]], axis=1), d_pre[0]]
    gathered = {tag: _exchange_wait(small_handles[tag], f"ag_small_{tag}_wait", res['sgu_w_in'][0])[0]
                for tag in ("a", "b")}
    (gathered["c"],) = _exchange([_pack(small_grads["c"])], ["row"], True, "ag_small_c", after=res['sgu_w_in'][0])

    upd, parts = {}, {}
    for tag, group in SMALL_GROUPS.items():
        shapes = [g.shape for g in small_grads[tag]]
        packs = [_pack([pick(small_full(name, prefix)) for name, pick in group]) for prefix in ('', 'm_', 'v_')]
        outs4 = _adam_small(gathered[tag].reshape(N_DEV, -1, LANE), *packs, "adam_small_" + tag)
        upd[tag] = [_unpack(t, shapes) for t in outs4]
        parts[tag] = _unpack(gathered[tag].reshape(N_DEV, -1), shapes, lead=(N_DEV,))
    for k_out in range(4):
        ab_rest, pre_rest, post_rest, ng_1, ws_1, bs_1, gq_all, gkv_all = upd["a"][k_out]
        ab_gate0, post_0, ng_0, ws_0, bs_0 = upd["b"][k_out]
        ab_0, pre_0 = upd["c"][k_out]
        for name, val in (
                ('ada_b', jnp.concatenate([jnp.concatenate([ab_0, ab_gate0], axis=1), ab_rest], axis=0)),
                ('pre_g', jnp.concatenate([pre_0, pre_rest], axis=0)),
                ('post_g', jnp.concatenate([post_0, post_rest], axis=0)),
                ('sgu_norm_g', jnp.concatenate([ng_0, ng_1], axis=0)),
                ('sgu_w_s', jnp.concatenate([ws_0, ws_1], axis=0)),
                ('sgu_b_s', jnp.concatenate([bs_0, bs_1], axis=0)),
                ('mla_q_norm_g', gq_all), ('mla_kv_norm_g', gkv_all)):
            if name in ('mla_q_norm_g', 'mla_kv_norm_g'):
                wdt = A[name].shape[1]
                val = lax.dynamic_slice_in_dim(val, me * wdt, wdt, axis=1)
            res.setdefault(name, [None] * 4)[k_out] = val

    dmod_all = jnp.concatenate([jnp.concatenate([parts["c"][0], parts["b"][0]], axis=2), parts["a"][0]], axis=1)
    dmod_cols = jnp.transpose(lax.dynamic_slice_in_dim(dmod_all, me * ncol, ncol, axis=2), (1, 0, 2))
    res['ada_w'] = _ada_bwd_adam(jnp.transpose(cond_raw), dmod_cols, A['ada_w'], A['m_ada_w'], A['v_ada_w'], "ada_bwd")

    outs = [loss, dx[None]]
    for k_out in range(4):
        outs += [res[n][k_out] for n in WEIGHTS]
    return tuple(outs)
```

```python
import functools
import math

import numpy as np
import jax
import jax.numpy as jnp
from jax import lax
from jax.experimental import pallas as pl
from jax.experimental.pallas import tpu as pltpu

F32 = jnp.float32
BF16 = jnp.bfloat16
MESH = pl.DeviceIdType.MESH

N_DEV = 8
DEPTH = 4
N_MIX = 2
NORM_EPS = 1e-6
CHUNK = 64
SGU_BLOCK = 128
SGU_GROUPS = 16
HEADS = 16
Q_RANK = 448
Q_RANK_PAD = 512
KV_RANK = 512
NOPE = 128
ROPE = 64
HALF = ROPE // 2
V_DIM = 128
HEAD_PAD = 256
ROPE_THETA = 10000.0
MLA_WIDTH = HEADS * V_DIM
LANE = 128
SUBLANE = 8
PROJ_CQ = 0
PROJ_CKV = Q_RANK_PAD
PROJ_KR = Q_RANK_PAD + KV_RANK
PROJ_Z = PROJ_KR + LANE
PROJ_W = PROJ_Z + MLA_WIDTH

ADAM_LR = 0.001
ADAM_B1 = 0.9
ADAM_B2 = 0.999
ADAM_EPS = 1e-08
ADAM_WD = 0.01
ADAM_STEP = 10

VMEM_LIMIT = 56 * 1024 * 1024
ATT_BLK = 512
ATT_SUB = 256
ATT_HEADS_PER_STEP = 2
ROW_BLK = 512
MM_TM, MM_TN, MM_TK = 1024, 1024, 2048
MM_TILE_BYTES = 40 * 1024 * 1024
SOFTMAX_SCALE = (NOPE + ROPE) ** -0.5
LOG2_E = 1.0 / math.log(2.0)
INV_SQRT2 = 1.0 / math.sqrt(2.0)
INV_SQRT_2PI = 1.0 / math.sqrt(2.0 * math.pi)


def _pcall(body, comm=False, **kw):
    return pl.pallas_call(body, **kw)


def _params(sem=None):
    return pltpu.CompilerParams(dimension_semantics=sem, vmem_limit_bytes=VMEM_LIMIT)


def _pick(dim, pref):
    if dim <= pref:
        return dim
    t = (pref // LANE) * LANE
    while t >= LANE:
        if dim % t == 0:
            return t
        t -= LANE
    return dim


def _gelu(x):
    return 0.5 * x * (1.0 + lax.erf(x * INV_SQRT2))


def _gelu_grad(x):
    return 0.5 * (1.0 + lax.erf(x * INV_SQRT2)) + x * jnp.exp(-0.5 * x * x) * INV_SQRT_2PI


def _sigmoid(x):
    return 1.0 / (1.0 + jnp.exp(-x))


def _dot_nt(a, b):
    return lax.dot_general(a, b, (((1,), (1,)), ((), ())), preferred_element_type=F32)


def _dot_tn(a, b):
    return lax.dot_general(a, b, (((0,), (0,)), ((), ())), preferred_element_type=F32)


def _mm(a, b, dims, out_dtype, name):
    if dims == "nn":
        (m, k), (k2, n) = a.shape, b.shape
    elif dims == "nt":
        (m, k), (n, k2) = a.shape, b.shape
    else:
        (k, m), (k2, n) = a.shape, b.shape
    assert k == k2, (a.shape, b.shape, dims)
    tm, tn = _pick(m, MM_TM), _pick(n, MM_TN)
    out_bytes = 2 * tm * tn * jnp.dtype(out_dtype).itemsize
    whole_k = 2 * (tm + tn) * k * a.dtype.itemsize + out_bytes <= MM_TILE_BYTES
    tk = k if whole_k else _pick(k, MM_TK)
    nk = k // tk

    def body(a_ref, b_ref, o_ref, *scratch):
        if dims == "nn":
            p = jnp.dot(a_ref[...], b_ref[...], preferred_element_type=F32)
        elif dims == "nt":
            p = _dot_nt(a_ref[...], b_ref[...])
        else:
            p = _dot_tn(a_ref[...], b_ref[...])
        if nk == 1:
            o_ref[...] = p.astype(o_ref.dtype)
            return
        acc_ref, = scratch
        kk = pl.program_id(2)

        @pl.when(kk == 0)
        def _():
            acc_ref[...] = p

        @pl.when(kk > 0)
        def _():
            acc_ref[...] += p

        @pl.when(kk == nk - 1)
        def _():
            o_ref[...] = acc_ref[...].astype(o_ref.dtype)

    if dims == "tn":
        a_spec = pl.BlockSpec((tk, tm), lambda i, j, kk: (kk, i))
    else:
        a_spec = pl.BlockSpec((tm, tk), lambda i, j, kk: (i, kk))
    if dims == "nt":
        b_spec = pl.BlockSpec((tn, tk), lambda i, j, kk: (j, kk))
    else:
        b_spec = pl.BlockSpec((tk, tn), lambda i, j, kk: (kk, j))
    return _pcall(
        body, name=name,
        grid=(m // tm, n // tn, nk),
        in_specs=[a_spec, b_spec],
        out_specs=pl.BlockSpec((tm, tn), lambda i, j, kk: (i, j)),
        out_shape=jax.ShapeDtypeStruct((m, n), out_dtype),
        scratch_shapes=[pltpu.VMEM((tm, tn), F32)] if nk > 1 else [],
        compiler_params=_params(("parallel", "parallel", "arbitrary")),
    )(a, b)


def _row_spec(ts, d):
    return pl.BlockSpec((ts, d), lambda i: (i, 0))


def _vec_spec(d):
    return pl.BlockSpec((1, d), lambda i: (0, 0))


def _pre_fwd(x, g, scale, shift, name):
    s, d = x.shape
    ts = _pick(s, ROW_BLK)

    def body(x_ref, g_ref, sc_ref, sh_ref, h_ref):
        xv = x_ref[...]
        r = lax.rsqrt(jnp.mean(xv * xv, axis=-1, keepdims=True) + NORM_EPS)
        h_ref[...] = ((xv * r * g_ref[...]) * (1.0 + sc_ref[...]) + sh_ref[...]).astype(BF16)

    return _pcall(
        body, name=name, grid=(s // ts,),
        in_specs=[_row_spec(ts, d), _vec_spec(d), _vec_spec(d), _vec_spec(d)],
        out_specs=_row_spec(ts, d),
        out_shape=jax.ShapeDtypeStruct((s, d), BF16),
        compiler_params=_params(("parallel",)),
    )(x, g, scale, shift)


def _pre_bwd(dh, x, dxo, g, scale, name):
    s, d = x.shape
    ts = _pick(s, ROW_BLK)

    def body(dh_ref, x_ref, dxo_ref, g_ref, sc_ref, dx_ref, dsh_ref, dsc_ref, dg_ref):
        i = pl.program_id(0)

        @pl.when(i == 0)
        def _():
            dsh_ref[...] = jnp.zeros_like(dsh_ref)
            dsc_ref[...] = jnp.zeros_like(dsc_ref)
            dg_ref[...] = jnp.zeros_like(dg_ref)

        xv = x_ref[...]
        dhv = dh_ref[...]
        gv = g_ref[...]
        one_sc = 1.0 + sc_ref[...]
        r = lax.rsqrt(jnp.mean(xv * xv, axis=-1, keepdims=True) + NORM_EPS)
        n = xv * r
        dsh_ref[...] += jnp.sum(dhv, axis=0, keepdims=True)
        dsc_ref[...] += jnp.sum(dhv * (n * gv), axis=0, keepdims=True)
        dng = dhv * one_sc
        dg_ref[...] += jnp.sum(dng * n, axis=0, keepdims=True)
        dn = dng * gv
        dx_ref[...] = dxo_ref[...] + r * (dn - n * jnp.mean(dn * n, axis=-1, keepdims=True))

    return _pcall(
        body, name=name, grid=(s // ts,),
        in_specs=[_row_spec(ts, d), _row_spec(ts, d), _row_spec(ts, d), _vec_spec(d), _vec_spec(d)],
        out_specs=[_row_spec(ts, d), _vec_spec(d), _vec_spec(d), _vec_spec(d)],
        out_shape=[jax.ShapeDtypeStruct((s, d), F32)] + [jax.ShapeDtypeStruct((1, d), F32)] * 3,
        compiler_params=_params(("arbitrary",)),
    )(dh, x, dxo, g, scale)


def _last_post_loss_bwd(x, out, gate, post_g, tgt, name):
    s, d = x.shape
    ts = _pick(s, ROW_BLK // 2)
    ns = s // ts

    def body(x_ref, o_ref, gate_ref, pg_ref, t_ref, dx_ref, do_ref, dgate_ref, dpg_ref, loss_ref, acc_ref):
        i = pl.program_id(0)

        @pl.when(i == 0)
        def _():
            for ref in (dgate_ref, dpg_ref, acc_ref):
                ref[...] = jnp.zeros_like(ref)

        o = o_ref[...]
        pgv = pg_ref[...]
        gatev = gate_ref[...]
        r = lax.rsqrt(jnp.mean(o * o, axis=-1, keepdims=True) + NORM_EPS)
        n = o * r
        err = (x_ref[...] + gatev * (n * pgv)) - t_ref[...]
        acc_ref[...] += jnp.sum(err * err, axis=0, keepdims=True)
        dx = err * (1.0 / d)
        dx_ref[...] = dx
        dyn = dx * gatev
        dgate_ref[...] += jnp.sum(dx * (n * pgv), axis=0, keepdims=True)
        dpg_ref[...] += jnp.sum(dyn * n, axis=0, keepdims=True)
        dn = dyn * pgv
        do_ref[...] = (r * (dn - n * jnp.mean(dn * n, axis=-1, keepdims=True))).astype(BF16)

        @pl.when(i == ns - 1)
        def _():
            tot = jnp.sum(acc_ref[...], axis=1, keepdims=True) * (0.5 / d)
            loss_ref[...] = jnp.broadcast_to(tot, loss_ref.shape)

    vec = jax.ShapeDtypeStruct((1, d), F32)
    return _pcall(
        body, name=name, grid=(ns,),
        in_specs=[_row_spec(ts, d), _row_spec(ts, d), _vec_spec(d), _vec_spec(d), _row_spec(ts, d)],
        out_specs=[_row_spec(ts, d), _row_spec(ts, d), _vec_spec(d), _vec_spec(d),
                   pl.BlockSpec((1, LANE), lambda i: (0, 0))],
        out_shape=[jax.ShapeDtypeStruct((s, d), F32), jax.ShapeDtypeStruct((s, d), BF16), vec, vec,
                   jax.ShapeDtypeStruct((1, LANE), F32)],
        scratch_shapes=[pltpu.VMEM((1, d), F32)],
        compiler_params=_params(("arbitrary",)),
    )(x, out, gate, post_g, tgt)


def _post_pre_fwd(x, out, gate, post_g, pre_g, scale, shift, name):
    s, d = x.shape
    ts = _pick(s, ROW_BLK // 2)

    def body(x_ref, o_ref, gate_ref, pg_ref, g_ref, sc_ref, sh_ref, y_ref, h_ref):
        o = o_ref[...]
        r = lax.rsqrt(jnp.mean(o * o, axis=-1, keepdims=True) + NORM_EPS)
        xn = x_ref[...] + gate_ref[...] * (o * r * pg_ref[...])
        y_ref[...] = xn
        r2 = lax.rsqrt(jnp.mean(xn * xn, axis=-1, keepdims=True) + NORM_EPS)
        h_ref[...] = ((xn * r2 * g_ref[...]) * (1.0 + sc_ref[...]) + sh_ref[...]).astype(BF16)

    return _pcall(
        body, name=name, grid=(s // ts,),
        in_specs=[_row_spec(ts, d), _row_spec(ts, d)] + [_vec_spec(d)] * 5,
        out_specs=[_row_spec(ts, d), _row_spec(ts, d)],
        out_shape=[jax.ShapeDtypeStruct((s, d), F32), jax.ShapeDtypeStruct((s, d), BF16)],
        compiler_params=_params(("parallel",)),
    )(x, out, gate, post_g, pre_g, scale, shift)


def _pre_post_bwd(dh, x, dxo, pre_g, scale, out, gate, post_g, name):
    s, d = x.shape
    ts = _pick(s, ROW_BLK // 2)

    def body(dh_ref, x_ref, dxo_ref, g_ref, sc_ref, o_ref, gate_ref, pg_ref,
             dx_ref, dsh_ref, dsc_ref, dg_ref, do_ref, dgate_ref, dpg_ref):
        i = pl.program_id(0)

        @pl.when(i == 0)
        def _():
            for ref in (dsh_ref, dsc_ref, dg_ref, dgate_ref, dpg_ref):
                ref[...] = jnp.zeros_like(ref)

        xv = x_ref[...]
        dhv = dh_ref[...]
        gv = g_ref[...]
        one_sc = 1.0 + sc_ref[...]
        r = lax.rsqrt(jnp.mean(xv * xv, axis=-1, keepdims=True) + NORM_EPS)
        n = xv * r
        dsh_ref[...] += jnp.sum(dhv, axis=0, keepdims=True)
        dsc_ref[...] += jnp.sum(dhv * (n * gv), axis=0, keepdims=True)
        dng = dhv * one_sc
        dg_ref[...] += jnp.sum(dng * n, axis=0, keepdims=True)
        dn = dng * gv
        dx = dxo_ref[...] + r * (dn - n * jnp.mean(dn * n, axis=-1, keepdims=True))
        dx_ref[...] = dx

        o = o_ref[...]
        pgv = pg_ref[...]
        ro = lax.rsqrt(jnp.mean(o * o, axis=-1, keepdims=True) + NORM_EPS)
        no = o * ro
        dyn = dx * gate_ref[...]
        dgate_ref[...] += jnp.sum(dx * (no * pgv), axis=0, keepdims=True)
        dpg_ref[...] += jnp.sum(dyn * no, axis=0, keepdims=True)
        dno = dyn * pgv
        do_ref[...] = (ro * (dno - no * jnp.mean(dno * no, axis=-1, keepdims=True))).astype(BF16)

    vec = jax.ShapeDtypeStruct((1, d), F32)
    return _pcall(
        body, name=name, grid=(s // ts,),
        in_specs=[_row_spec(ts, d)] * 3 + [_vec_spec(d)] * 2 + [_row_spec(ts, d)] + [_vec_spec(d)] * 2,
        out_specs=[_row_spec(ts, d)] + [_vec_spec(d)] * 3 + [_row_spec(ts, d)] + [_vec_spec(d)] * 2,
        out_shape=[jax.ShapeDtypeStruct((s, d), F32), vec, vec, vec, jax.ShapeDtypeStruct((s, d), BF16), vec, vec],
        compiler_params=_params(("arbitrary",)),
    )(dh, x, dxo, pre_g, scale, out, gate, post_g)


def _sgu_mask():
    t = lax.broadcasted_iota(jnp.int32, (SGU_BLOCK, SGU_BLOCK), 0) // CHUNK
    s = lax.broadcasted_iota(jnp.int32, (SGU_BLOCK, SGU_BLOCK), 1) // CHUNK
    return s <= t


def _sgu_norm(v_pre, g):
    e = v_pre.shape[-1]
    vg = _gelu(v_pre)
    mu = jnp.sum(vg, axis=-1, keepdims=True) * (1.0 / e)
    dlt = vg - mu
    var = jnp.sum(dlt * dlt, axis=-1, keepdims=True) * (1.0 / e)
    rstd = lax.rsqrt(var + NORM_EPS)
    vhat = dlt * rstd
    return vhat, rstd, (vhat * g).astype(BF16)


def _sgu_mid_fwd(uvz, norm_g, w_s, bias_full, name):
    s, e3 = uvz.shape
    e = e3 // 3
    gd = e // SGU_GROUPS
    nb = s // SGU_BLOCK

    def body(uvz_ref, g_ref, w_ref, b_ref, y_ref, wsc):
        @pl.when(pl.program_id(0) == 0)
        def _():
            msk = _sgu_mask()
            for gi in range(SGU_GROUPS):
                wsc[gi] = jnp.where(msk, w_ref[gi], 0.0).astype(BF16)

        _, _, vb = _sgu_norm(uvz_ref[:, e:2 * e], g_ref[...])
        for gi in range(SGU_GROUPS):
            lo = gi * gd
            vm = jnp.dot(wsc[gi], vb[:, lo:lo + gd], preferred_element_type=F32) + b_ref[:, lo:lo + gd]
            zg = uvz_ref[:, 2 * e + lo:2 * e + lo + gd]
            y_ref[:, lo:lo + gd] = (_gelu(uvz_ref[:, lo:lo + gd]) * vm * (zg * _sigmoid(zg))).astype(BF16)

    return _pcall(
        body, name=name, grid=(nb,),
        in_specs=[pl.BlockSpec((SGU_BLOCK, e3), lambda n: (n, 0)),
                  pl.BlockSpec((1, e), lambda n: (0, 0)),
                  pl.BlockSpec((SGU_GROUPS, SGU_BLOCK, SGU_BLOCK), lambda n: (0, 0, 0)),
                  pl.BlockSpec((SGU_BLOCK, e), lambda n: (0, 0))],
        out_specs=pl.BlockSpec((SGU_BLOCK, e), lambda n: (n, 0)),
        out_shape=jax.ShapeDtypeStruct((s, e), BF16),
        scratch_shapes=[pltpu.VMEM((SGU_GROUPS, SGU_BLOCK, SGU_BLOCK), BF16)],
        compiler_params=_params(("arbitrary",)),
    )(uvz, norm_g, w_s, bias_full)


def _sgu_mid_bwd(uvz, dy, norm_g, w_s, bias_full, name):
    s, e3 = uvz.shape
    e = e3 // 3
    gd = e // SGU_GROUPS
    nb = s // SGU_BLOCK

    def body(uvz_ref, dy_ref, g_ref, w_ref, b_ref, d_ref, dw_ref, db_ref, dg_ref, wsc, wtsc, dvh_sc, dbacc):
        n = pl.program_id(0)

        @pl.when(n == 0)
        def _():
            msk = _sgu_mask()
            for gi in range(SGU_GROUPS):
                wm = jnp.where(msk, w_ref[gi], 0.0)
                wsc[gi] = wm.astype(BF16)
                wtsc[gi] = wm.T.astype(BF16)
            dw_ref[...] = jnp.zeros_like(dw_ref)
            dg_ref[...] = jnp.zeros_like(dg_ref)
            dbacc[...] = jnp.zeros_like(dbacc)

        v_pre = uvz_ref[:, e:2 * e]
        gv = g_ref[...]
        vhat, rstd, vb = _sgu_norm(v_pre, gv)
        s1 = jnp.zeros((SGU_BLOCK, 1), F32)
        s2 = jnp.zeros((SGU_BLOCK, 1), F32)
        for gi in range(SGU_GROUPS):
            lo = gi * gd
            u_pre = uvz_ref[:, lo:lo + gd]
            zg = uvz_ref[:, 2 * e + lo:2 * e + lo + gd]
            dyg = dy_ref[:, lo:lo + gd]
            ug = _gelu(u_pre)
            sig = _sigmoid(zg)
            vbg = vb[:, lo:lo + gd]
            vhg = vhat[:, lo:lo + gd]
            vm = jnp.dot(wsc[gi], vbg, preferred_element_type=F32) + b_ref[:, lo:lo + gd]
            t = dyg * (zg * sig)
            d_ref[:, lo:lo + gd] = (t * vm * _gelu_grad(u_pre)).astype(BF16)
            dvm = t * ug
            d_ref[:, 2 * e + lo:2 * e + lo + gd] = (dyg * ug * vm * (sig * (1.0 + zg * (1.0 - sig)))).astype(BF16)
            dvm_b = dvm.astype(BF16)
            dv = jnp.dot(wtsc[gi], dvm_b, preferred_element_type=F32)
            dw_ref[gi] += _dot_nt(dvm_b, vbg)
            dbacc[:, lo:lo + gd] += dvm
            dg_ref[:, lo:lo + gd] += jnp.sum(dv * vhg, axis=0, keepdims=True)
            dvh = dv * gv[:, lo:lo + gd]
            dvh_sc[:, lo:lo + gd] = dvh
            s1 = s1 + jnp.sum(dvh, axis=-1, keepdims=True)
            s2 = s2 + jnp.sum(dvh * vhg, axis=-1, keepdims=True)
        dvg = rstd * (dvh_sc[...] - s1 * (1.0 / e) - vhat * (s2 * (1.0 / e)))
        d_ref[:, e:2 * e] = (dvg * _gelu_grad(v_pre)).astype(BF16)

        @pl.when(n == nb - 1)
        def _():
            msk = _sgu_mask()
            for gi in range(SGU_GROUPS):
                dw_ref[gi] = jnp.where(msk, dw_ref[gi], 0.0)
                db_ref[gi] = jnp.sum(dbacc[:, gi * gd:(gi + 1) * gd], axis=1, keepdims=True)

    return _pcall(
        body, name=name, grid=(nb,),
        in_specs=[pl.BlockSpec((SGU_BLOCK, e3), lambda n: (n, 0)),
                  pl.BlockSpec((SGU_BLOCK, e), lambda n: (n, 0)),
                  pl.BlockSpec((1, e), lambda n: (0, 0)),
                  pl.BlockSpec((SGU_GROUPS, SGU_BLOCK, SGU_BLOCK), lambda n: (0, 0, 0)),
                  pl.BlockSpec((SGU_BLOCK, e), lambda n: (0, 0))],
        out_specs=[pl.BlockSpec((SGU_BLOCK, e3), lambda n: (n, 0)),
                   pl.BlockSpec((SGU_GROUPS, SGU_BLOCK, SGU_BLOCK), lambda n: (0, 0, 0)),
                   pl.BlockSpec((SGU_GROUPS, SGU_BLOCK, 1), lambda n: (0, 0, 0)),
                   pl.BlockSpec((1, e), lambda n: (0, 0))],
        out_shape=[jax.ShapeDtypeStruct((s, e3), BF16),
                   jax.ShapeDtypeStruct((SGU_GROUPS, SGU_BLOCK, SGU_BLOCK), F32),
                   jax.ShapeDtypeStruct((SGU_GROUPS, SGU_BLOCK, 1), F32),
                   jax.ShapeDtypeStruct((1, e), F32)],
        scratch_shapes=[pltpu.VMEM((SGU_GROUPS, SGU_BLOCK, SGU_BLOCK), BF16),
                        pltpu.VMEM((SGU_GROUPS, SGU_BLOCK, SGU_BLOCK), BF16),
                        pltpu.VMEM((SGU_BLOCK, e), F32),
                        pltpu.VMEM((SGU_BLOCK, e), F32)],
        compiler_params=_params(("arbitrary",)),
    )(uvz, dy, norm_g, w_s, bias_full)


def _rope_tables(s):
    pos = jnp.arange(s, dtype=F32)
    inv_freq = ROPE_THETA ** (-jnp.arange(0, ROPE, 2, dtype=F32) / ROPE)
    ang = pos[:, None] * inv_freq[None, :]
    cos, sin = jnp.cos(ang), jnp.sin(ang)
    z32 = jnp.zeros((s, HALF), F32)
    z64 = jnp.zeros((s, ROPE), F32)
    ck = jnp.concatenate([cos, cos, z64], axis=1)
    s1k = jnp.concatenate([-sin, z32, z64], axis=1)
    s2k = jnp.concatenate([z32, sin, z64], axis=1)
    return ck, s1k, s2k


def _rot(x, c, s1, s2):
    w = x.shape[-1]
    return x * c + pltpu.roll(x, w - HALF, 1) * s1 + pltpu.roll(x, HALF, 1) * s2


def _rms(cv, n_real):
    r = lax.rsqrt(jnp.sum(cv * cv, axis=-1, keepdims=True) * (1.0 / n_real) + NORM_EPS)
    return r, cv * r


def _mla_norm_fwd(proj, gq, gkv, tabs, name):
    s = proj.shape[0]
    ts = _pick(s, ROW_BLK)
    ck, s1k, s2k = tabs

    def body(cq_ref, ckv_ref, kr_ref, gq_ref, gkv_ref, c_ref, s1_ref, s2_ref, qn_ref, kvn_ref, kro_ref):
        _, nq = _rms(cq_ref[...], Q_RANK)
        qn_ref[...] = (nq * gq_ref[...]).astype(BF16)
        _, nkv = _rms(ckv_ref[...], KV_RANK)
        kvn_ref[...] = (nkv * gkv_ref[...]).astype(BF16)
        kro_ref[...] = _rot(kr_ref[...], c_ref[...], s1_ref[...], s2_ref[...]).astype(BF16)

    tab = pl.BlockSpec((ts, LANE), lambda i: (i, 0))
    return _pcall(
        body, name=name, grid=(s // ts,),
        in_specs=[pl.BlockSpec((ts, Q_RANK_PAD), lambda i: (i, 0)),
                  pl.BlockSpec((ts, KV_RANK), lambda i: (i, PROJ_CKV // KV_RANK)),
                  pl.BlockSpec((ts, LANE), lambda i: (i, PROJ_KR // LANE)),
                  _vec_spec(Q_RANK_PAD), _vec_spec(KV_RANK), tab, tab, tab],
        out_specs=[pl.BlockSpec((ts, Q_RANK_PAD), lambda i: (i, 0)),
                   pl.BlockSpec((ts, KV_RANK), lambda i: (i, 0)), tab],
        out_shape=[jax.ShapeDtypeStruct((s, Q_RANK_PAD), BF16), jax.ShapeDtypeStruct((s, KV_RANK), BF16),
                   jax.ShapeDtypeStruct((s, LANE), BF16)],
        compiler_params=_params(("parallel",)),
    )(proj, proj, proj, gq, gkv, ck, s1k, s2k)


def _transpose_bf16(t):
    return t.astype(F32).T.astype(BF16)


def _diag_mask(tb, transposed):
    r = lax.broadcasted_iota(jnp.int32, (tb, tb), 0) // CHUNK
    c = lax.broadcasted_iota(jnp.int32, (tb, tb), 1) // CHUNK
    return (r <= c) if transposed else (c <= r)


def _attn_fwd(q, kv, kr, proj, tabs, name):
    s = q.shape[0]
    tb = _pick(s, ATT_BLK)
    nb = s // tb
    zcol = PROJ_Z // V_DIM
    mult = SOFTMAX_SCALE * LOG2_E

    hp = ATT_HEADS_PER_STEP

    def body(qf_ref, c_ref, s1_ref, s2_ref, kv_ref, kr_ref, *rest):
        z_refs = rest[:hp]
        o_ref, y_ref, lse_ref, q_ref = rest[hp:hp + 4]
        scratch = rest[hp + 4:]
        kt_sc, vx_sc, m_sc, acc_sc, sa_sc, sb_sc = (scratch[i * hp:(i + 1) * hp] for i in range(6))
        qi = pl.program_id(1)
        heads = range(hp)
        for hh in heads:
            lo = hh * HEAD_PAD
            q_ref[:, lo:lo + NOPE] = (qf_ref[:, lo:lo + NOPE] * mult).astype(BF16)
            q_ref[:, lo + NOPE:lo + HEAD_PAD] = (
                _rot(qf_ref[:, lo + NOPE:lo + HEAD_PAD], c_ref[...], s1_ref[...], s2_ref[...]) * mult).astype(BF16)

        @pl.when(qi == 0)
        def _():
            for hh in heads:
                kcol = hh * (NOPE + V_DIM)
                for b in range(nb):
                    rows = slice(b * tb, (b + 1) * tb)
                    kt_sc[hh][b] = _transpose_bf16(
                        jnp.concatenate([kv_ref[rows, kcol:kcol + NOPE], kr_ref[rows, :]], axis=1))
                    vx_sc[hh][b] = jnp.concatenate(
                        [kv_ref[rows, kcol + NOPE:kcol + NOPE + V_DIM], jnp.ones((tb, V_DIM), BF16)], axis=1)

        for hh in heads:
            m_sc[hh][...] = jnp.full_like(m_sc[hh], -1e30)
            acc_sc[hh][...] = jnp.zeros_like(acc_sc[hh])
        sub = min(tb, ATT_SUB)

        def scores(ki, bufs):
            for hh in heads:
                bufs[hh][...] = jnp.dot(q_ref[:, hh * HEAD_PAD:(hh + 1) * HEAD_PAD], kt_sc[hh][ki],
                                        preferred_element_type=F32)

        def step(ki, bufs, masked):
            for r in range(tb // sub):
                rs = slice(r * sub, (r + 1) * sub)
                for hh in heads:
                    sc = bufs[hh][rs, :]
                    if masked:
                        sc = jnp.where(_diag_mask(tb, False)[rs, :], sc, -1e30)
                    m_prev = m_sc[hh][rs, :]
                    m_new = jnp.maximum(m_prev, jnp.max(sc, axis=-1, keepdims=True))
                    p = jnp.exp2(sc - m_new).astype(BF16)
                    acc_sc[hh][rs, :] = (jnp.exp2(m_prev - m_new) * acc_sc[hh][rs, :]
                                         + jnp.dot(p, vx_sc[hh][ki], preferred_element_type=F32))
                    m_sc[hh][rs, :] = m_new

        def pair(t, carry):
            scores(2 * t + 1, sb_sc)
            step(2 * t, sa_sc, False)
            scores(2 * t + 2, sa_sc)
            step(2 * t + 1, sb_sc, False)
            return carry

        scores(0, sa_sc)
        lax.fori_loop(0, qi // 2, pair, 0)

        @pl.when(qi % 2 == 1)
        def _():
            scores(qi, sb_sc)
            step(qi - 1, sa_sc, False)
            step(qi, sb_sc, True)

        @pl.when(qi % 2 == 0)
        def _():
            step(qi, sa_sc, True)

        for hh in heads:
            l = acc_sc[hh][:, V_DIM:V_DIM + 1]
            o = acc_sc[hh][:, :V_DIM] / l
            z = z_refs[hh][...]
            o_ref[:, hh * V_DIM:(hh + 1) * V_DIM] = o.astype(BF16)
            y_ref[:, hh * V_DIM:(hh + 1) * V_DIM] = (o * (z * _sigmoid(z))).astype(BF16)
            lse_cols = jnp.broadcast_to(m_sc[hh][...] + jnp.log2(l), (tb, LANE))
            lse_ref[hh] = lse_cols.T[0:1, :]

    oblk = pl.BlockSpec((tb, hp * V_DIM), lambda g, qi: (qi, g))
    qblk = pl.BlockSpec((tb, hp * HEAD_PAD), lambda g, qi: (qi, g))
    tab = pl.BlockSpec((tb, LANE), lambda g, qi: (qi, 0))
    per_head = lambda shape, dtype: [pltpu.VMEM(shape, dtype)] * hp
    return _pcall(
        body, name=name, grid=(HEADS // hp, nb),
        in_specs=[qblk, tab, tab, tab,
                  pl.BlockSpec((s, hp * (NOPE + V_DIM)), lambda g, qi: (0, g)),
                  pl.BlockSpec((s, LANE), lambda g, qi: (0, 0))]
                 + [pl.BlockSpec((tb, V_DIM), functools.partial(lambda g, qi, hh: (qi, zcol + hp * g + hh), hh=hh))
                    for hh in range(hp)],
        out_specs=[oblk, oblk, pl.BlockSpec((hp, None, 1, tb), lambda g, qi: (g, qi, 0, 0)), qblk],
        out_shape=[jax.ShapeDtypeStruct((s, MLA_WIDTH), BF16), jax.ShapeDtypeStruct((s, MLA_WIDTH), BF16),
                   jax.ShapeDtypeStruct((HEADS, nb, 1, tb), F32), jax.ShapeDtypeStruct((s, HEADS * HEAD_PAD), BF16)],
        scratch_shapes=(per_head((nb, HEAD_PAD, tb), BF16) + per_head((nb, tb, HEAD_PAD), BF16)
                        + per_head((tb, 1), F32) + per_head((tb, HEAD_PAD), F32)
                        + per_head((tb, tb), F32) + per_head((tb, tb), F32)),
        compiler_params=_params(("parallel", "arbitrary")),
    )(q, *tabs, kv, kr, *([proj] * hp))


def _attn_bwd(q_cat, kv, kr, do, o, lse, tabs, name):
    s = q_cat.shape[0]
    tb = _pick(s, ATT_BLK)
    nb = s // tb
    ln2 = math.log(2.0)

    hp = ATT_HEADS_PER_STEP
    heads = range(hp)

    def body(q_ref, do_ref, o_ref, lse_ref, kv_ref, kr_ref, c_ref, s1_ref, s2_ref, dq_ref, dkv_ref, dkr_ref, *scratch):
        qt_sc, dot_sc, delta_sc, dqt_sc, dk_sc, dv_sc = (scratch[i * hp:(i + 1) * hp] for i in range(6))
        ki = pl.program_id(1)
        qcols = lambda hh: slice(hh * HEAD_PAD, (hh + 1) * HEAD_PAD)
        vcols = lambda hh: slice(hh * V_DIM, (hh + 1) * V_DIM)

        @pl.when(ki == 0)
        def _():
            for hh in heads:
                for b in range(nb):
                    rows = slice(b * tb, (b + 1) * tb)
                    qt_sc[hh][b] = _transpose_bf16(q_ref[rows, qcols(hh)])
                    do_t = do_ref[rows, vcols(hh)].astype(F32).T
                    dot_sc[hh][b] = do_t.astype(BF16)
                    delta_sc[hh][b] = jnp.sum(do_t * o_ref[rows, vcols(hh)].astype(F32).T, axis=0, keepdims=True)
                dqt_sc[hh][...] = jnp.zeros_like(dqt_sc[hh])

        k, kt, vb = [], [], []
        for hh in heads:
            kcol = hh * (NOPE + V_DIM)
            k.append(jnp.concatenate([kv_ref[:, kcol:kcol + NOPE], kr_ref[...]], axis=1))
            kt.append(_transpose_bf16(k[hh]))
            vb.append(kv_ref[:, kcol + NOPE:kcol + NOPE + V_DIM])
            dk_sc[hh][...] = jnp.zeros_like(dk_sc[hh])
            dv_sc[hh][...] = jnp.zeros_like(dv_sc[hh])

        def step(qi, masked):
            rows = pl.ds(pl.multiple_of(qi * tb, tb), tb)
            for hh in heads:
                sc_t = jnp.dot(k[hh], qt_sc[hh][qi], preferred_element_type=F32)
                if masked:
                    sc_t = jnp.where(_diag_mask(tb, True), sc_t, -1e30)
                p_t = jnp.exp2(sc_t - lse_ref[hh, qi])
                dp_t = jnp.dot(vb[hh], dot_sc[hh][qi], preferred_element_type=F32)
                ds_t = (p_t * (dp_t - delta_sc[hh][qi])).astype(BF16)
                dv_sc[hh][...] += jnp.dot(p_t.astype(BF16), do_ref[rows, vcols(hh)], preferred_element_type=F32)
                dk_sc[hh][...] += jnp.dot(ds_t, q_ref[rows, qcols(hh)], preferred_element_type=F32)
                dqt_sc[hh][qi] += jnp.dot(kt[hh], ds_t, preferred_element_type=F32)

        step(ki, True)

        def loop_body(qi, carry):
            step(qi, False)
            return carry

        lax.fori_loop(ki + 1, nb, loop_body, 0)

        for hh in heads:
            lo = hh * HEAD_PAD
            dkv_ref[:, lo:lo + NOPE] = (dk_sc[hh][:, :NOPE] * ln2).astype(BF16)
            dkv_ref[:, lo + NOPE:lo + HEAD_PAD] = dv_sc[hh][...].astype(BF16)
            dkr_ref[hh] = dk_sc[hh][:, NOPE:] * ln2

        @pl.when(ki == nb - 1)
        def _():
            for hh in heads:
                lo = hh * HEAD_PAD
                for b in range(nb):
                    rows = slice(b * tb, (b + 1) * tb)
                    dq = dqt_sc[hh][b].T
                    dq_ref[rows, lo:lo + NOPE] = (dq[:, :NOPE] * SOFTMAX_SCALE).astype(BF16)
                    dq_ref[rows, lo + NOPE:lo + HEAD_PAD] = (
                        _rot(dq[:, NOPE:], c_ref[rows, :], -s1_ref[rows, :], -s2_ref[rows, :])
                        * SOFTMAX_SCALE).astype(BF16)

    tab = pl.BlockSpec((s, LANE), lambda g, ki: (0, 0), pipeline_mode=pl.Buffered(1))
    per_head = lambda shape, dtype: [pltpu.VMEM(shape, dtype)] * hp
    return _pcall(
        body, name=name, grid=(HEADS // hp, nb),
        in_specs=[pl.BlockSpec((s, hp * HEAD_PAD), lambda g, ki: (0, g)),
                  pl.BlockSpec((s, hp * V_DIM), lambda g, ki: (0, g)),
                  pl.BlockSpec((s, hp * V_DIM), lambda g, ki: (0, g)),
                  pl.BlockSpec((hp, nb, 1, tb), lambda g, ki: (g, 0, 0, 0)),
                  pl.BlockSpec((tb, hp * (NOPE + V_DIM)), lambda g, ki: (ki, g)),
                  pl.BlockSpec((tb, LANE), lambda g, ki: (ki, 0)), tab, tab, tab],
        out_specs=[pl.BlockSpec((s, hp * HEAD_PAD), lambda g, ki: (0, g)),
                   pl.BlockSpec((tb, hp * HEAD_PAD), lambda g, ki: (ki, g)),
                   pl.BlockSpec((hp, tb, LANE), lambda g, ki: (g, ki, 0))],
        out_shape=[jax.ShapeDtypeStruct((s, HEADS * HEAD_PAD), BF16),
                   jax.ShapeDtypeStruct((s, HEADS * HEAD_PAD), BF16),
                   jax.ShapeDtypeStruct((HEADS, s, LANE), F32)],
        scratch_shapes=(per_head((nb, HEAD_PAD, tb), BF16) + per_head((nb, V_DIM, tb), BF16)
                        + per_head((nb, 1, tb), F32) + per_head((nb, HEAD_PAD, tb), F32)
                        + per_head((tb, HEAD_PAD), F32) + per_head((tb, V_DIM), F32)),
        compiler_params=_params(("parallel", "arbitrary")),
    )(q_cat, do, o, lse, kv, kr, *tabs)


def _mla_gate_bwd(dy, o, proj, name):
    s = dy.shape[0]
    ts = _pick(s, ROW_BLK)

    def body(dy_ref, o_ref, p_ref, do_ref, dp_ref):
        z = p_ref[:, PROJ_Z:]
        dyv = dy_ref[...]
        sig = _sigmoid(z)
        do_ref[...] = (dyv * (z * sig)).astype(BF16)
        dp_ref[:, :PROJ_Z] = jnp.zeros((ts, PROJ_Z), BF16)
        dp_ref[:, PROJ_Z:] = (dyv * o_ref[...].astype(F32) * (sig * (1.0 + z * (1.0 - sig)))).astype(BF16)

    blk = pl.BlockSpec((ts, MLA_WIDTH), lambda i: (i, 0))
    wide = pl.BlockSpec((ts, PROJ_W), lambda i: (i, 0))
    return _pcall(
        body, name=name, grid=(s // ts,),
        in_specs=[blk, blk, wide],
        out_specs=[blk, wide],
        out_shape=[jax.ShapeDtypeStruct((s, MLA_WIDTH), BF16), jax.ShapeDtypeStruct((s, PROJ_W), BF16)],
        compiler_params=_params(("parallel",)),
    )(dy, o, proj)


def _mla_norm_bwd(dqn, dkvn, dkr_heads, proj, gq, gkv, tabs, dproj, name):
    s = proj.shape[0]
    ts = _pick(s, ROW_BLK)
    ck, s1k, s2k = tabs

    def rms_bwd(cv, dn_in, g, n_real):
        r, n = _rms(cv, n_real)
        dg = jnp.sum(dn_in * n, axis=0, keepdims=True)
        dn = dn_in * g
        dc = r * (dn - n * (jnp.sum(dn * n, axis=-1, keepdims=True) * (1.0 / n_real)))
        return dc, dg

    def body(dqn_ref, dkvn_ref, dkr_ref, cq_ref, ckv_ref, gq_ref, gkv_ref, c_ref, s1_ref, s2_ref, dp_in_ref,
             dp_ref, dgq_ref, dgkv_ref):
        @pl.when(pl.program_id(0) == 0)
        def _():
            dgq_ref[...] = jnp.zeros_like(dgq_ref)
            dgkv_ref[...] = jnp.zeros_like(dgkv_ref)

        dcq, dgq = rms_bwd(cq_ref[...], dqn_ref[...], gq_ref[...], Q_RANK)
        dckv, dgkv = rms_bwd(ckv_ref[...], dkvn_ref[...], gkv_ref[...], KV_RANK)
        dgq_ref[...] += dgq
        dgkv_ref[...] += dgkv
        dkr = dkr_ref[0]
        for h in range(1, HEADS):
            dkr = dkr + dkr_ref[h]
        dp_ref[:, PROJ_CQ:PROJ_CKV] = dcq.astype(BF16)
        dp_ref[:, PROJ_CKV:PROJ_KR] = dckv.astype(BF16)
        dp_ref[:, PROJ_KR:PROJ_Z] = _rot(dkr, c_ref[...], -s1_ref[...], -s2_ref[...]).astype(BF16)

    tab = pl.BlockSpec((ts, LANE), lambda i: (i, 0))
    return _pcall(
        body, name=name, grid=(s // ts,),
        in_specs=[pl.BlockSpec((ts, Q_RANK_PAD), lambda i: (i, 0)),
                  pl.BlockSpec((ts, KV_RANK), lambda i: (i, 0)),
                  pl.BlockSpec((HEADS, ts, LANE), lambda i: (0, i, 0)),
                  pl.BlockSpec((ts, Q_RANK_PAD), lambda i: (i, 0)),
                  pl.BlockSpec((ts, KV_RANK), lambda i: (i, PROJ_CKV // KV_RANK)),
                  _vec_spec(Q_RANK_PAD), _vec_spec(KV_RANK), tab, tab, tab, pl.BlockSpec(memory_space=pl.ANY)],
        out_specs=[pl.BlockSpec((ts, PROJ_Z), lambda i: (i, 0)), _vec_spec(Q_RANK_PAD), _vec_spec(KV_RANK)],
        out_shape=[jax.ShapeDtypeStruct((s, PROJ_W), BF16), jax.ShapeDtypeStruct((1, Q_RANK_PAD), F32),
                   jax.ShapeDtypeStruct((1, KV_RANK), F32)],
        input_output_aliases={10: 0},
        compiler_params=_params(("arbitrary",)),
    )(dqn, dkvn, dkr_heads, proj, proj, gq, gkv, ck, s1k, s2k, dproj)


def _ada_mod(cond_raw, ada_w, bias_my, name):
    nl, d, ncol = ada_w.shape
    tk = _pick(d, 512)
    nk = d // tk

    def body(c_ref, w_ref, b_ref, o_ref, acc_ref):
        kk = pl.program_id(1)

        @pl.when(kk == 0)
        def _():
            acc_ref[...] = jnp.zeros_like(acc_ref)

        cv = c_ref[...]
        cond = (cv * _sigmoid(cv)).astype(BF16)
        acc_ref[...] += jnp.dot(cond, w_ref[...].astype(BF16), preferred_element_type=F32)

        @pl.when(kk == nk - 1)
        def _():
            o_ref[...] = acc_ref[...] + b_ref[...]

    return _pcall(
        body, name=name, grid=(nl, nk),
        in_specs=[pl.BlockSpec((N_DEV, tk), lambda l, kk: (0, kk)),
                  pl.BlockSpec((None, tk, ncol), lambda l, kk: (l, kk, 0)),
                  pl.BlockSpec((None, 1, ncol), lambda l, kk: (l, 0, 0))],
        out_specs=pl.BlockSpec((None, N_DEV, ncol), lambda l, kk: (l, 0, 0)),
        out_shape=jax.ShapeDtypeStruct((nl, N_DEV, ncol), F32),
        scratch_shapes=[pltpu.VMEM((N_DEV, ncol), F32)],
        compiler_params=_params(("parallel", "arbitrary")),
    )(cond_raw, ada_w, bias_my.reshape(nl, 1, ncol))


def _adam(w, g, m, v):
    m = ADAM_B1 * m + (1.0 - ADAM_B1) * g
    v = ADAM_B2 * v + (1.0 - ADAM_B2) * (g * g)
    m_hat = m / (1.0 - ADAM_B1 ** ADAM_STEP)
    v_hat = v / (1.0 - ADAM_B2 ** ADAM_STEP)
    delta = -ADAM_LR * (m_hat / (jnp.sqrt(v_hat) + ADAM_EPS) + ADAM_WD * w)
    return delta, m, v


def _ada_bwd_adam(cond_t, dmod_cols, w, m, v, name):
    nl, d, ncol = w.shape
    tk = _pick(d, 512)

    def body(c_ref, dm_ref, w_ref, m_ref, v_ref, g_ref, d_ref, mo_ref, vo_ref):
        cv = c_ref[...]
        cond = (cv * _sigmoid(cv)).astype(BF16)
        g = jnp.dot(cond, dm_ref[...].astype(BF16), preferred_element_type=F32)
        delta, m2, v2 = _adam(w_ref[...], g, m_ref[...], v_ref[...])
        g_ref[...] = g
        d_ref[...] = delta
        mo_ref[...] = m2
        vo_ref[...] = v2

    blk = pl.BlockSpec((None, tk, ncol), lambda l, kk: (l, kk, 0))
    shp = jax.ShapeDtypeStruct((nl, d, ncol), F32)
    return _pcall(
        body, name=name, grid=(nl, d // tk),
        in_specs=[pl.BlockSpec((tk, N_DEV), lambda l, kk: (kk, 0)),
                  pl.BlockSpec((None, N_DEV, ncol), lambda l, kk: (l, 0, 0)), blk, blk, blk],
        out_specs=[blk, blk, blk, blk], out_shape=[shp, shp, shp, shp],
        compiler_params=_params(("parallel", "parallel")),
    )(cond_t, dmod_cols, w, m, v)


def _adam_reduce(recv0, recv1, w, m, v, name):
    nl, r, c = w.shape
    tr = _pick(r, 128) if r % 128 == 0 else r
    tc = _pick(c, 1024)
    n0, n1 = recv0.shape[0] // r, recv1.shape[0] // r

    def body(r0_ref, r1_ref, w_ref, m_ref, v_ref, g_ref, d_ref, mo_ref, vo_ref):
        l = pl.program_id(0)

        def run(rr):
            g = rr[0].astype(F32)
            for sidx in range(1, rr.shape[0]):
                g = g + rr[sidx].astype(F32)
            delta, m2, v2 = _adam(w_ref[...], g, m_ref[...], v_ref[...])
            g_ref[...] = g
            d_ref[...] = delta
            mo_ref[...] = m2
            vo_ref[...] = v2

        @pl.when(l == 0)
        def _():
            run(r0_ref)

        @pl.when(l == 1)
        def _():
            run(r1_ref)

    def rblk(n, layer):
        return pl.BlockSpec((n, tr, tc), lambda l, i, j: (0, jnp.where(l == layer, i, 0), jnp.where(l == layer, j, 0)))

    blk = pl.BlockSpec((None, tr, tc), lambda l, i, j: (l, i, j))
    shp = jax.ShapeDtypeStruct((nl, r, c), F32)
    return _pcall(
        body, name=name, grid=(nl, r // tr, c // tc),
        in_specs=[rblk(n0, 0), rblk(n1, 1), blk, blk, blk],
        out_specs=[blk, blk, blk, blk], out_shape=[shp, shp, shp, shp],
        compiler_params=_params(("arbitrary", "parallel", "parallel")),
    )(recv0.reshape(n0, r, c), recv1.reshape(n1, r, c), w, m, v)


def _adam_small(gathered, w, m, v, name):
    r = w.shape[0]
    tr = _pick(r, 512) if r % 512 == 0 else r

    def body(p_ref, w_ref, m_ref, v_ref, g_ref, d_ref, mo_ref, vo_ref):
        g = p_ref[0]
        for sidx in range(1, N_DEV):
            g = g + p_ref[sidx]
        delta, m2, v2 = _adam(w_ref[...], g, m_ref[...], v_ref[...])
        g_ref[...] = g
        d_ref[...] = delta
        mo_ref[...] = m2
        vo_ref[...] = v2

    blk = pl.BlockSpec((tr, LANE), lambda i: (i, 0))
    shp = jax.ShapeDtypeStruct((r, LANE), F32)
    return _pcall(
        body, name=name, grid=(r // tr,),
        in_specs=[pl.BlockSpec((N_DEV, tr, LANE), lambda i: (0, i, 0)), blk, blk, blk],
        out_specs=[blk, blk, blk, blk], out_shape=[shp, shp, shp, shp],
        compiler_params=_params(("parallel",)),
    )(gathered, w, m, v)


def _my_place():
    x, y, c = lax.axis_index("x"), lax.axis_index("y"), lax.axis_index("c")
    return x, y, c, 4 * x + 2 * y + c


def _peer(x, y, c, k):
    px = 1 - x if (k >> 2) & 1 else x
    py = 1 - y if (k >> 1) & 1 else y
    pc = 1 - c if k & 1 else c
    return (px, py, pc), 4 * px + 2 * py + pc


def _slab(ref, shape, kind, p):
    r, cd = shape
    if kind == "row":
        return ref.at[pl.ds(pl.multiple_of(p * r, SUBLANE), r), :]
    return ref.at[:, pl.ds(pl.multiple_of(p * cd, LANE), cd)]


def _exchange_layout(arrays, kinds, gather):
    shard_shapes, dst_kinds, out_shapes = [], [], []
    for a, kind in zip(arrays, kinds):
        r, cd = a.shape
        if gather:
            shard, dst_kind = (r, cd), kind
        else:
            shard, dst_kind = ((r // N_DEV, cd) if kind == "row" else (r, cd // N_DEV)), "row"
        shard_shapes.append(shard)
        dst_kinds.append(dst_kind)
        full = (shard[0] * N_DEV, shard[1]) if dst_kind == "row" else (shard[0], shard[1] * N_DEV)
        out_shapes.append(jax.ShapeDtypeStruct(full, a.dtype))
    return shard_shapes, dst_kinds, out_shapes


def _exchange_copies(ins, outs, send_sems, recv_sems, sem_of, layout, kinds, gather):
    shard_shapes, dst_kinds, _ = layout
    x, y, c, me = _my_place()

    def src_for(a, p):
        return ins[a] if gather else _slab(ins[a], shard_shapes[a], kinds[a], p)

    def dst_slot(a, p):
        return _slab(outs[a], shard_shapes[a], dst_kinds[a], p)

    def remote(a, k, slot):
        peer, pidx = _peer(x, y, c, k)
        return pltpu.make_async_remote_copy(
            src_ref=src_for(a, pidx), dst_ref=dst_slot(a, me if slot == "mine" else pidx),
            send_sem=send_sems.at[sem_of(a, k)], recv_sem=recv_sems.at[sem_of(a, k)],
            device_id=peer, device_id_type=MESH)

    return remote


def _place_own(src, src_kind, slab_shape, dst_kind, full, name, index=None):
    r, cd = slab_shape
    tr = _pick(r, 512)
    nr = r // tr
    me = _my_place()[3] if index is None else index
    src, layer = src if isinstance(src, tuple) else (src, None)

    def body(me_ref, s_ref, o_ref):
        o_ref[...] = s_ref[...].astype(o_ref.dtype)

    def where(kind):
        if kind is None:
            return lambda i, me_ref: (i, 0)
        if kind == "row":
            return lambda i, me_ref: (me_ref[0] * nr + i, 0)
        return lambda i, me_ref: (i, me_ref[0])

    if layer is None:
        src_spec = pl.BlockSpec((tr, cd), where(src_kind))
    else:
        src_spec = pl.BlockSpec((None, tr, cd), lambda i, me_ref: (layer, *where(src_kind)(i, me_ref)))
    return _pcall(
        body, name=name,
        grid_spec=pltpu.PrefetchScalarGridSpec(
            num_scalar_prefetch=1, grid=(nr,),
            in_specs=[src_spec],
            out_specs=pl.BlockSpec((tr, cd), where(dst_kind))),
        out_shape=jax.ShapeDtypeStruct(full.shape, full.dtype),
        compiler_params=_params(("arbitrary",)),
    )(jnp.reshape(me, (1,)).astype(jnp.int32), src)


def _landing_with_own_slab(arrays, kinds, gather, layout, name=None, order_after=None):
    _, _, _, me = _my_place()
    index = None
    if order_after is not None:
        first = order_after.reshape(-1)[0].astype(jnp.int32)
        index = me + jnp.minimum(jnp.maximum(first, 0), 0)
    lands = []
    for a in range(len(arrays)):
        (r, cd), dst_kind, full = layout[0][a], layout[1][a], layout[2][a]
        if name is not None:
            lands.append(_place_own(arrays[a], None if gather else kinds[a], (r, cd), dst_kind, full, name,
                                    index=index))
            continue
        if gather:
            piece = arrays[a]
        elif kinds[a] == "row":
            piece = lax.dynamic_slice_in_dim(arrays[a], me * r, r, axis=0)
        else:
            piece = lax.dynamic_slice_in_dim(arrays[a], me * cd, cd, axis=1)
        at = (me * r, 0) if dst_kind == "row" else (0, me * cd)
        lands.append(lax.dynamic_update_slice(lax.empty(full.shape, full.dtype), piece, at))
    return lands


def _exchange(arrays, kinds, gather, name, after=None):
    n = len(arrays)
    n_extra = 0 if after is None else 1
    layout = _exchange_layout(arrays, kinds, gather)
    lands = _landing_with_own_slab(arrays, kinds, gather, layout)

    def body(*refs):
        ins, outs = refs[:n], refs[2 * n + n_extra:3 * n + n_extra]
        send_sems, recv_sems = refs[3 * n + n_extra:]
        remote = _exchange_copies(ins, outs, send_sems, recv_sems,
                                  lambda a, k: a * (N_DEV - 1) + k - 1, layout, kinds, gather)
        for a in range(n):
            for k in range(1, N_DEV):
                remote(a, k, "mine").start()
        for a in range(n):
            for k in range(1, N_DEV):
                arrival = remote(a, k, "theirs")
                arrival.wait_send()
                arrival.wait_recv()

    anyspec = pl.BlockSpec(memory_space=pl.ANY)
    outs = _pcall(
        body, comm=True, name=name,
        in_specs=[anyspec] * (2 * n + n_extra), out_specs=[anyspec] * n, out_shape=layout[2],
        input_output_aliases={n + a: a for a in range(n)},
        scratch_shapes=[pltpu.SemaphoreType.DMA((n * (N_DEV - 1),)), pltpu.SemaphoreType.DMA((n * (N_DEV - 1),))],
    )(*arrays, *lands, *([] if after is None else [after]))
    return list(outs)


HBM_SPEC = pl.BlockSpec(memory_space=pltpu.HBM)
SEM_SPEC = pl.BlockSpec(memory_space=pltpu.SEMAPHORE)
ANY_SPEC = pl.BlockSpec(memory_space=pl.ANY)
DATAFLOW = pltpu.SideEffectType.DATAFLOW_SIDE_EFFECTING


def _exchange_start(arrays, kinds, gather, name, after, carry=()):
    n, nc = len(arrays), len(carry)
    layout = _exchange_layout(arrays, kinds, gather)
    lands = _landing_with_own_slab(arrays, kinds, gather, layout, "place_own")

    def body(*refs):
        ins, outs = refs[:n], refs[n:2 * n]
        send_sems, recv_sems = refs[2 * n + nc + 1], refs[2 * n + nc + 2]
        token = refs[2 * n + nc + 3 + 2 * n + nc]
        remote = _exchange_copies(ins, outs, send_sems, recv_sems, lambda a, k: a, layout, kinds, gather)
        for a in range(n):
            for k in range(1, N_DEV):
                remote(a, k, "mine").start()
        token[...] = jnp.zeros_like(token)

    passed = list(arrays) + lands + list(carry)
    res = pl.pallas_call(
        body, name=name,
        out_shape=(pltpu.SemaphoreType.DMA((n,)), pltpu.SemaphoreType.DMA((n,)),
                   *[pltpu.HBM(t.shape, t.dtype) for t in passed], jax.ShapeDtypeStruct((SUBLANE, LANE), F32)),
        in_specs=[HBM_SPEC] * (2 * n + nc) + [ANY_SPEC],
        out_specs=(SEM_SPEC, SEM_SPEC, *([HBM_SPEC] * (2 * n + nc)), pl.BlockSpec(memory_space=pltpu.VMEM)),
        input_output_aliases={i: 2 + i for i in range(2 * n + nc)},
        compiler_params=pltpu.CompilerParams(has_side_effects=DATAFLOW),
    )(*[pltpu.with_memory_space_constraint(t, pltpu.HBM) for t in passed], after)
    handle = (res[0], res[1], list(res[2:2 + n]), list(res[2 + n:2 + 2 * n]), tuple(kinds), gather)
    return handle, res[-1], list(res[2 + 2 * n:2 + 2 * n + nc])


def _exchange_wait(handle, name, after):
    send_sems, recv_sems, ins_thru, lands_thru, kinds, gather = handle
    n = len(ins_thru)
    layout = _exchange_layout(ins_thru, kinds, gather)

    def body(*refs):
        ins, outs = refs[:n], refs[n:2 * n]
        s_sems, r_sems = refs[2 * n], refs[2 * n + 1]
        remote = _exchange_copies(ins, outs, s_sems, r_sems, lambda a, k: a, layout, kinds, gather)
        for a in range(n):
            for k in range(1, N_DEV):
                arrival = remote(a, k, "theirs")
                arrival.wait_send()
                arrival.wait_recv()

    res = pl.pallas_call(
        body, name=name,
        out_shape=[pltpu.HBM(t.shape, t.dtype) for t in ins_thru + lands_thru],
        in_specs=[HBM_SPEC] * (2 * n) + [SEM_SPEC, SEM_SPEC, ANY_SPEC],
        out_specs=[HBM_SPEC] * (2 * n),
        input_output_aliases={i: i for i in range(2 * n)},
        compiler_params=pltpu.CompilerParams(has_side_effects=DATAFLOW),
    )(*ins_thru, *lands_thru, send_sems, recv_sems, after)
    return list(res[n:2 * n])


STAGE1_KS = (1, 2, 4, 6)
FORWARD_KS = (2, 4, 6)


def _gather2_copies(lands, shard_shapes, kinds):
    x, y, c, me = _my_place()

    def slab(a, p):
        return _slab(lands[a], shard_shapes[a], kinds[a], p)

    def stage1(a, k, sems, arriving):
        peer, pidx = _peer(x, y, c, k)
        s = slab(a, pidx if arriving else me)
        return pltpu.make_async_remote_copy(src_ref=s, dst_ref=s, send_sem=sems[0].at[a], recv_sem=sems[1].at[a],
                                            device_id=peer, device_id_type=MESH)

    def stage2(a, k, sems, arriving):
        sib, _ = _peer(x, y, c, 1)
        _, mine = _peer(x, y, c, k)
        _, theirs = _peer(x, y, 1 - c, k)
        s = slab(a, theirs if arriving else mine)
        return pltpu.make_async_remote_copy(src_ref=s, dst_ref=s, send_sem=sems[0].at[a], recv_sem=sems[1].at[a],
                                            device_id=sib, device_id_type=MESH)

    return stage1, stage2


def _gather2_call(lands, sems_in, name, after, make_body, returns_sems):
    n = len(lands)
    n_in = len(sems_in)

    def body(*refs):
        land_refs = refs[:n]
        in_sems = refs[n:n + n_in]
        rest = refs[n + n_in + 1:]
        out_sems = rest[:2] if returns_sems else ()
        make_body(land_refs, in_sems, out_sems)
        if returns_sems:
            token = rest[2 + n]
            token[...] = jnp.zeros_like(token)

    sem_shapes = (pltpu.SemaphoreType.DMA((n,)), pltpu.SemaphoreType.DMA((n,))) if returns_sems else ()
    tok_shape = (jax.ShapeDtypeStruct((SUBLANE, LANE), F32),) if returns_sems else ()
    n_sem_out = len(sem_shapes)
    res = pl.pallas_call(
        body, name=name,
        out_shape=(*sem_shapes, *[pltpu.HBM(t.shape, t.dtype) for t in lands], *tok_shape),
        in_specs=[HBM_SPEC] * n + [SEM_SPEC] * n_in + [ANY_SPEC],
        out_specs=(*([SEM_SPEC] * n_sem_out), *([HBM_SPEC] * n),
                   *([pl.BlockSpec(memory_space=pltpu.VMEM)] if returns_sems else [])),
        input_output_aliases={i: n_sem_out + i for i in range(n)},
        compiler_params=pltpu.CompilerParams(has_side_effects=DATAFLOW),
    )(*[pltpu.with_memory_space_constraint(t, pltpu.HBM) for t in lands], *sems_in, after)
    sems_out = tuple(res[:n_sem_out])
    lands_thru = list(res[n_sem_out:n_sem_out + n])
    return sems_out, lands_thru, (res[-1] if returns_sems else None)


def _gather2_start(shards, kinds, name, after):
    n = len(shards)
    views = [jax.ShapeDtypeStruct(t.shape[1:], t.dtype) for t, _ in shards]
    shard_shapes, dst_kinds, fulls = _exchange_layout(views, kinds, True)
    layout = (shard_shapes, dst_kinds, [jax.ShapeDtypeStruct(f.shape, BF16) for f in fulls])
    lands = _landing_with_own_slab(shards, kinds, True, layout, "place_own", order_after=after)

    def make_body(land_refs, in_sems, out_sems):
        stage1, _ = _gather2_copies(land_refs, layout[0], kinds)
        for a in range(n):
            for k in STAGE1_KS:
                stage1(a, k, out_sems, False).start()

    sems, lands, token = _gather2_call(lands, (), name, after, make_body, True)
    return (sems, lands, layout[0], tuple(kinds)), token


def _gather2_forward(handle, name, after, carry=()):
    sems1, lands, shard_shapes, kinds = handle
    n = len(lands)

    def make_body(land_refs, in_sems, out_sems):
        stage1, stage2 = _gather2_copies(land_refs, shard_shapes, kinds)
        for a in range(n):
            for k in STAGE1_KS:
                arrival = stage1(a, k, in_sems, True)
                arrival.wait_send()
                arrival.wait_recv()
        for a in range(n):
            for k in FORWARD_KS:
                stage2(a, k, out_sems, False).start()

    sems2, passed, token = _gather2_call(list(lands) + list(carry), sems1, name, after, make_body, True)
    return (sems2, passed[:n], shard_shapes, kinds), token, passed[n:]


def _gather2_wait(handle, name, after):
    sems2, lands, shard_shapes, kinds = handle
    n = len(lands)

    def make_body(land_refs, in_sems, out_sems):
        _, stage2 = _gather2_copies(land_refs, shard_shapes, kinds)
        for a in range(n):
            for k in FORWARD_KS:
                arrival = stage2(a, k, in_sems, True)
                arrival.wait_send()
                arrival.wait_recv()

    _, lands, _ = _gather2_call(lands, sems2, name, after, make_body, False)
    return lands


N_CHIP = N_DEV // 2


def _scatter2_pair(g, name):
    r, c8 = g.shape
    cd = c8 // N_DEV

    def body(g_ref, o_ref, send_sems, recv_sems):
        x, y, c, _ = _my_place()
        sib, _ = _peer(x, y, c, 1)

        def copy(ch):
            theirs = 2 * ch + (1 - c)
            return pltpu.make_async_remote_copy(
                src_ref=_slab(g_ref, (r, cd), "col", theirs), dst_ref=_slab(o_ref, (r, cd), "row", ch),
                send_sem=send_sems.at[ch], recv_sem=recv_sems.at[ch], device_id=sib, device_id_type=MESH)

        for ch in range(N_CHIP):
            copy(ch).start()
        for ch in range(N_CHIP):
            copy(ch).wait_send()
            copy(ch).wait_recv()

    return _pcall(
        body, comm=True, name=name,
        in_specs=[ANY_SPEC], out_specs=ANY_SPEC, out_shape=jax.ShapeDtypeStruct((N_CHIP * r, cd), g.dtype),
        scratch_shapes=[pltpu.SemaphoreType.DMA((N_CHIP,)), pltpu.SemaphoreType.DMA((N_CHIP,))],
    )(g)


def _scatter2_add(g, from_sibling, name):
    r, c8 = g.shape
    cd = c8 // N_DEV
    tr = _pick(r, 512)
    nr = r // tr
    _, _, core, _ = _my_place()

    def body(c_ref, g_ref, s_ref, o_ref):
        o_ref[...] = (g_ref[...].astype(F32) + s_ref[...].astype(F32)).astype(o_ref.dtype)

    return _pcall(
        body, name=name,
        grid_spec=pltpu.PrefetchScalarGridSpec(
            num_scalar_prefetch=1, grid=(N_CHIP, nr),
            in_specs=[pl.BlockSpec((tr, cd), lambda ch, i, c_ref: (i, 2 * ch + c_ref[0])),
                      pl.BlockSpec((tr, cd), lambda ch, i, c_ref: (ch * nr + i, 0))],
            out_specs=pl.BlockSpec((tr, cd), lambda ch, i, c_ref: (ch * nr + i, 0))),
        out_shape=jax.ShapeDtypeStruct((N_CHIP * r, cd), g.dtype),
        compiler_params=_params(("arbitrary", "arbitrary")),
    )(jnp.reshape(core, (1,)).astype(jnp.int32), g, from_sibling)


def _scatter2_copies(refs, r):
    part_ref, recv_ref = refs
    x, y, c, _ = _my_place()
    my_chip = 2 * x + y

    def rows(ref, ch):
        return ref.at[pl.ds(pl.multiple_of(ch * r, SUBLANE), r), :]

    def copy(k, sems, arriving):
        peer, _ = _peer(x, y, c, k)
        peer_chip = 2 * peer[0] + peer[1]
        return pltpu.make_async_remote_copy(
            src_ref=rows(part_ref, peer_chip), dst_ref=rows(recv_ref, peer_chip if arriving else my_chip),
            send_sem=sems[0].at[0], recv_sem=sems[1].at[0], device_id=peer, device_id_type=MESH)

    return copy


def _scatter2_start(g, name, after, carry=()):
    r = g.shape[0]
    x, y, _, _ = _my_place()
    from_sibling = _scatter2_pair(g, name + "_pair")
    partial = _scatter2_add(g, from_sibling, name + "_add")
    cd = partial.shape[1]
    recv = _place_own(partial, "row", (r, cd), "row", jax.ShapeDtypeStruct(partial.shape, partial.dtype),
                      "place_own", index=2 * x + y)

    def make_body(land_refs, in_sems, out_sems):
        copy = _scatter2_copies(land_refs[:2], r)
        for k in FORWARD_KS:
            copy(k, out_sems, False).start()

    sems, lands, token = _gather2_call([partial, recv] + list(carry), (), name, after, make_body, True)
    return (sems, lands[:2], r), token, lands[2:]


def _scatter2_wait(handle, name, after):
    sems, lands, r = handle

    def make_body(land_refs, in_sems, out_sems):
        copy = _scatter2_copies(land_refs, r)
        for k in FORWARD_KS:
            arrival = copy(k, in_sems, True)
            arrival.wait_send()
            arrival.wait_recv()

    _, lands, _ = _gather2_call(lands, sems, name, after, make_body, False)
    return lands[1]


def _pad_mla_w_in(w):
    d = w.shape[0]
    z = lambda n: jnp.zeros((d, n), w.dtype)
    o1, o2, o3 = Q_RANK, Q_RANK + KV_RANK, Q_RANK + KV_RANK + ROPE
    return jnp.concatenate([w[:, :o1], z(Q_RANK_PAD - Q_RANK), w[:, o1:o2], w[:, o2:o3], z(LANE - ROPE), w[:, o3:]], axis=1)


def _unpad_mla_w_in(g):
    return jnp.concatenate([g[:, :Q_RANK], g[:, PROJ_CKV:PROJ_KR], g[:, PROJ_KR:PROJ_KR + ROPE], g[:, PROJ_Z:]], axis=1)


def _pad_w_uq(w):
    w3 = w.reshape(Q_RANK, HEADS, NOPE + ROPE)
    w3 = jnp.pad(w3, ((0, Q_RANK_PAD - Q_RANK), (0, 0), (0, HEAD_PAD - NOPE - ROPE)))
    return w3.reshape(Q_RANK_PAD, HEADS * HEAD_PAD)


def _unpad_w_uq(g):
    return g[:Q_RANK].reshape(Q_RANK, HEADS, HEAD_PAD)[:, :, :NOPE + ROPE].reshape(Q_RANK, HEADS * (NOPE + ROPE))


def _pack(pieces):
    flat = [p.reshape(-1).astype(F32) for p in pieces]
    tot = sum(f.shape[0] for f in flat)
    unit = SUBLANE * LANE
    padn = (-tot) % unit
    if padn:
        flat.append(jnp.zeros((padn,), F32))
    return jnp.concatenate(flat).reshape(-1, LANE)


def _unpack(packed, shapes, lead=()):
    flat = packed.reshape(tuple(lead) + (-1,))
    out, off = [], 0
    for shp in shapes:
        nel = int(np.prod(shp))
        out.append(flat[..., off:off + nel].reshape(tuple(lead) + tuple(shp)))
        off += nel
    return out


SMALL_GROUPS = {
    "a": [('ada_b', lambda t: t[1:]), ('pre_g', lambda t: t[1:]), ('post_g', lambda t: t[1:]),
          ('sgu_norm_g', lambda t: t[1:]), ('sgu_w_s', lambda t: t[1:]), ('sgu_b_s', lambda t: t[1:]),
          ('mla_q_norm_g', lambda t: t), ('mla_kv_norm_g', lambda t: t)],
    "b": [('ada_b', lambda t: t[0:1, 2 * t.shape[1] // 3:]), ('post_g', lambda t: t[0:1]),
          ('sgu_norm_g', lambda t: t[0:1]), ('sgu_w_s', lambda t: t[0:1]), ('sgu_b_s', lambda t: t[0:1])],
    "c": [('ada_b', lambda t: t[0:1, :2 * t.shape[1] // 3]), ('pre_g', lambda t: t[0:1])],
}


WEIGHTS = ['ada_w', 'ada_b', 'pre_g', 'post_g', 'sgu_w_in', 'sgu_norm_g', 'sgu_w_s', 'sgu_b_s', 'sgu_w_out',
           'mla_w_in', 'mla_q_norm_g', 'mla_kv_norm_g', 'mla_w_uq', 'mla_w_ukv', 'mla_w_out']
INPUTS = ['x', 'c'] + WEIGHTS + ['loss_target'] + ['m_' + n for n in WEIGHTS] + ['v_' + n for n in WEIGHTS]


def kernel(x, c, ada_w, ada_b, pre_g, post_g, sgu_w_in, sgu_norm_g, sgu_w_s, sgu_b_s, sgu_w_out, mla_w_in, mla_q_norm_g, mla_kv_norm_g, mla_w_uq, mla_w_ukv, mla_w_out, loss_target, m_ada_w, m_ada_b, m_pre_g, m_post_g, m_sgu_w_in, m_sgu_norm_g, m_sgu_w_s, m_sgu_b_s, m_sgu_w_out, m_mla_w_in, m_mla_q_norm_g, m_mla_kv_norm_g, m_mla_w_uq, m_mla_w_ukv, m_mla_w_out, v_ada_w, v_ada_b, v_pre_g, v_post_g, v_sgu_w_in, v_sgu_norm_g, v_sgu_w_s, v_sgu_b_s, v_sgu_w_out, v_mla_w_in, v_mla_q_norm_g, v_mla_kv_norm_g, v_mla_w_uq, v_mla_w_ukv, v_mla_w_out):
    given = locals()
    A = {name: given[name] for name in INPUTS}
    x0 = A['x'][0]
    tgt = A['loss_target'][0]
    s, d = x0.shape
    e = 2 * d
    ncol = 3 * d // N_DEV
    _, _, _, me = _my_place()
    ktabs = _rope_tables(s)

    gains = jnp.zeros((SUBLANE, LANE), F32)
    gains = gains.at[0:2, :Q_RANK // N_DEV].set(A['mla_q_norm_g'])
    gains = gains.at[2:4, :KV_RANK // N_DEV].set(A['mla_kv_norm_g'])
    c8 = jnp.broadcast_to(A['c'], (SUBLANE, d))
    cg, gg = _exchange([c8, gains], ["row", "row"], True, "ag_cond")
    cond_raw = cg.reshape(N_DEV, SUBLANE, d)[:, 0, :]
    gg = gg.reshape(N_DEV, SUBLANE, LANE)
    gq_full = jnp.transpose(gg[:, 0:2, :Q_RANK // N_DEV], (1, 0, 2)).reshape(N_MIX, Q_RANK)
    gkv_full = jnp.transpose(gg[:, 2:4, :KV_RANK // N_DEV], (1, 0, 2)).reshape(N_MIX, KV_RANK)
    gq_pad = jnp.pad(gq_full, ((0, 0), (0, Q_RANK_PAD - Q_RANK)))

    bias_my = lax.dynamic_slice_in_dim(A['ada_b'], me * ncol, ncol, axis=1)
    mod_part = _ada_mod(cond_raw, A['ada_w'], bias_my, "ada_mod")
    send = jnp.pad(jnp.transpose(mod_part, (1, 0, 2)), ((0, 0), (0, SUBLANE - DEPTH), (0, 0)))
    (rb,) = _exchange([send.reshape(N_DEV * SUBLANE, ncol)], ["row"], False, "a2a_mod")

    token = rb
    gathers = {}
    for i in range(DEPTH):
        j = i // N_MIX
        if i % N_MIX == 0:
            parts = [("in", [(A['sgu_w_in'], j)], ["col"]), ("out", [(A['sgu_w_out'], j)], ["row"])]
        else:
            parts = [("all", [(A['mla_w_in'], j), (A['mla_w_uq'], j), (A['mla_w_ukv'], j), (A['mla_w_out'], j)],
                      ["col", "col", "col", "row"])]
        for part, shards, kinds in parts:
            gathers[(i, part)], token = _gather2_start(shards, kinds, f"ag_start_{i}_{part}", token)

    def forward_gathers(i, carried):
        for key in [k for k in gathers if k[0] == i]:
            gathers[key], _, (carried,) = _gather2_forward(gathers[key], f"ag_forward_{key[0]}_{key[1]}", token,
                                                           carry=[carried])
        return carried

    mod = jnp.transpose(rb.reshape(N_DEV, SUBLANE, ncol)[:, :DEPTH, :], (1, 0, 2)).reshape(DEPTH, 3 * d) + token[0, 0]
    shift = [mod[i:i + 1, :d] for i in range(DEPTH)]
    scale = [mod[i:i + 1, d:2 * d] for i in range(DEPTH)]
    gate = [mod[i:i + 1, 2 * d:] for i in range(DEPTH)]

    saved = []
    x = x0
    h = _pre_fwd(x, A['pre_g'][0:1], scale[0], shift[0], "pre_fwd")
    for i in range(DEPTH):
        j = i // N_MIX
        if i == 0:
            gathers[(0, "in")], _, (h,) = _gather2_forward(gathers[(0, "in")], "ag_forward_0_in", token, carry=[h])
        if i % N_MIX == 0:
            (w_in,) = _gather2_wait(gathers[(i, "in")], f"ag_wait_{i}_in", h)
            uvz = _mm(h, w_in, "nn", F32, "sgu_in")
            if i == 0:
                gathers[(0, "out")], _, (uvz,) = _gather2_forward(gathers[(0, "out")], "ag_forward_0_out", token,
                                                                  carry=[uvz])
            bias_full = jnp.repeat(A['sgu_b_s'][j].T, e // SGU_GROUPS, axis=1)
            ng = A['sgu_norm_g'][j:j + 1]
            y = _sgu_mid_fwd(uvz, ng, A['sgu_w_s'][j], bias_full, "sgu_mid_fwd")
            y = forward_gathers(i + 1, y)
            (w_out,) = _gather2_wait(gathers[(i, "out")], f"ag_wait_{i}_out", y)
            out = _mm(y, w_out, "nn", F32, "sgu_out")
            saved.append(dict(x=x, h=h, uvz=uvz, y=y, out=out, w_in=w_in, w_out=w_out, bias=bias_full, ng=ng))
        else:
            w_in, w_uq, w_ukv, w_out = _gather2_wait(gathers[(i, "all")], f"ag_wait_{i}_all", h)
            w_in = _pad_mla_w_in(w_in)
            w_uq = _pad_w_uq(w_uq)
            gq, gkv = gq_pad[j:j + 1], gkv_full[j:j + 1]
            proj = _mm(h, w_in, "nn", F32, "mla_in")
            qn, kvn, kr = _mla_norm_fwd(proj, gq, gkv, ktabs, "mla_norm_fwd")
            q = _mm(qn, w_uq, "nn", F32, "mla_uq")
            kv = _mm(kvn, w_ukv, "nn", BF16, "mla_ukv")
            o, y, lse, q_cat = _attn_fwd(q, kv, kr, proj, ktabs, "attn_fwd")
            y = forward_gathers(i + 1, y)
            out = _mm(y, w_out, "nn", F32, "mla_out")
            saved.append(dict(x=x, h=h, proj=proj, qn=qn, kvn=kvn, kr=kr, q_cat=q_cat, kv=kv, o=o, y=y, lse=lse,
                              out=out, w_in=w_in, w_uq=w_uq, w_ukv=w_ukv, w_out=w_out, gq=gq, gkv=gkv))
        if i + 1 < DEPTH:
            x, h = _post_pre_fwd(x, out, gate[i], A['post_g'][i:i + 1], A['pre_g'][i + 1:i + 2], scale[i + 1],
                                 shift[i + 1], "post_pre_fwd")

    last = DEPTH - 1
    d_shift, d_scale, d_gate = [None] * DEPTH, [None] * DEPTH, [None] * DEPTH
    d_pre, d_post = [None] * DEPTH, [None] * DEPTH
    dx, dout, d_gate[last], d_post[last], loss_row = _last_post_loss_bwd(
        x, saved[last]['out'], gate[last], A['post_g'][last:], tgt, "last_post_loss_bwd")
    loss = lax.psum(loss_row[0, 0], ("x", "y", "c"))

    scatters, scatters_out = [None] * DEPTH, [None] * DEPTH
    small, small_grads, small_handles = {}, {}, {}
    for i in reversed(range(DEPTH)):
        j = i // N_MIX
        sv = saved[i]
        if i % N_MIX == 0:
            dy = _mm(dout, sv['w_out'], "nt", F32, "sgu_out_dx")
            g_w_out = _mm(sv['y'], dout, "tn", BF16, "sgu_out_dw")
            scatters_out[i], token, (dy,) = _exchange_start([g_w_out], ["row"], False, f"rs_out_start_{i}", token,
                                                            carry=[dy])
            duvz, dws, dbs, dng = _sgu_mid_bwd(sv['uvz'], dy, sv['ng'], A['sgu_w_s'][j], sv['bias'], "sgu_mid_bwd")
            small[('sgu', j)] = (dws, dbs.reshape(SGU_GROUPS, SGU_BLOCK), dng)
            g_w_in = _mm(sv['h'], duvz, "tn", BF16, "sgu_in_dw")
            if i == 0:
                scatters[i], token, (duvz,) = _scatter2_start(g_w_in, "rs2_start_0", token, carry=[duvz])
                small_grads["b"] = [d_gate[0], d_post[0], dng, dws[None], small[('sgu', 0)][1][None]]
                small_handles["b"], token, (duvz,) = _exchange_start(
                    [_pack(small_grads["b"])], ["row"], True, "ag_small_b_start", token, carry=[duvz])
            else:
                scatters[i], token, (duvz,) = _exchange_start([g_w_in], ["col"], False, f"rs_start_{i}", token,
                                                              carry=[duvz])
            dh = _mm(duvz, sv['w_in'], "nt", F32, "sgu_in_dx")
        else:
            dy = _mm(dout, sv['w_out'], "nt", F32, "mla_out_dx")
            g_w_out = _mm(sv['y'], dout, "tn", BF16, "mla_out_dw")
            scatters_out[i], token, (dy,) = _exchange_start([g_w_out], ["row"], False, f"rs_out_start_{i}", token,
                                                            carry=[dy])
            do, dproj = _mla_gate_bwd(dy, sv['o'], sv['proj'], "mla_gate_bwd")
            dq_b, dkv, dkr_heads = _attn_bwd(sv['q_cat'], sv['kv'], sv['kr'], do, sv['o'], sv['lse'], ktabs,
                                             "attn_bwd")
            dqn = _mm(dq_b, sv['w_uq'], "nt", F32, "mla_uq_dx")
            g_w_uq = _unpad_w_uq(_mm(sv['qn'], dq_b, "tn", BF16, "mla_uq_dw"))
            dkvn = _mm(dkv, sv['w_ukv'], "nt", F32, "mla_ukv_dx")
            g_w_ukv = _mm(sv['kvn'], dkv, "tn", BF16, "mla_ukv_dw")
            dproj, dgq, dgkv = _mla_norm_bwd(dqn, dkvn, dkr_heads, sv['proj'], sv['gq'], sv['gkv'], ktabs, dproj,
                                             "mla_norm_bwd")
            g_w_in = _unpad_mla_w_in(_mm(sv['h'], dproj, "tn", BF16, "mla_in_dw"))
            scatters[i], token, (dproj,) = _exchange_start(
                [g_w_in, g_w_uq, g_w_ukv], ["col", "col", "col"], False, f"rs_start_{i}", token, carry=[dproj])
            dh = _mm(dproj, sv['w_in'], "nt", F32, "mla_in_dx")
            small[('mla', j)] = (dgq[:, :Q_RANK], dgkv)
        if i > 0:
            (dx, d_shift[i], d_scale[i], d_pre[i], dout, d_gate[i - 1], d_post[i - 1]) = _pre_post_bwd(
                dh, sv['x'], dx, A['pre_g'][i:i + 1], scale[i], saved[i - 1]['out'], gate[i - 1],
                A['post_g'][i - 1:i], "pre_post_bwd")
        else:
            dx, d_shift[i], d_scale[i], d_pre[i] = _pre_bwd(dh, sv['x'], dx, A['pre_g'][i:i + 1], scale[i], "pre_bwd")
        if i == 1:
            small_grads["a"] = [
                jnp.concatenate([jnp.concatenate([d_shift[l], d_scale[l], d_gate[l]], axis=1)
                                 for l in range(1, DEPTH)], axis=0),
                jnp.concatenate(d_pre[1:], axis=0), jnp.concatenate(d_post[1:], axis=0),
                small[('sgu', 1)][2], small[('sgu', 1)][0][None], small[('sgu', 1)][1][None],
                jnp.concatenate([small[('mla', jj)][0] for jj in range(N_MIX)], axis=0),
                jnp.concatenate([small[('mla', jj)][1] for jj in range(N_MIX)], axis=0)]
            small_handles["a"], token, (dx,) = _exchange_start(
                [_pack(small_grads["a"])], ["row"], True, "ag_small_a_start", token, carry=[dx])

    res = {}

    def big(name, recv0, recv1):
        res[name] = _adam_reduce(recv0, recv1, A[name], A['m_' + name], A['v_' + name], "adam_" + name)

    def small_full(name, prefix):
        t = A[prefix + name]
        if name in ('mla_q_norm_g', 'mla_kv_norm_g'):
            t = lax.dynamic_update_slice_in_dim(jnp.zeros((t.shape[0], t.shape[1] * N_DEV), F32), t,
                                                me * t.shape[1], axis=1)
        return t

    def finish(i, after):
        first = ([_scatter2_wait(scatters[i], "rs2_wait_0", after)] if i == 0
                 else _exchange_wait(scatters[i], f"rs_wait_{i}", after))
        return first + _exchange_wait(scatters_out[i], f"rs_out_wait_{i}", after)

    recv_mla = {j: finish(N_MIX * j + 1, dx) for j in reversed(range(N_MIX))}
    for idx_w, name in enumerate(['mla_w_in', 'mla_w_uq', 'mla_w_ukv', 'mla_w_out']):
        big(name, recv_mla[0][idx_w], recv_mla[1][idx_w])

    recv_sgu = {j: finish(N_MIX * j, res['mla_w_out'][0]) for j in reversed(range(N_MIX))}
    for idx_w, name in enumerate(['sgu_w_in', 'sgu_w_out']):
        big(name, recv_sgu[0][idx_w], recv_sgu[1][idx_w])

    small_grads["c"] = [jnp.concatenate([d_shift[0], d_scale[0]], axis=1), d_pre[0]]
    gathered = {tag: _exchange_wait(small_handles[tag], f"ag_small_{tag}_wait", res['sgu_w_in'][0])[0]
                for tag in ("a", "b")}
    (gathered["c"],) = _exchange([_pack(small_grads["c"])], ["row"], True, "ag_small_c", after=res['sgu_w_in'][0])

    upd, parts = {}, {}
    for tag, group in SMALL_GROUPS.items():
        shapes = [g.shape for g in small_grads[tag]]
        packs = [_pack([pick(small_full(name, prefix)) for name, pick in group]) for prefix in ('', 'm_', 'v_')]
        outs4 = _adam_small(gathered[tag].reshape(N_DEV, -1, LANE), *packs, "adam_small_" + tag)
        upd[tag] = [_unpack(t, shapes) for t in outs4]
        parts[tag] = _unpack(gathered[tag].reshape(N_DEV, -1), shapes, lead=(N_DEV,))
    for k_out in range(4):
        ab_rest, pre_rest, post_rest, ng_1, ws_1, bs_1, gq_all, gkv_all = upd["a"][k_out]
        ab_gate0, post_0, ng_0, ws_0, bs_0 = upd["b"][k_out]
        ab_0, pre_0 = upd["c"][k_out]
        for name, val in (
                ('ada_b', jnp.concatenate([jnp.concatenate([ab_0, ab_gate0], axis=1), ab_rest], axis=0)),
                ('pre_g', jnp.concatenate([pre_0, pre_rest], axis=0)),
                ('post_g', jnp.concatenate([post_0, post_rest], axis=0)),
                ('sgu_norm_g', jnp.concatenate([ng_0, ng_1], axis=0)),
                ('sgu_w_s', jnp.concatenate([ws_0, ws_1], axis=0)),
                ('sgu_b_s', jnp.concatenate([bs_0, bs_1], axis=0)),
                ('mla_q_norm_g', gq_all), ('mla_kv_norm_g', gkv_all)):
            if name in ('mla_q_norm_g', 'mla_kv_norm_g'):
                wdt = A[name].shape[1]
                val = lax.dynamic_slice_in_dim(val, me * wdt, wdt, axis=1)
            res.setdefault(name, [None] * 4)[k_out] = val

    dmod_all = jnp.concatenate([jnp.concatenate([parts["c"][0], parts["b"][0]], axis=2), parts["a"][0]], axis=1)
    dmod_cols = jnp.transpose(lax.dynamic_slice_in_dim(dmod_all, me * ncol, ncol, axis=2), (1, 0, 2))
    res['ada_w'] = _ada_bwd_adam(jnp.transpose(cond_raw), dmod_cols, A['ada_w'], A['m_ada_w'], A['v_ada_w'], "ada_bwd")

    outs = [loss, dx[None]]
    for k_out in range(4):
        outs += [res[n][k_out] for n in WEIGHTS]
    return tuple(outs)
```

```python
import functools
import math

import numpy as np
import jax
import jax.numpy as jnp
from jax import lax
from jax.experimental import pallas as pl
from jax.experimental.pallas import tpu as pltpu

F32 = jnp.float32
BF16 = jnp.bfloat16
MESH = pl.DeviceIdType.MESH

N_DEV = 8
DEPTH = 4
N_MIX = 2
NORM_EPS = 1e-6
CHUNK = 64
SGU_BLOCK = 128
SGU_GROUPS = 16
HEADS = 16
Q_RANK = 448
Q_RANK_PAD = 512
KV_RANK = 512
NOPE = 128
ROPE = 64
HALF = ROPE // 2
V_DIM = 128
HEAD_PAD = 256
ROPE_THETA = 10000.0
MLA_WIDTH = HEADS * V_DIM
LANE = 128
SUBLANE = 8
PROJ_CQ = 0
PROJ_CKV = Q_RANK_PAD
PROJ_KR = Q_RANK_PAD + KV_RANK
PROJ_Z = PROJ_KR + LANE
PROJ_W = PROJ_Z + MLA_WIDTH

ADAM_LR = 0.001
ADAM_B1 = 0.9
ADAM_B2 = 0.999
ADAM_EPS = 1e-08
ADAM_WD = 0.01
ADAM_STEP = 10

VMEM_LIMIT = 56 * 1024 * 1024
ATT_BLK = 512
ATT_SUB = 256
ATT_HEADS_PER_STEP = 2
ROW_BLK = 512
MM_TM, MM_TN, MM_TK = 1024, 1024, 2048
MM_TILE_BYTES = 40 * 1024 * 1024
SOFTMAX_SCALE = (NOPE + ROPE) ** -0.5
LOG2_E = 1.0 / math.log(2.0)
INV_SQRT2 = 1.0 / math.sqrt(2.0)
INV_SQRT_2PI = 1.0 / math.sqrt(2.0 * math.pi)


def _pcall(body, comm=False, **kw):
    return pl.pallas_call(body, **kw)


def _params(sem=None):
    return pltpu.CompilerParams(dimension_semantics=sem, vmem_limit_bytes=VMEM_LIMIT)


def _pick(dim, pref):
    if dim <= pref:
        return dim
    t = (pref // LANE) * LANE
    while t >= LANE:
        if dim % t == 0:
            return t
        t -= LANE
    return dim


def _gelu(x):
    return 0.5 * x * (1.0 + lax.erf(x * INV_SQRT2))


def _gelu_grad(x):
    return 0.5 * (1.0 + lax.erf(x * INV_SQRT2)) + x * jnp.exp(-0.5 * x * x) * INV_SQRT_2PI


def _sigmoid(x):
    return 1.0 / (1.0 + jnp.exp(-x))


def _dot_nt(a, b):
    return lax.dot_general(a, b, (((1,), (1,)), ((), ())), preferred_element_type=F32)


def _dot_tn(a, b):
    return lax.dot_general(a, b, (((0,), (0,)), ((), ())), preferred_element_type=F32)


def _mm(a, b, dims, out_dtype, name):
    if dims == "nn":
        (m, k), (k2, n) = a.shape, b.shape
    elif dims == "nt":
        (m, k), (n, k2) = a.shape, b.shape
    else:
        (k, m), (k2, n) = a.shape, b.shape
    assert k == k2, (a.shape, b.shape, dims)
    tm, tn = _pick(m, MM_TM), _pick(n, MM_TN)
    out_bytes = 2 * tm * tn * jnp.dtype(out_dtype).itemsize
    whole_k = 2 * (tm + tn) * k * a.dtype.itemsize + out_bytes <= MM_TILE_BYTES
    tk = k if whole_k else _pick(k, MM_TK)
    nk = k // tk

    def body(a_ref, b_ref, o_ref, *scratch):
        if dims == "nn":
            p = jnp.dot(a_ref[...], b_ref[...], preferred_element_type=F32)
        elif dims == "nt":
            p = _dot_nt(a_ref[...], b_ref[...])
        else:
            p = _dot_tn(a_ref[...], b_ref[...])
        if nk == 1:
            o_ref[...] = p.astype(o_ref.dtype)
            return
        acc_ref, = scratch
        kk = pl.program_id(2)

        @pl.when(kk == 0)
        def _():
            acc_ref[...] = p

        @pl.when(kk > 0)
        def _():
            acc_ref[...] += p

        @pl.when(kk == nk - 1)
        def _():
            o_ref[...] = acc_ref[...].astype(o_ref.dtype)

    if dims == "tn":
        a_spec = pl.BlockSpec((tk, tm), lambda i, j, kk: (kk, i))
    else:
        a_spec = pl.BlockSpec((tm, tk), lambda i, j, kk: (i, kk))
    if dims == "nt":
        b_spec = pl.BlockSpec((tn, tk), lambda i, j, kk: (j, kk))
    else:
        b_spec = pl.BlockSpec((tk, tn), lambda i, j, kk: (kk, j))
    return _pcall(
        body, name=name,
        grid=(m // tm, n // tn, nk),
        in_specs=[a_spec, b_spec],
        out_specs=pl.BlockSpec((tm, tn), lambda i, j, kk: (i, j)),
        out_shape=jax.ShapeDtypeStruct((m, n), out_dtype),
        scratch_shapes=[pltpu.VMEM((tm, tn), F32)] if nk > 1 else [],
        compiler_params=_params(("parallel", "parallel", "arbitrary")),
    )(a, b)


def _row_spec(ts, d):
    return pl.BlockSpec((ts, d), lambda i: (i, 0))


def _vec_spec(d):
    return pl.BlockSpec((1, d), lambda i: (0, 0))


def _pre_fwd(x, g, scale, shift, name):
    s, d = x.shape
    ts = _pick(s, ROW_BLK)

    def body(x_ref, g_ref, sc_ref, sh_ref, h_ref):
        xv = x_ref[...]
        r = lax.rsqrt(jnp.mean(xv * xv, axis=-1, keepdims=True) + NORM_EPS)
        h_ref[...] = ((xv * r * g_ref[...]) * (1.0 + sc_ref[...]) + sh_ref[...]).astype(BF16)

    return _pcall(
        body, name=name, grid=(s // ts,),
        in_specs=[_row_spec(ts, d), _vec_spec(d), _vec_spec(d), _vec_spec(d)],
        out_specs=_row_spec(ts, d),
        out_shape=jax.ShapeDtypeStruct((s, d), BF16),
        compiler_params=_params(("parallel",)),
    )(x, g, scale, shift)


def _pre_bwd(dh, x, dxo, g, scale, name):
    s, d = x.shape
    ts = _pick(s, ROW_BLK)

    def body(dh_ref, x_ref, dxo_ref, g_ref, sc_ref, dx_ref, dsh_ref, dsc_ref, dg_ref):
        i = pl.program_id(0)

        @pl.when(i == 0)
        def _():
            dsh_ref[...] = jnp.zeros_like(dsh_ref)
            dsc_ref[...] = jnp.zeros_like(dsc_ref)
            dg_ref[...] = jnp.zeros_like(dg_ref)

        xv = x_ref[...]
        dhv = dh_ref[...]
        gv = g_ref[...]
        one_sc = 1.0 + sc_ref[...]
        r = lax.rsqrt(jnp.mean(xv * xv, axis=-1, keepdims=True) + NORM_EPS)
        n = xv * r
        dsh_ref[...] += jnp.sum(dhv, axis=0, keepdims=True)
        dsc_ref[...] += jnp.sum(dhv * (n * gv), axis=0, keepdims=True)
        dng = dhv * one_sc
        dg_ref[...] += jnp.sum(dng * n, axis=0, keepdims=True)
        dn = dng * gv
        dx_ref[...] = dxo_ref[...] + r * (dn - n * jnp.mean(dn * n, axis=-1, keepdims=True))

    return _pcall(
        body, name=name, grid=(s // ts,),
        in_specs=[_row_spec(ts, d), _row_spec(ts, d), _row_spec(ts, d), _vec_spec(d), _vec_spec(d)],
        out_specs=[_row_spec(ts, d), _vec_spec(d), _vec_spec(d), _vec_spec(d)],
        out_shape=[jax.ShapeDtypeStruct((s, d), F32)] + [jax.ShapeDtypeStruct((1, d), F32)] * 3,
        compiler_params=_params(("arbitrary",)),
    )(dh, x, dxo, g, scale)


def _last_post_loss_bwd(x, out, gate, post_g, tgt, name):
    s, d = x.shape
    ts = _pick(s, ROW_BLK // 2)
    ns = s // ts

    def body(x_ref, o_ref, gate_ref, pg_ref, t_ref, dx_ref, do_ref, dgate_ref, dpg_ref, loss_ref, acc_ref):
        i = pl.program_id(0)

        @pl.when(i == 0)
        def _():
            for ref in (dgate_ref, dpg_ref, acc_ref):
                ref[...] = jnp.zeros_like(ref)

        o = o_ref[...]
        pgv = pg_ref[...]
        gatev = gate_ref[...]
        r = lax.rsqrt(jnp.mean(o * o, axis=-1, keepdims=True) + NORM_EPS)
        n = o * r
        err = (x_ref[...] + gatev * (n * pgv)) - t_ref[...]
        acc_ref[...] += jnp.sum(err * err, axis=0, keepdims=True)
        dx = err * (1.0 / d)
        dx_ref[...] = dx
        dyn = dx * gatev
        dgate_ref[...] += jnp.sum(dx * (n * pgv), axis=0, keepdims=True)
        dpg_ref[...] += jnp.sum(dyn * n, axis=0, keepdims=True)
        dn = dyn * pgv
        do_ref[...] = (r * (dn - n * jnp.mean(dn * n, axis=-1, keepdims=True))).astype(BF16)

        @pl.when(i == ns - 1)
        def _():
            tot = jnp.sum(acc_ref[...], axis=1, keepdims=True) * (0.5 / d)
            loss_ref[...] = jnp.broadcast_to(tot, loss_ref.shape)

    vec = jax.ShapeDtypeStruct((1, d), F32)
    return _pcall(
        body, name=name, grid=(ns,),
        in_specs=[_row_spec(ts, d), _row_spec(ts, d), _vec_spec(d), _vec_spec(d), _row_spec(ts, d)],
        out_specs=[_row_spec(ts, d), _row_spec(ts, d), _vec_spec(d), _vec_spec(d),
                   pl.BlockSpec((1, LANE), lambda i: (0, 0))],
        out_shape=[jax.ShapeDtypeStruct((s, d), F32), jax.ShapeDtypeStruct((s, d), BF16), vec, vec,
                   jax.ShapeDtypeStruct((1, LANE), F32)],
        scratch_shapes=[pltpu.VMEM((1, d), F32)],
        compiler_params=_params(("arbitrary",)),
    )(x, out, gate, post_g, tgt)


def _post_pre_fwd(x, out, gate, post_g, pre_g, scale, shift, name):
    s, d = x.shape
    ts = _pick(s, ROW_BLK // 2)

    def body(x_ref, o_ref, gate_ref, pg_ref, g_ref, sc_ref, sh_ref, y_ref, h_ref):
        o = o_ref[...]
        r = lax.rsqrt(jnp.mean(o * o, axis=-1, keepdims=True) + NORM_EPS)
        xn = x_ref[...] + gate_ref[...] * (o * r * pg_ref[...])
        y_ref[...] = xn
        r2 = lax.rsqrt(jnp.mean(xn * xn, axis=-1, keepdims=True) + NORM_EPS)
        h_ref[...] = ((xn * r2 * g_ref[...]) * (1.0 + sc_ref[...]) + sh_ref[...]).astype(BF16)

    return _pcall(
        body, name=name, grid=(s // ts,),
        in_specs=[_row_spec(ts, d), _row_spec(ts, d)] + [_vec_spec(d)] * 5,
        out_specs=[_row_spec(ts, d), _row_spec(ts, d)],
        out_shape=[jax.ShapeDtypeStruct((s, d), F32), jax.ShapeDtypeStruct((s, d), BF16)],
        compiler_params=_params(("parallel",)),
    )(x, out, gate, post_g, pre_g, scale, shift)


def _pre_post_bwd(dh, x, dxo, pre_g, scale, out, gate, post_g, name):
    s, d = x.shape
    ts = _pick(s, ROW_BLK // 2)

    def body(dh_ref, x_ref, dxo_ref, g_ref, sc_ref, o_ref, gate_ref, pg_ref,
             dx_ref, dsh_ref, dsc_ref, dg_ref, do_ref, dgate_ref, dpg_ref):
        i = pl.program_id(0)

        @pl.when(i == 0)
        def _():
            for ref in (dsh_ref, dsc_ref, dg_ref, dgate_ref, dpg_ref):
                ref[...] = jnp.zeros_like(ref)

        xv = x_ref[...]
        dhv = dh_ref[...]
        gv = g_ref[...]
        one_sc = 1.0 + sc_ref[...]
        r = lax.rsqrt(jnp.mean(xv * xv, axis=-1, keepdims=True) + NORM_EPS)
        n = xv * r
        dsh_ref[...] += jnp.sum(dhv, axis=0, keepdims=True)
        dsc_ref[...] += jnp.sum(dhv * (n * gv), axis=0, keepdims=True)
        dng = dhv * one_sc
        dg_ref[...] += jnp.sum(dng * n, axis=0, keepdims=True)
        dn = dng * gv
        dx = dxo_ref[...] + r * (dn - n * jnp.mean(dn * n, axis=-1, keepdims=True))
        dx_ref[...] = dx

        o = o_ref[...]
        pgv = pg_ref[...]
        ro = lax.rsqrt(jnp.mean(o * o, axis=-1, keepdims=True) + NORM_EPS)
        no = o * ro
        dyn = dx * gate_ref[...]
        dgate_ref[...] += jnp.sum(dx * (no * pgv), axis=0, keepdims=True)
        dpg_ref[...] += jnp.sum(dyn * no, axis=0, keepdims=True)
        dno = dyn * pgv
        do_ref[...] = (ro * (dno - no * jnp.mean(dno * no, axis=-1, keepdims=True))).astype(BF16)

    vec = jax.ShapeDtypeStruct((1, d), F32)
    return _pcall(
        body, name=name, grid=(s // ts,),
        in_specs=[_row_spec(ts, d)] * 3 + [_vec_spec(d)] * 2 + [_row_spec(ts, d)] + [_vec_spec(d)] * 2,
        out_specs=[_row_spec(ts, d)] + [_vec_spec(d)] * 3 + [_row_spec(ts, d)] + [_vec_spec(d)] * 2,
        out_shape=[jax.ShapeDtypeStruct((s, d), F32), vec, vec, vec, jax.ShapeDtypeStruct((s, d), BF16), vec, vec],
        compiler_params=_params(("arbitrary",)),
    )(dh, x, dxo, pre_g, scale, out, gate, post_g)


def _sgu_mask():
    t = lax.broadcasted_iota(jnp.int32, (SGU_BLOCK, SGU_BLOCK), 0) // CHUNK
    s = lax.broadcasted_iota(jnp.int32, (SGU_BLOCK, SGU_BLOCK), 1) // CHUNK
    return s <= t


def _sgu_norm(v_pre, g):
    e = v_pre.shape[-1]
    vg = _gelu(v_pre)
    mu = jnp.sum(vg, axis=-1, keepdims=True) * (1.0 / e)
    dlt = vg - mu
    var = jnp.sum(dlt * dlt, axis=-1, keepdims=True) * (1.0 / e)
    rstd = lax.rsqrt(var + NORM_EPS)
    vhat = dlt * rstd
    return vhat, rstd, (vhat * g).astype(BF16)


def _sgu_mid_fwd(uvz, norm_g, w_s, bias_full, name):
    s, e3 = uvz.shape
    e = e3 // 3
    gd = e // SGU_GROUPS
    nb = s // SGU_BLOCK

    def body(uvz_ref, g_ref, w_ref, b_ref, y_ref, wsc):
        @pl.when(pl.program_id(0) == 0)
        def _():
            msk = _sgu_mask()
            for gi in range(SGU_GROUPS):
                wsc[gi] = jnp.where(msk, w_ref[gi], 0.0).astype(BF16)

        _, _, vb = _sgu_norm(uvz_ref[:, e:2 * e], g_ref[...])
        for gi in range(SGU_GROUPS):
            lo = gi * gd
            vm = jnp.dot(wsc[gi], vb[:, lo:lo + gd], preferred_element_type=F32) + b_ref[:, lo:lo + gd]
            zg = uvz_ref[:, 2 * e + lo:2 * e + lo + gd]
            y_ref[:, lo:lo + gd] = (_gelu(uvz_ref[:, lo:lo + gd]) * vm * (zg * _sigmoid(zg))).astype(BF16)

    return _pcall(
        body, name=name, grid=(nb,),
        in_specs=[pl.BlockSpec((SGU_BLOCK, e3), lambda n: (n, 0)),
                  pl.BlockSpec((1, e), lambda n: (0, 0)),
                  pl.BlockSpec((SGU_GROUPS, SGU_BLOCK, SGU_BLOCK), lambda n: (0, 0, 0)),
                  pl.BlockSpec((SGU_BLOCK, e), lambda n: (0, 0))],
        out_specs=pl.BlockSpec((SGU_BLOCK, e), lambda n: (n, 0)),
        out_shape=jax.ShapeDtypeStruct((s, e), BF16),
        scratch_shapes=[pltpu.VMEM((SGU_GROUPS, SGU_BLOCK, SGU_BLOCK), BF16)],
        compiler_params=_params(("arbitrary",)),
    )(uvz, norm_g, w_s, bias_full)


def _sgu_mid_bwd(uvz, dy, norm_g, w_s, bias_full, name):
    s, e3 = uvz.shape
    e = e3 // 3
    gd = e // SGU_GROUPS
    nb = s // SGU_BLOCK

    def body(uvz_ref, dy_ref, g_ref, w_ref, b_ref, d_ref, dw_ref, db_ref, dg_ref, wsc, wtsc, dvh_sc, dbacc):
        n = pl.program_id(0)

        @pl.when(n == 0)
        def _():
            msk = _sgu_mask()
            for gi in range(SGU_GROUPS):
                wm = jnp.where(msk, w_ref[gi], 0.0)
                wsc[gi] = wm.astype(BF16)
                wtsc[gi] = wm.T.astype(BF16)
            dw_ref[...] = jnp.zeros_like(dw_ref)
            dg_ref[...] = jnp.zeros_like(dg_ref)
            dbacc[...] = jnp.zeros_like(dbacc)

        v_pre = uvz_ref[:, e:2 * e]
        gv = g_ref[...]
        vhat, rstd, vb = _sgu_norm(v_pre, gv)
        s1 = jnp.zeros((SGU_BLOCK, 1), F32)
        s2 = jnp.zeros((SGU_BLOCK, 1), F32)
        for gi in range(SGU_GROUPS):
            lo = gi * gd
            u_pre = uvz_ref[:, lo:lo + gd]
            zg = uvz_ref[:, 2 * e + lo:2 * e + lo + gd]
            dyg = dy_ref[:, lo:lo + gd]
            ug = _gelu(u_pre)
            sig = _sigmoid(zg)
            vbg = vb[:, lo:lo + gd]
            vhg = vhat[:, lo:lo + gd]
            vm = jnp.dot(wsc[gi], vbg, preferred_element_type=F32) + b_ref[:, lo:lo + gd]
            t = dyg * (zg * sig)
            d_ref[:, lo:lo + gd] = (t * vm * _gelu_grad(u_pre)).astype(BF16)
            dvm = t * ug
            d_ref[:, 2 * e + lo:2 * e + lo + gd] = (dyg * ug * vm * (sig * (1.0 + zg * (1.0 - sig)))).astype(BF16)
            dvm_b = dvm.astype(BF16)
            dv = jnp.dot(wtsc[gi], dvm_b, preferred_element_type=F32)
            dw_ref[gi] += _dot_nt(dvm_b, vbg)
            dbacc[:, lo:lo + gd] += dvm
            dg_ref[:, lo:lo + gd] += jnp.sum(dv * vhg, axis=0, keepdims=True)
            dvh = dv * gv[:, lo:lo + gd]
            dvh_sc[:, lo:lo + gd] = dvh
            s1 = s1 + jnp.sum(dvh, axis=-1, keepdims=True)
            s2 = s2 + jnp.sum(dvh * vhg, axis=-1, keepdims=True)
        dvg = rstd * (dvh_sc[...] - s1 * (1.0 / e) - vhat * (s2 * (1.0 / e)))
        d_ref[:, e:2 * e] = (dvg * _gelu_grad(v_pre)).astype(BF16)

        @pl.when(n == nb - 1)
        def _():
            msk = _sgu_mask()
            for gi in range(SGU_GROUPS):
                dw_ref[gi] = jnp.where(msk, dw_ref[gi], 0.0)
                db_ref[gi] = jnp.sum(dbacc[:, gi * gd:(gi + 1) * gd], axis=1, keepdims=True)

    return _pcall(
        body, name=name, grid=(nb,),
        in_specs=[pl.BlockSpec((SGU_BLOCK, e3), lambda n: (n, 0)),
                  pl.BlockSpec((SGU_BLOCK, e), lambda n: (n, 0)),
                  pl.BlockSpec((1, e), lambda n: (0, 0)),
                  pl.BlockSpec((SGU_GROUPS, SGU_BLOCK, SGU_BLOCK), lambda n: (0, 0, 0)),
                  pl.BlockSpec((SGU_BLOCK, e), lambda n: (0, 0))],
        out_specs=[pl.BlockSpec((SGU_BLOCK, e3), lambda n: (n, 0)),
                   pl.BlockSpec((SGU_GROUPS, SGU_BLOCK, SGU_BLOCK), lambda n: (0, 0, 0)),
                   pl.BlockSpec((SGU_GROUPS, SGU_BLOCK, 1), lambda n: (0, 0, 0)),
                   pl.BlockSpec((1, e), lambda n: (0, 0))],
        out_shape=[jax.ShapeDtypeStruct((s, e3), BF16),
                   jax.ShapeDtypeStruct((SGU_GROUPS, SGU_BLOCK, SGU_BLOCK), F32),
                   jax.ShapeDtypeStruct((SGU_GROUPS, SGU_BLOCK, 1), F32),
                   jax.ShapeDtypeStruct((1, e), F32)],
        scratch_shapes=[pltpu.VMEM((SGU_GROUPS, SGU_BLOCK, SGU_BLOCK), BF16),
                        pltpu.VMEM((SGU_GROUPS, SGU_BLOCK, SGU_BLOCK), BF16),
                        pltpu.VMEM((SGU_BLOCK, e), F32),
                        pltpu.VMEM((SGU_BLOCK, e), F32)],
        compiler_params=_params(("arbitrary",)),
    )(uvz, dy, norm_g, w_s, bias_full)


def _rope_tables(s):
    pos = jnp.arange(s, dtype=F32)
    inv_freq = ROPE_THETA ** (-jnp.arange(0, ROPE, 2, dtype=F32) / ROPE)
    ang = pos[:, None] * inv_freq[None, :]
    cos, sin = jnp.cos(ang), jnp.sin(ang)
    z32 = jnp.zeros((s, HALF), F32)
    z64 = jnp.zeros((s, ROPE), F32)
    ck = jnp.concatenate([cos, cos, z64], axis=1)
    s1k = jnp.concatenate([-sin, z32, z64], axis=1)
    s2k = jnp.concatenate([z32, sin, z64], axis=1)
    return ck, s1k, s2k


def _rot(x, c, s1, s2):
    w = x.shape[-1]
    return x * c + pltpu.roll(x, w - HALF, 1) * s1 + pltpu.roll(x, HALF, 1) * s2


def _rms(cv, n_real):
    r = lax.rsqrt(jnp.sum(cv * cv, axis=-1, keepdims=True) * (1.0 / n_real) + NORM_EPS)
    return r, cv * r


def _mla_norm_fwd(proj, gq, gkv, tabs, name):
    s = proj.shape[0]
    ts = _pick(s, ROW_BLK)
    ck, s1k, s2k = tabs

    def body(cq_ref, ckv_ref, kr_ref, gq_ref, gkv_ref, c_ref, s1_ref, s2_ref, qn_ref, kvn_ref, kro_ref):
        _, nq = _rms(cq_ref[...], Q_RANK)
        qn_ref[...] = (nq * gq_ref[...]).astype(BF16)
        _, nkv = _rms(ckv_ref[...], KV_RANK)
        kvn_ref[...] = (nkv * gkv_ref[...]).astype(BF16)
        kro_ref[...] = _rot(kr_ref[...], c_ref[...], s1_ref[...], s2_ref[...]).astype(BF16)

    tab = pl.BlockSpec((ts, LANE), lambda i: (i, 0))
    return _pcall(
        body, name=name, grid=(s // ts,),
        in_specs=[pl.BlockSpec((ts, Q_RANK_PAD), lambda i: (i, 0)),
                  pl.BlockSpec((ts, KV_RANK), lambda i: (i, PROJ_CKV // KV_RANK)),
                  pl.BlockSpec((ts, LANE), lambda i: (i, PROJ_KR // LANE)),
                  _vec_spec(Q_RANK_PAD), _vec_spec(KV_RANK), tab, tab, tab],
        out_specs=[pl.BlockSpec((ts, Q_RANK_PAD), lambda i: (i, 0)),
                   pl.BlockSpec((ts, KV_RANK), lambda i: (i, 0)), tab],
        out_shape=[jax.ShapeDtypeStruct((s, Q_RANK_PAD), BF16), jax.ShapeDtypeStruct((s, KV_RANK), BF16),
                   jax.ShapeDtypeStruct((s, LANE), BF16)],
        compiler_params=_params(("parallel",)),
    )(proj, proj, proj, gq, gkv, ck, s1k, s2k)


def _transpose_bf16(t):
    return t.astype(F32).T.astype(BF16)


def _diag_mask(tb, transposed):
    r = lax.broadcasted_iota(jnp.int32, (tb, tb), 0) // CHUNK
    c = lax.broadcasted_iota(jnp.int32, (tb, tb), 1) // CHUNK
    return (r <= c) if transposed else (c <= r)


def _attn_fwd(q, kv, kr, proj, tabs, name):
    s = q.shape[0]
    tb = _pick(s, ATT_BLK)
    nb = s // tb
    zcol = PROJ_Z // V_DIM
    mult = SOFTMAX_SCALE * LOG2_E

    hp = ATT_HEADS_PER_STEP

    def body(qf_ref, c_ref, s1_ref, s2_ref, kv_ref, kr_ref, *rest):
        z_refs = rest[:hp]
        o_ref, y_ref, lse_ref, q_ref = rest[hp:hp + 4]
        scratch = rest[hp + 4:]
        kt_sc, vx_sc, m_sc, acc_sc, sa_sc, sb_sc = (scratch[i * hp:(i + 1) * hp] for i in range(6))
        qi = pl.program_id(1)
        heads = range(hp)
        for hh in heads:
            lo = hh * HEAD_PAD
            q_ref[:, lo:lo + NOPE] = (qf_ref[:, lo:lo + NOPE] * mult).astype(BF16)
            q_ref[:, lo + NOPE:lo + HEAD_PAD] = (
                _rot(qf_ref[:, lo + NOPE:lo + HEAD_PAD], c_ref[...], s1_ref[...], s2_ref[...]) * mult).astype(BF16)

        @pl.when(qi == 0)
        def _():
            for hh in heads:
                kcol = hh * (NOPE + V_DIM)
                for b in range(nb):
                    rows = slice(b * tb, (b + 1) * tb)
                    kt_sc[hh][b] = _transpose_bf16(
                        jnp.concatenate([kv_ref[rows, kcol:kcol + NOPE], kr_ref[rows, :]], axis=1))
                    vx_sc[hh][b] = jnp.concatenate(
                        [kv_ref[rows, kcol + NOPE:kcol + NOPE + V_DIM], jnp.ones((tb, V_DIM), BF16)], axis=1)

        for hh in heads:
            m_sc[hh][...] = jnp.full_like(m_sc[hh], -1e30)
            acc_sc[hh][...] = jnp.zeros_like(acc_sc[hh])
        sub = min(tb, ATT_SUB)

        def scores(ki, bufs):
            for hh in heads:
                bufs[hh][...] = jnp.dot(q_ref[:, hh * HEAD_PAD:(hh + 1) * HEAD_PAD], kt_sc[hh][ki],
                                        preferred_element_type=F32)

        def step(ki, bufs, masked):
            for r in range(tb // sub):
                rs = slice(r * sub, (r + 1) * sub)
                for hh in heads:
                    sc = bufs[hh][rs, :]
                    if masked:
                        sc = jnp.where(_diag_mask(tb, False)[rs, :], sc, -1e30)
                    m_prev = m_sc[hh][rs, :]
                    m_new = jnp.maximum(m_prev, jnp.max(sc, axis=-1, keepdims=True))
                    p = jnp.exp2(sc - m_new).astype(BF16)
                    acc_sc[hh][rs, :] = (jnp.exp2(m_prev - m_new) * acc_sc[hh][rs, :]
                                         + jnp.dot(p, vx_sc[hh][ki], preferred_element_type=F32))
                    m_sc[hh][rs, :] = m_new

        def pair(t, carry):
            scores(2 * t + 1, sb_sc)
            step(2 * t, sa_sc, False)
            scores(2 * t + 2, sa_sc)
            step(2 * t + 1, sb_sc, False)
            return carry

        scores(0, sa_sc)
        lax.fori_loop(0, qi // 2, pair, 0)

        @pl.when(qi % 2 == 1)
        def _():
            scores(qi, sb_sc)
            step(qi - 1, sa_sc, False)
            step(qi, sb_sc, True)

        @pl.when(qi % 2 == 0)
        def _():
            step(qi, sa_sc, True)

        for hh in heads:
            l = acc_sc[hh][:, V_DIM:V_DIM + 1]
            o = acc_sc[hh][:, :V_DIM] / l
            z = z_refs[hh][...]
            o_ref[:, hh * V_DIM:(hh + 1) * V_DIM] = o.astype(BF16)
            y_ref[:, hh * V_DIM:(hh + 1) * V_DIM] = (o * (z * _sigmoid(z))).astype(BF16)
            lse_cols = jnp.broadcast_to(m_sc[hh][...] + jnp.log2(l), (tb, LANE))
            lse_ref[hh] = lse_cols.T[0:1, :]

    oblk = pl.BlockSpec((tb, hp * V_DIM), lambda g, qi: (qi, g))
    qblk = pl.BlockSpec((tb, hp * HEAD_PAD), lambda g, qi: (qi, g))
    tab = pl.BlockSpec((tb, LANE), lambda g, qi: (qi, 0))
    per_head = lambda shape, dtype: [pltpu.VMEM(shape, dtype)] * hp
    return _pcall(
        body, name=name, grid=(HEADS // hp, nb),
        in_specs=[qblk, tab, tab, tab,
                  pl.BlockSpec((s, hp * (NOPE + V_DIM)), lambda g, qi: (0, g)),
                  pl.BlockSpec((s, LANE), lambda g, qi: (0, 0))]
                 + [pl.BlockSpec((tb, V_DIM), functools.partial(lambda g, qi, hh: (qi, zcol + hp * g + hh), hh=hh))
                    for hh in range(hp)],
        out_specs=[oblk, oblk, pl.BlockSpec((hp, None, 1, tb), lambda g, qi: (g, qi, 0, 0)), qblk],
        out_shape=[jax.ShapeDtypeStruct((s, MLA_WIDTH), BF16), jax.ShapeDtypeStruct((s, MLA_WIDTH), BF16),
                   jax.ShapeDtypeStruct((HEADS, nb, 1, tb), F32), jax.ShapeDtypeStruct((s, HEADS * HEAD_PAD), BF16)],
        scratch_shapes=(per_head((nb, HEAD_PAD, tb), BF16) + per_head((nb, tb, HEAD_PAD), BF16)
                        + per_head((tb, 1), F32) + per_head((tb, HEAD_PAD), F32)
                        + per_head((tb, tb), F32) + per_head((tb, tb), F32)),
        compiler_params=_params(("parallel", "arbitrary")),
    )(q, *tabs, kv, kr, *([proj] * hp))


def _attn_bwd(q_cat, kv, kr, do, o, lse, tabs, name):
    s = q_cat.shape[0]
    tb = _pick(s, ATT_BLK)
    nb = s // tb
    ln2 = math.log(2.0)

    hp = ATT_HEADS_PER_STEP
    heads = range(hp)

    def body(q_ref, do_ref, o_ref, lse_ref, kv_ref, kr_ref, c_ref, s1_ref, s2_ref, dq_ref, dkv_ref, dkr_ref, *scratch):
        qt_sc, dot_sc, delta_sc, dqt_sc, dk_sc, dv_sc = (scratch[i * hp:(i + 1) * hp] for i in range(6))
        ki = pl.program_id(1)
        qcols = lambda hh: slice(hh * HEAD_PAD, (hh + 1) * HEAD_PAD)
        vcols = lambda hh: slice(hh * V_DIM, (hh + 1) * V_DIM)

        @pl.when(ki == 0)
        def _():
            for hh in heads:
                for b in range(nb):
                    rows = slice(b * tb, (b + 1) * tb)
                    qt_sc[hh][b] = _transpose_bf16(q_ref[rows, qcols(hh)])
                    do_t = do_ref[rows, vcols(hh)].astype(F32).T
                    dot_sc[hh][b] = do_t.astype(BF16)
                    delta_sc[hh][b] = jnp.sum(do_t * o_ref[rows, vcols(hh)].astype(F32).T, axis=0, keepdims=True)
                dqt_sc[hh][...] = jnp.zeros_like(dqt_sc[hh])

        k, kt, vb = [], [], []
        for hh in heads:
            kcol = hh * (NOPE + V_DIM)
            k.append(jnp.concatenate([kv_ref[:, kcol:kcol + NOPE], kr_ref[...]], axis=1))
            kt.append(_transpose_bf16(k[hh]))
            vb.append(kv_ref[:, kcol + NOPE:kcol + NOPE + V_DIM])
            dk_sc[hh][...] = jnp.zeros_like(dk_sc[hh])
            dv_sc[hh][...] = jnp.zeros_like(dv_sc[hh])

        def step(qi, masked):
            rows = pl.ds(pl.multiple_of(qi * tb, tb), tb)
            for hh in heads:
                sc_t = jnp.dot(k[hh], qt_sc[hh][qi], preferred_element_type=F32)
                if masked:
                    sc_t = jnp.where(_diag_mask(tb, True), sc_t, -1e30)
                p_t = jnp.exp2(sc_t - lse_ref[hh, qi])
                dp_t = jnp.dot(vb[hh], dot_sc[hh][qi], preferred_element_type=F32)
                ds_t = (p_t * (dp_t - delta_sc[hh][qi])).astype(BF16)
                dv_sc[hh][...] += jnp.dot(p_t.astype(BF16), do_ref[rows, vcols(hh)], preferred_element_type=F32)
                dk_sc[hh][...] += jnp.dot(ds_t, q_ref[rows, qcols(hh)], preferred_element_type=F32)
                dqt_sc[hh][qi] += jnp.dot(kt[hh], ds_t, preferred_element_type=F32)

        step(ki, True)

        def loop_body(qi, carry):
            step(qi, False)
            return carry

        lax.fori_loop(ki + 1, nb, loop_body, 0)

        for hh in heads:
            lo = hh * HEAD_PAD
            dkv_ref[:, lo:lo + NOPE] = (dk_sc[hh][:, :NOPE] * ln2).astype(BF16)
            dkv_ref[:, lo + NOPE:lo + HEAD_PAD] = dv_sc[hh][...].astype(BF16)
            dkr_ref[hh] = dk_sc[hh][:, NOPE:] * ln2

        @pl.when(ki == nb - 1)
        def _():
            for hh in heads:
                lo = hh * HEAD_PAD
                for b in range(nb):
                    rows = slice(b * tb, (b + 1) * tb)
                    dq = dqt_sc[hh][b].T
                    dq_ref[rows, lo:lo + NOPE] = (dq[:, :NOPE] * SOFTMAX_SCALE).astype(BF16)
                    dq_ref[rows, lo + NOPE:lo + HEAD_PAD] = (
                        _rot(dq[:, NOPE:], c_ref[rows, :], -s1_ref[rows, :], -s2_ref[rows, :])
                        * SOFTMAX_SCALE).astype(BF16)

    tab = pl.BlockSpec((s, LANE), lambda g, ki: (0, 0), pipeline_mode=pl.Buffered(1))
    per_head = lambda shape, dtype: [pltpu.VMEM(shape, dtype)] * hp
    return _pcall(
        body, name=name, grid=(HEADS // hp, nb),
        in_specs=[pl.BlockSpec((s, hp * HEAD_PAD), lambda g, ki: (0, g)),
                  pl.BlockSpec((s, hp * V_DIM), lambda g, ki: (0, g)),
                  pl.BlockSpec((s, hp * V_DIM), lambda g, ki: (0, g)),
                  pl.BlockSpec((hp, nb, 1, tb), lambda g, ki: (g, 0, 0, 0)),
                  pl.BlockSpec((tb, hp * (NOPE + V_DIM)), lambda g, ki: (ki, g)),
                  pl.BlockSpec((tb, LANE), lambda g, ki: (ki, 0)), tab, tab, tab],
        out_specs=[pl.BlockSpec((s, hp * HEAD_PAD), lambda g, ki: (0, g)),
                   pl.BlockSpec((tb, hp * HEAD_PAD), lambda g, ki: (ki, g)),
                   pl.BlockSpec((hp, tb, LANE), lambda g, ki: (g, ki, 0))],
        out_shape=[jax.ShapeDtypeStruct((s, HEADS * HEAD_PAD), BF16),
                   jax.ShapeDtypeStruct((s, HEADS * HEAD_PAD), BF16),
                   jax.ShapeDtypeStruct((HEADS, s, LANE), F32)],
        scratch_shapes=(per_head((nb, HEAD_PAD, tb), BF16) + per_head((nb, V_DIM, tb), BF16)
                        + per_head((nb, 1, tb), F32) + per_head((nb, HEAD_PAD, tb), F32)
                        + per_head((tb, HEAD_PAD), F32) + per_head((tb, V_DIM), F32)),
        compiler_params=_params(("parallel", "arbitrary")),
    )(q_cat, do, o, lse, kv, kr, *tabs)


def _mla_gate_bwd(dy, o, proj, name):
    s = dy.shape[0]
    ts = _pick(s, ROW_BLK)

    def body(dy_ref, o_ref, p_ref, do_ref, dp_ref):
        z = p_ref[:, PROJ_Z:]
        dyv = dy_ref[...]
        sig = _sigmoid(z)
        do_ref[...] = (dyv * (z * sig)).astype(BF16)
        dp_ref[:, :PROJ_Z] = jnp.zeros((ts, PROJ_Z), BF16)
        dp_ref[:, PROJ_Z:] = (dyv * o_ref[...].astype(F32) * (sig * (1.0 + z * (1.0 - sig)))).astype(BF16)

    blk = pl.BlockSpec((ts, MLA_WIDTH), lambda i: (i, 0))
    wide = pl.BlockSpec((ts, PROJ_W), lambda i: (i, 0))
    return _pcall(
        body, name=name, grid=(s // ts,),
        in_specs=[blk, blk, wide],
        out_specs=[blk, wide],
        out_shape=[jax.ShapeDtypeStruct((s, MLA_WIDTH), BF16), jax.ShapeDtypeStruct((s, PROJ_W), BF16)],
        compiler_params=_params(("parallel",)),
    )(dy, o, proj)


def _mla_norm_bwd(dqn, dkvn, dkr_heads, proj, gq, gkv, tabs, dproj, name):
    s = proj.shape[0]
    ts = _pick(s, ROW_BLK)
    ck, s1k, s2k = tabs

    def rms_bwd(cv, dn_in, g, n_real):
        r, n = _rms(cv, n_real)
        dg = jnp.sum(dn_in * n, axis=0, keepdims=True)
        dn = dn_in * g
        dc = r * (dn - n * (jnp.sum(dn * n, axis=-1, keepdims=True) * (1.0 / n_real)))
        return dc, dg

    def body(dqn_ref, dkvn_ref, dkr_ref, cq_ref, ckv_ref, gq_ref, gkv_ref, c_ref, s1_ref, s2_ref, dp_in_ref,
             dp_ref, dgq_ref, dgkv_ref):
        @pl.when(pl.program_id(0) == 0)
        def _():
            dgq_ref[...] = jnp.zeros_like(dgq_ref)
            dgkv_ref[...] = jnp.zeros_like(dgkv_ref)

        dcq, dgq = rms_bwd(cq_ref[...], dqn_ref[...], gq_ref[...], Q_RANK)
        dckv, dgkv = rms_bwd(ckv_ref[...], dkvn_ref[...], gkv_ref[...], KV_RANK)
        dgq_ref[...] += dgq
        dgkv_ref[...] += dgkv
        dkr = dkr_ref[0]
        for h in range(1, HEADS):
            dkr = dkr + dkr_ref[h]
        dp_ref[:, PROJ_CQ:PROJ_CKV] = dcq.astype(BF16)
        dp_ref[:, PROJ_CKV:PROJ_KR] = dckv.astype(BF16)
        dp_ref[:, PROJ_KR:PROJ_Z] = _rot(dkr, c_ref[...], -s1_ref[...], -s2_ref[...]).astype(BF16)

    tab = pl.BlockSpec((ts, LANE), lambda i: (i, 0))
    return _pcall(
        body, name=name, grid=(s // ts,),
        in_specs=[pl.BlockSpec((ts, Q_RANK_PAD), lambda i: (i, 0)),
                  pl.BlockSpec((ts, KV_RANK), lambda i: (i, 0)),
                  pl.BlockSpec((HEADS, ts, LANE), lambda i: (0, i, 0)),
                  pl.BlockSpec((ts, Q_RANK_PAD), lambda i: (i, 0)),
                  pl.BlockSpec((ts, KV_RANK), lambda i: (i, PROJ_CKV // KV_RANK)),
                  _vec_spec(Q_RANK_PAD), _vec_spec(KV_RANK), tab, tab, tab, pl.BlockSpec(memory_space=pl.ANY)],
        out_specs=[pl.BlockSpec((ts, PROJ_Z), lambda i: (i, 0)), _vec_spec(Q_RANK_PAD), _vec_spec(KV_RANK)],
        out_shape=[jax.ShapeDtypeStruct((s, PROJ_W), BF16), jax.ShapeDtypeStruct((1, Q_RANK_PAD), F32),
                   jax.ShapeDtypeStruct((1, KV_RANK), F32)],
        input_output_aliases={10: 0},
        compiler_params=_params(("arbitrary",)),
    )(dqn, dkvn, dkr_heads, proj, proj, gq, gkv, ck, s1k, s2k, dproj)


def _ada_mod(cond_raw, ada_w, bias_my, name):
    nl, d, ncol = ada_w.shape
    tk = _pick(d, 512)
    nk = d // tk

    def body(c_ref, w_ref, b_ref, o_ref, acc_ref):
        kk = pl.program_id(1)

        @pl.when(kk == 0)
        def _():
            acc_ref[...] = jnp.zeros_like(acc_ref)

        cv = c_ref[...]
        cond = (cv * _sigmoid(cv)).astype(BF16)
        acc_ref[...] += jnp.dot(cond, w_ref[...].astype(BF16), preferred_element_type=F32)

        @pl.when(kk == nk - 1)
        def _():
            o_ref[...] = acc_ref[...] + b_ref[...]

    return _pcall(
        body, name=name, grid=(nl, nk),
        in_specs=[pl.BlockSpec((N_DEV, tk), lambda l, kk: (0, kk)),
                  pl.BlockSpec((None, tk, ncol), lambda l, kk: (l, kk, 0)),
                  pl.BlockSpec((None, 1, ncol), lambda l, kk: (l, 0, 0))],
        out_specs=pl.BlockSpec((None, N_DEV, ncol), lambda l, kk: (l, 0, 0)),
        out_shape=jax.ShapeDtypeStruct((nl, N_DEV, ncol), F32),
        scratch_shapes=[pltpu.VMEM((N_DEV, ncol), F32)],
        compiler_params=_params(("parallel", "arbitrary")),
    )(cond_raw, ada_w, bias_my.reshape(nl, 1, ncol))


def _adam(w, g, m, v):
    m = ADAM_B1 * m + (1.0 - ADAM_B1) * g
    v = ADAM_B2 * v + (1.0 - ADAM_B2) * (g * g)
    m_hat = m / (1.0 - ADAM_B1 ** ADAM_STEP)
    v_hat = v / (1.0 - ADAM_B2 ** ADAM_STEP)
    delta = -ADAM_LR * (m_hat / (jnp.sqrt(v_hat) + ADAM_EPS) + ADAM_WD * w)
    return delta, m, v


def _ada_bwd_adam(cond_t, dmod_cols, w, m, v, name):
    nl, d, ncol = w.shape
    tk = _pick(d, 512)

    def body(c_ref, dm_ref, w_ref, m_ref, v_ref, g_ref, d_ref, mo_ref, vo_ref):
        cv = c_ref[...]
        cond = (cv * _sigmoid(cv)).astype(BF16)
        g = jnp.dot(cond, dm_ref[...].astype(BF16), preferred_element_type=F32)
        delta, m2, v2 = _adam(w_ref[...], g, m_ref[...], v_ref[...])
        g_ref[...] = g
        d_ref[...] = delta
        mo_ref[...] = m2
        vo_ref[...] = v2

    blk = pl.BlockSpec((None, tk, ncol), lambda l, kk: (l, kk, 0))
    shp = jax.ShapeDtypeStruct((nl, d, ncol), F32)
    return _pcall(
        body, name=name, grid=(nl, d // tk),
        in_specs=[pl.BlockSpec((tk, N_DEV), lambda l, kk: (kk, 0)),
                  pl.BlockSpec((None, N_DEV, ncol), lambda l, kk: (l, 0, 0)), blk, blk, blk],
        out_specs=[blk, blk, blk, blk], out_shape=[shp, shp, shp, shp],
        compiler_params=_params(("parallel", "parallel")),
    )(cond_t, dmod_cols, w, m, v)


def _adam_reduce(recv0, recv1, w, m, v, name):
    nl, r, c = w.shape
    tr = _pick(r, 256) if r % 256 == 0 else r
    tc = _pick(c, 1024)
    n0, n1 = recv0.shape[0] // r, recv1.shape[0] // r

    def body(r0_ref, r1_ref, w_ref, m_ref, v_ref, g_ref, d_ref, mo_ref, vo_ref):
        l = pl.program_id(0)

        def run(rr):
            g = rr[0].astype(F32)
            for sidx in range(1, rr.shape[0]):
                g = g + rr[sidx].astype(F32)
            delta, m2, v2 = _adam(w_ref[...], g, m_ref[...], v_ref[...])
            g_ref[...] = g
            d_ref[...] = delta
            mo_ref[...] = m2
            vo_ref[...] = v2

        @pl.when(l == 0)
        def _():
            run(r0_ref)

        @pl.when(l == 1)
        def _():
            run(r1_ref)

    def rblk(n, layer):
        return pl.BlockSpec((n, tr, tc), lambda l, i, j: (0, jnp.where(l == layer, i, 0), jnp.where(l == layer, j, 0)))

    blk = pl.BlockSpec((None, tr, tc), lambda l, i, j: (l, i, j))
    shp = jax.ShapeDtypeStruct((nl, r, c), F32)
    return _pcall(
        body, name=name, grid=(nl, r // tr, c // tc),
        in_specs=[rblk(n0, 0), rblk(n1, 1), blk, blk, blk],
        out_specs=[blk, blk, blk, blk], out_shape=[shp, shp, shp, shp],
        compiler_params=_params(("arbitrary", "parallel", "parallel")),
    )(recv0.reshape(n0, r, c), recv1.reshape(n1, r, c), w, m, v)


def _adam_small(gathered, w, m, v, name):
    r = w.shape[0]
    tr = _pick(r, 512) if r % 512 == 0 else r

    def body(p_ref, w_ref, m_ref, v_ref, g_ref, d_ref, mo_ref, vo_ref):
        g = p_ref[0]
        for sidx in range(1, N_DEV):
            g = g + p_ref[sidx]
        delta, m2, v2 = _adam(w_ref[...], g, m_ref[...], v_ref[...])
        g_ref[...] = g
        d_ref[...] = delta
        mo_ref[...] = m2
        vo_ref[...] = v2

    blk = pl.BlockSpec((tr, LANE), lambda i: (i, 0))
    shp = jax.ShapeDtypeStruct((r, LANE), F32)
    return _pcall(
        body, name=name, grid=(r // tr,),
        in_specs=[pl.BlockSpec((N_DEV, tr, LANE), lambda i: (0, i, 0)), blk, blk, blk],
        out_specs=[blk, blk, blk, blk], out_shape=[shp, shp, shp, shp],
        compiler_params=_params(("parallel",)),
    )(gathered, w, m, v)


def _my_place():
    x, y, c = lax.axis_index("x"), lax.axis_index("y"), lax.axis_index("c")
    return x, y, c, 4 * x + 2 * y + c


def _peer(x, y, c, k):
    px = 1 - x if (k >> 2) & 1 else x
    py = 1 - y if (k >> 1) & 1 else y
    pc = 1 - c if k & 1 else c
    return (px, py, pc), 4 * px + 2 * py + pc


def _slab(ref, shape, kind, p):
    r, cd = shape
    if kind == "row":
        return ref.at[pl.ds(pl.multiple_of(p * r, SUBLANE), r), :]
    return ref.at[:, pl.ds(pl.multiple_of(p * cd, LANE), cd)]


def _exchange_layout(arrays, kinds, gather):
    shard_shapes, dst_kinds, out_shapes = [], [], []
    for a, kind in zip(arrays, kinds):
        r, cd = a.shape
        if gather:
            shard, dst_kind = (r, cd), kind
        else:
            shard, dst_kind = ((r // N_DEV, cd) if kind == "row" else (r, cd // N_DEV)), "row"
        shard_shapes.append(shard)
        dst_kinds.append(dst_kind)
        full = (shard[0] * N_DEV, shard[1]) if dst_kind == "row" else (shard[0], shard[1] * N_DEV)
        out_shapes.append(jax.ShapeDtypeStruct(full, a.dtype))
    return shard_shapes, dst_kinds, out_shapes


def _exchange_copies(ins, outs, send_sems, recv_sems, sem_of, layout, kinds, gather):
    shard_shapes, dst_kinds, _ = layout
    x, y, c, me = _my_place()

    def src_for(a, p):
        return ins[a] if gather else _slab(ins[a], shard_shapes[a], kinds[a], p)

    def dst_slot(a, p):
        return _slab(outs[a], shard_shapes[a], dst_kinds[a], p)

    def remote(a, k, slot):
        peer, pidx = _peer(x, y, c, k)
        return pltpu.make_async_remote_copy(
            src_ref=src_for(a, pidx), dst_ref=dst_slot(a, me if slot == "mine" else pidx),
            send_sem=send_sems.at[sem_of(a, k)], recv_sem=recv_sems.at[sem_of(a, k)],
            device_id=peer, device_id_type=MESH)

    return remote


def _place_own(src, src_kind, slab_shape, dst_kind, full, name, index=None):
    r, cd = slab_shape
    tr = _pick(r, 512)
    nr = r // tr
    me = _my_place()[3] if index is None else index
    src, layer = src if isinstance(src, tuple) else (src, None)

    def body(me_ref, s_ref, o_ref):
        o_ref[...] = s_ref[...].astype(o_ref.dtype)

    def where(kind):
        if kind is None:
            return lambda i, me_ref: (i, 0)
        if kind == "row":
            return lambda i, me_ref: (me_ref[0] * nr + i, 0)
        return lambda i, me_ref: (i, me_ref[0])

    if layer is None:
        src_spec = pl.BlockSpec((tr, cd), where(src_kind))
    else:
        src_spec = pl.BlockSpec((None, tr, cd), lambda i, me_ref: (layer, *where(src_kind)(i, me_ref)))
    return _pcall(
        body, name=name,
        grid_spec=pltpu.PrefetchScalarGridSpec(
            num_scalar_prefetch=1, grid=(nr,),
            in_specs=[src_spec],
            out_specs=pl.BlockSpec((tr, cd), where(dst_kind))),
        out_shape=jax.ShapeDtypeStruct(full.shape, full.dtype),
        compiler_params=_params(("arbitrary",)),
    )(jnp.reshape(me, (1,)).astype(jnp.int32), src)


def _landing_with_own_slab(arrays, kinds, gather, layout, name=None, order_after=None):
    _, _, _, me = _my_place()
    index = None
    if order_after is not None:
        first = order_after.reshape(-1)[0].astype(jnp.int32)
        index = me + jnp.minimum(jnp.maximum(first, 0), 0)
    lands = []
    for a in range(len(arrays)):
        (r, cd), dst_kind, full = layout[0][a], layout[1][a], layout[2][a]
        if name is not None:
            lands.append(_place_own(arrays[a], None if gather else kinds[a], (r, cd), dst_kind, full, name,
                                    index=index))
            continue
        if gather:
            piece = arrays[a]
        elif kinds[a] == "row":
            piece = lax.dynamic_slice_in_dim(arrays[a], me * r, r, axis=0)
        else:
            piece = lax.dynamic_slice_in_dim(arrays[a], me * cd, cd, axis=1)
        at = (me * r, 0) if dst_kind == "row" else (0, me * cd)
        lands.append(lax.dynamic_update_slice(lax.empty(full.shape, full.dtype), piece, at))
    return lands


def _exchange(arrays, kinds, gather, name, after=None):
    n = len(arrays)
    n_extra = 0 if after is None else 1
    layout = _exchange_layout(arrays, kinds, gather)
    lands = _landing_with_own_slab(arrays, kinds, gather, layout)

    def body(*refs):
        ins, outs = refs[:n], refs[2 * n + n_extra:3 * n + n_extra]
        send_sems, recv_sems = refs[3 * n + n_extra:]
        remote = _exchange_copies(ins, outs, send_sems, recv_sems,
                                  lambda a, k: a * (N_DEV - 1) + k - 1, layout, kinds, gather)
        for a in range(n):
            for k in range(1, N_DEV):
                remote(a, k, "mine").start()
        for a in range(n):
            for k in range(1, N_DEV):
                arrival = remote(a, k, "theirs")
                arrival.wait_send()
                arrival.wait_recv()

    anyspec = pl.BlockSpec(memory_space=pl.ANY)
    outs = _pcall(
        body, comm=True, name=name,
        in_specs=[anyspec] * (2 * n + n_extra), out_specs=[anyspec] * n, out_shape=layout[2],
        input_output_aliases={n + a: a for a in range(n)},
        scratch_shapes=[pltpu.SemaphoreType.DMA((n * (N_DEV - 1),)), pltpu.SemaphoreType.DMA((n * (N_DEV - 1),))],
    )(*arrays, *lands, *([] if after is None else [after]))
    return list(outs)


HBM_SPEC = pl.BlockSpec(memory_space=pltpu.HBM)
SEM_SPEC = pl.BlockSpec(memory_space=pltpu.SEMAPHORE)
ANY_SPEC = pl.BlockSpec(memory_space=pl.ANY)
DATAFLOW = pltpu.SideEffectType.DATAFLOW_SIDE_EFFECTING


def _exchange_start(arrays, kinds, gather, name, after, carry=()):
    n, nc = len(arrays), len(carry)
    layout = _exchange_layout(arrays, kinds, gather)
    lands = _landing_with_own_slab(arrays, kinds, gather, layout, "place_own")

    def body(*refs):
        ins, outs = refs[:n], refs[n:2 * n]
        send_sems, recv_sems = refs[2 * n + nc + 1], refs[2 * n + nc + 2]
        token = refs[2 * n + nc + 3 + 2 * n + nc]
        remote = _exchange_copies(ins, outs, send_sems, recv_sems, lambda a, k: a, layout, kinds, gather)
        for a in range(n):
            for k in range(1, N_DEV):
                remote(a, k, "mine").start()
        token[...] = jnp.zeros_like(token)

    passed = list(arrays) + lands + list(carry)
    res = pl.pallas_call(
        body, name=name,
        out_shape=(pltpu.SemaphoreType.DMA((n,)), pltpu.SemaphoreType.DMA((n,)),
                   *[pltpu.HBM(t.shape, t.dtype) for t in passed], jax.ShapeDtypeStruct((SUBLANE, LANE), F32)),
        in_specs=[HBM_SPEC] * (2 * n + nc) + [ANY_SPEC],
        out_specs=(SEM_SPEC, SEM_SPEC, *([HBM_SPEC] * (2 * n + nc)), pl.BlockSpec(memory_space=pltpu.VMEM)),
        input_output_aliases={i: 2 + i for i in range(2 * n + nc)},
        compiler_params=pltpu.CompilerParams(has_side_effects=DATAFLOW),
    )(*[pltpu.with_memory_space_constraint(t, pltpu.HBM) for t in passed], after)
    handle = (res[0], res[1], list(res[2:2 + n]), list(res[2 + n:2 + 2 * n]), tuple(kinds), gather)
    return handle, res[-1], list(res[2 + 2 * n:2 + 2 * n + nc])


def _exchange_wait(handle, name, after):
    send_sems, recv_sems, ins_thru, lands_thru, kinds, gather = handle
    n = len(ins_thru)
    layout = _exchange_layout(ins_thru, kinds, gather)

    def body(*refs):
        ins, outs = refs[:n], refs[n:2 * n]
        s_sems, r_sems = refs[2 * n], refs[2 * n + 1]
        remote = _exchange_copies(ins, outs, s_sems, r_sems, lambda a, k: a, layout, kinds, gather)
        for a in range(n):
            for k in range(1, N_DEV):
                arrival = remote(a, k, "theirs")
                arrival.wait_send()
                arrival.wait_recv()

    res = pl.pallas_call(
        body, name=name,
        out_shape=[pltpu.HBM(t.shape, t.dtype) for t in ins_thru + lands_thru],
        in_specs=[HBM_SPEC] * (2 * n) + [SEM_SPEC, SEM_SPEC, ANY_SPEC],
        out_specs=[HBM_SPEC] * (2 * n),
        input_output_aliases={i: i for i in range(2 * n)},
        compiler_params=pltpu.CompilerParams(has_side_effects=DATAFLOW),
    )(*ins_thru, *lands_thru, send_sems, recv_sems, after)
    return list(res[n:2 * n])


STAGE1_KS = (1, 2, 4, 6)
FORWARD_KS = (2, 4, 6)


def _gather2_copies(lands, shard_shapes, kinds):
    x, y, c, me = _my_place()

    def slab(a, p):
        return _slab(lands[a], shard_shapes[a], kinds[a], p)

    def stage1(a, k, sems, arriving):
        peer, pidx = _peer(x, y, c, k)
        s = slab(a, pidx if arriving else me)
        return pltpu.make_async_remote_copy(src_ref=s, dst_ref=s, send_sem=sems[0].at[a], recv_sem=sems[1].at[a],
                                            device_id=peer, device_id_type=MESH)

    def stage2(a, k, sems, arriving):
        sib, _ = _peer(x, y, c, 1)
        _, mine = _peer(x, y, c, k)
        _, theirs = _peer(x, y, 1 - c, k)
        s = slab(a, theirs if arriving else mine)
        return pltpu.make_async_remote_copy(src_ref=s, dst_ref=s, send_sem=sems[0].at[a], recv_sem=sems[1].at[a],
                                            device_id=sib, device_id_type=MESH)

    return stage1, stage2


def _gather2_call(lands, sems_in, name, after, make_body, returns_sems):
    n = len(lands)
    n_in = len(sems_in)

    def body(*refs):
        land_refs = refs[:n]
        in_sems = refs[n:n + n_in]
        rest = refs[n + n_in + 1:]
        out_sems = rest[:2] if returns_sems else ()
        make_body(land_refs, in_sems, out_sems)
        if returns_sems:
            token = rest[2 + n]
            token[...] = jnp.zeros_like(token)

    sem_shapes = (pltpu.SemaphoreType.DMA((n,)), pltpu.SemaphoreType.DMA((n,))) if returns_sems else ()
    tok_shape = (jax.ShapeDtypeStruct((SUBLANE, LANE), F32),) if returns_sems else ()
    n_sem_out = len(sem_shapes)
    res = pl.pallas_call(
        body, name=name,
        out_shape=(*sem_shapes, *[pltpu.HBM(t.shape, t.dtype) for t in lands], *tok_shape),
        in_specs=[HBM_SPEC] * n + [SEM_SPEC] * n_in + [ANY_SPEC],
        out_specs=(*([SEM_SPEC] * n_sem_out), *([HBM_SPEC] * n),
                   *([pl.BlockSpec(memory_space=pltpu.VMEM)] if returns_sems else [])),
        input_output_aliases={i: n_sem_out + i for i in range(n)},
        compiler_params=pltpu.CompilerParams(has_side_effects=DATAFLOW),
    )(*[pltpu.with_memory_space_constraint(t, pltpu.HBM) for t in lands], *sems_in, after)
    sems_out = tuple(res[:n_sem_out])
    lands_thru = list(res[n_sem_out:n_sem_out + n])
    return sems_out, lands_thru, (res[-1] if returns_sems else None)


def _gather2_start(shards, kinds, name, after):
    n = len(shards)
    views = [jax.ShapeDtypeStruct(t.shape[1:], t.dtype) for t, _ in shards]
    shard_shapes, dst_kinds, fulls = _exchange_layout(views, kinds, True)
    layout = (shard_shapes, dst_kinds, [jax.ShapeDtypeStruct(f.shape, BF16) for f in fulls])
    lands = _landing_with_own_slab(shards, kinds, True, layout, "place_own", order_after=after)

    def make_body(land_refs, in_sems, out_sems):
        stage1, _ = _gather2_copies(land_refs, layout[0], kinds)
        for a in range(n):
            for k in STAGE1_KS:
                stage1(a, k, out_sems, False).start()

    sems, lands, token = _gather2_call(lands, (), name, after, make_body, True)
    return (sems, lands, layout[0], tuple(kinds)), token


def _gather2_forward(handle, name, after, carry=()):
    sems1, lands, shard_shapes, kinds = handle
    n = len(lands)

    def make_body(land_refs, in_sems, out_sems):
        stage1, stage2 = _gather2_copies(land_refs, shard_shapes, kinds)
        for a in range(n):
            for k in STAGE1_KS:
                arrival = stage1(a, k, in_sems, True)
                arrival.wait_send()
                arrival.wait_recv()
        for a in range(n):
            for k in FORWARD_KS:
                stage2(a, k, out_sems, False).start()

    sems2, passed, token = _gather2_call(list(lands) + list(carry), sems1, name, after, make_body, True)
    return (sems2, passed[:n], shard_shapes, kinds), token, passed[n:]


def _gather2_wait(handle, name, after):
    sems2, lands, shard_shapes, kinds = handle
    n = len(lands)

    def make_body(land_refs, in_sems, out_sems):
        _, stage2 = _gather2_copies(land_refs, shard_shapes, kinds)
        for a in range(n):
            for k in FORWARD_KS:
                arrival = stage2(a, k, in_sems, True)
                arrival.wait_send()
                arrival.wait_recv()

    _, lands, _ = _gather2_call(lands, sems2, name, after, make_body, False)
    return lands


N_CHIP = N_DEV // 2


def _scatter2_pair(g, name):
    r, c8 = g.shape
    cd = c8 // N_DEV

    def body(g_ref, o_ref, send_sems, recv_sems):
        x, y, c, _ = _my_place()
        sib, _ = _peer(x, y, c, 1)

        def copy(ch):
            theirs = 2 * ch + (1 - c)
            return pltpu.make_async_remote_copy(
                src_ref=_slab(g_ref, (r, cd), "col", theirs), dst_ref=_slab(o_ref, (r, cd), "row", ch),
                send_sem=send_sems.at[ch], recv_sem=recv_sems.at[ch], device_id=sib, device_id_type=MESH)

        for ch in range(N_CHIP):
            copy(ch).start()
        for ch in range(N_CHIP):
            copy(ch).wait_send()
            copy(ch).wait_recv()

    return _pcall(
        body, comm=True, name=name,
        in_specs=[ANY_SPEC], out_specs=ANY_SPEC, out_shape=jax.ShapeDtypeStruct((N_CHIP * r, cd), g.dtype),
        scratch_shapes=[pltpu.SemaphoreType.DMA((N_CHIP,)), pltpu.SemaphoreType.DMA((N_CHIP,))],
    )(g)


def _scatter2_add(g, from_sibling, name):
    r, c8 = g.shape
    cd = c8 // N_DEV
    tr = _pick(r, 512)
    nr = r // tr
    _, _, core, _ = _my_place()

    def body(c_ref, g_ref, s_ref, o_ref):
        o_ref[...] = (g_ref[...].astype(F32) + s_ref[...].astype(F32)).astype(o_ref.dtype)

    return _pcall(
        body, name=name,
        grid_spec=pltpu.PrefetchScalarGridSpec(
            num_scalar_prefetch=1, grid=(N_CHIP, nr),
            in_specs=[pl.BlockSpec((tr, cd), lambda ch, i, c_ref: (i, 2 * ch + c_ref[0])),
                      pl.BlockSpec((tr, cd), lambda ch, i, c_ref: (ch * nr + i, 0))],
            out_specs=pl.BlockSpec((tr, cd), lambda ch, i, c_ref: (ch * nr + i, 0))),
        out_shape=jax.ShapeDtypeStruct((N_CHIP * r, cd), g.dtype),
        compiler_params=_params(("arbitrary", "arbitrary")),
    )(jnp.reshape(core, (1,)).astype(jnp.int32), g, from_sibling)


def _scatter2_copies(refs, r):
    part_ref, recv_ref = refs
    x, y, c, _ = _my_place()
    my_chip = 2 * x + y

    def rows(ref, ch):
        return ref.at[pl.ds(pl.multiple_of(ch * r, SUBLANE), r), :]

    def copy(k, sems, arriving):
        peer, _ = _peer(x, y, c, k)
        peer_chip = 2 * peer[0] + peer[1]
        return pltpu.make_async_remote_copy(
            src_ref=rows(part_ref, peer_chip), dst_ref=rows(recv_ref, peer_chip if arriving else my_chip),
            send_sem=sems[0].at[0], recv_sem=sems[1].at[0], device_id=peer, device_id_type=MESH)

    return copy


def _scatter2_start(g, name, after, carry=()):
    r = g.shape[0]
    x, y, _, _ = _my_place()
    from_sibling = _scatter2_pair(g, name + "_pair")
    partial = _scatter2_add(g, from_sibling, name + "_add")
    cd = partial.shape[1]
    recv = _place_own(partial, "row", (r, cd), "row", jax.ShapeDtypeStruct(partial.shape, partial.dtype),
                      "place_own", index=2 * x + y)

    def make_body(land_refs, in_sems, out_sems):
        copy = _scatter2_copies(land_refs[:2], r)
        for k in FORWARD_KS:
            copy(k, out_sems, False).start()

    sems, lands, token = _gather2_call([partial, recv] + list(carry), (), name, after, make_body, True)
    return (sems, lands[:2], r), token, lands[2:]


def _scatter2_wait(handle, name, after):
    sems, lands, r = handle

    def make_body(land_refs, in_sems, out_sems):
        copy = _scatter2_copies(land_refs, r)
        for k in FORWARD_KS:
            arrival = copy(k, in_sems, True)
            arrival.wait_send()
            arrival.wait_recv()

    _, lands, _ = _gather2_call(lands, sems, name, after, make_body, False)
    return lands[1]


def _pad_mla_w_in(w):
    d = w.shape[0]
    z = lambda n: jnp.zeros((d, n), w.dtype)
    o1, o2, o3 = Q_RANK, Q_RANK + KV_RANK, Q_RANK + KV_RANK + ROPE
    return jnp.concatenate([w[:, :o1], z(Q_RANK_PAD - Q_RANK), w[:, o1:o2], w[:, o2:o3], z(LANE - ROPE), w[:, o3:]], axis=1)


def _unpad_mla_w_in(g):
    return jnp.concatenate([g[:, :Q_RANK], g[:, PROJ_CKV:PROJ_KR], g[:, PROJ_KR:PROJ_KR + ROPE], g[:, PROJ_Z:]], axis=1)


def _pad_w_uq(w):
    w3 = w.reshape(Q_RANK, HEADS, NOPE + ROPE)
    w3 = jnp.pad(w3, ((0, Q_RANK_PAD - Q_RANK), (0, 0), (0, HEAD_PAD - NOPE - ROPE)))
    return w3.reshape(Q_RANK_PAD, HEADS * HEAD_PAD)


def _unpad_w_uq(g):
    return g[:Q_RANK].reshape(Q_RANK, HEADS, HEAD_PAD)[:, :, :NOPE + ROPE].reshape(Q_RANK, HEADS * (NOPE + ROPE))


def _pack(pieces):
    flat = [p.reshape(-1).astype(F32) for p in pieces]
    tot = sum(f.shape[0] for f in flat)
    unit = SUBLANE * LANE
    padn = (-tot) % unit
    if padn:
        flat.append(jnp.zeros((padn,), F32))
    return jnp.concatenate(flat).reshape(-1, LANE)


def _unpack(packed, shapes, lead=()):
    flat = packed.reshape(tuple(lead) + (-1,))
    out, off = [], 0
    for shp in shapes:
        nel = int(np.prod(shp))
        out.append(flat[..., off:off + nel].reshape(tuple(lead) + tuple(shp)))
        off += nel
    return out


SMALL_GROUPS = {
    "a": [('ada_b', lambda t: t[1:]), ('pre_g', lambda t: t[1:]), ('post_g', lambda t: t[1:]),
          ('sgu_norm_g', lambda t: t[1:]), ('sgu_w_s', lambda t: t[1:]), ('sgu_b_s', lambda t: t[1:]),
          ('mla_q_norm_g', lambda t: t), ('mla_kv_norm_g', lambda t: t)],
    "b": [('ada_b', lambda t: t[0:1, 2 * t.shape[1] // 3:]), ('post_g', lambda t: t[0:1]),
          ('sgu_norm_g', lambda t: t[0:1]), ('sgu_w_s', lambda t: t[0:1]), ('sgu_b_s', lambda t: t[0:1])],
    "c": [('ada_b', lambda t: t[0:1, :2 * t.shape[1] // 3]), ('pre_g', lambda t: t[0:1])],
}


WEIGHTS = ['ada_w', 'ada_b', 'pre_g', 'post_g', 'sgu_w_in', 'sgu_norm_g', 'sgu_w_s', 'sgu_b_s', 'sgu_w_out',
           'mla_w_in', 'mla_q_norm_g', 'mla_kv_norm_g', 'mla_w_uq', 'mla_w_ukv', 'mla_w_out']
INPUTS = ['x', 'c'] + WEIGHTS + ['loss_target'] + ['m_' + n for n in WEIGHTS] + ['v_' + n for n in WEIGHTS]


def kernel(x, c, ada_w, ada_b, pre_g, post_g, sgu_w_in, sgu_norm_g, sgu_w_s, sgu_b_s, sgu_w_out, mla_w_in, mla_q_norm_g, mla_kv_norm_g, mla_w_uq, mla_w_ukv, mla_w_out, loss_target, m_ada_w, m_ada_b, m_pre_g, m_post_g, m_sgu_w_in, m_sgu_norm_g, m_sgu_w_s, m_sgu_b_s, m_sgu_w_out, m_mla_w_in, m_mla_q_norm_g, m_mla_kv_norm_g, m_mla_w_uq, m_mla_w_ukv, m_mla_w_out, v_ada_w, v_ada_b, v_pre_g, v_post_g, v_sgu_w_in, v_sgu_norm_g, v_sgu_w_s, v_sgu_b_s, v_sgu_w_out, v_mla_w_in, v_mla_q_norm_g, v_mla_kv_norm_g, v_mla_w_uq, v_mla_w_ukv, v_mla_w_out):
    given = locals()
    A = {name: given[name] for name in INPUTS}
    x0 = A['x'][0]
    tgt = A['loss_target'][0]
    s, d = x0.shape
    e = 2 * d
    ncol = 3 * d // N_DEV
    _, _, _, me = _my_place()
    ktabs = _rope_tables(s)

    gains = jnp.zeros((SUBLANE, LANE), F32)
    gains = gains.at[0:2, :Q_RANK // N_DEV].set(A['mla_q_norm_g'])
    gains = gains.at[2:4, :KV_RANK // N_DEV].set(A['mla_kv_norm_g'])
    c8 = jnp.broadcast_to(A['c'], (SUBLANE, d))
    cg, gg = _exchange([c8, gains], ["row", "row"], True, "ag_cond")
    cond_raw = cg.reshape(N_DEV, SUBLANE, d)[:, 0, :]
    gg = gg.reshape(N_DEV, SUBLANE, LANE)
    gq_full = jnp.transpose(gg[:, 0:2, :Q_RANK // N_DEV], (1, 0, 2)).reshape(N_MIX, Q_RANK)
    gkv_full = jnp.transpose(gg[:, 2:4, :KV_RANK // N_DEV], (1, 0, 2)).reshape(N_MIX, KV_RANK)
    gq_pad = jnp.pad(gq_full, ((0, 0), (0, Q_RANK_PAD - Q_RANK)))

    bias_my = lax.dynamic_slice_in_dim(A['ada_b'], me * ncol, ncol, axis=1)
    mod_part = _ada_mod(cond_raw, A['ada_w'], bias_my, "ada_mod")
    send = jnp.pad(jnp.transpose(mod_part, (1, 0, 2)), ((0, 0), (0, SUBLANE - DEPTH), (0, 0)))
    (rb,) = _exchange([send.reshape(N_DEV * SUBLANE, ncol)], ["row"], False, "a2a_mod")

    token = rb
    gathers = {}
    for i in range(DEPTH):
        j = i // N_MIX
        if i % N_MIX == 0:
            parts = [("in", [(A['sgu_w_in'], j)], ["col"]), ("out", [(A['sgu_w_out'], j)], ["row"])]
        else:
            parts = [("all", [(A['mla_w_in'], j), (A['mla_w_uq'], j), (A['mla_w_ukv'], j), (A['mla_w_out'], j)],
                      ["col", "col", "col", "row"])]
        for part, shards, kinds in parts:
            gathers[(i, part)], token = _gather2_start(shards, kinds, f"ag_start_{i}_{part}", token)

    def forward_gathers(i, carried):
        for key in [k for k in gathers if k[0] == i]:
            gathers[key], _, (carried,) = _gather2_forward(gathers[key], f"ag_forward_{key[0]}_{key[1]}", token,
                                                           carry=[carried])
        return carried

    mod = jnp.transpose(rb.reshape(N_DEV, SUBLANE, ncol)[:, :DEPTH, :], (1, 0, 2)).reshape(DEPTH, 3 * d) + token[0, 0]
    shift = [mod[i:i + 1, :d] for i in range(DEPTH)]
    scale = [mod[i:i + 1, d:2 * d] for i in range(DEPTH)]
    gate = [mod[i:i + 1, 2 * d:] for i in range(DEPTH)]

    saved = []
    x = x0
    h = _pre_fwd(x, A['pre_g'][0:1], scale[0], shift[0], "pre_fwd")
    for i in range(DEPTH):
        j = i // N_MIX
        if i == 0:
            gathers[(0, "in")], _, (h,) = _gather2_forward(gathers[(0, "in")], "ag_forward_0_in", token, carry=[h])
        if i % N_MIX == 0:
            (w_in,) = _gather2_wait(gathers[(i, "in")], f"ag_wait_{i}_in", h)
            uvz = _mm(h, w_in, "nn", F32, "sgu_in")
            if i == 0:
                gathers[(0, "out")], _, (uvz,) = _gather2_forward(gathers[(0, "out")], "ag_forward_0_out", token,
                                                                  carry=[uvz])
            bias_full = jnp.repeat(A['sgu_b_s'][j].T, e // SGU_GROUPS, axis=1)
            ng = A['sgu_norm_g'][j:j + 1]
            y = _sgu_mid_fwd(uvz, ng, A['sgu_w_s'][j], bias_full, "sgu_mid_fwd")
            y = forward_gathers(i + 1, y)
            (w_out,) = _gather2_wait(gathers[(i, "out")], f"ag_wait_{i}_out", y)
            out = _mm(y, w_out, "nn", F32, "sgu_out")
            saved.append(dict(x=x, h=h, uvz=uvz, y=y, out=out, w_in=w_in, w_out=w_out, bias=bias_full, ng=ng))
        else:
            w_in, w_uq, w_ukv, w_out = _gather2_wait(gathers[(i, "all")], f"ag_wait_{i}_all", h)
            w_in = _pad_mla_w_in(w_in)
            w_uq = _pad_w_uq(w_uq)
            gq, gkv = gq_pad[j:j + 1], gkv_full[j:j + 1]
            proj = _mm(h, w_in, "nn", F32, "mla_in")
            qn, kvn, kr = _mla_norm_fwd(proj, gq, gkv, ktabs, "mla_norm_fwd")
            q = _mm(qn, w_uq, "nn", F32, "mla_uq")
            kv = _mm(kvn, w_ukv, "nn", BF16, "mla_ukv")
            o, y, lse, q_cat = _attn_fwd(q, kv, kr, proj, ktabs, "attn_fwd")
            y = forward_gathers(i + 1, y)
            out = _mm(y, w_out, "nn", F32, "mla_out")
            saved.append(dict(x=x, h=h, proj=proj, qn=qn, kvn=kvn, kr=kr, q_cat=q_cat, kv=kv, o=o, y=y, lse=lse,
                              out=out, w_in=w_in, w_uq=w_uq, w_ukv=w_ukv, w_out=w_out, gq=gq, gkv=gkv))
        if i + 1 < DEPTH:
            x, h = _post_pre_fwd(x, out, gate[i], A['post_g'][i:i + 1], A['pre_g'][i + 1:i + 2], scale[i + 1],
                                 shift[i + 1], "post_pre_fwd")

    last = DEPTH - 1
    d_shift, d_scale, d_gate = [None] * DEPTH, [None] * DEPTH, [None] * DEPTH
    d_pre, d_post = [None] * DEPTH, [None] * DEPTH
    dx, dout, d_gate[last], d_post[last], loss_row = _last_post_loss_bwd(
        x, saved[last]['out'], gate[last], A['post_g'][last:], tgt, "last_post_loss_bwd")
    loss = lax.psum(loss_row[0, 0], ("x", "y", "c"))

    scatters, scatters_out = [None] * DEPTH, [None] * DEPTH
    small, small_grads, small_handles = {}, {}, {}
    for i in reversed(range(DEPTH)):
        j = i // N_MIX
        sv = saved[i]
        if i % N_MIX == 0:
            dy = _mm(dout, sv['w_out'], "nt", F32, "sgu_out_dx")
            g_w_out = _mm(sv['y'], dout, "tn", BF16, "sgu_out_dw")
            scatters_out[i], token, (dy,) = _exchange_start([g_w_out], ["row"], False, f"rs_out_start_{i}", token,
                                                            carry=[dy])
            duvz, dws, dbs, dng = _sgu_mid_bwd(sv['uvz'], dy, sv['ng'], A['sgu_w_s'][j], sv['bias'], "sgu_mid_bwd")
            small[('sgu', j)] = (dws, dbs.reshape(SGU_GROUPS, SGU_BLOCK), dng)
            g_w_in = _mm(sv['h'], duvz, "tn", BF16, "sgu_in_dw")
            if i == 0:
                scatters[i], token, (duvz,) = _scatter2_start(g_w_in, "rs2_start_0", token, carry=[duvz])
                small_grads["b"] = [d_gate[0], d_post[0], dng, dws[None], small[('sgu', 0)][1][None]]
                small_handles["b"], token, (duvz,) = _exchange_start(
                    [_pack(small_grads["b"])], ["row"], True, "ag_small_b_start", token, carry=[duvz])
            else:
                scatters[i], token, (duvz,) = _exchange_start([g_w_in], ["col"], False, f"rs_start_{i}", token,
                                                              carry=[duvz])
            dh = _mm(duvz, sv['w_in'], "nt", F32, "sgu_in_dx")
        else:
            dy = _mm(dout, sv['w_out'], "nt", F32, "mla_out_dx")
            g_w_out = _mm(sv['y'], dout, "tn", BF16, "mla_out_dw")
            scatters_out[i], token, (dy,) = _exchange_start([g_w_out], ["row"], False, f"rs_out_start_{i}", token,
                                                            carry=[dy])
            do, dproj = _mla_gate_bwd(dy, sv['o'], sv['proj'], "mla_gate_bwd")
            dq_b, dkv, dkr_heads = _attn_bwd(sv['q_cat'], sv['kv'], sv['kr'], do, sv['o'], sv['lse'], ktabs,
                                             "attn_bwd")
            dqn = _mm(dq_b, sv['w_uq'], "nt", F32, "mla_uq_dx")
            g_w_uq = _unpad_w_uq(_mm(sv['qn'], dq_b, "tn", BF16, "mla_uq_dw"))
            dkvn = _mm(dkv, sv['w_ukv'], "nt", F32, "mla_ukv_dx")
            g_w_ukv = _mm(sv['kvn'], dkv, "tn", BF16, "mla_ukv_dw")
            dproj, dgq, dgkv = _mla_norm_bwd(dqn, dkvn, dkr_heads, sv['proj'], sv['gq'], sv['gkv'], ktabs, dproj,
                                             "mla_norm_bwd")
            g_w_in = _unpad_mla_w_in(_mm(sv['h'], dproj, "tn", BF16, "mla_in_dw"))
            scatters[i], token, (dproj,) = _exchange_start(
                [g_w_in, g_w_uq, g_w_ukv], ["col", "col", "col"], False, f"rs_start_{i}", token, carry=[dproj])
            dh = _mm(dproj, sv['w_in'], "nt", F32, "mla_in_dx")
            small[('mla', j)] = (dgq[:, :Q_RANK], dgkv)
        if i > 0:
            (dx, d_shift[i], d_scale[i], d_pre[i], dout, d_gate[i - 1], d_post[i - 1]) = _pre_post_bwd(
                dh, sv['x'], dx, A['pre_g'][i:i + 1], scale[i], saved[i - 1]['out'], gate[i - 1],
                A['post_g'][i - 1:i], "pre_post_bwd")
        else:
            dx, d_shift[i], d_scale[i], d_pre[i] = _pre_bwd(dh, sv['x'], dx, A['pre_g'][i:i + 1], scale[i], "pre_bwd")
        if i == 1:
            small_grads["a"] = [
                jnp.concatenate([jnp.concatenate([d_shift[l], d_scale[l], d_gate[l]], axis=1)
                                 for l in range(1, DEPTH)], axis=0),
                jnp.concatenate(d_pre[1:], axis=0), jnp.concatenate(d_post[1:], axis=0),
                small[('sgu', 1)][2], small[('sgu', 1)][0][None], small[('sgu', 1)][1][None],
                jnp.concatenate([small[('mla', jj)][0] for jj in range(N_MIX)], axis=0),
                jnp.concatenate([small[('mla', jj)][1] for jj in range(N_MIX)], axis=0)]
            small_handles["a"], token, (dx,) = _exchange_start(
                [_pack(small_grads["a"])], ["row"], True, "ag_small_a_start", token, carry=[dx])

    res = {}

    def big(name, recv0, recv1):
        res[name] = _adam_reduce(recv0, recv1, A[name], A['m_' + name], A['v_' + name], "adam_" + name)

    def small_full(name, prefix):
        t = A[prefix + name]
        if name in ('mla_q_norm_g', 'mla_kv_norm_g'):
            t = lax.dynamic_update_slice_in_dim(jnp.zeros((t.shape[0], t.shape[1] * N_DEV), F32), t,
                                                me * t.shape[1], axis=1)
        return t

    def finish(i, after):
        first = ([_scatter2_wait(scatters[i], "rs2_wait_0", after)] if i == 0
                 else _exchange_wait(scatters[i], f"rs_wait_{i}", after))
        return first + _exchange_wait(scatters_out[i], f"rs_out_wait_{i}", after)

    recv_mla = {j: finish(N_MIX * j + 1, dx) for j in reversed(range(N_MIX))}
    for idx_w, name in enumerate(['mla_w_in', 'mla_w_uq', 'mla_w_ukv', 'mla_w_out']):
        big(name, recv_mla[0][idx_w], recv_mla[1][idx_w])

    recv_sgu = {j: finish(N_MIX * j, res['mla_w_out'][0]) for j in reversed(range(N_MIX))}
    for idx_w, name in enumerate(['sgu_w_in', 'sgu_w_out']):
        big(name, recv_sgu[0][idx_w], recv_sgu[1][idx_w])

    small_grads["c"] = [jnp.concatenate([d_shift[0], d_scale[0]], axis=1), d_pre[0]]
    gathered = {tag: _exchange_wait(small_handles[tag], f"ag_small_{tag}_wait", res['sgu_w_in'][0])[0]
                for tag in ("a", "b")}
    (gathered["c"],) = _exchange([_pack(small_grads["c"])], ["row"], True, "ag_small_c", after=res['sgu_w_in'][0])

    upd, parts = {}, {}
    for tag, group in SMALL_GROUPS.items():
        shapes = [g.shape for g in small_grads[tag]]
        packs = [_pack([pick(small_full(name, prefix)) for name, pick in group]) for prefix in ('', 'm_', 'v_')]
        outs4 = _adam_small(gathered[tag].reshape(N_DEV, -1, LANE), *packs, "adam_small_" + tag)
        upd[tag] = [_unpack(t, shapes) for t in outs4]
        parts[tag] = _unpack(gathered[tag].reshape(N_DEV, -1), shapes, lead=(N_DEV,))
    for k_out in range(4):
        ab_rest, pre_rest, post_rest, ng_1, ws_1, bs_1, gq_all, gkv_all = upd["a"][k_out]
        ab_gate0, post_0, ng_0, ws_0, bs_0 = upd["b"][k_out]
        ab_0, pre_0 = upd["c"][k_out]
        for name, val in (
                ('ada_b', jnp.concatenate([jnp.concatenate([ab_0, ab_gate0], axis=1), ab_rest], axis=0)),
                ('pre_g', jnp.concatenate([pre_0, pre_rest], axis=0)),
                ('post_g', jnp.concatenate([post_0, post_rest], axis=0)),
                ('sgu_norm_g', jnp.concatenate([ng_0, ng_1], axis=0)),
                ('sgu_w_s', jnp.concatenate([ws_0, ws_1], axis=0)),
                ('sgu_b_s', jnp.concatenate([bs_0, bs_1], axis=0)),
                ('mla_q_norm_g', gq_all), ('mla_kv_norm_g', gkv_all)):
            if name in ('mla_q_norm_g', 'mla_kv_norm_g'):
                wdt = A[name].shape[1]
                val = lax.dynamic_slice_in_dim(val, me * wdt, wdt, axis=1)
            res.setdefault(name, [None] * 4)[k_out] = val

    dmod_all = jnp.concatenate([jnp.concatenate([parts["c"][0], parts["b"][0]], axis=2), parts["a"][0]], axis=1)
    dmod_cols = jnp.transpose(lax.dynamic_slice_in_dim(dmod_all, me * ncol, ncol, axis=2), (1, 0, 2))
    res['ada_w'] = _ada_bwd_adam(jnp.transpose(cond_raw), dmod_cols, A['ada_w'], A['m_ada_w'], A['v_ada_w'], "ada_bwd")

    outs = [loss, dx[None]]
    for k_out in range(4):
        outs += [res[n][k_out] for n in WEIGHTS]
    return tuple(outs)
```

```python
import functools
import math

import numpy as np
import jax
import jax.numpy as jnp
from jax import lax
from jax.experimental import pallas as pl
from jax.experimental.pallas import tpu as pltpu

F32 = jnp.float32
BF16 = jnp.bfloat16
MESH = pl.DeviceIdType.MESH

N_DEV = 8
DEPTH = 4
N_MIX = 2
NORM_EPS = 1e-6
CHUNK = 64
SGU_BLOCK = 128
SGU_GROUPS = 16
HEADS = 16
Q_RANK = 448
Q_RANK_PAD = 512
KV_RANK = 512
NOPE = 128
ROPE = 64
HALF = ROPE // 2
V_DIM = 128
HEAD_PAD = 256
ROPE_THETA = 10000.0
MLA_WIDTH = HEADS * V_DIM
LANE = 128
SUBLANE = 8
PROJ_CQ = 0
PROJ_CKV = Q_RANK_PAD
PROJ_KR = Q_RANK_PAD + KV_RANK
PROJ_Z = PROJ_KR + LANE
PROJ_W = PROJ_Z + MLA_WIDTH

ADAM_LR = 0.001
ADAM_B1 = 0.9
ADAM_B2 = 0.999
ADAM_EPS = 1e-08
ADAM_WD = 0.01
ADAM_STEP = 10

VMEM_LIMIT = 56 * 1024 * 1024
ATT_BLK = 512
ATT_SUB = 256
ATT_HEADS_PER_STEP = 2
ROW_BLK = 512
MM_TM, MM_TN, MM_TK = 1024, 1024, 2048
MM_TILE_BYTES = 40 * 1024 * 1024
SOFTMAX_SCALE = (NOPE + ROPE) ** -0.5
LOG2_E = 1.0 / math.log(2.0)
INV_SQRT2 = 1.0 / math.sqrt(2.0)
INV_SQRT_2PI = 1.0 / math.sqrt(2.0 * math.pi)


def _pcall(body, comm=False, **kw):
    return pl.pallas_call(body, **kw)


def _params(sem=None):
    return pltpu.CompilerParams(dimension_semantics=sem, vmem_limit_bytes=VMEM_LIMIT)


def _pick(dim, pref):
    if dim <= pref:
        return dim
    t = (pref // LANE) * LANE
    while t >= LANE:
        if dim % t == 0:
            return t
        t -= LANE
    return dim


def _gelu(x):
    return 0.5 * x * (1.0 + lax.erf(x * INV_SQRT2))


def _gelu_grad(x):
    return 0.5 * (1.0 + lax.erf(x * INV_SQRT2)) + x * jnp.exp(-0.5 * x * x) * INV_SQRT_2PI


def _sigmoid(x):
    return 0.5 * jnp.tanh(0.5 * x) + 0.5


def _dot_nt(a, b):
    return lax.dot_general(a, b, (((1,), (1,)), ((), ())), preferred_element_type=F32)


def _dot_tn(a, b):
    return lax.dot_general(a, b, (((0,), (0,)), ((), ())), preferred_element_type=F32)


def _mm(a, b, dims, out_dtype, name):
    if dims == "nn":
        (m, k), (k2, n) = a.shape, b.shape
    elif dims == "nt":
        (m, k), (n, k2) = a.shape, b.shape
    else:
        (k, m), (k2, n) = a.shape, b.shape
    assert k == k2, (a.shape, b.shape, dims)
    tm, tn = _pick(m, MM_TM), _pick(n, MM_TN)
    out_bytes = 2 * tm * tn * jnp.dtype(out_dtype).itemsize
    whole_k = 2 * (tm + tn) * k * a.dtype.itemsize + out_bytes <= MM_TILE_BYTES
    tk = k if whole_k else _pick(k, MM_TK)
    nk = k // tk

    def body(a_ref, b_ref, o_ref, *scratch):
        if dims == "nn":
            p = jnp.dot(a_ref[...], b_ref[...], preferred_element_type=F32)
        elif dims == "nt":
            p = _dot_nt(a_ref[...], b_ref[...])
        else:
            p = _dot_tn(a_ref[...], b_ref[...])
        if nk == 1:
            o_ref[...] = p.astype(o_ref.dtype)
            return
        acc_ref, = scratch
        kk = pl.program_id(2)

        @pl.when(kk == 0)
        def _():
            acc_ref[...] = p

        @pl.when(kk > 0)
        def _():
            acc_ref[...] += p

        @pl.when(kk == nk - 1)
        def _():
            o_ref[...] = acc_ref[...].astype(o_ref.dtype)

    if dims == "tn":
        a_spec = pl.BlockSpec((tk, tm), lambda i, j, kk: (kk, i))
    else:
        a_spec = pl.BlockSpec((tm, tk), lambda i, j, kk: (i, kk))
    if dims == "nt":
        b_spec = pl.BlockSpec((tn, tk), lambda i, j, kk: (j, kk))
    else:
        b_spec = pl.BlockSpec((tk, tn), lambda i, j, kk: (kk, j))
    return _pcall(
        body, name=name,
        grid=(m // tm, n // tn, nk),
        in_specs=[a_spec, b_spec],
        out_specs=pl.BlockSpec((tm, tn), lambda i, j, kk: (i, j)),
        out_shape=jax.ShapeDtypeStruct((m, n), out_dtype),
        scratch_shapes=[pltpu.VMEM((tm, tn), F32)] if nk > 1 else [],
        compiler_params=_params(("parallel", "parallel", "arbitrary")),
    )(a, b)


def _row_spec(ts, d):
    return pl.BlockSpec((ts, d), lambda i: (i, 0))


def _vec_spec(d):
    return pl.BlockSpec((1, d), lambda i: (0, 0))


def _pre_fwd(x, g, scale, shift, name):
    s, d = x.shape
    ts = _pick(s, ROW_BLK)

    def body(x_ref, g_ref, sc_ref, sh_ref, h_ref):
        xv = x_ref[...]
        r = lax.rsqrt(jnp.mean(xv * xv, axis=-1, keepdims=True) + NORM_EPS)
        h_ref[...] = ((xv * r * g_ref[...]) * (1.0 + sc_ref[...]) + sh_ref[...]).astype(BF16)

    return _pcall(
        body, name=name, grid=(s // ts,),
        in_specs=[_row_spec(ts, d), _vec_spec(d), _vec_spec(d), _vec_spec(d)],
        out_specs=_row_spec(ts, d),
        out_shape=jax.ShapeDtypeStruct((s, d), BF16),
        compiler_params=_params(("parallel",)),
    )(x, g, scale, shift)


def _pre_bwd(dh, x, dxo, g, scale, name):
    s, d = x.shape
    ts = _pick(s, ROW_BLK)

    def body(dh_ref, x_ref, dxo_ref, g_ref, sc_ref, dx_ref, dsh_ref, dsc_ref, dg_ref):
        i = pl.program_id(0)

        @pl.when(i == 0)
        def _():
            dsh_ref[...] = jnp.zeros_like(dsh_ref)
            dsc_ref[...] = jnp.zeros_like(dsc_ref)
            dg_ref[...] = jnp.zeros_like(dg_ref)

        xv = x_ref[...]
        dhv = dh_ref[...]
        gv = g_ref[...]
        one_sc = 1.0 + sc_ref[...]
        r = lax.rsqrt(jnp.mean(xv * xv, axis=-1, keepdims=True) + NORM_EPS)
        n = xv * r
        dsh_ref[...] += jnp.sum(dhv, axis=0, keepdims=True)
        dsc_ref[...] += jnp.sum(dhv * (n * gv), axis=0, keepdims=True)
        dng = dhv * one_sc
        dg_ref[...] += jnp.sum(dng * n, axis=0, keepdims=True)
        dn = dng * gv
        dx_ref[...] = dxo_ref[...] + r * (dn - n * jnp.mean(dn * n, axis=-1, keepdims=True))

    return _pcall(
        body, name=name, grid=(s // ts,),
        in_specs=[_row_spec(ts, d), _row_spec(ts, d), _row_spec(ts, d), _vec_spec(d), _vec_spec(d)],
        out_specs=[_row_spec(ts, d), _vec_spec(d), _vec_spec(d), _vec_spec(d)],
        out_shape=[jax.ShapeDtypeStruct((s, d), F32)] + [jax.ShapeDtypeStruct((1, d), F32)] * 3,
        compiler_params=_params(("arbitrary",)),
    )(dh, x, dxo, g, scale)


def _last_post_loss_bwd(x, out, gate, post_g, tgt, name):
    s, d = x.shape
    ts = _pick(s, ROW_BLK // 2)
    ns = s // ts

    def body(x_ref, o_ref, gate_ref, pg_ref, t_ref, dx_ref, do_ref, dgate_ref, dpg_ref, loss_ref, acc_ref):
        i = pl.program_id(0)

        @pl.when(i == 0)
        def _():
            for ref in (dgate_ref, dpg_ref, acc_ref):
                ref[...] = jnp.zeros_like(ref)

        o = o_ref[...]
        pgv = pg_ref[...]
        gatev = gate_ref[...]
        r = lax.rsqrt(jnp.mean(o * o, axis=-1, keepdims=True) + NORM_EPS)
        n = o * r
        err = (x_ref[...] + gatev * (n * pgv)) - t_ref[...]
        acc_ref[...] += jnp.sum(err * err, axis=0, keepdims=True)
        dx = err * (1.0 / d)
        dx_ref[...] = dx
        dyn = dx * gatev
        dgate_ref[...] += jnp.sum(dx * (n * pgv), axis=0, keepdims=True)
        dpg_ref[...] += jnp.sum(dyn * n, axis=0, keepdims=True)
        dn = dyn * pgv
        do_ref[...] = (r * (dn - n * jnp.mean(dn * n, axis=-1, keepdims=True))).astype(BF16)

        @pl.when(i == ns - 1)
        def _():
            tot = jnp.sum(acc_ref[...], axis=1, keepdims=True) * (0.5 / d)
            loss_ref[...] = jnp.broadcast_to(tot, loss_ref.shape)

    vec = jax.ShapeDtypeStruct((1, d), F32)
    return _pcall(
        body, name=name, grid=(ns,),
        in_specs=[_row_spec(ts, d), _row_spec(ts, d), _vec_spec(d), _vec_spec(d), _row_spec(ts, d)],
        out_specs=[_row_spec(ts, d), _row_spec(ts, d), _vec_spec(d), _vec_spec(d),
                   pl.BlockSpec((1, LANE), lambda i: (0, 0))],
        out_shape=[jax.ShapeDtypeStruct((s, d), F32), jax.ShapeDtypeStruct((s, d), BF16), vec, vec,
                   jax.ShapeDtypeStruct((1, LANE), F32)],
        scratch_shapes=[pltpu.VMEM((1, d), F32)],
        compiler_params=_params(("arbitrary",)),
    )(x, out, gate, post_g, tgt)


def _post_pre_fwd(x, out, gate, post_g, pre_g, scale, shift, name):
    s, d = x.shape
    ts = _pick(s, ROW_BLK // 2)

    def body(x_ref, o_ref, gate_ref, pg_ref, g_ref, sc_ref, sh_ref, y_ref, h_ref):
        o = o_ref[...]
        r = lax.rsqrt(jnp.mean(o * o, axis=-1, keepdims=True) + NORM_EPS)
        xn = x_ref[...] + gate_ref[...] * (o * r * pg_ref[...])
        y_ref[...] = xn
        r2 = lax.rsqrt(jnp.mean(xn * xn, axis=-1, keepdims=True) + NORM_EPS)
        h_ref[...] = ((xn * r2 * g_ref[...]) * (1.0 + sc_ref[...]) + sh_ref[...]).astype(BF16)

    return _pcall(
        body, name=name, grid=(s // ts,),
        in_specs=[_row_spec(ts, d), _row_spec(ts, d)] + [_vec_spec(d)] * 5,
        out_specs=[_row_spec(ts, d), _row_spec(ts, d)],
        out_shape=[jax.ShapeDtypeStruct((s, d), F32), jax.ShapeDtypeStruct((s, d), BF16)],
        compiler_params=_params(("parallel",)),
    )(x, out, gate, post_g, pre_g, scale, shift)


def _pre_post_bwd(dh, x, dxo, pre_g, scale, out, gate, post_g, name):
    s, d = x.shape
    ts = _pick(s, ROW_BLK // 2)

    def body(dh_ref, x_ref, dxo_ref, g_ref, sc_ref, o_ref, gate_ref, pg_ref,
             dx_ref, dsh_ref, dsc_ref, dg_ref, do_ref, dgate_ref, dpg_ref):
        i = pl.program_id(0)

        @pl.when(i == 0)
        def _():
            for ref in (dsh_ref, dsc_ref, dg_ref, dgate_ref, dpg_ref):
                ref[...] = jnp.zeros_like(ref)

        xv = x_ref[...]
        dhv = dh_ref[...]
        gv = g_ref[...]
        one_sc = 1.0 + sc_ref[...]
        r = lax.rsqrt(jnp.mean(xv * xv, axis=-1, keepdims=True) + NORM_EPS)
        n = xv * r
        dsh_ref[...] += jnp.sum(dhv, axis=0, keepdims=True)
        dsc_ref[...] += jnp.sum(dhv * (n * gv), axis=0, keepdims=True)
        dng = dhv * one_sc
        dg_ref[...] += jnp.sum(dng * n, axis=0, keepdims=True)
        dn = dng * gv
        dx = dxo_ref[...] + r * (dn - n * jnp.mean(dn * n, axis=-1, keepdims=True))
        dx_ref[...] = dx

        o = o_ref[...]
        pgv = pg_ref[...]
        ro = lax.rsqrt(jnp.mean(o * o, axis=-1, keepdims=True) + NORM_EPS)
        no = o * ro
        dyn = dx * gate_ref[...]
        dgate_ref[...] += jnp.sum(dx * (no * pgv), axis=0, keepdims=True)
        dpg_ref[...] += jnp.sum(dyn * no, axis=0, keepdims=True)
        dno = dyn * pgv
        do_ref[...] = (ro * (dno - no * jnp.mean(dno * no, axis=-1, keepdims=True))).astype(BF16)

    vec = jax.ShapeDtypeStruct((1, d), F32)
    return _pcall(
        body, name=name, grid=(s // ts,),
        in_specs=[_row_spec(ts, d)] * 3 + [_vec_spec(d)] * 2 + [_row_spec(ts, d)] + [_vec_spec(d)] * 2,
        out_specs=[_row_spec(ts, d)] + [_vec_spec(d)] * 3 + [_row_spec(ts, d)] + [_vec_spec(d)] * 2,
        out_shape=[jax.ShapeDtypeStruct((s, d), F32), vec, vec, vec, jax.ShapeDtypeStruct((s, d), BF16), vec, vec],
        compiler_params=_params(("arbitrary",)),
    )(dh, x, dxo, pre_g, scale, out, gate, post_g)


def _sgu_mask():
    t = lax.broadcasted_iota(jnp.int32, (SGU_BLOCK, SGU_BLOCK), 0) // CHUNK
    s = lax.broadcasted_iota(jnp.int32, (SGU_BLOCK, SGU_BLOCK), 1) // CHUNK
    return s <= t


def _sgu_norm(v_pre, g):
    e = v_pre.shape[-1]
    vg = _gelu(v_pre)
    mu = jnp.sum(vg, axis=-1, keepdims=True) * (1.0 / e)
    dlt = vg - mu
    var = jnp.sum(dlt * dlt, axis=-1, keepdims=True) * (1.0 / e)
    rstd = lax.rsqrt(var + NORM_EPS)
    vhat = dlt * rstd
    return vhat, rstd, (vhat * g).astype(BF16)


def _sgu_mid_fwd(uvz, norm_g, w_s, bias_full, name):
    s, e3 = uvz.shape
    e = e3 // 3
    gd = e // SGU_GROUPS
    nb = s // SGU_BLOCK

    def body(uvz_ref, g_ref, w_ref, b_ref, y_ref, wsc):
        @pl.when(pl.program_id(0) == 0)
        def _():
            msk = _sgu_mask()
            for gi in range(SGU_GROUPS):
                wsc[gi] = jnp.where(msk, w_ref[gi], 0.0).astype(BF16)

        _, _, vb = _sgu_norm(uvz_ref[:, e:2 * e], g_ref[...])
        for gi in range(SGU_GROUPS):
            lo = gi * gd
            vm = jnp.dot(wsc[gi], vb[:, lo:lo + gd], preferred_element_type=F32) + b_ref[:, lo:lo + gd]
            zg = uvz_ref[:, 2 * e + lo:2 * e + lo + gd]
            y_ref[:, lo:lo + gd] = (_gelu(uvz_ref[:, lo:lo + gd]) * vm * (zg * _sigmoid(zg))).astype(BF16)

    return _pcall(
        body, name=name, grid=(nb,),
        in_specs=[pl.BlockSpec((SGU_BLOCK, e3), lambda n: (n, 0)),
                  pl.BlockSpec((1, e), lambda n: (0, 0)),
                  pl.BlockSpec((SGU_GROUPS, SGU_BLOCK, SGU_BLOCK), lambda n: (0, 0, 0)),
                  pl.BlockSpec((SGU_BLOCK, e), lambda n: (0, 0))],
        out_specs=pl.BlockSpec((SGU_BLOCK, e), lambda n: (n, 0)),
        out_shape=jax.ShapeDtypeStruct((s, e), BF16),
        scratch_shapes=[pltpu.VMEM((SGU_GROUPS, SGU_BLOCK, SGU_BLOCK), BF16)],
        compiler_params=_params(("arbitrary",)),
    )(uvz, norm_g, w_s, bias_full)


def _sgu_mid_bwd(uvz, dy, norm_g, w_s, bias_full, name):
    s, e3 = uvz.shape
    e = e3 // 3
    gd = e // SGU_GROUPS
    nb = s // SGU_BLOCK

    def body(uvz_ref, dy_ref, g_ref, w_ref, b_ref, d_ref, dw_ref, db_ref, dg_ref, wsc, wtsc, dvh_sc, dbacc):
        n = pl.program_id(0)

        @pl.when(n == 0)
        def _():
            msk = _sgu_mask()
            for gi in range(SGU_GROUPS):
                wm = jnp.where(msk, w_ref[gi], 0.0)
                wsc[gi] = wm.astype(BF16)
                wtsc[gi] = wm.T.astype(BF16)
            dw_ref[...] = jnp.zeros_like(dw_ref)
            dg_ref[...] = jnp.zeros_like(dg_ref)
            dbacc[...] = jnp.zeros_like(dbacc)

        v_pre = uvz_ref[:, e:2 * e]
        gv = g_ref[...]
        vhat, rstd, vb = _sgu_norm(v_pre, gv)
        s1 = jnp.zeros((SGU_BLOCK, 1), F32)
        s2 = jnp.zeros((SGU_BLOCK, 1), F32)
        for gi in range(SGU_GROUPS):
            lo = gi * gd
            u_pre = uvz_ref[:, lo:lo + gd]
            zg = uvz_ref[:, 2 * e + lo:2 * e + lo + gd]
            dyg = dy_ref[:, lo:lo + gd]
            ug = _gelu(u_pre)
            sig = _sigmoid(zg)
            vbg = vb[:, lo:lo + gd]
            vhg = vhat[:, lo:lo + gd]
            vm = jnp.dot(wsc[gi], vbg, preferred_element_type=F32) + b_ref[:, lo:lo + gd]
            t = dyg * (zg * sig)
            d_ref[:, lo:lo + gd] = (t * vm * _gelu_grad(u_pre)).astype(BF16)
            dvm = t * ug
            d_ref[:, 2 * e + lo:2 * e + lo + gd] = (dyg * ug * vm * (sig * (1.0 + zg * (1.0 - sig)))).astype(BF16)
            dvm_b = dvm.astype(BF16)
            dv = jnp.dot(wtsc[gi], dvm_b, preferred_element_type=F32)
            dw_ref[gi] += _dot_nt(dvm_b, vbg)
            dbacc[:, lo:lo + gd] += dvm
            dg_ref[:, lo:lo + gd] += jnp.sum(dv * vhg, axis=0, keepdims=True)
            dvh = dv * gv[:, lo:lo + gd]
            dvh_sc[:, lo:lo + gd] = dvh
            s1 = s1 + jnp.sum(dvh, axis=-1, keepdims=True)
            s2 = s2 + jnp.sum(dvh * vhg, axis=-1, keepdims=True)
        dvg = rstd * (dvh_sc[...] - s1 * (1.0 / e) - vhat * (s2 * (1.0 / e)))
        d_ref[:, e:2 * e] = (dvg * _gelu_grad(v_pre)).astype(BF16)

        @pl.when(n == nb - 1)
        def _():
            msk = _sgu_mask()
            for gi in range(SGU_GROUPS):
                dw_ref[gi] = jnp.where(msk, dw_ref[gi], 0.0)
                db_ref[gi] = jnp.sum(dbacc[:, gi * gd:(gi + 1) * gd], axis=1, keepdims=True)

    return _pcall(
        body, name=name, grid=(nb,),
        in_specs=[pl.BlockSpec((SGU_BLOCK, e3), lambda n: (n, 0)),
                  pl.BlockSpec((SGU_BLOCK, e), lambda n: (n, 0)),
                  pl.BlockSpec((1, e), lambda n: (0, 0)),
                  pl.BlockSpec((SGU_GROUPS, SGU_BLOCK, SGU_BLOCK), lambda n: (0, 0, 0)),
                  pl.BlockSpec((SGU_BLOCK, e), lambda n: (0, 0))],
        out_specs=[pl.BlockSpec((SGU_BLOCK, e3), lambda n: (n, 0)),
                   pl.BlockSpec((SGU_GROUPS, SGU_BLOCK, SGU_BLOCK), lambda n: (0, 0, 0)),
                   pl.BlockSpec((SGU_GROUPS, SGU_BLOCK, 1), lambda n: (0, 0, 0)),
                   pl.BlockSpec((1, e), lambda n: (0, 0))],
        out_shape=[jax.ShapeDtypeStruct((s, e3), BF16),
                   jax.ShapeDtypeStruct((SGU_GROUPS, SGU_BLOCK, SGU_BLOCK), F32),
                   jax.ShapeDtypeStruct((SGU_GROUPS, SGU_BLOCK, 1), F32),
                   jax.ShapeDtypeStruct((1, e), F32)],
        scratch_shapes=[pltpu.VMEM((SGU_GROUPS, SGU_BLOCK, SGU_BLOCK), BF16),
                        pltpu.VMEM((SGU_GROUPS, SGU_BLOCK, SGU_BLOCK), BF16),
                        pltpu.VMEM((SGU_BLOCK, e), F32),
                        pltpu.VMEM((SGU_BLOCK, e), F32)],
        compiler_params=_params(("arbitrary",)),
    )(uvz, dy, norm_g, w_s, bias_full)


def _rope_tables(s):
    pos = jnp.arange(s, dtype=F32)
    inv_freq = ROPE_THETA ** (-jnp.arange(0, ROPE, 2, dtype=F32) / ROPE)
    ang = pos[:, None] * inv_freq[None, :]
    cos, sin = jnp.cos(ang), jnp.sin(ang)
    z32 = jnp.zeros((s, HALF), F32)
    z64 = jnp.zeros((s, ROPE), F32)
    ck = jnp.concatenate([cos, cos, z64], axis=1)
    s1k = jnp.concatenate([-sin, z32, z64], axis=1)
    s2k = jnp.concatenate([z32, sin, z64], axis=1)
    return ck, s1k, s2k


def _rot(x, c, s1, s2):
    w = x.shape[-1]
    return x * c + pltpu.roll(x, w - HALF, 1) * s1 + pltpu.roll(x, HALF, 1) * s2


def _rms(cv, n_real):
    r = lax.rsqrt(jnp.sum(cv * cv, axis=-1, keepdims=True) * (1.0 / n_real) + NORM_EPS)
    return r, cv * r


def _mla_norm_fwd(proj, gq, gkv, tabs, name):
    s = proj.shape[0]
    ts = _pick(s, ROW_BLK)
    ck, s1k, s2k = tabs

    def body(cq_ref, ckv_ref, kr_ref, gq_ref, gkv_ref, c_ref, s1_ref, s2_ref, qn_ref, kvn_ref, kro_ref):
        _, nq = _rms(cq_ref[...], Q_RANK)
        qn_ref[...] = (nq * gq_ref[...]).astype(BF16)
        _, nkv = _rms(ckv_ref[...], KV_RANK)
        kvn_ref[...] = (nkv * gkv_ref[...]).astype(BF16)
        kro_ref[...] = _rot(kr_ref[...], c_ref[...], s1_ref[...], s2_ref[...]).astype(BF16)

    tab = pl.BlockSpec((ts, LANE), lambda i: (i, 0))
    return _pcall(
        body, name=name, grid=(s // ts,),
        in_specs=[pl.BlockSpec((ts, Q_RANK_PAD), lambda i: (i, 0)),
                  pl.BlockSpec((ts, KV_RANK), lambda i: (i, PROJ_CKV // KV_RANK)),
                  pl.BlockSpec((ts, LANE), lambda i: (i, PROJ_KR // LANE)),
                  _vec_spec(Q_RANK_PAD), _vec_spec(KV_RANK), tab, tab, tab],
        out_specs=[pl.BlockSpec((ts, Q_RANK_PAD), lambda i: (i, 0)),
                   pl.BlockSpec((ts, KV_RANK), lambda i: (i, 0)), tab],
        out_shape=[jax.ShapeDtypeStruct((s, Q_RANK_PAD), BF16), jax.ShapeDtypeStruct((s, KV_RANK), BF16),
                   jax.ShapeDtypeStruct((s, LANE), BF16)],
        compiler_params=_params(("parallel",)),
    )(proj, proj, proj, gq, gkv, ck, s1k, s2k)


def _transpose_bf16(t):
    return t.astype(F32).T.astype(BF16)


def _diag_mask(tb, transposed):
    r = lax.broadcasted_iota(jnp.int32, (tb, tb), 0) // CHUNK
    c = lax.broadcasted_iota(jnp.int32, (tb, tb), 1) // CHUNK
    return (r <= c) if transposed else (c <= r)


def _attn_fwd(q, kv, kr, proj, tabs, name):
    s = q.shape[0]
    tb = _pick(s, ATT_BLK)
    nb = s // tb
    zcol = PROJ_Z // V_DIM
    mult = SOFTMAX_SCALE * LOG2_E

    hp = ATT_HEADS_PER_STEP

    def body(qf_ref, c_ref, s1_ref, s2_ref, kv_ref, kr_ref, *rest):
        z_refs = rest[:hp]
        o_ref, y_ref, lse_ref, q_ref = rest[hp:hp + 4]
        scratch = rest[hp + 4:]
        kt_sc, vx_sc, m_sc, acc_sc, sa_sc, sb_sc = (scratch[i * hp:(i + 1) * hp] for i in range(6))
        qi = pl.program_id(1)
        heads = range(hp)
        for hh in heads:
            lo = hh * HEAD_PAD
            q_ref[:, lo:lo + NOPE] = (qf_ref[:, lo:lo + NOPE] * mult).astype(BF16)
            q_ref[:, lo + NOPE:lo + HEAD_PAD] = (
                _rot(qf_ref[:, lo + NOPE:lo + HEAD_PAD], c_ref[...], s1_ref[...], s2_ref[...]) * mult).astype(BF16)

        @pl.when(qi == 0)
        def _():
            for hh in heads:
                kcol = hh * (NOPE + V_DIM)
                for b in range(nb):
                    rows = slice(b * tb, (b + 1) * tb)
                    kt_sc[hh][b] = _transpose_bf16(
                        jnp.concatenate([kv_ref[rows, kcol:kcol + NOPE], kr_ref[rows, :]], axis=1))
                    vx_sc[hh][b] = jnp.concatenate(
                        [kv_ref[rows, kcol + NOPE:kcol + NOPE + V_DIM], jnp.ones((tb, V_DIM), BF16)], axis=1)

        for hh in heads:
            m_sc[hh][...] = jnp.full_like(m_sc[hh], -1e30)
            acc_sc[hh][...] = jnp.zeros_like(acc_sc[hh])
        sub = min(tb, ATT_SUB)

        def scores(ki, bufs):
            for hh in heads:
                bufs[hh][...] = jnp.dot(q_ref[:, hh * HEAD_PAD:(hh + 1) * HEAD_PAD], kt_sc[hh][ki],
                                        preferred_element_type=F32)

        def step(ki, bufs, masked):
            for r in range(tb // sub):
                rs = slice(r * sub, (r + 1) * sub)
                for hh in heads:
                    sc = bufs[hh][rs, :]
                    if masked:
                        sc = jnp.where(_diag_mask(tb, False)[rs, :], sc, -1e30)
                    m_prev = m_sc[hh][rs, :]
                    m_new = jnp.maximum(m_prev, jnp.max(sc, axis=-1, keepdims=True))
                    p = jnp.exp2(sc - m_new).astype(BF16)
                    acc_sc[hh][rs, :] = (jnp.exp2(m_prev - m_new) * acc_sc[hh][rs, :]
                                         + jnp.dot(p, vx_sc[hh][ki], preferred_element_type=F32))
                    m_sc[hh][rs, :] = m_new

        def pair(t, carry):
            scores(2 * t + 1, sb_sc)
            step(2 * t, sa_sc, False)
            scores(2 * t + 2, sa_sc)
            step(2 * t + 1, sb_sc, False)
            return carry

        scores(0, sa_sc)
        lax.fori_loop(0, qi // 2, pair, 0)

        @pl.when(qi % 2 == 1)
        def _():
            scores(qi, sb_sc)
            step(qi - 1, sa_sc, False)
            step(qi, sb_sc, True)

        @pl.when(qi % 2 == 0)
        def _():
            step(qi, sa_sc, True)

        for hh in heads:
            l = acc_sc[hh][:, V_DIM:V_DIM + 1]
            o = acc_sc[hh][:, :V_DIM] / l
            z = z_refs[hh][...]
            o_ref[:, hh * V_DIM:(hh + 1) * V_DIM] = o.astype(BF16)
            y_ref[:, hh * V_DIM:(hh + 1) * V_DIM] = (o * (z * _sigmoid(z))).astype(BF16)
            lse_cols = jnp.broadcast_to(m_sc[hh][...] + jnp.log2(l), (tb, LANE))
            lse_ref[hh] = lse_cols.T[0:1, :]

    oblk = pl.BlockSpec((tb, hp * V_DIM), lambda g, qi: (qi, g))
    qblk = pl.BlockSpec((tb, hp * HEAD_PAD), lambda g, qi: (qi, g))
    tab = pl.BlockSpec((tb, LANE), lambda g, qi: (qi, 0))
    per_head = lambda shape, dtype: [pltpu.VMEM(shape, dtype)] * hp
    return _pcall(
        body, name=name, grid=(HEADS // hp, nb),
        in_specs=[qblk, tab, tab, tab,
                  pl.BlockSpec((s, hp * (NOPE + V_DIM)), lambda g, qi: (0, g)),
                  pl.BlockSpec((s, LANE), lambda g, qi: (0, 0))]
                 + [pl.BlockSpec((tb, V_DIM), functools.partial(lambda g, qi, hh: (qi, zcol + hp * g + hh), hh=hh))
                    for hh in range(hp)],
        out_specs=[oblk, oblk, pl.BlockSpec((hp, None, 1, tb), lambda g, qi: (g, qi, 0, 0)), qblk],
        out_shape=[jax.ShapeDtypeStruct((s, MLA_WIDTH), BF16), jax.ShapeDtypeStruct((s, MLA_WIDTH), BF16),
                   jax.ShapeDtypeStruct((HEADS, nb, 1, tb), F32), jax.ShapeDtypeStruct((s, HEADS * HEAD_PAD), BF16)],
        scratch_shapes=(per_head((nb, HEAD_PAD, tb), BF16) + per_head((nb, tb, HEAD_PAD), BF16)
                        + per_head((tb, 1), F32) + per_head((tb, HEAD_PAD), F32)
                        + per_head((tb, tb), F32) + per_head((tb, tb), F32)),
        compiler_params=_params(("parallel", "arbitrary")),
    )(q, *tabs, kv, kr, *([proj] * hp))


def _attn_bwd(q_cat, kv, kr, do, o, lse, tabs, name):
    s = q_cat.shape[0]
    tb = _pick(s, ATT_BLK)
    nb = s // tb
    ln2 = math.log(2.0)

    hp = ATT_HEADS_PER_STEP
    heads = range(hp)

    def body(q_ref, do_ref, o_ref, lse_ref, kv_ref, kr_ref, c_ref, s1_ref, s2_ref, dq_ref, dkv_ref, dkr_ref, *scratch):
        qt_sc, dot_sc, delta_sc, dqt_sc, dk_sc, dv_sc = (scratch[i * hp:(i + 1) * hp] for i in range(6))
        ki = pl.program_id(1)
        qcols = lambda hh: slice(hh * HEAD_PAD, (hh + 1) * HEAD_PAD)
        vcols = lambda hh: slice(hh * V_DIM, (hh + 1) * V_DIM)

        @pl.when(ki == 0)
        def _():
            for hh in heads:
                for b in range(nb):
                    rows = slice(b * tb, (b + 1) * tb)
                    qt_sc[hh][b] = _transpose_bf16(q_ref[rows, qcols(hh)])
                    do_t = do_ref[rows, vcols(hh)].astype(F32).T
                    dot_sc[hh][b] = do_t.astype(BF16)
                    delta_sc[hh][b] = jnp.sum(do_t * o_ref[rows, vcols(hh)].astype(F32).T, axis=0, keepdims=True)
                dqt_sc[hh][...] = jnp.zeros_like(dqt_sc[hh])

        k, kt, vb = [], [], []
        for hh in heads:
            kcol = hh * (NOPE + V_DIM)
            k.append(jnp.concatenate([kv_ref[:, kcol:kcol + NOPE], kr_ref[...]], axis=1))
            kt.append(_transpose_bf16(k[hh]))
            vb.append(kv_ref[:, kcol + NOPE:kcol + NOPE + V_DIM])
            dk_sc[hh][...] = jnp.zeros_like(dk_sc[hh])
            dv_sc[hh][...] = jnp.zeros_like(dv_sc[hh])

        def step(qi, masked):
            rows = pl.ds(pl.multiple_of(qi * tb, tb), tb)
            for hh in heads:
                sc_t = jnp.dot(k[hh], qt_sc[hh][qi], preferred_element_type=F32)
                if masked:
                    sc_t = jnp.where(_diag_mask(tb, True), sc_t, -1e30)
                p_t = jnp.exp2(sc_t - lse_ref[hh, qi])
                dp_t = jnp.dot(vb[hh], dot_sc[hh][qi], preferred_element_type=F32)
                ds_t = (p_t * (dp_t - delta_sc[hh][qi])).astype(BF16)
                dv_sc[hh][...] += jnp.dot(p_t.astype(BF16), do_ref[rows, vcols(hh)], preferred_element_type=F32)
                dk_sc[hh][...] += jnp.dot(ds_t, q_ref[rows, qcols(hh)], preferred_element_type=F32)
                dqt_sc[hh][qi] += jnp.dot(kt[hh], ds_t, preferred_element_type=F32)

        step(ki, True)

        def loop_body(qi, carry):
            step(qi, False)
            return carry

        lax.fori_loop(ki + 1, nb, loop_body, 0)

        for hh in heads:
            lo = hh * HEAD_PAD
            dkv_ref[:, lo:lo + NOPE] = (dk_sc[hh][:, :NOPE] * ln2).astype(BF16)
            dkv_ref[:, lo + NOPE:lo + HEAD_PAD] = dv_sc[hh][...].astype(BF16)
            dkr_ref[hh] = dk_sc[hh][:, NOPE:] * ln2

        @pl.when(ki == nb - 1)
        def _():
            for hh in heads:
                lo = hh * HEAD_PAD
                for b in range(nb):
                    rows = slice(b * tb, (b + 1) * tb)
                    dq = dqt_sc[hh][b].T
                    dq_ref[rows, lo:lo + NOPE] = (dq[:, :NOPE] * SOFTMAX_SCALE).astype(BF16)
                    dq_ref[rows, lo + NOPE:lo + HEAD_PAD] = (
                        _rot(dq[:, NOPE:], c_ref[rows, :], -s1_ref[rows, :], -s2_ref[rows, :])
                        * SOFTMAX_SCALE).astype(BF16)

    tab = pl.BlockSpec((s, LANE), lambda g, ki: (0, 0), pipeline_mode=pl.Buffered(1))
    per_head = lambda shape, dtype: [pltpu.VMEM(shape, dtype)] * hp
    return _pcall(
        body, name=name, grid=(HEADS // hp, nb),
        in_specs=[pl.BlockSpec((s, hp * HEAD_PAD), lambda g, ki: (0, g)),
                  pl.BlockSpec((s, hp * V_DIM), lambda g, ki: (0, g)),
                  pl.BlockSpec((s, hp * V_DIM), lambda g, ki: (0, g)),
                  pl.BlockSpec((hp, nb, 1, tb), lambda g, ki: (g, 0, 0, 0)),
                  pl.BlockSpec((tb, hp * (NOPE + V_DIM)), lambda g, ki: (ki, g)),
                  pl.BlockSpec((tb, LANE), lambda g, ki: (ki, 0)), tab, tab, tab],
        out_specs=[pl.BlockSpec((s, hp * HEAD_PAD), lambda g, ki: (0, g)),
                   pl.BlockSpec((tb, hp * HEAD_PAD), lambda g, ki: (ki, g)),
                   pl.BlockSpec((hp, tb, LANE), lambda g, ki: (g, ki, 0))],
        out_shape=[jax.ShapeDtypeStruct((s, HEADS * HEAD_PAD), BF16),
                   jax.ShapeDtypeStruct((s, HEADS * HEAD_PAD), BF16),
                   jax.ShapeDtypeStruct((HEADS, s, LANE), F32)],
        scratch_shapes=(per_head((nb, HEAD_PAD, tb), BF16) + per_head((nb, V_DIM, tb), BF16)
                        + per_head((nb, 1, tb), F32) + per_head((nb, HEAD_PAD, tb), F32)
                        + per_head((tb, HEAD_PAD), F32) + per_head((tb, V_DIM), F32)),
        compiler_params=_params(("parallel", "arbitrary")),
    )(q_cat, do, o, lse, kv, kr, *tabs)


def _mla_gate_bwd(dy, o, proj, name):
    s = dy.shape[0]
    ts = _pick(s, ROW_BLK)

    def body(dy_ref, o_ref, p_ref, do_ref, dp_ref):
        z = p_ref[:, PROJ_Z:]
        dyv = dy_ref[...]
        sig = _sigmoid(z)
        do_ref[...] = (dyv * (z * sig)).astype(BF16)
        dp_ref[:, :PROJ_Z] = jnp.zeros((ts, PROJ_Z), BF16)
        dp_ref[:, PROJ_Z:] = (dyv * o_ref[...].astype(F32) * (sig * (1.0 + z * (1.0 - sig)))).astype(BF16)

    blk = pl.BlockSpec((ts, MLA_WIDTH), lambda i: (i, 0))
    wide = pl.BlockSpec((ts, PROJ_W), lambda i: (i, 0))
    return _pcall(
        body, name=name, grid=(s // ts,),
        in_specs=[blk, blk, wide],
        out_specs=[blk, wide],
        out_shape=[jax.ShapeDtypeStruct((s, MLA_WIDTH), BF16), jax.ShapeDtypeStruct((s, PROJ_W), BF16)],
        compiler_params=_params(("parallel",)),
    )(dy, o, proj)


def _mla_norm_bwd(dqn, dkvn, dkr_heads, proj, gq, gkv, tabs, dproj, name):
    s = proj.shape[0]
    ts = _pick(s, ROW_BLK)
    ck, s1k, s2k = tabs

    def rms_bwd(cv, dn_in, g, n_real):
        r, n = _rms(cv, n_real)
        dg = jnp.sum(dn_in * n, axis=0, keepdims=True)
        dn = dn_in * g
        dc = r * (dn - n * (jnp.sum(dn * n, axis=-1, keepdims=True) * (1.0 / n_real)))
        return dc, dg

    def body(dqn_ref, dkvn_ref, dkr_ref, cq_ref, ckv_ref, gq_ref, gkv_ref, c_ref, s1_ref, s2_ref, dp_in_ref,
             dp_ref, dgq_ref, dgkv_ref):
        @pl.when(pl.program_id(0) == 0)
        def _():
            dgq_ref[...] = jnp.zeros_like(dgq_ref)
            dgkv_ref[...] = jnp.zeros_like(dgkv_ref)

        dcq, dgq = rms_bwd(cq_ref[...], dqn_ref[...], gq_ref[...], Q_RANK)
        dckv, dgkv = rms_bwd(ckv_ref[...], dkvn_ref[...], gkv_ref[...], KV_RANK)
        dgq_ref[...] += dgq
        dgkv_ref[...] += dgkv
        dkr = dkr_ref[0]
        for h in range(1, HEADS):
            dkr = dkr + dkr_ref[h]
        dp_ref[:, PROJ_CQ:PROJ_CKV] = dcq.astype(BF16)
        dp_ref[:, PROJ_CKV:PROJ_KR] = dckv.astype(BF16)
        dp_ref[:, PROJ_KR:PROJ_Z] = _rot(dkr, c_ref[...], -s1_ref[...], -s2_ref[...]).astype(BF16)

    tab = pl.BlockSpec((ts, LANE), lambda i: (i, 0))
    return _pcall(
        body, name=name, grid=(s // ts,),
        in_specs=[pl.BlockSpec((ts, Q_RANK_PAD), lambda i: (i, 0)),
                  pl.BlockSpec((ts, KV_RANK), lambda i: (i, 0)),
                  pl.BlockSpec((HEADS, ts, LANE), lambda i: (0, i, 0)),
                  pl.BlockSpec((ts, Q_RANK_PAD), lambda i: (i, 0)),
                  pl.BlockSpec((ts, KV_RANK), lambda i: (i, PROJ_CKV // KV_RANK)),
                  _vec_spec(Q_RANK_PAD), _vec_spec(KV_RANK), tab, tab, tab, pl.BlockSpec(memory_space=pl.ANY)],
        out_specs=[pl.BlockSpec((ts, PROJ_Z), lambda i: (i, 0)), _vec_spec(Q_RANK_PAD), _vec_spec(KV_RANK)],
        out_shape=[jax.ShapeDtypeStruct((s, PROJ_W), BF16), jax.ShapeDtypeStruct((1, Q_RANK_PAD), F32),
                   jax.ShapeDtypeStruct((1, KV_RANK), F32)],
        input_output_aliases={10: 0},
        compiler_params=_params(("arbitrary",)),
    )(dqn, dkvn, dkr_heads, proj, proj, gq, gkv, ck, s1k, s2k, dproj)


def _ada_mod(cond_raw, ada_w, bias_my, name):
    nl, d, ncol = ada_w.shape
    tk = _pick(d, 512)
    nk = d // tk

    def body(c_ref, w_ref, b_ref, o_ref, acc_ref):
        kk = pl.program_id(1)

        @pl.when(kk == 0)
        def _():
            acc_ref[...] = jnp.zeros_like(acc_ref)

        cv = c_ref[...]
        cond = (cv * _sigmoid(cv)).astype(BF16)
        acc_ref[...] += jnp.dot(cond, w_ref[...].astype(BF16), preferred_element_type=F32)

        @pl.when(kk == nk - 1)
        def _():
            o_ref[...] = acc_ref[...] + b_ref[...]

    return _pcall(
        body, name=name, grid=(nl, nk),
        in_specs=[pl.BlockSpec((N_DEV, tk), lambda l, kk: (0, kk)),
                  pl.BlockSpec((None, tk, ncol), lambda l, kk: (l, kk, 0)),
                  pl.BlockSpec((None, 1, ncol), lambda l, kk: (l, 0, 0))],
        out_specs=pl.BlockSpec((None, N_DEV, ncol), lambda l, kk: (l, 0, 0)),
        out_shape=jax.ShapeDtypeStruct((nl, N_DEV, ncol), F32),
        scratch_shapes=[pltpu.VMEM((N_DEV, ncol), F32)],
        compiler_params=_params(("parallel", "arbitrary")),
    )(cond_raw, ada_w, bias_my.reshape(nl, 1, ncol))


def _adam(w, g, m, v):
    m = ADAM_B1 * m + (1.0 - ADAM_B1) * g
    v = ADAM_B2 * v + (1.0 - ADAM_B2) * (g * g)
    m_hat = m / (1.0 - ADAM_B1 ** ADAM_STEP)
    v_hat = v / (1.0 - ADAM_B2 ** ADAM_STEP)
    delta = -ADAM_LR * (m_hat / (jnp.sqrt(v_hat) + ADAM_EPS) + ADAM_WD * w)
    return delta, m, v


def _ada_bwd_adam(cond_t, dmod_cols, w, m, v, name):
    nl, d, ncol = w.shape
    tk = _pick(d, 512)

    def body(c_ref, dm_ref, w_ref, m_ref, v_ref, g_ref, d_ref, mo_ref, vo_ref):
        cv = c_ref[...]
        cond = (cv * _sigmoid(cv)).astype(BF16)
        g = jnp.dot(cond, dm_ref[...].astype(BF16), preferred_element_type=F32)
        delta, m2, v2 = _adam(w_ref[...], g, m_ref[...], v_ref[...])
        g_ref[...] = g
        d_ref[...] = delta
        mo_ref[...] = m2
        vo_ref[...] = v2

    blk = pl.BlockSpec((None, tk, ncol), lambda l, kk: (l, kk, 0))
    shp = jax.ShapeDtypeStruct((nl, d, ncol), F32)
    return _pcall(
        body, name=name, grid=(nl, d // tk),
        in_specs=[pl.BlockSpec((tk, N_DEV), lambda l, kk: (kk, 0)),
                  pl.BlockSpec((None, N_DEV, ncol), lambda l, kk: (l, 0, 0)), blk, blk, blk],
        out_specs=[blk, blk, blk, blk], out_shape=[shp, shp, shp, shp],
        compiler_params=_params(("parallel", "parallel")),
    )(cond_t, dmod_cols, w, m, v)


def _adam_reduce(recv0, recv1, w, m, v, name):
    nl, r, c = w.shape
    tr = _pick(r, 256) if r % 256 == 0 else r
    tc = _pick(c, 1024)
    n0, n1 = recv0.shape[0] // r, recv1.shape[0] // r

    def body(r0_ref, r1_ref, w_ref, m_ref, v_ref, g_ref, d_ref, mo_ref, vo_ref):
        l = pl.program_id(0)

        def run(rr):
            g = rr[0].astype(F32)
            for sidx in range(1, rr.shape[0]):
                g = g + rr[sidx].astype(F32)
            delta, m2, v2 = _adam(w_ref[...], g, m_ref[...], v_ref[...])
            g_ref[...] = g
            d_ref[...] = delta
            mo_ref[...] = m2
            vo_ref[...] = v2

        @pl.when(l == 0)
        def _():
            run(r0_ref)

        @pl.when(l == 1)
        def _():
            run(r1_ref)

    def rblk(n, layer):
        return pl.BlockSpec((n, tr, tc), lambda l, i, j: (0, jnp.where(l == layer, i, 0), jnp.where(l == layer, j, 0)))

    blk = pl.BlockSpec((None, tr, tc), lambda l, i, j: (l, i, j))
    shp = jax.ShapeDtypeStruct((nl, r, c), F32)
    return _pcall(
        body, name=name, grid=(nl, r // tr, c // tc),
        in_specs=[rblk(n0, 0), rblk(n1, 1), blk, blk, blk],
        out_specs=[blk, blk, blk, blk], out_shape=[shp, shp, shp, shp],
        compiler_params=_params(("arbitrary", "parallel", "parallel")),
    )(recv0.reshape(n0, r, c), recv1.reshape(n1, r, c), w, m, v)


def _adam_small(gathered, w, m, v, name):
    r = w.shape[0]
    tr = _pick(r, 512) if r % 512 == 0 else r

    def body(p_ref, w_ref, m_ref, v_ref, g_ref, d_ref, mo_ref, vo_ref):
        g = p_ref[0]
        for sidx in range(1, N_DEV):
            g = g + p_ref[sidx]
        delta, m2, v2 = _adam(w_ref[...], g, m_ref[...], v_ref[...])
        g_ref[...] = g
        d_ref[...] = delta
        mo_ref[...] = m2
        vo_ref[...] = v2

    blk = pl.BlockSpec((tr, LANE), lambda i: (i, 0))
    shp = jax.ShapeDtypeStruct((r, LANE), F32)
    return _pcall(
        body, name=name, grid=(r // tr,),
        in_specs=[pl.BlockSpec((N_DEV, tr, LANE), lambda i: (0, i, 0)), blk, blk, blk],
        out_specs=[blk, blk, blk, blk], out_shape=[shp, shp, shp, shp],
        compiler_params=_params(("parallel",)),
    )(gathered, w, m, v)


def _my_place():
    x, y, c = lax.axis_index("x"), lax.axis_index("y"), lax.axis_index("c")
    return x, y, c, 4 * x + 2 * y + c


def _peer(x, y, c, k):
    px = 1 - x if (k >> 2) & 1 else x
    py = 1 - y if (k >> 1) & 1 else y
    pc = 1 - c if k & 1 else c
    return (px, py, pc), 4 * px + 2 * py + pc


def _slab(ref, shape, kind, p):
    r, cd = shape
    if kind == "row":
        return ref.at[pl.ds(pl.multiple_of(p * r, SUBLANE), r), :]
    return ref.at[:, pl.ds(pl.multiple_of(p * cd, LANE), cd)]


def _exchange_layout(arrays, kinds, gather):
    shard_shapes, dst_kinds, out_shapes = [], [], []
    for a, kind in zip(arrays, kinds):
        r, cd = a.shape
        if gather:
            shard, dst_kind = (r, cd), kind
        else:
            shard, dst_kind = ((r // N_DEV, cd) if kind == "row" else (r, cd // N_DEV)), "row"
        shard_shapes.append(shard)
        dst_kinds.append(dst_kind)
        full = (shard[0] * N_DEV, shard[1]) if dst_kind == "row" else (shard[0], shard[1] * N_DEV)
        out_shapes.append(jax.ShapeDtypeStruct(full, a.dtype))
    return shard_shapes, dst_kinds, out_shapes


def _exchange_copies(ins, outs, send_sems, recv_sems, sem_of, layout, kinds, gather):
    shard_shapes, dst_kinds, _ = layout
    x, y, c, me = _my_place()

    def src_for(a, p):
        return ins[a] if gather else _slab(ins[a], shard_shapes[a], kinds[a], p)

    def dst_slot(a, p):
        return _slab(outs[a], shard_shapes[a], dst_kinds[a], p)

    def remote(a, k, slot):
        peer, pidx = _peer(x, y, c, k)
        return pltpu.make_async_remote_copy(
            src_ref=src_for(a, pidx), dst_ref=dst_slot(a, me if slot == "mine" else pidx),
            send_sem=send_sems.at[sem_of(a, k)], recv_sem=recv_sems.at[sem_of(a, k)],
            device_id=peer, device_id_type=MESH)

    return remote


def _place_own(src, src_kind, slab_shape, dst_kind, full, name, index=None):
    r, cd = slab_shape
    tr = _pick(r, 512)
    nr = r // tr
    me = _my_place()[3] if index is None else index
    src, layer = src if isinstance(src, tuple) else (src, None)

    def body(me_ref, s_ref, o_ref):
        o_ref[...] = s_ref[...].astype(o_ref.dtype)

    def where(kind):
        if kind is None:
            return lambda i, me_ref: (i, 0)
        if kind == "row":
            return lambda i, me_ref: (me_ref[0] * nr + i, 0)
        return lambda i, me_ref: (i, me_ref[0])

    if layer is None:
        src_spec = pl.BlockSpec((tr, cd), where(src_kind))
    else:
        src_spec = pl.BlockSpec((None, tr, cd), lambda i, me_ref: (layer, *where(src_kind)(i, me_ref)))
    return _pcall(
        body, name=name,
        grid_spec=pltpu.PrefetchScalarGridSpec(
            num_scalar_prefetch=1, grid=(nr,),
            in_specs=[src_spec],
            out_specs=pl.BlockSpec((tr, cd), where(dst_kind))),
        out_shape=jax.ShapeDtypeStruct(full.shape, full.dtype),
        compiler_params=_params(("arbitrary",)),
    )(jnp.reshape(me, (1,)).astype(jnp.int32), src)


def _landing_with_own_slab(arrays, kinds, gather, layout, name=None, order_after=None):
    _, _, _, me = _my_place()
    index = None
    if order_after is not None:
        first = order_after.reshape(-1)[0].astype(jnp.int32)
        index = me + jnp.minimum(jnp.maximum(first, 0), 0)
    lands = []
    for a in range(len(arrays)):
        (r, cd), dst_kind, full = layout[0][a], layout[1][a], layout[2][a]
        if name is not None:
            lands.append(_place_own(arrays[a], None if gather else kinds[a], (r, cd), dst_kind, full, name,
                                    index=index))
            continue
        if gather:
            piece = arrays[a]
        elif kinds[a] == "row":
            piece = lax.dynamic_slice_in_dim(arrays[a], me * r, r, axis=0)
        else:
            piece = lax.dynamic_slice_in_dim(arrays[a], me * cd, cd, axis=1)
        at = (me * r, 0) if dst_kind == "row" else (0, me * cd)
        lands.append(lax.dynamic_update_slice(lax.empty(full.shape, full.dtype), piece, at))
    return lands


def _exchange(arrays, kinds, gather, name, after=None):
    n = len(arrays)
    n_extra = 0 if after is None else 1
    layout = _exchange_layout(arrays, kinds, gather)
    lands = _landing_with_own_slab(arrays, kinds, gather, layout)

    def body(*refs):
        ins, outs = refs[:n], refs[2 * n + n_extra:3 * n + n_extra]
        send_sems, recv_sems = refs[3 * n + n_extra:]
        remote = _exchange_copies(ins, outs, send_sems, recv_sems,
                                  lambda a, k: a * (N_DEV - 1) + k - 1, layout, kinds, gather)
        for a in range(n):
            for k in range(1, N_DEV):
                remote(a, k, "mine").start()
        for a in range(n):
            for k in range(1, N_DEV):
                arrival = remote(a, k, "theirs")
                arrival.wait_send()
                arrival.wait_recv()

    anyspec = pl.BlockSpec(memory_space=pl.ANY)
    outs = _pcall(
        body, comm=True, name=name,
        in_specs=[anyspec] * (2 * n + n_extra), out_specs=[anyspec] * n, out_shape=layout[2],
        input_output_aliases={n + a: a for a in range(n)},
        scratch_shapes=[pltpu.SemaphoreType.DMA((n * (N_DEV - 1),)), pltpu.SemaphoreType.DMA((n * (N_DEV - 1),))],
    )(*arrays, *lands, *([] if after is None else [after]))
    return list(outs)


HBM_SPEC = pl.BlockSpec(memory_space=pltpu.HBM)
SEM_SPEC = pl.BlockSpec(memory_space=pltpu.SEMAPHORE)
ANY_SPEC = pl.BlockSpec(memory_space=pl.ANY)
DATAFLOW = pltpu.SideEffectType.DATAFLOW_SIDE_EFFECTING


def _exchange_start(arrays, kinds, gather, name, after, carry=()):
    n, nc = len(arrays), len(carry)
    layout = _exchange_layout(arrays, kinds, gather)
    lands = _landing_with_own_slab(arrays, kinds, gather, layout, "place_own")

    def body(*refs):
        ins, outs = refs[:n], refs[n:2 * n]
        send_sems, recv_sems = refs[2 * n + nc + 1], refs[2 * n + nc + 2]
        token = refs[2 * n + nc + 3 + 2 * n + nc]
        remote = _exchange_copies(ins, outs, send_sems, recv_sems, lambda a, k: a, layout, kinds, gather)
        for a in range(n):
            for k in range(1, N_DEV):
                remote(a, k, "mine").start()
        token[...] = jnp.zeros_like(token)

    passed = list(arrays) + lands + list(carry)
    res = pl.pallas_call(
        body, name=name,
        out_shape=(pltpu.SemaphoreType.DMA((n,)), pltpu.SemaphoreType.DMA((n,)),
                   *[pltpu.HBM(t.shape, t.dtype) for t in passed], jax.ShapeDtypeStruct((SUBLANE, LANE), F32)),
        in_specs=[HBM_SPEC] * (2 * n + nc) + [ANY_SPEC],
        out_specs=(SEM_SPEC, SEM_SPEC, *([HBM_SPEC] * (2 * n + nc)), pl.BlockSpec(memory_space=pltpu.VMEM)),
        input_output_aliases={i: 2 + i for i in range(2 * n + nc)},
        compiler_params=pltpu.CompilerParams(has_side_effects=DATAFLOW),
    )(*[pltpu.with_memory_space_constraint(t, pltpu.HBM) for t in passed], after)
    handle = (res[0], res[1], list(res[2:2 + n]), list(res[2 + n:2 + 2 * n]), tuple(kinds), gather)
    return handle, res[-1], list(res[2 + 2 * n:2 + 2 * n + nc])


def _exchange_wait(handle, name, after):
    send_sems, recv_sems, ins_thru, lands_thru, kinds, gather = handle
    n = len(ins_thru)
    layout = _exchange_layout(ins_thru, kinds, gather)

    def body(*refs):
        ins, outs = refs[:n], refs[n:2 * n]
        s_sems, r_sems = refs[2 * n], refs[2 * n + 1]
        remote = _exchange_copies(ins, outs, s_sems, r_sems, lambda a, k: a, layout, kinds, gather)
        for a in range(n):
            for k in range(1, N_DEV):
                arrival = remote(a, k, "theirs")
                arrival.wait_send()
                arrival.wait_recv()

    res = pl.pallas_call(
        body, name=name,
        out_shape=[pltpu.HBM(t.shape, t.dtype) for t in ins_thru + lands_thru],
        in_specs=[HBM_SPEC] * (2 * n) + [SEM_SPEC, SEM_SPEC, ANY_SPEC],
        out_specs=[HBM_SPEC] * (2 * n),
        input_output_aliases={i: i for i in range(2 * n)},
        compiler_params=pltpu.CompilerParams(has_side_effects=DATAFLOW),
    )(*ins_thru, *lands_thru, send_sems, recv_sems, after)
    return list(res[n:2 * n])


STAGE1_KS = (1, 2, 4, 6)
FORWARD_KS = (2, 4, 6)


def _gather2_copies(lands, shard_shapes, kinds):
    x, y, c, me = _my_place()

    def slab(a, p):
        return _slab(lands[a], shard_shapes[a], kinds[a], p)

    def stage1(a, k, sems, arriving):
        peer, pidx = _peer(x, y, c, k)
        s = slab(a, pidx if arriving else me)
        return pltpu.make_async_remote_copy(src_ref=s, dst_ref=s, send_sem=sems[0].at[a], recv_sem=sems[1].at[a],
                                            device_id=peer, device_id_type=MESH)

    def stage2(a, k, sems, arriving):
        sib, _ = _peer(x, y, c, 1)
        _, mine = _peer(x, y, c, k)
        _, theirs = _peer(x, y, 1 - c, k)
        s = slab(a, theirs if arriving else mine)
        return pltpu.make_async_remote_copy(src_ref=s, dst_ref=s, send_sem=sems[0].at[a], recv_sem=sems[1].at[a],
                                            device_id=sib, device_id_type=MESH)

    return stage1, stage2


def _gather2_call(lands, sems_in, name, after, make_body, returns_sems):
    n = len(lands)
    n_in = len(sems_in)

    def body(*refs):
        land_refs = refs[:n]
        in_sems = refs[n:n + n_in]
        rest = refs[n + n_in + 1:]
        out_sems = rest[:2] if returns_sems else ()
        make_body(land_refs, in_sems, out_sems)
        if returns_sems:
            token = rest[2 + n]
            token[...] = jnp.zeros_like(token)

    sem_shapes = (pltpu.SemaphoreType.DMA((n,)), pltpu.SemaphoreType.DMA((n,))) if returns_sems else ()
    tok_shape = (jax.ShapeDtypeStruct((SUBLANE, LANE), F32),) if returns_sems else ()
    n_sem_out = len(sem_shapes)
    res = pl.pallas_call(
        body, name=name,
        out_shape=(*sem_shapes, *[pltpu.HBM(t.shape, t.dtype) for t in lands], *tok_shape),
        in_specs=[HBM_SPEC] * n + [SEM_SPEC] * n_in + [ANY_SPEC],
        out_specs=(*([SEM_SPEC] * n_sem_out), *([HBM_SPEC] * n),
                   *([pl.BlockSpec(memory_space=pltpu.VMEM)] if returns_sems else [])),
        input_output_aliases={i: n_sem_out + i for i in range(n)},
        compiler_params=pltpu.CompilerParams(has_side_effects=DATAFLOW),
    )(*[pltpu.with_memory_space_constraint(t, pltpu.HBM) for t in lands], *sems_in, after)
    sems_out = tuple(res[:n_sem_out])
    lands_thru = list(res[n_sem_out:n_sem_out + n])
    return sems_out, lands_thru, (res[-1] if returns_sems else None)


def _gather2_start(shards, kinds, name, after):
    n = len(shards)
    views = [jax.ShapeDtypeStruct(t.shape[1:], t.dtype) for t, _ in shards]
    shard_shapes, dst_kinds, fulls = _exchange_layout(views, kinds, True)
    layout = (shard_shapes, dst_kinds, [jax.ShapeDtypeStruct(f.shape, BF16) for f in fulls])
    lands = _landing_with_own_slab(shards, kinds, True, layout, "place_own", order_after=after)

    def make_body(land_refs, in_sems, out_sems):
        stage1, _ = _gather2_copies(land_refs, layout[0], kinds)
        for a in range(n):
            for k in STAGE1_KS:
                stage1(a, k, out_sems, False).start()

    sems, lands, token = _gather2_call(lands, (), name, after, make_body, True)
    return (sems, lands, layout[0], tuple(kinds)), token


def _gather2_forward(handle, name, after, carry=()):
    sems1, lands, shard_shapes, kinds = handle
    n = len(lands)

    def make_body(land_refs, in_sems, out_sems):
        stage1, stage2 = _gather2_copies(land_refs, shard_shapes, kinds)
        for a in range(n):
            for k in STAGE1_KS:
                arrival = stage1(a, k, in_sems, True)
                arrival.wait_send()
                arrival.wait_recv()
        for a in range(n):
            for k in FORWARD_KS:
                stage2(a, k, out_sems, False).start()

    sems2, passed, token = _gather2_call(list(lands) + list(carry), sems1, name, after, make_body, True)
    return (sems2, passed[:n], shard_shapes, kinds), token, passed[n:]


def _gather2_wait(handle, name, after):
    sems2, lands, shard_shapes, kinds = handle
    n = len(lands)

    def make_body(land_refs, in_sems, out_sems):
        _, stage2 = _gather2_copies(land_refs, shard_shapes, kinds)
        for a in range(n):
            for k in FORWARD_KS:
                arrival = stage2(a, k, in_sems, True)
                arrival.wait_send()
                arrival.wait_recv()

    _, lands, _ = _gather2_call(lands, sems2, name, after, make_body, False)
    return lands


N_CHIP = N_DEV // 2


def _scatter2_pair(g, name):
    r, c8 = g.shape
    cd = c8 // N_DEV

    def body(g_ref, o_ref, send_sems, recv_sems):
        x, y, c, _ = _my_place()
        sib, _ = _peer(x, y, c, 1)

        def copy(ch):
            theirs = 2 * ch + (1 - c)
            return pltpu.make_async_remote_copy(
                src_ref=_slab(g_ref, (r, cd), "col", theirs), dst_ref=_slab(o_ref, (r, cd), "row", ch),
                send_sem=send_sems.at[ch], recv_sem=recv_sems.at[ch], device_id=sib, device_id_type=MESH)

        for ch in range(N_CHIP):
            copy(ch).start()
        for ch in range(N_CHIP):
            copy(ch).wait_send()
            copy(ch).wait_recv()

    return _pcall(
        body, comm=True, name=name,
        in_specs=[ANY_SPEC], out_specs=ANY_SPEC, out_shape=jax.ShapeDtypeStruct((N_CHIP * r, cd), g.dtype),
        scratch_shapes=[pltpu.SemaphoreType.DMA((N_CHIP,)), pltpu.SemaphoreType.DMA((N_CHIP,))],
    )(g)


def _scatter2_add(g, from_sibling, name):
    r, c8 = g.shape
    cd = c8 // N_DEV
    tr = _pick(r, 512)
    nr = r // tr
    _, _, core, _ = _my_place()

    def body(c_ref, g_ref, s_ref, o_ref):
        o_ref[...] = (g_ref[...].astype(F32) + s_ref[...].astype(F32)).astype(o_ref.dtype)

    return _pcall(
        body, name=name,
        grid_spec=pltpu.PrefetchScalarGridSpec(
            num_scalar_prefetch=1, grid=(N_CHIP, nr),
            in_specs=[pl.BlockSpec((tr, cd), lambda ch, i, c_ref: (i, 2 * ch + c_ref[0])),
                      pl.BlockSpec((tr, cd), lambda ch, i, c_ref: (ch * nr + i, 0))],
            out_specs=pl.BlockSpec((tr, cd), lambda ch, i, c_ref: (ch * nr + i, 0))),
        out_shape=jax.ShapeDtypeStruct((N_CHIP * r, cd), g.dtype),
        compiler_params=_params(("arbitrary", "arbitrary")),
    )(jnp.reshape(core, (1,)).astype(jnp.int32), g, from_sibling)


def _scatter2_copies(refs, r):
    part_ref, recv_ref = refs
    x, y, c, _ = _my_place()
    my_chip = 2 * x + y

    def rows(ref, ch):
        return ref.at[pl.ds(pl.multiple_of(ch * r, SUBLANE), r), :]

    def copy(k, sems, arriving):
        peer, _ = _peer(x, y, c, k)
        peer_chip = 2 * peer[0] + peer[1]
        return pltpu.make_async_remote_copy(
            src_ref=rows(part_ref, peer_chip), dst_ref=rows(recv_ref, peer_chip if arriving else my_chip),
            send_sem=sems[0].at[0], recv_sem=sems[1].at[0], device_id=peer, device_id_type=MESH)

    return copy


def _scatter2_start(g, name, after, carry=()):
    r = g.shape[0]
    x, y, _, _ = _my_place()
    from_sibling = _scatter2_pair(g, name + "_pair")
    partial = _scatter2_add(g, from_sibling, name + "_add")
    cd = partial.shape[1]
    recv = _place_own(partial, "row", (r, cd), "row", jax.ShapeDtypeStruct(partial.shape, partial.dtype),
                      "place_own", index=2 * x + y)

    def make_body(land_refs, in_sems, out_sems):
        copy = _scatter2_copies(land_refs[:2], r)
        for k in FORWARD_KS:
            copy(k, out_sems, False).start()

    sems, lands, token = _gather2_call([partial, recv] + list(carry), (), name, after, make_body, True)
    return (sems, lands[:2], r), token, lands[2:]


def _scatter2_wait(handle, name, after):
    sems, lands, r = handle

    def make_body(land_refs, in_sems, out_sems):
        copy = _scatter2_copies(land_refs, r)
        for k in FORWARD_KS:
            arrival = copy(k, in_sems, True)
            arrival.wait_send()
            arrival.wait_recv()

    _, lands, _ = _gather2_call(lands, sems, name, after, make_body, False)
    return lands[1]


def _pad_mla_w_in(w):
    d = w.shape[0]
    z = lambda n: jnp.zeros((d, n), w.dtype)
    o1, o2, o3 = Q_RANK, Q_RANK + KV_RANK, Q_RANK + KV_RANK + ROPE
    return jnp.concatenate([w[:, :o1], z(Q_RANK_PAD - Q_RANK), w[:, o1:o2], w[:, o2:o3], z(LANE - ROPE), w[:, o3:]], axis=1)


def _unpad_mla_w_in(g):
    return jnp.concatenate([g[:, :Q_RANK], g[:, PROJ_CKV:PROJ_KR], g[:, PROJ_KR:PROJ_KR + ROPE], g[:, PROJ_Z:]], axis=1)


def _pad_w_uq(w):
    w3 = w.reshape(Q_RANK, HEADS, NOPE + ROPE)
    w3 = jnp.pad(w3, ((0, Q_RANK_PAD - Q_RANK), (0, 0), (0, HEAD_PAD - NOPE - ROPE)))
    return w3.reshape(Q_RANK_PAD, HEADS * HEAD_PAD)


def _unpad_w_uq(g):
    return g[:Q_RANK].reshape(Q_RANK, HEADS, HEAD_PAD)[:, :, :NOPE + ROPE].reshape(Q_RANK, HEADS * (NOPE + ROPE))


def _pack(pieces):
    flat = [p.reshape(-1).astype(F32) for p in pieces]
    tot = sum(f.shape[0] for f in flat)
    unit = SUBLANE * LANE
    padn = (-tot) % unit
    if padn:
        flat.append(jnp.zeros((padn,), F32))
    return jnp.concatenate(flat).reshape(-1, LANE)


def _unpack(packed, shapes, lead=()):
    flat = packed.reshape(tuple(lead) + (-1,))
    out, off = [], 0
    for shp in shapes:
        nel = int(np.prod(shp))
        out.append(flat[..., off:off + nel].reshape(tuple(lead) + tuple(shp)))
        off += nel
    return out


SMALL_GROUPS = {
    "a": [('ada_b', lambda t: t[1:]), ('pre_g', lambda t: t[1:]), ('post_g', lambda t: t[1:]),
          ('sgu_norm_g', lambda t: t[1:]), ('sgu_w_s', lambda t: t[1:]), ('sgu_b_s', lambda t: t[1:]),
          ('mla_q_norm_g', lambda t: t), ('mla_kv_norm_g', lambda t: t)],
    "b": [('ada_b', lambda t: t[0:1, 2 * t.shape[1] // 3:]), ('post_g', lambda t: t[0:1]),
          ('sgu_norm_g', lambda t: t[0:1]), ('sgu_w_s', lambda t: t[0:1]), ('sgu_b_s', lambda t: t[0:1])],
    "c": [('ada_b', lambda t: t[0:1, :2 * t.shape[1] // 3]), ('pre_g', lambda t: t[0:1])],
}


WEIGHTS = ['ada_w', 'ada_b', 'pre_g', 'post_g', 'sgu_w_in', 'sgu_norm_g', 'sgu_w_s', 'sgu_b_s', 'sgu_w_out',
           'mla_w_in', 'mla_q_norm_g', 'mla_kv_norm_g', 'mla_w_uq', 'mla_w_ukv', 'mla_w_out']
INPUTS = ['x', 'c'] + WEIGHTS + ['loss_target'] + ['m_' + n for n in WEIGHTS] + ['v_' + n for n in WEIGHTS]


def kernel(x, c, ada_w, ada_b, pre_g, post_g, sgu_w_in, sgu_norm_g, sgu_w_s, sgu_b_s, sgu_w_out, mla_w_in, mla_q_norm_g, mla_kv_norm_g, mla_w_uq, mla_w_ukv, mla_w_out, loss_target, m_ada_w, m_ada_b, m_pre_g, m_post_g, m_sgu_w_in, m_sgu_norm_g, m_sgu_w_s, m_sgu_b_s, m_sgu_w_out, m_mla_w_in, m_mla_q_norm_g, m_mla_kv_norm_g, m_mla_w_uq, m_mla_w_ukv, m_mla_w_out, v_ada_w, v_ada_b, v_pre_g, v_post_g, v_sgu_w_in, v_sgu_norm_g, v_sgu_w_s, v_sgu_b_s, v_sgu_w_out, v_mla_w_in, v_mla_q_norm_g, v_mla_kv_norm_g, v_mla_w_uq, v_mla_w_ukv, v_mla_w_out):
    given = locals()
    A = {name: given[name] for name in INPUTS}
    x0 = A['x'][0]
    tgt = A['loss_target'][0]
    s, d = x0.shape
    e = 2 * d
    ncol = 3 * d // N_DEV
    _, _, _, me = _my_place()
    ktabs = _rope_tables(s)

    gains = jnp.zeros((SUBLANE, LANE), F32)
    gains = gains.at[0:2, :Q_RANK // N_DEV].set(A['mla_q_norm_g'])
    gains = gains.at[2:4, :KV_RANK // N_DEV].set(A['mla_kv_norm_g'])
    c8 = jnp.broadcast_to(A['c'], (SUBLANE, d))
    cg, gg = _exchange([c8, gains], ["row", "row"], True, "ag_cond")
    cond_raw = cg.reshape(N_DEV, SUBLANE, d)[:, 0, :]
    gg = gg.reshape(N_DEV, SUBLANE, LANE)
    gq_full = jnp.transpose(gg[:, 0:2, :Q_RANK // N_DEV], (1, 0, 2)).reshape(N_MIX, Q_RANK)
    gkv_full = jnp.transpose(gg[:, 2:4, :KV_RANK // N_DEV], (1, 0, 2)).reshape(N_MIX, KV_RANK)
    gq_pad = jnp.pad(gq_full, ((0, 0), (0, Q_RANK_PAD - Q_RANK)))

    bias_my = lax.dynamic_slice_in_dim(A['ada_b'], me * ncol, ncol, axis=1)
    mod_part = _ada_mod(cond_raw, A['ada_w'], bias_my, "ada_mod")
    send = jnp.pad(jnp.transpose(mod_part, (1, 0, 2)), ((0, 0), (0, SUBLANE - DEPTH), (0, 0)))
    (rb,) = _exchange([send.reshape(N_DEV * SUBLANE, ncol)], ["row"], False, "a2a_mod")

    token = rb
    gathers = {}
    for i in range(DEPTH):
        j = i // N_MIX
        if i % N_MIX == 0:
            parts = [("in", [(A['sgu_w_in'], j)], ["col"]), ("out", [(A['sgu_w_out'], j)], ["row"])]
        else:
            parts = [("all", [(A['mla_w_in'], j), (A['mla_w_uq'], j), (A['mla_w_ukv'], j), (A['mla_w_out'], j)],
                      ["col", "col", "col", "row"])]
        for part, shards, kinds in parts:
            gathers[(i, part)], token = _gather2_start(shards, kinds, f"ag_start_{i}_{part}", token)

    def forward_gathers(i, carried):
        for key in [k for k in gathers if k[0] == i]:
            gathers[key], _, (carried,) = _gather2_forward(gathers[key], f"ag_forward_{key[0]}_{key[1]}", token,
                                                           carry=[carried])
        return carried

    mod = jnp.transpose(rb.reshape(N_DEV, SUBLANE, ncol)[:, :DEPTH, :], (1, 0, 2)).reshape(DEPTH, 3 * d) + token[0, 0]
    shift = [mod[i:i + 1, :d] for i in range(DEPTH)]
    scale = [mod[i:i + 1, d:2 * d] for i in range(DEPTH)]
    gate = [mod[i:i + 1, 2 * d:] for i in range(DEPTH)]

    saved = []
    x = x0
    h = _pre_fwd(x, A['pre_g'][0:1], scale[0], shift[0], "pre_fwd")
    for i in range(DEPTH):
        j = i // N_MIX
        if i == 0:
            gathers[(0, "in")], _, (h,) = _gather2_forward(gathers[(0, "in")], "ag_forward_0_in", token, carry=[h])
        if i % N_MIX == 0:
            (w_in,) = _gather2_wait(gathers[(i, "in")], f"ag_wait_{i}_in", h)
            uvz = _mm(h, w_in, "nn", F32, "sgu_in")
            if i == 0:
                gathers[(0, "out")], _, (uvz,) = _gather2_forward(gathers[(0, "out")], "ag_forward_0_out", token,
                                                                  carry=[uvz])
            bias_full = jnp.repeat(A['sgu_b_s'][j].T, e // SGU_GROUPS, axis=1)
            ng = A['sgu_norm_g'][j:j + 1]
            y = _sgu_mid_fwd(uvz, ng, A['sgu_w_s'][j], bias_full, "sgu_mid_fwd")
            y = forward_gathers(i + 1, y)
            (w_out,) = _gather2_wait(gathers[(i, "out")], f"ag_wait_{i}_out", y)
            out = _mm(y, w_out, "nn", F32, "sgu_out")
            saved.append(dict(x=x, h=h, uvz=uvz, y=y, out=out, w_in=w_in, w_out=w_out, bias=bias_full, ng=ng))
        else:
            w_in, w_uq, w_ukv, w_out = _gather2_wait(gathers[(i, "all")], f"ag_wait_{i}_all", h)
            w_in = _pad_mla_w_in(w_in)
            w_uq = _pad_w_uq(w_uq)
            gq, gkv = gq_pad[j:j + 1], gkv_full[j:j + 1]
            proj = _mm(h, w_in, "nn", F32, "mla_in")
            qn, kvn, kr = _mla_norm_fwd(proj, gq, gkv, ktabs, "mla_norm_fwd")
            q = _mm(qn, w_uq, "nn", F32, "mla_uq")
            kv = _mm(kvn, w_ukv, "nn", BF16, "mla_ukv")
            o, y, lse, q_cat = _attn_fwd(q, kv, kr, proj, ktabs, "attn_fwd")
            y = forward_gathers(i + 1, y)
            out = _mm(y, w_out, "nn", F32, "mla_out")
            saved.append(dict(x=x, h=h, proj=proj, qn=qn, kvn=kvn, kr=kr, q_cat=q_cat, kv=kv, o=o, y=y, lse=lse,
                              out=out, w_in=w_in, w_uq=w_uq, w_ukv=w_ukv, w_out=w_out, gq=gq, gkv=gkv))
        if i + 1 < DEPTH:
            x, h = _post_pre_fwd(x, out, gate[i], A['post_g'][i:i + 1], A['pre_g'][i + 1:i + 2], scale[i + 1],
                                 shift[i + 1], "post_pre_fwd")

    last = DEPTH - 1
    d_shift, d_scale, d_gate = [None] * DEPTH, [None] * DEPTH, [None] * DEPTH
    d_pre, d_post = [None] * DEPTH, [None] * DEPTH
    dx, dout, d_gate[last], d_post[last], loss_row = _last_post_loss_bwd(
        x, saved[last]['out'], gate[last], A['post_g'][last:], tgt, "last_post_loss_bwd")
    loss = lax.psum(loss_row[0, 0], ("x", "y", "c"))

    scatters, scatters_out = [None] * DEPTH, [None] * DEPTH
    small, small_grads, small_handles = {}, {}, {}
    for i in reversed(range(DEPTH)):
        j = i // N_MIX
        sv = saved[i]
        if i % N_MIX == 0:
            dy = _mm(dout, sv['w_out'], "nt", F32, "sgu_out_dx")
            g_w_out = _mm(sv['y'], dout, "tn", BF16, "sgu_out_dw")
            scatters_out[i], token, (dy,) = _exchange_start([g_w_out], ["row"], False, f"rs_out_start_{i}", token,
                                                            carry=[dy])
            duvz, dws, dbs, dng = _sgu_mid_bwd(sv['uvz'], dy, sv['ng'], A['sgu_w_s'][j], sv['bias'], "sgu_mid_bwd")
            small[('sgu', j)] = (dws, dbs.reshape(SGU_GROUPS, SGU_BLOCK), dng)
            g_w_in = _mm(sv['h'], duvz, "tn", BF16, "sgu_in_dw")
            if i == 0:
                scatters[i], token, (duvz,) = _scatter2_start(g_w_in, "rs2_start_0", token, carry=[duvz])
                small_grads["b"] = [d_gate[0], d_post[0], dng, dws[None], small[('sgu', 0)][1][None]]
                small_handles["b"], token, (duvz,) = _exchange_start(
                    [_pack(small_grads["b"])], ["row"], True, "ag_small_b_start", token, carry=[duvz])
            else:
                scatters[i], token, (duvz,) = _exchange_start([g_w_in], ["col"], False, f"rs_start_{i}", token,
                                                              carry=[duvz])
            dh = _mm(duvz, sv['w_in'], "nt", F32, "sgu_in_dx")
        else:
            dy = _mm(dout, sv['w_out'], "nt", F32, "mla_out_dx")
            g_w_out = _mm(sv['y'], dout, "tn", BF16, "mla_out_dw")
            scatters_out[i], token, (dy,) = _exchange_start([g_w_out], ["row"], False, f"rs_out_start_{i}", token,
                                                            carry=[dy])
            do, dproj = _mla_gate_bwd(dy, sv['o'], sv['proj'], "mla_gate_bwd")
            dq_b, dkv, dkr_heads = _attn_bwd(sv['q_cat'], sv['kv'], sv['kr'], do, sv['o'], sv['lse'], ktabs,
                                             "attn_bwd")
            dqn = _mm(dq_b, sv['w_uq'], "nt", F32, "mla_uq_dx")
            g_w_uq = _unpad_w_uq(_mm(sv['qn'], dq_b, "tn", BF16, "mla_uq_dw"))
            dkvn = _mm(dkv, sv['w_ukv'], "nt", F32, "mla_ukv_dx")
            g_w_ukv = _mm(sv['kvn'], dkv, "tn", BF16, "mla_ukv_dw")
            dproj, dgq, dgkv = _mla_norm_bwd(dqn, dkvn, dkr_heads, sv['proj'], sv['gq'], sv['gkv'], ktabs, dproj,
                                             "mla_norm_bwd")
            g_w_in = _unpad_mla_w_in(_mm(sv['h'], dproj, "tn", BF16, "mla_in_dw"))
            scatters[i], token, (dproj,) = _exchange_start(
                [g_w_in, g_w_uq, g_w_ukv], ["col", "col", "col"], False, f"rs_start_{i}", token, carry=[dproj])
            dh = _mm(dproj, sv['w_in'], "nt", F32, "mla_in_dx")
            small[('mla', j)] = (dgq[:, :Q_RANK], dgkv)
        if i > 0:
            (dx, d_shift[i], d_scale[i], d_pre[i], dout, d_gate[i - 1], d_post[i - 1]) = _pre_post_bwd(
                dh, sv['x'], dx, A['pre_g'][i:i + 1], scale[i], saved[i - 1]['out'], gate[i - 1],
                A['post_g'][i - 1:i], "pre_post_bwd")
        else:
            dx, d_shift[i], d_scale[i], d_pre[i] = _pre_bwd(dh, sv['x'], dx, A['pre_g'][i:i + 1], scale[i], "pre_bwd")
        if i == 1:
            small_grads["a"] = [
                jnp.concatenate([jnp.concatenate([d_shift[l], d_scale[l], d_gate[l]], axis=1)
                                 for l in range(1, DEPTH)], axis=0),
                jnp.concatenate(d_pre[1:], axis=0), jnp.concatenate(d_post[1:], axis=0),
                small[('sgu', 1)][2], small[('sgu', 1)][0][None], small[('sgu', 1)][1][None],
                jnp.concatenate([small[('mla', jj)][0] for jj in range(N_MIX)], axis=0),
                jnp.concatenate([small[('mla', jj)][1] for jj in range(N_MIX)], axis=0)]
            small_handles["a"], token, (dx,) = _exchange_start(
                [_pack(small_grads["a"])], ["row"], True, "ag_small_a_start", token, carry=[dx])

    res = {}

    def big(name, recv0, recv1):
        res[name] = _adam_reduce(recv0, recv1, A[name], A['m_' + name], A['v_' + name], "adam_" + name)

    def small_full(name, prefix):
        t = A[prefix + name]
        if name in ('mla_q_norm_g', 'mla_kv_norm_g'):
            t = lax.dynamic_update_slice_in_dim(jnp.zeros((t.shape[0], t.shape[1] * N_DEV), F32), t,
                                                me * t.shape[1], axis=1)
        return t

    def finish(i, after):
        first = ([_scatter2_wait(scatters[i], "rs2_wait_0", after)] if i == 0
                 else _exchange_wait(scatters[i], f"rs_wait_{i}", after))
        return first + _exchange_wait(scatters_out[i], f"rs_out_wait_{i}", after)

    recv_mla = {j: finish(N_MIX * j + 1, dx) for j in reversed(range(N_MIX))}
    for idx_w, name in enumerate(['mla_w_in', 'mla_w_uq', 'mla_w_ukv', 'mla_w_out']):
        big(name, recv_mla[0][idx_w], recv_mla[1][idx_w])

    recv_sgu = {j: finish(N_MIX * j, res['mla_w_out'][0]) for j in reversed(range(N_MIX))}
    for idx_w, name in enumerate(['sgu_w_in', 'sgu_w_out']):
        big(name, recv_sgu[0][idx_w], recv_sgu[1][idx_w])

    small_grads["c"] = [jnp.concatenate([d_shift[0], d_scale[0]], axis=1), d_pre[0]]
    gathered = {tag: _exchange_wait(small_handles[tag], f"ag_small_{tag}_wait", res['sgu_w_in'][0])[0]
                for tag in ("a", "b")}
    (gathered["c"],) = _exchange([_pack(small_grads["c"])], ["row"], True, "ag_small_c", after=res['sgu_w_in'][0])

    upd, parts = {}, {}
    for tag, group in SMALL_GROUPS.items():
        shapes = [g.shape for g in small_grads[tag]]
        packs = [_pack([pick(small_full(name, prefix)) for name, pick in group]) for prefix in ('', 'm_', 'v_')]
        outs4 = _adam_small(gathered[tag].reshape(N_DEV, -1, LANE), *packs, "adam_small_" + tag)
        upd[tag] = [_unpack(t, shapes) for t in outs4]
        parts[tag] = _unpack(gathered[tag].reshape(N_DEV, -1), shapes, lead=(N_DEV,))
    for k_out in range(4):
        ab_rest, pre_rest, post_rest, ng_1, ws_1, bs_1, gq_all, gkv_all = upd["a"][k_out]
        ab_gate0, post_0, ng_0, ws_0, bs_0 = upd["b"][k_out]
        ab_0, pre_0 = upd["c"][k_out]
        for name, val in (
                ('ada_b', jnp.concatenate([jnp.concatenate([ab_0, ab_gate0], axis=1), ab_rest], axis=0)),
                ('pre_g', jnp.concatenate([pre_0, pre_rest], axis=0)),
                ('post_g', jnp.concatenate([post_0, post_rest], axis=0)),
                ('sgu_norm_g', jnp.concatenate([ng_0, ng_1], axis=0)),
                ('sgu_w_s', jnp.concatenate([ws_0, ws_1], axis=0)),
                ('sgu_b_s', jnp.concatenate([bs_0, bs_1], axis=0)),
                ('mla_q_norm_g', gq_all), ('mla_kv_norm_g', gkv_all)):
            if name in ('mla_q_norm_g', 'mla_kv_norm_g'):
                wdt = A[name].shape[1]
                val = lax.dynamic_slice_in_dim(val, me * wdt, wdt, axis=1)
            res.setdefault(name, [None] * 4)[k_out] = val

    dmod_all = jnp.concatenate([jnp.concatenate([parts["c"][0], parts["b"][0]], axis=2), parts["a"][0]], axis=1)
    dmod_cols = jnp.transpose(lax.dynamic_slice_in_dim(dmod_all, me * ncol, ncol, axis=2), (1, 0, 2))
    res['ada_w'] = _ada_bwd_adam(jnp.transpose(cond_raw), dmod_cols, A['ada_w'], A['m_ada_w'], A['v_ada_w'], "ada_bwd")

    outs = [loss, dx[None]]
    for k_out in range(4):
        outs += [res[n][k_out] for n in WEIGHTS]
    return tuple(outs)
```
